```python
import jax
import jax.numpy as jnp
from jax import lax
import numpy as np

D_MODEL = 1024
BATCH = 16
SEQ = 4096
DEPTH = 2

CTX_LEN = 256
GRID_W = 64
NORM_EPS = 1e-6
ROPE_BASE = 10000.0
NEG_INF = -1e30
F32 = jnp.float32

M_HEADS = 4
M_HEAD_DIM = 128
M_WIDTH = M_HEADS * M_HEAD_DIM
M_CONV = 3
M_CHUNK = 64
M_FORGET_BIAS_LO = 3.0
M_FORGET_BIAS_HI = 6.0

A_HEADS = 8
A_KV_HEADS = 2
A_GROUP = A_HEADS // A_KV_HEADS
A_HEAD_DIM = 64
A_WIDTH = A_HEADS * A_HEAD_DIM
A_KV_WIDTH = A_KV_HEADS * A_HEAD_DIM
A_WINDOW = 128
A_BLOCK = 128

R_HEADS = 4
R_QK_DIM = 128
R_V_DIM = 256
R_QK_WIDTH = R_HEADS * R_QK_DIM
R_V_WIDTH = R_HEADS * R_V_DIM
R_CHUNK = 128

FFN_DIM = 2816
FFN_CONV = 3

IN_SPLITS = (M_WIDTH, M_WIDTH, M_WIDTH, M_WIDTH, 4 * M_HEADS,
             A_WIDTH, A_KV_WIDTH, A_KV_WIDTH,
             R_QK_WIDTH, R_QK_WIDTH, R_V_WIDTH, R_V_WIDTH,
             D_MODEL, D_MODEL, D_MODEL)
IN_COLS = sum(IN_SPLITS)

kernel_name = 'hybrid_mlstm_swa_retention_dit_block'


def _split_cols(w):
    offs = []
    acc = 0
    for s in IN_SPLITS[:-1]:
        acc += s
        offs.append(acc)
    return jnp.split(w, offs, axis=-1)


def _rmsnorm(x, w):
    xf = x.astype(F32)
    y = xf * lax.rsqrt(jnp.mean(xf * xf, axis=-1, keepdims=True) + NORM_EPS)
    return (y * w.astype(F32)).astype(x.dtype)


def _head_norm(y, w, n_heads, dt):
    shp = y.shape
    yf = y.astype(F32).reshape(shp[:-1] + (n_heads, shp[-1] // n_heads))
    mu = jnp.mean(yf, axis=-1, keepdims=True)
    var = jnp.mean(jnp.square(yf - mu), axis=-1, keepdims=True)
    yf = ((yf - mu) * lax.rsqrt(var + NORM_EPS)).reshape(shp)
    return (yf * w.astype(F32)).astype(dt)


def _modulation(cvec, w, b, n):
    return jnp.split(jax.nn.silu(cvec) @ w + b, n, axis=-1)


def _dwconv(x, w):
    K = w.shape[0]
    T = x.shape[1]
    r = K // 2
    xp = jnp.pad(x, ((0, 0), (r, r), (0, 0)))
    out = xp[:, 0:T] * w[0]
    for j in range(1, K):
        out = out + xp[:, j:j + T] * w[j]
    return out


def _rope_rotate(x, pos):
    half = x.shape[-1] // 2
    inv = ROPE_BASE ** (-jnp.arange(half, dtype=F32) / half)
    ang = pos.astype(F32)[:, None] * inv[None, :]
    cos = jnp.cos(ang)[None, :, None, :]
    sin = jnp.sin(ang)[None, :, None, :]
    xf = x.astype(F32)
    x1, x2 = xf[..., :half], xf[..., half:]
    return jnp.concatenate([x1 * cos - x2 * sin, x1 * sin + x2 * cos], axis=-1).astype(x.dtype)


def _axial_rope(x, row, col):
    half = x.shape[-1] // 2
    return jnp.concatenate([_rope_rotate(x[..., :half], row), _rope_rotate(x[..., half:], col)], axis=-1)


def _to_heads(t, n_heads):
    B, T, W = t.shape
    return t.reshape(B, T, n_heads, W // n_heads).transpose(0, 2, 1, 3)


def _from_heads(t):
    B, H, T, d = t.shape
    return t.transpose(0, 2, 1, 3).reshape(B, T, H * d)


def _flip(t):
    return jnp.flip(t, axis=2)


def _flip_opt(t):
    return None if t is None else _flip(t)


def _window_attention(q, k, v, kc, vc, sink):
    B, S, Hkv, G, hd = q.shape
    L = kc.shape[1]
    nb = S // A_BLOCK
    span = 3 * A_BLOCK
    scale = hd ** -0.5
    pad = ((0, 0), (A_BLOCK, A_BLOCK), (0, 0), (0, 0))
    kp = jnp.pad(k, pad).reshape(B, nb + 2, A_BLOCK, Hkv, hd)
    vp = jnp.pad(v, pad).reshape(B, nb + 2, A_BLOCK, Hkv, hd)
    kw = jnp.concatenate([kp[:, :nb], kp[:, 1:nb + 1], kp[:, 2:]], axis=2)
    vw = jnp.concatenate([vp[:, :nb], vp[:, 1:nb + 1], vp[:, 2:]], axis=2)
    qb = q.reshape(B, nb, A_BLOCK, Hkv, G, hd)
    blk = jnp.arange(nb)[:, None, None]
    qi = blk * A_BLOCK + jnp.arange(A_BLOCK)[None, :, None]
    kj = (blk - 1) * A_BLOCK + jnp.arange(span)[None, None, :]
    mask = (jnp.abs(qi - kj) <= A_WINDOW) & (kj >= 0) & (kj < S)
    sink_col = jnp.broadcast_to(sink[None, :, :, None, None], (B, Hkv, G, A_BLOCK, 1))

    def block(args):
        qx, kx, vx, mx = args
        s_loc = jnp.einsum('bqhgd,bkhd->bhgqk', qx, kx).astype(F32) * scale
        s_loc = jnp.where(mx[None, None, None], s_loc, NEG_INF)
        s_ctx = jnp.einsum('bqhgd,bchd->bhgqc', qx, kc).astype(F32) * scale
        p = jax.nn.softmax(jnp.concatenate([s_loc, s_ctx, sink_col], axis=-1), axis=-1).astype(vx.dtype)
        return (jnp.einsum('bhgqk,bkhd->bqhgd', p[..., :span], vx)
                + jnp.einsum('bhgqc,bchd->bqhgd', p[..., span:span + L], vc))

    out = lax.map(block, (jnp.moveaxis(qb, 1, 0), jnp.moveaxis(kw, 1, 0), jnp.moveaxis(vw, 1, 0), mask))
    return jnp.moveaxis(out, 0, 1).reshape(B, S, Hkv * G * hd)


def _context_attention(qc, kc, vc, sink):
    B, L, Hkv, G, hd = qc.shape
    s = jnp.einsum('bqhgd,bkhd->bhgqk', qc, kc).astype(F32) * (hd ** -0.5)
    sink_col = jnp.broadcast_to(sink[None, :, :, None, None], (B, Hkv, G, L, 1))
    p = jax.nn.softmax(jnp.concatenate([s, sink_col], axis=-1), axis=-1)[..., :L].astype(vc.dtype)
    return jnp.einsum('bhgqk,bkhd->bqhgd', p, vc).reshape(B, L, Hkv * G * hd)


def _mlstm_states(k, v, log_i, log_f, state0):
    B, H, T, dk = k.shape
    dv = v.shape[-1]
    N = T // M_CHUNK
    kc = k.reshape(B, H, N, M_CHUNK, dk)
    vc = v.reshape(B, H, N, M_CHUNK, dv)
    b = jnp.cumsum(log_f.reshape(B, H, N, M_CHUNK), axis=-1)
    g = b[..., -1]
    a = g[..., None] - b + log_i.reshape(B, H, N, M_CHUNK)

    def step(carry, xs):
        C, n, m = carry
        kx, vx, ax, gx = xs
        m_new = jnp.maximum(gx + m, jnp.max(ax, axis=-1))
        w = jnp.exp(ax - m_new[..., None])
        decay = jnp.exp(gx + m - m_new)
        C_new = decay[..., None, None] * C + jnp.einsum('bhlv,bhlk->bhvk', vx * w[..., None], kx)
        n_new = decay[..., None] * n + jnp.einsum('bhl,bhlk->bhk', w, kx)
        return (C_new, n_new, m_new), (C, n, m)

    xs = (jnp.moveaxis(kc, 2, 0), jnp.moveaxis(vc, 2, 0), jnp.moveaxis(a, 2, 0), jnp.moveaxis(g, 2, 0))
    final, starts = lax.scan(step, state0, xs)
    return starts, final, b


def _mlstm_outputs(q, k, v, log_i, b, starts):
    B, H, T, dk = q.shape
    dv = v.shape[-1]
    N = T // M_CHUNK
    C, n, m = [jnp.moveaxis(s_, 0, 2) for s_ in starts]
    qc = q.reshape(B, H, N, M_CHUNK, dk)
    kc = k.reshape(B, H, N, M_CHUNK, dk)
    vc = v.reshape(B, H, N, M_CHUNK, dv)
    ic = log_i.reshape(B, H, N, M_CHUNK)
    lower = jnp.tril(jnp.ones((M_CHUNK, M_CHUNK), dtype=bool))
    d_log = jnp.where(lower, b[..., :, None] - b[..., None, :] + ic[..., None, :], -jnp.inf)
    inter_log = b + m[..., None]
    m_j = jnp.maximum(inter_log, jnp.max(d_log, axis=-1))
    inter_w = jnp.exp(inter_log - m_j)
    s = jnp.einsum('bhnjd,bhnrd->bhnjr', qc, kc) * jnp.exp(d_log - m_j[..., None])
    num = (inter_w[..., None] * jnp.einsum('bhnvk,bhnjk->bhnjv', C, qc)
           + jnp.einsum('bhnjr,bhnrv->bhnjv', s, vc))
    den = inter_w * jnp.einsum('bhnk,bhnjk->bhnj', n, qc) + jnp.sum(s, axis=-1)
    h = num / jnp.maximum(jnp.abs(den), jnp.exp(-m_j))[..., None]
    return h.reshape(B, H, T, dv)


def _mlstm_dir(qc, kc, vc, ic, fc, qx, kx, vx, ix, fx):
    B, H, _, dk = kx.shape
    dv = vx.shape[-1]
    zero = (jnp.zeros((B, H, dv, dk), F32), jnp.zeros((B, H, dk), F32), jnp.zeros((B, H), F32))
    st_c, fin_c, b_c = _mlstm_states(kc, vc, ic, fc, zero)
    st_x, _, b_x = _mlstm_states(kx, vx, ix, fx, fin_c)
    hx = _mlstm_outputs(qx, kx, vx, ix, b_x, st_x)
    hc = None if qc is None else _mlstm_outputs(qc, kc, vc, ic, b_c, st_c)
    return hx, hc


def _mlstm_bidir(qc, kc, vc, gc, qx, kx, vx, gx):
    icf, fcf, icb, fcb = gc
    ixf, fxf, ixb, fxb = gx
    hx_f, hc_f = _mlstm_dir(qc, kc, vc, icf, fcf, qx, kx, vx, ixf, fxf)
    hx_b, hc_b = _mlstm_dir(_flip_opt(qc), _flip(kc), _flip(vc), _flip(icb), _flip(fcb),
                            _flip(qx), _flip(kx), _flip(vx), _flip(ixb), _flip(fxb))
    hx = _from_heads(hx_f + _flip(hx_b))
    hc = None if qc is None else _from_heads(hc_f + _flip(hc_b))
    return hx, hc


def _ret_states(k, v, log_g, state0):
    B, H, T, dk = k.shape
    dv = v.shape[-1]
    N = T // R_CHUNK
    kc = k.reshape(B, H, N, R_CHUNK, dk)
    vc = v.reshape(B, H, N, R_CHUNK, dv)
    pos = jnp.arange(R_CHUNK, dtype=F32)
    w = jnp.exp(log_g[:, None] * (R_CHUNK - 1.0 - pos)[None, :])
    chunk_decay = jnp.exp(log_g * R_CHUNK)

    def step(R, xs):
        kx, vx = xs
        R_new = (chunk_decay[None, :, None, None] * R
                 + jnp.einsum('bhlk,bhlv->bhkv', kx * w[None, :, :, None], vx))
        return R_new, R

    final, starts = lax.scan(step, state0, (jnp.moveaxis(kc, 2, 0), jnp.moveaxis(vc, 2, 0)))
    return starts, final


def _ret_outputs(q, k, v, log_g, starts):
    B, H, T, dk = q.shape
    dv = v.shape[-1]
    N = T // R_CHUNK
    qc = q.reshape(B, H, N, R_CHUNK, dk)
    kc = k.reshape(B, H, N, R_CHUNK, dk)
    vc = v.reshape(B, H, N, R_CHUNK, dv)
    R = jnp.moveaxis(starts, 0, 2)
    pos = jnp.arange(R_CHUNK, dtype=F32)
    diff = pos[:, None] - pos[None, :]
    decay = jnp.where(diff >= 0, jnp.exp(log_g[:, None, None] * jnp.maximum(diff, 0.0)), 0.0)
    s = jnp.einsum('bhnjd,bhnrd->bhnjr', qc, kc) * decay[None, :, None]
    inter = jnp.exp(log_g[:, None] * (pos + 1.0)[None, :])
    out = (jnp.einsum('bhnjr,bhnrv->bhnjv', s, vc)
           + inter[None, :, None, :, None] * jnp.einsum('bhnjk,bhnkv->bhnjv', qc, R))
    return out.reshape(B, H, T, dv)


def _ret_dir(qc, kc, vc, qx, kx, vx, log_g):
    B, H, _, dk = kx.shape
    dv = vx.shape[-1]
    st_c, fin_c = _ret_states(kc, vc, log_g, jnp.zeros((B, H, dk, dv), F32))
    st_x, _ = _ret_states(kx, vx, log_g, fin_c)
    hx = _ret_outputs(qx, kx, vx, log_g, st_x)
    hc = None if qc is None else _ret_outputs(qc, kc, vc, log_g, st_c)
    return hx, hc


def _ret_bidir(qc, kc, vc, qx, kx, vx, log_g):
    hx_f, hc_f = _ret_dir(qc, kc, vc, qx, kx, vx, log_g[0])
    hx_b, hc_b = _ret_dir(_flip_opt(qc), _flip(kc), _flip(vc), _flip(qx), _flip(kx), _flip(vx), log_g[1])
    hx = _from_heads(hx_f + _flip(hx_b))
    hc = None if qc is None else _from_heads(hc_f + _flip(hc_b))
    return hx, hc


def _token_mixer(h, hc, row, col, w_in, m_gate_bias, m_conv_w, m_norm_w, a_sink, ret_logit, ret_norm_w,
                 w_br_m, w_br_a, w_br_r, w_out, need_ctx):
    dt = h.dtype
    B, S, _ = h.shape
    L = hc.shape[1]
    (w_mq, w_mk, w_mv, w_mo, w_mg, w_aq, w_ak, w_av,
     w_rq, w_rk, w_rv, w_rg, w_gm, w_ga, w_gr) = _split_cols(w_in)
    conv_q, conv_k = m_conv_w[:, :M_WIDTH], m_conv_w[:, M_WIDTH:]

    def m_q(t):
        return _to_heads(jax.nn.silu(_dwconv(t @ w_mq, conv_q)), M_HEADS).astype(F32)

    def m_k(t):
        return _to_heads(jax.nn.silu(_dwconv(t @ w_mk, conv_k)), M_HEADS).astype(F32) * (M_HEAD_DIM ** -0.5)

    def m_v(t):
        return _to_heads(t @ w_mv, M_HEADS).astype(F32)

    def m_gates(t):
        Bt, Tt, _ = t.shape
        g = (t @ w_mg + m_gate_bias).astype(F32).reshape(Bt, Tt, 4, M_HEADS).transpose(2, 0, 3, 1)
        return (g[0], jax.nn.log_sigmoid(g[1]), g[2], jax.nn.log_sigmoid(g[3]))

    hm_x, hm_c = _mlstm_bidir(m_q(hc) if need_ctx else None, m_k(hc), m_v(hc), m_gates(hc),
                              m_q(h), m_k(h), m_v(h), m_gates(h))

    sink = a_sink.astype(F32).reshape(A_KV_HEADS, A_GROUP)
    q = _axial_rope((h @ w_aq).reshape(B, S, A_HEADS, A_HEAD_DIM), row, col)
    q = q.reshape(B, S, A_KV_HEADS, A_GROUP, A_HEAD_DIM)
    k = _axial_rope((h @ w_ak).reshape(B, S, A_KV_HEADS, A_HEAD_DIM), row, col)
    v = (h @ w_av).reshape(B, S, A_KV_HEADS, A_HEAD_DIM)
    kc = (hc @ w_ak).reshape(B, L, A_KV_HEADS, A_HEAD_DIM)
    vc = (hc @ w_av).reshape(B, L, A_KV_HEADS, A_HEAD_DIM)
    ya = _window_attention(q, k, v, kc, vc, sink)

    log_g = jax.nn.log_sigmoid(ret_logit.astype(F32))
    pos_c = jnp.arange(L)
    pos_x = L + jnp.arange(S)

    def r_qk(t, w, pos, scale):
        Bt, Tt, _ = t.shape
        r = _rope_rotate((t @ w).reshape(Bt, Tt, R_HEADS, R_QK_DIM), pos)
        return r.transpose(0, 2, 1, 3).astype(F32) * scale

    def r_v(t):
        return _to_heads(t @ w_rv, R_HEADS).astype(F32)

    k_scale = R_QK_DIM ** -0.5
    hr_x, hr_c = _ret_bidir(r_qk(hc, w_rq, pos_c, 1.0) if need_ctx else None, r_qk(hc, w_rk, pos_c, k_scale),
                            r_v(hc), r_qk(h, w_rq, pos_x, 1.0), r_qk(h, w_rk, pos_x, k_scale), r_v(h), log_g)

    def mlstm_out(t, hm):
        return jax.nn.sigmoid(t @ w_mo) * _head_norm(hm, m_norm_w, M_HEADS, dt)

    def ret_out(t, hr):
        return jax.nn.silu(t @ w_rg) * _head_norm(hr, ret_norm_w, R_HEADS, dt)

    def merge(t, ym_, ya_, yr_):
        return (jax.nn.sigmoid(t @ w_gm) * (ym_ @ w_br_m)
                + jax.nn.sigmoid(t @ w_ga) * (ya_ @ w_br_a)
                + jax.nn.sigmoid(t @ w_gr) * (yr_ @ w_br_r)) @ w_out

    y = merge(h, mlstm_out(h, hm_x), ya, ret_out(h, hr_x))
    if need_ctx:
        qcx = (hc @ w_aq).reshape(B, L, A_KV_HEADS, A_GROUP, A_HEAD_DIM)
        ya_c = _context_attention(qcx, kc, vc, sink)
        yc = merge(hc, mlstm_out(hc, hm_c), ya_c, ret_out(hc, hr_c))
    else:
        yc = None
    return y, yc


def _conv_ffn(h, w_up, conv_w, conv_b, w_down):
    a, b = jnp.split(h @ w_up, 2, axis=-1)
    return (jax.nn.silu(_dwconv(a, conv_w) + conv_b) * b) @ w_down


def setup_inputs(seed: int = 0) -> dict:
    key = jax.random.key(seed)
    ks = jax.random.split(key, 24)
    D = D_MODEL

    def nrm(k, shape, scale):
        return jax.random.normal(k, shape, F32) * scale

    lin = jnp.linspace(M_FORGET_BIAS_LO, M_FORGET_BIAS_HI, M_HEADS, dtype=F32)
    zer = jnp.zeros((M_HEADS,), F32)
    gate_base = jnp.concatenate([zer, lin, zer, lin])
    ret_base = jnp.log(2.0 ** (5.0 + jnp.arange(R_HEADS, dtype=F32)) - 1.0)
    return {
        'x': nrm(ks[0], (BATCH, SEQ, D), 1.0),
        'c': nrm(ks[1], (BATCH, D), 1.0),
        'ctx': nrm(ks[2], (BATCH, CTX_LEN, D), 1.0),
        'c_ctx': nrm(ks[3], (D,), 1.0),
        'mod_w': nrm(ks[4], (DEPTH, D, 6 * D), 0.5 * D ** -0.5),
        'mod_b': nrm(ks[5], (DEPTH, 6 * D), 0.02),
        'norm1_w': 1.0 + nrm(ks[6], (DEPTH, D), 0.02),
        'norm2_w': 1.0 + nrm(ks[7], (DEPTH, D), 0.02),
        'w_in': nrm(ks[8], (DEPTH, D, IN_COLS), D ** -0.5),
        'm_gate_bias': gate_base[None, :] + nrm(ks[9], (DEPTH, 4 * M_HEADS), 0.1),
        'm_conv_w': nrm(ks[10], (DEPTH, M_CONV, 2 * M_WIDTH), M_CONV ** -0.5),
        'm_norm_w': 1.0 + nrm(ks[11], (DEPTH, M_WIDTH), 0.02),
        'a_sink': nrm(ks[12], (DEPTH, A_HEADS), 0.5),
        'ret_logit': ret_base[None, None, :] + nrm(ks[13], (DEPTH, 2, R_HEADS), 0.1),
        'ret_norm_w': 1.0 + nrm(ks[14], (DEPTH, R_V_WIDTH), 0.02),
        'w_br_m': nrm(ks[15], (DEPTH, M_WIDTH, D), M_WIDTH ** -0.5),
        'w_br_a': nrm(ks[16], (DEPTH, A_WIDTH, D), A_WIDTH ** -0.5),
        'w_br_r': nrm(ks[17], (DEPTH, R_V_WIDTH, D), R_V_WIDTH ** -0.5),
        'w_out': nrm(ks[18], (DEPTH, D, D), D ** -0.5),
        'ffn_up': nrm(ks[19], (DEPTH, D, 2 * FFN_DIM), D ** -0.5),
        'ffn_conv_w': nrm(ks[20], (DEPTH, FFN_CONV, FFN_DIM), FFN_CONV ** -0.5),
        'ffn_conv_b': nrm(ks[21], (DEPTH, FFN_DIM), 0.02),
        'ffn_down': nrm(ks[22], (DEPTH, FFN_DIM, D), FFN_DIM ** -0.5),
        'final_norm_w': 1.0 + nrm(ks[23], (D,), 0.02),
    }


def reference(x, c, ctx, c_ctx, mod_w, mod_b, norm1_w, norm2_w, w_in, m_gate_bias, m_conv_w, m_norm_w,
              a_sink, ret_logit, ret_norm_w, w_br_m, w_br_a, w_br_r, w_out, ffn_up, ffn_conv_w, ffn_conv_b,
              ffn_down, final_norm_w):
    S = x.shape[1]
    rows = S // GRID_W
    row = jnp.repeat(jnp.arange(rows), GRID_W)
    col = jnp.tile(jnp.arange(GRID_W), rows)
    cx = ctx
    for l in range(DEPTH):
        last = l == DEPTH - 1
        sh1, sc1, g1, sh2, sc2, g2 = [t[:, None, :] for t in _modulation(c, mod_w[l], mod_b[l], 6)]
        if last:
            csh1, csc1 = _modulation(c_ctx, mod_w[l][:, :2 * D_MODEL], mod_b[l][:2 * D_MODEL], 2)
        else:
            csh1, csc1, cg1, csh2, csc2, cg2 = _modulation(c_ctx, mod_w[l], mod_b[l], 6)
        h = _rmsnorm(x, norm1_w[l]) * (1.0 + sc1) + sh1
        hc = _rmsnorm(cx, norm1_w[l]) * (1.0 + csc1) + csh1
        y, yc = _token_mixer(h, hc, row, col, w_in[l], m_gate_bias[l], m_conv_w[l], m_norm_w[l], a_sink[l],
                             ret_logit[l], ret_norm_w[l], w_br_m[l], w_br_a[l], w_br_r[l], w_out[l],
                             not last)
        x = x + g1 * y
        h2 = _rmsnorm(x, norm2_w[l]) * (1.0 + sc2) + sh2
        x = x + g2 * _conv_ffn(h2, ffn_up[l], ffn_conv_w[l], ffn_conv_b[l], ffn_down[l])
        if not last:
            cx = cx + cg1 * yc
            hc2 = _rmsnorm(cx, norm2_w[l]) * (1.0 + csc2) + csh2
            cx = cx + cg2 * _conv_ffn(hc2, ffn_up[l], ffn_conv_w[l], ffn_conv_b[l], ffn_down[l])
    return _rmsnorm(x, final_norm_w)
```

```python
import functools

import jax
import jax.numpy as jnp
from jax import lax
from jax.experimental import pallas as pl
from jax.experimental.pallas import tpu as pltpu

F32 = jnp.float32
BF16 = jnp.bfloat16

D_MODEL = 1024
GRID_W = 64
NORM_EPS = 1e-6
ROPE_BASE = 10000.0
NEG_INF = -1e30

M_HEADS = 4
M_HEAD_DIM = 128
M_WIDTH = M_HEADS * M_HEAD_DIM
A_HEADS = 8
A_KV_HEADS = 2
A_GROUP = A_HEADS // A_KV_HEADS
A_HEAD_DIM = 64
A_WIDTH = A_HEADS * A_HEAD_DIM
A_KV_WIDTH = A_KV_HEADS * A_HEAD_DIM
A_WINDOW = 128
A_BLOCK = 128
R_HEADS = 4
R_QK_DIM = 128
R_V_DIM = 256
R_QK_WIDTH = R_HEADS * R_QK_DIM
R_V_WIDTH = R_HEADS * R_V_DIM
FFN_DIM = 2816

IN_SPLITS = (M_WIDTH, M_WIDTH, M_WIDTH, M_WIDTH, 4 * M_HEADS,
             A_WIDTH, A_KV_WIDTH, A_KV_WIDTH,
             R_QK_WIDTH, R_QK_WIDTH, R_V_WIDTH, R_V_WIDTH,
             D_MODEL, D_MODEL, D_MODEL)

TM = 256
HALO = 16
CHUNK = 128
FFN_NC = 256
LANES = 128
VMEM_LIMIT = 56 * 1024 * 1024


def _sigmoid(x):
    return 1.0 / (1.0 + jnp.exp(-x))


def _log_sigmoid(x):
    return jnp.minimum(x, 0.0) - jnp.log1p(jnp.exp(-jnp.abs(x)))


def _dot(a, b):
    return jnp.dot(a, b, preferred_element_type=F32)


def _dot_nt(a, b):
    return lax.dot_general(a, b, (((1,), (1,)), ((), ())), preferred_element_type=F32)


def _dot_tn(a, b):
    return lax.dot_general(a, b, (((0,), (0,)), ((), ())), preferred_element_type=F32)


def _const_spec(shape):
    nd = len(shape)
    return pl.BlockSpec(shape, lambda *_: (0,) * nd, pipeline_mode=pl.Buffered(1))


def _params(sem):
    return pltpu.CompilerParams(dimension_semantics=sem, vmem_limit_bytes=VMEM_LIMIT)


def _mod_kernel(c_ref, w_ref, b_ref, o_ref):
    c = c_ref[...]
    s = (c * _sigmoid(c)).astype(BF16)
    o_ref[0] = _dot(s, w_ref[0].astype(BF16)) + b_ref[0]


def _modulation(cpad, mod_w, mod_b):
    depth, d, n = mod_w.shape
    rows = cpad.shape[0]
    bn = 1024
    return pl.pallas_call(
        _mod_kernel,
        grid=(depth, n // bn),
        in_specs=[pl.BlockSpec((rows, d), lambda l, j: (0, 0)),
                  pl.BlockSpec((1, d, bn), lambda l, j: (l, 0, j)),
                  pl.BlockSpec((1, 1, bn), lambda l, j: (l, 0, j))],
        out_specs=pl.BlockSpec((1, rows, bn), lambda l, j: (l, 0, j)),
        out_shape=jax.ShapeDtypeStruct((depth, rows, n), F32),
        compiler_params=_params(("parallel", "parallel")),
        name="modulation",
    )(cpad, mod_w, mod_b.reshape(depth, 1, n))


def _norm_mod(xv, nw, sc, sh):
    ms = jnp.mean(xv * xv, axis=-1, keepdims=True)
    return (xv * lax.rsqrt(ms + NORM_EPS) * nw) * (1.0 + sc) + sh


def _seg_cumsum(x, axis, reverse):
    n = x.shape[axis]
    idx = lax.broadcasted_iota(jnp.int32, x.shape, axis)
    s = 1
    while s < n:
        if reverse:
            shifted = pltpu.roll(x, n - s, axis)
            x = x + jnp.where(idx < n - s, shifted, 0.0)
        else:
            shifted = pltpu.roll(x, s, axis)
            x = x + jnp.where(idx >= s, shifted, 0.0)
        s *= 2
    return x


def _gate_transform(raw, gate_axis):
    tok_axis = 1 - gate_axis
    k = lax.broadcasted_iota(jnp.int32, raw.shape, gate_axis) % 4
    lf = _log_sigmoid(raw)
    cum_f = _seg_cumsum(jnp.where(k == 1, lf, 0.0), tok_axis, False)
    cum_b = _seg_cumsum(jnp.where(k == 3, lf, 0.0), tok_axis, True)
    bsel = jnp.where(k == 1, cum_f, jnp.where(k == 3, cum_b, 0.0))
    n = raw.shape[gate_axis]
    bnext = pltpu.roll(bsel, n - 1, gate_axis)
    return jnp.where(k % 2 == 1, bsel, raw - bnext)


def _pre_kernel(x_ref, xp_ref, xn_ref, mod_ref, nw_ref, wqk_ref, cw_ref, wr_ref, wg_ref, wgt_ref,
                gb_ref, gbt_ref, ca_ref, sa_ref, cr_ref, sr_ref,
                hb_ref, mq_ref, mk_ref, mv_ref, gc_ref, gr_ref, aq_ref, ak_ref, av_ref,
                rq_ref, rk_ref, rv_ref, hext_ref):
    t = pl.program_id(0)
    nt = pl.num_programs(0)
    sh = mod_ref[0, 0, 0:1, :]
    sc = mod_ref[0, 0, 1:2, :]
    nw = nw_ref[...]
    hb = _norm_mod(x_ref[0], nw, sc, sh).astype(BF16)
    hb_ref[0] = hb

    hp = _norm_mod(xp_ref[0], nw, sc, sh)
    hn = _norm_mod(xn_ref[0], nw, sc, sh)
    hp = jnp.where(t <= 1, 0.0, hp)
    hn = jnp.where((t == 0) | (t == nt - 1), 0.0, hn)
    hext_ref[0:HALO, :] = hp.astype(BF16)
    hext_ref[HALO:HALO + TM, :] = hb
    hext_ref[HALO + TM:, :] = hn.astype(BF16)
    hext = hext_ref[...]
    ext = TM + 2 * HALO

    for j, (dst, scale) in enumerate(((mq_ref, 1.0), (mk_ref, M_HEAD_DIM ** -0.5))):
        p = _dot(hext, wqk_ref[:, j * M_WIDTH:(j + 1) * M_WIDTH])
        cw = cw_ref[:, j * M_WIDTH:(j + 1) * M_WIDTH]
        prev = pltpu.roll(p, 1, 0)[HALO:HALO + TM]
        nxt = pltpu.roll(p, ext - 1, 0)[HALO:HALO + TM]
        conv = prev * cw[0:1] + p[HALO:HALO + TM] * cw[1:2] + nxt * cw[2:3]
        act = conv * _sigmoid(conv) * scale
        for h in range(M_HEADS):
            dst[0, h] = act[:, h * M_HEAD_DIM:(h + 1) * M_HEAD_DIM].astype(BF16)

    off = 0
    p = _dot(hb, wr_ref[:, off:off + M_WIDTH])
    off += M_WIDTH
    for h in range(M_HEADS):
        mv_ref[0, h] = p[:, h * M_HEAD_DIM:(h + 1) * M_HEAD_DIM].astype(BF16)

    lane = lax.broadcasted_iota(jnp.int32, (TM, LANES), 1)
    first16 = (lane % 32) < 16
    ca = ca_ref[...]
    sa = sa_ref[...]

    def rope_a(xs):
        partner = jnp.where(first16, pltpu.roll(xs, LANES - 16, 1), pltpu.roll(xs, 16, 1))
        return xs * ca + partner * sa

    p = _dot(hb, wr_ref[:, off:off + A_WIDTH])
    off += A_WIDTH
    for s in range(A_WIDTH // LANES):
        r = rope_a(p[:, s * LANES:(s + 1) * LANES]).astype(BF16)
        aq_ref[0, 2 * s] = r[:, 0:A_HEAD_DIM]
        aq_ref[0, 2 * s + 1] = r[:, A_HEAD_DIM:]
    p = _dot(hb, wr_ref[:, off:off + 2 * A_KV_WIDTH])
    off += 2 * A_KV_WIDTH
    r = rope_a(p[:, 0:LANES]).astype(BF16)
    pv = p[:, LANES:].astype(BF16)
    for h in range(A_KV_HEADS):
        ak_ref[0, h] = r[:, h * A_HEAD_DIM:(h + 1) * A_HEAD_DIM]
        av_ref[0, h] = pv[:, h * A_HEAD_DIM:(h + 1) * A_HEAD_DIM]

    cr = cr_ref[...]
    sr = sr_ref[...]
    for dst, scale in ((rq_ref, 1.0), (rk_ref, R_QK_DIM ** -0.5)):
        p = _dot(hb, wr_ref[:, off:off + R_QK_WIDTH])
        off += R_QK_WIDTH
        for h in range(R_HEADS):
            xs = p[:, h * R_QK_DIM:(h + 1) * R_QK_DIM]
            rot = xs * cr + pltpu.roll(xs, R_QK_DIM // 2, 1) * sr
            dst[0, h] = (rot * scale).astype(BF16)
    p = _dot(hb, wr_ref[:, off:off + R_V_WIDTH])
    for h in range(R_HEADS):
        rv_ref[0, h] = p[:, h * R_V_DIM:(h + 1) * R_V_DIM].astype(BF16)

    gcol = _dot(hb, wg_ref[...]) + gb_ref[...]
    grow = _dot_nt(wgt_ref[...], hb) + gbt_ref[:, 0:1]
    for c in range(TM // CHUNK):
        gc = _gate_transform(gcol[c * CHUNK:(c + 1) * CHUNK], 1)
        gr = _gate_transform(grow[:, c * CHUNK:(c + 1) * CHUNK], 0)
        for hp_ in range(2):
            gc_ref[0, hp_, c * CHUNK:(c + 1) * CHUNK, :] = gc[:, 8 * hp_:8 * hp_ + 8]
            gr_ref[0, hp_, c] = gr[8 * hp_:8 * hp_ + 8]


def _pre_call(xc, modsel, nw, wqk, cw, wr, wg, wgt, gb, gbt, tabs):
    B, T, D = xc.shape
    nt = T // TM
    hb16 = TM // HALO
    nch = T // CHUNK

    def tile(t, b):
        return (b, t, 0)

    def head_tile(t, b):
        return (b, 0, t, 0)

    in_specs = [
        pl.BlockSpec((1, TM, D), tile),
        pl.BlockSpec((1, HALO, D), lambda t, b: (b, jnp.maximum(t * hb16 - 1, 0), 0)),
        pl.BlockSpec((1, HALO, D), lambda t, b: (b, jnp.minimum((t + 1) * hb16, T // HALO - 1), 0)),
        pl.BlockSpec((1, 1, 6, D), lambda t, b: (b, jnp.minimum(t, 1), 0, 0)),
        _const_spec(nw.shape), _const_spec(wqk.shape), _const_spec(cw.shape), _const_spec(wr.shape),
        _const_spec(wg.shape), _const_spec(wgt.shape), _const_spec(gb.shape), _const_spec(gbt.shape),
    ] + [pl.BlockSpec((TM, LANES), lambda t, b: (t, 0)) for _ in range(4)]
    out_shape = [
        jax.ShapeDtypeStruct((B, T, D), BF16),
        jax.ShapeDtypeStruct((B, M_HEADS, T, M_HEAD_DIM), BF16),
        jax.ShapeDtypeStruct((B, M_HEADS, T, M_HEAD_DIM), BF16),
        jax.ShapeDtypeStruct((B, M_HEADS, T, M_HEAD_DIM), BF16),
        jax.ShapeDtypeStruct((B, 2, T, 8), F32),
        jax.ShapeDtypeStruct((B, 2, nch, 8, CHUNK), F32),
        jax.ShapeDtypeStruct((B, A_HEADS, T, A_HEAD_DIM), BF16),
        jax.ShapeDtypeStruct((B, A_KV_HEADS, T, A_HEAD_DIM), BF16),
        jax.ShapeDtypeStruct((B, A_KV_HEADS, T, A_HEAD_DIM), BF16),
        jax.ShapeDtypeStruct((B, R_HEADS, T, R_QK_DIM), BF16),
        jax.ShapeDtypeStruct((B, R_HEADS, T, R_QK_DIM), BF16),
        jax.ShapeDtypeStruct((B, R_HEADS, T, R_V_DIM), BF16),
    ]
    out_specs = [
        pl.BlockSpec((1, TM, D), tile),
        pl.BlockSpec((1, M_HEADS, TM, M_HEAD_DIM), head_tile),
        pl.BlockSpec((1, M_HEADS, TM, M_HEAD_DIM), head_tile),
        pl.BlockSpec((1, M_HEADS, TM, M_HEAD_DIM), head_tile),
        pl.BlockSpec((1, 2, TM, 8), head_tile),
        pl.BlockSpec((1, 2, TM // CHUNK, 8, CHUNK), lambda t, b: (b, 0, t, 0, 0)),
        pl.BlockSpec((1, A_HEADS, TM, A_HEAD_DIM), head_tile),
        pl.BlockSpec((1, A_KV_HEADS, TM, A_HEAD_DIM), head_tile),
        pl.BlockSpec((1, A_KV_HEADS, TM, A_HEAD_DIM), head_tile),
        pl.BlockSpec((1, R_HEADS, TM, R_QK_DIM), head_tile),
        pl.BlockSpec((1, R_HEADS, TM, R_QK_DIM), head_tile),
        pl.BlockSpec((1, R_HEADS, TM, R_V_DIM), head_tile),
    ]
    return pl.pallas_call(
        _pre_kernel,
        grid=(nt, B),
        in_specs=in_specs,
        out_specs=out_specs,
        out_shape=out_shape,
        scratch_shapes=[pltpu.VMEM((TM + 2 * HALO, D), BF16)],
        compiler_params=_params(("parallel", "parallel")),
        name="pre_proj",
    )(xc, xc, xc, modsel, nw, wqk, cw, wr, wg, wgt, gb, gbt, *tabs)


def _bwd_chunk(i, n_ctx, n_all):
    return jnp.where(i < n_ctx, n_ctx - 1 - i, n_all - 1 + n_ctx - i)


def _mlstm_kernel(q_ref, k_ref, v_ref, gc_ref, gr_ref, o_ref, st_ref, n_ref, m_ref, *, n_ctx):
    n_all = gr_ref.shape[2]
    o_ref[...] = jnp.zeros(o_ref.shape, F32)
    st_ref[...] = jnp.zeros(st_ref.shape, F32)
    n_ref[...] = jnp.zeros(n_ref.shape, F32)
    m_ref[...] = jnp.zeros(m_ref.shape, F32)
    row = lax.broadcasted_iota(jnp.int32, (CHUNK, CHUNK), 0)
    col = lax.broadcasted_iota(jnp.int32, (CHUNK, CHUNK), 1)
    tri = (row >= col, row <= col)

    def unit(c, hh, d):
        u = 2 * hh + d
        c0 = pl.multiple_of(c * CHUNK, CHUNK)
        q = q_ref[0, hh, pl.ds(c0, CHUNK), :]
        k = k_ref[0, hh, pl.ds(c0, CHUNK), :]
        v = v_ref[0, hh, pl.ds(c0, CHUNK), :]
        gcv = gc_ref[0, 0, pl.ds(c0, CHUNK), :]
        g0 = 4 * hh + 2 * d
        a0c = gcv[:, g0:g0 + 1]
        bc = gcv[:, g0 + 1:g0 + 2]
        a0r = gr_ref[0, 0, c, g0:g0 + 1, :]
        g = bc[CHUNK - 1:CHUNK] if d == 0 else bc[0:1]
        st = st_ref[u]
        nv = n_ref[u]
        m = m_ref[u]

        dlog = jnp.where(tri[d], bc + a0r, NEG_INF)
        mj = jnp.maximum(bc + m, jnp.max(dlog, axis=1, keepdims=True))
        s = _dot_nt(q, k) * jnp.exp(dlog - mj)
        iw = jnp.exp(bc + m - mj)
        num = iw * _dot(q, st.astype(BF16)) + _dot(s.astype(BF16), v)
        den = (iw * jnp.sum(q.astype(F32) * nv, axis=1, keepdims=True)
               + jnp.sum(s, axis=1, keepdims=True))
        inv = 1.0 / jnp.maximum(jnp.abs(den), jnp.exp(-mj))
        o_ref[0, hh, pl.ds(c0, CHUNK), :] += num * inv

        ac = g + a0c
        m_new = jnp.maximum(g + m, jnp.max(ac, axis=0, keepdims=True))
        w = jnp.exp(ac - m_new)
        dec = jnp.exp(g + m - m_new)
        vw = (v.astype(F32) * w).astype(BF16)
        st_ref[u] = dec * st + _dot_tn(k, vw)
        n_ref[u] = dec * nv + jnp.sum(k.astype(F32) * w, axis=0, keepdims=True)
        m_ref[u] = m_new

    def step(i, carry):
        cb = _bwd_chunk(i, n_ctx, n_all)
        for hh in range(2):
            unit(i, hh, 0)
            unit(cb, hh, 1)
        return carry

    lax.fori_loop(0, n_all, step, 0)


def _mlstm_call(mq, mk, mv, gc, gr, n_ctx):
    B, H, T, dh = mq.shape
    nch = T // CHUNK
    qspec = pl.BlockSpec((1, 2, T, dh), lambda b, p: (b, p, 0, 0))
    return pl.pallas_call(
        functools.partial(_mlstm_kernel, n_ctx=n_ctx),
        grid=(B, H // 2),
        in_specs=[qspec, qspec, qspec,
                  pl.BlockSpec((1, 1, T, 8), lambda b, p: (b, p, 0, 0)),
                  pl.BlockSpec((1, 1, nch, 8, CHUNK), lambda b, p: (b, p, 0, 0, 0))],
        out_specs=pl.BlockSpec((1, 2, T, dh), lambda b, p: (b, p, 0, 0)),
        out_shape=jax.ShapeDtypeStruct((B, H, T, dh), F32),
        scratch_shapes=[pltpu.VMEM((4, dh, dh), F32), pltpu.VMEM((4, 1, dh), F32),
                        pltpu.VMEM((4, 1, 1), F32)],
        compiler_params=_params(("parallel", "parallel")),
        name="mlstm",
    )(mq, mk, mv, gc, gr)


def _ret_kernel(q_ref, k_ref, v_ref, lg_ref, o_ref, st_ref, dec_ref, *, n_ctx):
    n_all = q_ref.shape[2] // CHUNK
    o_ref[...] = jnp.zeros(o_ref.shape, F32)
    st_ref[...] = jnp.zeros(st_ref.shape, F32)
    row = lax.broadcasted_iota(jnp.int32, (CHUNK, CHUNK), 0).astype(F32)
    col = lax.broadcasted_iota(jnp.int32, (CHUNK, CHUNK), 1).astype(F32)
    pos = lax.broadcasted_iota(jnp.int32, (CHUNK, 1), 0).astype(F32)
    consts = []
    for hh in range(2):
        for d in range(2):
            lg = _log_sigmoid(lg_ref[hh, d])[0:1, 0:1]
            diff = row - col if d == 0 else col - row
            dec_ref[2 * hh + d] = jnp.where(diff >= 0, jnp.exp(lg * jnp.maximum(diff, 0.0)), 0.0)
            pin = pos if d == 0 else CHUNK - 1.0 - pos
            inter = jnp.exp(lg * (pin + 1.0))
            w = jnp.exp(lg * (CHUNK - 1.0 - pin))
            cd = jnp.exp(lg * CHUNK)
            consts.append((inter, w, cd))

    def unit(c, hh, d):
        u = 2 * hh + d
        inter, w, cd = consts[u]
        c0 = pl.multiple_of(c * CHUNK, CHUNK)
        q = q_ref[0, hh, pl.ds(c0, CHUNK), :]
        k = k_ref[0, hh, pl.ds(c0, CHUNK), :]
        v = v_ref[0, hh, pl.ds(c0, CHUNK), :]
        st = st_ref[u]
        s = (_dot_nt(q, k) * dec_ref[u]).astype(BF16)
        o_ref[0, hh, pl.ds(c0, CHUNK), :] += _dot(s, v) + inter * _dot(q, st.astype(BF16))
        kw = (k.astype(F32) * w).astype(BF16)
        st_ref[u] = cd * st + _dot_tn(kw, v)

    def step(i, carry):
        cb = _bwd_chunk(i, n_ctx, n_all)
        for hh in range(2):
            unit(i, hh, 0)
            unit(cb, hh, 1)
        return carry

    lax.fori_loop(0, n_all, step, 0)


def _ret_call(rq, rk, rv, lgb, n_ctx):
    B, H, T, dk = rq.shape
    dv = rv.shape[-1]
    qspec = pl.BlockSpec((1, 2, T, dk), lambda b, p: (b, p, 0, 0))
    vspec = pl.BlockSpec((1, 2, T, dv), lambda b, p: (b, p, 0, 0))
    return pl.pallas_call(
        functools.partial(_ret_kernel, n_ctx=n_ctx),
        grid=(B, H // 2),
        in_specs=[qspec, qspec, vspec,
                  pl.BlockSpec((2, 2, 8, LANES), lambda b, p: (p, 0, 0, 0))],
        out_specs=vspec,
        out_shape=jax.ShapeDtypeStruct((B, H, T, dv), F32),
        scratch_shapes=[pltpu.VMEM((4, dk, dv), F32), pltpu.VMEM((4, CHUNK, CHUNK), F32)],
        compiler_params=_params(("parallel", "parallel")),
        name="retention",
    )(rq, rk, rv, lgb)


def _attn_kernel(q_ref, k_ref, v_ref, sink_ref, o_ref, *, n_ctx_tok, need_ctx):
    T = q_ref.shape[2]
    L = n_ctx_tok
    S = T - L
    nb = S // A_BLOCK
    span = 3 * A_BLOCK
    rows = A_GROUP * A_BLOCK
    scale = A_HEAD_DIM ** -0.5
    kc = k_ref[0, 0, 0:L, :]
    vc = v_ref[0, 0, 0:L, :]
    sink = sink_ref[0]
    qpos = lax.broadcasted_iota(jnp.int32, (rows, 1), 0) % A_BLOCK
    kpos = lax.broadcasted_iota(jnp.int32, (1, span), 1)

    def store(q0, o):
        ob = o.astype(BF16)
        for g in range(A_GROUP):
            o_ref[0, pl.ds(q0, A_BLOCK), g * A_HEAD_DIM:(g + 1) * A_HEAD_DIM] = (
                ob[g * A_BLOCK:(g + 1) * A_BLOCK])

    for cbk in range(L // A_BLOCK):
        q0 = cbk * A_BLOCK
        if need_ctx:
            qs = q_ref[0, :, q0:q0 + A_BLOCK, :].reshape(rows, A_HEAD_DIM)
            s = _dot_nt(qs, kc) * scale
            m = jnp.maximum(jnp.max(s, axis=1, keepdims=True), sink)
            p = jnp.exp(s - m)
            den = jnp.sum(p, axis=1, keepdims=True) + jnp.exp(sink - m)
            store(q0, _dot(p.astype(BF16), vc) * (1.0 / den))
        else:
            store(q0, jnp.zeros((rows, A_HEAD_DIM), F32))

    def block(n, carry):
        q0 = pl.multiple_of(L + n * A_BLOCK, A_BLOCK)
        kstart = jnp.clip((n - 1) * A_BLOCK, 0, S - span)
        k0 = pl.multiple_of(L + kstart, A_BLOCK)
        qs = q_ref[0, :, pl.ds(q0, A_BLOCK), :].reshape(rows, A_HEAD_DIM)
        kl = k_ref[0, 0, pl.ds(k0, span), :]
        vl = v_ref[0, 0, pl.ds(k0, span), :]
        s_loc = _dot_nt(qs, kl) * scale
        qi = n * A_BLOCK + qpos
        kj = kstart + kpos
        s_loc = jnp.where(jnp.abs(qi - kj) <= A_WINDOW, s_loc, NEG_INF)
        s_ctx = _dot_nt(qs, kc) * scale
        m = jnp.maximum(jnp.maximum(jnp.max(s_loc, axis=1, keepdims=True),
                                    jnp.max(s_ctx, axis=1, keepdims=True)), sink)
        p_loc = jnp.exp(s_loc - m)
        p_ctx = jnp.exp(s_ctx - m)
        den = (jnp.sum(p_loc, axis=1, keepdims=True) + jnp.sum(p_ctx, axis=1, keepdims=True)
               + jnp.exp(sink - m))
        o = _dot(p_loc.astype(BF16), vl) + _dot(p_ctx.astype(BF16), vc)
        store(q0, o * (1.0 / den))
        return carry

    lax.fori_loop(0, nb, block, 0)


def _attn_call(aq, ak, av, sink_rows, n_ctx_tok, need_ctx):
    B, _, T, hd = aq.shape
    rows = A_GROUP * A_BLOCK
    kvspec = pl.BlockSpec((1, 1, T, hd), lambda b, h: (b, h, 0, 0))
    return pl.pallas_call(
        functools.partial(_attn_kernel, n_ctx_tok=n_ctx_tok, need_ctx=need_ctx),
        grid=(B, A_KV_HEADS),
        in_specs=[pl.BlockSpec((1, A_GROUP, T, hd), lambda b, h: (b, h, 0, 0)), kvspec, kvspec,
                  pl.BlockSpec((1, rows, 1), lambda b, h: (h, 0, 0))],
        out_specs=pl.BlockSpec((1, T, A_GROUP * hd), lambda b, h: (b, 0, h)),
        out_shape=jax.ShapeDtypeStruct((B, T, A_WIDTH), BF16),
        compiler_params=_params(("parallel", "parallel")),
        name="window_attn",
    )(aq, ak, av, sink_rows)


def _head_norm(y):
    mu = jnp.mean(y, axis=-1, keepdims=True)
    yc = y - mu
    var = jnp.mean(yc * yc, axis=-1, keepdims=True)
    return yc * lax.rsqrt(var + NORM_EPS)


def _post_kernel(x_ref, hb_ref, hm_ref, ya_ref, hr_ref, mod_ref, wg_ref, mnw_ref, rnw_ref,
                 wbm_ref, wba_ref, wbr_ref, wo_ref, o_ref):
    hb = hb_ref[0]
    off = 0
    hm = jnp.concatenate([_head_norm(hm_ref[0, h]) for h in range(M_HEADS)], axis=1) * mnw_ref[...]
    ym = (_sigmoid(_dot(hb, wg_ref[:, off:off + M_WIDTH])) * hm).astype(BF16)
    off += M_WIDTH
    hr = jnp.concatenate([_head_norm(hr_ref[0, h]) for h in range(R_HEADS)], axis=1) * rnw_ref[...]
    rg = _dot(hb, wg_ref[:, off:off + R_V_WIDTH])
    yr = (rg * _sigmoid(rg) * hr).astype(BF16)
    off += R_V_WIDTH
    z = _sigmoid(_dot(hb, wg_ref[:, off:off + D_MODEL])) * _dot(ym, wbm_ref[...])
    off += D_MODEL
    z = z + _sigmoid(_dot(hb, wg_ref[:, off:off + D_MODEL])) * _dot(ya_ref[0], wba_ref[...])
    off += D_MODEL
    z = z + _sigmoid(_dot(hb, wg_ref[:, off:off + D_MODEL])) * _dot(yr, wbr_ref[...])
    y = _dot(z.astype(BF16), wo_ref[...])
    o_ref[0] = x_ref[0] + mod_ref[0, 0, 2:3, :] * y


def _post_call(xc, hb, hm, ya, hr, modsel, wg, mnw, rnw, wbm, wba, wbr, wo, t_off):
    B, T, D = xc.shape
    nt = T // TM - t_off

    def tile(t, b):
        return (b, t + t_off, 0)

    def head_tile(t, b):
        return (b, 0, t + t_off, 0)

    return pl.pallas_call(
        _post_kernel,
        grid=(nt, B),
        in_specs=[pl.BlockSpec((1, TM, D), tile),
                  pl.BlockSpec((1, TM, D), tile),
                  pl.BlockSpec((1, M_HEADS, TM, M_HEAD_DIM), head_tile),
                  pl.BlockSpec((1, TM, A_WIDTH), tile),
                  pl.BlockSpec((1, R_HEADS, TM, R_V_DIM), head_tile),
                  pl.BlockSpec((1, 1, 6, D), lambda t, b: (b, jnp.minimum(t + t_off, 1), 0, 0)),
                  _const_spec(wg.shape), _const_spec(mnw.shape), _const_spec(rnw.shape),
                  _const_spec(wbm.shape), _const_spec(wba.shape), _const_spec(wbr.shape),
                  _const_spec(wo.shape)],
        out_specs=pl.BlockSpec((1, TM, D), lambda t, b: (b, t, 0)),
        out_shape=jax.ShapeDtypeStruct((B, nt * TM, D), F32),
        compiler_params=_params(("parallel", "parallel")),
        name="post_merge",
    )(xc, hb, hm, ya, hr, modsel, wg, mnw, rnw, wbm, wba, wbr, wo)


def _ffn_kernel(x_ref, xp_ref, xn_ref, mod_ref, nw_ref, wa_ref, wb_ref, cw_ref, cb_ref, wd_ref, fw_ref,
                o_ref, hext_ref, u_ref, *, has_ctx, final):
    t = pl.program_id(0)
    nt = pl.num_programs(0)
    first_lat = 1 if has_ctx else 0
    sh = mod_ref[0, 0, 3:4, :]
    sc = mod_ref[0, 0, 4:5, :]
    g2 = mod_ref[0, 0, 5:6, :]
    nw = nw_ref[...]
    x = x_ref[0]
    hp = _norm_mod(xp_ref[0], nw, sc, sh)
    hn = _norm_mod(xn_ref[0], nw, sc, sh)
    hp = jnp.where(t <= first_lat, 0.0, hp)
    hn = jnp.where((t == first_lat - 1) | (t == nt - 1), 0.0, hn)
    hext_ref[0:HALO, :] = hp.astype(BF16)
    hext_ref[HALO:HALO + TM, :] = _norm_mod(x, nw, sc, sh).astype(BF16)
    hext_ref[HALO + TM:, :] = hn.astype(BF16)
    ext = TM + 2 * HALO
    for j in range(FFN_DIM // FFN_NC):
        cs = slice(j * FFN_NC, (j + 1) * FFN_NC)
        a = _dot(hext_ref[...], wa_ref[:, cs])
        b = _dot(hext_ref[HALO:HALO + TM, :], wb_ref[:, cs])
        cw = cw_ref[:, cs]
        prev = pltpu.roll(a, 1, 0)[HALO:HALO + TM]
        nxt = pltpu.roll(a, ext - 1, 0)[HALO:HALO + TM]
        conv = prev * cw[0:1] + a[HALO:HALO + TM] * cw[1:2] + nxt * cw[2:3] + cb_ref[:, cs]
        u_ref[:, cs] = (conv * _sigmoid(conv) * b).astype(BF16)
    y = x + g2 * _dot(u_ref[...], wd_ref[...])
    if final:
        ms = jnp.mean(y * y, axis=-1, keepdims=True)
        y = y * lax.rsqrt(ms + NORM_EPS) * fw_ref[...]
    o_ref[0] = y


def _ffn_call(xs, modsel, nw, wa, wb, cw, cb, wd, fw, has_ctx, final):
    B, T, D = xs.shape
    nt = T // TM
    hb16 = TM // HALO
    first_lat = 1 if has_ctx else 0
    tile = pl.BlockSpec((1, TM, D), lambda t, b: (b, t, 0))
    return pl.pallas_call(
        functools.partial(_ffn_kernel, has_ctx=has_ctx, final=final),
        grid=(nt, B),
        in_specs=[tile,
                  pl.BlockSpec((1, HALO, D), lambda t, b: (b, jnp.maximum(t * hb16 - 1, 0), 0)),
                  pl.BlockSpec((1, HALO, D), lambda t, b: (b, jnp.minimum((t + 1) * hb16, T // HALO - 1), 0)),
                  pl.BlockSpec((1, 1, 6, D), lambda t, b: (b, jnp.minimum(t + 1 - first_lat, 1), 0, 0)),
                  _const_spec(nw.shape), _const_spec(wa.shape), _const_spec(wb.shape),
                  _const_spec(cw.shape), _const_spec(cb.shape), _const_spec(wd.shape),
                  _const_spec(fw.shape)],
        out_specs=tile,
        out_shape=jax.ShapeDtypeStruct((B, T, D), F32),
        scratch_shapes=[pltpu.VMEM((TM + 2 * HALO, D), BF16), pltpu.VMEM((TM, FFN_DIM), BF16)],
        compiler_params=_params(("parallel", "parallel")),
        name="conv_ffn",
    )(xs, xs, xs, modsel, nw, wa, wb, cw, cb, wd, fw)


def _split_cols(w):
    out = []
    acc = 0
    for s in IN_SPLITS:
        out.append(w[:, acc:acc + s])
        acc += s
    return out


def _rope_tables(L, S):
    T = L + S
    j = jnp.arange(LANES)
    jj = j % 32
    inv_a = ROPE_BASE ** (-(jj % 16).astype(F32) / 16.0)
    t = jnp.arange(S)
    pos = jnp.where(((j % A_HEAD_DIM) // 32 == 0)[None, :], (t // GRID_W)[:, None], (t % GRID_W)[:, None])
    ang = pos.astype(F32) * inv_a[None, :]
    sign = jnp.where(jj < 16, -1.0, 1.0)[None, :]
    ca = jnp.concatenate([jnp.ones((L, LANES), F32), jnp.cos(ang)], axis=0)
    sa = jnp.concatenate([jnp.zeros((L, LANES), F32), jnp.sin(ang) * sign], axis=0)
    half = R_QK_DIM // 2
    inv_r = ROPE_BASE ** (-(j % half).astype(F32) / half)
    angr = jnp.arange(T).astype(F32)[:, None] * inv_r[None, :]
    signr = jnp.where(j < half, -1.0, 1.0)[None, :]
    return ca, sa, jnp.cos(angr), jnp.sin(angr) * signr


def kernel(x, c, ctx, c_ctx, mod_w, mod_b, norm1_w, norm2_w, w_in, m_gate_bias, m_conv_w, m_norm_w,
           a_sink, ret_logit, ret_norm_w, w_br_m, w_br_a, w_br_r, w_out, ffn_up, ffn_conv_w, ffn_conv_b,
           ffn_down, final_norm_w):
    B, S, D = x.shape
    L = ctx.shape[1]
    depth = mod_w.shape[0]
    assert D == D_MODEL and L == TM and S % TM == 0 and S >= 3 * A_BLOCK
    T = L + S
    n_ctx = L // CHUNK

    rows = -(-(B + 1) // 8) * 8
    cpad = jnp.zeros((rows, D), F32).at[:B].set(c).at[B].set(c_ctx)
    mods = _modulation(cpad, mod_w, mod_b)
    tabs = _rope_tables(L, S)
    xc = jnp.concatenate([ctx, x], axis=1)

    gperm = jnp.arange(4 * M_HEADS).reshape(4, M_HEADS).T.reshape(-1)

    for l in range(depth):
        last = l == depth - 1
        lat = mods[l, :B].reshape(B, 6, D)
        cm = jnp.broadcast_to(mods[l, B].reshape(1, 6, D), (B, 6, D))
        modsel = jnp.stack([cm, lat], axis=1)

        (w_mq, w_mk, w_mv, w_mo, w_mg, w_aq, w_ak, w_av,
         w_rq, w_rk, w_rv, w_rg, w_gm, w_ga, w_gr) = _split_cols(w_in[l])
        wqk = jnp.concatenate([w_mq, w_mk], axis=1).astype(BF16)
        wr = jnp.concatenate([w_mv, w_aq, w_ak, w_av, w_rq, w_rk, w_rv], axis=1).astype(BF16)
        wg_cols = w_mg[:, gperm]
        wg = jnp.pad(wg_cols, ((0, 0), (0, LANES - 4 * M_HEADS))).astype(BF16)
        wgt = wg_cols.T.astype(BF16)
        gbias = m_gate_bias[l][gperm]
        gb = jnp.pad(gbias, (0, LANES - 4 * M_HEADS)).reshape(1, LANES)
        gbt = jnp.broadcast_to(gbias.reshape(-1, 1), (4 * M_HEADS, LANES))

        (hb, mq, mk, mv, gc, gr, aq, ak, av, rq, rk, rv) = _pre_call(
            xc, modsel, norm1_w[l].reshape(1, D), wqk, m_conv_w[l], wr, wg, wgt, gb, gbt, tabs)

        hm = _mlstm_call(mq, mk, mv, gc, gr, n_ctx)
        sink_rows = jnp.repeat(a_sink[l].reshape(A_KV_HEADS, A_GROUP), A_BLOCK, axis=1)[..., None]
        ya = _attn_call(aq, ak, av, sink_rows.astype(F32), L, not last)
        lgb = jnp.broadcast_to(ret_logit[l].T[:, :, None, None], (R_HEADS, 2, 8, LANES)).astype(F32)
        hr = _ret_call(rq, rk, rv, lgb, n_ctx)

        t_off = 1 if last else 0
        wgates = jnp.concatenate([w_mo, w_rg, w_gm, w_ga, w_gr], axis=1).astype(BF16)
        x1 = _post_call(xc, hb, hm, ya, hr, modsel, wgates, m_norm_w[l].reshape(1, -1),
                        ret_norm_w[l].reshape(1, -1), w_br_m[l].astype(BF16), w_br_a[l].astype(BF16),
                        w_br_r[l].astype(BF16), w_out[l].astype(BF16), t_off)
        xc = _ffn_call(x1, modsel, norm2_w[l].reshape(1, D), ffn_up[l][:, :FFN_DIM].astype(BF16),
                       ffn_up[l][:, FFN_DIM:].astype(BF16), ffn_conv_w[l], ffn_conv_b[l].reshape(1, -1),
                       ffn_down[l].astype(BF16), final_norm_w.reshape(1, D), not last, last)
    return xc
```

```python
import functools

import jax
import jax.numpy as jnp
from jax import lax
from jax.experimental import pallas as pl
from jax.experimental.pallas import tpu as pltpu

F32 = jnp.float32
BF16 = jnp.bfloat16

D_MODEL = 1024
GRID_W = 64
NORM_EPS = 1e-6
ROPE_BASE = 10000.0
NEG_INF = -1e30

M_HEADS = 4
M_HEAD_DIM = 128
M_WIDTH = M_HEADS * M_HEAD_DIM
A_HEADS = 8
A_KV_HEADS = 2
A_GROUP = A_HEADS // A_KV_HEADS
A_HEAD_DIM = 64
A_WIDTH = A_HEADS * A_HEAD_DIM
A_KV_WIDTH = A_KV_HEADS * A_HEAD_DIM
A_WINDOW = 128
A_BLOCK = 128
R_HEADS = 4
R_QK_DIM = 128
R_V_DIM = 256
R_QK_WIDTH = R_HEADS * R_QK_DIM
R_V_WIDTH = R_HEADS * R_V_DIM
FFN_DIM = 2816

IN_SPLITS = (M_WIDTH, M_WIDTH, M_WIDTH, M_WIDTH, 4 * M_HEADS,
             A_WIDTH, A_KV_WIDTH, A_KV_WIDTH,
             R_QK_WIDTH, R_QK_WIDTH, R_V_WIDTH, R_V_WIDTH,
             D_MODEL, D_MODEL, D_MODEL)

TM = 256
HALO = 16
CHUNK = 128
FFN_NC = 256
LANES = 128
VMEM_LIMIT = 56 * 1024 * 1024


def _sigmoid(x):
    return 1.0 / (1.0 + jnp.exp(-x))


def _log_sigmoid(x):
    return jnp.minimum(x, 0.0) - jnp.log1p(jnp.exp(-jnp.abs(x)))


def _dot(a, b):
    return jnp.dot(a, b, preferred_element_type=F32)


def _dot_nt(a, b):
    return lax.dot_general(a, b, (((1,), (1,)), ((), ())), preferred_element_type=F32)


def _dot_tn(a, b):
    return lax.dot_general(a, b, (((0,), (0,)), ((), ())), preferred_element_type=F32)


def _const_spec(shape):
    nd = len(shape)
    return pl.BlockSpec(shape, lambda *_: (0,) * nd, pipeline_mode=pl.Buffered(1))


def _params(sem):
    return pltpu.CompilerParams(dimension_semantics=sem, vmem_limit_bytes=VMEM_LIMIT)


def _mod_kernel(c_ref, w_ref, b_ref, o_ref):
    c = c_ref[...]
    s = (c * _sigmoid(c)).astype(BF16)
    o_ref[0] = _dot(s, w_ref[0].astype(BF16)) + b_ref[0]


def _modulation(cpad, mod_w, mod_b):
    depth, d, n = mod_w.shape
    rows = cpad.shape[0]
    bn = 1024
    return pl.pallas_call(
        _mod_kernel,
        grid=(depth, n // bn),
        in_specs=[pl.BlockSpec((rows, d), lambda l, j: (0, 0)),
                  pl.BlockSpec((1, d, bn), lambda l, j: (l, 0, j)),
                  pl.BlockSpec((1, 1, bn), lambda l, j: (l, 0, j))],
        out_specs=pl.BlockSpec((1, rows, bn), lambda l, j: (l, 0, j)),
        out_shape=jax.ShapeDtypeStruct((depth, rows, n), F32),
        compiler_params=_params(("parallel", "parallel")),
        name="modulation",
    )(cpad, mod_w, mod_b.reshape(depth, 1, n))


def _norm_mod(xv, nw, sc, sh):
    ms = jnp.mean(xv * xv, axis=-1, keepdims=True)
    return (xv * lax.rsqrt(ms + NORM_EPS) * nw) * (1.0 + sc) + sh


def _seg_cumsum(x, axis, reverse):
    n = x.shape[axis]
    idx = lax.broadcasted_iota(jnp.int32, x.shape, axis)
    s = 1
    while s < n:
        if reverse:
            shifted = pltpu.roll(x, n - s, axis)
            x = x + jnp.where(idx < n - s, shifted, 0.0)
        else:
            shifted = pltpu.roll(x, s, axis)
            x = x + jnp.where(idx >= s, shifted, 0.0)
        s *= 2
    return x


def _gate_transform(raw, gate_axis):
    tok_axis = 1 - gate_axis
    k = lax.broadcasted_iota(jnp.int32, raw.shape, gate_axis) % 4
    lf = _log_sigmoid(raw)
    cum_f = _seg_cumsum(jnp.where(k == 1, lf, 0.0), tok_axis, False)
    cum_b = _seg_cumsum(jnp.where(k == 3, lf, 0.0), tok_axis, True)
    bsel = jnp.where(k == 1, cum_f, jnp.where(k == 3, cum_b, 0.0))
    n = raw.shape[gate_axis]
    bnext = pltpu.roll(bsel, n - 1, gate_axis)
    return jnp.where(k % 2 == 1, bsel, raw - bnext)


def _pre_kernel(x_ref, xp_ref, xn_ref, mod_ref, nw_ref, wqk_ref, cw_ref, wr_ref, wg_ref, wgt_ref,
                gb_ref, gbt_ref, ca_ref, sa_ref, cr_ref, sr_ref,
                hb_ref, mq_ref, mk_ref, mv_ref, gc_ref, gr_ref, aq_ref, ak_ref, av_ref,
                rq_ref, rk_ref, rv_ref, hext_ref):
    t = pl.program_id(0)
    nt = pl.num_programs(0)
    sh = mod_ref[0, 0, 0:1, :]
    sc = mod_ref[0, 0, 1:2, :]
    nw = nw_ref[...]
    hb = _norm_mod(x_ref[0], nw, sc, sh).astype(BF16)
    hb_ref[0] = hb

    hp = _norm_mod(xp_ref[0], nw, sc, sh)
    hn = _norm_mod(xn_ref[0], nw, sc, sh)
    hp = jnp.where(t <= 1, 0.0, hp)
    hn = jnp.where((t == 0) | (t == nt - 1), 0.0, hn)
    hext_ref[0:HALO, :] = hp.astype(BF16)
    hext_ref[HALO:HALO + TM, :] = hb
    hext_ref[HALO + TM:, :] = hn.astype(BF16)
    hext = hext_ref[...]
    ext = TM + 2 * HALO

    for j, (dst, scale) in enumerate(((mq_ref, 1.0), (mk_ref, M_HEAD_DIM ** -0.5))):
        p = _dot(hext, wqk_ref[:, j * M_WIDTH:(j + 1) * M_WIDTH])
        cw = cw_ref[:, j * M_WIDTH:(j + 1) * M_WIDTH]
        prev = pltpu.roll(p, 1, 0)[HALO:HALO + TM]
        nxt = pltpu.roll(p, ext - 1, 0)[HALO:HALO + TM]
        conv = prev * cw[0:1] + p[HALO:HALO + TM] * cw[1:2] + nxt * cw[2:3]
        act = conv * _sigmoid(conv) * scale
        for h in range(M_HEADS):
            dst[0, h] = act[:, h * M_HEAD_DIM:(h + 1) * M_HEAD_DIM].astype(BF16)

    off = 0
    p = _dot(hb, wr_ref[:, off:off + M_WIDTH])
    off += M_WIDTH
    for h in range(M_HEADS):
        mv_ref[0, h] = p[:, h * M_HEAD_DIM:(h + 1) * M_HEAD_DIM].astype(BF16)

    lane = lax.broadcasted_iota(jnp.int32, (TM, LANES), 1)
    first16 = (lane % 32) < 16
    ca = ca_ref[...]
    sa = sa_ref[...]

    def rope_a(xs):
        partner = jnp.where(first16, pltpu.roll(xs, LANES - 16, 1), pltpu.roll(xs, 16, 1))
        return xs * ca + partner * sa

    p = _dot(hb, wr_ref[:, off:off + A_WIDTH])
    off += A_WIDTH
    for s in range(A_WIDTH // LANES):
        r = (rope_a(p[:, s * LANES:(s + 1) * LANES]) * A_HEAD_DIM ** -0.5).astype(BF16)
        aq_ref[0, 2 * s] = r[:, 0:A_HEAD_DIM]
        aq_ref[0, 2 * s + 1] = r[:, A_HEAD_DIM:]
    p = _dot(hb, wr_ref[:, off:off + 2 * A_KV_WIDTH])
    off += 2 * A_KV_WIDTH
    r = rope_a(p[:, 0:LANES]).astype(BF16)
    for h in range(A_KV_HEADS):
        ak_ref[0, h] = r[:, h * A_HEAD_DIM:(h + 1) * A_HEAD_DIM]
    pv = p[:, LANES:]
    for h in range(A_KV_HEADS):
        vh = pv if h == 0 else pltpu.roll(pv, A_HEAD_DIM, 1)
        ve = jnp.where(lane < A_HEAD_DIM, vh, 1.0)
        for c in range(TM // A_BLOCK):
            av_ref[0, h, c] = ve[c * A_BLOCK:(c + 1) * A_BLOCK].T.astype(BF16)

    cr = cr_ref[...]
    sr = sr_ref[...]
    for dst, scale in ((rq_ref, 1.0), (rk_ref, R_QK_DIM ** -0.5)):
        p = _dot(hb, wr_ref[:, off:off + R_QK_WIDTH])
        off += R_QK_WIDTH
        for h in range(R_HEADS):
            xs = p[:, h * R_QK_DIM:(h + 1) * R_QK_DIM]
            rot = xs * cr + pltpu.roll(xs, R_QK_DIM // 2, 1) * sr
            dst[0, h] = (rot * scale).astype(BF16)
    p = _dot(hb, wr_ref[:, off:off + R_V_WIDTH])
    for h in range(R_HEADS):
        rv_ref[0, h] = p[:, h * R_V_DIM:(h + 1) * R_V_DIM].astype(BF16)

    gcol = _dot(hb, wg_ref[...]) + gb_ref[...]
    grow = _dot_nt(wgt_ref[...], hb) + gbt_ref[:, 0:1]
    for c in range(TM // CHUNK):
        gc = _gate_transform(gcol[c * CHUNK:(c + 1) * CHUNK], 1)
        gr = _gate_transform(grow[:, c * CHUNK:(c + 1) * CHUNK], 0)
        for hp_ in range(2):
            gc_ref[0, hp_, c * CHUNK:(c + 1) * CHUNK, :] = gc[:, 8 * hp_:8 * hp_ + 8]
            gr_ref[0, hp_, c] = gr[8 * hp_:8 * hp_ + 8]


def _pre_call(xc, modsel, nw, wqk, cw, wr, wg, wgt, gb, gbt, tabs):
    B, T, D = xc.shape
    nt = T // TM
    hb16 = TM // HALO
    nch = T // CHUNK

    def tile(t, b):
        return (b, t, 0)

    def head_tile(t, b):
        return (b, 0, t, 0)

    in_specs = [
        pl.BlockSpec((1, TM, D), tile),
        pl.BlockSpec((1, HALO, D), lambda t, b: (b, jnp.maximum(t * hb16 - 1, 0), 0)),
        pl.BlockSpec((1, HALO, D), lambda t, b: (b, jnp.minimum((t + 1) * hb16, T // HALO - 1), 0)),
        pl.BlockSpec((1, 1, 6, D), lambda t, b: (b, jnp.minimum(t, 1), 0, 0)),
        _const_spec(nw.shape), _const_spec(wqk.shape), _const_spec(cw.shape), _const_spec(wr.shape),
        _const_spec(wg.shape), _const_spec(wgt.shape), _const_spec(gb.shape), _const_spec(gbt.shape),
    ] + [pl.BlockSpec((TM, LANES), lambda t, b: (t, 0)) for _ in range(4)]
    out_shape = [
        jax.ShapeDtypeStruct((B, T, D), BF16),
        jax.ShapeDtypeStruct((B, M_HEADS, T, M_HEAD_DIM), BF16),
        jax.ShapeDtypeStruct((B, M_HEADS, T, M_HEAD_DIM), BF16),
        jax.ShapeDtypeStruct((B, M_HEADS, T, M_HEAD_DIM), BF16),
        jax.ShapeDtypeStruct((B, 2, T, 8), F32),
        jax.ShapeDtypeStruct((B, 2, nch, 8, CHUNK), F32),
        jax.ShapeDtypeStruct((B, A_HEADS, T, A_HEAD_DIM), BF16),
        jax.ShapeDtypeStruct((B, A_KV_HEADS, T, A_HEAD_DIM), BF16),
        jax.ShapeDtypeStruct((B, A_KV_HEADS, T // A_BLOCK, LANES, A_BLOCK), BF16),
        jax.ShapeDtypeStruct((B, R_HEADS, T, R_QK_DIM), BF16),
        jax.ShapeDtypeStruct((B, R_HEADS, T, R_QK_DIM), BF16),
        jax.ShapeDtypeStruct((B, R_HEADS, T, R_V_DIM), BF16),
    ]
    out_specs = [
        pl.BlockSpec((1, TM, D), tile),
        pl.BlockSpec((1, M_HEADS, TM, M_HEAD_DIM), head_tile),
        pl.BlockSpec((1, M_HEADS, TM, M_HEAD_DIM), head_tile),
        pl.BlockSpec((1, M_HEADS, TM, M_HEAD_DIM), head_tile),
        pl.BlockSpec((1, 2, TM, 8), head_tile),
        pl.BlockSpec((1, 2, TM // CHUNK, 8, CHUNK), lambda t, b: (b, 0, t, 0, 0)),
        pl.BlockSpec((1, A_HEADS, TM, A_HEAD_DIM), head_tile),
        pl.BlockSpec((1, A_KV_HEADS, TM, A_HEAD_DIM), head_tile),
        pl.BlockSpec((1, A_KV_HEADS, TM // A_BLOCK, LANES, A_BLOCK), lambda t, b: (b, 0, t, 0, 0)),
        pl.BlockSpec((1, R_HEADS, TM, R_QK_DIM), head_tile),
        pl.BlockSpec((1, R_HEADS, TM, R_QK_DIM), head_tile),
        pl.BlockSpec((1, R_HEADS, TM, R_V_DIM), head_tile),
    ]
    return pl.pallas_call(
        _pre_kernel,
        grid=(nt, B),
        in_specs=in_specs,
        out_specs=out_specs,
        out_shape=out_shape,
        scratch_shapes=[pltpu.VMEM((TM + 2 * HALO, D), BF16)],
        compiler_params=_params(("parallel", "parallel")),
        name="pre_proj",
    )(xc, xc, xc, modsel, nw, wqk, cw, wr, wg, wgt, gb, gbt, *tabs)


def _bwd_chunk(i, n_ctx, n_all):
    return jnp.where(i < n_ctx, n_ctx - 1 - i, n_all - 1 + n_ctx - i)


def _mlstm_kernel(q_ref, k_ref, v_ref, gc_ref, gr_ref, o_ref, st_ref, n_ref, m_ref, *, n_ctx):
    n_all = gr_ref.shape[2]
    o_ref[...] = jnp.zeros(o_ref.shape, F32)
    st_ref[...] = jnp.zeros(st_ref.shape, F32)
    n_ref[...] = jnp.zeros(n_ref.shape, F32)
    m_ref[...] = jnp.zeros(m_ref.shape, F32)
    row = lax.broadcasted_iota(jnp.int32, (CHUNK, CHUNK), 0)
    col = lax.broadcasted_iota(jnp.int32, (CHUNK, CHUNK), 1)
    tri = (row >= col, row <= col)

    def unit(c, hh, d):
        u = 2 * hh + d
        c0 = pl.multiple_of(c * CHUNK, CHUNK)
        q = q_ref[0, hh, pl.ds(c0, CHUNK), :]
        k = k_ref[0, hh, pl.ds(c0, CHUNK), :]
        v = v_ref[0, hh, pl.ds(c0, CHUNK), :]
        gcv = gc_ref[0, 0, pl.ds(c0, CHUNK), :]
        g0 = 4 * hh + 2 * d
        a0c = gcv[:, g0:g0 + 1]
        bc = gcv[:, g0 + 1:g0 + 2]
        a0r = gr_ref[0, 0, c, g0:g0 + 1, :]
        g = bc[CHUNK - 1:CHUNK] if d == 0 else bc[0:1]
        st = st_ref[u]
        nv = n_ref[u]
        m = m_ref[u]

        dlog = jnp.where(tri[d], bc + a0r, NEG_INF)
        mj = jnp.maximum(bc + m, jnp.max(dlog, axis=1, keepdims=True))
        s = _dot_nt(q, k) * jnp.exp(dlog - mj)
        iw = jnp.exp(bc + m - mj)
        num = iw * _dot(q, st.astype(BF16)) + _dot(s.astype(BF16), v)
        den = (iw * jnp.sum(q.astype(F32) * nv, axis=1, keepdims=True)
               + jnp.sum(s, axis=1, keepdims=True))
        inv = 1.0 / jnp.maximum(jnp.abs(den), jnp.exp(-mj))
        o_ref[0, hh, pl.ds(c0, CHUNK), :] += num * inv

        ac = g + a0c
        m_new = jnp.maximum(g + m, jnp.max(ac, axis=0, keepdims=True))
        w = jnp.exp(ac - m_new)
        dec = jnp.exp(g + m - m_new)
        vw = (v.astype(F32) * w).astype(BF16)
        st_ref[u] = dec * st + _dot_tn(k, vw)
        n_ref[u] = dec * nv + jnp.sum(k.astype(F32) * w, axis=0, keepdims=True)
        m_ref[u] = m_new

    def step(i, carry):
        cb = _bwd_chunk(i, n_ctx, n_all)
        for hh in range(2):
            unit(i, hh, 0)
            unit(cb, hh, 1)
        return carry

    lax.fori_loop(0, n_all, step, 0)


def _mlstm_call(mq, mk, mv, gc, gr, n_ctx):
    B, H, T, dh = mq.shape
    nch = T // CHUNK
    qspec = pl.BlockSpec((1, 2, T, dh), lambda b, p: (b, p, 0, 0))
    return pl.pallas_call(
        functools.partial(_mlstm_kernel, n_ctx=n_ctx),
        grid=(B, H // 2),
        in_specs=[qspec, qspec, qspec,
                  pl.BlockSpec((1, 1, T, 8), lambda b, p: (b, p, 0, 0)),
                  pl.BlockSpec((1, 1, nch, 8, CHUNK), lambda b, p: (b, p, 0, 0, 0))],
        out_specs=pl.BlockSpec((1, 2, T, dh), lambda b, p: (b, p, 0, 0)),
        out_shape=jax.ShapeDtypeStruct((B, H, T, dh), F32),
        scratch_shapes=[pltpu.VMEM((4, dh, dh), F32), pltpu.VMEM((4, 1, dh), F32),
                        pltpu.VMEM((4, 1, 1), F32)],
        compiler_params=_params(("parallel", "parallel")),
        name="mlstm",
    )(mq, mk, mv, gc, gr)


def _ret_kernel(q_ref, k_ref, v_ref, lg_ref, o_ref, st_ref, dec_ref, ib_ref, wb_ref, *, n_ctx):
    n_all = q_ref.shape[2] // CHUNK
    dk = q_ref.shape[3]
    dv = v_ref.shape[3]
    o_ref[...] = jnp.zeros(o_ref.shape, F32)
    st_ref[...] = jnp.zeros(st_ref.shape, F32)
    row = lax.broadcasted_iota(jnp.int32, (CHUNK, CHUNK), 0).astype(F32)
    col = lax.broadcasted_iota(jnp.int32, (CHUNK, CHUNK), 1).astype(F32)
    pos = lax.broadcasted_iota(jnp.int32, (CHUNK, 1), 0).astype(F32)
    cds = []
    for hh in range(2):
        for d in range(2):
            u = 2 * hh + d
            lg = _log_sigmoid(lg_ref[hh, d])[0:1, 0:1]
            diff = row - col if d == 0 else col - row
            dec_ref[u] = jnp.where(diff >= 0, jnp.exp(lg * jnp.maximum(diff, 0.0)), 0.0)
            pin = pos if d == 0 else CHUNK - 1.0 - pos
            ib_ref[u] = jnp.broadcast_to(jnp.exp(lg * (pin + 1.0)), (CHUNK, dv))
            wb_ref[u] = jnp.broadcast_to(jnp.exp(lg * (CHUNK - 1.0 - pin)), (CHUNK, dk))
            cds.append(jnp.exp(lg * CHUNK))

    def unit(c, hh, d):
        u = 2 * hh + d
        c0 = pl.multiple_of(c * CHUNK, CHUNK)
        q = q_ref[0, hh, pl.ds(c0, CHUNK), :]
        k = k_ref[0, hh, pl.ds(c0, CHUNK), :]
        v = v_ref[0, hh, pl.ds(c0, CHUNK), :]
        st = st_ref[u]
        s = (_dot_nt(q, k) * dec_ref[u]).astype(BF16)
        o_ref[0, hh, pl.ds(c0, CHUNK), :] += _dot(s, v) + ib_ref[u] * _dot(q, st.astype(BF16))
        kw = (k.astype(F32) * wb_ref[u]).astype(BF16)
        st_ref[u] = cds[u] * st + _dot_tn(kw, v)

    def step(i, carry):
        cb = _bwd_chunk(i, n_ctx, n_all)
        for hh in range(2):
            unit(i, hh, 0)
            unit(cb, hh, 1)
        return carry

    lax.fori_loop(0, n_all, step, 0)


def _ret_call(rq, rk, rv, lgb, n_ctx):
    B, H, T, dk = rq.shape
    dv = rv.shape[-1]
    qspec = pl.BlockSpec((1, 2, T, dk), lambda b, p: (b, p, 0, 0))
    vspec = pl.BlockSpec((1, 2, T, dv), lambda b, p: (b, p, 0, 0))
    return pl.pallas_call(
        functools.partial(_ret_kernel, n_ctx=n_ctx),
        grid=(B, H // 2),
        in_specs=[qspec, qspec, vspec,
                  pl.BlockSpec((2, 2, 8, LANES), lambda b, p: (p, 0, 0, 0))],
        out_specs=vspec,
        out_shape=jax.ShapeDtypeStruct((B, H, T, dv), F32),
        scratch_shapes=[pltpu.VMEM((4, dk, dv), F32), pltpu.VMEM((4, CHUNK, CHUNK), F32),
                        pltpu.VMEM((4, CHUNK, dv), F32), pltpu.VMEM((4, CHUNK, dk), F32)],
        compiler_params=_params(("parallel", "parallel")),
        name="retention",
    )(rq, rk, rv, lgb)


def _attn_kernel(q_ref, k_ref, vt_ref, sink_ref, o_ref, *, n_ctx_tok, need_ctx):
    T = q_ref.shape[2]
    L = n_ctx_tok
    S = T - L
    nb = S // A_BLOCK
    nspan = 3
    span = nspan * A_BLOCK
    ncb = L // A_BLOCK
    cols = A_GROUP * A_BLOCK
    kc = k_ref[0, 0, 0:L, :]
    sink = sink_ref[0]
    qpos = lax.broadcasted_iota(jnp.int32, (1, cols), 1) % A_BLOCK
    kpos = lax.broadcasted_iota(jnp.int32, (span, 1), 0)

    def finish(blk, acc, m):
        den = acc[A_HEAD_DIM:A_HEAD_DIM + 1] + jnp.exp(sink - m)
        ot = (acc[0:A_HEAD_DIM] * (1.0 / den)).astype(BF16)
        for g in range(A_GROUP):
            o_ref[0, blk, g * A_HEAD_DIM:(g + 1) * A_HEAD_DIM, :] = ot[:, g * A_BLOCK:(g + 1) * A_BLOCK]

    def ctx_part(qs, p_ctx):
        acc = _dot(vt_ref[0, 0, 0], p_ctx[0:A_BLOCK])
        for j in range(1, ncb):
            acc = acc + _dot(vt_ref[0, 0, j], p_ctx[j * A_BLOCK:(j + 1) * A_BLOCK])
        return acc

    for cbk in range(ncb):
        if need_ctx:
            qs = q_ref[0, :, cbk * A_BLOCK:(cbk + 1) * A_BLOCK, :].reshape(cols, A_HEAD_DIM)
            s = _dot_nt(kc, qs)
            m = jnp.maximum(jnp.max(s, axis=0, keepdims=True), sink)
            finish(cbk, ctx_part(qs, jnp.exp(s - m).astype(BF16)), m)
        else:
            o_ref[0, cbk] = jnp.zeros(o_ref.shape[2:], BF16)

    def block(n, carry):
        q0 = pl.multiple_of(L + n * A_BLOCK, A_BLOCK)
        kb = jnp.clip(n - 1, 0, nb - nspan)
        k0 = pl.multiple_of(L + kb * A_BLOCK, A_BLOCK)
        qs = q_ref[0, :, pl.ds(q0, A_BLOCK), :].reshape(cols, A_HEAD_DIM)
        s_loc = _dot_nt(k_ref[0, 0, pl.ds(k0, span), :], qs)
        qi = n * A_BLOCK + qpos
        kj = kb * A_BLOCK + kpos
        s_loc = jnp.where(jnp.abs(qi - kj) <= A_WINDOW, s_loc, NEG_INF)
        s_ctx = _dot_nt(kc, qs)
        m = jnp.maximum(jnp.maximum(jnp.max(s_loc, axis=0, keepdims=True),
                                    jnp.max(s_ctx, axis=0, keepdims=True)), sink)
        p_loc = jnp.exp(s_loc - m).astype(BF16)
        acc = ctx_part(qs, jnp.exp(s_ctx - m).astype(BF16))
        for j in range(nspan):
            acc = acc + _dot(vt_ref[0, 0, ncb + kb + j], p_loc[j * A_BLOCK:(j + 1) * A_BLOCK])
        finish(ncb + n, acc, m)
        return carry

    lax.fori_loop(0, nb, block, 0)


def _attn_call(aq, ak, avt, sink_rows, n_ctx_tok, need_ctx):
    B, _, T, hd = aq.shape
    nblk = T // A_BLOCK
    cols = A_GROUP * A_BLOCK
    return pl.pallas_call(
        functools.partial(_attn_kernel, n_ctx_tok=n_ctx_tok, need_ctx=need_ctx),
        grid=(B, A_KV_HEADS),
        in_specs=[pl.BlockSpec((1, A_GROUP, T, hd), lambda b, h: (b, h, 0, 0)),
                  pl.BlockSpec((1, 1, T, hd), lambda b, h: (b, h, 0, 0)),
                  pl.BlockSpec((1, 1, nblk, LANES, A_BLOCK), lambda b, h: (b, h, 0, 0, 0)),
                  pl.BlockSpec((1, 1, cols), lambda b, h: (h, 0, 0))],
        out_specs=pl.BlockSpec((1, nblk, A_GROUP * hd, A_BLOCK), lambda b, h: (b, 0, h, 0)),
        out_shape=jax.ShapeDtypeStruct((B, nblk, A_WIDTH, A_BLOCK), BF16),
        compiler_params=_params(("parallel", "parallel")),
        name="window_attn",
    )(aq, ak, avt, sink_rows)


def _head_norm(y):
    mu = jnp.mean(y, axis=-1, keepdims=True)
    yc = y - mu
    var = jnp.mean(yc * yc, axis=-1, keepdims=True)
    return yc * lax.rsqrt(var + NORM_EPS)


def _post_kernel(x_ref, hb_ref, hm_ref, ya_ref, hr_ref, mod_ref, wg_ref, mnw_ref, rnw_ref,
                 wbm_ref, wba_ref, wbr_ref, wo_ref, o_ref):
    hb = hb_ref[0]
    off = 0
    hm = jnp.concatenate([_head_norm(hm_ref[0, h]) for h in range(M_HEADS)], axis=1) * mnw_ref[...]
    ym = (_sigmoid(_dot(hb, wg_ref[:, off:off + M_WIDTH])) * hm).astype(BF16)
    off += M_WIDTH
    hr = jnp.concatenate([_head_norm(hr_ref[0, h]) for h in range(R_HEADS)], axis=1) * rnw_ref[...]
    rg = _dot(hb, wg_ref[:, off:off + R_V_WIDTH])
    yr = (rg * _sigmoid(rg) * hr).astype(BF16)
    off += R_V_WIDTH
    z = _sigmoid(_dot(hb, wg_ref[:, off:off + D_MODEL])) * _dot(ym, wbm_ref[...])
    off += D_MODEL
    pa = jnp.concatenate([_dot_tn(ya_ref[0, c], wba_ref[...]) for c in range(TM // A_BLOCK)], axis=0)
    z = z + _sigmoid(_dot(hb, wg_ref[:, off:off + D_MODEL])) * pa
    off += D_MODEL
    z = z + _sigmoid(_dot(hb, wg_ref[:, off:off + D_MODEL])) * _dot(yr, wbr_ref[...])
    y = _dot(z.astype(BF16), wo_ref[...])
    o_ref[0] = x_ref[0] + mod_ref[0, 0, 2:3, :] * y


def _post_call(xc, hb, hm, ya, hr, modsel, wg, mnw, rnw, wbm, wba, wbr, wo, t_off):
    B, T, D = xc.shape
    nt = T // TM - t_off

    def tile(t, b):
        return (b, t + t_off, 0)

    def head_tile(t, b):
        return (b, 0, t + t_off, 0)

    return pl.pallas_call(
        _post_kernel,
        grid=(nt, B),
        in_specs=[pl.BlockSpec((1, TM, D), tile),
                  pl.BlockSpec((1, TM, D), tile),
                  pl.BlockSpec((1, M_HEADS, TM, M_HEAD_DIM), head_tile),
                  pl.BlockSpec((1, TM // A_BLOCK, A_WIDTH, A_BLOCK), lambda t, b: (b, t + t_off, 0, 0)),
                  pl.BlockSpec((1, R_HEADS, TM, R_V_DIM), head_tile),
                  pl.BlockSpec((1, 1, 6, D), lambda t, b: (b, jnp.minimum(t + t_off, 1), 0, 0)),
                  _const_spec(wg.shape), _const_spec(mnw.shape), _const_spec(rnw.shape),
                  _const_spec(wbm.shape), _const_spec(wba.shape), _const_spec(wbr.shape),
                  _const_spec(wo.shape)],
        out_specs=pl.BlockSpec((1, TM, D), lambda t, b: (b, t, 0)),
        out_shape=jax.ShapeDtypeStruct((B, nt * TM, D), F32),
        compiler_params=_params(("parallel", "parallel")),
        name="post_merge",
    )(xc, hb, hm, ya, hr, modsel, wg, mnw, rnw, wbm, wba, wbr, wo)


def _ffn_kernel(x_ref, xp_ref, xn_ref, mod_ref, nw_ref, wa_ref, wb_ref, cw_ref, cb_ref, wd_ref, fw_ref,
                o_ref, hext_ref, u_ref, *, has_ctx, final):
    t = pl.program_id(0)
    nt = pl.num_programs(0)
    first_lat = 1 if has_ctx else 0
    sh = mod_ref[0, 0, 3:4, :]
    sc = mod_ref[0, 0, 4:5, :]
    g2 = mod_ref[0, 0, 5:6, :]
    nw = nw_ref[...]
    x = x_ref[0]
    hp = _norm_mod(xp_ref[0], nw, sc, sh)
    hn = _norm_mod(xn_ref[0], nw, sc, sh)
    hp = jnp.where(t <= first_lat, 0.0, hp)
    hn = jnp.where((t == first_lat - 1) | (t == nt - 1), 0.0, hn)
    hext_ref[0:HALO, :] = hp.astype(BF16)
    hext_ref[HALO:HALO + TM, :] = _norm_mod(x, nw, sc, sh).astype(BF16)
    hext_ref[HALO + TM:, :] = hn.astype(BF16)
    ext = TM + 2 * HALO
    for j in range(FFN_DIM // FFN_NC):
        cs = slice(j * FFN_NC, (j + 1) * FFN_NC)
        a = _dot(hext_ref[...], wa_ref[:, cs])
        b = _dot(hext_ref[HALO:HALO + TM, :], wb_ref[:, cs])
        cw = cw_ref[:, cs]
        prev = pltpu.roll(a, 1, 0)[HALO:HALO + TM]
        nxt = pltpu.roll(a, ext - 1, 0)[HALO:HALO + TM]
        conv = prev * cw[0:1] + a[HALO:HALO + TM] * cw[1:2] + nxt * cw[2:3] + cb_ref[:, cs]
        u_ref[:, cs] = (conv * _sigmoid(conv) * b).astype(BF16)
    y = x + g2 * _dot(u_ref[...], wd_ref[...])
    if final:
        ms = jnp.mean(y * y, axis=-1, keepdims=True)
        y = y * lax.rsqrt(ms + NORM_EPS) * fw_ref[...]
    o_ref[0] = y


def _ffn_call(xs, modsel, nw, wa, wb, cw, cb, wd, fw, has_ctx, final):
    B, T, D = xs.shape
    nt = T // TM
    hb16 = TM // HALO
    first_lat = 1 if has_ctx else 0
    tile = pl.BlockSpec((1, TM, D), lambda t, b: (b, t, 0))
    return pl.pallas_call(
        functools.partial(_ffn_kernel, has_ctx=has_ctx, final=final),
        grid=(nt, B),
        in_specs=[tile,
                  pl.BlockSpec((1, HALO, D), lambda t, b: (b, jnp.maximum(t * hb16 - 1, 0), 0)),
                  pl.BlockSpec((1, HALO, D), lambda t, b: (b, jnp.minimum((t + 1) * hb16, T // HALO - 1), 0)),
                  pl.BlockSpec((1, 1, 6, D), lambda t, b: (b, jnp.minimum(t + 1 - first_lat, 1), 0, 0)),
                  _const_spec(nw.shape), _const_spec(wa.shape), _const_spec(wb.shape),
                  _const_spec(cw.shape), _const_spec(cb.shape), _const_spec(wd.shape),
                  _const_spec(fw.shape)],
        out_specs=tile,
        out_shape=jax.ShapeDtypeStruct((B, T, D), F32),
        scratch_shapes=[pltpu.VMEM((TM + 2 * HALO, D), BF16), pltpu.VMEM((TM, FFN_DIM), BF16)],
        compiler_params=_params(("parallel", "parallel")),
        name="conv_ffn",
    )(xs, xs, xs, modsel, nw, wa, wb, cw, cb, wd, fw)


def _split_cols(w):
    out = []
    acc = 0
    for s in IN_SPLITS:
        out.append(w[:, acc:acc + s])
        acc += s
    return out


def _rope_tables(L, S):
    T = L + S
    j = jnp.arange(LANES)
    jj = j % 32
    inv_a = ROPE_BASE ** (-(jj % 16).astype(F32) / 16.0)
    t = jnp.arange(S)
    pos = jnp.where(((j % A_HEAD_DIM) // 32 == 0)[None, :], (t // GRID_W)[:, None], (t % GRID_W)[:, None])
    ang = pos.astype(F32) * inv_a[None, :]
    sign = jnp.where(jj < 16, -1.0, 1.0)[None, :]
    ca = jnp.concatenate([jnp.ones((L, LANES), F32), jnp.cos(ang)], axis=0)
    sa = jnp.concatenate([jnp.zeros((L, LANES), F32), jnp.sin(ang) * sign], axis=0)
    half = R_QK_DIM // 2
    inv_r = ROPE_BASE ** (-(j % half).astype(F32) / half)
    angr = jnp.arange(T).astype(F32)[:, None] * inv_r[None, :]
    signr = jnp.where(j < half, -1.0, 1.0)[None, :]
    return ca, sa, jnp.cos(angr), jnp.sin(angr) * signr


def kernel(x, c, ctx, c_ctx, mod_w, mod_b, norm1_w, norm2_w, w_in, m_gate_bias, m_conv_w, m_norm_w,
           a_sink, ret_logit, ret_norm_w, w_br_m, w_br_a, w_br_r, w_out, ffn_up, ffn_conv_w, ffn_conv_b,
           ffn_down, final_norm_w):
    B, S, D = x.shape
    L = ctx.shape[1]
    depth = mod_w.shape[0]
    assert D == D_MODEL and L == TM and S % TM == 0 and S >= 3 * A_BLOCK
    T = L + S
    n_ctx = L // CHUNK

    rows = -(-(B + 1) // 8) * 8
    cpad = jnp.zeros((rows, D), F32).at[:B].set(c).at[B].set(c_ctx)
    mods = _modulation(cpad, mod_w, mod_b)
    tabs = _rope_tables(L, S)
    xc = jnp.concatenate([ctx, x], axis=1)

    gperm = jnp.arange(4 * M_HEADS).reshape(4, M_HEADS).T.reshape(-1)

    for l in range(depth):
        last = l == depth - 1
        lat = mods[l, :B].reshape(B, 6, D)
        cm = jnp.broadcast_to(mods[l, B].reshape(1, 6, D), (B, 6, D))
        modsel = jnp.stack([cm, lat], axis=1)

        (w_mq, w_mk, w_mv, w_mo, w_mg, w_aq, w_ak, w_av,
         w_rq, w_rk, w_rv, w_rg, w_gm, w_ga, w_gr) = _split_cols(w_in[l])
        wqk = jnp.concatenate([w_mq, w_mk], axis=1).astype(BF16)
        wr = jnp.concatenate([w_mv, w_aq, w_ak, w_av, w_rq, w_rk, w_rv], axis=1).astype(BF16)
        wg_cols = w_mg[:, gperm]
        wg = jnp.pad(wg_cols, ((0, 0), (0, LANES - 4 * M_HEADS))).astype(BF16)
        wgt = wg_cols.T.astype(BF16)
        gbias = m_gate_bias[l][gperm]
        gb = jnp.pad(gbias, (0, LANES - 4 * M_HEADS)).reshape(1, LANES)
        gbt = jnp.broadcast_to(gbias.reshape(-1, 1), (4 * M_HEADS, LANES))

        (hb, mq, mk, mv, gc, gr, aq, ak, av, rq, rk, rv) = _pre_call(
            xc, modsel, norm1_w[l].reshape(1, D), wqk, m_conv_w[l], wr, wg, wgt, gb, gbt, tabs)

        hm = _mlstm_call(mq, mk, mv, gc, gr, n_ctx)
        sink_rows = jnp.repeat(a_sink[l].reshape(A_KV_HEADS, A_GROUP), A_BLOCK, axis=1)[:, None, :]
        ya = _attn_call(aq, ak, av, sink_rows.astype(F32), L, not last)
        lgb = jnp.broadcast_to(ret_logit[l].T[:, :, None, None], (R_HEADS, 2, 8, LANES)).astype(F32)
        hr = _ret_call(rq, rk, rv, lgb, n_ctx)

        t_off = 1 if last else 0
        wgates = jnp.concatenate([w_mo, w_rg, w_gm, w_ga, w_gr], axis=1).astype(BF16)
        x1 = _post_call(xc, hb, hm, ya, hr, modsel, wgates, m_norm_w[l].reshape(1, -1),
                        ret_norm_w[l].reshape(1, -1), w_br_m[l].astype(BF16), w_br_a[l].astype(BF16),
                        w_br_r[l].astype(BF16), w_out[l].astype(BF16), t_off)
        xc = _ffn_call(x1, modsel, norm2_w[l].reshape(1, D), ffn_up[l][:, :FFN_DIM].astype(BF16),
                       ffn_up[l][:, FFN_DIM:].astype(BF16), ffn_conv_w[l], ffn_conv_b[l].reshape(1, -1),
                       ffn_down[l].astype(BF16), final_norm_w.reshape(1, D), not last, last)
    return xc
```

```python
import functools

import jax
import jax.numpy as jnp
from jax import lax
from jax.experimental import pallas as pl
from jax.experimental.pallas import tpu as pltpu

F32 = jnp.float32
BF16 = jnp.bfloat16

D_MODEL = 1024
GRID_W = 64
NORM_EPS = 1e-6
ROPE_BASE = 10000.0
NEG_INF = -1e30

M_HEADS = 4
M_HEAD_DIM = 128
M_WIDTH = M_HEADS * M_HEAD_DIM
A_HEADS = 8
A_KV_HEADS = 2
A_GROUP = A_HEADS // A_KV_HEADS
A_HEAD_DIM = 64
A_WIDTH = A_HEADS * A_HEAD_DIM
A_KV_WIDTH = A_KV_HEADS * A_HEAD_DIM
A_WINDOW = 128
A_BLOCK = 128
R_HEADS = 4
R_QK_DIM = 128
R_V_DIM = 256
R_QK_WIDTH = R_HEADS * R_QK_DIM
R_V_WIDTH = R_HEADS * R_V_DIM
FFN_DIM = 2816

IN_SPLITS = (M_WIDTH, M_WIDTH, M_WIDTH, M_WIDTH, 4 * M_HEADS,
             A_WIDTH, A_KV_WIDTH, A_KV_WIDTH,
             R_QK_WIDTH, R_QK_WIDTH, R_V_WIDTH, R_V_WIDTH,
             D_MODEL, D_MODEL, D_MODEL)

TM = 256
HALO = 16
CHUNK = 128
FFN_NC = 256
LANES = 128
VMEM_LIMIT = 56 * 1024 * 1024


def _sigmoid(x):
    return 1.0 / (1.0 + jnp.exp(-x))


def _log_sigmoid(x):
    return jnp.minimum(x, 0.0) - jnp.log1p(jnp.exp(-jnp.abs(x)))


def _dot(a, b):
    return jnp.dot(a, b, preferred_element_type=F32)


def _dot_nt(a, b):
    return lax.dot_general(a, b, (((1,), (1,)), ((), ())), preferred_element_type=F32)


def _dot_tn(a, b):
    return lax.dot_general(a, b, (((0,), (0,)), ((), ())), preferred_element_type=F32)


def _const_spec(shape):
    nd = len(shape)
    return pl.BlockSpec(shape, lambda *_: (0,) * nd, pipeline_mode=pl.Buffered(1))


def _params(sem):
    return pltpu.CompilerParams(dimension_semantics=sem, vmem_limit_bytes=VMEM_LIMIT)


def _mod_kernel(c_ref, w_ref, b_ref, o_ref):
    c = c_ref[...]
    s = (c * _sigmoid(c)).astype(BF16)
    o_ref[0] = _dot(s, w_ref[0].astype(BF16)) + b_ref[0]


def _modulation(cpad, mod_w, mod_b):
    depth, d, n = mod_w.shape
    rows = cpad.shape[0]
    bn = 1024
    return pl.pallas_call(
        _mod_kernel,
        grid=(depth, n // bn),
        in_specs=[pl.BlockSpec((rows, d), lambda l, j: (0, 0)),
                  pl.BlockSpec((1, d, bn), lambda l, j: (l, 0, j)),
                  pl.BlockSpec((1, 1, bn), lambda l, j: (l, 0, j))],
        out_specs=pl.BlockSpec((1, rows, bn), lambda l, j: (l, 0, j)),
        out_shape=jax.ShapeDtypeStruct((depth, rows, n), F32),
        compiler_params=_params(("parallel", "parallel")),
        name="modulation",
    )(cpad, mod_w, mod_b.reshape(depth, 1, n))


def _norm_mod(xv, nw, sc, sh):
    ms = jnp.mean(xv * xv, axis=-1, keepdims=True)
    return (xv * lax.rsqrt(ms + NORM_EPS) * nw) * (1.0 + sc) + sh


def _seg_cumsum(x, axis, reverse):
    n = x.shape[axis]
    idx = lax.broadcasted_iota(jnp.int32, x.shape, axis)
    s = 1
    while s < n:
        if reverse:
            shifted = pltpu.roll(x, n - s, axis)
            x = x + jnp.where(idx < n - s, shifted, 0.0)
        else:
            shifted = pltpu.roll(x, s, axis)
            x = x + jnp.where(idx >= s, shifted, 0.0)
        s *= 2
    return x


def _gate_transform(raw, gate_axis):
    tok_axis = 1 - gate_axis
    k = lax.broadcasted_iota(jnp.int32, raw.shape, gate_axis) % 4
    lf = _log_sigmoid(raw)
    cum_f = _seg_cumsum(jnp.where(k == 1, lf, 0.0), tok_axis, False)
    cum_b = _seg_cumsum(jnp.where(k == 3, lf, 0.0), tok_axis, True)
    bsel = jnp.where(k == 1, cum_f, jnp.where(k == 3, cum_b, 0.0))
    n = raw.shape[gate_axis]
    bnext = pltpu.roll(bsel, n - 1, gate_axis)
    nt = raw.shape[tok_axis]
    tot_f = lax.slice_in_dim(cum_f, nt - 1, nt, axis=tok_axis)
    tot_b = lax.slice_in_dim(cum_b, 0, 1, axis=tok_axis)
    total = jnp.broadcast_to(tot_f + tot_b, raw.shape)
    return jnp.where(k % 2 == 1, bsel, raw - bnext), total


def _pre_kernel(x_ref, xp_ref, xn_ref, mod_ref, nw_ref, wqk_ref, cw_ref, wr_ref, wg_ref, wgt_ref,
                gb_ref, gbt_ref, ca_ref, sa_ref, cr_ref, sr_ref,
                hb_ref, mq_ref, mk_ref, mv_ref, gc_ref, gr_ref, aq_ref, ak_ref, av_ref,
                rq_ref, rk_ref, rv_ref, hext_ref):
    t = pl.program_id(0)
    nt = pl.num_programs(0)
    sh = mod_ref[0, 0, 0:1, :]
    sc = mod_ref[0, 0, 1:2, :]
    nw = nw_ref[...]
    hb = _norm_mod(x_ref[0], nw, sc, sh).astype(BF16)
    hb_ref[0] = hb

    hp = _norm_mod(xp_ref[0], nw, sc, sh)
    hn = _norm_mod(xn_ref[0], nw, sc, sh)
    hp = jnp.where(t <= 1, 0.0, hp)
    hn = jnp.where((t == 0) | (t == nt - 1), 0.0, hn)
    hext_ref[0:HALO, :] = hp.astype(BF16)
    hext_ref[HALO:HALO + TM, :] = hb
    hext_ref[HALO + TM:, :] = hn.astype(BF16)
    hext = hext_ref[...]
    ext = TM + 2 * HALO

    for j, (dst, scale) in enumerate(((mq_ref, 1.0), (mk_ref, M_HEAD_DIM ** -0.5))):
        p = _dot(hext, wqk_ref[:, j * M_WIDTH:(j + 1) * M_WIDTH])
        cw = cw_ref[:, j * M_WIDTH:(j + 1) * M_WIDTH]
        prev = pltpu.roll(p, 1, 0)[HALO:HALO + TM]
        nxt = pltpu.roll(p, ext - 1, 0)[HALO:HALO + TM]
        conv = prev * cw[0:1] + p[HALO:HALO + TM] * cw[1:2] + nxt * cw[2:3]
        act = conv * _sigmoid(conv) * scale
        for h in range(M_HEADS):
            dst[0, h] = act[:, h * M_HEAD_DIM:(h + 1) * M_HEAD_DIM].astype(BF16)

    off = 0
    p = _dot(hb, wr_ref[:, off:off + M_WIDTH])
    off += M_WIDTH
    for h in range(M_HEADS):
        for c in range(TM // CHUNK):
            blk = p[c * CHUNK:(c + 1) * CHUNK, h * M_HEAD_DIM:(h + 1) * M_HEAD_DIM]
            mv_ref[0, h, c] = blk.T.astype(BF16)

    lane = lax.broadcasted_iota(jnp.int32, (TM, LANES), 1)
    first16 = (lane % 32) < 16
    ca = ca_ref[...]
    sa = sa_ref[...]

    def rope_a(xs):
        partner = jnp.where(first16, pltpu.roll(xs, LANES - 16, 1), pltpu.roll(xs, 16, 1))
        return xs * ca + partner * sa

    p = _dot(hb, wr_ref[:, off:off + A_WIDTH])
    off += A_WIDTH
    for s in range(A_WIDTH // LANES):
        r = (rope_a(p[:, s * LANES:(s + 1) * LANES]) * A_HEAD_DIM ** -0.5).astype(BF16)
        aq_ref[0, 2 * s] = r[:, 0:A_HEAD_DIM]
        aq_ref[0, 2 * s + 1] = r[:, A_HEAD_DIM:]
    p = _dot(hb, wr_ref[:, off:off + 2 * A_KV_WIDTH])
    off += 2 * A_KV_WIDTH
    r = rope_a(p[:, 0:LANES]).astype(BF16)
    for h in range(A_KV_HEADS):
        ak_ref[0, h] = r[:, h * A_HEAD_DIM:(h + 1) * A_HEAD_DIM]
    pv = p[:, LANES:]
    for h in range(A_KV_HEADS):
        vh = pv if h == 0 else pltpu.roll(pv, A_HEAD_DIM, 1)
        ve = jnp.where(lane < A_HEAD_DIM, vh, 1.0)
        for c in range(TM // A_BLOCK):
            av_ref[0, h, c] = ve[c * A_BLOCK:(c + 1) * A_BLOCK].T.astype(BF16)

    cr = cr_ref[...]
    sr = sr_ref[...]
    for dst, scale in ((rq_ref, 1.0), (rk_ref, R_QK_DIM ** -0.5)):
        p = _dot(hb, wr_ref[:, off:off + R_QK_WIDTH])
        off += R_QK_WIDTH
        for h in range(R_HEADS):
            xs = p[:, h * R_QK_DIM:(h + 1) * R_QK_DIM]
            rot = xs * cr + pltpu.roll(xs, R_QK_DIM // 2, 1) * sr
            dst[0, h] = (rot * scale).astype(BF16)
    p = _dot(hb, wr_ref[:, off:off + R_V_WIDTH])
    for h in range(R_HEADS):
        rv_ref[0, h] = p[:, h * R_V_DIM:(h + 1) * R_V_DIM].astype(BF16)

    gcol = _dot(hb, wg_ref[...]) + gb_ref[...]
    grow = _dot_nt(wgt_ref[...], hb) + gbt_ref[:, 0:1]
    for c in range(TM // CHUNK):
        gc, _ = _gate_transform(gcol[c * CHUNK:(c + 1) * CHUNK], 1)
        gr, tot = _gate_transform(grow[:, c * CHUNK:(c + 1) * CHUNK], 0)
        aend = gr + pltpu.roll(tot, 4 * M_HEADS - 1, 0)
        amax = jnp.broadcast_to(jnp.max(aend, axis=1, keepdims=True), aend.shape)
        for hp_ in range(2):
            gc_ref[0, hp_, c * CHUNK:(c + 1) * CHUNK, :] = gc[:, 8 * hp_:8 * hp_ + 8]
            for hh in range(2):
                r0 = 8 * hp_ + 4 * hh
                o0 = 8 * hh
                gr_ref[0, hp_, c, o0:o0 + 4, :] = jnp.where(
                    lax.broadcasted_iota(jnp.int32, (4, CHUNK), 0) % 2 == 1, gr[r0:r0 + 4], aend[r0:r0 + 4])
                for d in range(2):
                    gr_ref[0, hp_, c, o0 + 4 + d:o0 + 5 + d, :] = tot[r0 + 2 * d + 1:r0 + 2 * d + 2]
                    gr_ref[0, hp_, c, o0 + 6 + d:o0 + 7 + d, :] = amax[r0 + 2 * d:r0 + 2 * d + 1]


def _pre_call(xc, modsel, nw, wqk, cw, wr, wg, wgt, gb, gbt, tabs):
    B, T, D = xc.shape
    nt = T // TM
    hb16 = TM // HALO
    nch = T // CHUNK

    def tile(t, b):
        return (b, t, 0)

    def head_tile(t, b):
        return (b, 0, t, 0)

    in_specs = [
        pl.BlockSpec((1, TM, D), tile),
        pl.BlockSpec((1, HALO, D), lambda t, b: (b, jnp.maximum(t * hb16 - 1, 0), 0)),
        pl.BlockSpec((1, HALO, D), lambda t, b: (b, jnp.minimum((t + 1) * hb16, T // HALO - 1), 0)),
        pl.BlockSpec((1, 1, 6, D), lambda t, b: (b, jnp.minimum(t, 1), 0, 0)),
        _const_spec(nw.shape), _const_spec(wqk.shape), _const_spec(cw.shape), _const_spec(wr.shape),
        _const_spec(wg.shape), _const_spec(wgt.shape), _const_spec(gb.shape), _const_spec(gbt.shape),
    ] + [pl.BlockSpec((TM, LANES), lambda t, b: (t, 0)) for _ in range(4)]
    out_shape = [
        jax.ShapeDtypeStruct((B, T, D), BF16),
        jax.ShapeDtypeStruct((B, M_HEADS, T, M_HEAD_DIM), BF16),
        jax.ShapeDtypeStruct((B, M_HEADS, T, M_HEAD_DIM), BF16),
        jax.ShapeDtypeStruct((B, M_HEADS, nch, M_HEAD_DIM, CHUNK), BF16),
        jax.ShapeDtypeStruct((B, 2, T, 8), F32),
        jax.ShapeDtypeStruct((B, 2, nch, 16, CHUNK), F32),
        jax.ShapeDtypeStruct((B, A_HEADS, T, A_HEAD_DIM), BF16),
        jax.ShapeDtypeStruct((B, A_KV_HEADS, T, A_HEAD_DIM), BF16),
        jax.ShapeDtypeStruct((B, A_KV_HEADS, T // A_BLOCK, LANES, A_BLOCK), BF16),
        jax.ShapeDtypeStruct((B, R_HEADS, T, R_QK_DIM), BF16),
        jax.ShapeDtypeStruct((B, R_HEADS, T, R_QK_DIM), BF16),
        jax.ShapeDtypeStruct((B, R_HEADS, T, R_V_DIM), BF16),
    ]
    out_specs = [
        pl.BlockSpec((1, TM, D), tile),
        pl.BlockSpec((1, M_HEADS, TM, M_HEAD_DIM), head_tile),
        pl.BlockSpec((1, M_HEADS, TM, M_HEAD_DIM), head_tile),
        pl.BlockSpec((1, M_HEADS, TM // CHUNK, M_HEAD_DIM, CHUNK), lambda t, b: (b, 0, t, 0, 0)),
        pl.BlockSpec((1, 2, TM, 8), head_tile),
        pl.BlockSpec((1, 2, TM // CHUNK, 16, CHUNK), lambda t, b: (b, 0, t, 0, 0)),
        pl.BlockSpec((1, A_HEADS, TM, A_HEAD_DIM), head_tile),
        pl.BlockSpec((1, A_KV_HEADS, TM, A_HEAD_DIM), head_tile),
        pl.BlockSpec((1, A_KV_HEADS, TM // A_BLOCK, LANES, A_BLOCK), lambda t, b: (b, 0, t, 0, 0)),
        pl.BlockSpec((1, R_HEADS, TM, R_QK_DIM), head_tile),
        pl.BlockSpec((1, R_HEADS, TM, R_QK_DIM), head_tile),
        pl.BlockSpec((1, R_HEADS, TM, R_V_DIM), head_tile),
    ]
    return pl.pallas_call(
        _pre_kernel,
        grid=(nt, B),
        in_specs=in_specs,
        out_specs=out_specs,
        out_shape=out_shape,
        scratch_shapes=[pltpu.VMEM((TM + 2 * HALO, D), BF16)],
        compiler_params=_params(("parallel", "parallel")),
        name="pre_proj",
    )(xc, xc, xc, modsel, nw, wqk, cw, wr, wg, wgt, gb, gbt, *tabs)


def _bwd_chunk(i, n_ctx, n_all):
    return jnp.where(i < n_ctx, n_ctx - 1 - i, n_all - 1 + n_ctx - i)


def _mlstm_kernel(q_ref, k_ref, vt_ref, gc_ref, gr_ref, o_ref, c_ref, n_ref, m_ref,
                  cs_ref, ns_ref, ms_ref, *, n_ctx):
    n_all = gr_ref.shape[2]
    dh = q_ref.shape[3]
    c_ref[...] = jnp.zeros(c_ref.shape, F32)
    n_ref[...] = jnp.zeros(n_ref.shape, F32)
    m_ref[...] = jnp.zeros(m_ref.shape, F32)

    def scan_unit(c, hh, d):
        u = 2 * hh + d
        c0 = pl.multiple_of(c * CHUNK, CHUNK)
        k = k_ref[0, hh, pl.ds(c0, CHUNK), :]
        vt = vt_ref[0, hh, c]
        rows = gr_ref[0, 0, c, 8 * hh:8 * hh + 8, :]
        aend = rows[2 * d:2 * d + 1]
        tot = rows[4 + d:5 + d]
        amax = rows[6 + d:7 + d]
        cst = c_ref[u]
        nv = n_ref[u]
        m = m_ref[u]
        cs_ref[u, c] = cst.astype(BF16)
        ns_ref[u, c] = jnp.broadcast_to(nv, (16, dh)).astype(BF16)
        ms_ref[u, c] = jnp.broadcast_to(m, (8, CHUNK))
        m_new = jnp.maximum(tot + m, amax)
        w = jnp.exp(aend - m_new)
        dec = jnp.exp(tot + m - m_new)
        vw = (vt.astype(F32) * w).astype(BF16)
        c_ref[u] = dec * cst + _dot(vw, k)
        wk = _dot(jnp.broadcast_to(w, (16, CHUNK)).astype(BF16), k)
        n_ref[u] = dec * nv + wk[0:1]
        m_ref[u] = m_new

    def scan_step(i, carry):
        cb = _bwd_chunk(i, n_ctx, n_all)
        for hh in range(2):
            scan_unit(i, hh, 0)
            scan_unit(cb, hh, 1)
        return carry

    lax.fori_loop(0, n_all, scan_step, 0)

    row = lax.broadcasted_iota(jnp.int32, (CHUNK, CHUNK), 0)
    col = lax.broadcasted_iota(jnp.int32, (CHUNK, CHUNK), 1)
    tri = (row <= col, row >= col)

    def out_step(c, carry):
        c0 = pl.multiple_of(c * CHUNK, CHUNK)
        gcv = gc_ref[0, 0, pl.ds(c0, CHUNK), :]
        for hh in range(2):
            q = q_ref[0, hh, pl.ds(c0, CHUNK), :]
            k = k_ref[0, hh, pl.ds(c0, CHUNK), :]
            vt = vt_ref[0, hh, c]
            rows = gr_ref[0, 0, c, 8 * hh:8 * hh + 8, :]
            st = _dot_nt(k, q)
            acc = None
            for d in range(2):
                u = 2 * hh + d
                a0c = gcv[:, 4 * hh + 2 * d:4 * hh + 2 * d + 1]
                b = rows[2 * d + 1:2 * d + 2]
                m = ms_ref[u, c, 0:1, :]
                dlog = jnp.where(tri[d], a0c + b, NEG_INF)
                mj = jnp.maximum(b + m, jnp.max(dlog, axis=0, keepdims=True))
                sd = st * jnp.exp(dlog - mj)
                iw = jnp.exp(b + m - mj)
                nq = _dot_nt(ns_ref[u, c], q)[0:1]
                den = iw * nq + jnp.sum(sd, axis=0, keepdims=True)
                inv = 1.0 / jnp.maximum(jnp.abs(den), jnp.exp(-mj))
                ht = (_dot(vt, sd.astype(BF16)) + iw * _dot_nt(cs_ref[u, c], q)) * inv
                acc = ht if acc is None else acc + ht
            o_ref[0, hh, pl.ds(c0, CHUNK), :] = acc.T
        return carry

    lax.fori_loop(0, n_all, out_step, 0)


def _mlstm_call(mq, mk, mvt, gc, gr, n_ctx):
    B, H, T, dh = mq.shape
    nch = T // CHUNK
    qspec = pl.BlockSpec((1, 2, T, dh), lambda b, p: (b, p, 0, 0))
    return pl.pallas_call(
        functools.partial(_mlstm_kernel, n_ctx=n_ctx),
        grid=(B, H // 2),
        in_specs=[qspec, qspec,
                  pl.BlockSpec((1, 2, nch, dh, CHUNK), lambda b, p: (b, p, 0, 0, 0)),
                  pl.BlockSpec((1, 1, T, 8), lambda b, p: (b, p, 0, 0)),
                  pl.BlockSpec((1, 1, nch, 16, CHUNK), lambda b, p: (b, p, 0, 0, 0))],
        out_specs=pl.BlockSpec((1, 2, T, dh), lambda b, p: (b, p, 0, 0)),
        out_shape=jax.ShapeDtypeStruct((B, H, T, dh), F32),
        scratch_shapes=[pltpu.VMEM((4, dh, dh), F32), pltpu.VMEM((4, 1, dh), F32),
                        pltpu.VMEM((4, 1, CHUNK), F32),
                        pltpu.VMEM((4, nch, dh, dh), BF16), pltpu.VMEM((4, nch, 16, dh), BF16),
                        pltpu.VMEM((4, nch, 8, CHUNK), F32)],
        compiler_params=_params(("parallel", "parallel")),
        name="mlstm",
    )(mq, mk, mvt, gc, gr)


def _ret_kernel(q_ref, k_ref, v_ref, lg_ref, o_ref, st_ref, dec_ref, ib_ref, wb_ref, *, n_ctx):
    n_all = q_ref.shape[2] // CHUNK
    dk = q_ref.shape[3]
    dv = v_ref.shape[3]
    o_ref[...] = jnp.zeros(o_ref.shape, F32)
    st_ref[...] = jnp.zeros(st_ref.shape, F32)
    row = lax.broadcasted_iota(jnp.int32, (CHUNK, CHUNK), 0).astype(F32)
    col = lax.broadcasted_iota(jnp.int32, (CHUNK, CHUNK), 1).astype(F32)
    pos = lax.broadcasted_iota(jnp.int32, (CHUNK, 1), 0).astype(F32)
    cds = []
    for hh in range(2):
        for d in range(2):
            u = 2 * hh + d
            lg = _log_sigmoid(lg_ref[hh, d])[0:1, 0:1]
            diff = row - col if d == 0 else col - row
            dec_ref[u] = jnp.where(diff >= 0, jnp.exp(lg * jnp.maximum(diff, 0.0)), 0.0)
            pin = pos if d == 0 else CHUNK - 1.0 - pos
            ib_ref[u] = jnp.broadcast_to(jnp.exp(lg * (pin + 1.0)), (CHUNK, dv))
            wb_ref[u] = jnp.broadcast_to(jnp.exp(lg * (CHUNK - 1.0 - pin)), (CHUNK, dk))
            cds.append(jnp.exp(lg * CHUNK))

    def unit(c, hh, d):
        u = 2 * hh + d
        c0 = pl.multiple_of(c * CHUNK, CHUNK)
        q = q_ref[0, hh, pl.ds(c0, CHUNK), :]
        k = k_ref[0, hh, pl.ds(c0, CHUNK), :]
        v = v_ref[0, hh, pl.ds(c0, CHUNK), :]
        st = st_ref[u]
        s = (_dot_nt(q, k) * dec_ref[u]).astype(BF16)
        o_ref[0, hh, pl.ds(c0, CHUNK), :] += _dot(s, v) + ib_ref[u] * _dot(q, st.astype(BF16))
        kw = (k.astype(F32) * wb_ref[u]).astype(BF16)
        st_ref[u] = cds[u] * st + _dot_tn(kw, v)

    def step(i, carry):
        cb = _bwd_chunk(i, n_ctx, n_all)
        for hh in range(2):
            unit(i, hh, 0)
            unit(cb, hh, 1)
        return carry

    lax.fori_loop(0, n_all, step, 0)


def _ret_call(rq, rk, rv, lgb, n_ctx):
    B, H, T, dk = rq.shape
    dv = rv.shape[-1]
    qspec = pl.BlockSpec((1, 2, T, dk), lambda b, p: (b, p, 0, 0))
    vspec = pl.BlockSpec((1, 2, T, dv), lambda b, p: (b, p, 0, 0))
    return pl.pallas_call(
        functools.partial(_ret_kernel, n_ctx=n_ctx),
        grid=(B, H // 2),
        in_specs=[qspec, qspec, vspec,
                  pl.BlockSpec((2, 2, 8, LANES), lambda b, p: (p, 0, 0, 0))],
        out_specs=vspec,
        out_shape=jax.ShapeDtypeStruct((B, H, T, dv), F32),
        scratch_shapes=[pltpu.VMEM((4, dk, dv), F32), pltpu.VMEM((4, CHUNK, CHUNK), F32),
                        pltpu.VMEM((4, CHUNK, dv), F32), pltpu.VMEM((4, CHUNK, dk), F32)],
        compiler_params=_params(("parallel", "parallel")),
        name="retention",
    )(rq, rk, rv, lgb)


def _attn_kernel(q_ref, k_ref, vt_ref, sink_ref, o_ref, *, n_ctx_tok, need_ctx):
    T = q_ref.shape[2]
    L = n_ctx_tok
    S = T - L
    nb = S // A_BLOCK
    nspan = 3
    span = nspan * A_BLOCK
    ncb = L // A_BLOCK
    cols = A_GROUP * A_BLOCK
    kc = k_ref[0, 0, 0:L, :]
    sink = sink_ref[0]
    qpos = lax.broadcasted_iota(jnp.int32, (1, cols), 1) % A_BLOCK
    kpos = lax.broadcasted_iota(jnp.int32, (span, 1), 0)

    def finish(blk, acc, m):
        den = acc[A_HEAD_DIM:A_HEAD_DIM + 1] + jnp.exp(sink - m)
        ot = (acc[0:A_HEAD_DIM] * (1.0 / den)).astype(BF16)
        for g in range(A_GROUP):
            o_ref[0, blk, g * A_HEAD_DIM:(g + 1) * A_HEAD_DIM, :] = ot[:, g * A_BLOCK:(g + 1) * A_BLOCK]

    def ctx_part(qs, p_ctx):
        acc = _dot(vt_ref[0, 0, 0], p_ctx[0:A_BLOCK])
        for j in range(1, ncb):
            acc = acc + _dot(vt_ref[0, 0, j], p_ctx[j * A_BLOCK:(j + 1) * A_BLOCK])
        return acc

    for cbk in range(ncb):
        if need_ctx:
            qs = q_ref[0, :, cbk * A_BLOCK:(cbk + 1) * A_BLOCK, :].reshape(cols, A_HEAD_DIM)
            s = _dot_nt(kc, qs)
            m = jnp.maximum(jnp.max(s, axis=0, keepdims=True), sink)
            finish(cbk, ctx_part(qs, jnp.exp(s - m).astype(BF16)), m)
        else:
            o_ref[0, cbk] = jnp.zeros(o_ref.shape[2:], BF16)

    def block(n, carry):
        q0 = pl.multiple_of(L + n * A_BLOCK, A_BLOCK)
        kb = jnp.clip(n - 1, 0, nb - nspan)
        k0 = pl.multiple_of(L + kb * A_BLOCK, A_BLOCK)
        qs = q_ref[0, :, pl.ds(q0, A_BLOCK), :].reshape(cols, A_HEAD_DIM)
        s_loc = _dot_nt(k_ref[0, 0, pl.ds(k0, span), :], qs)
        qi = n * A_BLOCK + qpos
        kj = kb * A_BLOCK + kpos
        s_loc = jnp.where(jnp.abs(qi - kj) <= A_WINDOW, s_loc, NEG_INF)
        s_ctx = _dot_nt(kc, qs)
        m = jnp.maximum(jnp.maximum(jnp.max(s_loc, axis=0, keepdims=True),
                                    jnp.max(s_ctx, axis=0, keepdims=True)), sink)
        p_loc = jnp.exp(s_loc - m).astype(BF16)
        acc = ctx_part(qs, jnp.exp(s_ctx - m).astype(BF16))
        for j in range(nspan):
            acc = acc + _dot(vt_ref[0, 0, ncb + kb + j], p_loc[j * A_BLOCK:(j + 1) * A_BLOCK])
        finish(ncb + n, acc, m)
        return carry

    lax.fori_loop(0, nb, block, 0)


def _attn_call(aq, ak, avt, sink_rows, n_ctx_tok, need_ctx):
    B, _, T, hd = aq.shape
    nblk = T // A_BLOCK
    cols = A_GROUP * A_BLOCK
    return pl.pallas_call(
        functools.partial(_attn_kernel, n_ctx_tok=n_ctx_tok, need_ctx=need_ctx),
        grid=(B, A_KV_HEADS),
        in_specs=[pl.BlockSpec((1, A_GROUP, T, hd), lambda b, h: (b, h, 0, 0)),
                  pl.BlockSpec((1, 1, T, hd), lambda b, h: (b, h, 0, 0)),
                  pl.BlockSpec((1, 1, nblk, LANES, A_BLOCK), lambda b, h: (b, h, 0, 0, 0)),
                  pl.BlockSpec((1, 1, cols), lambda b, h: (h, 0, 0))],
        out_specs=pl.BlockSpec((1, nblk, A_GROUP * hd, A_BLOCK), lambda b, h: (b, 0, h, 0)),
        out_shape=jax.ShapeDtypeStruct((B, nblk, A_WIDTH, A_BLOCK), BF16),
        compiler_params=_params(("parallel", "parallel")),
        name="window_attn",
    )(aq, ak, avt, sink_rows)


def _head_norm(y):
    mu = jnp.mean(y, axis=-1, keepdims=True)
    yc = y - mu
    var = jnp.mean(yc * yc, axis=-1, keepdims=True)
    return yc * lax.rsqrt(var + NORM_EPS)


def _post_kernel(x_ref, hb_ref, hm_ref, ya_ref, hr_ref, mod_ref, wg_ref, mnw_ref, rnw_ref,
                 wbm_ref, wba_ref, wbr_ref, wo_ref, o_ref):
    hb = hb_ref[0]
    off = 0
    hm = jnp.concatenate([_head_norm(hm_ref[0, h]) for h in range(M_HEADS)], axis=1) * mnw_ref[...]
    ym = (_sigmoid(_dot(hb, wg_ref[:, off:off + M_WIDTH])) * hm).astype(BF16)
    off += M_WIDTH
    hr = jnp.concatenate([_head_norm(hr_ref[0, h]) for h in range(R_HEADS)], axis=1) * rnw_ref[...]
    rg = _dot(hb, wg_ref[:, off:off + R_V_WIDTH])
    yr = (rg * _sigmoid(rg) * hr).astype(BF16)
    off += R_V_WIDTH
    z = _sigmoid(_dot(hb, wg_ref[:, off:off + D_MODEL])) * _dot(ym, wbm_ref[...])
    off += D_MODEL
    pa = jnp.concatenate([_dot_tn(ya_ref[0, c], wba_ref[...]) for c in range(TM // A_BLOCK)], axis=0)
    z = z + _sigmoid(_dot(hb, wg_ref[:, off:off + D_MODEL])) * pa
    off += D_MODEL
    z = z + _sigmoid(_dot(hb, wg_ref[:, off:off + D_MODEL])) * _dot(yr, wbr_ref[...])
    y = _dot(z.astype(BF16), wo_ref[...])
    o_ref[0] = x_ref[0] + mod_ref[0, 0, 2:3, :] * y


def _post_call(xc, hb, hm, ya, hr, modsel, wg, mnw, rnw, wbm, wba, wbr, wo, t_off):
    B, T, D = xc.shape
    nt = T // TM - t_off

    def tile(t, b):
        return (b, t + t_off, 0)

    def head_tile(t, b):
        return (b, 0, t + t_off, 0)

    return pl.pallas_call(
        _post_kernel,
        grid=(nt, B),
        in_specs=[pl.BlockSpec((1, TM, D), tile),
                  pl.BlockSpec((1, TM, D), tile),
                  pl.BlockSpec((1, M_HEADS, TM, M_HEAD_DIM), head_tile),
                  pl.BlockSpec((1, TM // A_BLOCK, A_WIDTH, A_BLOCK), lambda t, b: (b, t + t_off, 0, 0)),
                  pl.BlockSpec((1, R_HEADS, TM, R_V_DIM), head_tile),
                  pl.BlockSpec((1, 1, 6, D), lambda t, b: (b, jnp.minimum(t + t_off, 1), 0, 0)),
                  _const_spec(wg.shape), _const_spec(mnw.shape), _const_spec(rnw.shape),
                  _const_spec(wbm.shape), _const_spec(wba.shape), _const_spec(wbr.shape),
                  _const_spec(wo.shape)],
        out_specs=pl.BlockSpec((1, TM, D), lambda t, b: (b, t, 0)),
        out_shape=jax.ShapeDtypeStruct((B, nt * TM, D), F32),
        compiler_params=_params(("parallel", "parallel")),
        name="post_merge",
    )(xc, hb, hm, ya, hr, modsel, wg, mnw, rnw, wbm, wba, wbr, wo)


def _ffn_kernel(x_ref, xp_ref, xn_ref, mod_ref, nw_ref, wa_ref, wb_ref, cw_ref, cb_ref, wd_ref, fw_ref,
                o_ref, hext_ref, u_ref, *, has_ctx, final):
    t = pl.program_id(0)
    nt = pl.num_programs(0)
    first_lat = 1 if has_ctx else 0
    sh = mod_ref[0, 0, 3:4, :]
    sc = mod_ref[0, 0, 4:5, :]
    g2 = mod_ref[0, 0, 5:6, :]
    nw = nw_ref[...]
    x = x_ref[0]
    hp = _norm_mod(xp_ref[0], nw, sc, sh)
    hn = _norm_mod(xn_ref[0], nw, sc, sh)
    hp = jnp.where(t <= first_lat, 0.0, hp)
    hn = jnp.where((t == first_lat - 1) | (t == nt - 1), 0.0, hn)
    hext_ref[0:HALO, :] = hp.astype(BF16)
    hext_ref[HALO:HALO + TM, :] = _norm_mod(x, nw, sc, sh).astype(BF16)
    hext_ref[HALO + TM:, :] = hn.astype(BF16)
    ext = TM + 2 * HALO
    for j in range(FFN_DIM // FFN_NC):
        cs = slice(j * FFN_NC, (j + 1) * FFN_NC)
        a = _dot(hext_ref[...], wa_ref[:, cs])
        b = _dot(hext_ref[HALO:HALO + TM, :], wb_ref[:, cs])
        cw = cw_ref[:, cs]
        prev = pltpu.roll(a, 1, 0)[HALO:HALO + TM]
        nxt = pltpu.roll(a, ext - 1, 0)[HALO:HALO + TM]
        conv = prev * cw[0:1] + a[HALO:HALO + TM] * cw[1:2] + nxt * cw[2:3] + cb_ref[:, cs]
        u_ref[:, cs] = (conv * _sigmoid(conv) * b).astype(BF16)
    y = x + g2 * _dot(u_ref[...], wd_ref[...])
    if final:
        ms = jnp.mean(y * y, axis=-1, keepdims=True)
        y = y * lax.rsqrt(ms + NORM_EPS) * fw_ref[...]
    o_ref[0] = y


def _ffn_call(xs, modsel, nw, wa, wb, cw, cb, wd, fw, has_ctx, final):
    B, T, D = xs.shape
    nt = T // TM
    hb16 = TM // HALO
    first_lat = 1 if has_ctx else 0
    tile = pl.BlockSpec((1, TM, D), lambda t, b: (b, t, 0))
    return pl.pallas_call(
        functools.partial(_ffn_kernel, has_ctx=has_ctx, final=final),
        grid=(nt, B),
        in_specs=[tile,
                  pl.BlockSpec((1, HALO, D), lambda t, b: (b, jnp.maximum(t * hb16 - 1, 0), 0)),
                  pl.BlockSpec((1, HALO, D), lambda t, b: (b, jnp.minimum((t + 1) * hb16, T // HALO - 1), 0)),
                  pl.BlockSpec((1, 1, 6, D), lambda t, b: (b, jnp.minimum(t + 1 - first_lat, 1), 0, 0)),
                  _const_spec(nw.shape), _const_spec(wa.shape), _const_spec(wb.shape),
                  _const_spec(cw.shape), _const_spec(cb.shape), _const_spec(wd.shape),
                  _const_spec(fw.shape)],
        out_specs=tile,
        out_shape=jax.ShapeDtypeStruct((B, T, D), F32),
        scratch_shapes=[pltpu.VMEM((TM + 2 * HALO, D), BF16), pltpu.VMEM((TM, FFN_DIM), BF16)],
        compiler_params=_params(("parallel", "parallel")),
        name="conv_ffn",
    )(xs, xs, xs, modsel, nw, wa, wb, cw, cb, wd, fw)


def _split_cols(w):
    out = []
    acc = 0
    for s in IN_SPLITS:
        out.append(w[:, acc:acc + s])
        acc += s
    return out


def _rope_tables(L, S):
    T = L + S
    j = jnp.arange(LANES)
    jj = j % 32
    inv_a = ROPE_BASE ** (-(jj % 16).astype(F32) / 16.0)
    t = jnp.arange(S)
    pos = jnp.where(((j % A_HEAD_DIM) // 32 == 0)[None, :], (t // GRID_W)[:, None], (t % GRID_W)[:, None])
    ang = pos.astype(F32) * inv_a[None, :]
    sign = jnp.where(jj < 16, -1.0, 1.0)[None, :]
    ca = jnp.concatenate([jnp.ones((L, LANES), F32), jnp.cos(ang)], axis=0)
    sa = jnp.concatenate([jnp.zeros((L, LANES), F32), jnp.sin(ang) * sign], axis=0)
    half = R_QK_DIM // 2
    inv_r = ROPE_BASE ** (-(j % half).astype(F32) / half)
    angr = jnp.arange(T).astype(F32)[:, None] * inv_r[None, :]
    signr = jnp.where(j < half, -1.0, 1.0)[None, :]
    return ca, sa, jnp.cos(angr), jnp.sin(angr) * signr


def kernel(x, c, ctx, c_ctx, mod_w, mod_b, norm1_w, norm2_w, w_in, m_gate_bias, m_conv_w, m_norm_w,
           a_sink, ret_logit, ret_norm_w, w_br_m, w_br_a, w_br_r, w_out, ffn_up, ffn_conv_w, ffn_conv_b,
           ffn_down, final_norm_w):
    B, S, D = x.shape
    L = ctx.shape[1]
    depth = mod_w.shape[0]
    assert D == D_MODEL and L == TM and S % TM == 0 and S >= 3 * A_BLOCK
    T = L + S
    n_ctx = L // CHUNK

    rows = -(-(B + 1) // 8) * 8
    cpad = jnp.zeros((rows, D), F32).at[:B].set(c).at[B].set(c_ctx)
    mods = _modulation(cpad, mod_w, mod_b)
    tabs = _rope_tables(L, S)
    xc = jnp.concatenate([ctx, x], axis=1)

    gperm = jnp.arange(4 * M_HEADS).reshape(4, M_HEADS).T.reshape(-1)

    for l in range(depth):
        last = l == depth - 1
        lat = mods[l, :B].reshape(B, 6, D)
        cm = jnp.broadcast_to(mods[l, B].reshape(1, 6, D), (B, 6, D))
        modsel = jnp.stack([cm, lat], axis=1)

        (w_mq, w_mk, w_mv, w_mo, w_mg, w_aq, w_ak, w_av,
         w_rq, w_rk, w_rv, w_rg, w_gm, w_ga, w_gr) = _split_cols(w_in[l])
        wqk = jnp.concatenate([w_mq, w_mk], axis=1).astype(BF16)
        wr = jnp.concatenate([w_mv, w_aq, w_ak, w_av, w_rq, w_rk, w_rv], axis=1).astype(BF16)
        wg_cols = w_mg[:, gperm]
        wg = jnp.pad(wg_cols, ((0, 0), (0, LANES - 4 * M_HEADS))).astype(BF16)
        wgt = wg_cols.T.astype(BF16)
        gbias = m_gate_bias[l][gperm]
        gb = jnp.pad(gbias, (0, LANES - 4 * M_HEADS)).reshape(1, LANES)
        gbt = jnp.broadcast_to(gbias.reshape(-1, 1), (4 * M_HEADS, LANES))

        (hb, mq, mk, mv, gc, gr, aq, ak, av, rq, rk, rv) = _pre_call(
            xc, modsel, norm1_w[l].reshape(1, D), wqk, m_conv_w[l], wr, wg, wgt, gb, gbt, tabs)

        hm = _mlstm_call(mq, mk, mv, gc, gr, n_ctx)
        sink_rows = jnp.repeat(a_sink[l].reshape(A_KV_HEADS, A_GROUP), A_BLOCK, axis=1)[:, None, :]
        ya = _attn_call(aq, ak, av, sink_rows.astype(F32), L, not last)
        lgb = jnp.broadcast_to(ret_logit[l].T[:, :, None, None], (R_HEADS, 2, 8, LANES)).astype(F32)
        hr = _ret_call(rq, rk, rv, lgb, n_ctx)

        t_off = 1 if last else 0
        wgates = jnp.concatenate([w_mo, w_rg, w_gm, w_ga, w_gr], axis=1).astype(BF16)
        x1 = _post_call(xc, hb, hm, ya, hr, modsel, wgates, m_norm_w[l].reshape(1, -1),
                        ret_norm_w[l].reshape(1, -1), w_br_m[l].astype(BF16), w_br_a[l].astype(BF16),
                        w_br_r[l].astype(BF16), w_out[l].astype(BF16), t_off)
        xc = _ffn_call(x1, modsel, norm2_w[l].reshape(1, D), ffn_up[l][:, :FFN_DIM].astype(BF16),
                       ffn_up[l][:, FFN_DIM:].astype(BF16), ffn_conv_w[l], ffn_conv_b[l].reshape(1, -1),
                       ffn_down[l].astype(BF16), final_norm_w.reshape(1, D), not last, last)
    return xc
```

```python
import functools

import jax
import jax.numpy as jnp
from jax import lax
from jax.experimental import pallas as pl
from jax.experimental.pallas import tpu as pltpu

F32 = jnp.float32
BF16 = jnp.bfloat16

D_MODEL = 1024
GRID_W = 64
NORM_EPS = 1e-6
ROPE_BASE = 10000.0
NEG_INF = -1e30

M_HEADS = 4
M_HEAD_DIM = 128
M_WIDTH = M_HEADS * M_HEAD_DIM
A_HEADS = 8
A_KV_HEADS = 2
A_GROUP = A_HEADS // A_KV_HEADS
A_HEAD_DIM = 64
A_WIDTH = A_HEADS * A_HEAD_DIM
A_KV_WIDTH = A_KV_HEADS * A_HEAD_DIM
A_WINDOW = 128
A_BLOCK = 128
R_HEADS = 4
R_QK_DIM = 128
R_V_DIM = 256
R_QK_WIDTH = R_HEADS * R_QK_DIM
R_V_WIDTH = R_HEADS * R_V_DIM
FFN_DIM = 2816

IN_SPLITS = (M_WIDTH, M_WIDTH, M_WIDTH, M_WIDTH, 4 * M_HEADS,
             A_WIDTH, A_KV_WIDTH, A_KV_WIDTH,
             R_QK_WIDTH, R_QK_WIDTH, R_V_WIDTH, R_V_WIDTH,
             D_MODEL, D_MODEL, D_MODEL)

TM = 256
HALO = 16
CHUNK = 128
FFN_NC = 256
LANES = 128
VMEM_LIMIT = 56 * 1024 * 1024


def _sigmoid(x):
    return 1.0 / (1.0 + jnp.exp(-x))


def _log_sigmoid(x):
    return jnp.minimum(x, 0.0) - jnp.log1p(jnp.exp(-jnp.abs(x)))


def _dot(a, b):
    return jnp.dot(a, b, preferred_element_type=F32)


def _dot_nt(a, b):
    return lax.dot_general(a, b, (((1,), (1,)), ((), ())), preferred_element_type=F32)


def _dot_tn(a, b):
    return lax.dot_general(a, b, (((0,), (0,)), ((), ())), preferred_element_type=F32)


def _const_spec(shape):
    nd = len(shape)
    return pl.BlockSpec(shape, lambda *_: (0,) * nd, pipeline_mode=pl.Buffered(1))


def _params(sem):
    return pltpu.CompilerParams(dimension_semantics=sem, vmem_limit_bytes=VMEM_LIMIT)


def _mod_kernel(c_ref, w_ref, b_ref, o_ref):
    c = c_ref[...]
    s = (c * _sigmoid(c)).astype(BF16)
    o_ref[0] = _dot(s, w_ref[0].astype(BF16)) + b_ref[0]


def _modulation(cpad, mod_w, mod_b):
    depth, d, n = mod_w.shape
    rows = cpad.shape[0]
    bn = 1024
    return pl.pallas_call(
        _mod_kernel,
        grid=(depth, n // bn),
        in_specs=[pl.BlockSpec((rows, d), lambda l, j: (0, 0)),
                  pl.BlockSpec((1, d, bn), lambda l, j: (l, 0, j)),
                  pl.BlockSpec((1, 1, bn), lambda l, j: (l, 0, j))],
        out_specs=pl.BlockSpec((1, rows, bn), lambda l, j: (l, 0, j)),
        out_shape=jax.ShapeDtypeStruct((depth, rows, n), F32),
        compiler_params=_params(("parallel", "parallel")),
        name="modulation",
    )(cpad, mod_w, mod_b.reshape(depth, 1, n))


def _norm_mod(xv, nw, sc, sh):
    ms = jnp.mean(xv * xv, axis=-1, keepdims=True)
    return (xv * lax.rsqrt(ms + NORM_EPS) * nw) * (1.0 + sc) + sh


def _seg_cumsum(x, axis, reverse):
    n = x.shape[axis]
    idx = lax.broadcasted_iota(jnp.int32, x.shape, axis)
    s = 1
    while s < n:
        if reverse:
            shifted = pltpu.roll(x, n - s, axis)
            x = x + jnp.where(idx < n - s, shifted, 0.0)
        else:
            shifted = pltpu.roll(x, s, axis)
            x = x + jnp.where(idx >= s, shifted, 0.0)
        s *= 2
    return x


def _gate_transform(raw, gate_axis):
    tok_axis = 1 - gate_axis
    k = lax.broadcasted_iota(jnp.int32, raw.shape, gate_axis) % 4
    lf = _log_sigmoid(raw)
    cum_f = _seg_cumsum(jnp.where(k == 1, lf, 0.0), tok_axis, False)
    cum_b = _seg_cumsum(jnp.where(k == 3, lf, 0.0), tok_axis, True)
    bsel = jnp.where(k == 1, cum_f, jnp.where(k == 3, cum_b, 0.0))
    n = raw.shape[gate_axis]
    bnext = pltpu.roll(bsel, n - 1, gate_axis)
    nt = raw.shape[tok_axis]
    tot_f = lax.slice_in_dim(cum_f, nt - 1, nt, axis=tok_axis)
    tot_b = lax.slice_in_dim(cum_b, 0, 1, axis=tok_axis)
    total = jnp.broadcast_to(tot_f + tot_b, raw.shape)
    return jnp.where(k % 2 == 1, bsel, raw - bnext), total


def _pre_kernel(x_ref, xp_ref, xn_ref, mod_ref, nw_ref, wqk_ref, cw_ref, wr_ref, wg_ref, wgt_ref,
                gb_ref, gbt_ref, ca_ref, sa_ref, cr_ref, sr_ref,
                hb_ref, mq_ref, mk_ref, mv_ref, gc_ref, gr_ref, aq_ref, ak_ref, av_ref,
                rq_ref, rk_ref, rv_ref, hext_ref):
    t = pl.program_id(0)
    nt = pl.num_programs(0)
    sh = mod_ref[0, 0, 0:1, :]
    sc = mod_ref[0, 0, 1:2, :]
    nw = nw_ref[...]
    hb = _norm_mod(x_ref[0], nw, sc, sh).astype(BF16)
    hb_ref[0] = hb

    hp = _norm_mod(xp_ref[0], nw, sc, sh)
    hn = _norm_mod(xn_ref[0], nw, sc, sh)
    hp = jnp.where(t <= 1, 0.0, hp)
    hn = jnp.where((t == 0) | (t == nt - 1), 0.0, hn)
    hext_ref[0:HALO, :] = hp.astype(BF16)
    hext_ref[HALO:HALO + TM, :] = hb
    hext_ref[HALO + TM:, :] = hn.astype(BF16)
    hext = hext_ref[...]
    ext = TM + 2 * HALO

    for j, (dst, scale) in enumerate(((mq_ref, 1.0), (mk_ref, M_HEAD_DIM ** -0.5))):
        p = _dot(hext, wqk_ref[:, j * M_WIDTH:(j + 1) * M_WIDTH])
        cw = cw_ref[:, j * M_WIDTH:(j + 1) * M_WIDTH]
        prev = pltpu.roll(p, 1, 0)[HALO:HALO + TM]
        nxt = pltpu.roll(p, ext - 1, 0)[HALO:HALO + TM]
        conv = prev * cw[0:1] + p[HALO:HALO + TM] * cw[1:2] + nxt * cw[2:3]
        act = conv * _sigmoid(conv) * scale
        for h in range(M_HEADS):
            dst[0, h] = act[:, h * M_HEAD_DIM:(h + 1) * M_HEAD_DIM].astype(BF16)

    off = 0
    p = _dot(hb, wr_ref[:, off:off + M_WIDTH])
    off += M_WIDTH
    for h in range(M_HEADS):
        for c in range(TM // CHUNK):
            blk = p[c * CHUNK:(c + 1) * CHUNK, h * M_HEAD_DIM:(h + 1) * M_HEAD_DIM]
            mv_ref[0, h, c] = blk.T.astype(BF16)

    lane = lax.broadcasted_iota(jnp.int32, (TM, LANES), 1)
    first16 = (lane % 32) < 16
    ca = ca_ref[...]
    sa = sa_ref[...]

    def rope_a(xs):
        partner = jnp.where(first16, pltpu.roll(xs, LANES - 16, 1), pltpu.roll(xs, 16, 1))
        return xs * ca + partner * sa

    p = _dot(hb, wr_ref[:, off:off + A_WIDTH])
    off += A_WIDTH
    for s in range(A_WIDTH // LANES):
        r = (rope_a(p[:, s * LANES:(s + 1) * LANES]) * A_HEAD_DIM ** -0.5).astype(BF16)
        aq_ref[0, 2 * s] = r[:, 0:A_HEAD_DIM]
        aq_ref[0, 2 * s + 1] = r[:, A_HEAD_DIM:]
    p = _dot(hb, wr_ref[:, off:off + 2 * A_KV_WIDTH])
    off += 2 * A_KV_WIDTH
    r = rope_a(p[:, 0:LANES]).astype(BF16)
    for h in range(A_KV_HEADS):
        ak_ref[0, h] = r[:, h * A_HEAD_DIM:(h + 1) * A_HEAD_DIM]
    pv = p[:, LANES:]
    for h in range(A_KV_HEADS):
        vh = pv if h == 0 else pltpu.roll(pv, A_HEAD_DIM, 1)
        ve = jnp.where(lane < A_HEAD_DIM, vh, 1.0)
        for c in range(TM // A_BLOCK):
            av_ref[0, h, c] = ve[c * A_BLOCK:(c + 1) * A_BLOCK].T.astype(BF16)

    cr = cr_ref[...]
    sr = sr_ref[...]
    for is_k in (False, True):
        p = _dot(hb, wr_ref[:, off:off + R_QK_WIDTH])
        off += R_QK_WIDTH
        for h in range(R_HEADS):
            xs = p[:, h * R_QK_DIM:(h + 1) * R_QK_DIM]
            rot = xs * cr + pltpu.roll(xs, R_QK_DIM // 2, 1) * sr
            if is_k:
                rot = rot * R_QK_DIM ** -0.5
                for c in range(TM // CHUNK):
                    rk_ref[0, h, c] = rot[c * CHUNK:(c + 1) * CHUNK].T.astype(BF16)
            else:
                rq_ref[0, h] = rot.astype(BF16)
    p = _dot(hb, wr_ref[:, off:off + R_V_WIDTH])
    for h in range(R_HEADS):
        rv_ref[0, h] = p[:, h * R_V_DIM:(h + 1) * R_V_DIM].astype(BF16)

    gcol = _dot(hb, wg_ref[...]) + gb_ref[...]
    grow = _dot_nt(wgt_ref[...], hb) + gbt_ref[:, 0:1]
    for c in range(TM // CHUNK):
        gc, _ = _gate_transform(gcol[c * CHUNK:(c + 1) * CHUNK], 1)
        gr, tot = _gate_transform(grow[:, c * CHUNK:(c + 1) * CHUNK], 0)
        aend = gr + pltpu.roll(tot, 4 * M_HEADS - 1, 0)
        amax = jnp.broadcast_to(jnp.max(aend, axis=1, keepdims=True), aend.shape)
        for hp_ in range(2):
            gc_ref[0, hp_, c * CHUNK:(c + 1) * CHUNK, :] = gc[:, 8 * hp_:8 * hp_ + 8]
            for hh in range(2):
                r0 = 8 * hp_ + 4 * hh
                o0 = 8 * hh
                gr_ref[0, hp_, c, o0:o0 + 4, :] = jnp.where(
                    lax.broadcasted_iota(jnp.int32, (4, CHUNK), 0) % 2 == 1, gr[r0:r0 + 4], aend[r0:r0 + 4])
                for d in range(2):
                    gr_ref[0, hp_, c, o0 + 4 + d:o0 + 5 + d, :] = tot[r0 + 2 * d + 1:r0 + 2 * d + 2]
                    gr_ref[0, hp_, c, o0 + 6 + d:o0 + 7 + d, :] = amax[r0 + 2 * d:r0 + 2 * d + 1]


def _pre_call(xc, modsel, nw, wqk, cw, wr, wg, wgt, gb, gbt, tabs):
    B, T, D = xc.shape
    nt = T // TM
    hb16 = TM // HALO
    nch = T // CHUNK

    def tile(t, b):
        return (b, t, 0)

    def head_tile(t, b):
        return (b, 0, t, 0)

    in_specs = [
        pl.BlockSpec((1, TM, D), tile),
        pl.BlockSpec((1, HALO, D), lambda t, b: (b, jnp.maximum(t * hb16 - 1, 0), 0)),
        pl.BlockSpec((1, HALO, D), lambda t, b: (b, jnp.minimum((t + 1) * hb16, T // HALO - 1), 0)),
        pl.BlockSpec((1, 1, 6, D), lambda t, b: (b, jnp.minimum(t, 1), 0, 0)),
        _const_spec(nw.shape), _const_spec(wqk.shape), _const_spec(cw.shape), _const_spec(wr.shape),
        _const_spec(wg.shape), _const_spec(wgt.shape), _const_spec(gb.shape), _const_spec(gbt.shape),
    ] + [pl.BlockSpec((TM, LANES), lambda t, b: (t, 0)) for _ in range(4)]
    out_shape = [
        jax.ShapeDtypeStruct((B, T, D), BF16),
        jax.ShapeDtypeStruct((B, M_HEADS, T, M_HEAD_DIM), BF16),
        jax.ShapeDtypeStruct((B, M_HEADS, T, M_HEAD_DIM), BF16),
        jax.ShapeDtypeStruct((B, M_HEADS, nch, M_HEAD_DIM, CHUNK), BF16),
        jax.ShapeDtypeStruct((B, 2, T, 8), F32),
        jax.ShapeDtypeStruct((B, 2, nch, 16, CHUNK), F32),
        jax.ShapeDtypeStruct((B, A_HEADS, T, A_HEAD_DIM), BF16),
        jax.ShapeDtypeStruct((B, A_KV_HEADS, T, A_HEAD_DIM), BF16),
        jax.ShapeDtypeStruct((B, A_KV_HEADS, T // A_BLOCK, LANES, A_BLOCK), BF16),
        jax.ShapeDtypeStruct((B, R_HEADS, T, R_QK_DIM), BF16),
        jax.ShapeDtypeStruct((B, R_HEADS, nch, R_QK_DIM, CHUNK), BF16),
        jax.ShapeDtypeStruct((B, R_HEADS, T, R_V_DIM), BF16),
    ]
    out_specs = [
        pl.BlockSpec((1, TM, D), tile),
        pl.BlockSpec((1, M_HEADS, TM, M_HEAD_DIM), head_tile),
        pl.BlockSpec((1, M_HEADS, TM, M_HEAD_DIM), head_tile),
        pl.BlockSpec((1, M_HEADS, TM // CHUNK, M_HEAD_DIM, CHUNK), lambda t, b: (b, 0, t, 0, 0)),
        pl.BlockSpec((1, 2, TM, 8), head_tile),
        pl.BlockSpec((1, 2, TM // CHUNK, 16, CHUNK), lambda t, b: (b, 0, t, 0, 0)),
        pl.BlockSpec((1, A_HEADS, TM, A_HEAD_DIM), head_tile),
        pl.BlockSpec((1, A_KV_HEADS, TM, A_HEAD_DIM), head_tile),
        pl.BlockSpec((1, A_KV_HEADS, TM // A_BLOCK, LANES, A_BLOCK), lambda t, b: (b, 0, t, 0, 0)),
        pl.BlockSpec((1, R_HEADS, TM, R_QK_DIM), head_tile),
        pl.BlockSpec((1, R_HEADS, TM // CHUNK, R_QK_DIM, CHUNK), lambda t, b: (b, 0, t, 0, 0)),
        pl.BlockSpec((1, R_HEADS, TM, R_V_DIM), head_tile),
    ]
    return pl.pallas_call(
        _pre_kernel,
        grid=(nt, B),
        in_specs=in_specs,
        out_specs=out_specs,
        out_shape=out_shape,
        scratch_shapes=[pltpu.VMEM((TM + 2 * HALO, D), BF16)],
        compiler_params=_params(("parallel", "parallel")),
        name="pre_proj",
    )(xc, xc, xc, modsel, nw, wqk, cw, wr, wg, wgt, gb, gbt, *tabs)


def _bwd_chunk(i, n_ctx, n_all):
    if isinstance(i, int):
        return n_ctx - 1 - i if i < n_ctx else n_all - 1 + n_ctx - i
    return n_all - 1 + n_ctx - i


def _chunk_start(c):
    return c * CHUNK if isinstance(c, int) else pl.multiple_of(c * CHUNK, CHUNK)


def _chunk_loop(body, n_ctx, n_all, unroll):
    for i in range(n_ctx):
        body(i, 0)
    n_lat = n_all - n_ctx
    while n_lat % unroll:
        unroll //= 2
    lax.fori_loop(n_ctx, n_all, body, 0, unroll=unroll)


def _mlstm_kernel(q_ref, k_ref, vt_ref, gc_ref, gr_ref, o_ref, c_ref, n_ref, m_ref,
                  cs_ref, ns_ref, ms_ref, *, n_ctx):
    n_all = gr_ref.shape[2]
    dh = q_ref.shape[3]
    c_ref[...] = jnp.zeros(c_ref.shape, F32)
    n_ref[...] = jnp.zeros(n_ref.shape, F32)
    m_ref[...] = jnp.zeros(m_ref.shape, F32)

    def scan_unit(c, hh, d):
        u = 2 * hh + d
        c0 = _chunk_start(c)
        k = k_ref[0, hh, pl.ds(c0, CHUNK), :]
        vt = vt_ref[0, hh, c]
        rows = gr_ref[0, 0, c, 8 * hh:8 * hh + 8, :]
        aend = rows[2 * d:2 * d + 1]
        tot = rows[4 + d:5 + d]
        amax = rows[6 + d:7 + d]
        cst = c_ref[u]
        nv = n_ref[u]
        m = m_ref[u]
        cs_ref[u, c] = cst.astype(BF16)
        ns_ref[u, c] = jnp.broadcast_to(nv, (16, dh)).astype(BF16)
        ms_ref[u, c] = jnp.broadcast_to(m, (8, CHUNK))
        m_new = jnp.maximum(tot + m, amax)
        w = jnp.exp(aend - m_new)
        dec = jnp.exp(tot + m - m_new)
        vw = (vt.astype(F32) * w).astype(BF16)
        c_ref[u] = dec * cst + _dot(vw, k)
        wk = _dot(jnp.broadcast_to(w, (16, CHUNK)).astype(BF16), k)
        n_ref[u] = dec * nv + wk[0:1]
        m_ref[u] = m_new

    def scan_step(i, carry):
        cb = _bwd_chunk(i, n_ctx, n_all)
        for hh in range(2):
            scan_unit(i, hh, 0)
            scan_unit(cb, hh, 1)
        return carry

    _chunk_loop(scan_step, n_ctx, n_all, 4)

    row = lax.broadcasted_iota(jnp.int32, (CHUNK, CHUNK), 0)
    col = lax.broadcasted_iota(jnp.int32, (CHUNK, CHUNK), 1)
    tri = (row <= col, row >= col)

    def out_step(c, carry):
        c0 = _chunk_start(c)
        gcv = gc_ref[0, 0, pl.ds(c0, CHUNK), :]
        for hh in range(2):
            q = q_ref[0, hh, pl.ds(c0, CHUNK), :]
            k = k_ref[0, hh, pl.ds(c0, CHUNK), :]
            vt = vt_ref[0, hh, c]
            rows = gr_ref[0, 0, c, 8 * hh:8 * hh + 8, :]
            st = _dot_nt(k, q)
            acc = None
            for d in range(2):
                u = 2 * hh + d
                a0c = gcv[:, 4 * hh + 2 * d:4 * hh + 2 * d + 1]
                b = rows[2 * d + 1:2 * d + 2]
                m = ms_ref[u, c, 0:1, :]
                dlog = jnp.where(tri[d], a0c + b, NEG_INF)
                mj = jnp.maximum(b + m, jnp.max(dlog, axis=0, keepdims=True))
                sd = st * jnp.exp(dlog - mj)
                iw = jnp.exp(b + m - mj)
                nq = _dot_nt(ns_ref[u, c], q)[0:1]
                den = iw * nq + jnp.sum(sd, axis=0, keepdims=True)
                inv = 1.0 / jnp.maximum(jnp.abs(den), jnp.exp(-mj))
                ht = (_dot(vt, sd.astype(BF16)) + iw * _dot_nt(cs_ref[u, c], q)) * inv
                acc = ht if acc is None else acc + ht
            o_ref[0, hh, pl.ds(c0, CHUNK), :] = acc.T
        return carry

    _chunk_loop(out_step, n_ctx, n_all, 2)


def _mlstm_call(mq, mk, mvt, gc, gr, n_ctx):
    B, H, T, dh = mq.shape
    nch = T // CHUNK
    qspec = pl.BlockSpec((1, 2, T, dh), lambda b, p: (b, p, 0, 0))
    return pl.pallas_call(
        functools.partial(_mlstm_kernel, n_ctx=n_ctx),
        grid=(B, H // 2),
        in_specs=[qspec, qspec,
                  pl.BlockSpec((1, 2, nch, dh, CHUNK), lambda b, p: (b, p, 0, 0, 0)),
                  pl.BlockSpec((1, 1, T, 8), lambda b, p: (b, p, 0, 0)),
                  pl.BlockSpec((1, 1, nch, 16, CHUNK), lambda b, p: (b, p, 0, 0, 0))],
        out_specs=pl.BlockSpec((1, 2, T, dh), lambda b, p: (b, p, 0, 0)),
        out_shape=jax.ShapeDtypeStruct((B, H, T, dh), F32),
        scratch_shapes=[pltpu.VMEM((4, dh, dh), F32), pltpu.VMEM((4, 1, dh), F32),
                        pltpu.VMEM((4, 1, CHUNK), F32),
                        pltpu.VMEM((4, nch, dh, dh), BF16), pltpu.VMEM((4, nch, 16, dh), BF16),
                        pltpu.VMEM((4, nch, 8, CHUNK), F32)],
        compiler_params=_params(("parallel", "parallel")),
        name="mlstm",
    )(mq, mk, mvt, gc, gr)


def _ret_kernel(q_ref, kt_ref, v_ref, lg_ref, o_ref, st_ref, sts_ref, dec_ref, ib_ref, wb_ref, *, n_ctx):
    n_all = kt_ref.shape[2]
    dk = q_ref.shape[3]
    dv = v_ref.shape[3]
    st_ref[...] = jnp.zeros(st_ref.shape, F32)
    row = lax.broadcasted_iota(jnp.int32, (CHUNK, CHUNK), 0).astype(F32)
    col = lax.broadcasted_iota(jnp.int32, (CHUNK, CHUNK), 1).astype(F32)
    cds = []
    dec = None
    for d in range(2):
        lg = _log_sigmoid(lg_ref[0, d])[0:1]
        lgk = lg[:, 0:CHUNK]
        diff = row - col if d == 0 else col - row
        dd = jnp.where(diff >= 0, jnp.exp(lgk * jnp.maximum(diff, 0.0)), 0.0)
        dec = dd if dec is None else dec + dd
        pin = row if d == 0 else CHUNK - 1.0 - row
        ib_ref[d] = jnp.exp(lg * (jnp.concatenate([pin, pin], axis=1) + 1.0))
        pkey = col if d == 0 else CHUNK - 1.0 - col
        wb_ref[d] = jnp.exp(lgk * (CHUNK - 1.0 - pkey))
        cds.append(jnp.exp(lg * CHUNK))
    dec_ref[...] = dec

    def scan_step(i, carry):
        cb = _bwd_chunk(i, n_ctx, n_all)
        for d, c in ((0, i), (1, cb)):
            c0 = _chunk_start(c)
            st = st_ref[d]
            sts_ref[c, :, d * dv:(d + 1) * dv] = st.astype(BF16)
            kw = (kt_ref[0, 0, c].astype(F32) * wb_ref[d]).astype(BF16)
            st_ref[d] = cds[d] * st + _dot(kw, v_ref[0, 0, pl.ds(c0, CHUNK), :])
        return carry

    _chunk_loop(scan_step, n_ctx, n_all, 4)

    def out_step(c, carry):
        c0 = _chunk_start(c)
        q = q_ref[0, 0, pl.ds(c0, CHUNK), :]
        s = (_dot(q, kt_ref[0, 0, c]) * dec_ref[...]).astype(BF16)
        qst = _dot(q, sts_ref[c])
        o_ref[0, 0, pl.ds(c0, CHUNK), :] = (_dot(s, v_ref[0, 0, pl.ds(c0, CHUNK), :])
                                            + ib_ref[0] * qst[:, 0:dv] + ib_ref[1] * qst[:, dv:])
        return carry

    _chunk_loop(out_step, n_ctx, n_all, 4)


def _ret_call(rq, rkt, rv, lgb, n_ctx):
    B, H, T, dk = rq.shape
    dv = rv.shape[-1]
    nch = T // CHUNK
    return pl.pallas_call(
        functools.partial(_ret_kernel, n_ctx=n_ctx),
        grid=(B, H),
        in_specs=[pl.BlockSpec((1, 1, T, dk), lambda b, h: (b, h, 0, 0)),
                  pl.BlockSpec((1, 1, nch, dk, CHUNK), lambda b, h: (b, h, 0, 0, 0)),
                  pl.BlockSpec((1, 1, T, dv), lambda b, h: (b, h, 0, 0)),
                  pl.BlockSpec((1, 2, 8, dv), lambda b, h: (h, 0, 0, 0))],
        out_specs=pl.BlockSpec((1, 1, T, dv), lambda b, h: (b, h, 0, 0)),
        out_shape=jax.ShapeDtypeStruct((B, H, T, dv), F32),
        scratch_shapes=[pltpu.VMEM((2, dk, dv), F32), pltpu.VMEM((nch, dk, 2 * dv), BF16),
                        pltpu.VMEM((CHUNK, CHUNK), F32), pltpu.VMEM((2, CHUNK, dv), F32),
                        pltpu.VMEM((2, dk, CHUNK), F32)],
        compiler_params=_params(("parallel", "parallel")),
        name="retention",
    )(rq, rkt, rv, lgb)


def _attn_kernel(q_ref, k_ref, vt_ref, sink_ref, o_ref, *, n_ctx_tok, need_ctx):
    T = q_ref.shape[2]
    L = n_ctx_tok
    S = T - L
    nb = S // A_BLOCK
    nspan = 3
    span = nspan * A_BLOCK
    ncb = L // A_BLOCK
    cols = A_GROUP * A_BLOCK
    kc = k_ref[0, 0, 0:L, :]
    sink = sink_ref[0]
    qpos = lax.broadcasted_iota(jnp.int32, (1, cols), 1) % A_BLOCK
    kpos = lax.broadcasted_iota(jnp.int32, (span, 1), 0)

    def finish(blk, acc, m):
        den = acc[A_HEAD_DIM:A_HEAD_DIM + 1] + jnp.exp(sink - m)
        ot = (acc[0:A_HEAD_DIM] * (1.0 / den)).astype(BF16)
        for g in range(A_GROUP):
            o_ref[0, blk, g * A_HEAD_DIM:(g + 1) * A_HEAD_DIM, :] = ot[:, g * A_BLOCK:(g + 1) * A_BLOCK]

    def ctx_part(qs, p_ctx):
        acc = _dot(vt_ref[0, 0, 0], p_ctx[0:A_BLOCK])
        for j in range(1, ncb):
            acc = acc + _dot(vt_ref[0, 0, j], p_ctx[j * A_BLOCK:(j + 1) * A_BLOCK])
        return acc

    for cbk in range(ncb):
        if need_ctx:
            qs = q_ref[0, :, cbk * A_BLOCK:(cbk + 1) * A_BLOCK, :].reshape(cols, A_HEAD_DIM)
            s = _dot_nt(kc, qs)
            m = jnp.maximum(jnp.max(s, axis=0, keepdims=True), sink)
            finish(cbk, ctx_part(qs, jnp.exp(s - m).astype(BF16)), m)
        else:
            o_ref[0, cbk] = jnp.zeros(o_ref.shape[2:], BF16)

    def block(n, carry):
        q0 = pl.multiple_of(L + n * A_BLOCK, A_BLOCK)
        kb = jnp.clip(n - 1, 0, nb - nspan)
        k0 = pl.multiple_of(L + kb * A_BLOCK, A_BLOCK)
        qs = q_ref[0, :, pl.ds(q0, A_BLOCK), :].reshape(cols, A_HEAD_DIM)
        s_loc = _dot_nt(k_ref[0, 0, pl.ds(k0, span), :], qs)
        qi = n * A_BLOCK + qpos
        kj = kb * A_BLOCK + kpos
        s_loc = jnp.where(jnp.abs(qi - kj) <= A_WINDOW, s_loc, NEG_INF)
        s_ctx = _dot_nt(kc, qs)
        m = jnp.maximum(jnp.maximum(jnp.max(s_loc, axis=0, keepdims=True),
                                    jnp.max(s_ctx, axis=0, keepdims=True)), sink)
        p_loc = jnp.exp(s_loc - m).astype(BF16)
        acc = ctx_part(qs, jnp.exp(s_ctx - m).astype(BF16))
        for j in range(nspan):
            acc = acc + _dot(vt_ref[0, 0, ncb + kb + j], p_loc[j * A_BLOCK:(j + 1) * A_BLOCK])
        finish(ncb + n, acc, m)
        return carry

    lax.fori_loop(0, nb, block, 0, unroll=2)


def _attn_call(aq, ak, avt, sink_rows, n_ctx_tok, need_ctx):
    B, _, T, hd = aq.shape
    nblk = T // A_BLOCK
    cols = A_GROUP * A_BLOCK
    return pl.pallas_call(
        functools.partial(_attn_kernel, n_ctx_tok=n_ctx_tok, need_ctx=need_ctx),
        grid=(B, A_KV_HEADS),
        in_specs=[pl.BlockSpec((1, A_GROUP, T, hd), lambda b, h: (b, h, 0, 0)),
                  pl.BlockSpec((1, 1, T, hd), lambda b, h: (b, h, 0, 0)),
                  pl.BlockSpec((1, 1, nblk, LANES, A_BLOCK), lambda b, h: (b, h, 0, 0, 0)),
                  pl.BlockSpec((1, 1, cols), lambda b, h: (h, 0, 0))],
        out_specs=pl.BlockSpec((1, nblk, A_GROUP * hd, A_BLOCK), lambda b, h: (b, 0, h, 0)),
        out_shape=jax.ShapeDtypeStruct((B, nblk, A_WIDTH, A_BLOCK), BF16),
        compiler_params=_params(("parallel", "parallel")),
        name="window_attn",
    )(aq, ak, avt, sink_rows)


def _head_norm(y):
    mu = jnp.mean(y, axis=-1, keepdims=True)
    yc = y - mu
    var = jnp.mean(yc * yc, axis=-1, keepdims=True)
    return yc * lax.rsqrt(var + NORM_EPS)


def _post_kernel(x_ref, hb_ref, hm_ref, ya_ref, hr_ref, mod_ref, wg_ref, mnw_ref, rnw_ref,
                 wbm_ref, wba_ref, wbr_ref, wo_ref, o_ref):
    hb = hb_ref[0]
    off = 0
    hm = jnp.concatenate([_head_norm(hm_ref[0, h]) for h in range(M_HEADS)], axis=1) * mnw_ref[...]
    ym = (_sigmoid(_dot(hb, wg_ref[:, off:off + M_WIDTH])) * hm).astype(BF16)
    off += M_WIDTH
    hr = jnp.concatenate([_head_norm(hr_ref[0, h]) for h in range(R_HEADS)], axis=1) * rnw_ref[...]
    rg = _dot(hb, wg_ref[:, off:off + R_V_WIDTH])
    yr = (rg * _sigmoid(rg) * hr).astype(BF16)
    off += R_V_WIDTH
    z = _sigmoid(_dot(hb, wg_ref[:, off:off + D_MODEL])) * _dot(ym, wbm_ref[...])
    off += D_MODEL
    pa = jnp.concatenate([_dot_tn(ya_ref[0, c], wba_ref[...]) for c in range(TM // A_BLOCK)], axis=0)
    z = z + _sigmoid(_dot(hb, wg_ref[:, off:off + D_MODEL])) * pa
    off += D_MODEL
    z = z + _sigmoid(_dot(hb, wg_ref[:, off:off + D_MODEL])) * _dot(yr, wbr_ref[...])
    y = _dot(z.astype(BF16), wo_ref[...])
    o_ref[0] = x_ref[0] + mod_ref[0, 0, 2:3, :] * y


def _post_call(xc, hb, hm, ya, hr, modsel, wg, mnw, rnw, wbm, wba, wbr, wo, t_off):
    B, T, D = xc.shape
    nt = T // TM - t_off

    def tile(t, b):
        return (b, t + t_off, 0)

    def head_tile(t, b):
        return (b, 0, t + t_off, 0)

    return pl.pallas_call(
        _post_kernel,
        grid=(nt, B),
        in_specs=[pl.BlockSpec((1, TM, D), tile),
                  pl.BlockSpec((1, TM, D), tile),
                  pl.BlockSpec((1, M_HEADS, TM, M_HEAD_DIM), head_tile),
                  pl.BlockSpec((1, TM // A_BLOCK, A_WIDTH, A_BLOCK), lambda t, b: (b, t + t_off, 0, 0)),
                  pl.BlockSpec((1, R_HEADS, TM, R_V_DIM), head_tile),
                  pl.BlockSpec((1, 1, 6, D), lambda t, b: (b, jnp.minimum(t + t_off, 1), 0, 0)),
                  _const_spec(wg.shape), _const_spec(mnw.shape), _const_spec(rnw.shape),
                  _const_spec(wbm.shape), _const_spec(wba.shape), _const_spec(wbr.shape),
                  _const_spec(wo.shape)],
        out_specs=pl.BlockSpec((1, TM, D), lambda t, b: (b, t, 0)),
        out_shape=jax.ShapeDtypeStruct((B, nt * TM, D), F32),
        compiler_params=_params(("parallel", "parallel")),
        name="post_merge",
    )(xc, hb, hm, ya, hr, modsel, wg, mnw, rnw, wbm, wba, wbr, wo)


def _ffn_kernel(x_ref, xp_ref, xn_ref, mod_ref, nw_ref, wa_ref, wb_ref, cw_ref, cb_ref, wd_ref, fw_ref,
                o_ref, hext_ref, u_ref, *, has_ctx, final):
    t = pl.program_id(0)
    nt = pl.num_programs(0)
    first_lat = 1 if has_ctx else 0
    sh = mod_ref[0, 0, 3:4, :]
    sc = mod_ref[0, 0, 4:5, :]
    g2 = mod_ref[0, 0, 5:6, :]
    nw = nw_ref[...]
    x = x_ref[0]
    hp = _norm_mod(xp_ref[0], nw, sc, sh)
    hn = _norm_mod(xn_ref[0], nw, sc, sh)
    hp = jnp.where(t <= first_lat, 0.0, hp)
    hn = jnp.where((t == first_lat - 1) | (t == nt - 1), 0.0, hn)
    hext_ref[0:HALO, :] = hp.astype(BF16)
    hext_ref[HALO:HALO + TM, :] = _norm_mod(x, nw, sc, sh).astype(BF16)
    hext_ref[HALO + TM:, :] = hn.astype(BF16)
    ext = TM + 2 * HALO
    for j in range(FFN_DIM // FFN_NC):
        cs = slice(j * FFN_NC, (j + 1) * FFN_NC)
        a = _dot(hext_ref[...], wa_ref[:, cs])
        b = _dot(hext_ref[HALO:HALO + TM, :], wb_ref[:, cs])
        cw = cw_ref[:, cs]
        prev = pltpu.roll(a, 1, 0)[HALO:HALO + TM]
        nxt = pltpu.roll(a, ext - 1, 0)[HALO:HALO + TM]
        conv = prev * cw[0:1] + a[HALO:HALO + TM] * cw[1:2] + nxt * cw[2:3] + cb_ref[:, cs]
        u_ref[:, cs] = (conv * _sigmoid(conv) * b).astype(BF16)
    y = x + g2 * _dot(u_ref[...], wd_ref[...])
    if final:
        ms = jnp.mean(y * y, axis=-1, keepdims=True)
        y = y * lax.rsqrt(ms + NORM_EPS) * fw_ref[...]
    o_ref[0] = y


def _ffn_call(xs, modsel, nw, wa, wb, cw, cb, wd, fw, has_ctx, final):
    B, T, D = xs.shape
    nt = T // TM
    hb16 = TM // HALO
    first_lat = 1 if has_ctx else 0
    tile = pl.BlockSpec((1, TM, D), lambda t, b: (b, t, 0))
    return pl.pallas_call(
        functools.partial(_ffn_kernel, has_ctx=has_ctx, final=final),
        grid=(nt, B),
        in_specs=[tile,
                  pl.BlockSpec((1, HALO, D), lambda t, b: (b, jnp.maximum(t * hb16 - 1, 0), 0)),
                  pl.BlockSpec((1, HALO, D), lambda t, b: (b, jnp.minimum((t + 1) * hb16, T // HALO - 1), 0)),
                  pl.BlockSpec((1, 1, 6, D), lambda t, b: (b, jnp.minimum(t + 1 - first_lat, 1), 0, 0)),
                  _const_spec(nw.shape), _const_spec(wa.shape), _const_spec(wb.shape),
                  _const_spec(cw.shape), _const_spec(cb.shape), _const_spec(wd.shape),
                  _const_spec(fw.shape)],
        out_specs=tile,
        out_shape=jax.ShapeDtypeStruct((B, T, D), F32),
        scratch_shapes=[pltpu.VMEM((TM + 2 * HALO, D), BF16), pltpu.VMEM((TM, FFN_DIM), BF16)],
        compiler_params=_params(("parallel", "parallel")),
        name="conv_ffn",
    )(xs, xs, xs, modsel, nw, wa, wb, cw, cb, wd, fw)


def _split_cols(w):
    out = []
    acc = 0
    for s in IN_SPLITS:
        out.append(w[:, acc:acc + s])
        acc += s
    return out


def _rope_tables(L, S):
    T = L + S
    j = jnp.arange(LANES)
    jj = j % 32
    inv_a = ROPE_BASE ** (-(jj % 16).astype(F32) / 16.0)
    t = jnp.arange(S)
    pos = jnp.where(((j % A_HEAD_DIM) // 32 == 0)[None, :], (t // GRID_W)[:, None], (t % GRID_W)[:, None])
    ang = pos.astype(F32) * inv_a[None, :]
    sign = jnp.where(jj < 16, -1.0, 1.0)[None, :]
    ca = jnp.concatenate([jnp.ones((L, LANES), F32), jnp.cos(ang)], axis=0)
    sa = jnp.concatenate([jnp.zeros((L, LANES), F32), jnp.sin(ang) * sign], axis=0)
    half = R_QK_DIM // 2
    inv_r = ROPE_BASE ** (-(j % half).astype(F32) / half)
    angr = jnp.arange(T).astype(F32)[:, None] * inv_r[None, :]
    signr = jnp.where(j < half, -1.0, 1.0)[None, :]
    return ca, sa, jnp.cos(angr), jnp.sin(angr) * signr


def kernel(x, c, ctx, c_ctx, mod_w, mod_b, norm1_w, norm2_w, w_in, m_gate_bias, m_conv_w, m_norm_w,
           a_sink, ret_logit, ret_norm_w, w_br_m, w_br_a, w_br_r, w_out, ffn_up, ffn_conv_w, ffn_conv_b,
           ffn_down, final_norm_w):
    B, S, D = x.shape
    L = ctx.shape[1]
    depth = mod_w.shape[0]
    assert D == D_MODEL and L == TM and S % TM == 0 and S >= 3 * A_BLOCK
    T = L + S
    n_ctx = L // CHUNK

    rows = -(-(B + 1) // 8) * 8
    cpad = jnp.zeros((rows, D), F32).at[:B].set(c).at[B].set(c_ctx)
    mods = _modulation(cpad, mod_w, mod_b)
    tabs = _rope_tables(L, S)
    xc = jnp.concatenate([ctx, x], axis=1)

    gperm = jnp.arange(4 * M_HEADS).reshape(4, M_HEADS).T.reshape(-1)

    for l in range(depth):
        last = l == depth - 1
        lat = mods[l, :B].reshape(B, 6, D)
        cm = jnp.broadcast_to(mods[l, B].reshape(1, 6, D), (B, 6, D))
        modsel = jnp.stack([cm, lat], axis=1)

        (w_mq, w_mk, w_mv, w_mo, w_mg, w_aq, w_ak, w_av,
         w_rq, w_rk, w_rv, w_rg, w_gm, w_ga, w_gr) = _split_cols(w_in[l])
        wqk = jnp.concatenate([w_mq, w_mk], axis=1).astype(BF16)
        wr = jnp.concatenate([w_mv, w_aq, w_ak, w_av, w_rq, w_rk, w_rv], axis=1).astype(BF16)
        wg_cols = w_mg[:, gperm]
        wg = jnp.pad(wg_cols, ((0, 0), (0, LANES - 4 * M_HEADS))).astype(BF16)
        wgt = wg_cols.T.astype(BF16)
        gbias = m_gate_bias[l][gperm]
        gb = jnp.pad(gbias, (0, LANES - 4 * M_HEADS)).reshape(1, LANES)
        gbt = jnp.broadcast_to(gbias.reshape(-1, 1), (4 * M_HEADS, LANES))

        (hb, mq, mk, mv, gc, gr, aq, ak, av, rq, rk, rv) = _pre_call(
            xc, modsel, norm1_w[l].reshape(1, D), wqk, m_conv_w[l], wr, wg, wgt, gb, gbt, tabs)

        hm = _mlstm_call(mq, mk, mv, gc, gr, n_ctx)
        sink_rows = jnp.repeat(a_sink[l].reshape(A_KV_HEADS, A_GROUP), A_BLOCK, axis=1)[:, None, :]
        ya = _attn_call(aq, ak, av, sink_rows.astype(F32), L, not last)
        lgb = jnp.broadcast_to(ret_logit[l].T[:, :, None, None], (R_HEADS, 2, 8, R_V_DIM)).astype(F32)
        hr = _ret_call(rq, rk, rv, lgb, n_ctx)

        t_off = 1 if last else 0
        wgates = jnp.concatenate([w_mo, w_rg, w_gm, w_ga, w_gr], axis=1).astype(BF16)
        x1 = _post_call(xc, hb, hm, ya, hr, modsel, wgates, m_norm_w[l].reshape(1, -1),
                        ret_norm_w[l].reshape(1, -1), w_br_m[l].astype(BF16), w_br_a[l].astype(BF16),
                        w_br_r[l].astype(BF16), w_out[l].astype(BF16), t_off)
        xc = _ffn_call(x1, modsel, norm2_w[l].reshape(1, D), ffn_up[l][:, :FFN_DIM].astype(BF16),
                       ffn_up[l][:, FFN_DIM:].astype(BF16), ffn_conv_w[l], ffn_conv_b[l].reshape(1, -1),
                       ffn_down[l].astype(BF16), final_norm_w.reshape(1, D), not last, last)
    return xc
```

```python
import functools

import jax
import jax.numpy as jnp
from jax import lax
from jax.experimental import pallas as pl
from jax.experimental.pallas import tpu as pltpu

F32 = jnp.float32
BF16 = jnp.bfloat16

D_MODEL = 1024
GRID_W = 64
NORM_EPS = 1e-6
ROPE_BASE = 10000.0
NEG_INF = -1e30

M_HEADS = 4
M_HEAD_DIM = 128
M_WIDTH = M_HEADS * M_HEAD_DIM
A_HEADS = 8
A_KV_HEADS = 2
A_GROUP = A_HEADS // A_KV_HEADS
A_HEAD_DIM = 64
A_WIDTH = A_HEADS * A_HEAD_DIM
A_KV_WIDTH = A_KV_HEADS * A_HEAD_DIM
A_WINDOW = 128
A_BLOCK = 128
R_HEADS = 4
R_QK_DIM = 128
R_V_DIM = 256
R_QK_WIDTH = R_HEADS * R_QK_DIM
R_V_WIDTH = R_HEADS * R_V_DIM
FFN_DIM = 2816

IN_SPLITS = (M_WIDTH, M_WIDTH, M_WIDTH, M_WIDTH, 4 * M_HEADS,
             A_WIDTH, A_KV_WIDTH, A_KV_WIDTH,
             R_QK_WIDTH, R_QK_WIDTH, R_V_WIDTH, R_V_WIDTH,
             D_MODEL, D_MODEL, D_MODEL)

TM = 256
HALO = 8
CHUNK = 128
FFN_NC = 256
LANES = 128
VMEM_LIMIT = 56 * 1024 * 1024


def _sigmoid(x):
    return 1.0 / (1.0 + jnp.exp(-x))


def _log_sigmoid(x):
    return jnp.minimum(x, 0.0) - jnp.log1p(jnp.exp(-jnp.abs(x)))


def _dot(a, b):
    return jnp.dot(a, b, preferred_element_type=F32)


def _dot_nt(a, b):
    return lax.dot_general(a, b, (((1,), (1,)), ((), ())), preferred_element_type=F32)


def _dot_tn(a, b):
    return lax.dot_general(a, b, (((0,), (0,)), ((), ())), preferred_element_type=F32)


def _const_spec(shape):
    nd = len(shape)
    return pl.BlockSpec(shape, lambda *_: (0,) * nd, pipeline_mode=pl.Buffered(1))


def _params(sem):
    return pltpu.CompilerParams(dimension_semantics=sem, vmem_limit_bytes=VMEM_LIMIT)


def _mod_kernel(c_ref, w_ref, b_ref, o_ref):
    c = c_ref[...]
    s = (c * _sigmoid(c)).astype(BF16)
    o_ref[0] = _dot(s, w_ref[0].astype(BF16)) + b_ref[0]


def _modulation(cpad, mod_w, mod_b):
    depth, d, n = mod_w.shape
    rows = cpad.shape[0]
    bn = 1024
    return pl.pallas_call(
        _mod_kernel,
        grid=(depth, n // bn),
        in_specs=[pl.BlockSpec((rows, d), lambda l, j: (0, 0)),
                  pl.BlockSpec((1, d, bn), lambda l, j: (l, 0, j)),
                  pl.BlockSpec((1, 1, bn), lambda l, j: (l, 0, j))],
        out_specs=pl.BlockSpec((1, rows, bn), lambda l, j: (l, 0, j)),
        out_shape=jax.ShapeDtypeStruct((depth, rows, n), F32),
        compiler_params=_params(("parallel", "parallel")),
        name="modulation",
    )(cpad, mod_w, mod_b.reshape(depth, 1, n))


def _norm_mod(xv, nw, sc, sh):
    ms = jnp.mean(xv * xv, axis=-1, keepdims=True)
    return (xv * lax.rsqrt(ms + NORM_EPS) * nw) * (1.0 + sc) + sh


def _seg_cumsum(x, axis, reverse):
    n = x.shape[axis]
    idx = lax.broadcasted_iota(jnp.int32, x.shape, axis)
    s = 1
    while s < n:
        if reverse:
            shifted = pltpu.roll(x, n - s, axis)
            x = x + jnp.where(idx < n - s, shifted, 0.0)
        else:
            shifted = pltpu.roll(x, s, axis)
            x = x + jnp.where(idx >= s, shifted, 0.0)
        s *= 2
    return x


def _gate_transform(raw, gate_axis):
    tok_axis = 1 - gate_axis
    k = lax.broadcasted_iota(jnp.int32, raw.shape, gate_axis) % 4
    lf = _log_sigmoid(raw)
    cum_f = _seg_cumsum(jnp.where(k == 1, lf, 0.0), tok_axis, False)
    cum_b = _seg_cumsum(jnp.where(k == 3, lf, 0.0), tok_axis, True)
    bsel = jnp.where(k == 1, cum_f, jnp.where(k == 3, cum_b, 0.0))
    n = raw.shape[gate_axis]
    bnext = pltpu.roll(bsel, n - 1, gate_axis)
    nt = raw.shape[tok_axis]
    tot_f = lax.slice_in_dim(cum_f, nt - 1, nt, axis=tok_axis)
    tot_b = lax.slice_in_dim(cum_b, 0, 1, axis=tok_axis)
    total = jnp.broadcast_to(tot_f + tot_b, raw.shape)
    return jnp.where(k % 2 == 1, bsel, raw - bnext), total


def _pre_kernel(x_ref, xp_ref, xn_ref, mod_ref, nw_ref, wqk_ref, cw_ref, wr_ref, wgt_ref,
                gbt_ref, ca_ref, sa_ref, cr_ref, sr_ref,
                hb_ref, mq_ref, mk_ref, mv_ref, gr_ref, aq_ref, ak_ref, av_ref,
                rq_ref, rk_ref, rv_ref, hext_ref):
    t = pl.program_id(0)
    nt = pl.num_programs(0)
    sh = mod_ref[0, 0, 0:1, :]
    sc = mod_ref[0, 0, 1:2, :]
    nw = nw_ref[...]
    hb = _norm_mod(x_ref[0], nw, sc, sh).astype(BF16)
    hb_ref[0] = hb

    grow = _dot_nt(wgt_ref[...], hb) + gbt_ref[:, 0:1]
    for c in range(TM // CHUNK):
        gr, tot = _gate_transform(grow[:, c * CHUNK:(c + 1) * CHUNK], 0)
        aend = gr + pltpu.roll(tot, 4 * M_HEADS - 1, 0)
        amax = jnp.broadcast_to(jnp.max(aend, axis=1, keepdims=True), aend.shape)
        for h in range(M_HEADS):
            hp_, o0 = h // 2, 8 * (h % 2)
            gr_ref[0, hp_, c, o0:o0 + 4, :] = gr[4 * h:4 * h + 4]
            for d in range(2):
                gr_ref[0, hp_, c, o0 + 4 + d:o0 + 5 + d, :] = tot[4 * h + 2 * d + 1:4 * h + 2 * d + 2]
                gr_ref[0, hp_, c, o0 + 6 + d:o0 + 7 + d, :] = amax[4 * h + 2 * d:4 * h + 2 * d + 1]

    hp = _norm_mod(xp_ref[0], nw, sc, sh)
    hn = _norm_mod(xn_ref[0], nw, sc, sh)
    hp = jnp.where(t <= 1, 0.0, hp)
    hn = jnp.where((t == 0) | (t == nt - 1), 0.0, hn)
    hext_ref[0:TM, :] = hb
    hext_ref[TM:, :] = jnp.concatenate([hn, hp], axis=0).astype(BF16)
    hext = hext_ref[...]
    ext = TM + 2 * HALO

    for j, (dst, scale) in enumerate(((mq_ref, 1.0), (mk_ref, M_HEAD_DIM ** -0.5))):
        p = _dot(hext, wqk_ref[:, j * M_WIDTH:(j + 1) * M_WIDTH])
        cw = cw_ref[:, j * M_WIDTH:(j + 1) * M_WIDTH]
        prev = pltpu.roll(p, 1, 0)[0:TM]
        nxt = pltpu.roll(p, ext - 1, 0)[0:TM]
        conv = prev * cw[0:1] + p[0:TM] * cw[1:2] + nxt * cw[2:3]
        act = conv * _sigmoid(conv) * scale
        for h in range(M_HEADS):
            dst[0, h] = act[:, h * M_HEAD_DIM:(h + 1) * M_HEAD_DIM].astype(BF16)

    off = 0
    p = _dot(hb, wr_ref[:, off:off + M_WIDTH])
    off += M_WIDTH
    for h in range(M_HEADS):
        for c in range(TM // CHUNK):
            blk = p[c * CHUNK:(c + 1) * CHUNK, h * M_HEAD_DIM:(h + 1) * M_HEAD_DIM]
            mv_ref[0, h, c] = blk.T.astype(BF16)

    lane = lax.broadcasted_iota(jnp.int32, (TM, LANES), 1)
    first16 = (lane % 32) < 16
    ca = ca_ref[...]
    sa = sa_ref[...]

    def rope_a(xs):
        partner = jnp.where(first16, pltpu.roll(xs, LANES - 16, 1), pltpu.roll(xs, 16, 1))
        return xs * ca + partner * sa

    p = _dot(hb, wr_ref[:, off:off + A_WIDTH])
    off += A_WIDTH
    for s in range(A_WIDTH // LANES):
        r = (rope_a(p[:, s * LANES:(s + 1) * LANES]) * A_HEAD_DIM ** -0.5).astype(BF16)
        aq_ref[0, 2 * s] = r[:, 0:A_HEAD_DIM]
        aq_ref[0, 2 * s + 1] = r[:, A_HEAD_DIM:]
    p = _dot(hb, wr_ref[:, off:off + 2 * A_KV_WIDTH])
    off += 2 * A_KV_WIDTH
    r = rope_a(p[:, 0:LANES]).astype(BF16)
    for h in range(A_KV_HEADS):
        ak_ref[0, h] = r[:, h * A_HEAD_DIM:(h + 1) * A_HEAD_DIM]
    pv = p[:, LANES:]
    for h in range(A_KV_HEADS):
        vh = pv if h == 0 else pltpu.roll(pv, A_HEAD_DIM, 1)
        ve = jnp.where(lane < A_HEAD_DIM, vh, 1.0)
        for c in range(TM // A_BLOCK):
            av_ref[0, h, c] = ve[c * A_BLOCK:(c + 1) * A_BLOCK].T.astype(BF16)

    cr = cr_ref[...]
    sr = sr_ref[...]
    for is_k in (False, True):
        p = _dot(hb, wr_ref[:, off:off + R_QK_WIDTH])
        off += R_QK_WIDTH
        for h in range(R_HEADS):
            xs = p[:, h * R_QK_DIM:(h + 1) * R_QK_DIM]
            rot = xs * cr + pltpu.roll(xs, R_QK_DIM // 2, 1) * sr
            if is_k:
                rot = rot * R_QK_DIM ** -0.5
                for c in range(TM // CHUNK):
                    rk_ref[0, h, c] = rot[c * CHUNK:(c + 1) * CHUNK].T.astype(BF16)
            else:
                rq_ref[0, h] = rot.astype(BF16)
    p = _dot(hb, wr_ref[:, off:off + R_V_WIDTH])
    for h in range(R_HEADS):
        rv_ref[0, h] = p[:, h * R_V_DIM:(h + 1) * R_V_DIM].astype(BF16)


def _pre_call(xc, modsel, nw, wqk, cw, wr, wgt, gbt, tabs):
    B, T, D = xc.shape
    nt = T // TM
    hb16 = TM // HALO
    nch = T // CHUNK

    def tile(t, b):
        return (b, t, 0)

    def head_tile(t, b):
        return (b, 0, t, 0)

    in_specs = [
        pl.BlockSpec((1, TM, D), tile),
        pl.BlockSpec((1, HALO, D), lambda t, b: (b, jnp.maximum(t * hb16 - 1, 0), 0)),
        pl.BlockSpec((1, HALO, D), lambda t, b: (b, jnp.minimum((t + 1) * hb16, T // HALO - 1), 0)),
        pl.BlockSpec((1, 1, 6, D), lambda t, b: (b, jnp.minimum(t, 1), 0, 0)),
        _const_spec(nw.shape), _const_spec(wqk.shape), _const_spec(cw.shape), _const_spec(wr.shape),
        _const_spec(wgt.shape), _const_spec(gbt.shape),
    ] + [pl.BlockSpec((TM, LANES), lambda t, b: (t, 0)) for _ in range(4)]
    out_shape = [
        jax.ShapeDtypeStruct((B, T, D), BF16),
        jax.ShapeDtypeStruct((B, M_HEADS, T, M_HEAD_DIM), BF16),
        jax.ShapeDtypeStruct((B, M_HEADS, T, M_HEAD_DIM), BF16),
        jax.ShapeDtypeStruct((B, M_HEADS, nch, M_HEAD_DIM, CHUNK), BF16),
        jax.ShapeDtypeStruct((B, 2, nch, 16, CHUNK), F32),
        jax.ShapeDtypeStruct((B, A_HEADS, T, A_HEAD_DIM), BF16),
        jax.ShapeDtypeStruct((B, A_KV_HEADS, T, A_HEAD_DIM), BF16),
        jax.ShapeDtypeStruct((B, A_KV_HEADS, T // A_BLOCK, LANES, A_BLOCK), BF16),
        jax.ShapeDtypeStruct((B, R_HEADS, T, R_QK_DIM), BF16),
        jax.ShapeDtypeStruct((B, R_HEADS, nch, R_QK_DIM, CHUNK), BF16),
        jax.ShapeDtypeStruct((B, R_HEADS, T, R_V_DIM), BF16),
    ]
    out_specs = [
        pl.BlockSpec((1, TM, D), tile),
        pl.BlockSpec((1, M_HEADS, TM, M_HEAD_DIM), head_tile),
        pl.BlockSpec((1, M_HEADS, TM, M_HEAD_DIM), head_tile),
        pl.BlockSpec((1, M_HEADS, TM // CHUNK, M_HEAD_DIM, CHUNK), lambda t, b: (b, 0, t, 0, 0)),
        pl.BlockSpec((1, 2, TM // CHUNK, 16, CHUNK), lambda t, b: (b, 0, t, 0, 0)),
        pl.BlockSpec((1, A_HEADS, TM, A_HEAD_DIM), head_tile),
        pl.BlockSpec((1, A_KV_HEADS, TM, A_HEAD_DIM), head_tile),
        pl.BlockSpec((1, A_KV_HEADS, TM // A_BLOCK, LANES, A_BLOCK), lambda t, b: (b, 0, t, 0, 0)),
        pl.BlockSpec((1, R_HEADS, TM, R_QK_DIM), head_tile),
        pl.BlockSpec((1, R_HEADS, TM // CHUNK, R_QK_DIM, CHUNK), lambda t, b: (b, 0, t, 0, 0)),
        pl.BlockSpec((1, R_HEADS, TM, R_V_DIM), head_tile),
    ]
    return pl.pallas_call(
        _pre_kernel,
        grid=(nt, B),
        in_specs=in_specs,
        out_specs=out_specs,
        out_shape=out_shape,
        scratch_shapes=[pltpu.VMEM((TM + 2 * HALO, D), BF16)],
        compiler_params=_params(("parallel", "parallel")),
        name="pre_proj",
    )(xc, xc, xc, modsel, nw, wqk, cw, wr, wgt, gbt, *tabs)


def _bwd_chunk(i, n_ctx, n_all):
    if isinstance(i, int):
        return n_ctx - 1 - i if i < n_ctx else n_all - 1 + n_ctx - i
    return n_all - 1 + n_ctx - i


def _chunk_start(c):
    return c * CHUNK if isinstance(c, int) else pl.multiple_of(c * CHUNK, CHUNK)


def _chunk_loop(body, n_ctx, n_all, unroll):
    for i in range(n_ctx):
        body(i, 0)
    n_lat = n_all - n_ctx
    while n_lat % unroll:
        unroll //= 2
    lax.fori_loop(n_ctx, n_all, body, 0, unroll=unroll)


def _mlstm_kernel(q_ref, k_ref, vt_ref, gr_ref, o_ref, c_ref, n_ref, m_ref,
                  cs_ref, ns_ref, ms_ref, *, n_ctx):
    n_all = gr_ref.shape[2]
    dh = q_ref.shape[3]
    c_ref[...] = jnp.zeros(c_ref.shape, F32)
    n_ref[...] = jnp.zeros(n_ref.shape, F32)
    m_ref[...] = jnp.zeros(m_ref.shape, F32)

    def scan_unit(c, hh, d):
        u = 2 * hh + d
        c0 = _chunk_start(c)
        k = k_ref[0, hh, pl.ds(c0, CHUNK), :]
        vt = vt_ref[0, hh, c]
        rows = gr_ref[0, 0, c, 8 * hh:8 * hh + 8, :]
        tot = rows[4 + d:5 + d]
        aend = rows[2 * d:2 * d + 1] + tot
        amax = rows[6 + d:7 + d]
        cst = c_ref[u]
        nv = n_ref[u]
        m = m_ref[u]
        cs_ref[u, c] = cst.astype(BF16)
        ns_ref[u, c] = jnp.broadcast_to(nv, (16, dh)).astype(BF16)
        ms_ref[u, c] = jnp.broadcast_to(m, (8, CHUNK))
        m_new = jnp.maximum(tot + m, amax)
        w = jnp.exp(aend - m_new)
        dec = jnp.exp(tot + m - m_new)
        vw = (vt.astype(F32) * w).astype(BF16)
        c_ref[u] = dec * cst + _dot(vw, k)
        wk = _dot(jnp.broadcast_to(w, (16, CHUNK)).astype(BF16), k)
        n_ref[u] = dec * nv + wk[0:1]
        m_ref[u] = m_new

    def scan_step(i, carry):
        cb = _bwd_chunk(i, n_ctx, n_all)
        for hh in range(2):
            scan_unit(i, hh, 0)
            scan_unit(cb, hh, 1)
        return carry

    _chunk_loop(scan_step, n_ctx, n_all, 4)

    row = lax.broadcasted_iota(jnp.int32, (CHUNK, CHUNK), 0)
    col = lax.broadcasted_iota(jnp.int32, (CHUNK, CHUNK), 1)
    tri = (row <= col, row >= col)

    def out_step(c, carry):
        c0 = _chunk_start(c)
        gcv = jnp.concatenate([gr_ref[0, 0, c], jnp.zeros((CHUNK - 16, CHUNK), F32)], axis=0).T
        for hh in range(2):
            q = q_ref[0, hh, pl.ds(c0, CHUNK), :]
            k = k_ref[0, hh, pl.ds(c0, CHUNK), :]
            vt = vt_ref[0, hh, c]
            rows = gr_ref[0, 0, c, 8 * hh:8 * hh + 8, :]
            st = _dot_nt(k, q)
            acc = None
            for d in range(2):
                u = 2 * hh + d
                a0c = gcv[:, 8 * hh + 2 * d:8 * hh + 2 * d + 1]
                b = rows[2 * d + 1:2 * d + 2]
                m = ms_ref[u, c, 0:1, :]
                dlog = jnp.where(tri[d], a0c + b, NEG_INF)
                mj = jnp.maximum(b + m, jnp.max(dlog, axis=0, keepdims=True))
                sd = st * jnp.exp(dlog - mj)
                iw = jnp.exp(b + m - mj)
                nq = _dot_nt(ns_ref[u, c], q)[0:1]
                den = iw * nq + jnp.sum(sd, axis=0, keepdims=True)
                inv = 1.0 / jnp.maximum(jnp.abs(den), jnp.exp(-mj))
                ht = (_dot(vt, sd.astype(BF16)) + iw * _dot_nt(cs_ref[u, c], q)) * inv
                acc = ht if acc is None else acc + ht
            o_ref[0, hh, pl.ds(c0, CHUNK), :] = acc.T
        return carry

    _chunk_loop(out_step, n_ctx, n_all, 2)


def _mlstm_call(mq, mk, mvt, gr, n_ctx):
    B, H, T, dh = mq.shape
    nch = T // CHUNK
    qspec = pl.BlockSpec((1, 2, T, dh), lambda b, p: (b, p, 0, 0))
    return pl.pallas_call(
        functools.partial(_mlstm_kernel, n_ctx=n_ctx),
        grid=(B, H // 2),
        in_specs=[qspec, qspec,
                  pl.BlockSpec((1, 2, nch, dh, CHUNK), lambda b, p: (b, p, 0, 0, 0)),
                  pl.BlockSpec((1, 1, nch, 16, CHUNK), lambda b, p: (b, p, 0, 0, 0))],
        out_specs=pl.BlockSpec((1, 2, T, dh), lambda b, p: (b, p, 0, 0)),
        out_shape=jax.ShapeDtypeStruct((B, H, T, dh), F32),
        scratch_shapes=[pltpu.VMEM((4, dh, dh), F32), pltpu.VMEM((4, 1, dh), F32),
                        pltpu.VMEM((4, 1, CHUNK), F32),
                        pltpu.VMEM((4, nch, dh, dh), BF16), pltpu.VMEM((4, nch, 16, dh), BF16),
                        pltpu.VMEM((4, nch, 8, CHUNK), F32)],
        compiler_params=_params(("parallel", "parallel")),
        name="mlstm",
    )(mq, mk, mvt, gr)


def _ret_kernel(q_ref, kt_ref, v_ref, lg_ref, o_ref, st_ref, sts_ref, dec_ref, ib_ref, wb_ref, *, n_ctx):
    n_all = kt_ref.shape[2]
    dk = q_ref.shape[3]
    dv = v_ref.shape[3]
    st_ref[...] = jnp.zeros(st_ref.shape, F32)
    row = lax.broadcasted_iota(jnp.int32, (CHUNK, CHUNK), 0).astype(F32)
    col = lax.broadcasted_iota(jnp.int32, (CHUNK, CHUNK), 1).astype(F32)
    cds = []
    dec = None
    for d in range(2):
        lg = _log_sigmoid(lg_ref[0, d])[0:1]
        lgk = lg[:, 0:CHUNK]
        diff = row - col if d == 0 else col - row
        dd = jnp.where(diff >= 0, jnp.exp(lgk * jnp.maximum(diff, 0.0)), 0.0)
        dec = dd if dec is None else dec + dd
        pin = row if d == 0 else CHUNK - 1.0 - row
        ib_ref[d] = jnp.exp(lg * (jnp.concatenate([pin, pin], axis=1) + 1.0))
        pkey = col if d == 0 else CHUNK - 1.0 - col
        wb_ref[d] = jnp.exp(lgk * (CHUNK - 1.0 - pkey))
        cds.append(jnp.exp(lg * CHUNK))
    dec_ref[...] = dec

    def scan_step(i, carry):
        cb = _bwd_chunk(i, n_ctx, n_all)
        for d, c in ((0, i), (1, cb)):
            c0 = _chunk_start(c)
            st = st_ref[d]
            sts_ref[c, :, d * dv:(d + 1) * dv] = st.astype(BF16)
            kw = (kt_ref[0, 0, c].astype(F32) * wb_ref[d]).astype(BF16)
            st_ref[d] = cds[d] * st + _dot(kw, v_ref[0, 0, pl.ds(c0, CHUNK), :])
        return carry

    _chunk_loop(scan_step, n_ctx, n_all, 4)

    def out_step(c, carry):
        c0 = _chunk_start(c)
        q = q_ref[0, 0, pl.ds(c0, CHUNK), :]
        s = (_dot(q, kt_ref[0, 0, c]) * dec_ref[...]).astype(BF16)
        qst = _dot(q, sts_ref[c])
        o_ref[0, 0, pl.ds(c0, CHUNK), :] = (_dot(s, v_ref[0, 0, pl.ds(c0, CHUNK), :])
                                            + ib_ref[0] * qst[:, 0:dv] + ib_ref[1] * qst[:, dv:])
        return carry

    _chunk_loop(out_step, n_ctx, n_all, 4)


def _ret_call(rq, rkt, rv, lgb, n_ctx):
    B, H, T, dk = rq.shape
    dv = rv.shape[-1]
    nch = T // CHUNK
    return pl.pallas_call(
        functools.partial(_ret_kernel, n_ctx=n_ctx),
        grid=(B, H),
        in_specs=[pl.BlockSpec((1, 1, T, dk), lambda b, h: (b, h, 0, 0)),
                  pl.BlockSpec((1, 1, nch, dk, CHUNK), lambda b, h: (b, h, 0, 0, 0)),
                  pl.BlockSpec((1, 1, T, dv), lambda b, h: (b, h, 0, 0)),
                  pl.BlockSpec((1, 2, 8, dv), lambda b, h: (h, 0, 0, 0))],
        out_specs=pl.BlockSpec((1, 1, T, dv), lambda b, h: (b, h, 0, 0)),
        out_shape=jax.ShapeDtypeStruct((B, H, T, dv), F32),
        scratch_shapes=[pltpu.VMEM((2, dk, dv), F32), pltpu.VMEM((nch, dk, 2 * dv), BF16),
                        pltpu.VMEM((CHUNK, CHUNK), F32), pltpu.VMEM((2, CHUNK, dv), F32),
                        pltpu.VMEM((2, dk, CHUNK), F32)],
        compiler_params=_params(("parallel", "parallel")),
        name="retention",
    )(rq, rkt, rv, lgb)


def _attn_kernel(q_ref, k_ref, vt_ref, sink_ref, o_ref, *, n_ctx_tok, need_ctx):
    T = q_ref.shape[2]
    L = n_ctx_tok
    S = T - L
    nb = S // A_BLOCK
    nspan = 3
    span = nspan * A_BLOCK
    ncb = L // A_BLOCK
    cols = A_GROUP * A_BLOCK
    kc = k_ref[0, 0, 0:L, :]
    sink = sink_ref[0]
    qpos = lax.broadcasted_iota(jnp.int32, (1, cols), 1) % A_BLOCK
    kpos = lax.broadcasted_iota(jnp.int32, (span, 1), 0)

    def finish(blk, acc, m):
        den = acc[A_HEAD_DIM:A_HEAD_DIM + 1] + jnp.exp(sink - m)
        ot = (acc[0:A_HEAD_DIM] * (1.0 / den)).astype(BF16)
        for g in range(A_GROUP):
            o_ref[0, blk, g * A_HEAD_DIM:(g + 1) * A_HEAD_DIM, :] = ot[:, g * A_BLOCK:(g + 1) * A_BLOCK]

    def ctx_part(qs, p_ctx):
        acc = _dot(vt_ref[0, 0, 0], p_ctx[0:A_BLOCK])
        for j in range(1, ncb):
            acc = acc + _dot(vt_ref[0, 0, j], p_ctx[j * A_BLOCK:(j + 1) * A_BLOCK])
        return acc

    for cbk in range(ncb):
        if need_ctx:
            qs = q_ref[0, :, cbk * A_BLOCK:(cbk + 1) * A_BLOCK, :].reshape(cols, A_HEAD_DIM)
            s = _dot_nt(kc, qs)
            m = jnp.maximum(jnp.max(s, axis=0, keepdims=True), sink)
            finish(cbk, ctx_part(qs, jnp.exp(s - m).astype(BF16)), m)
        else:
            o_ref[0, cbk] = jnp.zeros(o_ref.shape[2:], BF16)

    def block(n, carry):
        q0 = pl.multiple_of(L + n * A_BLOCK, A_BLOCK)
        kb = jnp.clip(n - 1, 0, nb - nspan)
        k0 = pl.multiple_of(L + kb * A_BLOCK, A_BLOCK)
        qs = q_ref[0, :, pl.ds(q0, A_BLOCK), :].reshape(cols, A_HEAD_DIM)
        s_loc = _dot_nt(k_ref[0, 0, pl.ds(k0, span), :], qs)
        qi = n * A_BLOCK + qpos
        kj = kb * A_BLOCK + kpos
        s_loc = jnp.where(jnp.abs(qi - kj) <= A_WINDOW, s_loc, NEG_INF)
        s_ctx = _dot_nt(kc, qs)
        m = jnp.maximum(jnp.maximum(jnp.max(s_loc, axis=0, keepdims=True),
                                    jnp.max(s_ctx, axis=0, keepdims=True)), sink)
        p_loc = jnp.exp(s_loc - m).astype(BF16)
        acc = ctx_part(qs, jnp.exp(s_ctx - m).astype(BF16))
        for j in range(nspan):
            acc = acc + _dot(vt_ref[0, 0, ncb + kb + j], p_loc[j * A_BLOCK:(j + 1) * A_BLOCK])
        finish(ncb + n, acc, m)
        return carry

    lax.fori_loop(0, nb, block, 0, unroll=2)


def _attn_call(aq, ak, avt, sink_rows, n_ctx_tok, need_ctx):
    B, _, T, hd = aq.shape
    nblk = T // A_BLOCK
    cols = A_GROUP * A_BLOCK
    return pl.pallas_call(
        functools.partial(_attn_kernel, n_ctx_tok=n_ctx_tok, need_ctx=need_ctx),
        grid=(B, A_KV_HEADS),
        in_specs=[pl.BlockSpec((1, A_GROUP, T, hd), lambda b, h: (b, h, 0, 0)),
                  pl.BlockSpec((1, 1, T, hd), lambda b, h: (b, h, 0, 0)),
                  pl.BlockSpec((1, 1, nblk, LANES, A_BLOCK), lambda b, h: (b, h, 0, 0, 0)),
                  pl.BlockSpec((1, 1, cols), lambda b, h: (h, 0, 0))],
        out_specs=pl.BlockSpec((1, nblk, A_GROUP * hd, A_BLOCK), lambda b, h: (b, 0, h, 0)),
        out_shape=jax.ShapeDtypeStruct((B, nblk, A_WIDTH, A_BLOCK), BF16),
        compiler_params=_params(("parallel", "parallel")),
        name="window_attn",
    )(aq, ak, avt, sink_rows)


def _head_norm(y):
    mu = jnp.mean(y, axis=-1, keepdims=True)
    yc = y - mu
    var = jnp.mean(yc * yc, axis=-1, keepdims=True)
    return yc * lax.rsqrt(var + NORM_EPS)


def _post_kernel(x_ref, hb_ref, hm_ref, ya_ref, hr_ref, mod_ref, wg_ref, mnw_ref, rnw_ref,
                 wbm_ref, wba_ref, wbr_ref, wo_ref, o_ref):
    hb = hb_ref[0]
    off = 0
    hm = jnp.concatenate([_head_norm(hm_ref[0, h]) for h in range(M_HEADS)], axis=1) * mnw_ref[...]
    ym = (_sigmoid(_dot(hb, wg_ref[:, off:off + M_WIDTH])) * hm).astype(BF16)
    off += M_WIDTH
    hr = jnp.concatenate([_head_norm(hr_ref[0, h]) for h in range(R_HEADS)], axis=1) * rnw_ref[...]
    rg = _dot(hb, wg_ref[:, off:off + R_V_WIDTH])
    yr = (rg * _sigmoid(rg) * hr).astype(BF16)
    off += R_V_WIDTH
    z = _sigmoid(_dot(hb, wg_ref[:, off:off + D_MODEL])) * _dot(ym, wbm_ref[...])
    off += D_MODEL
    pa = jnp.concatenate([_dot_tn(ya_ref[0, c], wba_ref[...]) for c in range(TM // A_BLOCK)], axis=0)
    z = z + _sigmoid(_dot(hb, wg_ref[:, off:off + D_MODEL])) * pa
    off += D_MODEL
    z = z + _sigmoid(_dot(hb, wg_ref[:, off:off + D_MODEL])) * _dot(yr, wbr_ref[...])
    y = _dot(z.astype(BF16), wo_ref[...])
    o_ref[0] = x_ref[0] + mod_ref[0, 0, 2:3, :] * y


def _post_call(xc, hb, hm, ya, hr, modsel, wg, mnw, rnw, wbm, wba, wbr, wo, t_off):
    B, T, D = xc.shape
    nt = T // TM - t_off

    def tile(t, b):
        return (b, t + t_off, 0)

    def head_tile(t, b):
        return (b, 0, t + t_off, 0)

    return pl.pallas_call(
        _post_kernel,
        grid=(nt, B),
        in_specs=[pl.BlockSpec((1, TM, D), tile),
                  pl.BlockSpec((1, TM, D), tile),
                  pl.BlockSpec((1, M_HEADS, TM, M_HEAD_DIM), head_tile),
                  pl.BlockSpec((1, TM // A_BLOCK, A_WIDTH, A_BLOCK), lambda t, b: (b, t + t_off, 0, 0)),
                  pl.BlockSpec((1, R_HEADS, TM, R_V_DIM), head_tile),
                  pl.BlockSpec((1, 1, 6, D), lambda t, b: (b, jnp.minimum(t + t_off, 1), 0, 0)),
                  _const_spec(wg.shape), _const_spec(mnw.shape), _const_spec(rnw.shape),
                  _const_spec(wbm.shape), _const_spec(wba.shape), _const_spec(wbr.shape),
                  _const_spec(wo.shape)],
        out_specs=pl.BlockSpec((1, TM, D), lambda t, b: (b, t, 0)),
        out_shape=jax.ShapeDtypeStruct((B, nt * TM, D), F32),
        compiler_params=_params(("parallel", "parallel")),
        name="post_merge",
    )(xc, hb, hm, ya, hr, modsel, wg, mnw, rnw, wbm, wba, wbr, wo)


def _ffn_kernel(x_ref, xp_ref, xn_ref, mod_ref, nw_ref, wa_ref, wb_ref, cw_ref, cb_ref, wd_ref, fw_ref,
                o_ref, hext_ref, u_ref, *, has_ctx, final):
    t = pl.program_id(0)
    nt = pl.num_programs(0)
    first_lat = 1 if has_ctx else 0
    sh = mod_ref[0, 0, 3:4, :]
    sc = mod_ref[0, 0, 4:5, :]
    g2 = mod_ref[0, 0, 5:6, :]
    nw = nw_ref[...]
    x = x_ref[0]
    hp = _norm_mod(xp_ref[0], nw, sc, sh)
    hn = _norm_mod(xn_ref[0], nw, sc, sh)
    hp = jnp.where(t <= first_lat, 0.0, hp)
    hn = jnp.where((t == first_lat - 1) | (t == nt - 1), 0.0, hn)
    hext_ref[0:TM, :] = _norm_mod(x, nw, sc, sh).astype(BF16)
    hext_ref[TM:, :] = jnp.concatenate([hn, hp], axis=0).astype(BF16)
    ext = TM + 2 * HALO
    for j in range(FFN_DIM // FFN_NC):
        cs = slice(j * FFN_NC, (j + 1) * FFN_NC)
        a = _dot(hext_ref[...], wa_ref[:, cs])
        b = _dot(hext_ref[0:TM, :], wb_ref[:, cs])
        cw = cw_ref[:, cs]
        prev = pltpu.roll(a, 1, 0)[0:TM]
        nxt = pltpu.roll(a, ext - 1, 0)[0:TM]
        conv = prev * cw[0:1] + a[0:TM] * cw[1:2] + nxt * cw[2:3] + cb_ref[:, cs]
        u_ref[:, cs] = (conv * _sigmoid(conv) * b).astype(BF16)
    y = x + g2 * _dot(u_ref[...], wd_ref[...])
    if final:
        ms = jnp.mean(y * y, axis=-1, keepdims=True)
        y = y * lax.rsqrt(ms + NORM_EPS) * fw_ref[...]
    o_ref[0] = y


def _ffn_call(xs, modsel, nw, wa, wb, cw, cb, wd, fw, has_ctx, final):
    B, T, D = xs.shape
    nt = T // TM
    hb16 = TM // HALO
    first_lat = 1 if has_ctx else 0
    tile = pl.BlockSpec((1, TM, D), lambda t, b: (b, t, 0))
    return pl.pallas_call(
        functools.partial(_ffn_kernel, has_ctx=has_ctx, final=final),
        grid=(nt, B),
        in_specs=[tile,
                  pl.BlockSpec((1, HALO, D), lambda t, b: (b, jnp.maximum(t * hb16 - 1, 0), 0)),
                  pl.BlockSpec((1, HALO, D), lambda t, b: (b, jnp.minimum((t + 1) * hb16, T // HALO - 1), 0)),
                  pl.BlockSpec((1, 1, 6, D), lambda t, b: (b, jnp.minimum(t + 1 - first_lat, 1), 0, 0)),
                  _const_spec(nw.shape), _const_spec(wa.shape), _const_spec(wb.shape),
                  _const_spec(cw.shape), _const_spec(cb.shape), _const_spec(wd.shape),
                  _const_spec(fw.shape)],
        out_specs=tile,
        out_shape=jax.ShapeDtypeStruct((B, T, D), F32),
        scratch_shapes=[pltpu.VMEM((TM + 2 * HALO, D), BF16), pltpu.VMEM((TM, FFN_DIM), BF16)],
        compiler_params=_params(("parallel", "parallel")),
        name="conv_ffn",
    )(xs, xs, xs, modsel, nw, wa, wb, cw, cb, wd, fw)


def _split_cols(w):
    out = []
    acc = 0
    for s in IN_SPLITS:
        out.append(w[:, acc:acc + s])
        acc += s
    return out


def _rope_tables(L, S):
    T = L + S
    j = jnp.arange(LANES)
    jj = j % 32
    inv_a = ROPE_BASE ** (-(jj % 16).astype(F32) / 16.0)
    t = jnp.arange(S)
    pos = jnp.where(((j % A_HEAD_DIM) // 32 == 0)[None, :], (t // GRID_W)[:, None], (t % GRID_W)[:, None])
    ang = pos.astype(F32) * inv_a[None, :]
    sign = jnp.where(jj < 16, -1.0, 1.0)[None, :]
    ca = jnp.concatenate([jnp.ones((L, LANES), F32), jnp.cos(ang)], axis=0)
    sa = jnp.concatenate([jnp.zeros((L, LANES), F32), jnp.sin(ang) * sign], axis=0)
    half = R_QK_DIM // 2
    inv_r = ROPE_BASE ** (-(j % half).astype(F32) / half)
    angr = jnp.arange(T).astype(F32)[:, None] * inv_r[None, :]
    signr = jnp.where(j < half, -1.0, 1.0)[None, :]
    return ca, sa, jnp.cos(angr), jnp.sin(angr) * signr


def kernel(x, c, ctx, c_ctx, mod_w, mod_b, norm1_w, norm2_w, w_in, m_gate_bias, m_conv_w, m_norm_w,
           a_sink, ret_logit, ret_norm_w, w_br_m, w_br_a, w_br_r, w_out, ffn_up, ffn_conv_w, ffn_conv_b,
           ffn_down, final_norm_w):
    B, S, D = x.shape
    L = ctx.shape[1]
    depth = mod_w.shape[0]
    assert D == D_MODEL and L == TM and S % TM == 0 and S >= 3 * A_BLOCK
    T = L + S
    n_ctx = L // CHUNK

    rows = -(-(B + 1) // 8) * 8
    cpad = jnp.zeros((rows, D), F32).at[:B].set(c).at[B].set(c_ctx)
    mods = _modulation(cpad, mod_w, mod_b)
    tabs = _rope_tables(L, S)
    xc = jnp.concatenate([ctx, x], axis=1)

    gperm = jnp.arange(4 * M_HEADS).reshape(4, M_HEADS).T.reshape(-1)

    for l in range(depth):
        last = l == depth - 1
        lat = mods[l, :B].reshape(B, 6, D)
        cm = jnp.broadcast_to(mods[l, B].reshape(1, 6, D), (B, 6, D))
        modsel = jnp.stack([cm, lat], axis=1)

        (w_mq, w_mk, w_mv, w_mo, w_mg, w_aq, w_ak, w_av,
         w_rq, w_rk, w_rv, w_rg, w_gm, w_ga, w_gr) = _split_cols(w_in[l])
        wqk = jnp.concatenate([w_mq, w_mk], axis=1).astype(BF16)
        wr = jnp.concatenate([w_mv, w_aq, w_ak, w_av, w_rq, w_rk, w_rv], axis=1).astype(BF16)
        wgt = w_mg[:, gperm].T.astype(BF16)
        gbt = jnp.broadcast_to(m_gate_bias[l][gperm].reshape(-1, 1), (4 * M_HEADS, LANES))

        (hb, mq, mk, mv, gr, aq, ak, av, rq, rk, rv) = _pre_call(
            xc, modsel, norm1_w[l].reshape(1, D), wqk, m_conv_w[l], wr, wgt, gbt, tabs)

        hm = _mlstm_call(mq, mk, mv, gr, n_ctx)
        sink_rows = jnp.repeat(a_sink[l].reshape(A_KV_HEADS, A_GROUP), A_BLOCK, axis=1)[:, None, :]
        ya = _attn_call(aq, ak, av, sink_rows.astype(F32), L, not last)
        lgb = jnp.broadcast_to(ret_logit[l].T[:, :, None, None], (R_HEADS, 2, 8, R_V_DIM)).astype(F32)
        hr = _ret_call(rq, rk, rv, lgb, n_ctx)

        t_off = 1 if last else 0
        wgates = jnp.concatenate([w_mo, w_rg, w_gm, w_ga, w_gr], axis=1).astype(BF16)
        x1 = _post_call(xc, hb, hm, ya, hr, modsel, wgates, m_norm_w[l].reshape(1, -1),
                        ret_norm_w[l].reshape(1, -1), w_br_m[l].astype(BF16), w_br_a[l].astype(BF16),
                        w_br_r[l].astype(BF16), w_out[l].astype(BF16), t_off)
        xc = _ffn_call(x1, modsel, norm2_w[l].reshape(1, D), ffn_up[l][:, :FFN_DIM].astype(BF16),
                       ffn_up[l][:, FFN_DIM:].astype(BF16), ffn_conv_w[l], ffn_conv_b[l].reshape(1, -1),
                       ffn_down[l].astype(BF16), final_norm_w.reshape(1, D), not last, last)
    return xc
```

```python
import functools

import jax
import jax.numpy as jnp
from jax import lax
from jax.experimental import pallas as pl
from jax.experimental.pallas import tpu as pltpu

F32 = jnp.float32
BF16 = jnp.bfloat16

D_MODEL = 1024
GRID_W = 64
NORM_EPS = 1e-6
ROPE_BASE = 10000.0
NEG_INF = -1e30
LOG2E = 1.4426950408889634

M_HEADS = 4
M_HEAD_DIM = 128
M_WIDTH = M_HEADS * M_HEAD_DIM
A_HEADS = 8
A_KV_HEADS = 2
A_GROUP = A_HEADS // A_KV_HEADS
A_HEAD_DIM = 64
A_WIDTH = A_HEADS * A_HEAD_DIM
A_KV_WIDTH = A_KV_HEADS * A_HEAD_DIM
A_WINDOW = 128
A_BLOCK = 128
R_HEADS = 4
R_QK_DIM = 128
R_V_DIM = 256
R_QK_WIDTH = R_HEADS * R_QK_DIM
R_V_WIDTH = R_HEADS * R_V_DIM
FFN_DIM = 2816

IN_SPLITS = (M_WIDTH, M_WIDTH, M_WIDTH, M_WIDTH, 4 * M_HEADS,
             A_WIDTH, A_KV_WIDTH, A_KV_WIDTH,
             R_QK_WIDTH, R_QK_WIDTH, R_V_WIDTH, R_V_WIDTH,
             D_MODEL, D_MODEL, D_MODEL)

TM = 256
HALO = 8
CHUNK = 128
FFN_NC = 256
LANES = 128
VMEM_LIMIT = 56 * 1024 * 1024


def _sigmoid(x):
    return 1.0 / (1.0 + jnp.exp(-x))


def _log_sigmoid(x):
    return jnp.minimum(x, 0.0) - jnp.log1p(jnp.exp(-jnp.abs(x)))


def _dot(a, b):
    return jnp.dot(a, b, preferred_element_type=F32)


def _dot_nt(a, b):
    return lax.dot_general(a, b, (((1,), (1,)), ((), ())), preferred_element_type=F32)


def _dot_tn(a, b):
    return lax.dot_general(a, b, (((0,), (0,)), ((), ())), preferred_element_type=F32)


def _const_spec(shape):
    nd = len(shape)
    return pl.BlockSpec(shape, lambda *_: (0,) * nd, pipeline_mode=pl.Buffered(1))


def _params(sem):
    return pltpu.CompilerParams(dimension_semantics=sem, vmem_limit_bytes=VMEM_LIMIT)


def _mod_kernel(c_ref, w_ref, b_ref, o_ref):
    c = c_ref[...]
    s = (c * _sigmoid(c)).astype(BF16)
    o_ref[0] = _dot(s, w_ref[0].astype(BF16)) + b_ref[0]


def _modulation(cpad, mod_w, mod_b):
    depth, d, n = mod_w.shape
    rows = cpad.shape[0]
    bn = 1024
    return pl.pallas_call(
        _mod_kernel,
        grid=(depth, n // bn),
        in_specs=[pl.BlockSpec((rows, d), lambda l, j: (0, 0)),
                  pl.BlockSpec((1, d, bn), lambda l, j: (l, 0, j)),
                  pl.BlockSpec((1, 1, bn), lambda l, j: (l, 0, j))],
        out_specs=pl.BlockSpec((1, rows, bn), lambda l, j: (l, 0, j)),
        out_shape=jax.ShapeDtypeStruct((depth, rows, n), F32),
        compiler_params=_params(("parallel", "parallel")),
        name="modulation",
    )(cpad, mod_w, mod_b.reshape(depth, 1, n))


def _norm_mod(xv, nw, sc, sh):
    ms = jnp.mean(xv * xv, axis=-1, keepdims=True)
    return (xv * lax.rsqrt(ms + NORM_EPS) * nw) * (1.0 + sc) + sh


def _seg_cumsum(x, axis, reverse):
    n = x.shape[axis]
    idx = lax.broadcasted_iota(jnp.int32, x.shape, axis)
    s = 1
    while s < n:
        if reverse:
            shifted = pltpu.roll(x, n - s, axis)
            x = x + jnp.where(idx < n - s, shifted, 0.0)
        else:
            shifted = pltpu.roll(x, s, axis)
            x = x + jnp.where(idx >= s, shifted, 0.0)
        s *= 2
    return x


def _gate_transform(raw, gate_axis):
    tok_axis = 1 - gate_axis
    k = lax.broadcasted_iota(jnp.int32, raw.shape, gate_axis) % 4
    lf = _log_sigmoid(raw)
    cum_f = _seg_cumsum(jnp.where(k == 1, lf, 0.0), tok_axis, False)
    cum_b = _seg_cumsum(jnp.where(k == 3, lf, 0.0), tok_axis, True)
    bsel = jnp.where(k == 1, cum_f, jnp.where(k == 3, cum_b, 0.0))
    n = raw.shape[gate_axis]
    bnext = pltpu.roll(bsel, n - 1, gate_axis)
    nt = raw.shape[tok_axis]
    tot_f = lax.slice_in_dim(cum_f, nt - 1, nt, axis=tok_axis)
    tot_b = lax.slice_in_dim(cum_b, 0, 1, axis=tok_axis)
    total = jnp.broadcast_to(tot_f + tot_b, raw.shape)
    return jnp.where(k % 2 == 1, bsel, raw - bnext), total


def _pre_kernel(x_ref, xp_ref, xn_ref, mod_ref, nw_ref, wqk_ref, cw_ref, wr_ref, wgt_ref,
                gbt_ref, ca_ref, sa_ref, cr_ref, sr_ref,
                hb_ref, mq_ref, mk_ref, mv_ref, gr_ref, aq_ref, ak_ref, av_ref,
                rq_ref, rk_ref, rv_ref, hext_ref):
    t = pl.program_id(0)
    nt = pl.num_programs(0)
    sh = mod_ref[0, 0, 0:1, :]
    sc = mod_ref[0, 0, 1:2, :]
    nw = nw_ref[...]
    hb = _norm_mod(x_ref[0], nw, sc, sh).astype(BF16)
    hb_ref[0] = hb

    grow = _dot_nt(wgt_ref[...], hb) + gbt_ref[:, 0:1]
    for c in range(TM // CHUNK):
        gr, tot = _gate_transform(grow[:, c * CHUNK:(c + 1) * CHUNK], 0)
        aend = gr + pltpu.roll(tot, 4 * M_HEADS - 1, 0)
        amax = jnp.broadcast_to(jnp.max(aend, axis=1, keepdims=True), aend.shape)
        for h in range(M_HEADS):
            hp_, o0 = h // 2, 8 * (h % 2)
            gr_ref[0, hp_, c, o0:o0 + 4, :] = gr[4 * h:4 * h + 4]
            for d in range(2):
                gr_ref[0, hp_, c, o0 + 4 + d:o0 + 5 + d, :] = tot[4 * h + 2 * d + 1:4 * h + 2 * d + 2]
                gr_ref[0, hp_, c, o0 + 6 + d:o0 + 7 + d, :] = amax[4 * h + 2 * d:4 * h + 2 * d + 1]

    hp = _norm_mod(xp_ref[0], nw, sc, sh)
    hn = _norm_mod(xn_ref[0], nw, sc, sh)
    hp = jnp.where(t <= 1, 0.0, hp)
    hn = jnp.where((t == 0) | (t == nt - 1), 0.0, hn)
    hext_ref[0:TM, :] = hb
    hext_ref[TM:, :] = jnp.concatenate([hn, hp], axis=0).astype(BF16)
    hext = hext_ref[...]
    ext = TM + 2 * HALO

    for j, (dst, scale) in enumerate(((mq_ref, 1.0), (mk_ref, M_HEAD_DIM ** -0.5))):
        p = _dot(hext, wqk_ref[:, j * M_WIDTH:(j + 1) * M_WIDTH])
        cw = cw_ref[:, j * M_WIDTH:(j + 1) * M_WIDTH]
        prev = pltpu.roll(p, 1, 0)[0:TM]
        nxt = pltpu.roll(p, ext - 1, 0)[0:TM]
        conv = prev * cw[0:1] + p[0:TM] * cw[1:2] + nxt * cw[2:3]
        act = conv * _sigmoid(conv) * scale
        for h in range(M_HEADS):
            dst[0, h] = act[:, h * M_HEAD_DIM:(h + 1) * M_HEAD_DIM].astype(BF16)

    off = 0
    p = _dot(hb, wr_ref[:, off:off + M_WIDTH])
    off += M_WIDTH
    for h in range(M_HEADS):
        for c in range(TM // CHUNK):
            blk = p[c * CHUNK:(c + 1) * CHUNK, h * M_HEAD_DIM:(h + 1) * M_HEAD_DIM]
            mv_ref[0, h, c] = blk.T.astype(BF16)

    lane = lax.broadcasted_iota(jnp.int32, (TM, LANES), 1)
    first16 = (lane % 32) < 16
    ca = ca_ref[...]
    sa = sa_ref[...]

    def rope_a(xs):
        partner = jnp.where(first16, pltpu.roll(xs, LANES - 16, 1), pltpu.roll(xs, 16, 1))
        return xs * ca + partner * sa

    p = _dot(hb, wr_ref[:, off:off + A_WIDTH])
    off += A_WIDTH
    for s in range(A_WIDTH // LANES):
        r = (rope_a(p[:, s * LANES:(s + 1) * LANES]) * (A_HEAD_DIM ** -0.5 * LOG2E)).astype(BF16)
        aq_ref[0, 2 * s] = r[:, 0:A_HEAD_DIM]
        aq_ref[0, 2 * s + 1] = r[:, A_HEAD_DIM:]
    p = _dot(hb, wr_ref[:, off:off + 2 * A_KV_WIDTH])
    off += 2 * A_KV_WIDTH
    r = rope_a(p[:, 0:LANES]).astype(BF16)
    for h in range(A_KV_HEADS):
        ak_ref[0, h] = r[:, h * A_HEAD_DIM:(h + 1) * A_HEAD_DIM]
    pv = p[:, LANES:]
    for h in range(A_KV_HEADS):
        vh = pv if h == 0 else pltpu.roll(pv, A_HEAD_DIM, 1)
        ve = jnp.where(lane < A_HEAD_DIM, vh, 1.0)
        for c in range(TM // A_BLOCK):
            av_ref[0, h, c] = ve[c * A_BLOCK:(c + 1) * A_BLOCK].T.astype(BF16)

    cr = cr_ref[...]
    sr = sr_ref[...]
    for is_k in (False, True):
        p = _dot(hb, wr_ref[:, off:off + R_QK_WIDTH])
        off += R_QK_WIDTH
        for h in range(R_HEADS):
            xs = p[:, h * R_QK_DIM:(h + 1) * R_QK_DIM]
            rot = xs * cr + pltpu.roll(xs, R_QK_DIM // 2, 1) * sr
            if is_k:
                rot = rot * R_QK_DIM ** -0.5
                for c in range(TM // CHUNK):
                    rk_ref[0, h, c] = rot[c * CHUNK:(c + 1) * CHUNK].T.astype(BF16)
            else:
                rq_ref[0, h] = rot.astype(BF16)
    p = _dot(hb, wr_ref[:, off:off + R_V_WIDTH])
    for h in range(R_HEADS):
        rv_ref[0, h] = p[:, h * R_V_DIM:(h + 1) * R_V_DIM].astype(BF16)


def _pre_call(xc, modsel, nw, wqk, cw, wr, wgt, gbt, tabs):
    B, T, D = xc.shape
    nt = T // TM
    hb16 = TM // HALO
    nch = T // CHUNK

    def tile(t, b):
        return (b, t, 0)

    def head_tile(t, b):
        return (b, 0, t, 0)

    in_specs = [
        pl.BlockSpec((1, TM, D), tile),
        pl.BlockSpec((1, HALO, D), lambda t, b: (b, jnp.maximum(t * hb16 - 1, 0), 0)),
        pl.BlockSpec((1, HALO, D), lambda t, b: (b, jnp.minimum((t + 1) * hb16, T // HALO - 1), 0)),
        pl.BlockSpec((1, 1, 6, D), lambda t, b: (b, jnp.minimum(t, 1), 0, 0)),
        _const_spec(nw.shape), _const_spec(wqk.shape), _const_spec(cw.shape), _const_spec(wr.shape),
        _const_spec(wgt.shape), _const_spec(gbt.shape),
    ] + [pl.BlockSpec((TM, LANES), lambda t, b: (t, 0)) for _ in range(4)]
    out_shape = [
        jax.ShapeDtypeStruct((B, T, D), BF16),
        jax.ShapeDtypeStruct((B, M_HEADS, T, M_HEAD_DIM), BF16),
        jax.ShapeDtypeStruct((B, M_HEADS, T, M_HEAD_DIM), BF16),
        jax.ShapeDtypeStruct((B, M_HEADS, nch, M_HEAD_DIM, CHUNK), BF16),
        jax.ShapeDtypeStruct((B, 2, nch, 16, CHUNK), F32),
        jax.ShapeDtypeStruct((B, A_HEADS, T, A_HEAD_DIM), BF16),
        jax.ShapeDtypeStruct((B, A_KV_HEADS, T, A_HEAD_DIM), BF16),
        jax.ShapeDtypeStruct((B, A_KV_HEADS, T // A_BLOCK, LANES, A_BLOCK), BF16),
        jax.ShapeDtypeStruct((B, R_HEADS, T, R_QK_DIM), BF16),
        jax.ShapeDtypeStruct((B, R_HEADS, nch, R_QK_DIM, CHUNK), BF16),
        jax.ShapeDtypeStruct((B, R_HEADS, T, R_V_DIM), BF16),
    ]
    out_specs = [
        pl.BlockSpec((1, TM, D), tile),
        pl.BlockSpec((1, M_HEADS, TM, M_HEAD_DIM), head_tile),
        pl.BlockSpec((1, M_HEADS, TM, M_HEAD_DIM), head_tile),
        pl.BlockSpec((1, M_HEADS, TM // CHUNK, M_HEAD_DIM, CHUNK), lambda t, b: (b, 0, t, 0, 0)),
        pl.BlockSpec((1, 2, TM // CHUNK, 16, CHUNK), lambda t, b: (b, 0, t, 0, 0)),
        pl.BlockSpec((1, A_HEADS, TM, A_HEAD_DIM), head_tile),
        pl.BlockSpec((1, A_KV_HEADS, TM, A_HEAD_DIM), head_tile),
        pl.BlockSpec((1, A_KV_HEADS, TM // A_BLOCK, LANES, A_BLOCK), lambda t, b: (b, 0, t, 0, 0)),
        pl.BlockSpec((1, R_HEADS, TM, R_QK_DIM), head_tile),
        pl.BlockSpec((1, R_HEADS, TM // CHUNK, R_QK_DIM, CHUNK), lambda t, b: (b, 0, t, 0, 0)),
        pl.BlockSpec((1, R_HEADS, TM, R_V_DIM), head_tile),
    ]
    return pl.pallas_call(
        _pre_kernel,
        grid=(nt, B),
        in_specs=in_specs,
        out_specs=out_specs,
        out_shape=out_shape,
        scratch_shapes=[pltpu.VMEM((TM + 2 * HALO, D), BF16)],
        compiler_params=_params(("parallel", "parallel")),
        name="pre_proj",
    )(xc, xc, xc, modsel, nw, wqk, cw, wr, wgt, gbt, *tabs)


def _bwd_chunk(i, n_ctx, n_all):
    if isinstance(i, int):
        return n_ctx - 1 - i if i < n_ctx else n_all - 1 + n_ctx - i
    return n_all - 1 + n_ctx - i


def _chunk_start(c):
    return c * CHUNK if isinstance(c, int) else pl.multiple_of(c * CHUNK, CHUNK)


def _chunk_loop(body, n_ctx, n_all, unroll):
    for i in range(n_ctx):
        body(i, 0)
    n_lat = n_all - n_ctx
    while n_lat % unroll:
        unroll //= 2
    lax.fori_loop(n_ctx, n_all, body, 0, unroll=unroll)


def _mlstm_kernel(q_ref, k_ref, vt_ref, gr_ref, o_ref, c_ref, n_ref, m_ref,
                  cs_ref, ns_ref, ms_ref, *, n_ctx):
    n_all = gr_ref.shape[2]
    dh = q_ref.shape[3]
    c_ref[...] = jnp.zeros(c_ref.shape, F32)
    n_ref[...] = jnp.zeros(n_ref.shape, F32)
    m_ref[...] = jnp.zeros(m_ref.shape, F32)

    def scan_unit(c, hh, d):
        u = 2 * hh + d
        c0 = _chunk_start(c)
        k = k_ref[0, hh, pl.ds(c0, CHUNK), :]
        vt = vt_ref[0, hh, c]
        rows = gr_ref[0, 0, c, 8 * hh:8 * hh + 8, :]
        tot = rows[4 + d:5 + d]
        aend = rows[2 * d:2 * d + 1] + tot
        amax = rows[6 + d:7 + d]
        cst = c_ref[u]
        nv = n_ref[u]
        m = m_ref[u]
        cs_ref[u, c] = cst.astype(BF16)
        ns_ref[u, c] = jnp.broadcast_to(nv, (16, dh)).astype(BF16)
        ms_ref[u, c] = jnp.broadcast_to(m, (8, CHUNK))
        m_new = jnp.maximum(tot + m, amax)
        w = jnp.exp(aend - m_new)
        dec = jnp.exp(tot + m - m_new)
        vw = (vt.astype(F32) * w).astype(BF16)
        c_ref[u] = dec * cst + _dot(vw, k)
        wk = _dot(jnp.broadcast_to(w, (16, CHUNK)).astype(BF16), k)
        n_ref[u] = dec * nv + wk[0:1]
        m_ref[u] = m_new

    def scan_step(i, carry):
        cb = _bwd_chunk(i, n_ctx, n_all)
        for hh in range(2):
            scan_unit(i, hh, 0)
            scan_unit(cb, hh, 1)
        return carry

    _chunk_loop(scan_step, n_ctx, n_all, 4)

    row = lax.broadcasted_iota(jnp.int32, (CHUNK, CHUNK), 0)
    col = lax.broadcasted_iota(jnp.int32, (CHUNK, CHUNK), 1)
    tri = (row <= col, row >= col)

    def out_step(c, carry):
        c0 = _chunk_start(c)
        gcv = jnp.concatenate([gr_ref[0, 0, c], jnp.zeros((CHUNK - 16, CHUNK), F32)], axis=0).T
        for hh in range(2):
            q = q_ref[0, hh, pl.ds(c0, CHUNK), :]
            k = k_ref[0, hh, pl.ds(c0, CHUNK), :]
            vt = vt_ref[0, hh, c]
            rows = gr_ref[0, 0, c, 8 * hh:8 * hh + 8, :]
            st = _dot_nt(k, q)
            acc = None
            for d in range(2):
                u = 2 * hh + d
                a0c = gcv[:, 8 * hh + 2 * d:8 * hh + 2 * d + 1]
                b = rows[2 * d + 1:2 * d + 2]
                m = ms_ref[u, c, 0:1, :]
                dlog = jnp.where(tri[d], a0c + b, NEG_INF)
                mj = jnp.maximum(b + m, jnp.max(dlog, axis=0, keepdims=True))
                sd = st * jnp.exp(dlog - mj)
                iw = jnp.exp(b + m - mj)
                nq = _dot_nt(ns_ref[u, c], q)[0:1]
                den = iw * nq + jnp.sum(sd, axis=0, keepdims=True)
                inv = 1.0 / jnp.maximum(jnp.abs(den), jnp.exp(-mj))
                ht = (_dot(vt, sd.astype(BF16)) + iw * _dot_nt(cs_ref[u, c], q)) * inv
                acc = ht if acc is None else acc + ht
            o_ref[0, hh, pl.ds(c0, CHUNK), :] = acc.T
        return carry

    _chunk_loop(out_step, n_ctx, n_all, 2)


def _mlstm_call(mq, mk, mvt, gr, n_ctx):
    B, H, T, dh = mq.shape
    nch = T // CHUNK
    qspec = pl.BlockSpec((1, 2, T, dh), lambda b, p: (b, p, 0, 0))
    return pl.pallas_call(
        functools.partial(_mlstm_kernel, n_ctx=n_ctx),
        grid=(B, H // 2),
        in_specs=[qspec, qspec,
                  pl.BlockSpec((1, 2, nch, dh, CHUNK), lambda b, p: (b, p, 0, 0, 0)),
                  pl.BlockSpec((1, 1, nch, 16, CHUNK), lambda b, p: (b, p, 0, 0, 0))],
        out_specs=pl.BlockSpec((1, 2, T, dh), lambda b, p: (b, p, 0, 0)),
        out_shape=jax.ShapeDtypeStruct((B, H, T, dh), F32),
        scratch_shapes=[pltpu.VMEM((4, dh, dh), F32), pltpu.VMEM((4, 1, dh), F32),
                        pltpu.VMEM((4, 1, CHUNK), F32),
                        pltpu.VMEM((4, nch, dh, dh), BF16), pltpu.VMEM((4, nch, 16, dh), BF16),
                        pltpu.VMEM((4, nch, 8, CHUNK), F32)],
        compiler_params=_params(("parallel", "parallel")),
        name="mlstm",
    )(mq, mk, mvt, gr)


def _ret_kernel(q_ref, kt_ref, v_ref, lg_ref, o_ref, st_ref, sts_ref, dec_ref, ib_ref, wb_ref, *, n_ctx):
    n_all = kt_ref.shape[2]
    dk = q_ref.shape[3]
    dv = v_ref.shape[3]
    st_ref[...] = jnp.zeros(st_ref.shape, F32)
    row = lax.broadcasted_iota(jnp.int32, (CHUNK, CHUNK), 0).astype(F32)
    col = lax.broadcasted_iota(jnp.int32, (CHUNK, CHUNK), 1).astype(F32)
    cds = []
    dec = None
    for d in range(2):
        lg = _log_sigmoid(lg_ref[0, d])[0:1]
        lgk = lg[:, 0:CHUNK]
        diff = row - col if d == 0 else col - row
        dd = jnp.where(diff >= 0, jnp.exp(lgk * jnp.maximum(diff, 0.0)), 0.0)
        dec = dd if dec is None else dec + dd
        pin = row if d == 0 else CHUNK - 1.0 - row
        ib_ref[d] = jnp.exp(lg * (jnp.concatenate([pin, pin], axis=1) + 1.0))
        pkey = col if d == 0 else CHUNK - 1.0 - col
        wb_ref[d] = jnp.exp(lgk * (CHUNK - 1.0 - pkey))
        cds.append(jnp.exp(lg * CHUNK))
    dec_ref[...] = dec

    def scan_step(i, carry):
        cb = _bwd_chunk(i, n_ctx, n_all)
        for d, c in ((0, i), (1, cb)):
            c0 = _chunk_start(c)
            st = st_ref[d]
            sts_ref[c, :, d * dv:(d + 1) * dv] = st.astype(BF16)
            kw = (kt_ref[0, 0, c].astype(F32) * wb_ref[d]).astype(BF16)
            st_ref[d] = cds[d] * st + _dot(kw, v_ref[0, 0, pl.ds(c0, CHUNK), :])
        return carry

    _chunk_loop(scan_step, n_ctx, n_all, 4)

    def out_step(c, carry):
        c0 = _chunk_start(c)
        q = q_ref[0, 0, pl.ds(c0, CHUNK), :]
        s = (_dot(q, kt_ref[0, 0, c]) * dec_ref[...]).astype(BF16)
        qst = _dot(q, sts_ref[c])
        o_ref[0, 0, pl.ds(c0, CHUNK), :] = (_dot(s, v_ref[0, 0, pl.ds(c0, CHUNK), :])
                                            + ib_ref[0] * qst[:, 0:dv] + ib_ref[1] * qst[:, dv:])
        return carry

    _chunk_loop(out_step, n_ctx, n_all, 4)


def _ret_call(rq, rkt, rv, lgb, n_ctx):
    B, H, T, dk = rq.shape
    dv = rv.shape[-1]
    nch = T // CHUNK
    return pl.pallas_call(
        functools.partial(_ret_kernel, n_ctx=n_ctx),
        grid=(B, H),
        in_specs=[pl.BlockSpec((1, 1, T, dk), lambda b, h: (b, h, 0, 0)),
                  pl.BlockSpec((1, 1, nch, dk, CHUNK), lambda b, h: (b, h, 0, 0, 0)),
                  pl.BlockSpec((1, 1, T, dv), lambda b, h: (b, h, 0, 0)),
                  pl.BlockSpec((1, 2, 8, dv), lambda b, h: (h, 0, 0, 0))],
        out_specs=pl.BlockSpec((1, 1, T, dv), lambda b, h: (b, h, 0, 0)),
        out_shape=jax.ShapeDtypeStruct((B, H, T, dv), F32),
        scratch_shapes=[pltpu.VMEM((2, dk, dv), F32), pltpu.VMEM((nch, dk, 2 * dv), BF16),
                        pltpu.VMEM((CHUNK, CHUNK), F32), pltpu.VMEM((2, CHUNK, dv), F32),
                        pltpu.VMEM((2, dk, CHUNK), F32)],
        compiler_params=_params(("parallel", "parallel")),
        name="retention",
    )(rq, rkt, rv, lgb)


def _attn_kernel(q_ref, k_ref, vt_ref, sink_ref, o_ref, bias_ref, s_ref, *, n_ctx_tok, need_ctx):
    T = q_ref.shape[2]
    L = n_ctx_tok
    S = T - L
    nb = S // A_BLOCK
    nspan = 3
    span = nspan * A_BLOCK
    ncb = L // A_BLOCK
    cols = A_GROUP * A_BLOCK
    kc = k_ref[0, 0, 0:L, :]
    vtc = jnp.concatenate([vt_ref[0, 0, j] for j in range(ncb)], axis=1)
    sink = sink_ref[0]

    kpos = lax.broadcasted_iota(jnp.int32, (span, A_BLOCK), 0)
    qpos = lax.broadcasted_iota(jnp.int32, (span, A_BLOCK), 1)
    for delta in range(nspan):
        bias_ref[delta] = jnp.where(jnp.abs(qpos + delta * A_BLOCK - kpos) <= A_WINDOW, 0.0, NEG_INF)

    def finish(blk, acc, m):
        den = acc[A_HEAD_DIM:A_HEAD_DIM + 1] + jnp.exp2(sink - m)
        ot = (acc[0:A_HEAD_DIM] * (1.0 / den)).astype(BF16)
        for g in range(A_GROUP):
            o_ref[0, blk, g * A_HEAD_DIM:(g + 1) * A_HEAD_DIM, :] = ot[:, g * A_BLOCK:(g + 1) * A_BLOCK]

    for cbk in range(ncb):
        if need_ctx:
            qs = q_ref[0, :, cbk * A_BLOCK:(cbk + 1) * A_BLOCK, :].reshape(cols, A_HEAD_DIM)
            s = _dot_nt(kc, qs)
            m = jnp.maximum(jnp.max(s, axis=0, keepdims=True), sink)
            finish(cbk, _dot(vtc, jnp.exp2(s - m).astype(BF16)), m)
        else:
            o_ref[0, cbk] = jnp.zeros(o_ref.shape[2:], BF16)

    def first_key_block(n):
        return jnp.clip(n - 1, 0, nb - nspan)

    def scores(n, slot):
        q0 = pl.multiple_of(L + n * A_BLOCK, A_BLOCK)
        kb = first_key_block(n)
        k0 = pl.multiple_of(L + kb * A_BLOCK, A_BLOCK)
        qs = q_ref[0, :, pl.ds(q0, A_BLOCK), :].reshape(cols, A_HEAD_DIM)
        bias = bias_ref[n - kb]
        s_ref[slot, 0:span, :] = (_dot_nt(k_ref[0, 0, pl.ds(k0, span), :], qs)
                                  + jnp.concatenate([bias] * A_GROUP, axis=1))
        s_ref[slot, span:, :] = _dot_nt(kc, qs)

    def softmax_pv(n, slot):
        kb = first_key_block(n)
        s = s_ref[slot]
        m = jnp.maximum(jnp.max(s, axis=0, keepdims=True), sink)
        p = jnp.exp2(s - m).astype(BF16)
        vt = jnp.concatenate([vt_ref[0, 0, ncb + kb + j] for j in range(nspan)] + [vtc], axis=1)
        finish(ncb + n, _dot(vt, p), m)

    scores(0, 0)

    def block_pair(i, carry):
        n = 2 * i
        scores(n + 1, 1)
        softmax_pv(n, 0)
        scores(jnp.minimum(n + 2, nb - 1), 0)
        softmax_pv(n + 1, 1)
        return carry

    lax.fori_loop(0, nb // 2, block_pair, 0)


def _attn_call(aq, ak, avt, sink_rows, n_ctx_tok, need_ctx):
    B, _, T, hd = aq.shape
    nblk = T // A_BLOCK
    cols = A_GROUP * A_BLOCK
    return pl.pallas_call(
        functools.partial(_attn_kernel, n_ctx_tok=n_ctx_tok, need_ctx=need_ctx),
        grid=(B, A_KV_HEADS),
        in_specs=[pl.BlockSpec((1, A_GROUP, T, hd), lambda b, h: (b, h, 0, 0)),
                  pl.BlockSpec((1, 1, T, hd), lambda b, h: (b, h, 0, 0)),
                  pl.BlockSpec((1, 1, nblk, LANES, A_BLOCK), lambda b, h: (b, h, 0, 0, 0)),
                  pl.BlockSpec((1, 1, cols), lambda b, h: (h, 0, 0))],
        out_specs=pl.BlockSpec((1, nblk, A_GROUP * hd, A_BLOCK), lambda b, h: (b, 0, h, 0)),
        out_shape=jax.ShapeDtypeStruct((B, nblk, A_WIDTH, A_BLOCK), BF16),
        scratch_shapes=[pltpu.VMEM((3, 3 * A_BLOCK, A_BLOCK), F32),
                        pltpu.VMEM((2, 3 * A_BLOCK + n_ctx_tok, cols), F32)],
        compiler_params=_params(("parallel", "parallel")),
        name="window_attn",
    )(aq, ak, avt, sink_rows)


def _head_norm(y):
    mu = jnp.mean(y, axis=-1, keepdims=True)
    yc = y - mu
    var = jnp.mean(yc * yc, axis=-1, keepdims=True)
    return yc * lax.rsqrt(var + NORM_EPS)


def _post_kernel(x_ref, hb_ref, hm_ref, ya_ref, hr_ref, mod_ref, wg_ref, mnw_ref, rnw_ref,
                 wbm_ref, wba_ref, wbr_ref, wo_ref, o_ref):
    hb = hb_ref[0]
    off = 0
    hm = jnp.concatenate([_head_norm(hm_ref[0, h]) for h in range(M_HEADS)], axis=1) * mnw_ref[...]
    ym = (_sigmoid(_dot(hb, wg_ref[:, off:off + M_WIDTH])) * hm).astype(BF16)
    off += M_WIDTH
    hr = jnp.concatenate([_head_norm(hr_ref[0, h]) for h in range(R_HEADS)], axis=1) * rnw_ref[...]
    rg = _dot(hb, wg_ref[:, off:off + R_V_WIDTH])
    yr = (rg * _sigmoid(rg) * hr).astype(BF16)
    off += R_V_WIDTH
    z = _sigmoid(_dot(hb, wg_ref[:, off:off + D_MODEL])) * _dot(ym, wbm_ref[...])
    off += D_MODEL
    pa = jnp.concatenate([_dot_tn(ya_ref[0, c], wba_ref[...]) for c in range(TM // A_BLOCK)], axis=0)
    z = z + _sigmoid(_dot(hb, wg_ref[:, off:off + D_MODEL])) * pa
    off += D_MODEL
    z = z + _sigmoid(_dot(hb, wg_ref[:, off:off + D_MODEL])) * _dot(yr, wbr_ref[...])
    y = _dot(z.astype(BF16), wo_ref[...])
    o_ref[0] = x_ref[0] + mod_ref[0, 0, 2:3, :] * y


def _post_call(xc, hb, hm, ya, hr, modsel, wg, mnw, rnw, wbm, wba, wbr, wo, t_off):
    B, T, D = xc.shape
    nt = T // TM - t_off

    def tile(t, b):
        return (b, t + t_off, 0)

    def head_tile(t, b):
        return (b, 0, t + t_off, 0)

    return pl.pallas_call(
        _post_kernel,
        grid=(nt, B),
        in_specs=[pl.BlockSpec((1, TM, D), tile),
                  pl.BlockSpec((1, TM, D), tile),
                  pl.BlockSpec((1, M_HEADS, TM, M_HEAD_DIM), head_tile),
                  pl.BlockSpec((1, TM // A_BLOCK, A_WIDTH, A_BLOCK), lambda t, b: (b, t + t_off, 0, 0)),
                  pl.BlockSpec((1, R_HEADS, TM, R_V_DIM), head_tile),
                  pl.BlockSpec((1, 1, 6, D), lambda t, b: (b, jnp.minimum(t + t_off, 1), 0, 0)),
                  _const_spec(wg.shape), _const_spec(mnw.shape), _const_spec(rnw.shape),
                  _const_spec(wbm.shape), _const_spec(wba.shape), _const_spec(wbr.shape),
                  _const_spec(wo.shape)],
        out_specs=pl.BlockSpec((1, TM, D), lambda t, b: (b, t, 0)),
        out_shape=jax.ShapeDtypeStruct((B, nt * TM, D), F32),
        compiler_params=_params(("parallel", "parallel")),
        name="post_merge",
    )(xc, hb, hm, ya, hr, modsel, wg, mnw, rnw, wbm, wba, wbr, wo)


def _ffn_kernel(x_ref, xp_ref, xn_ref, mod_ref, nw_ref, wa_ref, wb_ref, cw_ref, cb_ref, wd_ref, fw_ref,
                o_ref, hext_ref, u_ref, *, has_ctx, final):
    t = pl.program_id(0)
    nt = pl.num_programs(0)
    first_lat = 1 if has_ctx else 0
    sh = mod_ref[0, 0, 3:4, :]
    sc = mod_ref[0, 0, 4:5, :]
    g2 = mod_ref[0, 0, 5:6, :]
    nw = nw_ref[...]
    x = x_ref[0]
    hp = _norm_mod(xp_ref[0], nw, sc, sh)
    hn = _norm_mod(xn_ref[0], nw, sc, sh)
    hp = jnp.where(t <= first_lat, 0.0, hp)
    hn = jnp.where((t == first_lat - 1) | (t == nt - 1), 0.0, hn)
    hext_ref[0:TM, :] = _norm_mod(x, nw, sc, sh).astype(BF16)
    hext_ref[TM:, :] = jnp.concatenate([hn, hp], axis=0).astype(BF16)
    ext = TM + 2 * HALO
    for j in range(FFN_DIM // FFN_NC):
        cs = slice(j * FFN_NC, (j + 1) * FFN_NC)
        a = _dot(hext_ref[...], wa_ref[:, cs])
        b = _dot(hext_ref[0:TM, :], wb_ref[:, cs])
        cw = cw_ref[:, cs]
        prev = pltpu.roll(a, 1, 0)[0:TM]
        nxt = pltpu.roll(a, ext - 1, 0)[0:TM]
        conv = prev * cw[0:1] + a[0:TM] * cw[1:2] + nxt * cw[2:3] + cb_ref[:, cs]
        u_ref[:, cs] = (conv * _sigmoid(conv) * b).astype(BF16)
    y = x + g2 * _dot(u_ref[...], wd_ref[...])
    if final:
        ms = jnp.mean(y * y, axis=-1, keepdims=True)
        y = y * lax.rsqrt(ms + NORM_EPS) * fw_ref[...]
    o_ref[0] = y


def _ffn_call(xs, modsel, nw, wa, wb, cw, cb, wd, fw, has_ctx, final):
    B, T, D = xs.shape
    nt = T // TM
    hb16 = TM // HALO
    first_lat = 1 if has_ctx else 0
    tile = pl.BlockSpec((1, TM, D), lambda t, b: (b, t, 0))
    return pl.pallas_call(
        functools.partial(_ffn_kernel, has_ctx=has_ctx, final=final),
        grid=(nt, B),
        in_specs=[tile,
                  pl.BlockSpec((1, HALO, D), lambda t, b: (b, jnp.maximum(t * hb16 - 1, 0), 0)),
                  pl.BlockSpec((1, HALO, D), lambda t, b: (b, jnp.minimum((t + 1) * hb16, T // HALO - 1), 0)),
                  pl.BlockSpec((1, 1, 6, D), lambda t, b: (b, jnp.minimum(t + 1 - first_lat, 1), 0, 0)),
                  _const_spec(nw.shape), _const_spec(wa.shape), _const_spec(wb.shape),
                  _const_spec(cw.shape), _const_spec(cb.shape), _const_spec(wd.shape),
                  _const_spec(fw.shape)],
        out_specs=tile,
        out_shape=jax.ShapeDtypeStruct((B, T, D), F32),
        scratch_shapes=[pltpu.VMEM((TM + 2 * HALO, D), BF16), pltpu.VMEM((TM, FFN_DIM), BF16)],
        compiler_params=_params(("parallel", "parallel")),
        name="conv_ffn",
    )(xs, xs, xs, modsel, nw, wa, wb, cw, cb, wd, fw)


def _split_cols(w):
    out = []
    acc = 0
    for s in IN_SPLITS:
        out.append(w[:, acc:acc + s])
        acc += s
    return out


def _rope_tables(L, S):
    T = L + S
    j = jnp.arange(LANES)
    jj = j % 32
    inv_a = ROPE_BASE ** (-(jj % 16).astype(F32) / 16.0)
    t = jnp.arange(S)
    pos = jnp.where(((j % A_HEAD_DIM) // 32 == 0)[None, :], (t // GRID_W)[:, None], (t % GRID_W)[:, None])
    ang = pos.astype(F32) * inv_a[None, :]
    sign = jnp.where(jj < 16, -1.0, 1.0)[None, :]
    ca = jnp.concatenate([jnp.ones((L, LANES), F32), jnp.cos(ang)], axis=0)
    sa = jnp.concatenate([jnp.zeros((L, LANES), F32), jnp.sin(ang) * sign], axis=0)
    half = R_QK_DIM // 2
    inv_r = ROPE_BASE ** (-(j % half).astype(F32) / half)
    angr = jnp.arange(T).astype(F32)[:, None] * inv_r[None, :]
    signr = jnp.where(j < half, -1.0, 1.0)[None, :]
    return ca, sa, jnp.cos(angr), jnp.sin(angr) * signr


def kernel(x, c, ctx, c_ctx, mod_w, mod_b, norm1_w, norm2_w, w_in, m_gate_bias, m_conv_w, m_norm_w,
           a_sink, ret_logit, ret_norm_w, w_br_m, w_br_a, w_br_r, w_out, ffn_up, ffn_conv_w, ffn_conv_b,
           ffn_down, final_norm_w):
    B, S, D = x.shape
    L = ctx.shape[1]
    depth = mod_w.shape[0]
    assert D == D_MODEL and L == TM and S % TM == 0 and S >= 3 * A_BLOCK
    T = L + S
    n_ctx = L // CHUNK

    rows = -(-(B + 1) // 8) * 8
    cpad = jnp.zeros((rows, D), F32).at[:B].set(c).at[B].set(c_ctx)
    mods = _modulation(cpad, mod_w, mod_b)
    tabs = _rope_tables(L, S)
    xc = jnp.concatenate([ctx, x], axis=1)

    gperm = jnp.arange(4 * M_HEADS).reshape(4, M_HEADS).T.reshape(-1)

    for l in range(depth):
        last = l == depth - 1
        lat = mods[l, :B].reshape(B, 6, D)
        cm = jnp.broadcast_to(mods[l, B].reshape(1, 6, D), (B, 6, D))
        modsel = jnp.stack([cm, lat], axis=1)

        (w_mq, w_mk, w_mv, w_mo, w_mg, w_aq, w_ak, w_av,
         w_rq, w_rk, w_rv, w_rg, w_gm, w_ga, w_gr) = _split_cols(w_in[l])
        wqk = jnp.concatenate([w_mq, w_mk], axis=1).astype(BF16)
        wr = jnp.concatenate([w_mv, w_aq, w_ak, w_av, w_rq, w_rk, w_rv], axis=1).astype(BF16)
        wgt = w_mg[:, gperm].T.astype(BF16)
        gbt = jnp.broadcast_to(m_gate_bias[l][gperm].reshape(-1, 1), (4 * M_HEADS, LANES))

        (hb, mq, mk, mv, gr, aq, ak, av, rq, rk, rv) = _pre_call(
            xc, modsel, norm1_w[l].reshape(1, D), wqk, m_conv_w[l], wr, wgt, gbt, tabs)

        hm = _mlstm_call(mq, mk, mv, gr, n_ctx)
        sink_rows = jnp.repeat(a_sink[l].reshape(A_KV_HEADS, A_GROUP), A_BLOCK, axis=1)[:, None, :]
        ya = _attn_call(aq, ak, av, sink_rows.astype(F32) * LOG2E, L, not last)
        lgb = jnp.broadcast_to(ret_logit[l].T[:, :, None, None], (R_HEADS, 2, 8, R_V_DIM)).astype(F32)
        hr = _ret_call(rq, rk, rv, lgb, n_ctx)

        t_off = 1 if last else 0
        wgates = jnp.concatenate([w_mo, w_rg, w_gm, w_ga, w_gr], axis=1).astype(BF16)
        x1 = _post_call(xc, hb, hm, ya, hr, modsel, wgates, m_norm_w[l].reshape(1, -1),
                        ret_norm_w[l].reshape(1, -1), w_br_m[l].astype(BF16), w_br_a[l].astype(BF16),
                        w_br_r[l].astype(BF16), w_out[l].astype(BF16), t_off)
        xc = _ffn_call(x1, modsel, norm2_w[l].reshape(1, D), ffn_up[l][:, :FFN_DIM].astype(BF16),
                       ffn_up[l][:, FFN_DIM:].astype(BF16), ffn_conv_w[l], ffn_conv_b[l].reshape(1, -1),
                       ffn_down[l].astype(BF16), final_norm_w.reshape(1, D), not last, last)
    return xc
```

```python
import functools

import jax
import jax.numpy as jnp
from jax import lax
from jax.experimental import pallas as pl
from jax.experimental.pallas import tpu as pltpu

F32 = jnp.float32
BF16 = jnp.bfloat16

D_MODEL = 1024
GRID_W = 64
NORM_EPS = 1e-6
ROPE_BASE = 10000.0
NEG_INF = -1e30
LOG2E = 1.4426950408889634

M_HEADS = 4
M_HEAD_DIM = 128
M_WIDTH = M_HEADS * M_HEAD_DIM
A_HEADS = 8
A_KV_HEADS = 2
A_GROUP = A_HEADS // A_KV_HEADS
A_HEAD_DIM = 64
A_WIDTH = A_HEADS * A_HEAD_DIM
A_KV_WIDTH = A_KV_HEADS * A_HEAD_DIM
A_WINDOW = 128
A_BLOCK = 128
R_HEADS = 4
R_QK_DIM = 128
R_V_DIM = 256
R_QK_WIDTH = R_HEADS * R_QK_DIM
R_V_WIDTH = R_HEADS * R_V_DIM
FFN_DIM = 2816

IN_SPLITS = (M_WIDTH, M_WIDTH, M_WIDTH, M_WIDTH, 4 * M_HEADS,
             A_WIDTH, A_KV_WIDTH, A_KV_WIDTH,
             R_QK_WIDTH, R_QK_WIDTH, R_V_WIDTH, R_V_WIDTH,
             D_MODEL, D_MODEL, D_MODEL)

TM = 256
HALO = 8
CHUNK = 128
FFN_NC = 256
LANES = 128
VMEM_LIMIT = 56 * 1024 * 1024


def _sigmoid(x):
    return 1.0 / (1.0 + jnp.exp(-x))


def _log_sigmoid(x):
    return jnp.minimum(x, 0.0) - jnp.log1p(jnp.exp(-jnp.abs(x)))


def _dot(a, b):
    return jnp.dot(a, b, preferred_element_type=F32)


def _dot_nt(a, b):
    return lax.dot_general(a, b, (((1,), (1,)), ((), ())), preferred_element_type=F32)


def _dot_tn(a, b):
    return lax.dot_general(a, b, (((0,), (0,)), ((), ())), preferred_element_type=F32)


def _const_spec(shape):
    nd = len(shape)
    return pl.BlockSpec(shape, lambda *_: (0,) * nd, pipeline_mode=pl.Buffered(1))


def _params(sem):
    return pltpu.CompilerParams(dimension_semantics=sem, vmem_limit_bytes=VMEM_LIMIT)


def _mod_kernel(c_ref, w_ref, b_ref, o_ref):
    c = c_ref[...]
    s = (c * _sigmoid(c)).astype(BF16)
    o_ref[0] = _dot(s, w_ref[0].astype(BF16)) + b_ref[0]


def _modulation(cpad, mod_w, mod_b):
    depth, d, n = mod_w.shape
    rows = cpad.shape[0]
    bn = 1024
    return pl.pallas_call(
        _mod_kernel,
        grid=(depth, n // bn),
        in_specs=[pl.BlockSpec((rows, d), lambda l, j: (0, 0)),
                  pl.BlockSpec((1, d, bn), lambda l, j: (l, 0, j)),
                  pl.BlockSpec((1, 1, bn), lambda l, j: (l, 0, j))],
        out_specs=pl.BlockSpec((1, rows, bn), lambda l, j: (l, 0, j)),
        out_shape=jax.ShapeDtypeStruct((depth, rows, n), F32),
        compiler_params=_params(("parallel", "parallel")),
        name="modulation",
    )(cpad, mod_w, mod_b.reshape(depth, 1, n))


def _norm_mod(xv, nw, sc, sh):
    ms = jnp.mean(xv * xv, axis=-1, keepdims=True)
    return (xv * lax.rsqrt(ms + NORM_EPS) * nw) * (1.0 + sc) + sh


def _seg_cumsum(x, axis, reverse):
    n = x.shape[axis]
    idx = lax.broadcasted_iota(jnp.int32, x.shape, axis)
    s = 1
    while s < n:
        if reverse:
            shifted = pltpu.roll(x, n - s, axis)
            x = x + jnp.where(idx < n - s, shifted, 0.0)
        else:
            shifted = pltpu.roll(x, s, axis)
            x = x + jnp.where(idx >= s, shifted, 0.0)
        s *= 2
    return x


def _gate_transform(raw, gate_axis):
    tok_axis = 1 - gate_axis
    k = lax.broadcasted_iota(jnp.int32, raw.shape, gate_axis) % 4
    lf = _log_sigmoid(raw)
    cum_f = _seg_cumsum(jnp.where(k == 1, lf, 0.0), tok_axis, False)
    cum_b = _seg_cumsum(jnp.where(k == 3, lf, 0.0), tok_axis, True)
    bsel = jnp.where(k == 1, cum_f, jnp.where(k == 3, cum_b, 0.0))
    n = raw.shape[gate_axis]
    bnext = pltpu.roll(bsel, n - 1, gate_axis)
    nt = raw.shape[tok_axis]
    tot_f = lax.slice_in_dim(cum_f, nt - 1, nt, axis=tok_axis)
    tot_b = lax.slice_in_dim(cum_b, 0, 1, axis=tok_axis)
    total = jnp.broadcast_to(tot_f + tot_b, raw.shape)
    return jnp.where(k % 2 == 1, bsel, raw - bnext), total


def _pre_kernel(x_ref, xp_ref, xn_ref, mod_ref, nw_ref, wqk_ref, cw_ref, wr_ref, wgt_ref,
                gbt_ref, ca_ref, sa_ref, cr_ref, sr_ref,
                hb_ref, mq_ref, mk_ref, mv_ref, gr_ref, aq_ref, ak_ref, av_ref,
                rq_ref, rk_ref, rv_ref, hext_ref):
    t = pl.program_id(0)
    nt = pl.num_programs(0)
    sh = mod_ref[0, 0, 0:1, :]
    sc = mod_ref[0, 0, 1:2, :]
    nw = nw_ref[...]
    hb = _norm_mod(x_ref[0], nw, sc, sh).astype(BF16)
    hb_ref[0] = hb

    grow = _dot_nt(wgt_ref[...], hb) + gbt_ref[:, 0:1]
    for c in range(TM // CHUNK):
        gr, tot = _gate_transform(grow[:, c * CHUNK:(c + 1) * CHUNK], 0)
        aend = gr + pltpu.roll(tot, 4 * M_HEADS - 1, 0)
        amax = jnp.broadcast_to(jnp.max(aend, axis=1, keepdims=True), aend.shape)
        for h in range(M_HEADS):
            hp_, o0 = h // 2, 8 * (h % 2)
            gr_ref[0, hp_, c, o0:o0 + 4, :] = gr[4 * h:4 * h + 4]
            for d in range(2):
                gr_ref[0, hp_, c, o0 + 4 + d:o0 + 5 + d, :] = tot[4 * h + 2 * d + 1:4 * h + 2 * d + 2]
                gr_ref[0, hp_, c, o0 + 6 + d:o0 + 7 + d, :] = amax[4 * h + 2 * d:4 * h + 2 * d + 1]

    hp = _norm_mod(xp_ref[0], nw, sc, sh)
    hn = _norm_mod(xn_ref[0], nw, sc, sh)
    hp = jnp.where(t <= 1, 0.0, hp)
    hn = jnp.where((t == 0) | (t == nt - 1), 0.0, hn)
    hext_ref[0:TM, :] = hb
    hext_ref[TM:, :] = jnp.concatenate([hn, hp], axis=0).astype(BF16)
    hext = hext_ref[...]
    ext = TM + 2 * HALO

    for j, (dst, scale) in enumerate(((mq_ref, 1.0), (mk_ref, M_HEAD_DIM ** -0.5))):
        p = _dot(hext, wqk_ref[:, j * M_WIDTH:(j + 1) * M_WIDTH])
        cw = cw_ref[:, j * M_WIDTH:(j + 1) * M_WIDTH]
        prev = pltpu.roll(p, 1, 0)[0:TM]
        nxt = pltpu.roll(p, ext - 1, 0)[0:TM]
        conv = prev * cw[0:1] + p[0:TM] * cw[1:2] + nxt * cw[2:3]
        act = conv * _sigmoid(conv) * scale
        for h in range(M_HEADS):
            dst[0, h] = act[:, h * M_HEAD_DIM:(h + 1) * M_HEAD_DIM].astype(BF16)

    off = 0
    p = _dot(hb, wr_ref[:, off:off + M_WIDTH])
    off += M_WIDTH
    for h in range(M_HEADS):
        for c in range(TM // CHUNK):
            blk = p[c * CHUNK:(c + 1) * CHUNK, h * M_HEAD_DIM:(h + 1) * M_HEAD_DIM]
            mv_ref[0, h, c] = blk.T.astype(BF16)

    lane = lax.broadcasted_iota(jnp.int32, (TM, LANES), 1)
    first16 = (lane % 32) < 16
    ca = ca_ref[...]
    sa = sa_ref[...]

    def rope_a(xs):
        partner = jnp.where(first16, pltpu.roll(xs, LANES - 16, 1), pltpu.roll(xs, 16, 1))
        return xs * ca + partner * sa

    p = _dot(hb, wr_ref[:, off:off + A_WIDTH])
    off += A_WIDTH
    for s in range(A_WIDTH // LANES):
        r = (rope_a(p[:, s * LANES:(s + 1) * LANES]) * (A_HEAD_DIM ** -0.5 * LOG2E)).astype(BF16)
        aq_ref[0, 2 * s] = r[:, 0:A_HEAD_DIM]
        aq_ref[0, 2 * s + 1] = r[:, A_HEAD_DIM:]
    p = _dot(hb, wr_ref[:, off:off + 2 * A_KV_WIDTH])
    off += 2 * A_KV_WIDTH
    r = rope_a(p[:, 0:LANES]).astype(BF16)
    for h in range(A_KV_HEADS):
        ak_ref[0, h] = r[:, h * A_HEAD_DIM:(h + 1) * A_HEAD_DIM]
    pv = p[:, LANES:]
    for h in range(A_KV_HEADS):
        vh = pv if h == 0 else pltpu.roll(pv, A_HEAD_DIM, 1)
        ve = jnp.where(lane < A_HEAD_DIM, vh, 1.0)
        for c in range(TM // A_BLOCK):
            av_ref[0, h, c] = ve[c * A_BLOCK:(c + 1) * A_BLOCK].T.astype(BF16)

    cr = cr_ref[...]
    sr = sr_ref[...]
    for is_k in (False, True):
        p = _dot(hb, wr_ref[:, off:off + R_QK_WIDTH])
        off += R_QK_WIDTH
        for h in range(R_HEADS):
            xs = p[:, h * R_QK_DIM:(h + 1) * R_QK_DIM]
            rot = xs * cr + pltpu.roll(xs, R_QK_DIM // 2, 1) * sr
            if is_k:
                rot = rot * R_QK_DIM ** -0.5
                for c in range(TM // CHUNK):
                    rk_ref[0, h, c] = rot[c * CHUNK:(c + 1) * CHUNK].T.astype(BF16)
            else:
                rq_ref[0, h] = rot.astype(BF16)
    p = _dot(hb, wr_ref[:, off:off + R_V_WIDTH])
    for h in range(R_HEADS):
        rv_ref[0, h] = p[:, h * R_V_DIM:(h + 1) * R_V_DIM].astype(BF16)


def _pre_call(xc, modsel, nw, wqk, cw, wr, wgt, gbt, tabs):
    B, T, D = xc.shape
    nt = T // TM
    hb16 = TM // HALO
    nch = T // CHUNK

    def tile(t, b):
        return (b, t, 0)

    def head_tile(t, b):
        return (b, 0, t, 0)

    in_specs = [
        pl.BlockSpec((1, TM, D), tile),
        pl.BlockSpec((1, HALO, D), lambda t, b: (b, jnp.maximum(t * hb16 - 1, 0), 0)),
        pl.BlockSpec((1, HALO, D), lambda t, b: (b, jnp.minimum((t + 1) * hb16, T // HALO - 1), 0)),
        pl.BlockSpec((1, 1, 6, D), lambda t, b: (b, jnp.minimum(t, 1), 0, 0)),
        _const_spec(nw.shape), _const_spec(wqk.shape), _const_spec(cw.shape), _const_spec(wr.shape),
        _const_spec(wgt.shape), _const_spec(gbt.shape),
    ] + [pl.BlockSpec((TM, LANES), lambda t, b: (t, 0)) for _ in range(4)]
    out_shape = [
        jax.ShapeDtypeStruct((B, T, D), BF16),
        jax.ShapeDtypeStruct((B, M_HEADS, T, M_HEAD_DIM), BF16),
        jax.ShapeDtypeStruct((B, M_HEADS, T, M_HEAD_DIM), BF16),
        jax.ShapeDtypeStruct((B, M_HEADS, nch, M_HEAD_DIM, CHUNK), BF16),
        jax.ShapeDtypeStruct((B, 2, nch, 16, CHUNK), F32),
        jax.ShapeDtypeStruct((B, A_HEADS, T, A_HEAD_DIM), BF16),
        jax.ShapeDtypeStruct((B, A_KV_HEADS, T, A_HEAD_DIM), BF16),
        jax.ShapeDtypeStruct((B, A_KV_HEADS, T // A_BLOCK, LANES, A_BLOCK), BF16),
        jax.ShapeDtypeStruct((B, R_HEADS, T, R_QK_DIM), BF16),
        jax.ShapeDtypeStruct((B, R_HEADS, nch, R_QK_DIM, CHUNK), BF16),
        jax.ShapeDtypeStruct((B, R_HEADS, T, R_V_DIM), BF16),
    ]
    out_specs = [
        pl.BlockSpec((1, TM, D), tile),
        pl.BlockSpec((1, M_HEADS, TM, M_HEAD_DIM), head_tile),
        pl.BlockSpec((1, M_HEADS, TM, M_HEAD_DIM), head_tile),
        pl.BlockSpec((1, M_HEADS, TM // CHUNK, M_HEAD_DIM, CHUNK), lambda t, b: (b, 0, t, 0, 0)),
        pl.BlockSpec((1, 2, TM // CHUNK, 16, CHUNK), lambda t, b: (b, 0, t, 0, 0)),
        pl.BlockSpec((1, A_HEADS, TM, A_HEAD_DIM), head_tile),
        pl.BlockSpec((1, A_KV_HEADS, TM, A_HEAD_DIM), head_tile),
        pl.BlockSpec((1, A_KV_HEADS, TM // A_BLOCK, LANES, A_BLOCK), lambda t, b: (b, 0, t, 0, 0)),
        pl.BlockSpec((1, R_HEADS, TM, R_QK_DIM), head_tile),
        pl.BlockSpec((1, R_HEADS, TM // CHUNK, R_QK_DIM, CHUNK), lambda t, b: (b, 0, t, 0, 0)),
        pl.BlockSpec((1, R_HEADS, TM, R_V_DIM), head_tile),
    ]
    return pl.pallas_call(
        _pre_kernel,
        grid=(nt, B),
        in_specs=in_specs,
        out_specs=out_specs,
        out_shape=out_shape,
        scratch_shapes=[pltpu.VMEM((TM + 2 * HALO, D), BF16)],
        compiler_params=_params(("parallel", "parallel")),
        name="pre_proj",
    )(xc, xc, xc, modsel, nw, wqk, cw, wr, wgt, gbt, *tabs)


def _bwd_chunk(i, n_ctx, n_all):
    if isinstance(i, int):
        return n_ctx - 1 - i if i < n_ctx else n_all - 1 + n_ctx - i
    return n_all - 1 + n_ctx - i


def _chunk_start(c):
    return c * CHUNK if isinstance(c, int) else pl.multiple_of(c * CHUNK, CHUNK)


def _chunk_loop(body, n_ctx, n_all, unroll):
    for i in range(n_ctx):
        body(i, 0)
    n_lat = n_all - n_ctx
    while n_lat % unroll:
        unroll //= 2
    lax.fori_loop(n_ctx, n_all, body, 0, unroll=unroll)


def _mlstm_kernel(q_ref, k_ref, vt_ref, gr_ref, o_ref, c_ref, n_ref, m_ref,
                  cs_ref, ns_ref, ms_ref, qp_ref, nq_ref, *, n_ctx):
    n_all = gr_ref.shape[2]
    dh = q_ref.shape[3]
    c_ref[...] = jnp.zeros(c_ref.shape, F32)
    n_ref[...] = jnp.zeros(n_ref.shape, F32)
    m_ref[...] = jnp.zeros(m_ref.shape, F32)

    def scan_unit(c, hh, d):
        u = 2 * hh + d
        c0 = _chunk_start(c)
        k = k_ref[0, hh, pl.ds(c0, CHUNK), :]
        vt = vt_ref[0, hh, c]
        rows = gr_ref[0, 0, c, 8 * hh:8 * hh + 8, :]
        tot = rows[4 + d:5 + d]
        aend = rows[2 * d:2 * d + 1] + tot
        amax = rows[6 + d:7 + d]
        cst = c_ref[u]
        nv = n_ref[u]
        m = m_ref[u]
        cs_ref[u, c] = cst.astype(BF16)
        ns_ref[u, c] = jnp.broadcast_to(nv, (16, dh)).astype(BF16)
        ms_ref[u, c] = jnp.broadcast_to(m, (8, CHUNK))
        m_new = jnp.maximum(tot + m, amax)
        w = jnp.exp(aend - m_new)
        dec = jnp.exp(tot + m - m_new)
        vw = (vt.astype(F32) * w).astype(BF16)
        c_ref[u] = dec * cst + _dot(vw, k)
        wk = _dot(jnp.broadcast_to(w, (16, CHUNK)).astype(BF16), k)
        n_ref[u] = dec * nv + wk[0:1]
        m_ref[u] = m_new

    def scan_step(i, carry):
        cb = _bwd_chunk(i, n_ctx, n_all)
        for hh in range(2):
            scan_unit(i, hh, 0)
            scan_unit(cb, hh, 1)
        return carry

    _chunk_loop(scan_step, n_ctx, n_all, 4)

    row = lax.broadcasted_iota(jnp.int32, (CHUNK, CHUNK), 0)
    col = lax.broadcasted_iota(jnp.int32, (CHUNK, CHUNK), 1)
    tri = (row <= col, row >= col)

    def query_products(c, slot):
        c0 = _chunk_start(c)
        for hh in range(2):
            q = q_ref[0, hh, pl.ds(c0, CHUNK), :]
            k = k_ref[0, hh, pl.ds(c0, CHUNK), :]
            qp_ref[slot, hh, 0] = _dot_nt(k, q)
            for d in range(2):
                u = 2 * hh + d
                qp_ref[slot, hh, 1 + d] = _dot_nt(cs_ref[u, c], q)
                nq_ref[slot, u] = _dot_nt(ns_ref[u, c], q)[0:8]

    def outputs(c, slot):
        c0 = _chunk_start(c)
        gcv = jnp.concatenate([gr_ref[0, 0, c], jnp.zeros((CHUNK - 16, CHUNK), F32)], axis=0).T
        for hh in range(2):
            vt = vt_ref[0, hh, c]
            rows = gr_ref[0, 0, c, 8 * hh:8 * hh + 8, :]
            st = qp_ref[slot, hh, 0]
            acc = None
            for d in range(2):
                u = 2 * hh + d
                a0c = gcv[:, 8 * hh + 2 * d:8 * hh + 2 * d + 1]
                b = rows[2 * d + 1:2 * d + 2]
                m = ms_ref[u, c, 0:1, :]
                dlog = jnp.where(tri[d], a0c + b, NEG_INF)
                mj = jnp.maximum(b + m, jnp.max(dlog, axis=0, keepdims=True))
                sd = st * jnp.exp(dlog - mj)
                iw = jnp.exp(b + m - mj)
                den = iw * nq_ref[slot, u, 0:1, :] + jnp.sum(sd, axis=0, keepdims=True)
                inv = 1.0 / jnp.maximum(jnp.abs(den), jnp.exp(-mj))
                ht = (_dot(vt, sd.astype(BF16)) + iw * qp_ref[slot, hh, 1 + d]) * inv
                acc = ht if acc is None else acc + ht
            o_ref[0, hh, pl.ds(c0, CHUNK), :] = acc.T

    query_products(0, 0)

    def out_pair(i, carry):
        c = 2 * i
        query_products(c + 1, 1)
        outputs(c, 0)
        query_products(jnp.minimum(c + 2, n_all - 1), 0)
        outputs(c + 1, 1)
        return carry

    n_pairs = n_all // 2
    unroll = 4
    first = n_pairs % unroll
    for i in range(first):
        out_pair(i, 0)
    lax.fori_loop(first, n_pairs, out_pair, 0, unroll=unroll)


def _mlstm_call(mq, mk, mvt, gr, n_ctx):
    B, H, T, dh = mq.shape
    nch = T // CHUNK
    qspec = pl.BlockSpec((1, 2, T, dh), lambda b, p: (b, p, 0, 0))
    return pl.pallas_call(
        functools.partial(_mlstm_kernel, n_ctx=n_ctx),
        grid=(B, H // 2),
        in_specs=[qspec, qspec,
                  pl.BlockSpec((1, 2, nch, dh, CHUNK), lambda b, p: (b, p, 0, 0, 0)),
                  pl.BlockSpec((1, 1, nch, 16, CHUNK), lambda b, p: (b, p, 0, 0, 0))],
        out_specs=pl.BlockSpec((1, 2, T, dh), lambda b, p: (b, p, 0, 0)),
        out_shape=jax.ShapeDtypeStruct((B, H, T, dh), F32),
        scratch_shapes=[pltpu.VMEM((4, dh, dh), F32), pltpu.VMEM((4, 1, dh), F32),
                        pltpu.VMEM((4, 1, CHUNK), F32),
                        pltpu.VMEM((4, nch, dh, dh), BF16), pltpu.VMEM((4, nch, 16, dh), BF16),
                        pltpu.VMEM((4, nch, 8, CHUNK), F32),
                        pltpu.VMEM((2, 2, 3, dh, CHUNK), F32), pltpu.VMEM((2, 4, 8, CHUNK), F32)],
        compiler_params=_params(("parallel", "parallel")),
        name="mlstm",
    )(mq, mk, mvt, gr)


def _ret_kernel(q_ref, kt_ref, v_ref, lg_ref, o_ref, st_ref, sts_ref, dec_ref, ib_ref, wb_ref, *, n_ctx):
    n_all = kt_ref.shape[2]
    dk = q_ref.shape[3]
    dv = v_ref.shape[3]
    st_ref[...] = jnp.zeros(st_ref.shape, F32)
    row = lax.broadcasted_iota(jnp.int32, (CHUNK, CHUNK), 0).astype(F32)
    col = lax.broadcasted_iota(jnp.int32, (CHUNK, CHUNK), 1).astype(F32)
    cds = []
    dec = None
    for d in range(2):
        lg = _log_sigmoid(lg_ref[0, d])[0:1]
        lgk = lg[:, 0:CHUNK]
        diff = row - col if d == 0 else col - row
        dd = jnp.where(diff >= 0, jnp.exp(lgk * jnp.maximum(diff, 0.0)), 0.0)
        dec = dd if dec is None else dec + dd
        pin = row if d == 0 else CHUNK - 1.0 - row
        ib_ref[d] = jnp.exp(lg * (jnp.concatenate([pin, pin], axis=1) + 1.0))
        pkey = col if d == 0 else CHUNK - 1.0 - col
        wb_ref[d] = jnp.exp(lgk * (CHUNK - 1.0 - pkey))
        cds.append(jnp.exp(lg * CHUNK))
    dec_ref[...] = dec

    def scan_step(i, carry):
        cb = _bwd_chunk(i, n_ctx, n_all)
        for d, c in ((0, i), (1, cb)):
            c0 = _chunk_start(c)
            st = st_ref[d]
            sts_ref[c, :, d * dv:(d + 1) * dv] = st.astype(BF16)
            kw = (kt_ref[0, 0, c].astype(F32) * wb_ref[d]).astype(BF16)
            st_ref[d] = cds[d] * st + _dot(kw, v_ref[0, 0, pl.ds(c0, CHUNK), :])
        return carry

    _chunk_loop(scan_step, n_ctx, n_all, 8)

    def out_step(c, carry):
        c0 = _chunk_start(c)
        q = q_ref[0, 0, pl.ds(c0, CHUNK), :]
        s = (_dot(q, kt_ref[0, 0, c]) * dec_ref[...]).astype(BF16)
        qst = _dot(q, sts_ref[c])
        o_ref[0, 0, pl.ds(c0, CHUNK), :] = (_dot(s, v_ref[0, 0, pl.ds(c0, CHUNK), :])
                                            + ib_ref[0] * qst[:, 0:dv] + ib_ref[1] * qst[:, dv:])
        return carry

    _chunk_loop(out_step, n_ctx, n_all, 8)


def _ret_call(rq, rkt, rv, lgb, n_ctx):
    B, H, T, dk = rq.shape
    dv = rv.shape[-1]
    nch = T // CHUNK
    return pl.pallas_call(
        functools.partial(_ret_kernel, n_ctx=n_ctx),
        grid=(B, H),
        in_specs=[pl.BlockSpec((1, 1, T, dk), lambda b, h: (b, h, 0, 0)),
                  pl.BlockSpec((1, 1, nch, dk, CHUNK), lambda b, h: (b, h, 0, 0, 0)),
                  pl.BlockSpec((1, 1, T, dv), lambda b, h: (b, h, 0, 0)),
                  pl.BlockSpec((1, 2, 8, dv), lambda b, h: (h, 0, 0, 0))],
        out_specs=pl.BlockSpec((1, 1, T, dv), lambda b, h: (b, h, 0, 0)),
        out_shape=jax.ShapeDtypeStruct((B, H, T, dv), F32),
        scratch_shapes=[pltpu.VMEM((2, dk, dv), F32), pltpu.VMEM((nch, dk, 2 * dv), BF16),
                        pltpu.VMEM((CHUNK, CHUNK), F32), pltpu.VMEM((2, CHUNK, dv), F32),
                        pltpu.VMEM((2, dk, CHUNK), F32)],
        compiler_params=_params(("parallel", "parallel")),
        name="retention",
    )(rq, rkt, rv, lgb)


def _attn_kernel(q_ref, k_ref, vt_ref, sink_ref, o_ref, bias_ref, s_ref, *, n_ctx_tok, need_ctx):
    T = q_ref.shape[2]
    L = n_ctx_tok
    S = T - L
    nb = S // A_BLOCK
    nspan = 3
    span = nspan * A_BLOCK
    ncb = L // A_BLOCK
    cols = A_GROUP * A_BLOCK
    kc = k_ref[0, 0, 0:L, :]
    vtc = jnp.concatenate([vt_ref[0, 0, j] for j in range(ncb)], axis=1)
    sink = sink_ref[0]

    kpos = lax.broadcasted_iota(jnp.int32, (span, A_BLOCK), 0)
    qpos = lax.broadcasted_iota(jnp.int32, (span, A_BLOCK), 1)
    for delta in range(nspan):
        bias_ref[delta] = jnp.where(jnp.abs(qpos + delta * A_BLOCK - kpos) <= A_WINDOW, 0.0, NEG_INF)

    def finish(blk, acc, m):
        den = acc[A_HEAD_DIM:A_HEAD_DIM + 1] + jnp.exp2(sink - m)
        ot = (acc[0:A_HEAD_DIM] * (1.0 / den)).astype(BF16)
        for g in range(A_GROUP):
            o_ref[0, blk, g * A_HEAD_DIM:(g + 1) * A_HEAD_DIM, :] = ot[:, g * A_BLOCK:(g + 1) * A_BLOCK]

    for cbk in range(ncb):
        if need_ctx:
            qs = q_ref[0, :, cbk * A_BLOCK:(cbk + 1) * A_BLOCK, :].reshape(cols, A_HEAD_DIM)
            s = _dot_nt(kc, qs)
            m = jnp.maximum(jnp.max(s, axis=0, keepdims=True), sink)
            finish(cbk, _dot(vtc, jnp.exp2(s - m).astype(BF16)), m)
        else:
            o_ref[0, cbk] = jnp.zeros(o_ref.shape[2:], BF16)

    def first_key_block(n):
        return jnp.clip(n - 1, 0, nb - nspan)

    def scores(n, slot):
        q0 = pl.multiple_of(L + n * A_BLOCK, A_BLOCK)
        kb = first_key_block(n)
        k0 = pl.multiple_of(L + kb * A_BLOCK, A_BLOCK)
        qs = q_ref[0, :, pl.ds(q0, A_BLOCK), :].reshape(cols, A_HEAD_DIM)
        bias = bias_ref[n - kb]
        s_ref[slot, 0:span, :] = (_dot_nt(k_ref[0, 0, pl.ds(k0, span), :], qs)
                                  + jnp.concatenate([bias] * A_GROUP, axis=1))
        s_ref[slot, span:, :] = _dot_nt(kc, qs)

    def softmax_pv(n, slot):
        kb = first_key_block(n)
        s = s_ref[slot]
        m = jnp.maximum(jnp.max(s, axis=0, keepdims=True), sink)
        p = jnp.exp2(s - m).astype(BF16)
        vt = jnp.concatenate([vt_ref[0, 0, ncb + kb + j] for j in range(nspan)] + [vtc], axis=1)
        finish(ncb + n, _dot(vt, p), m)

    scores(0, 0)

    def block_pair(i, carry):
        n = 2 * i
        scores(n + 1, 1)
        softmax_pv(n, 0)
        scores(jnp.minimum(n + 2, nb - 1), 0)
        softmax_pv(n + 1, 1)
        return carry

    n_pairs = nb // 2
    first = n_pairs % 2
    if first:
        block_pair(0, 0)
    lax.fori_loop(first, n_pairs, block_pair, 0, unroll=2)


def _attn_call(aq, ak, avt, sink_rows, n_ctx_tok, need_ctx):
    B, _, T, hd = aq.shape
    nblk = T // A_BLOCK
    cols = A_GROUP * A_BLOCK
    return pl.pallas_call(
        functools.partial(_attn_kernel, n_ctx_tok=n_ctx_tok, need_ctx=need_ctx),
        grid=(B, A_KV_HEADS),
        in_specs=[pl.BlockSpec((1, A_GROUP, T, hd), lambda b, h: (b, h, 0, 0)),
                  pl.BlockSpec((1, 1, T, hd), lambda b, h: (b, h, 0, 0)),
                  pl.BlockSpec((1, 1, nblk, LANES, A_BLOCK), lambda b, h: (b, h, 0, 0, 0)),
                  pl.BlockSpec((1, 1, cols), lambda b, h: (h, 0, 0))],
        out_specs=pl.BlockSpec((1, nblk, A_GROUP * hd, A_BLOCK), lambda b, h: (b, 0, h, 0)),
        out_shape=jax.ShapeDtypeStruct((B, nblk, A_WIDTH, A_BLOCK), BF16),
        scratch_shapes=[pltpu.VMEM((3, 3 * A_BLOCK, A_BLOCK), F32),
                        pltpu.VMEM((2, 3 * A_BLOCK + n_ctx_tok, cols), F32)],
        compiler_params=_params(("parallel", "parallel")),
        name="window_attn",
    )(aq, ak, avt, sink_rows)


def _head_norm(y):
    mu = jnp.mean(y, axis=-1, keepdims=True)
    yc = y - mu
    var = jnp.mean(yc * yc, axis=-1, keepdims=True)
    return yc * lax.rsqrt(var + NORM_EPS)


def _post_kernel(x_ref, hb_ref, hm_ref, ya_ref, hr_ref, mod_ref, wg_ref, mnw_ref, rnw_ref,
                 wbm_ref, wba_ref, wbr_ref, wo_ref, o_ref):
    hb = hb_ref[0]
    off = 0
    hm = jnp.concatenate([_head_norm(hm_ref[0, h]) for h in range(M_HEADS)], axis=1) * mnw_ref[...]
    ym = (_sigmoid(_dot(hb, wg_ref[:, off:off + M_WIDTH])) * hm).astype(BF16)
    off += M_WIDTH
    hr = jnp.concatenate([_head_norm(hr_ref[0, h]) for h in range(R_HEADS)], axis=1) * rnw_ref[...]
    rg = _dot(hb, wg_ref[:, off:off + R_V_WIDTH])
    yr = (rg * _sigmoid(rg) * hr).astype(BF16)
    off += R_V_WIDTH
    z = _sigmoid(_dot(hb, wg_ref[:, off:off + D_MODEL])) * _dot(ym, wbm_ref[...])
    off += D_MODEL
    pa = jnp.concatenate([_dot_tn(ya_ref[0, c], wba_ref[...]) for c in range(TM // A_BLOCK)], axis=0)
    z = z + _sigmoid(_dot(hb, wg_ref[:, off:off + D_MODEL])) * pa
    off += D_MODEL
    z = z + _sigmoid(_dot(hb, wg_ref[:, off:off + D_MODEL])) * _dot(yr, wbr_ref[...])
    y = _dot(z.astype(BF16), wo_ref[...])
    o_ref[0] = x_ref[0] + mod_ref[0, 0, 2:3, :] * y


def _post_call(xc, hb, hm, ya, hr, modsel, wg, mnw, rnw, wbm, wba, wbr, wo, t_off):
    B, T, D = xc.shape
    nt = T // TM - t_off

    def tile(t, b):
        return (b, t + t_off, 0)

    def head_tile(t, b):
        return (b, 0, t + t_off, 0)

    return pl.pallas_call(
        _post_kernel,
        grid=(nt, B),
        in_specs=[pl.BlockSpec((1, TM, D), tile),
                  pl.BlockSpec((1, TM, D), tile),
                  pl.BlockSpec((1, M_HEADS, TM, M_HEAD_DIM), head_tile),
                  pl.BlockSpec((1, TM // A_BLOCK, A_WIDTH, A_BLOCK), lambda t, b: (b, t + t_off, 0, 0)),
                  pl.BlockSpec((1, R_HEADS, TM, R_V_DIM), head_tile),
                  pl.BlockSpec((1, 1, 6, D), lambda t, b: (b, jnp.minimum(t + t_off, 1), 0, 0)),
                  _const_spec(wg.shape), _const_spec(mnw.shape), _const_spec(rnw.shape),
                  _const_spec(wbm.shape), _const_spec(wba.shape), _const_spec(wbr.shape),
                  _const_spec(wo.shape)],
        out_specs=pl.BlockSpec((1, TM, D), lambda t, b: (b, t, 0)),
        out_shape=jax.ShapeDtypeStruct((B, nt * TM, D), F32),
        compiler_params=_params(("parallel", "parallel")),
        name="post_merge",
    )(xc, hb, hm, ya, hr, modsel, wg, mnw, rnw, wbm, wba, wbr, wo)


def _ffn_kernel(x_ref, xp_ref, xn_ref, mod_ref, nw_ref, wa_ref, wb_ref, cw_ref, cb_ref, wd_ref, fw_ref,
                o_ref, hext_ref, u_ref, *, has_ctx, final):
    t = pl.program_id(0)
    nt = pl.num_programs(0)
    first_lat = 1 if has_ctx else 0
    sh = mod_ref[0, 0, 3:4, :]
    sc = mod_ref[0, 0, 4:5, :]
    g2 = mod_ref[0, 0, 5:6, :]
    nw = nw_ref[...]
    x = x_ref[0]
    hp = _norm_mod(xp_ref[0], nw, sc, sh)
    hn = _norm_mod(xn_ref[0], nw, sc, sh)
    hp = jnp.where(t <= first_lat, 0.0, hp)
    hn = jnp.where((t == first_lat - 1) | (t == nt - 1), 0.0, hn)
    hext_ref[0:TM, :] = _norm_mod(x, nw, sc, sh).astype(BF16)
    hext_ref[TM:, :] = jnp.concatenate([hn, hp], axis=0).astype(BF16)
    ext = TM + 2 * HALO
    for j in range(FFN_DIM // FFN_NC):
        cs = slice(j * FFN_NC, (j + 1) * FFN_NC)
        a = _dot(hext_ref[...], wa_ref[:, cs])
        b = _dot(hext_ref[0:TM, :], wb_ref[:, cs])
        cw = cw_ref[:, cs]
        prev = pltpu.roll(a, 1, 0)[0:TM]
        nxt = pltpu.roll(a, ext - 1, 0)[0:TM]
        conv = prev * cw[0:1] + a[0:TM] * cw[1:2] + nxt * cw[2:3] + cb_ref[:, cs]
        u_ref[:, cs] = (conv * _sigmoid(conv) * b).astype(BF16)
    y = x + g2 * _dot(u_ref[...], wd_ref[...])
    if final:
        ms = jnp.mean(y * y, axis=-1, keepdims=True)
        y = y * lax.rsqrt(ms + NORM_EPS) * fw_ref[...]
    o_ref[0] = y


def _ffn_call(xs, modsel, nw, wa, wb, cw, cb, wd, fw, has_ctx, final):
    B, T, D = xs.shape
    nt = T // TM
    hb16 = TM // HALO
    first_lat = 1 if has_ctx else 0
    tile = pl.BlockSpec((1, TM, D), lambda t, b: (b, t, 0))
    return pl.pallas_call(
        functools.partial(_ffn_kernel, has_ctx=has_ctx, final=final),
        grid=(nt, B),
        in_specs=[tile,
                  pl.BlockSpec((1, HALO, D), lambda t, b: (b, jnp.maximum(t * hb16 - 1, 0), 0)),
                  pl.BlockSpec((1, HALO, D), lambda t, b: (b, jnp.minimum((t + 1) * hb16, T // HALO - 1), 0)),
                  pl.BlockSpec((1, 1, 6, D), lambda t, b: (b, jnp.minimum(t + 1 - first_lat, 1), 0, 0)),
                  _const_spec(nw.shape), _const_spec(wa.shape), _const_spec(wb.shape),
                  _const_spec(cw.shape), _const_spec(cb.shape), _const_spec(wd.shape),
                  _const_spec(fw.shape)],
        out_specs=tile,
        out_shape=jax.ShapeDtypeStruct((B, T, D), F32),
        scratch_shapes=[pltpu.VMEM((TM + 2 * HALO, D), BF16), pltpu.VMEM((TM, FFN_DIM), BF16)],
        compiler_params=_params(("parallel", "parallel")),
        name="conv_ffn",
    )(xs, xs, xs, modsel, nw, wa, wb, cw, cb, wd, fw)


def _split_cols(w):
    out = []
    acc = 0
    for s in IN_SPLITS:
        out.append(w[:, acc:acc + s])
        acc += s
    return out


def _rope_tables(L, S):
    T = L + S
    j = jnp.arange(LANES)
    jj = j % 32
    inv_a = ROPE_BASE ** (-(jj % 16).astype(F32) / 16.0)
    t = jnp.arange(S)
    pos = jnp.where(((j % A_HEAD_DIM) // 32 == 0)[None, :], (t // GRID_W)[:, None], (t % GRID_W)[:, None])
    ang = pos.astype(F32) * inv_a[None, :]
    sign = jnp.where(jj < 16, -1.0, 1.0)[None, :]
    ca = jnp.concatenate([jnp.ones((L, LANES), F32), jnp.cos(ang)], axis=0)
    sa = jnp.concatenate([jnp.zeros((L, LANES), F32), jnp.sin(ang) * sign], axis=0)
    half = R_QK_DIM // 2
    inv_r = ROPE_BASE ** (-(j % half).astype(F32) / half)
    angr = jnp.arange(T).astype(F32)[:, None] * inv_r[None, :]
    signr = jnp.where(j < half, -1.0, 1.0)[None, :]
    return ca, sa, jnp.cos(angr), jnp.sin(angr) * signr


def kernel(x, c, ctx, c_ctx, mod_w, mod_b, norm1_w, norm2_w, w_in, m_gate_bias, m_conv_w, m_norm_w,
           a_sink, ret_logit, ret_norm_w, w_br_m, w_br_a, w_br_r, w_out, ffn_up, ffn_conv_w, ffn_conv_b,
           ffn_down, final_norm_w):
    B, S, D = x.shape
    L = ctx.shape[1]
    depth = mod_w.shape[0]
    assert D == D_MODEL and L == TM and S % TM == 0 and S >= 3 * A_BLOCK
    T = L + S
    n_ctx = L // CHUNK

    rows = -(-(B + 1) // 8) * 8
    cpad = jnp.zeros((rows, D), F32).at[:B].set(c).at[B].set(c_ctx)
    mods = _modulation(cpad, mod_w, mod_b)
    tabs = _rope_tables(L, S)
    xc = jnp.concatenate([ctx, x], axis=1)

    gperm = jnp.arange(4 * M_HEADS).reshape(4, M_HEADS).T.reshape(-1)

    for l in range(depth):
        last = l == depth - 1
        lat = mods[l, :B].reshape(B, 6, D)
        cm = jnp.broadcast_to(mods[l, B].reshape(1, 6, D), (B, 6, D))
        modsel = jnp.stack([cm, lat], axis=1)

        (w_mq, w_mk, w_mv, w_mo, w_mg, w_aq, w_ak, w_av,
         w_rq, w_rk, w_rv, w_rg, w_gm, w_ga, w_gr) = _split_cols(w_in[l])
        wqk = jnp.concatenate([w_mq, w_mk], axis=1).astype(BF16)
        wr = jnp.concatenate([w_mv, w_aq, w_ak, w_av, w_rq, w_rk, w_rv], axis=1).astype(BF16)
        wgt = w_mg[:, gperm].T.astype(BF16)
        gbt = jnp.broadcast_to(m_gate_bias[l][gperm].reshape(-1, 1), (4 * M_HEADS, LANES))

        (hb, mq, mk, mv, gr, aq, ak, av, rq, rk, rv) = _pre_call(
            xc, modsel, norm1_w[l].reshape(1, D), wqk, m_conv_w[l], wr, wgt, gbt, tabs)

        hm = _mlstm_call(mq, mk, mv, gr, n_ctx)
        sink_rows = jnp.repeat(a_sink[l].reshape(A_KV_HEADS, A_GROUP), A_BLOCK, axis=1)[:, None, :]
        ya = _attn_call(aq, ak, av, sink_rows.astype(F32) * LOG2E, L, not last)
        lgb = jnp.broadcast_to(ret_logit[l].T[:, :, None, None], (R_HEADS, 2, 8, R_V_DIM)).astype(F32)
        hr = _ret_call(rq, rk, rv, lgb, n_ctx)

        t_off = 1 if last else 0
        wgates = jnp.concatenate([w_mo, w_rg, w_gm, w_ga, w_gr], axis=1).astype(BF16)
        x1 = _post_call(xc, hb, hm, ya, hr, modsel, wgates, m_norm_w[l].reshape(1, -1),
                        ret_norm_w[l].reshape(1, -1), w_br_m[l].astype(BF16), w_br_a[l].astype(BF16),
                        w_br_r[l].astype(BF16), w_out[l].astype(BF16), t_off)
        xc = _ffn_call(x1, modsel, norm2_w[l].reshape(1, D), ffn_up[l][:, :FFN_DIM].astype(BF16),
                       ffn_up[l][:, FFN_DIM:].astype(BF16), ffn_conv_w[l], ffn_conv_b[l].reshape(1, -1),
                       ffn_down[l].astype(BF16), final_norm_w.reshape(1, D), not last, last)
    return xc
```

```python
import functools

import jax
import jax.numpy as jnp
from jax import lax
from jax.experimental import pallas as pl
from jax.experimental.pallas import tpu as pltpu

F32 = jnp.float32
BF16 = jnp.bfloat16

D_MODEL = 1024
GRID_W = 64
NORM_EPS = 1e-6
ROPE_BASE = 10000.0
NEG_INF = -1e30
LOG2E = 1.4426950408889634

M_HEADS = 4
M_HEAD_DIM = 128
M_WIDTH = M_HEADS * M_HEAD_DIM
A_HEADS = 8
A_KV_HEADS = 2
A_GROUP = A_HEADS // A_KV_HEADS
A_HEAD_DIM = 64
A_WIDTH = A_HEADS * A_HEAD_DIM
A_KV_WIDTH = A_KV_HEADS * A_HEAD_DIM
A_WINDOW = 128
A_BLOCK = 128
R_HEADS = 4
R_QK_DIM = 128
R_V_DIM = 256
R_QK_WIDTH = R_HEADS * R_QK_DIM
R_V_WIDTH = R_HEADS * R_V_DIM
FFN_DIM = 2816

IN_SPLITS = (M_WIDTH, M_WIDTH, M_WIDTH, M_WIDTH, 4 * M_HEADS,
             A_WIDTH, A_KV_WIDTH, A_KV_WIDTH,
             R_QK_WIDTH, R_QK_WIDTH, R_V_WIDTH, R_V_WIDTH,
             D_MODEL, D_MODEL, D_MODEL)

TM = 256
BB = 2
HALO = 8
CHUNK = 128
FFN_NC = 256
LANES = 128
VMEM_LIMIT = 56 * 1024 * 1024


def _sigmoid(x):
    return 1.0 / (1.0 + jnp.exp(-x))


def _log_sigmoid(x):
    return jnp.minimum(x, 0.0) - jnp.log1p(jnp.exp(-jnp.abs(x)))


def _dot(a, b):
    return jnp.dot(a, b, preferred_element_type=F32)


def _dot_nt(a, b):
    return lax.dot_general(a, b, (((1,), (1,)), ((), ())), preferred_element_type=F32)


def _dot_tn(a, b):
    return lax.dot_general(a, b, (((0,), (0,)), ((), ())), preferred_element_type=F32)


def _const_spec(shape):
    nd = len(shape)
    return pl.BlockSpec(shape, lambda *_: (0,) * nd, pipeline_mode=pl.Buffered(1))


def _params(sem):
    return pltpu.CompilerParams(dimension_semantics=sem, vmem_limit_bytes=VMEM_LIMIT)


def _mod_kernel(c_ref, w_ref, b_ref, o_ref):
    c = c_ref[...]
    s = (c * _sigmoid(c)).astype(BF16)
    o_ref[0] = _dot(s, w_ref[0].astype(BF16)) + b_ref[0]


def _modulation(cpad, mod_w, mod_b):
    depth, d, n = mod_w.shape
    rows = cpad.shape[0]
    bn = 1024
    return pl.pallas_call(
        _mod_kernel,
        grid=(depth, n // bn),
        in_specs=[pl.BlockSpec((rows, d), lambda l, j: (0, 0)),
                  pl.BlockSpec((1, d, bn), lambda l, j: (l, 0, j)),
                  pl.BlockSpec((1, 1, bn), lambda l, j: (l, 0, j))],
        out_specs=pl.BlockSpec((1, rows, bn), lambda l, j: (l, 0, j)),
        out_shape=jax.ShapeDtypeStruct((depth, rows, n), F32),
        compiler_params=_params(("parallel", "parallel")),
        name="modulation",
    )(cpad, mod_w, mod_b.reshape(depth, 1, n))


def _norm_mod(xv, nw, sc, sh):
    ms = jnp.mean(xv * xv, axis=-1, keepdims=True)
    return (xv * lax.rsqrt(ms + NORM_EPS) * nw) * (1.0 + sc) + sh


def _seg_cumsum(x, axis, reverse):
    n = x.shape[axis]
    idx = lax.broadcasted_iota(jnp.int32, x.shape, axis)
    s = 1
    while s < n:
        if reverse:
            shifted = pltpu.roll(x, n - s, axis)
            x = x + jnp.where(idx < n - s, shifted, 0.0)
        else:
            shifted = pltpu.roll(x, s, axis)
            x = x + jnp.where(idx >= s, shifted, 0.0)
        s *= 2
    return x


def _gate_transform(raw, gate_axis):
    tok_axis = 1 - gate_axis
    k = lax.broadcasted_iota(jnp.int32, raw.shape, gate_axis) % 4
    lf = _log_sigmoid(raw)
    cum_f = _seg_cumsum(jnp.where(k == 1, lf, 0.0), tok_axis, False)
    cum_b = _seg_cumsum(jnp.where(k == 3, lf, 0.0), tok_axis, True)
    bsel = jnp.where(k == 1, cum_f, jnp.where(k == 3, cum_b, 0.0))
    n = raw.shape[gate_axis]
    bnext = pltpu.roll(bsel, n - 1, gate_axis)
    nt = raw.shape[tok_axis]
    tot_f = lax.slice_in_dim(cum_f, nt - 1, nt, axis=tok_axis)
    tot_b = lax.slice_in_dim(cum_b, 0, 1, axis=tok_axis)
    total = jnp.broadcast_to(tot_f + tot_b, raw.shape)
    return jnp.where(k % 2 == 1, bsel, raw - bnext), total


def _pre_kernel(x_ref, xp_ref, xn_ref, mod_ref, nw_ref, wqk_ref, cw_ref, wr_ref, wgt_ref,
                gbt_ref, ca_ref, sa_ref, cr_ref, sr_ref, *out_and_scratch):
    for i in range(x_ref.shape[0]):
        one = pl.ds(i, 1)
        _pre_tile(x_ref.at[one], xp_ref.at[one], xn_ref.at[one], mod_ref.at[one], nw_ref, wqk_ref, cw_ref,
                  wr_ref, wgt_ref, gbt_ref, ca_ref, sa_ref, cr_ref, sr_ref,
                  *[r.at[one] for r in out_and_scratch[:-1]], out_and_scratch[-1].at[i])


def _pre_tile(x_ref, xp_ref, xn_ref, mod_ref, nw_ref, wqk_ref, cw_ref, wr_ref, wgt_ref,
              gbt_ref, ca_ref, sa_ref, cr_ref, sr_ref,
              hb_ref, mq_ref, mk_ref, mv_ref, gr_ref, aq_ref, ak_ref, av_ref,
              rq_ref, rk_ref, rv_ref, hext_ref):
    t = pl.program_id(0)
    nt = pl.num_programs(0)
    sh = mod_ref[0, 0, 0:1, :]
    sc = mod_ref[0, 0, 1:2, :]
    nw = nw_ref[...]
    hb = _norm_mod(x_ref[0], nw, sc, sh).astype(BF16)
    hb_ref[0] = hb

    grow = _dot_nt(wgt_ref[...], hb) + gbt_ref[:, 0:1]
    for c in range(TM // CHUNK):
        gr, tot = _gate_transform(grow[:, c * CHUNK:(c + 1) * CHUNK], 0)
        aend = gr + pltpu.roll(tot, 4 * M_HEADS - 1, 0)
        amax = jnp.broadcast_to(jnp.max(aend, axis=1, keepdims=True), aend.shape)
        for h in range(M_HEADS):
            hp_, o0 = h // 2, 8 * (h % 2)
            gr_ref[0, hp_, c, o0:o0 + 4, :] = gr[4 * h:4 * h + 4]
            for d in range(2):
                gr_ref[0, hp_, c, o0 + 4 + d:o0 + 5 + d, :] = tot[4 * h + 2 * d + 1:4 * h + 2 * d + 2]
                gr_ref[0, hp_, c, o0 + 6 + d:o0 + 7 + d, :] = amax[4 * h + 2 * d:4 * h + 2 * d + 1]

    hp = _norm_mod(xp_ref[0], nw, sc, sh)
    hn = _norm_mod(xn_ref[0], nw, sc, sh)
    hp = jnp.where(t <= 1, 0.0, hp)
    hn = jnp.where((t == 0) | (t == nt - 1), 0.0, hn)
    hext_ref[0:TM, :] = hb
    hext_ref[TM:, :] = jnp.concatenate([hn, hp], axis=0).astype(BF16)
    hext = hext_ref[...]
    ext = TM + 2 * HALO

    for j, (dst, scale) in enumerate(((mq_ref, 1.0), (mk_ref, M_HEAD_DIM ** -0.5))):
        p = _dot(hext, wqk_ref[:, j * M_WIDTH:(j + 1) * M_WIDTH])
        cw = cw_ref[:, j * M_WIDTH:(j + 1) * M_WIDTH]
        prev = pltpu.roll(p, 1, 0)[0:TM]
        nxt = pltpu.roll(p, ext - 1, 0)[0:TM]
        conv = prev * cw[0:1] + p[0:TM] * cw[1:2] + nxt * cw[2:3]
        act = conv * _sigmoid(conv) * scale
        for h in range(M_HEADS):
            dst[0, h] = act[:, h * M_HEAD_DIM:(h + 1) * M_HEAD_DIM].astype(BF16)

    off = 0
    p = _dot(hb, wr_ref[:, off:off + M_WIDTH])
    off += M_WIDTH
    for h in range(M_HEADS):
        for c in range(TM // CHUNK):
            blk = p[c * CHUNK:(c + 1) * CHUNK, h * M_HEAD_DIM:(h + 1) * M_HEAD_DIM]
            mv_ref[0, h, c] = blk.T.astype(BF16)

    lane = lax.broadcasted_iota(jnp.int32, (TM, LANES), 1)
    first16 = (lane % 32) < 16
    ca = ca_ref[...]
    sa = sa_ref[...]

    def rope_a(xs):
        partner = jnp.where(first16, pltpu.roll(xs, LANES - 16, 1), pltpu.roll(xs, 16, 1))
        return xs * ca + partner * sa

    p = _dot(hb, wr_ref[:, off:off + A_WIDTH])
    off += A_WIDTH
    for s in range(A_WIDTH // LANES):
        r = (rope_a(p[:, s * LANES:(s + 1) * LANES]) * (A_HEAD_DIM ** -0.5 * LOG2E)).astype(BF16)
        aq_ref[0, 2 * s] = r[:, 0:A_HEAD_DIM]
        aq_ref[0, 2 * s + 1] = r[:, A_HEAD_DIM:]
    p = _dot(hb, wr_ref[:, off:off + 2 * A_KV_WIDTH])
    off += 2 * A_KV_WIDTH
    r = rope_a(p[:, 0:LANES]).astype(BF16)
    for h in range(A_KV_HEADS):
        ak_ref[0, h] = r[:, h * A_HEAD_DIM:(h + 1) * A_HEAD_DIM]
    pv = p[:, LANES:]
    for h in range(A_KV_HEADS):
        vh = pv if h == 0 else pltpu.roll(pv, A_HEAD_DIM, 1)
        ve = jnp.where(lane < A_HEAD_DIM, vh, 1.0)
        for c in range(TM // A_BLOCK):
            av_ref[0, h, c] = ve[c * A_BLOCK:(c + 1) * A_BLOCK].T.astype(BF16)

    cr = cr_ref[...]
    sr = sr_ref[...]
    for is_k in (False, True):
        p = _dot(hb, wr_ref[:, off:off + R_QK_WIDTH])
        off += R_QK_WIDTH
        for h in range(R_HEADS):
            xs = p[:, h * R_QK_DIM:(h + 1) * R_QK_DIM]
            rot = xs * cr + pltpu.roll(xs, R_QK_DIM // 2, 1) * sr
            if is_k:
                rot = rot * R_QK_DIM ** -0.5
                for c in range(TM // CHUNK):
                    rk_ref[0, h, c] = rot[c * CHUNK:(c + 1) * CHUNK].T.astype(BF16)
            else:
                rq_ref[0, h] = rot.astype(BF16)
    p = _dot(hb, wr_ref[:, off:off + R_V_WIDTH])
    for h in range(R_HEADS):
        rv_ref[0, h] = p[:, h * R_V_DIM:(h + 1) * R_V_DIM].astype(BF16)


def _pre_call(xc, modsel, nw, wqk, cw, wr, wgt, gbt, tabs):
    B, T, D = xc.shape
    nt = T // TM
    hb16 = TM // HALO
    nch = T // CHUNK

    def tile(t, b):
        return (b, t, 0)

    def head_tile(t, b):
        return (b, 0, t, 0)

    in_specs = [
        pl.BlockSpec((BB, TM, D), tile),
        pl.BlockSpec((BB, HALO, D), lambda t, b: (b, jnp.maximum(t * hb16 - 1, 0), 0)),
        pl.BlockSpec((BB, HALO, D), lambda t, b: (b, jnp.minimum((t + 1) * hb16, T // HALO - 1), 0)),
        pl.BlockSpec((BB, 1, 6, D), lambda t, b: (b, jnp.minimum(t, 1), 0, 0)),
        _const_spec(nw.shape), _const_spec(wqk.shape), _const_spec(cw.shape), _const_spec(wr.shape),
        _const_spec(wgt.shape), _const_spec(gbt.shape),
    ] + [pl.BlockSpec((TM, LANES), lambda t, b: (t, 0)) for _ in range(4)]
    out_shape = [
        jax.ShapeDtypeStruct((B, T, D), BF16),
        jax.ShapeDtypeStruct((B, M_HEADS, T, M_HEAD_DIM), BF16),
        jax.ShapeDtypeStruct((B, M_HEADS, T, M_HEAD_DIM), BF16),
        jax.ShapeDtypeStruct((B, M_HEADS, nch, M_HEAD_DIM, CHUNK), BF16),
        jax.ShapeDtypeStruct((B, 2, nch, 16, CHUNK), F32),
        jax.ShapeDtypeStruct((B, A_HEADS, T, A_HEAD_DIM), BF16),
        jax.ShapeDtypeStruct((B, A_KV_HEADS, T, A_HEAD_DIM), BF16),
        jax.ShapeDtypeStruct((B, A_KV_HEADS, T // A_BLOCK, LANES, A_BLOCK), BF16),
        jax.ShapeDtypeStruct((B, R_HEADS, T, R_QK_DIM), BF16),
        jax.ShapeDtypeStruct((B, R_HEADS, nch, R_QK_DIM, CHUNK), BF16),
        jax.ShapeDtypeStruct((B, R_HEADS, T, R_V_DIM), BF16),
    ]
    out_specs = [
        pl.BlockSpec((BB, TM, D), tile),
        pl.BlockSpec((BB, M_HEADS, TM, M_HEAD_DIM), head_tile),
        pl.BlockSpec((BB, M_HEADS, TM, M_HEAD_DIM), head_tile),
        pl.BlockSpec((BB, M_HEADS, TM // CHUNK, M_HEAD_DIM, CHUNK), lambda t, b: (b, 0, t, 0, 0)),
        pl.BlockSpec((BB, 2, TM // CHUNK, 16, CHUNK), lambda t, b: (b, 0, t, 0, 0)),
        pl.BlockSpec((BB, A_HEADS, TM, A_HEAD_DIM), head_tile),
        pl.BlockSpec((BB, A_KV_HEADS, TM, A_HEAD_DIM), head_tile),
        pl.BlockSpec((BB, A_KV_HEADS, TM // A_BLOCK, LANES, A_BLOCK), lambda t, b: (b, 0, t, 0, 0)),
        pl.BlockSpec((BB, R_HEADS, TM, R_QK_DIM), head_tile),
        pl.BlockSpec((BB, R_HEADS, TM // CHUNK, R_QK_DIM, CHUNK), lambda t, b: (b, 0, t, 0, 0)),
        pl.BlockSpec((BB, R_HEADS, TM, R_V_DIM), head_tile),
    ]
    return pl.pallas_call(
        _pre_kernel,
        grid=(nt, B // BB),
        in_specs=in_specs,
        out_specs=out_specs,
        out_shape=out_shape,
        scratch_shapes=[pltpu.VMEM((BB, TM + 2 * HALO, D), BF16)],
        compiler_params=_params(("parallel", "parallel")),
        name="pre_proj",
    )(xc, xc, xc, modsel, nw, wqk, cw, wr, wgt, gbt, *tabs)


def _bwd_chunk(i, n_ctx, n_all):
    if isinstance(i, int):
        return n_ctx - 1 - i if i < n_ctx else n_all - 1 + n_ctx - i
    return n_all - 1 + n_ctx - i


def _chunk_start(c):
    return c * CHUNK if isinstance(c, int) else pl.multiple_of(c * CHUNK, CHUNK)


def _chunk_loop(body, n_ctx, n_all, unroll):
    for i in range(n_ctx):
        body(i, 0)
    n_lat = n_all - n_ctx
    while n_lat % unroll:
        unroll //= 2
    lax.fori_loop(n_ctx, n_all, body, 0, unroll=unroll)


def _mlstm_kernel(q_ref, k_ref, vt_ref, gr_ref, o_ref, c_ref, n_ref, m_ref,
                  cs_ref, ns_ref, ms_ref, qp_ref, nq_ref, *, n_ctx):
    n_all = gr_ref.shape[2]
    dh = q_ref.shape[3]
    c_ref[...] = jnp.zeros(c_ref.shape, F32)
    n_ref[...] = jnp.zeros(n_ref.shape, F32)
    m_ref[...] = jnp.zeros(m_ref.shape, F32)

    def scan_unit(c, hh, d):
        u = 2 * hh + d
        c0 = _chunk_start(c)
        k = k_ref[0, hh, pl.ds(c0, CHUNK), :]
        vt = vt_ref[0, hh, c]
        rows = gr_ref[0, 0, c, 8 * hh:8 * hh + 8, :]
        tot = rows[4 + d:5 + d]
        aend = rows[2 * d:2 * d + 1] + tot
        amax = rows[6 + d:7 + d]
        cst = c_ref[u]
        nv = n_ref[u]
        m = m_ref[u]
        cs_ref[u, c] = cst.astype(BF16)
        ns_ref[u, c] = jnp.broadcast_to(nv, (16, dh)).astype(BF16)
        ms_ref[u, c] = jnp.broadcast_to(m, (8, CHUNK))
        m_new = jnp.maximum(tot + m, amax)
        w = jnp.exp(aend - m_new)
        dec = jnp.exp(tot + m - m_new)
        vw = (vt.astype(F32) * w).astype(BF16)
        c_ref[u] = dec * cst + _dot(vw, k)
        wk = _dot(jnp.broadcast_to(w, (16, CHUNK)).astype(BF16), k)
        n_ref[u] = dec * nv + wk[0:1]
        m_ref[u] = m_new

    def scan_step(i, carry):
        cb = _bwd_chunk(i, n_ctx, n_all)
        for hh in range(2):
            scan_unit(i, hh, 0)
            scan_unit(cb, hh, 1)
        return carry

    _chunk_loop(scan_step, n_ctx, n_all, 4)

    row = lax.broadcasted_iota(jnp.int32, (CHUNK, CHUNK), 0)
    col = lax.broadcasted_iota(jnp.int32, (CHUNK, CHUNK), 1)
    tri = (row <= col, row >= col)

    def query_products(c, slot):
        c0 = _chunk_start(c)
        for hh in range(2):
            q = q_ref[0, hh, pl.ds(c0, CHUNK), :]
            k = k_ref[0, hh, pl.ds(c0, CHUNK), :]
            qp_ref[slot, hh, 0] = _dot_nt(k, q)
            for d in range(2):
                u = 2 * hh + d
                qp_ref[slot, hh, 1 + d] = _dot_nt(cs_ref[u, c], q)
                nq_ref[slot, u] = _dot_nt(ns_ref[u, c], q)[0:8]

    def outputs(c, slot):
        c0 = _chunk_start(c)
        gcv = jnp.concatenate([gr_ref[0, 0, c], jnp.zeros((CHUNK - 16, CHUNK), F32)], axis=0).T
        for hh in range(2):
            vt = vt_ref[0, hh, c]
            rows = gr_ref[0, 0, c, 8 * hh:8 * hh + 8, :]
            st = qp_ref[slot, hh, 0]
            acc = None
            for d in range(2):
                u = 2 * hh + d
                a0c = gcv[:, 8 * hh + 2 * d:8 * hh + 2 * d + 1]
                b = rows[2 * d + 1:2 * d + 2]
                m = ms_ref[u, c, 0:1, :]
                dlog = jnp.where(tri[d], a0c + b, NEG_INF)
                mj = jnp.maximum(b + m, jnp.max(dlog, axis=0, keepdims=True))
                sd = st * jnp.exp(dlog - mj)
                iw = jnp.exp(b + m - mj)
                den = iw * nq_ref[slot, u, 0:1, :] + jnp.sum(sd, axis=0, keepdims=True)
                inv = 1.0 / jnp.maximum(jnp.abs(den), jnp.exp(-mj))
                ht = (_dot(vt, sd.astype(BF16)) + iw * qp_ref[slot, hh, 1 + d]) * inv
                acc = ht if acc is None else acc + ht
            o_ref[0, hh, pl.ds(c0, CHUNK), :] = acc.T

    query_products(0, 0)

    def out_pair(i, carry):
        c = 2 * i
        query_products(c + 1, 1)
        outputs(c, 0)
        query_products(jnp.minimum(c + 2, n_all - 1), 0)
        outputs(c + 1, 1)
        return carry

    n_pairs = n_all // 2
    unroll = 4
    first = n_pairs % unroll
    for i in range(first):
        out_pair(i, 0)
    lax.fori_loop(first, n_pairs, out_pair, 0, unroll=unroll)


def _mlstm_call(mq, mk, mvt, gr, n_ctx):
    B, H, T, dh = mq.shape
    nch = T // CHUNK
    qspec = pl.BlockSpec((1, 2, T, dh), lambda b, p: (b, p, 0, 0))
    return pl.pallas_call(
        functools.partial(_mlstm_kernel, n_ctx=n_ctx),
        grid=(B, H // 2),
        in_specs=[qspec, qspec,
                  pl.BlockSpec((1, 2, nch, dh, CHUNK), lambda b, p: (b, p, 0, 0, 0)),
                  pl.BlockSpec((1, 1, nch, 16, CHUNK), lambda b, p: (b, p, 0, 0, 0))],
        out_specs=pl.BlockSpec((1, 2, T, dh), lambda b, p: (b, p, 0, 0)),
        out_shape=jax.ShapeDtypeStruct((B, H, T, dh), F32),
        scratch_shapes=[pltpu.VMEM((4, dh, dh), F32), pltpu.VMEM((4, 1, dh), F32),
                        pltpu.VMEM((4, 1, CHUNK), F32),
                        pltpu.VMEM((4, nch, dh, dh), BF16), pltpu.VMEM((4, nch, 16, dh), BF16),
                        pltpu.VMEM((4, nch, 8, CHUNK), F32),
                        pltpu.VMEM((2, 2, 3, dh, CHUNK), F32), pltpu.VMEM((2, 4, 8, CHUNK), F32)],
        compiler_params=_params(("parallel", "parallel")),
        name="mlstm",
    )(mq, mk, mvt, gr)


def _ret_kernel(q_ref, kt_ref, v_ref, lg_ref, o_ref, st_ref, sts_ref, dec_ref, ib_ref, wb_ref, *, n_ctx):
    n_all = kt_ref.shape[2]
    dk = q_ref.shape[3]
    dv = v_ref.shape[3]
    st_ref[...] = jnp.zeros(st_ref.shape, F32)
    row = lax.broadcasted_iota(jnp.int32, (CHUNK, CHUNK), 0).astype(F32)
    col = lax.broadcasted_iota(jnp.int32, (CHUNK, CHUNK), 1).astype(F32)
    cds = []
    dec = None
    for d in range(2):
        lg = _log_sigmoid(lg_ref[0, d])[0:1]
        lgk = lg[:, 0:CHUNK]
        diff = row - col if d == 0 else col - row
        dd = jnp.where(diff >= 0, jnp.exp(lgk * jnp.maximum(diff, 0.0)), 0.0)
        dec = dd if dec is None else dec + dd
        pin = row if d == 0 else CHUNK - 1.0 - row
        ib_ref[d] = jnp.exp(lg * (jnp.concatenate([pin, pin], axis=1) + 1.0))
        pkey = col if d == 0 else CHUNK - 1.0 - col
        wb_ref[d] = jnp.exp(lgk * (CHUNK - 1.0 - pkey))
        cds.append(jnp.exp(lg * CHUNK))
    dec_ref[...] = dec

    def scan_step(i, carry):
        cb = _bwd_chunk(i, n_ctx, n_all)
        for d, c in ((0, i), (1, cb)):
            c0 = _chunk_start(c)
            st = st_ref[d]
            sts_ref[c, :, d * dv:(d + 1) * dv] = st.astype(BF16)
            kw = (kt_ref[0, 0, c].astype(F32) * wb_ref[d]).astype(BF16)
            st_ref[d] = cds[d] * st + _dot(kw, v_ref[0, 0, pl.ds(c0, CHUNK), :])
        return carry

    _chunk_loop(scan_step, n_ctx, n_all, 8)

    def out_step(c, carry):
        c0 = _chunk_start(c)
        q = q_ref[0, 0, pl.ds(c0, CHUNK), :]
        s = (_dot(q, kt_ref[0, 0, c]) * dec_ref[...]).astype(BF16)
        qst = _dot(q, sts_ref[c])
        o_ref[0, 0, pl.ds(c0, CHUNK), :] = (_dot(s, v_ref[0, 0, pl.ds(c0, CHUNK), :])
                                            + ib_ref[0] * qst[:, 0:dv] + ib_ref[1] * qst[:, dv:])
        return carry

    _chunk_loop(out_step, n_ctx, n_all, 8)


def _ret_call(rq, rkt, rv, lgb, n_ctx):
    B, H, T, dk = rq.shape
    dv = rv.shape[-1]
    nch = T // CHUNK
    return pl.pallas_call(
        functools.partial(_ret_kernel, n_ctx=n_ctx),
        grid=(B, H),
        in_specs=[pl.BlockSpec((1, 1, T, dk), lambda b, h: (b, h, 0, 0)),
                  pl.BlockSpec((1, 1, nch, dk, CHUNK), lambda b, h: (b, h, 0, 0, 0)),
                  pl.BlockSpec((1, 1, T, dv), lambda b, h: (b, h, 0, 0)),
                  pl.BlockSpec((1, 2, 8, dv), lambda b, h: (h, 0, 0, 0))],
        out_specs=pl.BlockSpec((1, 1, T, dv), lambda b, h: (b, h, 0, 0)),
        out_shape=jax.ShapeDtypeStruct((B, H, T, dv), F32),
        scratch_shapes=[pltpu.VMEM((2, dk, dv), F32), pltpu.VMEM((nch, dk, 2 * dv), BF16),
                        pltpu.VMEM((CHUNK, CHUNK), F32), pltpu.VMEM((2, CHUNK, dv), F32),
                        pltpu.VMEM((2, dk, CHUNK), F32)],
        compiler_params=_params(("parallel", "parallel")),
        name="retention",
    )(rq, rkt, rv, lgb)


def _attn_kernel(q_ref, k_ref, vt_ref, sink_ref, o_ref, bias_ref, s_ref, *, n_ctx_tok, need_ctx):
    T = q_ref.shape[2]
    L = n_ctx_tok
    S = T - L
    nb = S // A_BLOCK
    nspan = 3
    span = nspan * A_BLOCK
    ncb = L // A_BLOCK
    cols = A_GROUP * A_BLOCK
    kc = k_ref[0, 0, 0:L, :]
    vtc = jnp.concatenate([vt_ref[0, 0, j] for j in range(ncb)], axis=1)
    sink = sink_ref[0]

    kpos = lax.broadcasted_iota(jnp.int32, (span, A_BLOCK), 0)
    qpos = lax.broadcasted_iota(jnp.int32, (span, A_BLOCK), 1)
    for delta in range(nspan):
        bias_ref[delta] = jnp.where(jnp.abs(qpos + delta * A_BLOCK - kpos) <= A_WINDOW, 0.0, NEG_INF)

    def finish(blk, acc, m):
        den = acc[A_HEAD_DIM:A_HEAD_DIM + 1] + jnp.exp2(sink - m)
        ot = (acc[0:A_HEAD_DIM] * (1.0 / den)).astype(BF16)
        for g in range(A_GROUP):
            o_ref[0, blk, g * A_HEAD_DIM:(g + 1) * A_HEAD_DIM, :] = ot[:, g * A_BLOCK:(g + 1) * A_BLOCK]

    for cbk in range(ncb):
        if need_ctx:
            qs = q_ref[0, :, cbk * A_BLOCK:(cbk + 1) * A_BLOCK, :].reshape(cols, A_HEAD_DIM)
            s = _dot_nt(kc, qs)
            m = jnp.maximum(jnp.max(s, axis=0, keepdims=True), sink)
            finish(cbk, _dot(vtc, jnp.exp2(s - m).astype(BF16)), m)
        else:
            o_ref[0, cbk] = jnp.zeros(o_ref.shape[2:], BF16)

    def first_key_block(n):
        return jnp.clip(n - 1, 0, nb - nspan)

    def scores(n, slot):
        q0 = pl.multiple_of(L + n * A_BLOCK, A_BLOCK)
        kb = first_key_block(n)
        k0 = pl.multiple_of(L + kb * A_BLOCK, A_BLOCK)
        qs = q_ref[0, :, pl.ds(q0, A_BLOCK), :].reshape(cols, A_HEAD_DIM)
        bias = bias_ref[n - kb]
        s_ref[slot, 0:span, :] = (_dot_nt(k_ref[0, 0, pl.ds(k0, span), :], qs)
                                  + jnp.concatenate([bias] * A_GROUP, axis=1))
        s_ref[slot, span:, :] = _dot_nt(kc, qs)

    def softmax_pv(n, slot):
        kb = first_key_block(n)
        s = s_ref[slot]
        m = jnp.maximum(jnp.max(s, axis=0, keepdims=True), sink)
        p = jnp.exp2(s - m).astype(BF16)
        vt = jnp.concatenate([vt_ref[0, 0, ncb + kb + j] for j in range(nspan)] + [vtc], axis=1)
        finish(ncb + n, _dot(vt, p), m)

    scores(0, 0)

    def block_pair(i, carry):
        n = 2 * i
        scores(n + 1, 1)
        softmax_pv(n, 0)
        scores(jnp.minimum(n + 2, nb - 1), 0)
        softmax_pv(n + 1, 1)
        return carry

    n_pairs = nb // 2
    first = n_pairs % 2
    if first:
        block_pair(0, 0)
    lax.fori_loop(first, n_pairs, block_pair, 0, unroll=2)


def _attn_call(aq, ak, avt, sink_rows, n_ctx_tok, need_ctx):
    B, _, T, hd = aq.shape
    nblk = T // A_BLOCK
    cols = A_GROUP * A_BLOCK
    return pl.pallas_call(
        functools.partial(_attn_kernel, n_ctx_tok=n_ctx_tok, need_ctx=need_ctx),
        grid=(B, A_KV_HEADS),
        in_specs=[pl.BlockSpec((1, A_GROUP, T, hd), lambda b, h: (b, h, 0, 0)),
                  pl.BlockSpec((1, 1, T, hd), lambda b, h: (b, h, 0, 0)),
                  pl.BlockSpec((1, 1, nblk, LANES, A_BLOCK), lambda b, h: (b, h, 0, 0, 0)),
                  pl.BlockSpec((1, 1, cols), lambda b, h: (h, 0, 0))],
        out_specs=pl.BlockSpec((1, nblk, A_GROUP * hd, A_BLOCK), lambda b, h: (b, 0, h, 0)),
        out_shape=jax.ShapeDtypeStruct((B, nblk, A_WIDTH, A_BLOCK), BF16),
        scratch_shapes=[pltpu.VMEM((3, 3 * A_BLOCK, A_BLOCK), F32),
                        pltpu.VMEM((2, 3 * A_BLOCK + n_ctx_tok, cols), F32)],
        compiler_params=_params(("parallel", "parallel")),
        name="window_attn",
    )(aq, ak, avt, sink_rows)


def _head_norm(y):
    mu = jnp.mean(y, axis=-1, keepdims=True)
    yc = y - mu
    var = jnp.mean(yc * yc, axis=-1, keepdims=True)
    return yc * lax.rsqrt(var + NORM_EPS)


def _post_kernel(x_ref, hb_ref, hm_ref, ya_ref, hr_ref, mod_ref, wg_ref, mnw_ref, rnw_ref,
                 wbm_ref, wba_ref, wbr_ref, wo_ref, o_ref):
    for i in range(x_ref.shape[0]):
        hb = hb_ref[i]
        off = 0
        hm = jnp.concatenate([_head_norm(hm_ref[i, h]) for h in range(M_HEADS)], axis=1) * mnw_ref[...]
        ym = (_sigmoid(_dot(hb, wg_ref[:, off:off + M_WIDTH])) * hm).astype(BF16)
        off += M_WIDTH
        hr = jnp.concatenate([_head_norm(hr_ref[i, h]) for h in range(R_HEADS)], axis=1) * rnw_ref[...]
        rg = _dot(hb, wg_ref[:, off:off + R_V_WIDTH])
        yr = (rg * _sigmoid(rg) * hr).astype(BF16)
        off += R_V_WIDTH
        z = _sigmoid(_dot(hb, wg_ref[:, off:off + D_MODEL])) * _dot(ym, wbm_ref[...])
        off += D_MODEL
        pa = jnp.concatenate([_dot_tn(ya_ref[i, c], wba_ref[...]) for c in range(TM // A_BLOCK)], axis=0)
        z = z + _sigmoid(_dot(hb, wg_ref[:, off:off + D_MODEL])) * pa
        off += D_MODEL
        z = z + _sigmoid(_dot(hb, wg_ref[:, off:off + D_MODEL])) * _dot(yr, wbr_ref[...])
        y = _dot(z.astype(BF16), wo_ref[...])
        o_ref[i] = x_ref[i] + mod_ref[i, 0, 2:3, :] * y


def _post_call(xc, hb, hm, ya, hr, modsel, wg, mnw, rnw, wbm, wba, wbr, wo, t_off):
    B, T, D = xc.shape
    nt = T // TM - t_off

    def tile(t, b):
        return (b, t + t_off, 0)

    def head_tile(t, b):
        return (b, 0, t + t_off, 0)

    return pl.pallas_call(
        _post_kernel,
        grid=(nt, B // BB),
        in_specs=[pl.BlockSpec((BB, TM, D), tile),
                  pl.BlockSpec((BB, TM, D), tile),
                  pl.BlockSpec((BB, M_HEADS, TM, M_HEAD_DIM), head_tile),
                  pl.BlockSpec((BB, TM // A_BLOCK, A_WIDTH, A_BLOCK), lambda t, b: (b, t + t_off, 0, 0)),
                  pl.BlockSpec((BB, R_HEADS, TM, R_V_DIM), head_tile),
                  pl.BlockSpec((BB, 1, 6, D), lambda t, b: (b, jnp.minimum(t + t_off, 1), 0, 0)),
                  _const_spec(wg.shape), _const_spec(mnw.shape), _const_spec(rnw.shape),
                  _const_spec(wbm.shape), _const_spec(wba.shape), _const_spec(wbr.shape),
                  _const_spec(wo.shape)],
        out_specs=pl.BlockSpec((BB, TM, D), lambda t, b: (b, t, 0)),
        out_shape=jax.ShapeDtypeStruct((B, nt * TM, D), F32),
        compiler_params=_params(("parallel", "parallel")),
        name="post_merge",
    )(xc, hb, hm, ya, hr, modsel, wg, mnw, rnw, wbm, wba, wbr, wo)


def _ffn_kernel(x_ref, xp_ref, xn_ref, mod_ref, nw_ref, wa_ref, wb_ref, cw_ref, cb_ref, wd_ref, fw_ref,
                o_ref, hext_ref, u_ref, *, has_ctx, final):
    t = pl.program_id(0)
    nt = pl.num_programs(0)
    first_lat = 1 if has_ctx else 0
    nw = nw_ref[...]
    ext = TM + 2 * HALO
    for i in range(x_ref.shape[0]):
        sh = mod_ref[i, 0, 3:4, :]
        sc = mod_ref[i, 0, 4:5, :]
        g2 = mod_ref[i, 0, 5:6, :]
        x = x_ref[i]
        hp = _norm_mod(xp_ref[i], nw, sc, sh)
        hn = _norm_mod(xn_ref[i], nw, sc, sh)
        hp = jnp.where(t <= first_lat, 0.0, hp)
        hn = jnp.where((t == first_lat - 1) | (t == nt - 1), 0.0, hn)
        hext_ref[i, 0:TM, :] = _norm_mod(x, nw, sc, sh).astype(BF16)
        hext_ref[i, TM:, :] = jnp.concatenate([hn, hp], axis=0).astype(BF16)
        for j in range(FFN_DIM // FFN_NC):
            cs = slice(j * FFN_NC, (j + 1) * FFN_NC)
            a = _dot(hext_ref[i], wa_ref[:, cs])
            b = _dot(hext_ref[i, 0:TM, :], wb_ref[:, cs])
            cw = cw_ref[:, cs]
            prev = pltpu.roll(a, 1, 0)[0:TM]
            nxt = pltpu.roll(a, ext - 1, 0)[0:TM]
            conv = prev * cw[0:1] + a[0:TM] * cw[1:2] + nxt * cw[2:3] + cb_ref[:, cs]
            u_ref[i, :, cs] = (conv * _sigmoid(conv) * b).astype(BF16)
        y = x + g2 * _dot(u_ref[i], wd_ref[...])
        if final:
            ms = jnp.mean(y * y, axis=-1, keepdims=True)
            y = y * lax.rsqrt(ms + NORM_EPS) * fw_ref[...]
        o_ref[i] = y


def _ffn_call(xs, modsel, nw, wa, wb, cw, cb, wd, fw, has_ctx, final):
    B, T, D = xs.shape
    nt = T // TM
    hb16 = TM // HALO
    first_lat = 1 if has_ctx else 0
    tile = pl.BlockSpec((BB, TM, D), lambda t, b: (b, t, 0))
    return pl.pallas_call(
        functools.partial(_ffn_kernel, has_ctx=has_ctx, final=final),
        grid=(nt, B // BB),
        in_specs=[tile,
                  pl.BlockSpec((BB, HALO, D), lambda t, b: (b, jnp.maximum(t * hb16 - 1, 0), 0)),
                  pl.BlockSpec((BB, HALO, D), lambda t, b: (b, jnp.minimum((t + 1) * hb16, T // HALO - 1), 0)),
                  pl.BlockSpec((BB, 1, 6, D), lambda t, b: (b, jnp.minimum(t + 1 - first_lat, 1), 0, 0)),
                  _const_spec(nw.shape), _const_spec(wa.shape), _const_spec(wb.shape),
                  _const_spec(cw.shape), _const_spec(cb.shape), _const_spec(wd.shape),
                  _const_spec(fw.shape)],
        out_specs=tile,
        out_shape=jax.ShapeDtypeStruct((B, T, D), F32),
        scratch_shapes=[pltpu.VMEM((BB, TM + 2 * HALO, D), BF16), pltpu.VMEM((BB, TM, FFN_DIM), BF16)],
        compiler_params=_params(("parallel", "parallel")),
        name="conv_ffn",
    )(xs, xs, xs, modsel, nw, wa, wb, cw, cb, wd, fw)


def _split_cols(w):
    out = []
    acc = 0
    for s in IN_SPLITS:
        out.append(w[:, acc:acc + s])
        acc += s
    return out


def _rope_tables(L, S):
    T = L + S
    j = jnp.arange(LANES)
    jj = j % 32
    inv_a = ROPE_BASE ** (-(jj % 16).astype(F32) / 16.0)
    t = jnp.arange(S)
    pos = jnp.where(((j % A_HEAD_DIM) // 32 == 0)[None, :], (t // GRID_W)[:, None], (t % GRID_W)[:, None])
    ang = pos.astype(F32) * inv_a[None, :]
    sign = jnp.where(jj < 16, -1.0, 1.0)[None, :]
    ca = jnp.concatenate([jnp.ones((L, LANES), F32), jnp.cos(ang)], axis=0)
    sa = jnp.concatenate([jnp.zeros((L, LANES), F32), jnp.sin(ang) * sign], axis=0)
    half = R_QK_DIM // 2
    inv_r = ROPE_BASE ** (-(j % half).astype(F32) / half)
    angr = jnp.arange(T).astype(F32)[:, None] * inv_r[None, :]
    signr = jnp.where(j < half, -1.0, 1.0)[None, :]
    return ca, sa, jnp.cos(angr), jnp.sin(angr) * signr


def kernel(x, c, ctx, c_ctx, mod_w, mod_b, norm1_w, norm2_w, w_in, m_gate_bias, m_conv_w, m_norm_w,
           a_sink, ret_logit, ret_norm_w, w_br_m, w_br_a, w_br_r, w_out, ffn_up, ffn_conv_w, ffn_conv_b,
           ffn_down, final_norm_w):
    B, S, D = x.shape
    L = ctx.shape[1]
    depth = mod_w.shape[0]
    assert D == D_MODEL and L == TM and S % TM == 0 and S >= 3 * A_BLOCK and B % BB == 0
    T = L + S
    n_ctx = L // CHUNK

    rows = -(-(B + 1) // 8) * 8
    cpad = jnp.zeros((rows, D), F32).at[:B].set(c).at[B].set(c_ctx)
    mods = _modulation(cpad, mod_w, mod_b)
    tabs = _rope_tables(L, S)
    xc = jnp.concatenate([ctx, x], axis=1)

    gperm = jnp.arange(4 * M_HEADS).reshape(4, M_HEADS).T.reshape(-1)

    for l in range(depth):
        last = l == depth - 1
        lat = mods[l, :B].reshape(B, 6, D)
        cm = jnp.broadcast_to(mods[l, B].reshape(1, 6, D), (B, 6, D))
        modsel = jnp.stack([cm, lat], axis=1)

        (w_mq, w_mk, w_mv, w_mo, w_mg, w_aq, w_ak, w_av,
         w_rq, w_rk, w_rv, w_rg, w_gm, w_ga, w_gr) = _split_cols(w_in[l])
        wqk = jnp.concatenate([w_mq, w_mk], axis=1).astype(BF16)
        wr = jnp.concatenate([w_mv, w_aq, w_ak, w_av, w_rq, w_rk, w_rv], axis=1).astype(BF16)
        wgt = w_mg[:, gperm].T.astype(BF16)
        gbt = jnp.broadcast_to(m_gate_bias[l][gperm].reshape(-1, 1), (4 * M_HEADS, LANES))

        (hb, mq, mk, mv, gr, aq, ak, av, rq, rk, rv) = _pre_call(
            xc, modsel, norm1_w[l].reshape(1, D), wqk, m_conv_w[l], wr, wgt, gbt, tabs)

        hm = _mlstm_call(mq, mk, mv, gr, n_ctx)
        sink_rows = jnp.repeat(a_sink[l].reshape(A_KV_HEADS, A_GROUP), A_BLOCK, axis=1)[:, None, :]
        ya = _attn_call(aq, ak, av, sink_rows.astype(F32) * LOG2E, L, not last)
        lgb = jnp.broadcast_to(ret_logit[l].T[:, :, None, None], (R_HEADS, 2, 8, R_V_DIM)).astype(F32)
        hr = _ret_call(rq, rk, rv, lgb, n_ctx)

        t_off = 1 if last else 0
        wgates = jnp.concatenate([w_mo, w_rg, w_gm, w_ga, w_gr], axis=1).astype(BF16)
        x1 = _post_call(xc, hb, hm, ya, hr, modsel, wgates, m_norm_w[l].reshape(1, -1),
                        ret_norm_w[l].reshape(1, -1), w_br_m[l].astype(BF16), w_br_a[l].astype(BF16),
                        w_br_r[l].astype(BF16), w_out[l].astype(BF16), t_off)
        xc = _ffn_call(x1, modsel, norm2_w[l].reshape(1, D), ffn_up[l][:, :FFN_DIM].astype(BF16),
                       ffn_up[l][:, FFN_DIM:].astype(BF16), ffn_conv_w[l], ffn_conv_b[l].reshape(1, -1),
                       ffn_down[l].astype(BF16), final_norm_w.reshape(1, D), not last, last)
    return xc
```

```python
import functools

import jax
import jax.numpy as jnp
from jax import lax
from jax.experimental import pallas as pl
from jax.experimental.pallas import tpu as pltpu

F32 = jnp.float32
BF16 = jnp.bfloat16

D_MODEL = 1024
GRID_W = 64
NORM_EPS = 1e-6
ROPE_BASE = 10000.0
NEG_INF = -1e30
LOG2E = 1.4426950408889634

M_HEADS = 4
M_HEAD_DIM = 128
M_WIDTH = M_HEADS * M_HEAD_DIM
A_HEADS = 8
A_KV_HEADS = 2
A_GROUP = A_HEADS // A_KV_HEADS
A_HEAD_DIM = 64
A_WIDTH = A_HEADS * A_HEAD_DIM
A_KV_WIDTH = A_KV_HEADS * A_HEAD_DIM
A_WINDOW = 128
A_BLOCK = 128
R_HEADS = 4
R_QK_DIM = 128
R_V_DIM = 256
R_QK_WIDTH = R_HEADS * R_QK_DIM
R_V_WIDTH = R_HEADS * R_V_DIM
FFN_DIM = 2816

IN_SPLITS = (M_WIDTH, M_WIDTH, M_WIDTH, M_WIDTH, 4 * M_HEADS,
             A_WIDTH, A_KV_WIDTH, A_KV_WIDTH,
             R_QK_WIDTH, R_QK_WIDTH, R_V_WIDTH, R_V_WIDTH,
             D_MODEL, D_MODEL, D_MODEL)

TM = 256
BB = 2
HALO = 8
CHUNK = 128
FFN_NC = 256
LANES = 128
VMEM_LIMIT = 56 * 1024 * 1024


def _sigmoid(x):
    return 1.0 / (1.0 + jnp.exp(-x))


def _log_sigmoid(x):
    return jnp.minimum(x, 0.0) - jnp.log1p(jnp.exp(-jnp.abs(x)))


def _dot(a, b):
    return jnp.dot(a, b, preferred_element_type=F32)


def _dot_nt(a, b):
    return lax.dot_general(a, b, (((1,), (1,)), ((), ())), preferred_element_type=F32)


def _dot_tn(a, b):
    return lax.dot_general(a, b, (((0,), (0,)), ((), ())), preferred_element_type=F32)


def _const_spec(shape):
    nd = len(shape)
    return pl.BlockSpec(shape, lambda *_: (0,) * nd, pipeline_mode=pl.Buffered(1))


def _params(sem):
    return pltpu.CompilerParams(dimension_semantics=sem, vmem_limit_bytes=VMEM_LIMIT)


def _mod_kernel(c_ref, w_ref, b_ref, o_ref):
    c = c_ref[...]
    s = (c * _sigmoid(c)).astype(BF16)
    o_ref[0] = _dot(s, w_ref[0].astype(BF16)) + b_ref[0]


def _modulation(cpad, mod_w, mod_b):
    depth, d, n = mod_w.shape
    rows = cpad.shape[0]
    bn = 1024
    return pl.pallas_call(
        _mod_kernel,
        grid=(depth, n // bn),
        in_specs=[pl.BlockSpec((rows, d), lambda l, j: (0, 0)),
                  pl.BlockSpec((1, d, bn), lambda l, j: (l, 0, j)),
                  pl.BlockSpec((1, 1, bn), lambda l, j: (l, 0, j))],
        out_specs=pl.BlockSpec((1, rows, bn), lambda l, j: (l, 0, j)),
        out_shape=jax.ShapeDtypeStruct((depth, rows, n), F32),
        compiler_params=_params(("parallel", "parallel")),
        name="modulation",
    )(cpad, mod_w, mod_b.reshape(depth, 1, n))


def _norm_mod(xv, nw, sc, sh):
    ms = jnp.mean(xv * xv, axis=-1, keepdims=True)
    return (xv * lax.rsqrt(ms + NORM_EPS) * nw) * (1.0 + sc) + sh


def _seg_cumsum(x, axis, reverse):
    n = x.shape[axis]
    idx = lax.broadcasted_iota(jnp.int32, x.shape, axis)
    s = 1
    while s < n:
        if reverse:
            shifted = pltpu.roll(x, n - s, axis)
            x = x + jnp.where(idx < n - s, shifted, 0.0)
        else:
            shifted = pltpu.roll(x, s, axis)
            x = x + jnp.where(idx >= s, shifted, 0.0)
        s *= 2
    return x


def _gate_transform(raw, gate_axis):
    tok_axis = 1 - gate_axis
    k = lax.broadcasted_iota(jnp.int32, raw.shape, gate_axis) % 4
    lf = _log_sigmoid(raw)
    cum_f = _seg_cumsum(jnp.where(k == 1, lf, 0.0), tok_axis, False)
    cum_b = _seg_cumsum(jnp.where(k == 3, lf, 0.0), tok_axis, True)
    bsel = jnp.where(k == 1, cum_f, jnp.where(k == 3, cum_b, 0.0))
    n = raw.shape[gate_axis]
    bnext = pltpu.roll(bsel, n - 1, gate_axis)
    nt = raw.shape[tok_axis]
    tot_f = lax.slice_in_dim(cum_f, nt - 1, nt, axis=tok_axis)
    tot_b = lax.slice_in_dim(cum_b, 0, 1, axis=tok_axis)
    total = jnp.broadcast_to(tot_f + tot_b, raw.shape)
    return jnp.where(k % 2 == 1, bsel, raw - bnext), total


def _pre_kernel(c_ref, x_ref, xp_ref, xn_ref, mod_ref, nw_ref, wqk_ref, cw_ref, wr_ref, wgt_ref,
                gbt_ref, ca_ref, sa_ref, cr_ref, sr_ref, *out_and_scratch, split):
    for i in range(x_ref.shape[0]):
        one = pl.ds(i, 1)
        _pre_tile(c_ref.at[one] if split else None,
                  x_ref.at[one], xp_ref.at[one], xn_ref.at[one], mod_ref.at[one], nw_ref, wqk_ref, cw_ref,
                  wr_ref, wgt_ref, gbt_ref, ca_ref, sa_ref, cr_ref, sr_ref,
                  *[r.at[one] for r in out_and_scratch[:-1]], out_and_scratch[-1].at[i])


def _tile_rows(c_ref, x_ref):
    if c_ref is None:
        return x_ref[0]
    return jnp.where(pl.program_id(0) == 0, c_ref[0], x_ref[0])


def _pre_tile(c_ref, x_ref, xp_ref, xn_ref, mod_ref, nw_ref, wqk_ref, cw_ref, wr_ref, wgt_ref,
              gbt_ref, ca_ref, sa_ref, cr_ref, sr_ref,
              hb_ref, mq_ref, mk_ref, mv_ref, gr_ref, aq_ref, ak_ref, av_ref,
              rq_ref, rk_ref, rv_ref, hext_ref):
    t = pl.program_id(0)
    nt = pl.num_programs(0)
    sh = mod_ref[0, 0, 0:1, :]
    sc = mod_ref[0, 0, 1:2, :]
    nw = nw_ref[...]
    hb = _norm_mod(_tile_rows(c_ref, x_ref), nw, sc, sh).astype(BF16)
    hb_ref[0] = hb

    grow = _dot_nt(wgt_ref[...], hb) + gbt_ref[:, 0:1]
    for c in range(TM // CHUNK):
        gr, tot = _gate_transform(grow[:, c * CHUNK:(c + 1) * CHUNK], 0)
        aend = gr + pltpu.roll(tot, 4 * M_HEADS - 1, 0)
        amax = jnp.broadcast_to(jnp.max(aend, axis=1, keepdims=True), aend.shape)
        for h in range(M_HEADS):
            hp_, o0 = h // 2, 8 * (h % 2)
            gr_ref[0, hp_, c, o0:o0 + 4, :] = gr[4 * h:4 * h + 4]
            for d in range(2):
                gr_ref[0, hp_, c, o0 + 4 + d:o0 + 5 + d, :] = tot[4 * h + 2 * d + 1:4 * h + 2 * d + 2]
                gr_ref[0, hp_, c, o0 + 6 + d:o0 + 7 + d, :] = amax[4 * h + 2 * d:4 * h + 2 * d + 1]

    hp = _norm_mod(xp_ref[0], nw, sc, sh)
    hn = _norm_mod(xn_ref[0], nw, sc, sh)
    hp = jnp.where(t <= 1, 0.0, hp)
    hn = jnp.where((t == 0) | (t == nt - 1), 0.0, hn)
    hext_ref[0:TM, :] = hb
    hext_ref[TM:, :] = jnp.concatenate([hn, hp], axis=0).astype(BF16)
    hext = hext_ref[...]
    ext = TM + 2 * HALO

    for j, (dst, scale) in enumerate(((mq_ref, 1.0), (mk_ref, M_HEAD_DIM ** -0.5))):
        p = _dot(hext, wqk_ref[:, j * M_WIDTH:(j + 1) * M_WIDTH])
        cw = cw_ref[:, j * M_WIDTH:(j + 1) * M_WIDTH]
        prev = pltpu.roll(p, 1, 0)[0:TM]
        nxt = pltpu.roll(p, ext - 1, 0)[0:TM]
        conv = prev * cw[0:1] + p[0:TM] * cw[1:2] + nxt * cw[2:3]
        act = conv * _sigmoid(conv) * scale
        for h in range(M_HEADS):
            dst[0, h] = act[:, h * M_HEAD_DIM:(h + 1) * M_HEAD_DIM].astype(BF16)

    off = 0
    p = _dot(hb, wr_ref[:, off:off + M_WIDTH])
    off += M_WIDTH
    for h in range(M_HEADS):
        for c in range(TM // CHUNK):
            blk = p[c * CHUNK:(c + 1) * CHUNK, h * M_HEAD_DIM:(h + 1) * M_HEAD_DIM]
            mv_ref[0, h, c] = blk.T.astype(BF16)

    lane = lax.broadcasted_iota(jnp.int32, (TM, LANES), 1)
    first16 = (lane % 32) < 16
    ca = ca_ref[...]
    sa = sa_ref[...]

    def rope_a(xs):
        partner = jnp.where(first16, pltpu.roll(xs, LANES - 16, 1), pltpu.roll(xs, 16, 1))
        return xs * ca + partner * sa

    p = _dot(hb, wr_ref[:, off:off + A_WIDTH])
    off += A_WIDTH
    for s in range(A_WIDTH // LANES):
        r = (rope_a(p[:, s * LANES:(s + 1) * LANES]) * (A_HEAD_DIM ** -0.5 * LOG2E)).astype(BF16)
        aq_ref[0, 2 * s] = r[:, 0:A_HEAD_DIM]
        aq_ref[0, 2 * s + 1] = r[:, A_HEAD_DIM:]
    p = _dot(hb, wr_ref[:, off:off + 2 * A_KV_WIDTH])
    off += 2 * A_KV_WIDTH
    r = rope_a(p[:, 0:LANES]).astype(BF16)
    for h in range(A_KV_HEADS):
        ak_ref[0, h] = r[:, h * A_HEAD_DIM:(h + 1) * A_HEAD_DIM]
    pv = p[:, LANES:]
    for h in range(A_KV_HEADS):
        vh = pv if h == 0 else pltpu.roll(pv, A_HEAD_DIM, 1)
        ve = jnp.where(lane < A_HEAD_DIM, vh, 1.0)
        for c in range(TM // A_BLOCK):
            av_ref[0, h, c] = ve[c * A_BLOCK:(c + 1) * A_BLOCK].T.astype(BF16)

    cr = cr_ref[...]
    sr = sr_ref[...]
    for is_k in (False, True):
        p = _dot(hb, wr_ref[:, off:off + R_QK_WIDTH])
        off += R_QK_WIDTH
        for h in range(R_HEADS):
            xs = p[:, h * R_QK_DIM:(h + 1) * R_QK_DIM]
            rot = xs * cr + pltpu.roll(xs, R_QK_DIM // 2, 1) * sr
            if is_k:
                rot = rot * R_QK_DIM ** -0.5
                for c in range(TM // CHUNK):
                    rk_ref[0, h, c] = rot[c * CHUNK:(c + 1) * CHUNK].T.astype(BF16)
            else:
                rq_ref[0, h] = rot.astype(BF16)
    p = _dot(hb, wr_ref[:, off:off + R_V_WIDTH])
    for h in range(R_HEADS):
        rv_ref[0, h] = p[:, h * R_V_DIM:(h + 1) * R_V_DIM].astype(BF16)


def _stream_specs(ctx, xs):
    hb = TM // HALO
    D = xs.shape[-1]
    off = 0 if ctx is None else 1
    last = xs.shape[1] // HALO - 1
    return [
        pl.BlockSpec((BB, TM if off else HALO, D), lambda t, b: (b, 0, 0)),
        pl.BlockSpec((BB, TM, D), lambda t, b: (b, jnp.maximum(t - off, 0), 0)),
        pl.BlockSpec((BB, HALO, D), lambda t, b: (b, jnp.maximum((t - off) * hb - 1, 0), 0)),
        pl.BlockSpec((BB, HALO, D), lambda t, b: (b, jnp.clip((t - off + 1) * hb, 0, last), 0)),
    ]


def _pre_call(ctx, xs, modsel, nw, wqk, cw, wr, wgt, gbt, tabs):
    B, _, D = xs.shape
    T = xs.shape[1] + (0 if ctx is None else ctx.shape[1])
    nt = T // TM
    nch = T // CHUNK

    def tile(t, b):
        return (b, t, 0)

    def head_tile(t, b):
        return (b, 0, t, 0)

    in_specs = _stream_specs(ctx, xs) + [
        pl.BlockSpec((BB, 1, 6, D), lambda t, b: (b, jnp.minimum(t, 1), 0, 0)),
        _const_spec(nw.shape), _const_spec(wqk.shape), _const_spec(cw.shape), _const_spec(wr.shape),
        _const_spec(wgt.shape), _const_spec(gbt.shape),
    ] + [pl.BlockSpec((TM, LANES), lambda t, b: (t, 0)) for _ in range(4)]
    out_shape = [
        jax.ShapeDtypeStruct((B, T, D), BF16),
        jax.ShapeDtypeStruct((B, M_HEADS, T, M_HEAD_DIM), BF16),
        jax.ShapeDtypeStruct((B, M_HEADS, T, M_HEAD_DIM), BF16),
        jax.ShapeDtypeStruct((B, M_HEADS, nch, M_HEAD_DIM, CHUNK), BF16),
        jax.ShapeDtypeStruct((B, 2, nch, 16, CHUNK), F32),
        jax.ShapeDtypeStruct((B, A_HEADS, T, A_HEAD_DIM), BF16),
        jax.ShapeDtypeStruct((B, A_KV_HEADS, T, A_HEAD_DIM), BF16),
        jax.ShapeDtypeStruct((B, A_KV_HEADS, T // A_BLOCK, LANES, A_BLOCK), BF16),
        jax.ShapeDtypeStruct((B, R_HEADS, T, R_QK_DIM), BF16),
        jax.ShapeDtypeStruct((B, R_HEADS, nch, R_QK_DIM, CHUNK), BF16),
        jax.ShapeDtypeStruct((B, R_HEADS, T, R_V_DIM), BF16),
    ]
    out_specs = [
        pl.BlockSpec((BB, TM, D), tile),
        pl.BlockSpec((BB, M_HEADS, TM, M_HEAD_DIM), head_tile),
        pl.BlockSpec((BB, M_HEADS, TM, M_HEAD_DIM), head_tile),
        pl.BlockSpec((BB, M_HEADS, TM // CHUNK, M_HEAD_DIM, CHUNK), lambda t, b: (b, 0, t, 0, 0)),
        pl.BlockSpec((BB, 2, TM // CHUNK, 16, CHUNK), lambda t, b: (b, 0, t, 0, 0)),
        pl.BlockSpec((BB, A_HEADS, TM, A_HEAD_DIM), head_tile),
        pl.BlockSpec((BB, A_KV_HEADS, TM, A_HEAD_DIM), head_tile),
        pl.BlockSpec((BB, A_KV_HEADS, TM // A_BLOCK, LANES, A_BLOCK), lambda t, b: (b, 0, t, 0, 0)),
        pl.BlockSpec((BB, R_HEADS, TM, R_QK_DIM), head_tile),
        pl.BlockSpec((BB, R_HEADS, TM // CHUNK, R_QK_DIM, CHUNK), lambda t, b: (b, 0, t, 0, 0)),
        pl.BlockSpec((BB, R_HEADS, TM, R_V_DIM), head_tile),
    ]
    return pl.pallas_call(
        functools.partial(_pre_kernel, split=ctx is not None),
        grid=(nt, B // BB),
        in_specs=in_specs,
        out_specs=out_specs,
        out_shape=out_shape,
        scratch_shapes=[pltpu.VMEM((BB, TM + 2 * HALO, D), BF16)],
        compiler_params=_params(("parallel", "parallel")),
        name="pre_proj",
    )(xs if ctx is None else ctx, xs, xs, xs, modsel, nw, wqk, cw, wr, wgt, gbt, *tabs)


def _bwd_chunk(i, n_ctx, n_all):
    if isinstance(i, int):
        return n_ctx - 1 - i if i < n_ctx else n_all - 1 + n_ctx - i
    return n_all - 1 + n_ctx - i


def _chunk_start(c):
    return c * CHUNK if isinstance(c, int) else pl.multiple_of(c * CHUNK, CHUNK)


def _chunk_loop(body, n_ctx, n_all, unroll):
    for i in range(n_ctx):
        body(i, 0)
    n_lat = n_all - n_ctx
    while n_lat % unroll:
        unroll //= 2
    lax.fori_loop(n_ctx, n_all, body, 0, unroll=unroll)


def _mlstm_kernel(q_ref, k_ref, vt_ref, gr_ref, o_ref, c_ref, n_ref, m_ref,
                  cs_ref, ns_ref, ms_ref, qp_ref, nq_ref, *, n_ctx):
    n_all = gr_ref.shape[2]
    dh = q_ref.shape[3]
    c_ref[...] = jnp.zeros(c_ref.shape, F32)
    n_ref[...] = jnp.zeros(n_ref.shape, F32)
    m_ref[...] = jnp.zeros(m_ref.shape, F32)

    def scan_unit(c, hh, d):
        u = 2 * hh + d
        c0 = _chunk_start(c)
        k = k_ref[0, hh, pl.ds(c0, CHUNK), :]
        vt = vt_ref[0, hh, c]
        rows = gr_ref[0, 0, c, 8 * hh:8 * hh + 8, :]
        tot = rows[4 + d:5 + d]
        aend = rows[2 * d:2 * d + 1] + tot
        amax = rows[6 + d:7 + d]
        cst = c_ref[u]
        nv = n_ref[u]
        m = m_ref[u]
        cs_ref[u, c] = cst.astype(BF16)
        ns_ref[u, c] = jnp.broadcast_to(nv, (16, dh)).astype(BF16)
        ms_ref[u, c] = jnp.broadcast_to(m, (8, CHUNK))
        m_new = jnp.maximum(tot + m, amax)
        w = jnp.exp(aend - m_new)
        dec = jnp.exp(tot + m - m_new)
        vw = (vt.astype(F32) * w).astype(BF16)
        c_ref[u] = dec * cst + _dot(vw, k)
        wk = _dot(jnp.broadcast_to(w, (16, CHUNK)).astype(BF16), k)
        n_ref[u] = dec * nv + wk[0:1]
        m_ref[u] = m_new

    def scan_step(i, carry):
        cb = _bwd_chunk(i, n_ctx, n_all)
        for hh in range(2):
            scan_unit(i, hh, 0)
            scan_unit(cb, hh, 1)
        return carry

    _chunk_loop(scan_step, n_ctx, n_all, 4)

    row = lax.broadcasted_iota(jnp.int32, (CHUNK, CHUNK), 0)
    col = lax.broadcasted_iota(jnp.int32, (CHUNK, CHUNK), 1)
    tri = (row <= col, row >= col)

    def query_products(c, slot):
        c0 = _chunk_start(c)
        for hh in range(2):
            q = q_ref[0, hh, pl.ds(c0, CHUNK), :]
            k = k_ref[0, hh, pl.ds(c0, CHUNK), :]
            qp_ref[slot, hh, 0] = _dot_nt(k, q)
            for d in range(2):
                u = 2 * hh + d
                qp_ref[slot, hh, 1 + d] = _dot_nt(cs_ref[u, c], q)
                nq_ref[slot, u] = _dot_nt(ns_ref[u, c], q)[0:8]

    def outputs(c, slot):
        c0 = _chunk_start(c)
        gcv = jnp.concatenate([gr_ref[0, 0, c], jnp.zeros((CHUNK - 16, CHUNK), F32)], axis=0).T
        for hh in range(2):
            vt = vt_ref[0, hh, c]
            rows = gr_ref[0, 0, c, 8 * hh:8 * hh + 8, :]
            st = qp_ref[slot, hh, 0]
            acc = None
            for d in range(2):
                u = 2 * hh + d
                a0c = gcv[:, 8 * hh + 2 * d:8 * hh + 2 * d + 1]
                b = rows[2 * d + 1:2 * d + 2]
                m = ms_ref[u, c, 0:1, :]
                dlog = jnp.where(tri[d], a0c + b, NEG_INF)
                mj = jnp.maximum(b + m, jnp.max(dlog, axis=0, keepdims=True))
                sd = st * jnp.exp(dlog - mj)
                iw = jnp.exp(b + m - mj)
                den = iw * nq_ref[slot, u, 0:1, :] + jnp.sum(sd, axis=0, keepdims=True)
                inv = 1.0 / jnp.maximum(jnp.abs(den), jnp.exp(-mj))
                ht = (_dot(vt, sd.astype(BF16)) + iw * qp_ref[slot, hh, 1 + d]) * inv
                acc = ht if acc is None else acc + ht
            o_ref[0, hh, pl.ds(c0, CHUNK), :] = acc.T

    query_products(0, 0)

    def out_pair(i, carry):
        c = 2 * i
        query_products(c + 1, 1)
        outputs(c, 0)
        query_products(jnp.minimum(c + 2, n_all - 1), 0)
        outputs(c + 1, 1)
        return carry

    n_pairs = n_all // 2
    unroll = 4
    first = n_pairs % unroll
    for i in range(first):
        out_pair(i, 0)
    lax.fori_loop(first, n_pairs, out_pair, 0, unroll=unroll)


def _mlstm_call(mq, mk, mvt, gr, n_ctx):
    B, H, T, dh = mq.shape
    nch = T // CHUNK
    qspec = pl.BlockSpec((1, 2, T, dh), lambda b, p: (b, p, 0, 0))
    return pl.pallas_call(
        functools.partial(_mlstm_kernel, n_ctx=n_ctx),
        grid=(B, H // 2),
        in_specs=[qspec, qspec,
                  pl.BlockSpec((1, 2, nch, dh, CHUNK), lambda b, p: (b, p, 0, 0, 0)),
                  pl.BlockSpec((1, 1, nch, 16, CHUNK), lambda b, p: (b, p, 0, 0, 0))],
        out_specs=pl.BlockSpec((1, 2, T, dh), lambda b, p: (b, p, 0, 0)),
        out_shape=jax.ShapeDtypeStruct((B, H, T, dh), F32),
        scratch_shapes=[pltpu.VMEM((4, dh, dh), F32), pltpu.VMEM((4, 1, dh), F32),
                        pltpu.VMEM((4, 1, CHUNK), F32),
                        pltpu.VMEM((4, nch, dh, dh), BF16), pltpu.VMEM((4, nch, 16, dh), BF16),
                        pltpu.VMEM((4, nch, 8, CHUNK), F32),
                        pltpu.VMEM((2, 2, 3, dh, CHUNK), F32), pltpu.VMEM((2, 4, 8, CHUNK), F32)],
        compiler_params=_params(("parallel", "parallel")),
        name="mlstm",
    )(mq, mk, mvt, gr)


def _ret_kernel(q_ref, kt_ref, v_ref, lg_ref, o_ref, st_ref, sts_ref, dec_ref, ib_ref, wb_ref, *, n_ctx):
    n_all = kt_ref.shape[2]
    dk = q_ref.shape[3]
    dv = v_ref.shape[3]
    st_ref[...] = jnp.zeros(st_ref.shape, F32)
    row = lax.broadcasted_iota(jnp.int32, (CHUNK, CHUNK), 0).astype(F32)
    col = lax.broadcasted_iota(jnp.int32, (CHUNK, CHUNK), 1).astype(F32)
    cds = []
    dec = None
    for d in range(2):
        lg = _log_sigmoid(lg_ref[0, d])[0:1]
        lgk = lg[:, 0:CHUNK]
        diff = row - col if d == 0 else col - row
        dd = jnp.where(diff >= 0, jnp.exp(lgk * jnp.maximum(diff, 0.0)), 0.0)
        dec = dd if dec is None else dec + dd
        pin = row if d == 0 else CHUNK - 1.0 - row
        ib_ref[d] = jnp.exp(lg * (jnp.concatenate([pin, pin], axis=1) + 1.0))
        pkey = col if d == 0 else CHUNK - 1.0 - col
        wb_ref[d] = jnp.exp(lgk * (CHUNK - 1.0 - pkey))
        cds.append(jnp.exp(lg * CHUNK))
    dec_ref[...] = dec

    def scan_step(i, carry):
        cb = _bwd_chunk(i, n_ctx, n_all)
        for d, c in ((0, i), (1, cb)):
            c0 = _chunk_start(c)
            st = st_ref[d]
            sts_ref[c, :, d * dv:(d + 1) * dv] = st.astype(BF16)
            kw = (kt_ref[0, 0, c].astype(F32) * wb_ref[d]).astype(BF16)
            st_ref[d] = cds[d] * st + _dot(kw, v_ref[0, 0, pl.ds(c0, CHUNK), :])
        return carry

    _chunk_loop(scan_step, n_ctx, n_all, 8)

    def out_step(c, carry):
        c0 = _chunk_start(c)
        q = q_ref[0, 0, pl.ds(c0, CHUNK), :]
        s = (_dot(q, kt_ref[0, 0, c]) * dec_ref[...]).astype(BF16)
        qst = _dot(q, sts_ref[c])
        o_ref[0, 0, pl.ds(c0, CHUNK), :] = (_dot(s, v_ref[0, 0, pl.ds(c0, CHUNK), :])
                                            + ib_ref[0] * qst[:, 0:dv] + ib_ref[1] * qst[:, dv:])
        return carry

    _chunk_loop(out_step, n_ctx, n_all, 8)


def _ret_call(rq, rkt, rv, lgb, n_ctx):
    B, H, T, dk = rq.shape
    dv = rv.shape[-1]
    nch = T // CHUNK
    return pl.pallas_call(
        functools.partial(_ret_kernel, n_ctx=n_ctx),
        grid=(B, H),
        in_specs=[pl.BlockSpec((1, 1, T, dk), lambda b, h: (b, h, 0, 0)),
                  pl.BlockSpec((1, 1, nch, dk, CHUNK), lambda b, h: (b, h, 0, 0, 0)),
                  pl.BlockSpec((1, 1, T, dv), lambda b, h: (b, h, 0, 0)),
                  pl.BlockSpec((1, 2, 8, dv), lambda b, h: (h, 0, 0, 0))],
        out_specs=pl.BlockSpec((1, 1, T, dv), lambda b, h: (b, h, 0, 0)),
        out_shape=jax.ShapeDtypeStruct((B, H, T, dv), F32),
        scratch_shapes=[pltpu.VMEM((2, dk, dv), F32), pltpu.VMEM((nch, dk, 2 * dv), BF16),
                        pltpu.VMEM((CHUNK, CHUNK), F32), pltpu.VMEM((2, CHUNK, dv), F32),
                        pltpu.VMEM((2, dk, CHUNK), F32)],
        compiler_params=_params(("parallel", "parallel")),
        name="retention",
    )(rq, rkt, rv, lgb)


def _attn_kernel(q_ref, k_ref, vt_ref, sink_ref, o_ref, bias_ref, s_ref, *, n_ctx_tok, need_ctx):
    T = q_ref.shape[2]
    L = n_ctx_tok
    S = T - L
    nb = S // A_BLOCK
    nspan = 3
    span = nspan * A_BLOCK
    ncb = L // A_BLOCK
    cols = A_GROUP * A_BLOCK
    kc = k_ref[0, 0, 0:L, :]
    vtc = jnp.concatenate([vt_ref[0, 0, j] for j in range(ncb)], axis=1)
    sink = sink_ref[0]

    kpos = lax.broadcasted_iota(jnp.int32, (span, A_BLOCK), 0)
    qpos = lax.broadcasted_iota(jnp.int32, (span, A_BLOCK), 1)
    for delta in range(nspan):
        bias_ref[delta] = jnp.where(jnp.abs(qpos + delta * A_BLOCK - kpos) <= A_WINDOW, 0.0, NEG_INF)

    def finish(blk, acc, m):
        den = acc[A_HEAD_DIM:A_HEAD_DIM + 1] + jnp.exp2(sink - m)
        ot = (acc[0:A_HEAD_DIM] * (1.0 / den)).astype(BF16)
        for g in range(A_GROUP):
            o_ref[0, blk, g * A_HEAD_DIM:(g + 1) * A_HEAD_DIM, :] = ot[:, g * A_BLOCK:(g + 1) * A_BLOCK]

    for cbk in range(ncb):
        if need_ctx:
            qs = q_ref[0, :, cbk * A_BLOCK:(cbk + 1) * A_BLOCK, :].reshape(cols, A_HEAD_DIM)
            s = _dot_nt(kc, qs)
            m = jnp.maximum(jnp.max(s, axis=0, keepdims=True), sink)
            finish(cbk, _dot(vtc, jnp.exp2(s - m).astype(BF16)), m)
        else:
            o_ref[0, cbk] = jnp.zeros(o_ref.shape[2:], BF16)

    def first_key_block(n):
        return jnp.clip(n - 1, 0, nb - nspan)

    def scores(n, slot):
        q0 = pl.multiple_of(L + n * A_BLOCK, A_BLOCK)
        kb = first_key_block(n)
        k0 = pl.multiple_of(L + kb * A_BLOCK, A_BLOCK)
        qs = q_ref[0, :, pl.ds(q0, A_BLOCK), :].reshape(cols, A_HEAD_DIM)
        bias = bias_ref[n - kb]
        s_ref[slot, 0:span, :] = (_dot_nt(k_ref[0, 0, pl.ds(k0, span), :], qs)
                                  + jnp.concatenate([bias] * A_GROUP, axis=1))
        s_ref[slot, span:, :] = _dot_nt(kc, qs)

    def softmax_pv(n, slot):
        kb = first_key_block(n)
        s = s_ref[slot]
        m = jnp.maximum(jnp.max(s, axis=0, keepdims=True), sink)
        p = jnp.exp2(s - m).astype(BF16)
        vt = jnp.concatenate([vt_ref[0, 0, ncb + kb + j] for j in range(nspan)] + [vtc], axis=1)
        finish(ncb + n, _dot(vt, p), m)

    scores(0, 0)

    def block_pair(i, carry):
        n = 2 * i
        scores(n + 1, 1)
        softmax_pv(n, 0)
        scores(jnp.minimum(n + 2, nb - 1), 0)
        softmax_pv(n + 1, 1)
        return carry

    n_pairs = nb // 2
    first = n_pairs % 2
    if first:
        block_pair(0, 0)
    lax.fori_loop(first, n_pairs, block_pair, 0, unroll=2)


def _attn_call(aq, ak, avt, sink_rows, n_ctx_tok, need_ctx):
    B, _, T, hd = aq.shape
    nblk = T // A_BLOCK
    cols = A_GROUP * A_BLOCK
    return pl.pallas_call(
        functools.partial(_attn_kernel, n_ctx_tok=n_ctx_tok, need_ctx=need_ctx),
        grid=(B, A_KV_HEADS),
        in_specs=[pl.BlockSpec((1, A_GROUP, T, hd), lambda b, h: (b, h, 0, 0)),
                  pl.BlockSpec((1, 1, T, hd), lambda b, h: (b, h, 0, 0)),
                  pl.BlockSpec((1, 1, nblk, LANES, A_BLOCK), lambda b, h: (b, h, 0, 0, 0)),
                  pl.BlockSpec((1, 1, cols), lambda b, h: (h, 0, 0))],
        out_specs=pl.BlockSpec((1, nblk, A_GROUP * hd, A_BLOCK), lambda b, h: (b, 0, h, 0)),
        out_shape=jax.ShapeDtypeStruct((B, nblk, A_WIDTH, A_BLOCK), BF16),
        scratch_shapes=[pltpu.VMEM((3, 3 * A_BLOCK, A_BLOCK), F32),
                        pltpu.VMEM((2, 3 * A_BLOCK + n_ctx_tok, cols), F32)],
        compiler_params=_params(("parallel", "parallel")),
        name="window_attn",
    )(aq, ak, avt, sink_rows)


def _head_norm(y):
    mu = jnp.mean(y, axis=-1, keepdims=True)
    yc = y - mu
    var = jnp.mean(yc * yc, axis=-1, keepdims=True)
    return yc * lax.rsqrt(var + NORM_EPS)


def _post_kernel(c_ref, x_ref, hb_ref, hm_ref, ya_ref, hr_ref, mod_ref, wg_ref, mnw_ref, rnw_ref,
                 wbm_ref, wba_ref, wbr_ref, wo_ref, o_ref, *, split):
    for i in range(x_ref.shape[0]):
        hb = hb_ref[i]
        off = 0
        hm = jnp.concatenate([_head_norm(hm_ref[i, h]) for h in range(M_HEADS)], axis=1) * mnw_ref[...]
        ym = (_sigmoid(_dot(hb, wg_ref[:, off:off + M_WIDTH])) * hm).astype(BF16)
        off += M_WIDTH
        hr = jnp.concatenate([_head_norm(hr_ref[i, h]) for h in range(R_HEADS)], axis=1) * rnw_ref[...]
        rg = _dot(hb, wg_ref[:, off:off + R_V_WIDTH])
        yr = (rg * _sigmoid(rg) * hr).astype(BF16)
        off += R_V_WIDTH
        z = _sigmoid(_dot(hb, wg_ref[:, off:off + D_MODEL])) * _dot(ym, wbm_ref[...])
        off += D_MODEL
        pa = jnp.concatenate([_dot_tn(ya_ref[i, c], wba_ref[...]) for c in range(TM // A_BLOCK)], axis=0)
        z = z + _sigmoid(_dot(hb, wg_ref[:, off:off + D_MODEL])) * pa
        off += D_MODEL
        z = z + _sigmoid(_dot(hb, wg_ref[:, off:off + D_MODEL])) * _dot(yr, wbr_ref[...])
        y = _dot(z.astype(BF16), wo_ref[...])
        one = pl.ds(i, 1)
        xres = _tile_rows(c_ref.at[one] if split else None, x_ref.at[one])
        o_ref[i] = xres + mod_ref[i, 0, 2:3, :] * y


def _post_call(ctx, xs, hb, hm, ya, hr, modsel, wg, mnw, rnw, wbm, wba, wbr, wo, t_off):
    B, T, D = hb.shape
    nt = T // TM - t_off
    split = ctx is not None
    assert not (split and t_off)

    def tile(t, b):
        return (b, t + t_off, 0)

    def head_tile(t, b):
        return (b, 0, t + t_off, 0)

    return pl.pallas_call(
        functools.partial(_post_kernel, split=split),
        grid=(nt, B // BB),
        in_specs=[pl.BlockSpec((BB, TM if split else HALO, D), lambda t, b: (b, 0, 0)),
                  pl.BlockSpec((BB, TM, D), (lambda t, b: (b, jnp.maximum(t - 1, 0), 0)) if split else tile),
                  pl.BlockSpec((BB, TM, D), tile),
                  pl.BlockSpec((BB, M_HEADS, TM, M_HEAD_DIM), head_tile),
                  pl.BlockSpec((BB, TM // A_BLOCK, A_WIDTH, A_BLOCK), lambda t, b: (b, t + t_off, 0, 0)),
                  pl.BlockSpec((BB, R_HEADS, TM, R_V_DIM), head_tile),
                  pl.BlockSpec((BB, 1, 6, D), lambda t, b: (b, jnp.minimum(t + t_off, 1), 0, 0)),
                  _const_spec(wg.shape), _const_spec(mnw.shape), _const_spec(rnw.shape),
                  _const_spec(wbm.shape), _const_spec(wba.shape), _const_spec(wbr.shape),
                  _const_spec(wo.shape)],
        out_specs=pl.BlockSpec((BB, TM, D), lambda t, b: (b, t, 0)),
        out_shape=jax.ShapeDtypeStruct((B, nt * TM, D), F32),
        compiler_params=_params(("parallel", "parallel")),
        name="post_merge",
    )(ctx if split else xs, xs, hb, hm, ya, hr, modsel, wg, mnw, rnw, wbm, wba, wbr, wo)


def _ffn_kernel(x_ref, xp_ref, xn_ref, mod_ref, nw_ref, wa_ref, wb_ref, cw_ref, cb_ref, wd_ref, fw_ref,
                o_ref, hext_ref, u_ref, *, has_ctx, final):
    t = pl.program_id(0)
    nt = pl.num_programs(0)
    first_lat = 1 if has_ctx else 0
    nw = nw_ref[...]
    ext = TM + 2 * HALO
    for i in range(x_ref.shape[0]):
        sh = mod_ref[i, 0, 3:4, :]
        sc = mod_ref[i, 0, 4:5, :]
        g2 = mod_ref[i, 0, 5:6, :]
        x = x_ref[i]
        hp = _norm_mod(xp_ref[i], nw, sc, sh)
        hn = _norm_mod(xn_ref[i], nw, sc, sh)
        hp = jnp.where(t <= first_lat, 0.0, hp)
        hn = jnp.where((t == first_lat - 1) | (t == nt - 1), 0.0, hn)
        hext_ref[i, 0:TM, :] = _norm_mod(x, nw, sc, sh).astype(BF16)
        hext_ref[i, TM:, :] = jnp.concatenate([hn, hp], axis=0).astype(BF16)
        for j in range(FFN_DIM // FFN_NC):
            cs = slice(j * FFN_NC, (j + 1) * FFN_NC)
            a = _dot(hext_ref[i], wa_ref[:, cs])
            b = _dot(hext_ref[i, 0:TM, :], wb_ref[:, cs])
            cw = cw_ref[:, cs]
            prev = pltpu.roll(a, 1, 0)[0:TM]
            nxt = pltpu.roll(a, ext - 1, 0)[0:TM]
            conv = prev * cw[0:1] + a[0:TM] * cw[1:2] + nxt * cw[2:3] + cb_ref[:, cs]
            u_ref[i, :, cs] = (conv * _sigmoid(conv) * b).astype(BF16)
        y = x + g2 * _dot(u_ref[i], wd_ref[...])
        if final:
            ms = jnp.mean(y * y, axis=-1, keepdims=True)
            y = y * lax.rsqrt(ms + NORM_EPS) * fw_ref[...]
        o_ref[i] = y


def _ffn_call(xs, modsel, nw, wa, wb, cw, cb, wd, fw, has_ctx, final):
    B, T, D = xs.shape
    nt = T // TM
    hb16 = TM // HALO
    first_lat = 1 if has_ctx else 0
    tile = pl.BlockSpec((BB, TM, D), lambda t, b: (b, t, 0))
    return pl.pallas_call(
        functools.partial(_ffn_kernel, has_ctx=has_ctx, final=final),
        grid=(nt, B // BB),
        in_specs=[tile,
                  pl.BlockSpec((BB, HALO, D), lambda t, b: (b, jnp.maximum(t * hb16 - 1, 0), 0)),
                  pl.BlockSpec((BB, HALO, D), lambda t, b: (b, jnp.minimum((t + 1) * hb16, T // HALO - 1), 0)),
                  pl.BlockSpec((BB, 1, 6, D), lambda t, b: (b, jnp.minimum(t + 1 - first_lat, 1), 0, 0)),
                  _const_spec(nw.shape), _const_spec(wa.shape), _const_spec(wb.shape),
                  _const_spec(cw.shape), _const_spec(cb.shape), _const_spec(wd.shape),
                  _const_spec(fw.shape)],
        out_specs=tile,
        out_shape=jax.ShapeDtypeStruct((B, T, D), F32),
        scratch_shapes=[pltpu.VMEM((BB, TM + 2 * HALO, D), BF16), pltpu.VMEM((BB, TM, FFN_DIM), BF16)],
        compiler_params=_params(("parallel", "parallel")),
        name="conv_ffn",
    )(xs, xs, xs, modsel, nw, wa, wb, cw, cb, wd, fw)


def _split_cols(w):
    out = []
    acc = 0
    for s in IN_SPLITS:
        out.append(w[:, acc:acc + s])
        acc += s
    return out


def _rope_tables(L, S):
    T = L + S
    j = jnp.arange(LANES)
    jj = j % 32
    inv_a = ROPE_BASE ** (-(jj % 16).astype(F32) / 16.0)
    t = jnp.arange(S)
    pos = jnp.where(((j % A_HEAD_DIM) // 32 == 0)[None, :], (t // GRID_W)[:, None], (t % GRID_W)[:, None])
    ang = pos.astype(F32) * inv_a[None, :]
    sign = jnp.where(jj < 16, -1.0, 1.0)[None, :]
    ca = jnp.concatenate([jnp.ones((L, LANES), F32), jnp.cos(ang)], axis=0)
    sa = jnp.concatenate([jnp.zeros((L, LANES), F32), jnp.sin(ang) * sign], axis=0)
    half = R_QK_DIM // 2
    inv_r = ROPE_BASE ** (-(j % half).astype(F32) / half)
    angr = jnp.arange(T).astype(F32)[:, None] * inv_r[None, :]
    signr = jnp.where(j < half, -1.0, 1.0)[None, :]
    return ca, sa, jnp.cos(angr), jnp.sin(angr) * signr


def kernel(x, c, ctx, c_ctx, mod_w, mod_b, norm1_w, norm2_w, w_in, m_gate_bias, m_conv_w, m_norm_w,
           a_sink, ret_logit, ret_norm_w, w_br_m, w_br_a, w_br_r, w_out, ffn_up, ffn_conv_w, ffn_conv_b,
           ffn_down, final_norm_w):
    B, S, D = x.shape
    L = ctx.shape[1]
    depth = mod_w.shape[0]
    assert D == D_MODEL and L == TM and S % TM == 0 and S >= 3 * A_BLOCK and B % BB == 0
    T = L + S
    n_ctx = L // CHUNK

    rows = -(-(B + 1) // 8) * 8
    cpad = jnp.zeros((rows, D), F32).at[:B].set(c).at[B].set(c_ctx)
    mods = _modulation(cpad, mod_w, mod_b)
    tabs = _rope_tables(L, S)
    stream = (ctx, x) if depth > 1 else (None, jnp.concatenate([ctx, x], axis=1))

    gperm = jnp.arange(4 * M_HEADS).reshape(4, M_HEADS).T.reshape(-1)

    for l in range(depth):
        last = l == depth - 1
        lat = mods[l, :B].reshape(B, 6, D)
        cm = jnp.broadcast_to(mods[l, B].reshape(1, 6, D), (B, 6, D))
        modsel = jnp.stack([cm, lat], axis=1)

        (w_mq, w_mk, w_mv, w_mo, w_mg, w_aq, w_ak, w_av,
         w_rq, w_rk, w_rv, w_rg, w_gm, w_ga, w_gr) = _split_cols(w_in[l])
        wqk = jnp.concatenate([w_mq, w_mk], axis=1).astype(BF16)
        wr = jnp.concatenate([w_mv, w_aq, w_ak, w_av, w_rq, w_rk, w_rv], axis=1).astype(BF16)
        wgt = w_mg[:, gperm].T.astype(BF16)
        gbt = jnp.broadcast_to(m_gate_bias[l][gperm].reshape(-1, 1), (4 * M_HEADS, LANES))

        (hb, mq, mk, mv, gr, aq, ak, av, rq, rk, rv) = _pre_call(
            *stream, modsel, norm1_w[l].reshape(1, D), wqk, m_conv_w[l], wr, wgt, gbt, tabs)

        hm = _mlstm_call(mq, mk, mv, gr, n_ctx)
        sink_rows = jnp.repeat(a_sink[l].reshape(A_KV_HEADS, A_GROUP), A_BLOCK, axis=1)[:, None, :]
        ya = _attn_call(aq, ak, av, sink_rows.astype(F32) * LOG2E, L, not last)
        lgb = jnp.broadcast_to(ret_logit[l].T[:, :, None, None], (R_HEADS, 2, 8, R_V_DIM)).astype(F32)
        hr = _ret_call(rq, rk, rv, lgb, n_ctx)

        t_off = 1 if last else 0
        wgates = jnp.concatenate([w_mo, w_rg, w_gm, w_ga, w_gr], axis=1).astype(BF16)
        x1 = _post_call(*stream, hb, hm, ya, hr, modsel, wgates, m_norm_w[l].reshape(1, -1),
                        ret_norm_w[l].reshape(1, -1), w_br_m[l].astype(BF16), w_br_a[l].astype(BF16),
                        w_br_r[l].astype(BF16), w_out[l].astype(BF16), t_off)
        xc = _ffn_call(x1, modsel, norm2_w[l].reshape(1, D), ffn_up[l][:, :FFN_DIM].astype(BF16),
                       ffn_up[l][:, FFN_DIM:].astype(BF16), ffn_conv_w[l], ffn_conv_b[l].reshape(1, -1),
                       ffn_down[l].astype(BF16), final_norm_w.reshape(1, D), not last, last)
        stream = (None, xc)
    return xc
```

```python
import functools

import jax
import jax.numpy as jnp
from jax import lax
from jax.experimental import pallas as pl
from jax.experimental.pallas import tpu as pltpu

F32 = jnp.float32
BF16 = jnp.bfloat16

D_MODEL = 1024
GRID_W = 64
NORM_EPS = 1e-6
ROPE_BASE = 10000.0
NEG_INF = -1e30
LOG2E = 1.4426950408889634

M_HEADS = 4
M_HEAD_DIM = 128
M_WIDTH = M_HEADS * M_HEAD_DIM
A_HEADS = 8
A_KV_HEADS = 2
A_GROUP = A_HEADS // A_KV_HEADS
A_HEAD_DIM = 64
A_WIDTH = A_HEADS * A_HEAD_DIM
A_KV_WIDTH = A_KV_HEADS * A_HEAD_DIM
A_WINDOW = 128
A_BLOCK = 128
R_HEADS = 4
R_QK_DIM = 128
R_V_DIM = 256
R_QK_WIDTH = R_HEADS * R_QK_DIM
R_V_WIDTH = R_HEADS * R_V_DIM
FFN_DIM = 2816

IN_SPLITS = (M_WIDTH, M_WIDTH, M_WIDTH, M_WIDTH, 4 * M_HEADS,
             A_WIDTH, A_KV_WIDTH, A_KV_WIDTH,
             R_QK_WIDTH, R_QK_WIDTH, R_V_WIDTH, R_V_WIDTH,
             D_MODEL, D_MODEL, D_MODEL)

TM = 256
BB = 2
HALO = 8
CHUNK = 128
FFN_NC = 256
LANES = 128
VMEM_LIMIT = 56 * 1024 * 1024


def _sigmoid(x):
    return 1.0 / (1.0 + jnp.exp(-x))


def _log_sigmoid(x):
    return jnp.minimum(x, 0.0) - jnp.log1p(jnp.exp(-jnp.abs(x)))


def _dot(a, b):
    return jnp.dot(a, b, preferred_element_type=F32)


def _dot_nt(a, b):
    return lax.dot_general(a, b, (((1,), (1,)), ((), ())), preferred_element_type=F32)


def _dot_tn(a, b):
    return lax.dot_general(a, b, (((0,), (0,)), ((), ())), preferred_element_type=F32)


def _const_spec(shape):
    nd = len(shape)
    return pl.BlockSpec(shape, lambda *_: (0,) * nd, pipeline_mode=pl.Buffered(1))


def _params(sem):
    return pltpu.CompilerParams(dimension_semantics=sem, vmem_limit_bytes=VMEM_LIMIT)


def _mod_kernel(c_ref, w_ref, b_ref, o_ref):
    c = c_ref[...]
    s = (c * _sigmoid(c)).astype(BF16)
    o_ref[0] = _dot(s, w_ref[0].astype(BF16)) + b_ref[0]


def _modulation(cpad, mod_w, mod_b):
    depth, d, n = mod_w.shape
    rows = cpad.shape[0]
    bn = 1024
    return pl.pallas_call(
        _mod_kernel,
        grid=(depth, n // bn),
        in_specs=[pl.BlockSpec((rows, d), lambda l, j: (0, 0)),
                  pl.BlockSpec((1, d, bn), lambda l, j: (l, 0, j)),
                  pl.BlockSpec((1, 1, bn), lambda l, j: (l, 0, j))],
        out_specs=pl.BlockSpec((1, rows, bn), lambda l, j: (l, 0, j)),
        out_shape=jax.ShapeDtypeStruct((depth, rows, n), F32),
        compiler_params=_params(("parallel", "parallel")),
        name="modulation",
    )(cpad, mod_w, mod_b.reshape(depth, 1, n))


def _norm_mod(xv, nw, sc, sh):
    ms = jnp.mean(xv * xv, axis=-1, keepdims=True)
    return (xv * lax.rsqrt(ms + NORM_EPS) * nw) * (1.0 + sc) + sh


def _seg_cumsum(x, axis, reverse):
    n = x.shape[axis]
    idx = lax.broadcasted_iota(jnp.int32, x.shape, axis)
    s = 1
    while s < n:
        if reverse:
            shifted = pltpu.roll(x, n - s, axis)
            x = x + jnp.where(idx < n - s, shifted, 0.0)
        else:
            shifted = pltpu.roll(x, s, axis)
            x = x + jnp.where(idx >= s, shifted, 0.0)
        s *= 2
    return x


def _gate_transform(raw, gate_axis):
    tok_axis = 1 - gate_axis
    k = lax.broadcasted_iota(jnp.int32, raw.shape, gate_axis) % 4
    lf = _log_sigmoid(raw)
    cum_f = _seg_cumsum(jnp.where(k == 1, lf, 0.0), tok_axis, False)
    cum_b = _seg_cumsum(jnp.where(k == 3, lf, 0.0), tok_axis, True)
    bsel = jnp.where(k == 1, cum_f, jnp.where(k == 3, cum_b, 0.0))
    n = raw.shape[gate_axis]
    bnext = pltpu.roll(bsel, n - 1, gate_axis)
    nt = raw.shape[tok_axis]
    tot_f = lax.slice_in_dim(cum_f, nt - 1, nt, axis=tok_axis)
    tot_b = lax.slice_in_dim(cum_b, 0, 1, axis=tok_axis)
    total = jnp.broadcast_to(tot_f + tot_b, raw.shape)
    return jnp.where(k % 2 == 1, bsel, raw - bnext), total


def _pre_kernel(c_ref, x_ref, xp_ref, xn_ref, mod_ref, nw_ref, wqk_ref, cw_ref, wr_ref, wgt_ref,
                gbt_ref, ca_ref, sa_ref, cr_ref, sr_ref, *out_and_scratch, split):
    for i in range(x_ref.shape[0]):
        one = pl.ds(i, 1)
        _pre_tile(c_ref.at[one] if split else None,
                  x_ref.at[one], xp_ref.at[one], xn_ref.at[one], mod_ref.at[one], nw_ref, wqk_ref, cw_ref,
                  wr_ref, wgt_ref, gbt_ref, ca_ref, sa_ref, cr_ref, sr_ref,
                  *[r.at[one] for r in out_and_scratch[:-1]], out_and_scratch[-1].at[i])


def _tile_rows(c_ref, x_ref):
    if c_ref is None:
        return x_ref[0]
    return jnp.where(pl.program_id(0) == 0, c_ref[0], x_ref[0])


def _pre_tile(c_ref, x_ref, xp_ref, xn_ref, mod_ref, nw_ref, wqk_ref, cw_ref, wr_ref, wgt_ref,
              gbt_ref, ca_ref, sa_ref, cr_ref, sr_ref,
              hb_ref, mq_ref, mk_ref, mv_ref, gr_ref, aq_ref, ak_ref, av_ref,
              rq_ref, rk_ref, rv_ref, hext_ref):
    t = pl.program_id(0)
    nt = pl.num_programs(0)
    sh = mod_ref[0, 0, 0:1, :]
    sc = mod_ref[0, 0, 1:2, :]
    nw = nw_ref[...]
    hb = _norm_mod(_tile_rows(c_ref, x_ref), nw, sc, sh).astype(BF16)
    hb_ref[0] = hb

    grow = _dot_nt(wgt_ref[...], hb) + gbt_ref[:, 0:1]
    for c in range(TM // CHUNK):
        gr, tot = _gate_transform(grow[:, c * CHUNK:(c + 1) * CHUNK], 0)
        aend = gr + pltpu.roll(tot, 4 * M_HEADS - 1, 0)
        amax = jnp.broadcast_to(jnp.max(aend, axis=1, keepdims=True), aend.shape)
        for h in range(M_HEADS):
            hp_, o0 = h // 2, 8 * (h % 2)
            gr_ref[0, hp_, c, o0:o0 + 4, :] = gr[4 * h:4 * h + 4]
            for d in range(2):
                gr_ref[0, hp_, c, o0 + 4 + d:o0 + 5 + d, :] = tot[4 * h + 2 * d + 1:4 * h + 2 * d + 2]
                gr_ref[0, hp_, c, o0 + 6 + d:o0 + 7 + d, :] = amax[4 * h + 2 * d:4 * h + 2 * d + 1]

    hp = _norm_mod(xp_ref[0], nw, sc, sh)
    hn = _norm_mod(xn_ref[0], nw, sc, sh)
    hp = jnp.where(t <= 1, 0.0, hp)
    hn = jnp.where((t == 0) | (t == nt - 1), 0.0, hn)
    hext_ref[0:TM, :] = hb
    hext_ref[TM:, :] = jnp.concatenate([hn, hp], axis=0).astype(BF16)
    hext = hext_ref[...]
    ext = TM + 2 * HALO

    for j, (dst, scale) in enumerate(((mq_ref, 1.0), (mk_ref, M_HEAD_DIM ** -0.5))):
        p = _dot(hext, wqk_ref[:, j * M_WIDTH:(j + 1) * M_WIDTH])
        cw = cw_ref[:, j * M_WIDTH:(j + 1) * M_WIDTH]
        prev = pltpu.roll(p, 1, 0)[0:TM]
        nxt = pltpu.roll(p, ext - 1, 0)[0:TM]
        conv = prev * cw[0:1] + p[0:TM] * cw[1:2] + nxt * cw[2:3]
        act = conv * _sigmoid(conv) * scale
        for h in range(M_HEADS):
            dst[0, h] = act[:, h * M_HEAD_DIM:(h + 1) * M_HEAD_DIM].astype(BF16)

    off = 0
    p = _dot(hb, wr_ref[:, off:off + M_WIDTH])
    off += M_WIDTH
    for h in range(M_HEADS):
        for c in range(TM // CHUNK):
            blk = p[c * CHUNK:(c + 1) * CHUNK, h * M_HEAD_DIM:(h + 1) * M_HEAD_DIM]
            mv_ref[0, h, c] = blk.T.astype(BF16)

    lane = lax.broadcasted_iota(jnp.int32, (TM, LANES), 1)
    first16 = (lane % 32) < 16
    ca = ca_ref[...]
    sa = sa_ref[...]

    def rope_a(xs):
        partner = jnp.where(first16, pltpu.roll(xs, LANES - 16, 1), pltpu.roll(xs, 16, 1))
        return xs * ca + partner * sa

    p = _dot(hb, wr_ref[:, off:off + A_WIDTH])
    off += A_WIDTH
    for s in range(A_WIDTH // LANES):
        r = (rope_a(p[:, s * LANES:(s + 1) * LANES]) * (A_HEAD_DIM ** -0.5 * LOG2E)).astype(BF16)
        aq_ref[0, 2 * s] = r[:, 0:A_HEAD_DIM]
        aq_ref[0, 2 * s + 1] = r[:, A_HEAD_DIM:]
    p = _dot(hb, wr_ref[:, off:off + 2 * A_KV_WIDTH])
    off += 2 * A_KV_WIDTH
    r = rope_a(p[:, 0:LANES]).astype(BF16)
    for h in range(A_KV_HEADS):
        ak_ref[0, h] = r[:, h * A_HEAD_DIM:(h + 1) * A_HEAD_DIM]
    pv = p[:, LANES:]
    for h in range(A_KV_HEADS):
        vh = pv if h == 0 else pltpu.roll(pv, A_HEAD_DIM, 1)
        ve = jnp.where(lane < A_HEAD_DIM, vh, 1.0)
        for c in range(TM // A_BLOCK):
            av_ref[0, h, c] = ve[c * A_BLOCK:(c + 1) * A_BLOCK].T.astype(BF16)

    cr = cr_ref[...]
    sr = sr_ref[...]
    for is_k in (False, True):
        p = _dot(hb, wr_ref[:, off:off + R_QK_WIDTH])
        off += R_QK_WIDTH
        for h in range(R_HEADS):
            xs = p[:, h * R_QK_DIM:(h + 1) * R_QK_DIM]
            rot = xs * cr + pltpu.roll(xs, R_QK_DIM // 2, 1) * sr
            if is_k:
                rot = rot * R_QK_DIM ** -0.5
                for c in range(TM // CHUNK):
                    rk_ref[0, h, c] = rot[c * CHUNK:(c + 1) * CHUNK].T.astype(BF16)
            else:
                rq_ref[0, h] = rot.astype(BF16)
    p = _dot(hb, wr_ref[:, off:off + R_V_WIDTH])
    for h in range(R_HEADS):
        rv_ref[0, h] = p[:, h * R_V_DIM:(h + 1) * R_V_DIM].astype(BF16)


def _stream_specs(ctx, xs):
    hb = TM // HALO
    D = xs.shape[-1]
    off = 0 if ctx is None else 1
    last = xs.shape[1] // HALO - 1
    return [
        pl.BlockSpec((BB, TM if off else HALO, D), lambda t, b: (b, 0, 0)),
        pl.BlockSpec((BB, TM, D), lambda t, b: (b, jnp.maximum(t - off, 0), 0)),
        pl.BlockSpec((BB, HALO, D), lambda t, b: (b, jnp.maximum((t - off) * hb - 1, 0), 0)),
        pl.BlockSpec((BB, HALO, D), lambda t, b: (b, jnp.clip((t - off + 1) * hb, 0, last), 0)),
    ]


def _pre_call(ctx, xs, modsel, nw, wqk, cw, wr, wgt, gbt, tabs):
    B, _, D = xs.shape
    T = xs.shape[1] + (0 if ctx is None else ctx.shape[1])
    nt = T // TM
    nch = T // CHUNK

    def tile(t, b):
        return (b, t, 0)

    def head_tile(t, b):
        return (b, 0, t, 0)

    in_specs = _stream_specs(ctx, xs) + [
        pl.BlockSpec((BB, 1, 6, D), lambda t, b: (b, jnp.minimum(t, 1), 0, 0)),
        _const_spec(nw.shape), _const_spec(wqk.shape), _const_spec(cw.shape), _const_spec(wr.shape),
        _const_spec(wgt.shape), _const_spec(gbt.shape),
    ] + [pl.BlockSpec((TM, LANES), lambda t, b: (t, 0)) for _ in range(4)]
    out_shape = [
        jax.ShapeDtypeStruct((B, T, D), BF16),
        jax.ShapeDtypeStruct((B, M_HEADS, T, M_HEAD_DIM), BF16),
        jax.ShapeDtypeStruct((B, M_HEADS, T, M_HEAD_DIM), BF16),
        jax.ShapeDtypeStruct((B, M_HEADS, nch, M_HEAD_DIM, CHUNK), BF16),
        jax.ShapeDtypeStruct((B, 2, nch, 16, CHUNK), F32),
        jax.ShapeDtypeStruct((B, A_HEADS, T, A_HEAD_DIM), BF16),
        jax.ShapeDtypeStruct((B, A_KV_HEADS, T, A_HEAD_DIM), BF16),
        jax.ShapeDtypeStruct((B, A_KV_HEADS, T // A_BLOCK, LANES, A_BLOCK), BF16),
        jax.ShapeDtypeStruct((B, R_HEADS, T, R_QK_DIM), BF16),
        jax.ShapeDtypeStruct((B, R_HEADS, nch, R_QK_DIM, CHUNK), BF16),
        jax.ShapeDtypeStruct((B, R_HEADS, T, R_V_DIM), BF16),
    ]
    out_specs = [
        pl.BlockSpec((BB, TM, D), tile),
        pl.BlockSpec((BB, M_HEADS, TM, M_HEAD_DIM), head_tile),
        pl.BlockSpec((BB, M_HEADS, TM, M_HEAD_DIM), head_tile),
        pl.BlockSpec((BB, M_HEADS, TM // CHUNK, M_HEAD_DIM, CHUNK), lambda t, b: (b, 0, t, 0, 0)),
        pl.BlockSpec((BB, 2, TM // CHUNK, 16, CHUNK), lambda t, b: (b, 0, t, 0, 0)),
        pl.BlockSpec((BB, A_HEADS, TM, A_HEAD_DIM), head_tile),
        pl.BlockSpec((BB, A_KV_HEADS, TM, A_HEAD_DIM), head_tile),
        pl.BlockSpec((BB, A_KV_HEADS, TM // A_BLOCK, LANES, A_BLOCK), lambda t, b: (b, 0, t, 0, 0)),
        pl.BlockSpec((BB, R_HEADS, TM, R_QK_DIM), head_tile),
        pl.BlockSpec((BB, R_HEADS, TM // CHUNK, R_QK_DIM, CHUNK), lambda t, b: (b, 0, t, 0, 0)),
        pl.BlockSpec((BB, R_HEADS, TM, R_V_DIM), head_tile),
    ]
    return pl.pallas_call(
        functools.partial(_pre_kernel, split=ctx is not None),
        grid=(nt, B // BB),
        in_specs=in_specs,
        out_specs=out_specs,
        out_shape=out_shape,
        scratch_shapes=[pltpu.VMEM((BB, TM + 2 * HALO, D), BF16)],
        compiler_params=_params(("parallel", "parallel")),
        name="pre_proj",
    )(xs if ctx is None else ctx, xs, xs, xs, modsel, nw, wqk, cw, wr, wgt, gbt, *tabs)


def _bwd_chunk(i, n_ctx, n_all):
    if isinstance(i, int):
        return n_ctx - 1 - i if i < n_ctx else n_all - 1 + n_ctx - i
    return n_all - 1 + n_ctx - i


def _chunk_start(c):
    return c * CHUNK if isinstance(c, int) else pl.multiple_of(c * CHUNK, CHUNK)


def _chunk_loop(body, n_ctx, n_all, unroll):
    for i in range(n_ctx):
        body(i, 0)
    n_lat = n_all - n_ctx
    while n_lat % unroll:
        unroll //= 2
    lax.fori_loop(n_ctx, n_all, body, 0, unroll=unroll)


def _mlstm_kernel(q_ref, k_ref, vt_ref, gr_ref, o_ref, c_ref, n_ref, m_ref,
                  sb_ref, ms_ref, qp_ref, *, n_ctx):
    n_all = gr_ref.shape[2]
    dh = q_ref.shape[3]
    nrow = 2 * dh
    c_ref[...] = jnp.zeros(c_ref.shape, F32)
    n_ref[...] = jnp.zeros(n_ref.shape, F32)
    m_ref[...] = jnp.zeros(m_ref.shape, F32)

    def scan_unit(c, hh, d):
        u = 2 * hh + d
        c0 = _chunk_start(c)
        k = k_ref[0, hh, pl.ds(c0, CHUNK), :]
        vt = vt_ref[0, hh, c]
        rows = gr_ref[0, 0, c, 8 * hh:8 * hh + 8, :]
        tot = rows[4 + d:5 + d]
        aend = rows[2 * d:2 * d + 1] + tot
        amax = rows[6 + d:7 + d]
        cst = c_ref[u]
        nv = n_ref[u]
        m = m_ref[u]
        sb_ref[hh, c, d * dh:(d + 1) * dh, :] = cst.astype(BF16)
        sb_ref[hh, c, nrow + 16 * d:nrow + 16 * (d + 1), :] = jnp.broadcast_to(nv, (16, dh)).astype(BF16)
        ms_ref[u, c] = jnp.broadcast_to(m, (8, CHUNK))
        m_new = jnp.maximum(tot + m, amax)
        w = jnp.exp(aend - m_new)
        dec = jnp.exp(tot + m - m_new)
        lhs = jnp.concatenate([vt.astype(F32) * w, jnp.broadcast_to(w, (16, CHUNK))], axis=0).astype(BF16)
        upd = _dot(lhs, k)
        c_ref[u] = dec * cst + upd[0:dh]
        n_ref[u] = dec * nv + upd[dh:dh + 1]
        m_ref[u] = m_new

    def scan_step(i, carry):
        cb = _bwd_chunk(i, n_ctx, n_all)
        for hh in range(2):
            scan_unit(i, hh, 0)
            scan_unit(cb, hh, 1)
        return carry

    _chunk_loop(scan_step, n_ctx, n_all, 4)

    row = lax.broadcasted_iota(jnp.int32, (CHUNK, CHUNK), 0)
    col = lax.broadcasted_iota(jnp.int32, (CHUNK, CHUNK), 1)
    tri = (row <= col, row >= col)

    def query_products(c, slot):
        c0 = _chunk_start(c)
        for hh in range(2):
            q = q_ref[0, hh, pl.ds(c0, CHUNK), :]
            k = k_ref[0, hh, pl.ds(c0, CHUNK), :]
            qp_ref[slot, hh] = _dot_nt(jnp.concatenate([k, sb_ref[hh, c]], axis=0), q)

    def outputs(c, slot):
        c0 = _chunk_start(c)
        gcv = jnp.concatenate([gr_ref[0, 0, c], jnp.zeros((CHUNK - 16, CHUNK), F32)], axis=0).T
        for hh in range(2):
            vt = vt_ref[0, hh, c]
            rows = gr_ref[0, 0, c, 8 * hh:8 * hh + 8, :]
            st = qp_ref[slot, hh, 0:CHUNK, :]
            sds, iws, invs = [], [], []
            for d in range(2):
                u = 2 * hh + d
                a0c = gcv[:, 8 * hh + 2 * d:8 * hh + 2 * d + 1]
                b = rows[2 * d + 1:2 * d + 2]
                m = ms_ref[u, c, 0:1, :]
                dlog = jnp.where(tri[d], a0c + b, NEG_INF)
                mj = jnp.maximum(b + m, jnp.max(dlog, axis=0, keepdims=True))
                sd = st * jnp.exp(dlog - mj)
                iw = jnp.exp(b + m - mj)
                nq = qp_ref[slot, hh, CHUNK + nrow + 16 * d:CHUNK + nrow + 16 * d + 1, :]
                den = iw * nq + jnp.sum(sd, axis=0, keepdims=True)
                sds.append(sd.astype(BF16))
                iws.append(iw)
                invs.append(1.0 / jnp.maximum(jnp.abs(den), jnp.exp(-mj)))
            num = _dot(vt, jnp.concatenate(sds, axis=1))
            acc = None
            for d in range(2):
                cq = qp_ref[slot, hh, CHUNK + d * dh:CHUNK + (d + 1) * dh, :]
                ht = (num[:, d * CHUNK:(d + 1) * CHUNK] + iws[d] * cq) * invs[d]
                acc = ht if acc is None else acc + ht
            o_ref[0, hh, pl.ds(c0, CHUNK), :] = acc.T

    query_products(0, 0)

    def out_pair(i, carry):
        c = 2 * i
        query_products(c + 1, 1)
        outputs(c, 0)
        query_products(jnp.minimum(c + 2, n_all - 1), 0)
        outputs(c + 1, 1)
        return carry

    n_pairs = n_all // 2
    unroll = 4
    first = n_pairs % unroll
    for i in range(first):
        out_pair(i, 0)
    lax.fori_loop(first, n_pairs, out_pair, 0, unroll=unroll)


def _mlstm_call(mq, mk, mvt, gr, n_ctx):
    B, H, T, dh = mq.shape
    nch = T // CHUNK
    qspec = pl.BlockSpec((1, 2, T, dh), lambda b, p: (b, p, 0, 0))
    return pl.pallas_call(
        functools.partial(_mlstm_kernel, n_ctx=n_ctx),
        grid=(B, H // 2),
        in_specs=[qspec, qspec,
                  pl.BlockSpec((1, 2, nch, dh, CHUNK), lambda b, p: (b, p, 0, 0, 0)),
                  pl.BlockSpec((1, 1, nch, 16, CHUNK), lambda b, p: (b, p, 0, 0, 0))],
        out_specs=pl.BlockSpec((1, 2, T, dh), lambda b, p: (b, p, 0, 0)),
        out_shape=jax.ShapeDtypeStruct((B, H, T, dh), F32),
        scratch_shapes=[pltpu.VMEM((4, dh, dh), F32), pltpu.VMEM((4, 1, dh), F32),
                        pltpu.VMEM((4, 1, CHUNK), F32),
                        pltpu.VMEM((2, nch, 2 * dh + 32, dh), BF16),
                        pltpu.VMEM((4, nch, 8, CHUNK), F32),
                        pltpu.VMEM((2, 2, CHUNK + 2 * dh + 32, CHUNK), F32)],
        compiler_params=_params(("parallel", "parallel")),
        name="mlstm",
    )(mq, mk, mvt, gr)


def _ret_kernel(q_ref, kt_ref, v_ref, lg_ref, o_ref, st_ref, sts_ref, dec_ref, ib_ref, wb_ref, *, n_ctx):
    n_all = kt_ref.shape[2]
    dk = q_ref.shape[3]
    dv = v_ref.shape[3]
    st_ref[...] = jnp.zeros(st_ref.shape, F32)
    row = lax.broadcasted_iota(jnp.int32, (CHUNK, CHUNK), 0).astype(F32)
    col = lax.broadcasted_iota(jnp.int32, (CHUNK, CHUNK), 1).astype(F32)
    cds = []
    dec = None
    for d in range(2):
        lg = _log_sigmoid(lg_ref[0, d])[0:1]
        lgk = lg[:, 0:CHUNK]
        diff = row - col if d == 0 else col - row
        dd = jnp.where(diff >= 0, jnp.exp(lgk * jnp.maximum(diff, 0.0)), 0.0)
        dec = dd if dec is None else dec + dd
        pin = row if d == 0 else CHUNK - 1.0 - row
        ib_ref[d] = jnp.exp(lg * (jnp.concatenate([pin, pin], axis=1) + 1.0))
        pkey = col if d == 0 else CHUNK - 1.0 - col
        wb_ref[d] = jnp.exp(lgk * (CHUNK - 1.0 - pkey))
        cds.append(jnp.exp(lg * CHUNK))
    dec_ref[...] = dec

    def scan_step(i, carry):
        cb = _bwd_chunk(i, n_ctx, n_all)
        for d, c in ((0, i), (1, cb)):
            c0 = _chunk_start(c)
            st = st_ref[d]
            sts_ref[c, :, d * dv:(d + 1) * dv] = st.astype(BF16)
            kw = (kt_ref[0, 0, c].astype(F32) * wb_ref[d]).astype(BF16)
            st_ref[d] = cds[d] * st + _dot(kw, v_ref[0, 0, pl.ds(c0, CHUNK), :])
        return carry

    _chunk_loop(scan_step, n_ctx, n_all, 8)

    def out_step(c, carry):
        c0 = _chunk_start(c)
        q = q_ref[0, 0, pl.ds(c0, CHUNK), :]
        s = (_dot(q, kt_ref[0, 0, c]) * dec_ref[...]).astype(BF16)
        qst = _dot(q, sts_ref[c])
        o_ref[0, 0, pl.ds(c0, CHUNK), :] = (_dot(s, v_ref[0, 0, pl.ds(c0, CHUNK), :])
                                            + ib_ref[0] * qst[:, 0:dv] + ib_ref[1] * qst[:, dv:])
        return carry

    _chunk_loop(out_step, n_ctx, n_all, 8)


def _ret_call(rq, rkt, rv, lgb, n_ctx):
    B, H, T, dk = rq.shape
    dv = rv.shape[-1]
    nch = T // CHUNK
    return pl.pallas_call(
        functools.partial(_ret_kernel, n_ctx=n_ctx),
        grid=(B, H),
        in_specs=[pl.BlockSpec((1, 1, T, dk), lambda b, h: (b, h, 0, 0)),
                  pl.BlockSpec((1, 1, nch, dk, CHUNK), lambda b, h: (b, h, 0, 0, 0)),
                  pl.BlockSpec((1, 1, T, dv), lambda b, h: (b, h, 0, 0)),
                  pl.BlockSpec((1, 2, 8, dv), lambda b, h: (h, 0, 0, 0))],
        out_specs=pl.BlockSpec((1, 1, T, dv), lambda b, h: (b, h, 0, 0)),
        out_shape=jax.ShapeDtypeStruct((B, H, T, dv), F32),
        scratch_shapes=[pltpu.VMEM((2, dk, dv), F32), pltpu.VMEM((nch, dk, 2 * dv), BF16),
                        pltpu.VMEM((CHUNK, CHUNK), F32), pltpu.VMEM((2, CHUNK, dv), F32),
                        pltpu.VMEM((2, dk, CHUNK), F32)],
        compiler_params=_params(("parallel", "parallel")),
        name="retention",
    )(rq, rkt, rv, lgb)


def _attn_kernel(q_ref, k_ref, vt_ref, sink_ref, o_ref, bias_ref, s_ref, *, n_ctx_tok, need_ctx):
    T = q_ref.shape[2]
    L = n_ctx_tok
    S = T - L
    nb = S // A_BLOCK
    nspan = 3
    span = nspan * A_BLOCK
    ncb = L // A_BLOCK
    cols = A_GROUP * A_BLOCK
    kc = k_ref[0, 0, 0:L, :]
    vtc = jnp.concatenate([vt_ref[0, 0, j] for j in range(ncb)], axis=1)
    sink = sink_ref[0]

    kpos = lax.broadcasted_iota(jnp.int32, (span, A_BLOCK), 0)
    qpos = lax.broadcasted_iota(jnp.int32, (span, A_BLOCK), 1)
    for delta in range(nspan):
        bias_ref[delta] = jnp.where(jnp.abs(qpos + delta * A_BLOCK - kpos) <= A_WINDOW, 0.0, NEG_INF)

    def finish(blk, acc, m):
        den = acc[A_HEAD_DIM:A_HEAD_DIM + 1] + jnp.exp2(sink - m)
        ot = (acc[0:A_HEAD_DIM] * (1.0 / den)).astype(BF16)
        for g in range(A_GROUP):
            o_ref[0, blk, g * A_HEAD_DIM:(g + 1) * A_HEAD_DIM, :] = ot[:, g * A_BLOCK:(g + 1) * A_BLOCK]

    for cbk in range(ncb):
        if need_ctx:
            qs = q_ref[0, :, cbk * A_BLOCK:(cbk + 1) * A_BLOCK, :].reshape(cols, A_HEAD_DIM)
            s = _dot_nt(kc, qs)
            m = jnp.maximum(jnp.max(s, axis=0, keepdims=True), sink)
            finish(cbk, _dot(vtc, jnp.exp2(s - m).astype(BF16)), m)
        else:
            o_ref[0, cbk] = jnp.zeros(o_ref.shape[2:], BF16)

    def first_key_block(n):
        return jnp.clip(n - 1, 0, nb - nspan)

    def scores(n, slot):
        q0 = pl.multiple_of(L + n * A_BLOCK, A_BLOCK)
        kb = first_key_block(n)
        k0 = pl.multiple_of(L + kb * A_BLOCK, A_BLOCK)
        qs = q_ref[0, :, pl.ds(q0, A_BLOCK), :].reshape(cols, A_HEAD_DIM)
        bias = bias_ref[n - kb]
        s_ref[slot, 0:span, :] = (_dot_nt(k_ref[0, 0, pl.ds(k0, span), :], qs)
                                  + jnp.concatenate([bias] * A_GROUP, axis=1))
        s_ref[slot, span:, :] = _dot_nt(kc, qs)

    def softmax_pv(n, slot):
        kb = first_key_block(n)
        s = s_ref[slot]
        m = jnp.maximum(jnp.max(s, axis=0, keepdims=True), sink)
        p = jnp.exp2(s - m).astype(BF16)
        vt = jnp.concatenate([vt_ref[0, 0, ncb + kb + j] for j in range(nspan)] + [vtc], axis=1)
        finish(ncb + n, _dot(vt, p), m)

    scores(0, 0)

    def block_pair(i, carry):
        n = 2 * i
        scores(n + 1, 1)
        softmax_pv(n, 0)
        scores(jnp.minimum(n + 2, nb - 1), 0)
        softmax_pv(n + 1, 1)
        return carry

    n_pairs = nb // 2
    first = n_pairs % 2
    if first:
        block_pair(0, 0)
    lax.fori_loop(first, n_pairs, block_pair, 0, unroll=2)


def _attn_call(aq, ak, avt, sink_rows, n_ctx_tok, need_ctx):
    B, _, T, hd = aq.shape
    nblk = T // A_BLOCK
    cols = A_GROUP * A_BLOCK
    return pl.pallas_call(
        functools.partial(_attn_kernel, n_ctx_tok=n_ctx_tok, need_ctx=need_ctx),
        grid=(B, A_KV_HEADS),
        in_specs=[pl.BlockSpec((1, A_GROUP, T, hd), lambda b, h: (b, h, 0, 0)),
                  pl.BlockSpec((1, 1, T, hd), lambda b, h: (b, h, 0, 0)),
                  pl.BlockSpec((1, 1, nblk, LANES, A_BLOCK), lambda b, h: (b, h, 0, 0, 0)),
                  pl.BlockSpec((1, 1, cols), lambda b, h: (h, 0, 0))],
        out_specs=pl.BlockSpec((1, nblk, A_GROUP * hd, A_BLOCK), lambda b, h: (b, 0, h, 0)),
        out_shape=jax.ShapeDtypeStruct((B, nblk, A_WIDTH, A_BLOCK), BF16),
        scratch_shapes=[pltpu.VMEM((3, 3 * A_BLOCK, A_BLOCK), F32),
                        pltpu.VMEM((2, 3 * A_BLOCK + n_ctx_tok, cols), F32)],
        compiler_params=_params(("parallel", "parallel")),
        name="window_attn",
    )(aq, ak, avt, sink_rows)


def _head_norm(y):
    mu = jnp.mean(y, axis=-1, keepdims=True)
    yc = y - mu
    var = jnp.mean(yc * yc, axis=-1, keepdims=True)
    return yc * lax.rsqrt(var + NORM_EPS)


def _post_kernel(c_ref, x_ref, hb_ref, hm_ref, ya_ref, hr_ref, mod_ref, wg_ref, mnw_ref, rnw_ref,
                 wbm_ref, wba_ref, wbr_ref, wo_ref, o_ref, *, split):
    for i in range(x_ref.shape[0]):
        hb = hb_ref[i]
        off = 0
        hm = jnp.concatenate([_head_norm(hm_ref[i, h]) for h in range(M_HEADS)], axis=1) * mnw_ref[...]
        ym = (_sigmoid(_dot(hb, wg_ref[:, off:off + M_WIDTH])) * hm).astype(BF16)
        off += M_WIDTH
        hr = jnp.concatenate([_head_norm(hr_ref[i, h]) for h in range(R_HEADS)], axis=1) * rnw_ref[...]
        rg = _dot(hb, wg_ref[:, off:off + R_V_WIDTH])
        yr = (rg * _sigmoid(rg) * hr).astype(BF16)
        off += R_V_WIDTH
        z = _sigmoid(_dot(hb, wg_ref[:, off:off + D_MODEL])) * _dot(ym, wbm_ref[...])
        off += D_MODEL
        pa = jnp.concatenate([_dot_tn(ya_ref[i, c], wba_ref[...]) for c in range(TM // A_BLOCK)], axis=0)
        z = z + _sigmoid(_dot(hb, wg_ref[:, off:off + D_MODEL])) * pa
        off += D_MODEL
        z = z + _sigmoid(_dot(hb, wg_ref[:, off:off + D_MODEL])) * _dot(yr, wbr_ref[...])
        y = _dot(z.astype(BF16), wo_ref[...])
        one = pl.ds(i, 1)
        xres = _tile_rows(c_ref.at[one] if split else None, x_ref.at[one])
        o_ref[i] = xres + mod_ref[i, 0, 2:3, :] * y


def _post_call(ctx, xs, hb, hm, ya, hr, modsel, wg, mnw, rnw, wbm, wba, wbr, wo, t_off):
    B, T, D = hb.shape
    nt = T // TM - t_off
    split = ctx is not None
    assert not (split and t_off)

    def tile(t, b):
        return (b, t + t_off, 0)

    def head_tile(t, b):
        return (b, 0, t + t_off, 0)

    return pl.pallas_call(
        functools.partial(_post_kernel, split=split),
        grid=(nt, B // BB),
        in_specs=[pl.BlockSpec((BB, TM if split else HALO, D), lambda t, b: (b, 0, 0)),
                  pl.BlockSpec((BB, TM, D), (lambda t, b: (b, jnp.maximum(t - 1, 0), 0)) if split else tile),
                  pl.BlockSpec((BB, TM, D), tile),
                  pl.BlockSpec((BB, M_HEADS, TM, M_HEAD_DIM), head_tile),
                  pl.BlockSpec((BB, TM // A_BLOCK, A_WIDTH, A_BLOCK), lambda t, b: (b, t + t_off, 0, 0)),
                  pl.BlockSpec((BB, R_HEADS, TM, R_V_DIM), head_tile),
                  pl.BlockSpec((BB, 1, 6, D), lambda t, b: (b, jnp.minimum(t + t_off, 1), 0, 0)),
                  _const_spec(wg.shape), _const_spec(mnw.shape), _const_spec(rnw.shape),
                  _const_spec(wbm.shape), _const_spec(wba.shape), _const_spec(wbr.shape),
                  _const_spec(wo.shape)],
        out_specs=pl.BlockSpec((BB, TM, D), lambda t, b: (b, t, 0)),
        out_shape=jax.ShapeDtypeStruct((B, nt * TM, D), F32),
        compiler_params=_params(("parallel", "parallel")),
        name="post_merge",
    )(ctx if split else xs, xs, hb, hm, ya, hr, modsel, wg, mnw, rnw, wbm, wba, wbr, wo)


def _ffn_kernel(x_ref, xp_ref, xn_ref, mod_ref, nw_ref, wa_ref, wb_ref, cw_ref, cb_ref, wd_ref, fw_ref,
                o_ref, hext_ref, u_ref, *, has_ctx, final):
    t = pl.program_id(0)
    nt = pl.num_programs(0)
    first_lat = 1 if has_ctx else 0
    nw = nw_ref[...]
    ext = TM + 2 * HALO
    for i in range(x_ref.shape[0]):
        sh = mod_ref[i, 0, 3:4, :]
        sc = mod_ref[i, 0, 4:5, :]
        g2 = mod_ref[i, 0, 5:6, :]
        x = x_ref[i]
        hp = _norm_mod(xp_ref[i], nw, sc, sh)
        hn = _norm_mod(xn_ref[i], nw, sc, sh)
        hp = jnp.where(t <= first_lat, 0.0, hp)
        hn = jnp.where((t == first_lat - 1) | (t == nt - 1), 0.0, hn)
        hext_ref[i, 0:TM, :] = _norm_mod(x, nw, sc, sh).astype(BF16)
        hext_ref[i, TM:, :] = jnp.concatenate([hn, hp], axis=0).astype(BF16)
        for j in range(FFN_DIM // FFN_NC):
            cs = slice(j * FFN_NC, (j + 1) * FFN_NC)
            a = _dot(hext_ref[i], wa_ref[:, cs])
            b = _dot(hext_ref[i, 0:TM, :], wb_ref[:, cs])
            cw = cw_ref[:, cs]
            prev = pltpu.roll(a, 1, 0)[0:TM]
            nxt = pltpu.roll(a, ext - 1, 0)[0:TM]
            conv = prev * cw[0:1] + a[0:TM] * cw[1:2] + nxt * cw[2:3] + cb_ref[:, cs]
            u_ref[i, :, cs] = (conv * _sigmoid(conv) * b).astype(BF16)
        y = x + g2 * _dot(u_ref[i], wd_ref[...])
        if final:
            ms = jnp.mean(y * y, axis=-1, keepdims=True)
            y = y * lax.rsqrt(ms + NORM_EPS) * fw_ref[...]
        o_ref[i] = y


def _ffn_call(xs, modsel, nw, wa, wb, cw, cb, wd, fw, has_ctx, final):
    B, T, D = xs.shape
    nt = T // TM
    hb16 = TM // HALO
    first_lat = 1 if has_ctx else 0
    tile = pl.BlockSpec((BB, TM, D), lambda t, b: (b, t, 0))
    return pl.pallas_call(
        functools.partial(_ffn_kernel, has_ctx=has_ctx, final=final),
        grid=(nt, B // BB),
        in_specs=[tile,
                  pl.BlockSpec((BB, HALO, D), lambda t, b: (b, jnp.maximum(t * hb16 - 1, 0), 0)),
                  pl.BlockSpec((BB, HALO, D), lambda t, b: (b, jnp.minimum((t + 1) * hb16, T // HALO - 1), 0)),
                  pl.BlockSpec((BB, 1, 6, D), lambda t, b: (b, jnp.minimum(t + 1 - first_lat, 1), 0, 0)),
                  _const_spec(nw.shape), _const_spec(wa.shape), _const_spec(wb.shape),
                  _const_spec(cw.shape), _const_spec(cb.shape), _const_spec(wd.shape),
                  _const_spec(fw.shape)],
        out_specs=tile,
        out_shape=jax.ShapeDtypeStruct((B, T, D), F32),
        scratch_shapes=[pltpu.VMEM((BB, TM + 2 * HALO, D), BF16), pltpu.VMEM((BB, TM, FFN_DIM), BF16)],
        compiler_params=_params(("parallel", "parallel")),
        name="conv_ffn",
    )(xs, xs, xs, modsel, nw, wa, wb, cw, cb, wd, fw)


def _split_cols(w):
    out = []
    acc = 0
    for s in IN_SPLITS:
        out.append(w[:, acc:acc + s])
        acc += s
    return out


def _rope_tables(L, S):
    T = L + S
    j = jnp.arange(LANES)
    jj = j % 32
    inv_a = ROPE_BASE ** (-(jj % 16).astype(F32) / 16.0)
    t = jnp.arange(S)
    pos = jnp.where(((j % A_HEAD_DIM) // 32 == 0)[None, :], (t // GRID_W)[:, None], (t % GRID_W)[:, None])
    ang = pos.astype(F32) * inv_a[None, :]
    sign = jnp.where(jj < 16, -1.0, 1.0)[None, :]
    ca = jnp.concatenate([jnp.ones((L, LANES), F32), jnp.cos(ang)], axis=0)
    sa = jnp.concatenate([jnp.zeros((L, LANES), F32), jnp.sin(ang) * sign], axis=0)
    half = R_QK_DIM // 2
    inv_r = ROPE_BASE ** (-(j % half).astype(F32) / half)
    angr = jnp.arange(T).astype(F32)[:, None] * inv_r[None, :]
    signr = jnp.where(j < half, -1.0, 1.0)[None, :]
    return ca, sa, jnp.cos(angr), jnp.sin(angr) * signr


def kernel(x, c, ctx, c_ctx, mod_w, mod_b, norm1_w, norm2_w, w_in, m_gate_bias, m_conv_w, m_norm_w,
           a_sink, ret_logit, ret_norm_w, w_br_m, w_br_a, w_br_r, w_out, ffn_up, ffn_conv_w, ffn_conv_b,
           ffn_down, final_norm_w):
    B, S, D = x.shape
    L = ctx.shape[1]
    depth = mod_w.shape[0]
    assert D == D_MODEL and L == TM and S % TM == 0 and S >= 3 * A_BLOCK and B % BB == 0
    T = L + S
    n_ctx = L // CHUNK

    rows = -(-(B + 1) // 8) * 8
    cpad = jnp.zeros((rows, D), F32).at[:B].set(c).at[B].set(c_ctx)
    mods = _modulation(cpad, mod_w, mod_b)
    tabs = _rope_tables(L, S)
    stream = (ctx, x) if depth > 1 else (None, jnp.concatenate([ctx, x], axis=1))

    gperm = jnp.arange(4 * M_HEADS).reshape(4, M_HEADS).T.reshape(-1)

    for l in range(depth):
        last = l == depth - 1
        lat = mods[l, :B].reshape(B, 6, D)
        cm = jnp.broadcast_to(mods[l, B].reshape(1, 6, D), (B, 6, D))
        modsel = jnp.stack([cm, lat], axis=1)

        (w_mq, w_mk, w_mv, w_mo, w_mg, w_aq, w_ak, w_av,
         w_rq, w_rk, w_rv, w_rg, w_gm, w_ga, w_gr) = _split_cols(w_in[l])
        wqk = jnp.concatenate([w_mq, w_mk], axis=1).astype(BF16)
        wr = jnp.concatenate([w_mv, w_aq, w_ak, w_av, w_rq, w_rk, w_rv], axis=1).astype(BF16)
        wgt = w_mg[:, gperm].T.astype(BF16)
        gbt = jnp.broadcast_to(m_gate_bias[l][gperm].reshape(-1, 1), (4 * M_HEADS, LANES))

        (hb, mq, mk, mv, gr, aq, ak, av, rq, rk, rv) = _pre_call(
            *stream, modsel, norm1_w[l].reshape(1, D), wqk, m_conv_w[l], wr, wgt, gbt, tabs)

        hm = _mlstm_call(mq, mk, mv, gr, n_ctx)
        sink_rows = jnp.repeat(a_sink[l].reshape(A_KV_HEADS, A_GROUP), A_BLOCK, axis=1)[:, None, :]
        ya = _attn_call(aq, ak, av, sink_rows.astype(F32) * LOG2E, L, not last)
        lgb = jnp.broadcast_to(ret_logit[l].T[:, :, None, None], (R_HEADS, 2, 8, R_V_DIM)).astype(F32)
        hr = _ret_call(rq, rk, rv, lgb, n_ctx)

        t_off = 1 if last else 0
        wgates = jnp.concatenate([w_mo, w_rg, w_gm, w_ga, w_gr], axis=1).astype(BF16)
        x1 = _post_call(*stream, hb, hm, ya, hr, modsel, wgates, m_norm_w[l].reshape(1, -1),
                        ret_norm_w[l].reshape(1, -1), w_br_m[l].astype(BF16), w_br_a[l].astype(BF16),
                        w_br_r[l].astype(BF16), w_out[l].astype(BF16), t_off)
        xc = _ffn_call(x1, modsel, norm2_w[l].reshape(1, D), ffn_up[l][:, :FFN_DIM].astype(BF16),
                       ffn_up[l][:, FFN_DIM:].astype(BF16), ffn_conv_w[l], ffn_conv_b[l].reshape(1, -1),
                       ffn_down[l].astype(BF16), final_norm_w.reshape(1, D), not last, last)
        stream = (None, xc)
    return xc
```

```python
import functools

import jax
import jax.numpy as jnp
from jax import lax
from jax.experimental import pallas as pl
from jax.experimental.pallas import tpu as pltpu

F32 = jnp.float32
BF16 = jnp.bfloat16

D_MODEL = 1024
GRID_W = 64
NORM_EPS = 1e-6
ROPE_BASE = 10000.0
NEG_INF = -1e30
LOG2E = 1.4426950408889634

M_HEADS = 4
M_HEAD_DIM = 128
M_WIDTH = M_HEADS * M_HEAD_DIM
A_HEADS = 8
A_KV_HEADS = 2
A_GROUP = A_HEADS // A_KV_HEADS
A_HEAD_DIM = 64
A_WIDTH = A_HEADS * A_HEAD_DIM
A_KV_WIDTH = A_KV_HEADS * A_HEAD_DIM
A_WINDOW = 128
A_BLOCK = 128
R_HEADS = 4
R_QK_DIM = 128
R_V_DIM = 256
R_QK_WIDTH = R_HEADS * R_QK_DIM
R_V_WIDTH = R_HEADS * R_V_DIM
FFN_DIM = 2816

IN_SPLITS = (M_WIDTH, M_WIDTH, M_WIDTH, M_WIDTH, 4 * M_HEADS,
             A_WIDTH, A_KV_WIDTH, A_KV_WIDTH,
             R_QK_WIDTH, R_QK_WIDTH, R_V_WIDTH, R_V_WIDTH,
             D_MODEL, D_MODEL, D_MODEL)

TM = 256
BB = 2
HALO = 8
CHUNK = 128
RCHUNK = 256
FFN_NC = 256
LANES = 128
VMEM_LIMIT = 56 * 1024 * 1024


def _sigmoid(x):
    return 1.0 / (1.0 + jnp.exp(-x))


def _log_sigmoid(x):
    return jnp.minimum(x, 0.0) - jnp.log1p(jnp.exp(-jnp.abs(x)))


def _dot(a, b):
    return jnp.dot(a, b, preferred_element_type=F32)


def _dot_nt(a, b):
    return lax.dot_general(a, b, (((1,), (1,)), ((), ())), preferred_element_type=F32)


def _dot_tn(a, b):
    return lax.dot_general(a, b, (((0,), (0,)), ((), ())), preferred_element_type=F32)


def _const_spec(shape):
    nd = len(shape)
    return pl.BlockSpec(shape, lambda *_: (0,) * nd, pipeline_mode=pl.Buffered(1))


def _params(sem):
    return pltpu.CompilerParams(dimension_semantics=sem, vmem_limit_bytes=VMEM_LIMIT)


def _mod_kernel(c_ref, w_ref, b_ref, o_ref):
    c = c_ref[...]
    s = (c * _sigmoid(c)).astype(BF16)
    o_ref[0] = _dot(s, w_ref[0].astype(BF16)) + b_ref[0]


def _modulation(cpad, mod_w, mod_b):
    depth, d, n = mod_w.shape
    rows = cpad.shape[0]
    bn = 1024
    return pl.pallas_call(
        _mod_kernel,
        grid=(depth, n // bn),
        in_specs=[pl.BlockSpec((rows, d), lambda l, j: (0, 0)),
                  pl.BlockSpec((1, d, bn), lambda l, j: (l, 0, j)),
                  pl.BlockSpec((1, 1, bn), lambda l, j: (l, 0, j))],
        out_specs=pl.BlockSpec((1, rows, bn), lambda l, j: (l, 0, j)),
        out_shape=jax.ShapeDtypeStruct((depth, rows, n), F32),
        compiler_params=_params(("parallel", "parallel")),
        name="modulation",
    )(cpad, mod_w, mod_b.reshape(depth, 1, n))


def _norm_mod(xv, nw, sc, sh):
    ms = jnp.mean(xv * xv, axis=-1, keepdims=True)
    return (xv * lax.rsqrt(ms + NORM_EPS) * nw) * (1.0 + sc) + sh


def _seg_cumsum(x, axis, reverse):
    n = x.shape[axis]
    idx = lax.broadcasted_iota(jnp.int32, x.shape, axis)
    s = 1
    while s < n:
        if reverse:
            shifted = pltpu.roll(x, n - s, axis)
            x = x + jnp.where(idx < n - s, shifted, 0.0)
        else:
            shifted = pltpu.roll(x, s, axis)
            x = x + jnp.where(idx >= s, shifted, 0.0)
        s *= 2
    return x


def _gate_transform(raw, gate_axis):
    tok_axis = 1 - gate_axis
    k = lax.broadcasted_iota(jnp.int32, raw.shape, gate_axis) % 4
    lf = _log_sigmoid(raw)
    cum_f = _seg_cumsum(jnp.where(k == 1, lf, 0.0), tok_axis, False)
    cum_b = _seg_cumsum(jnp.where(k == 3, lf, 0.0), tok_axis, True)
    bsel = jnp.where(k == 1, cum_f, jnp.where(k == 3, cum_b, 0.0))
    n = raw.shape[gate_axis]
    bnext = pltpu.roll(bsel, n - 1, gate_axis)
    nt = raw.shape[tok_axis]
    tot_f = lax.slice_in_dim(cum_f, nt - 1, nt, axis=tok_axis)
    tot_b = lax.slice_in_dim(cum_b, 0, 1, axis=tok_axis)
    total = jnp.broadcast_to(tot_f + tot_b, raw.shape)
    return jnp.where(k % 2 == 1, bsel, raw - bnext), total


def _pre_kernel(c_ref, x_ref, xp_ref, xn_ref, mod_ref, nw_ref, wqk_ref, cw_ref, wr_ref, wgt_ref,
                gbt_ref, ca_ref, sa_ref, cr_ref, sr_ref, *out_and_scratch, split):
    for i in range(x_ref.shape[0]):
        one = pl.ds(i, 1)
        _pre_tile(c_ref.at[one] if split else None,
                  x_ref.at[one], xp_ref.at[one], xn_ref.at[one], mod_ref.at[one], nw_ref, wqk_ref, cw_ref,
                  wr_ref, wgt_ref, gbt_ref, ca_ref, sa_ref, cr_ref, sr_ref,
                  *[r.at[one] for r in out_and_scratch[:-1]], out_and_scratch[-1].at[i])


def _tile_rows(c_ref, x_ref):
    if c_ref is None:
        return x_ref[0]
    return jnp.where(pl.program_id(0) == 0, c_ref[0], x_ref[0])


def _pre_tile(c_ref, x_ref, xp_ref, xn_ref, mod_ref, nw_ref, wqk_ref, cw_ref, wr_ref, wgt_ref,
              gbt_ref, ca_ref, sa_ref, cr_ref, sr_ref,
              hb_ref, mq_ref, mk_ref, mv_ref, gr_ref, aq_ref, ak_ref, av_ref,
              rq_ref, rk_ref, rv_ref, hext_ref):
    t = pl.program_id(0)
    nt = pl.num_programs(0)
    sh = mod_ref[0, 0, 0:1, :]
    sc = mod_ref[0, 0, 1:2, :]
    nw = nw_ref[...]
    hb = _norm_mod(_tile_rows(c_ref, x_ref), nw, sc, sh).astype(BF16)
    hb_ref[0] = hb

    grow = _dot_nt(wgt_ref[...], hb) + gbt_ref[:, 0:1]
    for c in range(TM // CHUNK):
        gr, tot = _gate_transform(grow[:, c * CHUNK:(c + 1) * CHUNK], 0)
        aend = gr + pltpu.roll(tot, 4 * M_HEADS - 1, 0)
        amax = jnp.broadcast_to(jnp.max(aend, axis=1, keepdims=True), aend.shape)
        for h in range(M_HEADS):
            hp_, o0 = h // 2, 8 * (h % 2)
            gr_ref[0, hp_, c, o0:o0 + 4, :] = gr[4 * h:4 * h + 4]
            for d in range(2):
                gr_ref[0, hp_, c, o0 + 4 + d:o0 + 5 + d, :] = tot[4 * h + 2 * d + 1:4 * h + 2 * d + 2]
                gr_ref[0, hp_, c, o0 + 6 + d:o0 + 7 + d, :] = amax[4 * h + 2 * d:4 * h + 2 * d + 1]

    hp = _norm_mod(xp_ref[0], nw, sc, sh)
    hn = _norm_mod(xn_ref[0], nw, sc, sh)
    hp = jnp.where(t <= 1, 0.0, hp)
    hn = jnp.where((t == 0) | (t == nt - 1), 0.0, hn)
    hext_ref[0:TM, :] = hb
    hext_ref[TM:, :] = jnp.concatenate([hn, hp], axis=0).astype(BF16)
    hext = hext_ref[...]
    ext = TM + 2 * HALO

    for j, (dst, scale) in enumerate(((mq_ref, 1.0), (mk_ref, M_HEAD_DIM ** -0.5))):
        p = _dot(hext, wqk_ref[:, j * M_WIDTH:(j + 1) * M_WIDTH])
        cw = cw_ref[:, j * M_WIDTH:(j + 1) * M_WIDTH]
        prev = pltpu.roll(p, 1, 0)[0:TM]
        nxt = pltpu.roll(p, ext - 1, 0)[0:TM]
        conv = prev * cw[0:1] + p[0:TM] * cw[1:2] + nxt * cw[2:3]
        act = conv * _sigmoid(conv) * scale
        for h in range(M_HEADS):
            dst[0, h] = act[:, h * M_HEAD_DIM:(h + 1) * M_HEAD_DIM].astype(BF16)

    off = 0
    p = _dot(hb, wr_ref[:, off:off + M_WIDTH])
    off += M_WIDTH
    for h in range(M_HEADS):
        for c in range(TM // CHUNK):
            blk = p[c * CHUNK:(c + 1) * CHUNK, h * M_HEAD_DIM:(h + 1) * M_HEAD_DIM]
            mv_ref[0, h, c] = blk.T.astype(BF16)

    lane = lax.broadcasted_iota(jnp.int32, (TM, LANES), 1)
    first16 = (lane % 32) < 16
    ca = ca_ref[...]
    sa = sa_ref[...]

    def rope_a(xs):
        partner = jnp.where(first16, pltpu.roll(xs, LANES - 16, 1), pltpu.roll(xs, 16, 1))
        return xs * ca + partner * sa

    p = _dot(hb, wr_ref[:, off:off + A_WIDTH])
    off += A_WIDTH
    for s in range(A_WIDTH // LANES):
        r = (rope_a(p[:, s * LANES:(s + 1) * LANES]) * (A_HEAD_DIM ** -0.5 * LOG2E)).astype(BF16)
        aq_ref[0, 2 * s] = r[:, 0:A_HEAD_DIM]
        aq_ref[0, 2 * s + 1] = r[:, A_HEAD_DIM:]
    p = _dot(hb, wr_ref[:, off:off + 2 * A_KV_WIDTH])
    off += 2 * A_KV_WIDTH
    r = rope_a(p[:, 0:LANES]).astype(BF16)
    for h in range(A_KV_HEADS):
        ak_ref[0, h] = r[:, h * A_HEAD_DIM:(h + 1) * A_HEAD_DIM]
    pv = p[:, LANES:]
    for h in range(A_KV_HEADS):
        vh = pv if h == 0 else pltpu.roll(pv, A_HEAD_DIM, 1)
        ve = jnp.where(lane < A_HEAD_DIM, vh, 1.0)
        for c in range(TM // A_BLOCK):
            av_ref[0, h, c] = ve[c * A_BLOCK:(c + 1) * A_BLOCK].T.astype(BF16)

    cr = cr_ref[...]
    sr = sr_ref[...]
    for is_k in (False, True):
        p = _dot(hb, wr_ref[:, off:off + R_QK_WIDTH])
        off += R_QK_WIDTH
        for h in range(R_HEADS):
            xs = p[:, h * R_QK_DIM:(h + 1) * R_QK_DIM]
            rot = xs * cr + pltpu.roll(xs, R_QK_DIM // 2, 1) * sr
            if is_k:
                rot = rot * R_QK_DIM ** -0.5
                for c in range(TM // CHUNK):
                    rk_ref[0, h, c] = rot[c * CHUNK:(c + 1) * CHUNK].T.astype(BF16)
            else:
                rq_ref[0, h] = rot.astype(BF16)
    p = _dot(hb, wr_ref[:, off:off + R_V_WIDTH])
    for h in range(R_HEADS):
        rv_ref[0, h] = p[:, h * R_V_DIM:(h + 1) * R_V_DIM].astype(BF16)


def _stream_specs(ctx, xs):
    hb = TM // HALO
    D = xs.shape[-1]
    off = 0 if ctx is None else 1
    last = xs.shape[1] // HALO - 1
    return [
        pl.BlockSpec((BB, TM if off else HALO, D), lambda t, b: (b, 0, 0)),
        pl.BlockSpec((BB, TM, D), lambda t, b: (b, jnp.maximum(t - off, 0), 0)),
        pl.BlockSpec((BB, HALO, D), lambda t, b: (b, jnp.maximum((t - off) * hb - 1, 0), 0)),
        pl.BlockSpec((BB, HALO, D), lambda t, b: (b, jnp.clip((t - off + 1) * hb, 0, last), 0)),
    ]


def _pre_call(ctx, xs, modsel, nw, wqk, cw, wr, wgt, gbt, tabs):
    B, _, D = xs.shape
    T = xs.shape[1] + (0 if ctx is None else ctx.shape[1])
    nt = T // TM
    nch = T // CHUNK

    def tile(t, b):
        return (b, t, 0)

    def head_tile(t, b):
        return (b, 0, t, 0)

    in_specs = _stream_specs(ctx, xs) + [
        pl.BlockSpec((BB, 1, 6, D), lambda t, b: (b, jnp.minimum(t, 1), 0, 0)),
        _const_spec(nw.shape), _const_spec(wqk.shape), _const_spec(cw.shape), _const_spec(wr.shape),
        _const_spec(wgt.shape), _const_spec(gbt.shape),
    ] + [pl.BlockSpec((TM, LANES), lambda t, b: (t, 0)) for _ in range(4)]
    out_shape = [
        jax.ShapeDtypeStruct((B, T, D), BF16),
        jax.ShapeDtypeStruct((B, M_HEADS, T, M_HEAD_DIM), BF16),
        jax.ShapeDtypeStruct((B, M_HEADS, T, M_HEAD_DIM), BF16),
        jax.ShapeDtypeStruct((B, M_HEADS, nch, M_HEAD_DIM, CHUNK), BF16),
        jax.ShapeDtypeStruct((B, 2, nch, 16, CHUNK), F32),
        jax.ShapeDtypeStruct((B, A_HEADS, T, A_HEAD_DIM), BF16),
        jax.ShapeDtypeStruct((B, A_KV_HEADS, T, A_HEAD_DIM), BF16),
        jax.ShapeDtypeStruct((B, A_KV_HEADS, T // A_BLOCK, LANES, A_BLOCK), BF16),
        jax.ShapeDtypeStruct((B, R_HEADS, T, R_QK_DIM), BF16),
        jax.ShapeDtypeStruct((B, R_HEADS, nch, R_QK_DIM, CHUNK), BF16),
        jax.ShapeDtypeStruct((B, R_HEADS, T, R_V_DIM), BF16),
    ]
    out_specs = [
        pl.BlockSpec((BB, TM, D), tile),
        pl.BlockSpec((BB, M_HEADS, TM, M_HEAD_DIM), head_tile),
        pl.BlockSpec((BB, M_HEADS, TM, M_HEAD_DIM), head_tile),
        pl.BlockSpec((BB, M_HEADS, TM // CHUNK, M_HEAD_DIM, CHUNK), lambda t, b: (b, 0, t, 0, 0)),
        pl.BlockSpec((BB, 2, TM // CHUNK, 16, CHUNK), lambda t, b: (b, 0, t, 0, 0)),
        pl.BlockSpec((BB, A_HEADS, TM, A_HEAD_DIM), head_tile),
        pl.BlockSpec((BB, A_KV_HEADS, TM, A_HEAD_DIM), head_tile),
        pl.BlockSpec((BB, A_KV_HEADS, TM // A_BLOCK, LANES, A_BLOCK), lambda t, b: (b, 0, t, 0, 0)),
        pl.BlockSpec((BB, R_HEADS, TM, R_QK_DIM), head_tile),
        pl.BlockSpec((BB, R_HEADS, TM // CHUNK, R_QK_DIM, CHUNK), lambda t, b: (b, 0, t, 0, 0)),
        pl.BlockSpec((BB, R_HEADS, TM, R_V_DIM), head_tile),
    ]
    return pl.pallas_call(
        functools.partial(_pre_kernel, split=ctx is not None),
        grid=(nt, B // BB),
        in_specs=in_specs,
        out_specs=out_specs,
        out_shape=out_shape,
        scratch_shapes=[pltpu.VMEM((BB, TM + 2 * HALO, D), BF16)],
        compiler_params=_params(("parallel", "parallel")),
        name="pre_proj",
    )(xs if ctx is None else ctx, xs, xs, xs, modsel, nw, wqk, cw, wr, wgt, gbt, *tabs)


def _bwd_chunk(i, n_ctx, n_all):
    if isinstance(i, int):
        return n_ctx - 1 - i if i < n_ctx else n_all - 1 + n_ctx - i
    return n_all - 1 + n_ctx - i


def _chunk_start(c):
    return c * CHUNK if isinstance(c, int) else pl.multiple_of(c * CHUNK, CHUNK)


def _chunk_loop(body, n_ctx, n_all, unroll):
    for i in range(n_ctx):
        body(i, 0)
    n_lat = n_all - n_ctx
    while n_lat % unroll:
        unroll //= 2
    lax.fori_loop(n_ctx, n_all, body, 0, unroll=unroll)


def _mlstm_kernel(q_ref, k_ref, vt_ref, gr_ref, o_ref, c_ref, n_ref, m_ref,
                  cs_ref, ns_ref, ms_ref, qp_ref, nq_ref, *, n_ctx):
    n_all = gr_ref.shape[2]
    dh = q_ref.shape[3]
    c_ref[...] = jnp.zeros(c_ref.shape, F32)
    n_ref[...] = jnp.zeros(n_ref.shape, F32)
    m_ref[...] = jnp.zeros(m_ref.shape, F32)

    def scan_unit(c, hh, d):
        u = 2 * hh + d
        c0 = _chunk_start(c)
        k = k_ref[0, hh, pl.ds(c0, CHUNK), :]
        vt = vt_ref[0, hh, c]
        rows = gr_ref[0, 0, c, 8 * hh:8 * hh + 8, :]
        tot = rows[4 + d:5 + d]
        aend = rows[2 * d:2 * d + 1] + tot
        amax = rows[6 + d:7 + d]
        cst = c_ref[u]
        nv = n_ref[u]
        m = m_ref[u]
        cs_ref[u, c] = cst.astype(BF16)
        ns_ref[u, c] = jnp.broadcast_to(nv, (16, dh)).astype(BF16)
        ms_ref[u, c] = jnp.broadcast_to(m, (8, CHUNK))
        m_new = jnp.maximum(tot + m, amax)
        w = jnp.exp(aend - m_new)
        dec = jnp.exp(tot + m - m_new)
        vw = (vt.astype(F32) * w).astype(BF16)
        c_ref[u] = dec * cst + _dot(vw, k)
        wk = _dot(jnp.broadcast_to(w, (16, CHUNK)).astype(BF16), k)
        n_ref[u] = dec * nv + wk[0:1]
        m_ref[u] = m_new

    def scan_step(i, carry):
        cb = _bwd_chunk(i, n_ctx, n_all)
        for hh in range(2):
            scan_unit(i, hh, 0)
            scan_unit(cb, hh, 1)
        return carry

    _chunk_loop(scan_step, n_ctx, n_all, 4)

    row = lax.broadcasted_iota(jnp.int32, (CHUNK, CHUNK), 0)
    col = lax.broadcasted_iota(jnp.int32, (CHUNK, CHUNK), 1)
    tri = (row <= col, row >= col)

    def query_products(c, slot):
        c0 = _chunk_start(c)
        for hh in range(2):
            q = q_ref[0, hh, pl.ds(c0, CHUNK), :]
            k = k_ref[0, hh, pl.ds(c0, CHUNK), :]
            qp_ref[slot, hh, 0] = _dot_nt(k, q)
            for d in range(2):
                u = 2 * hh + d
                qp_ref[slot, hh, 1 + d] = _dot_nt(cs_ref[u, c], q)
                nq_ref[slot, u] = _dot_nt(ns_ref[u, c], q)[0:8]

    def outputs(c, slot):
        c0 = _chunk_start(c)
        gcv = jnp.concatenate([gr_ref[0, 0, c], jnp.zeros((CHUNK - 16, CHUNK), F32)], axis=0).T
        for hh in range(2):
            vt = vt_ref[0, hh, c]
            rows = gr_ref[0, 0, c, 8 * hh:8 * hh + 8, :]
            st = qp_ref[slot, hh, 0]
            acc = None
            for d in range(2):
                u = 2 * hh + d
                a0c = gcv[:, 8 * hh + 2 * d:8 * hh + 2 * d + 1]
                b = rows[2 * d + 1:2 * d + 2]
                m = ms_ref[u, c, 0:1, :]
                dlog = jnp.where(tri[d], a0c + b, NEG_INF)
                mj = jnp.maximum(b + m, jnp.max(dlog, axis=0, keepdims=True))
                sd = st * jnp.exp(dlog - mj)
                iw = jnp.exp(b + m - mj)
                den = iw * nq_ref[slot, u, 0:1, :] + jnp.sum(sd, axis=0, keepdims=True)
                inv = 1.0 / jnp.maximum(jnp.abs(den), jnp.exp(-mj))
                ht = (_dot(vt, sd.astype(BF16)) + iw * qp_ref[slot, hh, 1 + d]) * inv
                acc = ht if acc is None else acc + ht
            o_ref[0, hh, pl.ds(c0, CHUNK), :] = acc.T

    query_products(0, 0)

    def out_pair(i, carry):
        c = 2 * i
        query_products(c + 1, 1)
        outputs(c, 0)
        query_products(jnp.minimum(c + 2, n_all - 1), 0)
        outputs(c + 1, 1)
        return carry

    n_pairs = n_all // 2
    unroll = 4
    first = n_pairs % unroll
    for i in range(first):
        out_pair(i, 0)
    lax.fori_loop(first, n_pairs, out_pair, 0, unroll=unroll)


def _mlstm_call(mq, mk, mvt, gr, n_ctx):
    B, H, T, dh = mq.shape
    nch = T // CHUNK
    qspec = pl.BlockSpec((1, 2, T, dh), lambda b, p: (b, p, 0, 0))
    return pl.pallas_call(
        functools.partial(_mlstm_kernel, n_ctx=n_ctx),
        grid=(B, H // 2),
        in_specs=[qspec, qspec,
                  pl.BlockSpec((1, 2, nch, dh, CHUNK), lambda b, p: (b, p, 0, 0, 0)),
                  pl.BlockSpec((1, 1, nch, 16, CHUNK), lambda b, p: (b, p, 0, 0, 0))],
        out_specs=pl.BlockSpec((1, 2, T, dh), lambda b, p: (b, p, 0, 0)),
        out_shape=jax.ShapeDtypeStruct((B, H, T, dh), F32),
        scratch_shapes=[pltpu.VMEM((4, dh, dh), F32), pltpu.VMEM((4, 1, dh), F32),
                        pltpu.VMEM((4, 1, CHUNK), F32),
                        pltpu.VMEM((4, nch, dh, dh), BF16), pltpu.VMEM((4, nch, 16, dh), BF16),
                        pltpu.VMEM((4, nch, 8, CHUNK), F32),
                        pltpu.VMEM((2, 2, 3, dh, CHUNK), F32), pltpu.VMEM((2, 4, 8, CHUNK), F32)],
        compiler_params=_params(("parallel", "parallel")),
        name="mlstm",
    )(mq, mk, mvt, gr)


def _ret_kernel(q_ref, kt_ref, v_ref, lg_ref, o_ref, st_ref, sts_ref, dec_ref, ib_ref, wb_ref, *, n_ctx):
    sub = RCHUNK // CHUNK
    n_all = kt_ref.shape[2] // sub
    dk = q_ref.shape[3]
    dv = v_ref.shape[3]
    st_ref[...] = jnp.zeros(st_ref.shape, F32)
    row = lax.broadcasted_iota(jnp.int32, (RCHUNK, RCHUNK), 0).astype(F32)
    col = lax.broadcasted_iota(jnp.int32, (RCHUNK, RCHUNK), 1).astype(F32)
    kcol = lax.broadcasted_iota(jnp.int32, (dk, RCHUNK), 1).astype(F32)
    cds = []
    dec = None
    for d in range(2):
        lg = _log_sigmoid(lg_ref[0, d])[0:1]
        diff = row - col if d == 0 else col - row
        dd = jnp.where(diff >= 0, jnp.exp(lg * jnp.maximum(diff, 0.0)), 0.0)
        dec = dd if dec is None else dec + dd
        pin = row if d == 0 else RCHUNK - 1.0 - row
        ib_ref[d] = jnp.exp(lg * (pin + 1.0))
        pkey = kcol if d == 0 else RCHUNK - 1.0 - kcol
        wb_ref[d] = jnp.exp(lg * (RCHUNK - 1.0 - pkey))
        cds.append(jnp.exp(lg * RCHUNK))
    dec_ref[...] = dec

    def rows(c):
        r0 = c * RCHUNK
        return pl.ds(r0 if isinstance(c, int) else pl.multiple_of(r0, RCHUNK), RCHUNK)

    def k_t(c):
        return jnp.concatenate([kt_ref[0, 0, sub * c + j] for j in range(sub)], axis=1)

    def scan_step(i, carry):
        cb = _bwd_chunk(i, n_ctx, n_all)
        for d, c in ((0, i), (1, cb)):
            st = st_ref[d]
            sts_ref[c, :, d * dv:(d + 1) * dv] = st.astype(BF16)
            kw = (k_t(c).astype(F32) * wb_ref[d]).astype(BF16)
            st_ref[d] = cds[d] * st + _dot(kw, v_ref[0, 0, rows(c), :])
        return carry

    _chunk_loop(scan_step, n_ctx, n_all, 8)

    def out_step(c, carry):
        q = q_ref[0, 0, rows(c), :]
        s = (_dot(q, k_t(c)) * dec_ref[...]).astype(BF16)
        qst = _dot(q, sts_ref[c])
        o_ref[0, 0, rows(c), :] = (_dot(s, v_ref[0, 0, rows(c), :])
                                   + ib_ref[0] * qst[:, 0:dv] + ib_ref[1] * qst[:, dv:])
        return carry

    _chunk_loop(out_step, n_ctx, n_all, 4)


def _ret_call(rq, rkt, rv, lgb, n_ctx):
    B, H, T, dk = rq.shape
    dv = rv.shape[-1]
    nch = T // CHUNK
    assert dv == RCHUNK and dk <= RCHUNK and T % RCHUNK == 0
    return pl.pallas_call(
        functools.partial(_ret_kernel, n_ctx=n_ctx),
        grid=(B, H),
        in_specs=[pl.BlockSpec((1, 1, T, dk), lambda b, h: (b, h, 0, 0)),
                  pl.BlockSpec((1, 1, nch, dk, CHUNK), lambda b, h: (b, h, 0, 0, 0)),
                  pl.BlockSpec((1, 1, T, dv), lambda b, h: (b, h, 0, 0)),
                  pl.BlockSpec((1, 2, 8, dv), lambda b, h: (h, 0, 0, 0))],
        out_specs=pl.BlockSpec((1, 1, T, dv), lambda b, h: (b, h, 0, 0)),
        out_shape=jax.ShapeDtypeStruct((B, H, T, dv), F32),
        scratch_shapes=[pltpu.VMEM((2, dk, dv), F32), pltpu.VMEM((T // RCHUNK, dk, 2 * dv), BF16),
                        pltpu.VMEM((RCHUNK, RCHUNK), F32), pltpu.VMEM((2, RCHUNK, dv), F32),
                        pltpu.VMEM((2, dk, RCHUNK), F32)],
        compiler_params=_params(("parallel", "parallel")),
        name="retention",
    )(rq, rkt, rv, lgb)


def _attn_kernel(q_ref, k_ref, vt_ref, sink_ref, o_ref, bias_ref, s_ref, *, n_ctx_tok, need_ctx):
    T = q_ref.shape[2]
    L = n_ctx_tok
    S = T - L
    nb = S // A_BLOCK
    nspan = 3
    span = nspan * A_BLOCK
    ncb = L // A_BLOCK
    cols = A_GROUP * A_BLOCK
    kc = k_ref[0, 0, 0:L, :]
    vtc = jnp.concatenate([vt_ref[0, 0, j] for j in range(ncb)], axis=1)
    sink = sink_ref[0]

    kpos = lax.broadcasted_iota(jnp.int32, (span, A_BLOCK), 0)
    qpos = lax.broadcasted_iota(jnp.int32, (span, A_BLOCK), 1)
    for delta in range(nspan):
        bias_ref[delta] = jnp.where(jnp.abs(qpos + delta * A_BLOCK - kpos) <= A_WINDOW, 0.0, NEG_INF)

    def finish(blk, acc, m):
        den = acc[A_HEAD_DIM:A_HEAD_DIM + 1] + jnp.exp2(sink - m)
        ot = (acc[0:A_HEAD_DIM] * (1.0 / den)).astype(BF16)
        for g in range(A_GROUP):
            o_ref[0, blk, g * A_HEAD_DIM:(g + 1) * A_HEAD_DIM, :] = ot[:, g * A_BLOCK:(g + 1) * A_BLOCK]

    for cbk in range(ncb):
        if need_ctx:
            qs = q_ref[0, :, cbk * A_BLOCK:(cbk + 1) * A_BLOCK, :].reshape(cols, A_HEAD_DIM)
            s = _dot_nt(kc, qs)
            m = jnp.maximum(jnp.max(s, axis=0, keepdims=True), sink)
            finish(cbk, _dot(vtc, jnp.exp2(s - m).astype(BF16)), m)
        else:
            o_ref[0, cbk] = jnp.zeros(o_ref.shape[2:], BF16)

    def first_key_block(n):
        return jnp.clip(n - 1, 0, nb - nspan)

    def scores(n, slot):
        q0 = pl.multiple_of(L + n * A_BLOCK, A_BLOCK)
        kb = first_key_block(n)
        k0 = pl.multiple_of(L + kb * A_BLOCK, A_BLOCK)
        qs = q_ref[0, :, pl.ds(q0, A_BLOCK), :].reshape(cols, A_HEAD_DIM)
        bias = bias_ref[n - kb]
        s_ref[slot, 0:span, :] = (_dot_nt(k_ref[0, 0, pl.ds(k0, span), :], qs)
                                  + jnp.concatenate([bias] * A_GROUP, axis=1))
        s_ref[slot, span:, :] = _dot_nt(kc, qs)

    def softmax_pv(n, slot):
        kb = first_key_block(n)
        s = s_ref[slot]
        m = jnp.maximum(jnp.max(s, axis=0, keepdims=True), sink)
        p = jnp.exp2(s - m).astype(BF16)
        vt = jnp.concatenate([vt_ref[0, 0, ncb + kb + j] for j in range(nspan)] + [vtc], axis=1)
        finish(ncb + n, _dot(vt, p), m)

    scores(0, 0)

    def block_pair(i, carry):
        n = 2 * i
        scores(n + 1, 1)
        softmax_pv(n, 0)
        scores(jnp.minimum(n + 2, nb - 1), 0)
        softmax_pv(n + 1, 1)
        return carry

    n_pairs = nb // 2
    first = n_pairs % 2
    if first:
        block_pair(0, 0)
    lax.fori_loop(first, n_pairs, block_pair, 0, unroll=2)


def _attn_call(aq, ak, avt, sink_rows, n_ctx_tok, need_ctx):
    B, _, T, hd = aq.shape
    nblk = T // A_BLOCK
    cols = A_GROUP * A_BLOCK
    return pl.pallas_call(
        functools.partial(_attn_kernel, n_ctx_tok=n_ctx_tok, need_ctx=need_ctx),
        grid=(B, A_KV_HEADS),
        in_specs=[pl.BlockSpec((1, A_GROUP, T, hd), lambda b, h: (b, h, 0, 0)),
                  pl.BlockSpec((1, 1, T, hd), lambda b, h: (b, h, 0, 0)),
                  pl.BlockSpec((1, 1, nblk, LANES, A_BLOCK), lambda b, h: (b, h, 0, 0, 0)),
                  pl.BlockSpec((1, 1, cols), lambda b, h: (h, 0, 0))],
        out_specs=pl.BlockSpec((1, nblk, A_GROUP * hd, A_BLOCK), lambda b, h: (b, 0, h, 0)),
        out_shape=jax.ShapeDtypeStruct((B, nblk, A_WIDTH, A_BLOCK), BF16),
        scratch_shapes=[pltpu.VMEM((3, 3 * A_BLOCK, A_BLOCK), F32),
                        pltpu.VMEM((2, 3 * A_BLOCK + n_ctx_tok, cols), F32)],
        compiler_params=_params(("parallel", "parallel")),
        name="window_attn",
    )(aq, ak, avt, sink_rows)


def _head_norm(y):
    mu = jnp.mean(y, axis=-1, keepdims=True)
    yc = y - mu
    var = jnp.mean(yc * yc, axis=-1, keepdims=True)
    return yc * lax.rsqrt(var + NORM_EPS)


def _post_kernel(c_ref, x_ref, hb_ref, hm_ref, ya_ref, hr_ref, mod_ref, wg_ref, mnw_ref, rnw_ref,
                 wbm_ref, wba_ref, wbr_ref, wo_ref, o_ref, *, split):
    for i in range(x_ref.shape[0]):
        hb = hb_ref[i]
        off = 0
        hm = jnp.concatenate([_head_norm(hm_ref[i, h]) for h in range(M_HEADS)], axis=1) * mnw_ref[...]
        ym = (_sigmoid(_dot(hb, wg_ref[:, off:off + M_WIDTH])) * hm).astype(BF16)
        off += M_WIDTH
        hr = jnp.concatenate([_head_norm(hr_ref[i, h]) for h in range(R_HEADS)], axis=1) * rnw_ref[...]
        rg = _dot(hb, wg_ref[:, off:off + R_V_WIDTH])
        yr = (rg * _sigmoid(rg) * hr).astype(BF16)
        off += R_V_WIDTH
        z = _sigmoid(_dot(hb, wg_ref[:, off:off + D_MODEL])) * _dot(ym, wbm_ref[...])
        off += D_MODEL
        pa = jnp.concatenate([_dot_tn(ya_ref[i, c], wba_ref[...]) for c in range(TM // A_BLOCK)], axis=0)
        z = z + _sigmoid(_dot(hb, wg_ref[:, off:off + D_MODEL])) * pa
        off += D_MODEL
        z = z + _sigmoid(_dot(hb, wg_ref[:, off:off + D_MODEL])) * _dot(yr, wbr_ref[...])
        y = _dot(z.astype(BF16), wo_ref[...])
        one = pl.ds(i, 1)
        xres = _tile_rows(c_ref.at[one] if split else None, x_ref.at[one])
        o_ref[i] = xres + mod_ref[i, 0, 2:3, :] * y


def _post_call(ctx, xs, hb, hm, ya, hr, modsel, wg, mnw, rnw, wbm, wba, wbr, wo, t_off):
    B, T, D = hb.shape
    nt = T // TM - t_off
    split = ctx is not None
    assert not (split and t_off)

    def tile(t, b):
        return (b, t + t_off, 0)

    def head_tile(t, b):
        return (b, 0, t + t_off, 0)

    return pl.pallas_call(
        functools.partial(_post_kernel, split=split),
        grid=(nt, B // BB),
        in_specs=[pl.BlockSpec((BB, TM if split else HALO, D), lambda t, b: (b, 0, 0)),
                  pl.BlockSpec((BB, TM, D), (lambda t, b: (b, jnp.maximum(t - 1, 0), 0)) if split else tile),
                  pl.BlockSpec((BB, TM, D), tile),
                  pl.BlockSpec((BB, M_HEADS, TM, M_HEAD_DIM), head_tile),
                  pl.BlockSpec((BB, TM // A_BLOCK, A_WIDTH, A_BLOCK), lambda t, b: (b, t + t_off, 0, 0)),
                  pl.BlockSpec((BB, R_HEADS, TM, R_V_DIM), head_tile),
                  pl.BlockSpec((BB, 1, 6, D), lambda t, b: (b, jnp.minimum(t + t_off, 1), 0, 0)),
                  _const_spec(wg.shape), _const_spec(mnw.shape), _const_spec(rnw.shape),
                  _const_spec(wbm.shape), _const_spec(wba.shape), _const_spec(wbr.shape),
                  _const_spec(wo.shape)],
        out_specs=pl.BlockSpec((BB, TM, D), lambda t, b: (b, t, 0)),
        out_shape=jax.ShapeDtypeStruct((B, nt * TM, D), F32),
        compiler_params=_params(("parallel", "parallel")),
        name="post_merge",
    )(ctx if split else xs, xs, hb, hm, ya, hr, modsel, wg, mnw, rnw, wbm, wba, wbr, wo)


def _ffn_kernel(x_ref, xp_ref, xn_ref, mod_ref, nw_ref, wa_ref, wb_ref, cw_ref, cb_ref, wd_ref, fw_ref,
                o_ref, hext_ref, u_ref, *, has_ctx, final):
    t = pl.program_id(0)
    nt = pl.num_programs(0)
    first_lat = 1 if has_ctx else 0
    nw = nw_ref[...]
    ext = TM + 2 * HALO
    for i in range(x_ref.shape[0]):
        sh = mod_ref[i, 0, 3:4, :]
        sc = mod_ref[i, 0, 4:5, :]
        g2 = mod_ref[i, 0, 5:6, :]
        x = x_ref[i]
        hp = _norm_mod(xp_ref[i], nw, sc, sh)
        hn = _norm_mod(xn_ref[i], nw, sc, sh)
        hp = jnp.where(t <= first_lat, 0.0, hp)
        hn = jnp.where((t == first_lat - 1) | (t == nt - 1), 0.0, hn)
        hext_ref[i, 0:TM, :] = _norm_mod(x, nw, sc, sh).astype(BF16)
        hext_ref[i, TM:, :] = jnp.concatenate([hn, hp], axis=0).astype(BF16)
        for j in range(FFN_DIM // FFN_NC):
            cs = slice(j * FFN_NC, (j + 1) * FFN_NC)
            a = _dot(hext_ref[i], wa_ref[:, cs])
            b = _dot(hext_ref[i, 0:TM, :], wb_ref[:, cs])
            cw = cw_ref[:, cs]
            prev = pltpu.roll(a, 1, 0)[0:TM]
            nxt = pltpu.roll(a, ext - 1, 0)[0:TM]
            conv = prev * cw[0:1] + a[0:TM] * cw[1:2] + nxt * cw[2:3] + cb_ref[:, cs]
            u_ref[i, :, cs] = (conv * _sigmoid(conv) * b).astype(BF16)
        y = x + g2 * _dot(u_ref[i], wd_ref[...])
        if final:
            ms = jnp.mean(y * y, axis=-1, keepdims=True)
            y = y * lax.rsqrt(ms + NORM_EPS) * fw_ref[...]
        o_ref[i] = y


def _ffn_call(xs, modsel, nw, wa, wb, cw, cb, wd, fw, has_ctx, final):
    B, T, D = xs.shape
    nt = T // TM
    hb16 = TM // HALO
    first_lat = 1 if has_ctx else 0
    tile = pl.BlockSpec((BB, TM, D), lambda t, b: (b, t, 0))
    return pl.pallas_call(
        functools.partial(_ffn_kernel, has_ctx=has_ctx, final=final),
        grid=(nt, B // BB),
        in_specs=[tile,
                  pl.BlockSpec((BB, HALO, D), lambda t, b: (b, jnp.maximum(t * hb16 - 1, 0), 0)),
                  pl.BlockSpec((BB, HALO, D), lambda t, b: (b, jnp.minimum((t + 1) * hb16, T // HALO - 1), 0)),
                  pl.BlockSpec((BB, 1, 6, D), lambda t, b: (b, jnp.minimum(t + 1 - first_lat, 1), 0, 0)),
                  _const_spec(nw.shape), _const_spec(wa.shape), _const_spec(wb.shape),
                  _const_spec(cw.shape), _const_spec(cb.shape), _const_spec(wd.shape),
                  _const_spec(fw.shape)],
        out_specs=tile,
        out_shape=jax.ShapeDtypeStruct((B, T, D), F32),
        scratch_shapes=[pltpu.VMEM((BB, TM + 2 * HALO, D), BF16), pltpu.VMEM((BB, TM, FFN_DIM), BF16)],
        compiler_params=_params(("parallel", "parallel")),
        name="conv_ffn",
    )(xs, xs, xs, modsel, nw, wa, wb, cw, cb, wd, fw)


def _split_cols(w):
    out = []
    acc = 0
    for s in IN_SPLITS:
        out.append(w[:, acc:acc + s])
        acc += s
    return out


def _rope_tables(L, S):
    T = L + S
    j = jnp.arange(LANES)
    jj = j % 32
    inv_a = ROPE_BASE ** (-(jj % 16).astype(F32) / 16.0)
    t = jnp.arange(S)
    pos = jnp.where(((j % A_HEAD_DIM) // 32 == 0)[None, :], (t // GRID_W)[:, None], (t % GRID_W)[:, None])
    ang = pos.astype(F32) * inv_a[None, :]
    sign = jnp.where(jj < 16, -1.0, 1.0)[None, :]
    ca = jnp.concatenate([jnp.ones((L, LANES), F32), jnp.cos(ang)], axis=0)
    sa = jnp.concatenate([jnp.zeros((L, LANES), F32), jnp.sin(ang) * sign], axis=0)
    half = R_QK_DIM // 2
    inv_r = ROPE_BASE ** (-(j % half).astype(F32) / half)
    angr = jnp.arange(T).astype(F32)[:, None] * inv_r[None, :]
    signr = jnp.where(j < half, -1.0, 1.0)[None, :]
    return ca, sa, jnp.cos(angr), jnp.sin(angr) * signr


def kernel(x, c, ctx, c_ctx, mod_w, mod_b, norm1_w, norm2_w, w_in, m_gate_bias, m_conv_w, m_norm_w,
           a_sink, ret_logit, ret_norm_w, w_br_m, w_br_a, w_br_r, w_out, ffn_up, ffn_conv_w, ffn_conv_b,
           ffn_down, final_norm_w):
    B, S, D = x.shape
    L = ctx.shape[1]
    depth = mod_w.shape[0]
    assert D == D_MODEL and L == TM and S % TM == 0 and S >= 3 * A_BLOCK and B % BB == 0
    T = L + S
    n_ctx = L // CHUNK

    rows = -(-(B + 1) // 8) * 8
    cpad = jnp.zeros((rows, D), F32).at[:B].set(c).at[B].set(c_ctx)
    mods = _modulation(cpad, mod_w, mod_b)
    tabs = _rope_tables(L, S)
    stream = (ctx, x) if depth > 1 else (None, jnp.concatenate([ctx, x], axis=1))

    gperm = jnp.arange(4 * M_HEADS).reshape(4, M_HEADS).T.reshape(-1)

    for l in range(depth):
        last = l == depth - 1
        lat = mods[l, :B].reshape(B, 6, D)
        cm = jnp.broadcast_to(mods[l, B].reshape(1, 6, D), (B, 6, D))
        modsel = jnp.stack([cm, lat], axis=1)

        (w_mq, w_mk, w_mv, w_mo, w_mg, w_aq, w_ak, w_av,
         w_rq, w_rk, w_rv, w_rg, w_gm, w_ga, w_gr) = _split_cols(w_in[l])
        wqk = jnp.concatenate([w_mq, w_mk], axis=1).astype(BF16)
        wr = jnp.concatenate([w_mv, w_aq, w_ak, w_av, w_rq, w_rk, w_rv], axis=1).astype(BF16)
        wgt = w_mg[:, gperm].T.astype(BF16)
        gbt = jnp.broadcast_to(m_gate_bias[l][gperm].reshape(-1, 1), (4 * M_HEADS, LANES))

        (hb, mq, mk, mv, gr, aq, ak, av, rq, rk, rv) = _pre_call(
            *stream, modsel, norm1_w[l].reshape(1, D), wqk, m_conv_w[l], wr, wgt, gbt, tabs)

        hm = _mlstm_call(mq, mk, mv, gr, n_ctx)
        sink_rows = jnp.repeat(a_sink[l].reshape(A_KV_HEADS, A_GROUP), A_BLOCK, axis=1)[:, None, :]
        ya = _attn_call(aq, ak, av, sink_rows.astype(F32) * LOG2E, L, not last)
        lgb = jnp.broadcast_to(ret_logit[l].T[:, :, None, None], (R_HEADS, 2, 8, R_V_DIM)).astype(F32)
        hr = _ret_call(rq, rk, rv, lgb, L // RCHUNK)

        t_off = 1 if last else 0
        wgates = jnp.concatenate([w_mo, w_rg, w_gm, w_ga, w_gr], axis=1).astype(BF16)
        x1 = _post_call(*stream, hb, hm, ya, hr, modsel, wgates, m_norm_w[l].reshape(1, -1),
                        ret_norm_w[l].reshape(1, -1), w_br_m[l].astype(BF16), w_br_a[l].astype(BF16),
                        w_br_r[l].astype(BF16), w_out[l].astype(BF16), t_off)
        xc = _ffn_call(x1, modsel, norm2_w[l].reshape(1, D), ffn_up[l][:, :FFN_DIM].astype(BF16),
                       ffn_up[l][:, FFN_DIM:].astype(BF16), ffn_conv_w[l], ffn_conv_b[l].reshape(1, -1),
                       ffn_down[l].astype(BF16), final_norm_w.reshape(1, D), not last, last)
        stream = (None, xc)
    return xc
```

```python
import functools

import jax
import jax.numpy as jnp
from jax import lax
from jax.experimental import pallas as pl
from jax.experimental.pallas import tpu as pltpu

F32 = jnp.float32
BF16 = jnp.bfloat16

D_MODEL = 1024
GRID_W = 64
NORM_EPS = 1e-6
ROPE_BASE = 10000.0
NEG_INF = -1e30
LOG2E = 1.4426950408889634

M_HEADS = 4
M_HEAD_DIM = 128
M_WIDTH = M_HEADS * M_HEAD_DIM
A_HEADS = 8
A_KV_HEADS = 2
A_GROUP = A_HEADS // A_KV_HEADS
A_HEAD_DIM = 64
A_WIDTH = A_HEADS * A_HEAD_DIM
A_KV_WIDTH = A_KV_HEADS * A_HEAD_DIM
A_WINDOW = 128
A_BLOCK = 128
R_HEADS = 4
R_QK_DIM = 128
R_V_DIM = 256
R_QK_WIDTH = R_HEADS * R_QK_DIM
R_V_WIDTH = R_HEADS * R_V_DIM
FFN_DIM = 2816

IN_SPLITS = (M_WIDTH, M_WIDTH, M_WIDTH, M_WIDTH, 4 * M_HEADS,
             A_WIDTH, A_KV_WIDTH, A_KV_WIDTH,
             R_QK_WIDTH, R_QK_WIDTH, R_V_WIDTH, R_V_WIDTH,
             D_MODEL, D_MODEL, D_MODEL)

TM = 256
BB = 2
HALO = 8
CHUNK = 128
RCHUNK = 256
FFN_NC = 256
LANES = 128
VMEM_LIMIT = 56 * 1024 * 1024


def _sigmoid(x):
    return 1.0 / (1.0 + jnp.exp(-x))


def _log_sigmoid(x):
    return jnp.minimum(x, 0.0) - jnp.log1p(jnp.exp(-jnp.abs(x)))


def _dot(a, b):
    return jnp.dot(a, b, preferred_element_type=F32)


def _dot_nt(a, b):
    return lax.dot_general(a, b, (((1,), (1,)), ((), ())), preferred_element_type=F32)


def _dot_tn(a, b):
    return lax.dot_general(a, b, (((0,), (0,)), ((), ())), preferred_element_type=F32)


def _const_spec(shape):
    nd = len(shape)
    return pl.BlockSpec(shape, lambda *_: (0,) * nd, pipeline_mode=pl.Buffered(1))


def _params(sem):
    return pltpu.CompilerParams(dimension_semantics=sem, vmem_limit_bytes=VMEM_LIMIT)


def _mod_kernel(c_ref, w_ref, b_ref, o_ref):
    c = c_ref[...]
    s = (c * _sigmoid(c)).astype(BF16)
    o_ref[0] = _dot(s, w_ref[0].astype(BF16)) + b_ref[0]


def _modulation(cpad, mod_w, mod_b):
    depth, d, n = mod_w.shape
    rows = cpad.shape[0]
    bn = 1024
    return pl.pallas_call(
        _mod_kernel,
        grid=(depth, n // bn),
        in_specs=[pl.BlockSpec((rows, d), lambda l, j: (0, 0)),
                  pl.BlockSpec((1, d, bn), lambda l, j: (l, 0, j)),
                  pl.BlockSpec((1, 1, bn), lambda l, j: (l, 0, j))],
        out_specs=pl.BlockSpec((1, rows, bn), lambda l, j: (l, 0, j)),
        out_shape=jax.ShapeDtypeStruct((depth, rows, n), F32),
        compiler_params=_params(("parallel", "parallel")),
        name="modulation",
    )(cpad, mod_w, mod_b.reshape(depth, 1, n))


def _norm_mod(xv, nw, sc, sh):
    ms = jnp.mean(xv * xv, axis=-1, keepdims=True)
    return (xv * lax.rsqrt(ms + NORM_EPS) * nw) * (1.0 + sc) + sh


def _seg_cumsum(x, axis, reverse):
    n = x.shape[axis]
    idx = lax.broadcasted_iota(jnp.int32, x.shape, axis)
    s = 1
    while s < n:
        if reverse:
            shifted = pltpu.roll(x, n - s, axis)
            x = x + jnp.where(idx < n - s, shifted, 0.0)
        else:
            shifted = pltpu.roll(x, s, axis)
            x = x + jnp.where(idx >= s, shifted, 0.0)
        s *= 2
    return x


def _gate_transform(raw, gate_axis):
    tok_axis = 1 - gate_axis
    k = lax.broadcasted_iota(jnp.int32, raw.shape, gate_axis) % 4
    lf = _log_sigmoid(raw)
    cum_f = _seg_cumsum(jnp.where(k == 1, lf, 0.0), tok_axis, False)
    cum_b = _seg_cumsum(jnp.where(k == 3, lf, 0.0), tok_axis, True)
    bsel = jnp.where(k == 1, cum_f, jnp.where(k == 3, cum_b, 0.0))
    n = raw.shape[gate_axis]
    bnext = pltpu.roll(bsel, n - 1, gate_axis)
    nt = raw.shape[tok_axis]
    tot_f = lax.slice_in_dim(cum_f, nt - 1, nt, axis=tok_axis)
    tot_b = lax.slice_in_dim(cum_b, 0, 1, axis=tok_axis)
    total = jnp.broadcast_to(tot_f + tot_b, raw.shape)
    return jnp.where(k % 2 == 1, bsel, raw - bnext), total


def _pre_kernel(c_ref, x_ref, xp_ref, xn_ref, mod_ref, nw_ref, wqk_ref, cw_ref, wr_ref, wgt_ref,
                gbt_ref, ca_ref, sa_ref, cr_ref, sr_ref, *out_and_scratch, split):
    for i in range(x_ref.shape[0]):
        one = pl.ds(i, 1)
        _pre_tile(c_ref.at[one] if split else None,
                  x_ref.at[one], xp_ref.at[one], xn_ref.at[one], mod_ref.at[one], nw_ref, wqk_ref, cw_ref,
                  wr_ref, wgt_ref, gbt_ref, ca_ref, sa_ref, cr_ref, sr_ref,
                  *[r.at[one] for r in out_and_scratch[:-1]], out_and_scratch[-1].at[i])


def _tile_rows(c_ref, x_ref):
    if c_ref is None:
        return x_ref[0]
    return jnp.where(pl.program_id(0) == 0, c_ref[0], x_ref[0])


def _pre_tile(c_ref, x_ref, xp_ref, xn_ref, mod_ref, nw_ref, wqk_ref, cw_ref, wr_ref, wgt_ref,
              gbt_ref, ca_ref, sa_ref, cr_ref, sr_ref,
              hb_ref, mq_ref, mk_ref, mv_ref, gr_ref, aq_ref, ak_ref, av_ref,
              rq_ref, rk_ref, rv_ref, hext_ref):
    t = pl.program_id(0)
    nt = pl.num_programs(0)
    sh = mod_ref[0, 0, 0:1, :]
    sc = mod_ref[0, 0, 1:2, :]
    nw = nw_ref[...]
    hb = _norm_mod(_tile_rows(c_ref, x_ref), nw, sc, sh).astype(BF16)
    hb_ref[0] = hb

    grow = _dot_nt(wgt_ref[...], hb) + gbt_ref[:, 0:1]
    for c in range(TM // CHUNK):
        gr, tot = _gate_transform(grow[:, c * CHUNK:(c + 1) * CHUNK], 0)
        aend = gr + pltpu.roll(tot, 4 * M_HEADS - 1, 0)
        amax = jnp.broadcast_to(jnp.max(aend, axis=1, keepdims=True), aend.shape)
        for h in range(M_HEADS):
            hp_, o0 = h // 2, 8 * (h % 2)
            gr_ref[0, hp_, c, o0:o0 + 4, :] = gr[4 * h:4 * h + 4]
            for d in range(2):
                gr_ref[0, hp_, c, o0 + 4 + d:o0 + 5 + d, :] = tot[4 * h + 2 * d + 1:4 * h + 2 * d + 2]
                gr_ref[0, hp_, c, o0 + 6 + d:o0 + 7 + d, :] = amax[4 * h + 2 * d:4 * h + 2 * d + 1]

    hp = _norm_mod(xp_ref[0], nw, sc, sh)
    hn = _norm_mod(xn_ref[0], nw, sc, sh)
    hp = jnp.where(t <= 1, 0.0, hp)
    hn = jnp.where((t == 0) | (t == nt - 1), 0.0, hn)
    hext_ref[0:TM, :] = hb
    hext_ref[TM:, :] = jnp.concatenate([hn, hp], axis=0).astype(BF16)
    hext = hext_ref[...]
    ext = TM + 2 * HALO

    for j, (dst, scale) in enumerate(((mq_ref, 1.0), (mk_ref, M_HEAD_DIM ** -0.5))):
        p = _dot(hext, wqk_ref[:, j * M_WIDTH:(j + 1) * M_WIDTH])
        cw = cw_ref[:, j * M_WIDTH:(j + 1) * M_WIDTH]
        prev = pltpu.roll(p, 1, 0)[0:TM]
        nxt = pltpu.roll(p, ext - 1, 0)[0:TM]
        conv = prev * cw[0:1] + p[0:TM] * cw[1:2] + nxt * cw[2:3]
        act = conv * _sigmoid(conv) * scale
        for h in range(M_HEADS):
            dst[0, h] = act[:, h * M_HEAD_DIM:(h + 1) * M_HEAD_DIM].astype(BF16)

    off = 0
    p = _dot(hb, wr_ref[:, off:off + M_WIDTH])
    off += M_WIDTH
    for h in range(M_HEADS):
        for c in range(TM // CHUNK):
            blk = p[c * CHUNK:(c + 1) * CHUNK, h * M_HEAD_DIM:(h + 1) * M_HEAD_DIM]
            mv_ref[0, h, c] = blk.T.astype(BF16)

    lane = lax.broadcasted_iota(jnp.int32, (TM, LANES), 1)
    first16 = (lane % 32) < 16
    ca = ca_ref[...]
    sa = sa_ref[...]

    def rope_a(xs):
        partner = jnp.where(first16, pltpu.roll(xs, LANES - 16, 1), pltpu.roll(xs, 16, 1))
        return xs * ca + partner * sa

    p = _dot(hb, wr_ref[:, off:off + A_WIDTH])
    off += A_WIDTH
    for s in range(A_WIDTH // LANES):
        r = (rope_a(p[:, s * LANES:(s + 1) * LANES]) * (A_HEAD_DIM ** -0.5 * LOG2E)).astype(BF16)
        aq_ref[0, 2 * s] = r[:, 0:A_HEAD_DIM]
        aq_ref[0, 2 * s + 1] = r[:, A_HEAD_DIM:]
    p = _dot(hb, wr_ref[:, off:off + 2 * A_KV_WIDTH])
    off += 2 * A_KV_WIDTH
    r = rope_a(p[:, 0:LANES]).astype(BF16)
    for h in range(A_KV_HEADS):
        ak_ref[0, h] = r[:, h * A_HEAD_DIM:(h + 1) * A_HEAD_DIM]
    pv = p[:, LANES:]
    for h in range(A_KV_HEADS):
        vh = pv if h == 0 else pltpu.roll(pv, A_HEAD_DIM, 1)
        ve = jnp.where(lane < A_HEAD_DIM, vh, 1.0)
        for c in range(TM // A_BLOCK):
            av_ref[0, h, c] = ve[c * A_BLOCK:(c + 1) * A_BLOCK].T.astype(BF16)

    cr = cr_ref[...]
    sr = sr_ref[...]
    for is_k in (False, True):
        p = _dot(hb, wr_ref[:, off:off + R_QK_WIDTH])
        off += R_QK_WIDTH
        for h in range(R_HEADS):
            xs = p[:, h * R_QK_DIM:(h + 1) * R_QK_DIM]
            rot = xs * cr + pltpu.roll(xs, R_QK_DIM // 2, 1) * sr
            if is_k:
                rot = rot * R_QK_DIM ** -0.5
                for c in range(TM // CHUNK):
                    rk_ref[0, h, c] = rot[c * CHUNK:(c + 1) * CHUNK].T.astype(BF16)
            else:
                rq_ref[0, h] = rot.astype(BF16)
    p = _dot(hb, wr_ref[:, off:off + R_V_WIDTH])
    for h in range(R_HEADS):
        rv_ref[0, h] = p[:, h * R_V_DIM:(h + 1) * R_V_DIM].astype(BF16)


def _stream_specs(ctx, xs):
    hb = TM // HALO
    D = xs.shape[-1]
    off = 0 if ctx is None else 1
    last = xs.shape[1] // HALO - 1
    return [
        pl.BlockSpec((BB, TM if off else HALO, D), lambda t, b: (b, 0, 0)),
        pl.BlockSpec((BB, TM, D), lambda t, b: (b, jnp.maximum(t - off, 0), 0)),
        pl.BlockSpec((BB, HALO, D), lambda t, b: (b, jnp.maximum((t - off) * hb - 1, 0), 0)),
        pl.BlockSpec((BB, HALO, D), lambda t, b: (b, jnp.clip((t - off + 1) * hb, 0, last), 0)),
    ]


def _pre_call(ctx, xs, modsel, nw, wqk, cw, wr, wgt, gbt, tabs):
    B, _, D = xs.shape
    T = xs.shape[1] + (0 if ctx is None else ctx.shape[1])
    nt = T // TM
    nch = T // CHUNK

    def tile(t, b):
        return (b, t, 0)

    def head_tile(t, b):
        return (b, 0, t, 0)

    in_specs = _stream_specs(ctx, xs) + [
        pl.BlockSpec((BB, 1, 6, D), lambda t, b: (b, jnp.minimum(t, 1), 0, 0)),
        _const_spec(nw.shape), _const_spec(wqk.shape), _const_spec(cw.shape), _const_spec(wr.shape),
        _const_spec(wgt.shape), _const_spec(gbt.shape),
    ] + [pl.BlockSpec((TM, LANES), lambda t, b: (t, 0)) for _ in range(4)]
    out_shape = [
        jax.ShapeDtypeStruct((B, T, D), BF16),
        jax.ShapeDtypeStruct((B, M_HEADS, T, M_HEAD_DIM), BF16),
        jax.ShapeDtypeStruct((B, M_HEADS, T, M_HEAD_DIM), BF16),
        jax.ShapeDtypeStruct((B, M_HEADS, nch, M_HEAD_DIM, CHUNK), BF16),
        jax.ShapeDtypeStruct((B, 2, nch, 16, CHUNK), F32),
        jax.ShapeDtypeStruct((B, A_HEADS, T, A_HEAD_DIM), BF16),
        jax.ShapeDtypeStruct((B, A_KV_HEADS, T, A_HEAD_DIM), BF16),
        jax.ShapeDtypeStruct((B, A_KV_HEADS, T // A_BLOCK, LANES, A_BLOCK), BF16),
        jax.ShapeDtypeStruct((B, R_HEADS, T, R_QK_DIM), BF16),
        jax.ShapeDtypeStruct((B, R_HEADS, nch, R_QK_DIM, CHUNK), BF16),
        jax.ShapeDtypeStruct((B, R_HEADS, T, R_V_DIM), BF16),
    ]
    out_specs = [
        pl.BlockSpec((BB, TM, D), tile),
        pl.BlockSpec((BB, M_HEADS, TM, M_HEAD_DIM), head_tile),
        pl.BlockSpec((BB, M_HEADS, TM, M_HEAD_DIM), head_tile),
        pl.BlockSpec((BB, M_HEADS, TM // CHUNK, M_HEAD_DIM, CHUNK), lambda t, b: (b, 0, t, 0, 0)),
        pl.BlockSpec((BB, 2, TM // CHUNK, 16, CHUNK), lambda t, b: (b, 0, t, 0, 0)),
        pl.BlockSpec((BB, A_HEADS, TM, A_HEAD_DIM), head_tile),
        pl.BlockSpec((BB, A_KV_HEADS, TM, A_HEAD_DIM), head_tile),
        pl.BlockSpec((BB, A_KV_HEADS, TM // A_BLOCK, LANES, A_BLOCK), lambda t, b: (b, 0, t, 0, 0)),
        pl.BlockSpec((BB, R_HEADS, TM, R_QK_DIM), head_tile),
        pl.BlockSpec((BB, R_HEADS, TM // CHUNK, R_QK_DIM, CHUNK), lambda t, b: (b, 0, t, 0, 0)),
        pl.BlockSpec((BB, R_HEADS, TM, R_V_DIM), head_tile),
    ]
    return pl.pallas_call(
        functools.partial(_pre_kernel, split=ctx is not None),
        grid=(nt, B // BB),
        in_specs=in_specs,
        out_specs=out_specs,
        out_shape=out_shape,
        scratch_shapes=[pltpu.VMEM((BB, TM + 2 * HALO, D), BF16)],
        compiler_params=_params(("parallel", "parallel")),
        name="pre_proj",
    )(xs if ctx is None else ctx, xs, xs, xs, modsel, nw, wqk, cw, wr, wgt, gbt, *tabs)


def _bwd_chunk(i, n_ctx, n_all):
    if isinstance(i, int):
        return n_ctx - 1 - i if i < n_ctx else n_all - 1 + n_ctx - i
    return n_all - 1 + n_ctx - i


def _chunk_start(c):
    return c * CHUNK if isinstance(c, int) else pl.multiple_of(c * CHUNK, CHUNK)


def _chunk_loop(body, n_ctx, n_all, unroll):
    for i in range(n_ctx):
        body(i, 0)
    n_lat = n_all - n_ctx
    while n_lat % unroll:
        unroll //= 2
    lax.fori_loop(n_ctx, n_all, body, 0, unroll=unroll)


def _mlstm_kernel(q_ref, k_ref, vt_ref, gr_ref, o_ref, c_ref, n_ref, m_ref,
                  cs_ref, ns_ref, ms_ref, qp_ref, nq_ref, *, n_ctx):
    n_all = gr_ref.shape[2]
    dh = q_ref.shape[3]
    c_ref[...] = jnp.zeros(c_ref.shape, F32)
    n_ref[...] = jnp.zeros(n_ref.shape, F32)
    m_ref[...] = jnp.zeros(m_ref.shape, F32)

    def scan_unit(c, hh, d):
        u = 2 * hh + d
        c0 = _chunk_start(c)
        k = k_ref[0, hh, pl.ds(c0, CHUNK), :]
        vt = vt_ref[0, hh, c]
        rows = gr_ref[0, 0, c, 8 * hh:8 * hh + 8, :]
        tot = rows[4 + d:5 + d]
        aend = rows[2 * d:2 * d + 1] + tot
        amax = rows[6 + d:7 + d]
        cst = c_ref[u]
        nv = n_ref[u]
        m = m_ref[u]
        cs_ref[u, c] = cst.astype(BF16)
        ns_ref[u, c] = jnp.broadcast_to(nv, (16, dh)).astype(BF16)
        ms_ref[u, c] = jnp.broadcast_to(m, (8, CHUNK))
        m_new = jnp.maximum(tot + m, amax)
        w = jnp.exp(aend - m_new)
        dec = jnp.exp(tot + m - m_new)
        vw = (vt.astype(F32) * w).astype(BF16)
        c_ref[u] = dec * cst + _dot(vw, k)
        wk = _dot(jnp.broadcast_to(w, (16, CHUNK)).astype(BF16), k)
        n_ref[u] = dec * nv + wk[0:1]
        m_ref[u] = m_new

    def scan_step(i, carry):
        cb = _bwd_chunk(i, n_ctx, n_all)
        for hh in range(2):
            scan_unit(i, hh, 0)
            scan_unit(cb, hh, 1)
        return carry

    _chunk_loop(scan_step, n_ctx, n_all, 4)

    row = lax.broadcasted_iota(jnp.int32, (CHUNK, CHUNK), 0)
    col = lax.broadcasted_iota(jnp.int32, (CHUNK, CHUNK), 1)
    tri = (row <= col, row >= col)

    def query_products(c, slot):
        c0 = _chunk_start(c)
        for hh in range(2):
            q = q_ref[0, hh, pl.ds(c0, CHUNK), :]
            k = k_ref[0, hh, pl.ds(c0, CHUNK), :]
            qp_ref[slot, hh, 0] = _dot_nt(k, q)
            for d in range(2):
                u = 2 * hh + d
                qp_ref[slot, hh, 1 + d] = _dot_nt(cs_ref[u, c], q)
                nq_ref[slot, u] = _dot_nt(ns_ref[u, c], q)[0:8]

    def outputs(c, slot):
        c0 = _chunk_start(c)
        gcv = jnp.concatenate([gr_ref[0, 0, c], jnp.zeros((CHUNK - 16, CHUNK), F32)], axis=0).T
        for hh in range(2):
            vt = vt_ref[0, hh, c]
            rows = gr_ref[0, 0, c, 8 * hh:8 * hh + 8, :]
            st = qp_ref[slot, hh, 0]
            sds, iws, invs = [], [], []
            for d in range(2):
                u = 2 * hh + d
                a0c = gcv[:, 8 * hh + 2 * d:8 * hh + 2 * d + 1]
                b = rows[2 * d + 1:2 * d + 2]
                m = ms_ref[u, c, 0:1, :]
                dlog = jnp.where(tri[d], a0c + b, NEG_INF)
                mj = jnp.maximum(b + m, jnp.max(dlog, axis=0, keepdims=True))
                sd = st * jnp.exp(dlog - mj)
                iw = jnp.exp(b + m - mj)
                den = iw * nq_ref[slot, u, 0:1, :] + jnp.sum(sd, axis=0, keepdims=True)
                sds.append(sd.astype(BF16))
                iws.append(iw)
                invs.append(1.0 / jnp.maximum(jnp.abs(den), jnp.exp(-mj)))
            num = _dot(vt, jnp.concatenate(sds, axis=1))
            acc = None
            for d in range(2):
                ht = (num[:, d * CHUNK:(d + 1) * CHUNK] + iws[d] * qp_ref[slot, hh, 1 + d]) * invs[d]
                acc = ht if acc is None else acc + ht
            o_ref[0, hh, pl.ds(c0, CHUNK), :] = acc.T

    query_products(0, 0)

    def out_pair(i, carry):
        c = 2 * i
        query_products(c + 1, 1)
        outputs(c, 0)
        query_products(jnp.minimum(c + 2, n_all - 1), 0)
        outputs(c + 1, 1)
        return carry

    n_pairs = n_all // 2
    unroll = 4
    first = n_pairs % unroll
    for i in range(first):
        out_pair(i, 0)
    lax.fori_loop(first, n_pairs, out_pair, 0, unroll=unroll)


def _mlstm_call(mq, mk, mvt, gr, n_ctx):
    B, H, T, dh = mq.shape
    nch = T // CHUNK
    qspec = pl.BlockSpec((1, 2, T, dh), lambda b, p: (b, p, 0, 0))
    return pl.pallas_call(
        functools.partial(_mlstm_kernel, n_ctx=n_ctx),
        grid=(B, H // 2),
        in_specs=[qspec, qspec,
                  pl.BlockSpec((1, 2, nch, dh, CHUNK), lambda b, p: (b, p, 0, 0, 0)),
                  pl.BlockSpec((1, 1, nch, 16, CHUNK), lambda b, p: (b, p, 0, 0, 0))],
        out_specs=pl.BlockSpec((1, 2, T, dh), lambda b, p: (b, p, 0, 0)),
        out_shape=jax.ShapeDtypeStruct((B, H, T, dh), F32),
        scratch_shapes=[pltpu.VMEM((4, dh, dh), F32), pltpu.VMEM((4, 1, dh), F32),
                        pltpu.VMEM((4, 1, CHUNK), F32),
                        pltpu.VMEM((4, nch, dh, dh), BF16), pltpu.VMEM((4, nch, 16, dh), BF16),
                        pltpu.VMEM((4, nch, 8, CHUNK), F32),
                        pltpu.VMEM((2, 2, 3, dh, CHUNK), F32), pltpu.VMEM((2, 4, 8, CHUNK), F32)],
        compiler_params=_params(("parallel", "parallel")),
        name="mlstm",
    )(mq, mk, mvt, gr)


def _ret_kernel(q_ref, kt_ref, v_ref, lg_ref, o_ref, st_ref, sts_ref, dec_ref, ib_ref, wb_ref, cd_ref,
                *, n_ctx):
    sub = RCHUNK // CHUNK
    n_all = kt_ref.shape[2] // sub
    dk = q_ref.shape[3]
    dv = v_ref.shape[3]
    st_ref[...] = jnp.zeros(st_ref.shape, F32)

    @pl.when(pl.program_id(1) == 0)
    def _():
        row = lax.broadcasted_iota(jnp.int32, (RCHUNK, RCHUNK), 0).astype(F32)
        col = lax.broadcasted_iota(jnp.int32, (RCHUNK, RCHUNK), 1).astype(F32)
        kcol = lax.broadcasted_iota(jnp.int32, (dk, RCHUNK), 1).astype(F32)
        dec = None
        for d in range(2):
            lg = _log_sigmoid(lg_ref[0, d])[0:1]
            diff = row - col if d == 0 else col - row
            dd = jnp.where(diff >= 0, jnp.exp(lg * jnp.maximum(diff, 0.0)), 0.0)
            dec = dd if dec is None else dec + dd
            pin = row if d == 0 else RCHUNK - 1.0 - row
            ib_ref[d] = jnp.exp(lg * (pin + 1.0))
            pkey = kcol if d == 0 else RCHUNK - 1.0 - kcol
            wb_ref[d] = jnp.exp(lg * (RCHUNK - 1.0 - pkey))
            cd_ref[d] = jnp.broadcast_to(jnp.exp(lg * RCHUNK), (8, dv))
        dec_ref[...] = dec

    cds = [cd_ref[d, 0:1, :] for d in range(2)]

    def rows(c):
        r0 = c * RCHUNK
        return pl.ds(r0 if isinstance(c, int) else pl.multiple_of(r0, RCHUNK), RCHUNK)

    def k_t(c):
        return jnp.concatenate([kt_ref[0, 0, sub * c + j] for j in range(sub)], axis=1)

    def scan_step(i, carry):
        cb = _bwd_chunk(i, n_ctx, n_all)
        for d, c in ((0, i), (1, cb)):
            st = st_ref[d]
            sts_ref[c, :, d * dv:(d + 1) * dv] = st.astype(BF16)
            kw = (k_t(c).astype(F32) * wb_ref[d]).astype(BF16)
            st_ref[d] = cds[d] * st + _dot(kw, v_ref[0, 0, rows(c), :])
        return carry

    _chunk_loop(scan_step, n_ctx, n_all, 8)

    def out_step(c, carry):
        q = q_ref[0, 0, rows(c), :]
        s = (_dot(q, k_t(c)) * dec_ref[...]).astype(BF16)
        qst = _dot(q, sts_ref[c])
        o_ref[0, 0, rows(c), :] = (_dot(s, v_ref[0, 0, rows(c), :])
                                   + ib_ref[0] * qst[:, 0:dv] + ib_ref[1] * qst[:, dv:])
        return carry

    _chunk_loop(out_step, n_ctx, n_all, 8)


def _ret_call(rq, rkt, rv, lgb, n_ctx):
    B, H, T, dk = rq.shape
    dv = rv.shape[-1]
    nch = T // CHUNK
    assert dv == RCHUNK and dk <= RCHUNK and T % RCHUNK == 0
    return pl.pallas_call(
        functools.partial(_ret_kernel, n_ctx=n_ctx),
        grid=(H, B),
        in_specs=[pl.BlockSpec((1, 1, T, dk), lambda h, b: (b, h, 0, 0)),
                  pl.BlockSpec((1, 1, nch, dk, CHUNK), lambda h, b: (b, h, 0, 0, 0)),
                  pl.BlockSpec((1, 1, T, dv), lambda h, b: (b, h, 0, 0)),
                  pl.BlockSpec((1, 2, 8, dv), lambda h, b: (h, 0, 0, 0))],
        out_specs=pl.BlockSpec((1, 1, T, dv), lambda h, b: (b, h, 0, 0)),
        out_shape=jax.ShapeDtypeStruct((B, H, T, dv), F32),
        scratch_shapes=[pltpu.VMEM((2, dk, dv), F32), pltpu.VMEM((T // RCHUNK, dk, 2 * dv), BF16),
                        pltpu.VMEM((RCHUNK, RCHUNK), F32), pltpu.VMEM((2, RCHUNK, dv), F32),
                        pltpu.VMEM((2, dk, RCHUNK), F32), pltpu.VMEM((2, 8, dv), F32)],
        compiler_params=_params(("parallel", "arbitrary")),
        name="retention",
    )(rq, rkt, rv, lgb)


def _attn_kernel(q_ref, k_ref, vt_ref, sink_ref, o_ref, bias_ref, s_ref, *, n_ctx_tok, need_ctx):
    T = q_ref.shape[2]
    L = n_ctx_tok
    S = T - L
    nb = S // A_BLOCK
    nspan = 3
    span = nspan * A_BLOCK
    ncb = L // A_BLOCK
    cols = A_GROUP * A_BLOCK
    kc = k_ref[0, 0, 0:L, :]
    vtc = jnp.concatenate([vt_ref[0, 0, j] for j in range(ncb)], axis=1)
    sink = sink_ref[0]

    @pl.when((pl.program_id(0) == 0) & (pl.program_id(1) == 0))
    def _():
        kpos = lax.broadcasted_iota(jnp.int32, (span, A_BLOCK), 0)
        qpos = lax.broadcasted_iota(jnp.int32, (span, A_BLOCK), 1)
        for delta in range(nspan):
            bias_ref[delta] = jnp.where(jnp.abs(qpos + delta * A_BLOCK - kpos) <= A_WINDOW, 0.0, NEG_INF)

    def finish(blk, acc, m):
        den = acc[A_HEAD_DIM:A_HEAD_DIM + 1] + jnp.exp2(sink - m)
        ot = (acc[0:A_HEAD_DIM] * (1.0 / den)).astype(BF16)
        for g in range(A_GROUP):
            o_ref[0, blk, g * A_HEAD_DIM:(g + 1) * A_HEAD_DIM, :] = ot[:, g * A_BLOCK:(g + 1) * A_BLOCK]

    for cbk in range(ncb):
        if need_ctx:
            qs = q_ref[0, :, cbk * A_BLOCK:(cbk + 1) * A_BLOCK, :].reshape(cols, A_HEAD_DIM)
            s = _dot_nt(kc, qs)
            m = jnp.maximum(jnp.max(s, axis=0, keepdims=True), sink)
            finish(cbk, _dot(vtc, jnp.exp2(s - m).astype(BF16)), m)
        else:
            o_ref[0, cbk] = jnp.zeros(o_ref.shape[2:], BF16)

    def first_key_block(n):
        return jnp.clip(n - 1, 0, nb - nspan)

    def scores(n, slot):
        q0 = pl.multiple_of(L + n * A_BLOCK, A_BLOCK)
        kb = first_key_block(n)
        k0 = pl.multiple_of(L + kb * A_BLOCK, A_BLOCK)
        qs = q_ref[0, :, pl.ds(q0, A_BLOCK), :].reshape(cols, A_HEAD_DIM)
        bias = bias_ref[n - kb]
        s_ref[slot, 0:span, :] = (_dot_nt(k_ref[0, 0, pl.ds(k0, span), :], qs)
                                  + jnp.concatenate([bias] * A_GROUP, axis=1))
        s_ref[slot, span:, :] = _dot_nt(kc, qs)

    def softmax_pv(n, slot):
        kb = first_key_block(n)
        s = s_ref[slot]
        m = jnp.maximum(jnp.max(s, axis=0, keepdims=True), sink)
        p = jnp.exp2(s - m).astype(BF16)
        vt = jnp.concatenate([vt_ref[0, 0, ncb + kb + j] for j in range(nspan)] + [vtc], axis=1)
        finish(ncb + n, _dot(vt, p), m)

    scores(0, 0)

    def block_pair(i, carry):
        n = 2 * i
        scores(n + 1, 1)
        softmax_pv(n, 0)
        scores(jnp.minimum(n + 2, nb - 1), 0)
        softmax_pv(n + 1, 1)
        return carry

    n_pairs = nb // 2
    first = n_pairs % 2
    if first:
        block_pair(0, 0)
    lax.fori_loop(first, n_pairs, block_pair, 0, unroll=2)


def _attn_call(aq, ak, avt, sink_rows, n_ctx_tok, need_ctx):
    B, _, T, hd = aq.shape
    nblk = T // A_BLOCK
    cols = A_GROUP * A_BLOCK
    return pl.pallas_call(
        functools.partial(_attn_kernel, n_ctx_tok=n_ctx_tok, need_ctx=need_ctx),
        grid=(B, A_KV_HEADS),
        in_specs=[pl.BlockSpec((1, A_GROUP, T, hd), lambda b, h: (b, h, 0, 0)),
                  pl.BlockSpec((1, 1, T, hd), lambda b, h: (b, h, 0, 0)),
                  pl.BlockSpec((1, 1, nblk, LANES, A_BLOCK), lambda b, h: (b, h, 0, 0, 0)),
                  pl.BlockSpec((1, 1, cols), lambda b, h: (h, 0, 0))],
        out_specs=pl.BlockSpec((1, nblk, A_GROUP * hd, A_BLOCK), lambda b, h: (b, 0, h, 0)),
        out_shape=jax.ShapeDtypeStruct((B, nblk, A_WIDTH, A_BLOCK), BF16),
        scratch_shapes=[pltpu.VMEM((3, 3 * A_BLOCK, A_BLOCK), F32),
                        pltpu.VMEM((2, 3 * A_BLOCK + n_ctx_tok, cols), F32)],
        compiler_params=_params(("arbitrary", "arbitrary")),
        name="window_attn",
    )(aq, ak, avt, sink_rows)


def _head_norm(y):
    mu = jnp.mean(y, axis=-1, keepdims=True)
    yc = y - mu
    var = jnp.mean(yc * yc, axis=-1, keepdims=True)
    return yc * lax.rsqrt(var + NORM_EPS)


def _post_kernel(c_ref, x_ref, hb_ref, hm_ref, ya_ref, hr_ref, mod_ref, wg_ref, mnw_ref, rnw_ref,
                 wbm_ref, wba_ref, wbr_ref, wo_ref, o_ref, *, split):
    for i in range(x_ref.shape[0]):
        hb = hb_ref[i]
        off = 0
        hm = jnp.concatenate([_head_norm(hm_ref[i, h]) for h in range(M_HEADS)], axis=1) * mnw_ref[...]
        ym = (_sigmoid(_dot(hb, wg_ref[:, off:off + M_WIDTH])) * hm).astype(BF16)
        off += M_WIDTH
        hr = jnp.concatenate([_head_norm(hr_ref[i, h]) for h in range(R_HEADS)], axis=1) * rnw_ref[...]
        rg = _dot(hb, wg_ref[:, off:off + R_V_WIDTH])
        yr = (rg * _sigmoid(rg) * hr).astype(BF16)
        off += R_V_WIDTH
        z = _sigmoid(_dot(hb, wg_ref[:, off:off + D_MODEL])) * _dot(ym, wbm_ref[...])
        off += D_MODEL
        pa = jnp.concatenate([_dot_tn(ya_ref[i, c], wba_ref[...]) for c in range(TM // A_BLOCK)], axis=0)
        z = z + _sigmoid(_dot(hb, wg_ref[:, off:off + D_MODEL])) * pa
        off += D_MODEL
        z = z + _sigmoid(_dot(hb, wg_ref[:, off:off + D_MODEL])) * _dot(yr, wbr_ref[...])
        y = _dot(z.astype(BF16), wo_ref[...])
        one = pl.ds(i, 1)
        xres = _tile_rows(c_ref.at[one] if split else None, x_ref.at[one])
        o_ref[i] = xres + mod_ref[i, 0, 2:3, :] * y


def _post_call(ctx, xs, hb, hm, ya, hr, modsel, wg, mnw, rnw, wbm, wba, wbr, wo, t_off):
    B, T, D = hb.shape
    nt = T // TM - t_off
    split = ctx is not None
    assert not (split and t_off)

    def tile(t, b):
        return (b, t + t_off, 0)

    def head_tile(t, b):
        return (b, 0, t + t_off, 0)

    return pl.pallas_call(
        functools.partial(_post_kernel, split=split),
        grid=(nt, B // BB),
        in_specs=[pl.BlockSpec((BB, TM if split else HALO, D), lambda t, b: (b, 0, 0)),
                  pl.BlockSpec((BB, TM, D), (lambda t, b: (b, jnp.maximum(t - 1, 0), 0)) if split else tile),
                  pl.BlockSpec((BB, TM, D), tile),
                  pl.BlockSpec((BB, M_HEADS, TM, M_HEAD_DIM), head_tile),
                  pl.BlockSpec((BB, TM // A_BLOCK, A_WIDTH, A_BLOCK), lambda t, b: (b, t + t_off, 0, 0)),
                  pl.BlockSpec((BB, R_HEADS, TM, R_V_DIM), head_tile),
                  pl.BlockSpec((BB, 1, 6, D), lambda t, b: (b, jnp.minimum(t + t_off, 1), 0, 0)),
                  _const_spec(wg.shape), _const_spec(mnw.shape), _const_spec(rnw.shape),
                  _const_spec(wbm.shape), _const_spec(wba.shape), _const_spec(wbr.shape),
                  _const_spec(wo.shape)],
        out_specs=pl.BlockSpec((BB, TM, D), lambda t, b: (b, t, 0)),
        out_shape=jax.ShapeDtypeStruct((B, nt * TM, D), F32),
        compiler_params=_params(("parallel", "parallel")),
        name="post_merge",
    )(ctx if split else xs, xs, hb, hm, ya, hr, modsel, wg, mnw, rnw, wbm, wba, wbr, wo)


def _ffn_kernel(x_ref, xp_ref, xn_ref, mod_ref, nw_ref, wa_ref, wb_ref, cw_ref, cb_ref, wd_ref, fw_ref,
                o_ref, hext_ref, u_ref, *, has_ctx, final):
    t = pl.program_id(0)
    nt = pl.num_programs(0)
    first_lat = 1 if has_ctx else 0
    nw = nw_ref[...]
    ext = TM + 2 * HALO
    for i in range(x_ref.shape[0]):
        sh = mod_ref[i, 0, 3:4, :]
        sc = mod_ref[i, 0, 4:5, :]
        g2 = mod_ref[i, 0, 5:6, :]
        x = x_ref[i]
        hp = _norm_mod(xp_ref[i], nw, sc, sh)
        hn = _norm_mod(xn_ref[i], nw, sc, sh)
        hp = jnp.where(t <= first_lat, 0.0, hp)
        hn = jnp.where((t == first_lat - 1) | (t == nt - 1), 0.0, hn)
        hext_ref[i, 0:TM, :] = _norm_mod(x, nw, sc, sh).astype(BF16)
        hext_ref[i, TM:, :] = jnp.concatenate([hn, hp], axis=0).astype(BF16)
        for j in range(FFN_DIM // FFN_NC):
            cs = slice(j * FFN_NC, (j + 1) * FFN_NC)
            a = _dot(hext_ref[i], wa_ref[:, cs])
            b = _dot(hext_ref[i, 0:TM, :], wb_ref[:, cs])
            cw = cw_ref[:, cs]
            prev = pltpu.roll(a, 1, 0)[0:TM]
            nxt = pltpu.roll(a, ext - 1, 0)[0:TM]
            conv = prev * cw[0:1] + a[0:TM] * cw[1:2] + nxt * cw[2:3] + cb_ref[:, cs]
            u_ref[i, :, cs] = (conv * _sigmoid(conv) * b).astype(BF16)
        y = x + g2 * _dot(u_ref[i], wd_ref[...])
        if final:
            ms = jnp.mean(y * y, axis=-1, keepdims=True)
            y = y * lax.rsqrt(ms + NORM_EPS) * fw_ref[...]
        o_ref[i] = y


def _ffn_call(xs, modsel, nw, wa, wb, cw, cb, wd, fw, has_ctx, final):
    B, T, D = xs.shape
    nt = T // TM
    hb16 = TM // HALO
    first_lat = 1 if has_ctx else 0
    tile = pl.BlockSpec((BB, TM, D), lambda t, b: (b, t, 0))
    return pl.pallas_call(
        functools.partial(_ffn_kernel, has_ctx=has_ctx, final=final),
        grid=(nt, B // BB),
        in_specs=[tile,
                  pl.BlockSpec((BB, HALO, D), lambda t, b: (b, jnp.maximum(t * hb16 - 1, 0), 0)),
                  pl.BlockSpec((BB, HALO, D), lambda t, b: (b, jnp.minimum((t + 1) * hb16, T // HALO - 1), 0)),
                  pl.BlockSpec((BB, 1, 6, D), lambda t, b: (b, jnp.minimum(t + 1 - first_lat, 1), 0, 0)),
                  _const_spec(nw.shape), _const_spec(wa.shape), _const_spec(wb.shape),
                  _const_spec(cw.shape), _const_spec(cb.shape), _const_spec(wd.shape),
                  _const_spec(fw.shape)],
        out_specs=tile,
        out_shape=jax.ShapeDtypeStruct((B, T, D), F32),
        scratch_shapes=[pltpu.VMEM((BB, TM + 2 * HALO, D), BF16), pltpu.VMEM((BB, TM, FFN_DIM), BF16)],
        compiler_params=_params(("parallel", "parallel")),
        name="conv_ffn",
    )(xs, xs, xs, modsel, nw, wa, wb, cw, cb, wd, fw)


def _split_cols(w):
    out = []
    acc = 0
    for s in IN_SPLITS:
        out.append(w[:, acc:acc + s])
        acc += s
    return out


def _rope_tables(L, S):
    T = L + S
    j = jnp.arange(LANES)
    jj = j % 32
    inv_a = ROPE_BASE ** (-(jj % 16).astype(F32) / 16.0)
    t = jnp.arange(S)
    pos = jnp.where(((j % A_HEAD_DIM) // 32 == 0)[None, :], (t // GRID_W)[:, None], (t % GRID_W)[:, None])
    ang = pos.astype(F32) * inv_a[None, :]
    sign = jnp.where(jj < 16, -1.0, 1.0)[None, :]
    ca = jnp.concatenate([jnp.ones((L, LANES), F32), jnp.cos(ang)], axis=0)
    sa = jnp.concatenate([jnp.zeros((L, LANES), F32), jnp.sin(ang) * sign], axis=0)
    half = R_QK_DIM // 2
    inv_r = ROPE_BASE ** (-(j % half).astype(F32) / half)
    angr = jnp.arange(T).astype(F32)[:, None] * inv_r[None, :]
    signr = jnp.where(j < half, -1.0, 1.0)[None, :]
    return ca, sa, jnp.cos(angr), jnp.sin(angr) * signr


def kernel(x, c, ctx, c_ctx, mod_w, mod_b, norm1_w, norm2_w, w_in, m_gate_bias, m_conv_w, m_norm_w,
           a_sink, ret_logit, ret_norm_w, w_br_m, w_br_a, w_br_r, w_out, ffn_up, ffn_conv_w, ffn_conv_b,
           ffn_down, final_norm_w):
    B, S, D = x.shape
    L = ctx.shape[1]
    depth = mod_w.shape[0]
    assert D == D_MODEL and L == TM and S % TM == 0 and S >= 3 * A_BLOCK and B % BB == 0
    T = L + S
    n_ctx = L // CHUNK

    rows = -(-(B + 1) // 8) * 8
    cpad = jnp.zeros((rows, D), F32).at[:B].set(c).at[B].set(c_ctx)
    mods = _modulation(cpad, mod_w, mod_b)
    tabs = _rope_tables(L, S)
    stream = (ctx, x) if depth > 1 else (None, jnp.concatenate([ctx, x], axis=1))

    gperm = jnp.arange(4 * M_HEADS).reshape(4, M_HEADS).T.reshape(-1)

    for l in range(depth):
        last = l == depth - 1
        lat = mods[l, :B].reshape(B, 6, D)
        cm = jnp.broadcast_to(mods[l, B].reshape(1, 6, D), (B, 6, D))
        modsel = jnp.stack([cm, lat], axis=1)

        (w_mq, w_mk, w_mv, w_mo, w_mg, w_aq, w_ak, w_av,
         w_rq, w_rk, w_rv, w_rg, w_gm, w_ga, w_gr) = _split_cols(w_in[l])
        wqk = jnp.concatenate([w_mq, w_mk], axis=1).astype(BF16)
        wr = jnp.concatenate([w_mv, w_aq, w_ak, w_av, w_rq, w_rk, w_rv], axis=1).astype(BF16)
        wgt = w_mg[:, gperm].T.astype(BF16)
        gbt = jnp.broadcast_to(m_gate_bias[l][gperm].reshape(-1, 1), (4 * M_HEADS, LANES))

        (hb, mq, mk, mv, gr, aq, ak, av, rq, rk, rv) = _pre_call(
            *stream, modsel, norm1_w[l].reshape(1, D), wqk, m_conv_w[l], wr, wgt, gbt, tabs)

        hm = _mlstm_call(mq, mk, mv, gr, n_ctx)
        sink_rows = jnp.repeat(a_sink[l].reshape(A_KV_HEADS, A_GROUP), A_BLOCK, axis=1)[:, None, :]
        ya = _attn_call(aq, ak, av, sink_rows.astype(F32) * LOG2E, L, not last)
        lgb = jnp.broadcast_to(ret_logit[l].T[:, :, None, None], (R_HEADS, 2, 8, R_V_DIM)).astype(F32)
        hr = _ret_call(rq, rk, rv, lgb, L // RCHUNK)

        t_off = 1 if last else 0
        wgates = jnp.concatenate([w_mo, w_rg, w_gm, w_ga, w_gr], axis=1).astype(BF16)
        x1 = _post_call(*stream, hb, hm, ya, hr, modsel, wgates, m_norm_w[l].reshape(1, -1),
                        ret_norm_w[l].reshape(1, -1), w_br_m[l].astype(BF16), w_br_a[l].astype(BF16),
                        w_br_r[l].astype(BF16), w_out[l].astype(BF16), t_off)
        xc = _ffn_call(x1, modsel, norm2_w[l].reshape(1, D), ffn_up[l][:, :FFN_DIM].astype(BF16),
                       ffn_up[l][:, FFN_DIM:].astype(BF16), ffn_conv_w[l], ffn_conv_b[l].reshape(1, -1),
                       ffn_down[l].astype(BF16), final_norm_w.reshape(1, D), not last, last)
        stream = (None, xc)
    return xc
```

```python
import functools

import jax
import jax.numpy as jnp
import numpy as np
from jax import lax
from jax.experimental import pallas as pl
from jax.experimental.pallas import tpu as pltpu

F32 = jnp.float32
BF16 = jnp.bfloat16

D_MODEL = 1024
GRID_W = 64
NORM_EPS = 1e-6
ROPE_BASE = 10000.0
NEG_INF = -1e30
LOG2E = 1.4426950408889634

M_HEADS = 4
M_HEAD_DIM = 128
M_WIDTH = M_HEADS * M_HEAD_DIM
A_HEADS = 8
A_KV_HEADS = 2
A_GROUP = A_HEADS // A_KV_HEADS
A_HEAD_DIM = 64
A_WIDTH = A_HEADS * A_HEAD_DIM
A_KV_WIDTH = A_KV_HEADS * A_HEAD_DIM
A_WINDOW = 128
A_BLOCK = 128
R_HEADS = 4
R_QK_DIM = 128
R_V_DIM = 256
R_QK_WIDTH = R_HEADS * R_QK_DIM
R_V_WIDTH = R_HEADS * R_V_DIM
FFN_DIM = 2816

IN_SPLITS = (M_WIDTH, M_WIDTH, M_WIDTH, M_WIDTH, 4 * M_HEADS,
             A_WIDTH, A_KV_WIDTH, A_KV_WIDTH,
             R_QK_WIDTH, R_QK_WIDTH, R_V_WIDTH, R_V_WIDTH,
             D_MODEL, D_MODEL, D_MODEL)

TM = 256
BB = 2
HALO = 8
CHUNK = 128
RCHUNK = 256
FFN_NC = 256
LANES = 128
VMEM_LIMIT = 56 * 1024 * 1024


def _sigmoid(x):
    return 1.0 / (1.0 + jnp.exp(-x))


def _log_sigmoid(x):
    return jnp.minimum(x, 0.0) - jnp.log1p(jnp.exp(-jnp.abs(x)))


def _dot(a, b):
    return jnp.dot(a, b, preferred_element_type=F32)


def _dot_nt(a, b):
    return lax.dot_general(a, b, (((1,), (1,)), ((), ())), preferred_element_type=F32)


def _dot_tn(a, b):
    return lax.dot_general(a, b, (((0,), (0,)), ((), ())), preferred_element_type=F32)


def _const_spec(shape):
    nd = len(shape)
    return pl.BlockSpec(shape, lambda *_: (0,) * nd, pipeline_mode=pl.Buffered(1))


def _params(sem):
    return pltpu.CompilerParams(dimension_semantics=sem, vmem_limit_bytes=VMEM_LIMIT)


def _mod_kernel(c_ref, w_ref, b_ref, o_ref):
    c = c_ref[...]
    s = (c * _sigmoid(c)).astype(BF16)
    o_ref[0] = _dot(s, w_ref[0].astype(BF16)) + b_ref[0]


def _modulation(cpad, mod_w, mod_b):
    depth, d, n = mod_w.shape
    rows = cpad.shape[0]
    bn = 1024
    return pl.pallas_call(
        _mod_kernel,
        grid=(depth, n // bn),
        in_specs=[pl.BlockSpec((rows, d), lambda l, j: (0, 0)),
                  pl.BlockSpec((1, d, bn), lambda l, j: (l, 0, j)),
                  pl.BlockSpec((1, 1, bn), lambda l, j: (l, 0, j))],
        out_specs=pl.BlockSpec((1, rows, bn), lambda l, j: (l, 0, j)),
        out_shape=jax.ShapeDtypeStruct((depth, rows, n), F32),
        compiler_params=_params(("parallel", "parallel")),
        name="modulation",
    )(cpad, mod_w, mod_b.reshape(depth, 1, n))


def _norm_mod(xv, nw, sc, sh):
    ms = jnp.mean(xv * xv, axis=-1, keepdims=True)
    return (xv * lax.rsqrt(ms + NORM_EPS) * nw) * (1.0 + sc) + sh


def _seg_cumsum(x, axis, reverse):
    n = x.shape[axis]
    idx = lax.broadcasted_iota(jnp.int32, x.shape, axis)
    s = 1
    while s < n:
        if reverse:
            shifted = pltpu.roll(x, n - s, axis)
            x = x + jnp.where(idx < n - s, shifted, 0.0)
        else:
            shifted = pltpu.roll(x, s, axis)
            x = x + jnp.where(idx >= s, shifted, 0.0)
        s *= 2
    return x


def _gate_transform(raw, gate_axis):
    tok_axis = 1 - gate_axis
    k = lax.broadcasted_iota(jnp.int32, raw.shape, gate_axis) % 4
    lf = _log_sigmoid(raw)
    cum_f = _seg_cumsum(jnp.where(k == 1, lf, 0.0), tok_axis, False)
    cum_b = _seg_cumsum(jnp.where(k == 3, lf, 0.0), tok_axis, True)
    bsel = jnp.where(k == 1, cum_f, jnp.where(k == 3, cum_b, 0.0))
    n = raw.shape[gate_axis]
    bnext = pltpu.roll(bsel, n - 1, gate_axis)
    nt = raw.shape[tok_axis]
    tot_f = lax.slice_in_dim(cum_f, nt - 1, nt, axis=tok_axis)
    tot_b = lax.slice_in_dim(cum_b, 0, 1, axis=tok_axis)
    total = jnp.broadcast_to(tot_f + tot_b, raw.shape)
    return jnp.where(k % 2 == 1, bsel, raw - bnext), total


def _pre_kernel(c_ref, x_ref, xp_ref, xn_ref, mod_ref, nw_ref, wqk_ref, cw_ref, wr_ref, wgt_ref,
                gbt_ref, ca_ref, sa_ref, cr_ref, sr_ref, *out_and_scratch, split):
    for i in range(x_ref.shape[0]):
        one = pl.ds(i, 1)
        _pre_tile(c_ref.at[one] if split else None,
                  x_ref.at[one], xp_ref.at[one], xn_ref.at[one], mod_ref.at[one], nw_ref, wqk_ref, cw_ref,
                  wr_ref, wgt_ref, gbt_ref, ca_ref, sa_ref, cr_ref, sr_ref,
                  *[r.at[one] for r in out_and_scratch[:-1]], out_and_scratch[-1].at[i])


def _tile_rows(c_ref, x_ref):
    if c_ref is None:
        return x_ref[0]
    return jnp.where(pl.program_id(0) == 0, c_ref[0], x_ref[0])


def _pre_tile(c_ref, x_ref, xp_ref, xn_ref, mod_ref, nw_ref, wqk_ref, cw_ref, wr_ref, wgt_ref,
              gbt_ref, ca_ref, sa_ref, cr_ref, sr_ref,
              hb_ref, mq_ref, mk_ref, mv_ref, gr_ref, aq_ref, ak_ref, av_ref,
              rq_ref, rk_ref, rv_ref, hext_ref):
    t = pl.program_id(0)
    nt = pl.num_programs(0)
    sh = mod_ref[0, 0, 0:1, :]
    sc = mod_ref[0, 0, 1:2, :]
    nw = nw_ref[...]
    hb = _norm_mod(_tile_rows(c_ref, x_ref), nw, sc, sh).astype(BF16)
    hb_ref[0] = hb

    grow = _dot_nt(wgt_ref[...], hb) + gbt_ref[:, 0:1]
    for c in range(TM // CHUNK):
        gr, tot = _gate_transform(grow[:, c * CHUNK:(c + 1) * CHUNK], 0)
        aend = gr + pltpu.roll(tot, 4 * M_HEADS - 1, 0)
        amax = jnp.broadcast_to(jnp.max(aend, axis=1, keepdims=True), aend.shape)
        for h in range(M_HEADS):
            hp_, o0 = h // 2, 8 * (h % 2)
            gr_ref[0, hp_, c, o0:o0 + 4, :] = gr[4 * h:4 * h + 4]
            for d in range(2):
                gr_ref[0, hp_, c, o0 + 4 + d:o0 + 5 + d, :] = tot[4 * h + 2 * d + 1:4 * h + 2 * d + 2]
                gr_ref[0, hp_, c, o0 + 6 + d:o0 + 7 + d, :] = amax[4 * h + 2 * d:4 * h + 2 * d + 1]

    hp = _norm_mod(xp_ref[0], nw, sc, sh)
    hn = _norm_mod(xn_ref[0], nw, sc, sh)
    hp = jnp.where(t <= 1, 0.0, hp)
    hn = jnp.where((t == 0) | (t == nt - 1), 0.0, hn)
    hext_ref[0:TM, :] = hb
    hext_ref[TM:, :] = jnp.concatenate([hn, hp], axis=0).astype(BF16)
    hext = hext_ref[...]
    ext = TM + 2 * HALO

    for j, (dst, scale) in enumerate(((mq_ref, 1.0), (mk_ref, M_HEAD_DIM ** -0.5))):
        p = _dot(hext, wqk_ref[:, j * M_WIDTH:(j + 1) * M_WIDTH])
        cw = cw_ref[:, j * M_WIDTH:(j + 1) * M_WIDTH]
        prev = pltpu.roll(p, 1, 0)[0:TM]
        nxt = pltpu.roll(p, ext - 1, 0)[0:TM]
        conv = prev * cw[0:1] + p[0:TM] * cw[1:2] + nxt * cw[2:3]
        act = conv * _sigmoid(conv) * scale
        for h in range(M_HEADS):
            dst[0, h] = act[:, h * M_HEAD_DIM:(h + 1) * M_HEAD_DIM].astype(BF16)

    off = 0
    p = _dot(hb, wr_ref[:, off:off + M_WIDTH])
    off += M_WIDTH
    for h in range(M_HEADS):
        for c in range(TM // CHUNK):
            blk = p[c * CHUNK:(c + 1) * CHUNK, h * M_HEAD_DIM:(h + 1) * M_HEAD_DIM]
            mv_ref[0, h, c] = blk.T.astype(BF16)

    lane = lax.broadcasted_iota(jnp.int32, (TM, LANES), 1)
    first16 = (lane % 32) < 16
    ca = ca_ref[...]
    sa = sa_ref[...]

    def rope_a(xs):
        partner = jnp.where(first16, pltpu.roll(xs, LANES - 16, 1), pltpu.roll(xs, 16, 1))
        return xs * ca + partner * sa

    p = _dot(hb, wr_ref[:, off:off + A_WIDTH])
    off += A_WIDTH
    for s in range(A_WIDTH // LANES):
        r = (rope_a(p[:, s * LANES:(s + 1) * LANES]) * (A_HEAD_DIM ** -0.5 * LOG2E)).astype(BF16)
        aq_ref[0, 2 * s] = r[:, 0:A_HEAD_DIM]
        aq_ref[0, 2 * s + 1] = r[:, A_HEAD_DIM:]
    p = _dot(hb, wr_ref[:, off:off + 2 * A_KV_WIDTH])
    off += 2 * A_KV_WIDTH
    r = rope_a(p[:, 0:LANES]).astype(BF16)
    for h in range(A_KV_HEADS):
        ak_ref[0, h] = r[:, h * A_HEAD_DIM:(h + 1) * A_HEAD_DIM]
    pv = p[:, LANES:]
    for h in range(A_KV_HEADS):
        vh = pv if h == 0 else pltpu.roll(pv, A_HEAD_DIM, 1)
        ve = jnp.where(lane < A_HEAD_DIM, vh, 1.0)
        for c in range(TM // A_BLOCK):
            av_ref[0, h, c] = ve[c * A_BLOCK:(c + 1) * A_BLOCK].T.astype(BF16)

    cr = cr_ref[...]
    sr = sr_ref[...]
    for is_k in (False, True):
        p = _dot(hb, wr_ref[:, off:off + R_QK_WIDTH])
        off += R_QK_WIDTH
        for h in range(R_HEADS):
            xs = p[:, h * R_QK_DIM:(h + 1) * R_QK_DIM]
            rot = xs * cr + pltpu.roll(xs, R_QK_DIM // 2, 1) * sr
            if is_k:
                rot = rot * R_QK_DIM ** -0.5
                for c in range(TM // CHUNK):
                    rk_ref[0, h, c] = rot[c * CHUNK:(c + 1) * CHUNK].T.astype(BF16)
            else:
                rq_ref[0, h] = rot.astype(BF16)
    p = _dot(hb, wr_ref[:, off:off + R_V_WIDTH])
    for h in range(R_HEADS):
        rv_ref[0, h] = p[:, h * R_V_DIM:(h + 1) * R_V_DIM].astype(BF16)


def _stream_specs(ctx, xs):
    hb = TM // HALO
    D = xs.shape[-1]
    off = 0 if ctx is None else 1
    last = xs.shape[1] // HALO - 1
    return [
        pl.BlockSpec((BB, TM if off else HALO, D), lambda t, b: (b, 0, 0)),
        pl.BlockSpec((BB, TM, D), lambda t, b: (b, jnp.maximum(t - off, 0), 0)),
        pl.BlockSpec((BB, HALO, D), lambda t, b: (b, jnp.maximum((t - off) * hb - 1, 0), 0)),
        pl.BlockSpec((BB, HALO, D), lambda t, b: (b, jnp.clip((t - off + 1) * hb, 0, last), 0)),
    ]


def _pre_call(ctx, xs, modsel, nw, wqk, cw, wr, wgt, gbt, tabs):
    B, _, D = xs.shape
    T = xs.shape[1] + (0 if ctx is None else ctx.shape[1])
    nt = T // TM
    nch = T // CHUNK

    def tile(t, b):
        return (b, t, 0)

    def head_tile(t, b):
        return (b, 0, t, 0)

    in_specs = _stream_specs(ctx, xs) + [
        pl.BlockSpec((BB, 1, 6, D), lambda t, b: (b, jnp.minimum(t, 1), 0, 0)),
        _const_spec(nw.shape), _const_spec(wqk.shape), _const_spec(cw.shape), _const_spec(wr.shape),
        _const_spec(wgt.shape), _const_spec(gbt.shape),
    ] + [pl.BlockSpec((TM, LANES), lambda t, b: (t, 0)) for _ in range(4)]
    out_shape = [
        jax.ShapeDtypeStruct((B, T, D), BF16),
        jax.ShapeDtypeStruct((B, M_HEADS, T, M_HEAD_DIM), BF16),
        jax.ShapeDtypeStruct((B, M_HEADS, T, M_HEAD_DIM), BF16),
        jax.ShapeDtypeStruct((B, M_HEADS, nch, M_HEAD_DIM, CHUNK), BF16),
        jax.ShapeDtypeStruct((B, 2, nch, 16, CHUNK), F32),
        jax.ShapeDtypeStruct((B, A_HEADS, T, A_HEAD_DIM), BF16),
        jax.ShapeDtypeStruct((B, A_KV_HEADS, T, A_HEAD_DIM), BF16),
        jax.ShapeDtypeStruct((B, A_KV_HEADS, T // A_BLOCK, LANES, A_BLOCK), BF16),
        jax.ShapeDtypeStruct((B, R_HEADS, T, R_QK_DIM), BF16),
        jax.ShapeDtypeStruct((B, R_HEADS, nch, R_QK_DIM, CHUNK), BF16),
        jax.ShapeDtypeStruct((B, R_HEADS, T, R_V_DIM), BF16),
    ]
    out_specs = [
        pl.BlockSpec((BB, TM, D), tile),
        pl.BlockSpec((BB, M_HEADS, TM, M_HEAD_DIM), head_tile),
        pl.BlockSpec((BB, M_HEADS, TM, M_HEAD_DIM), head_tile),
        pl.BlockSpec((BB, M_HEADS, TM // CHUNK, M_HEAD_DIM, CHUNK), lambda t, b: (b, 0, t, 0, 0)),
        pl.BlockSpec((BB, 2, TM // CHUNK, 16, CHUNK), lambda t, b: (b, 0, t, 0, 0)),
        pl.BlockSpec((BB, A_HEADS, TM, A_HEAD_DIM), head_tile),
        pl.BlockSpec((BB, A_KV_HEADS, TM, A_HEAD_DIM), head_tile),
        pl.BlockSpec((BB, A_KV_HEADS, TM // A_BLOCK, LANES, A_BLOCK), lambda t, b: (b, 0, t, 0, 0)),
        pl.BlockSpec((BB, R_HEADS, TM, R_QK_DIM), head_tile),
        pl.BlockSpec((BB, R_HEADS, TM // CHUNK, R_QK_DIM, CHUNK), lambda t, b: (b, 0, t, 0, 0)),
        pl.BlockSpec((BB, R_HEADS, TM, R_V_DIM), head_tile),
    ]
    return pl.pallas_call(
        functools.partial(_pre_kernel, split=ctx is not None),
        grid=(nt, B // BB),
        in_specs=in_specs,
        out_specs=out_specs,
        out_shape=out_shape,
        scratch_shapes=[pltpu.VMEM((BB, TM + 2 * HALO, D), BF16)],
        compiler_params=_params(("parallel", "parallel")),
        name="pre_proj",
    )(xs if ctx is None else ctx, xs, xs, xs, modsel, nw, wqk, cw, wr, wgt, gbt, *tabs)


def _bwd_chunk(i, n_ctx, n_all):
    if isinstance(i, int):
        return n_ctx - 1 - i if i < n_ctx else n_all - 1 + n_ctx - i
    return n_all - 1 + n_ctx - i


def _chunk_start(c):
    return c * CHUNK if isinstance(c, int) else pl.multiple_of(c * CHUNK, CHUNK)


def _chunk_loop(body, n_ctx, n_all, unroll):
    for i in range(n_ctx):
        body(i, 0)
    n_lat = n_all - n_ctx
    while n_lat % unroll:
        unroll //= 2
    lax.fori_loop(n_ctx, n_all, body, 0, unroll=unroll)


def _mlstm_kernel(q_ref, k_ref, vt_ref, gr_ref, o_ref, c_ref, n_ref, m_ref,
                  cs_ref, ns_ref, ms_ref, qp_ref, nq_ref, *, n_ctx):
    n_all = gr_ref.shape[2]
    dh = q_ref.shape[3]
    c_ref[...] = jnp.zeros(c_ref.shape, F32)
    n_ref[...] = jnp.zeros(n_ref.shape, F32)
    m_ref[...] = jnp.zeros(m_ref.shape, F32)

    def scan_unit(c, hh, d):
        u = 2 * hh + d
        c0 = _chunk_start(c)
        k = k_ref[0, hh, pl.ds(c0, CHUNK), :]
        vt = vt_ref[0, hh, c]
        rows = gr_ref[0, 0, c, 8 * hh:8 * hh + 8, :]
        tot = rows[4 + d:5 + d]
        aend = rows[2 * d:2 * d + 1] + tot
        amax = rows[6 + d:7 + d]
        cst = c_ref[u]
        nv = n_ref[u]
        m = m_ref[u]
        cs_ref[u, c] = cst.astype(BF16)
        ns_ref[u, c] = jnp.broadcast_to(nv, (16, dh)).astype(BF16)
        ms_ref[u, c] = jnp.broadcast_to(m, (8, CHUNK))
        m_new = jnp.maximum(tot + m, amax)
        w = jnp.exp(aend - m_new)
        dec = jnp.exp(tot + m - m_new)
        vw = (vt.astype(F32) * w).astype(BF16)
        c_ref[u] = dec * cst + _dot(vw, k)
        wk = _dot(jnp.broadcast_to(w, (16, CHUNK)).astype(BF16), k)
        n_ref[u] = dec * nv + wk[0:1]
        m_ref[u] = m_new

    def scan_step(i, carry):
        cb = _bwd_chunk(i, n_ctx, n_all)
        for hh in range(2):
            scan_unit(i, hh, 0)
            scan_unit(cb, hh, 1)
        return carry

    _chunk_loop(scan_step, n_ctx, n_all, 4)

    row = lax.broadcasted_iota(jnp.int32, (CHUNK, CHUNK), 0)
    col = lax.broadcasted_iota(jnp.int32, (CHUNK, CHUNK), 1)
    tri = (row <= col, row >= col)

    def query_products(c, slot):
        c0 = _chunk_start(c)
        for hh in range(2):
            q = q_ref[0, hh, pl.ds(c0, CHUNK), :]
            k = k_ref[0, hh, pl.ds(c0, CHUNK), :]
            qp_ref[slot, hh, 0] = _dot_nt(k, q)
            for d in range(2):
                u = 2 * hh + d
                qp_ref[slot, hh, 1 + d] = _dot_nt(cs_ref[u, c], q)
                nq_ref[slot, u] = _dot_nt(ns_ref[u, c], q)[0:8]

    def outputs(c, slot):
        c0 = _chunk_start(c)
        gcv = jnp.concatenate([gr_ref[0, 0, c], jnp.zeros((CHUNK - 16, CHUNK), F32)], axis=0).T
        for hh in range(2):
            vt = vt_ref[0, hh, c]
            rows = gr_ref[0, 0, c, 8 * hh:8 * hh + 8, :]
            st = qp_ref[slot, hh, 0]
            sds, iws, invs = [], [], []
            for d in range(2):
                u = 2 * hh + d
                a0c = gcv[:, 8 * hh + 2 * d:8 * hh + 2 * d + 1]
                b = rows[2 * d + 1:2 * d + 2]
                m = ms_ref[u, c, 0:1, :]
                dlog = jnp.where(tri[d], a0c + b, NEG_INF)
                mj = jnp.maximum(b + m, jnp.max(dlog, axis=0, keepdims=True))
                sd = st * jnp.exp(dlog - mj)
                iw = jnp.exp(b + m - mj)
                den = iw * nq_ref[slot, u, 0:1, :] + jnp.sum(sd, axis=0, keepdims=True)
                sds.append(sd.astype(BF16))
                iws.append(iw)
                invs.append(1.0 / jnp.maximum(jnp.abs(den), jnp.exp(-mj)))
            num = _dot(vt, jnp.concatenate(sds, axis=1))
            acc = None
            for d in range(2):
                ht = (num[:, d * CHUNK:(d + 1) * CHUNK] + iws[d] * qp_ref[slot, hh, 1 + d]) * invs[d]
                acc = ht if acc is None else acc + ht
            o_ref[0, hh, pl.ds(c0, CHUNK), :] = acc.T

    query_products(0, 0)

    def out_pair(i, carry):
        c = 2 * i
        query_products(c + 1, 1)
        outputs(c, 0)
        query_products(jnp.minimum(c + 2, n_all - 1), 0)
        outputs(c + 1, 1)
        return carry

    n_pairs = n_all // 2
    unroll = 4
    first = n_pairs % unroll
    for i in range(first):
        out_pair(i, 0)
    lax.fori_loop(first, n_pairs, out_pair, 0, unroll=unroll)


def _mlstm_call(mq, mk, mvt, gr, n_ctx):
    B, H, T, dh = mq.shape
    nch = T // CHUNK
    qspec = pl.BlockSpec((1, 2, T, dh), lambda b, p: (b, p, 0, 0))
    return pl.pallas_call(
        functools.partial(_mlstm_kernel, n_ctx=n_ctx),
        grid=(B, H // 2),
        in_specs=[qspec, qspec,
                  pl.BlockSpec((1, 2, nch, dh, CHUNK), lambda b, p: (b, p, 0, 0, 0)),
                  pl.BlockSpec((1, 1, nch, 16, CHUNK), lambda b, p: (b, p, 0, 0, 0))],
        out_specs=pl.BlockSpec((1, 2, T, dh), lambda b, p: (b, p, 0, 0)),
        out_shape=jax.ShapeDtypeStruct((B, H, T, dh), F32),
        scratch_shapes=[pltpu.VMEM((4, dh, dh), F32), pltpu.VMEM((4, 1, dh), F32),
                        pltpu.VMEM((4, 1, CHUNK), F32),
                        pltpu.VMEM((4, nch, dh, dh), BF16), pltpu.VMEM((4, nch, 16, dh), BF16),
                        pltpu.VMEM((4, nch, 8, CHUNK), F32),
                        pltpu.VMEM((2, 2, 3, dh, CHUNK), F32), pltpu.VMEM((2, 4, 8, CHUNK), F32)],
        compiler_params=_params(("parallel", "parallel")),
        name="mlstm",
    )(mq, mk, mvt, gr)


def _ret_kernel(q_ref, kt_ref, v_ref, lg_ref, o_ref, st_ref, sts_ref, dec_ref, ib_ref, wb_ref, cd_ref,
                *, n_ctx):
    sub = RCHUNK // CHUNK
    n_all = kt_ref.shape[2] // sub
    dk = q_ref.shape[3]
    dv = v_ref.shape[3]
    st_ref[...] = jnp.zeros(st_ref.shape, F32)

    @pl.when(pl.program_id(1) == 0)
    def _():
        row = lax.broadcasted_iota(jnp.int32, (RCHUNK, RCHUNK), 0).astype(F32)
        col = lax.broadcasted_iota(jnp.int32, (RCHUNK, RCHUNK), 1).astype(F32)
        kcol = lax.broadcasted_iota(jnp.int32, (dk, RCHUNK), 1).astype(F32)
        dec = None
        for d in range(2):
            lg = _log_sigmoid(lg_ref[0, d])[0:1]
            diff = row - col if d == 0 else col - row
            dd = jnp.where(diff >= 0, jnp.exp(lg * jnp.maximum(diff, 0.0)), 0.0)
            dec = dd if dec is None else dec + dd
            pin = row if d == 0 else RCHUNK - 1.0 - row
            ib_ref[d] = jnp.exp(lg * (pin + 1.0))
            pkey = kcol if d == 0 else RCHUNK - 1.0 - kcol
            wb_ref[d] = jnp.exp(lg * (RCHUNK - 1.0 - pkey))
            cd_ref[d] = jnp.broadcast_to(jnp.exp(lg * RCHUNK), (8, dv))
        dec_ref[...] = dec

    cds = [cd_ref[d, 0:1, :] for d in range(2)]

    def rows(c):
        r0 = c * RCHUNK
        return pl.ds(r0 if isinstance(c, int) else pl.multiple_of(r0, RCHUNK), RCHUNK)

    def k_t(c):
        return jnp.concatenate([kt_ref[0, 0, sub * c + j] for j in range(sub)], axis=1)

    def scan_step(i, carry):
        cb = _bwd_chunk(i, n_ctx, n_all)
        for d, c in ((0, i), (1, cb)):
            st = st_ref[d]
            sts_ref[c, :, d * dv:(d + 1) * dv] = st.astype(BF16)
            kw = (k_t(c).astype(F32) * wb_ref[d]).astype(BF16)
            st_ref[d] = cds[d] * st + _dot(kw, v_ref[0, 0, rows(c), :])
        return carry

    _chunk_loop(scan_step, n_ctx, n_all, 8)

    def out_step(c, carry):
        q = q_ref[0, 0, rows(c), :]
        s = (_dot(q, k_t(c)) * dec_ref[...]).astype(BF16)
        qst = _dot(q, sts_ref[c])
        o_ref[0, 0, rows(c), :] = (_dot(s, v_ref[0, 0, rows(c), :])
                                   + ib_ref[0] * qst[:, 0:dv] + ib_ref[1] * qst[:, dv:])
        return carry

    _chunk_loop(out_step, n_ctx, n_all, 8)


def _ret_call(rq, rkt, rv, lgb, n_ctx):
    B, H, T, dk = rq.shape
    dv = rv.shape[-1]
    nch = T // CHUNK
    assert dv == RCHUNK and dk <= RCHUNK and T % RCHUNK == 0
    return pl.pallas_call(
        functools.partial(_ret_kernel, n_ctx=n_ctx),
        grid=(H, B),
        in_specs=[pl.BlockSpec((1, 1, T, dk), lambda h, b: (b, h, 0, 0)),
                  pl.BlockSpec((1, 1, nch, dk, CHUNK), lambda h, b: (b, h, 0, 0, 0)),
                  pl.BlockSpec((1, 1, T, dv), lambda h, b: (b, h, 0, 0)),
                  pl.BlockSpec((1, 2, 8, dv), lambda h, b: (h, 0, 0, 0))],
        out_specs=pl.BlockSpec((1, 1, T, dv), lambda h, b: (b, h, 0, 0)),
        out_shape=jax.ShapeDtypeStruct((B, H, T, dv), F32),
        scratch_shapes=[pltpu.VMEM((2, dk, dv), F32), pltpu.VMEM((T // RCHUNK, dk, 2 * dv), BF16),
                        pltpu.VMEM((RCHUNK, RCHUNK), F32), pltpu.VMEM((2, RCHUNK, dv), F32),
                        pltpu.VMEM((2, dk, RCHUNK), F32), pltpu.VMEM((2, 8, dv), F32)],
        compiler_params=_params(("parallel", "arbitrary")),
        name="retention",
    )(rq, rkt, rv, lgb)


def _attn_kernel(q_ref, k_ref, vt_ref, sink_ref, o_ref, bias_ref, s_ref, *, n_ctx_tok, need_ctx):
    T = q_ref.shape[2]
    L = n_ctx_tok
    S = T - L
    nb = S // A_BLOCK
    nspan = 3
    span = nspan * A_BLOCK
    ncb = L // A_BLOCK
    cols = A_GROUP * A_BLOCK
    kc = k_ref[0, 0, 0:L, :]
    vtc = jnp.concatenate([vt_ref[0, 0, j] for j in range(ncb)], axis=1)
    sink = sink_ref[0]

    @pl.when((pl.program_id(0) == 0) & (pl.program_id(1) == 0))
    def _():
        kpos = lax.broadcasted_iota(jnp.int32, (span, A_BLOCK), 0)
        qpos = lax.broadcasted_iota(jnp.int32, (span, A_BLOCK), 1)
        for delta in range(nspan):
            bias_ref[delta] = jnp.where(jnp.abs(qpos + delta * A_BLOCK - kpos) <= A_WINDOW, 0.0, NEG_INF)

    def finish(blk, acc, m):
        den = acc[A_HEAD_DIM:A_HEAD_DIM + 1] + jnp.exp2(sink - m)
        ot = (acc[0:A_HEAD_DIM] * (1.0 / den)).astype(BF16)
        for g in range(A_GROUP):
            o_ref[0, blk, g * A_HEAD_DIM:(g + 1) * A_HEAD_DIM, :] = ot[:, g * A_BLOCK:(g + 1) * A_BLOCK]

    for cbk in range(ncb):
        if need_ctx:
            qs = q_ref[0, :, cbk * A_BLOCK:(cbk + 1) * A_BLOCK, :].reshape(cols, A_HEAD_DIM)
            s = _dot_nt(kc, qs)
            m = jnp.maximum(jnp.max(s, axis=0, keepdims=True), sink)
            finish(cbk, _dot(vtc, jnp.exp2(s - m).astype(BF16)), m)
        else:
            o_ref[0, cbk] = jnp.zeros(o_ref.shape[2:], BF16)

    def first_key_block(n):
        return jnp.clip(n - 1, 0, nb - nspan)

    def scores(n, slot):
        q0 = pl.multiple_of(L + n * A_BLOCK, A_BLOCK)
        kb = first_key_block(n)
        k0 = pl.multiple_of(L + kb * A_BLOCK, A_BLOCK)
        qs = q_ref[0, :, pl.ds(q0, A_BLOCK), :].reshape(cols, A_HEAD_DIM)
        bias = bias_ref[n - kb]
        s_ref[slot, 0:span, :] = (_dot_nt(k_ref[0, 0, pl.ds(k0, span), :], qs)
                                  + jnp.concatenate([bias] * A_GROUP, axis=1))
        s_ref[slot, span:, :] = _dot_nt(kc, qs)

    def softmax_pv(n, slot):
        kb = first_key_block(n)
        s = s_ref[slot]
        m = jnp.maximum(jnp.max(s, axis=0, keepdims=True), sink)
        p = jnp.exp2(s - m).astype(BF16)
        vt = jnp.concatenate([vt_ref[0, 0, ncb + kb + j] for j in range(nspan)] + [vtc], axis=1)
        finish(ncb + n, _dot(vt, p), m)

    scores(0, 0)

    def block_pair(i, carry):
        n = 2 * i
        scores(n + 1, 1)
        softmax_pv(n, 0)
        scores(jnp.minimum(n + 2, nb - 1), 0)
        softmax_pv(n + 1, 1)
        return carry

    n_pairs = nb // 2
    first = n_pairs % 2
    if first:
        block_pair(0, 0)
    lax.fori_loop(first, n_pairs, block_pair, 0, unroll=2)


def _attn_call(aq, ak, avt, sink_rows, n_ctx_tok, need_ctx):
    B, _, T, hd = aq.shape
    nblk = T // A_BLOCK
    cols = A_GROUP * A_BLOCK
    return pl.pallas_call(
        functools.partial(_attn_kernel, n_ctx_tok=n_ctx_tok, need_ctx=need_ctx),
        grid=(B, A_KV_HEADS),
        in_specs=[pl.BlockSpec((1, A_GROUP, T, hd), lambda b, h: (b, h, 0, 0)),
                  pl.BlockSpec((1, 1, T, hd), lambda b, h: (b, h, 0, 0)),
                  pl.BlockSpec((1, 1, nblk, LANES, A_BLOCK), lambda b, h: (b, h, 0, 0, 0)),
                  pl.BlockSpec((1, 1, cols), lambda b, h: (h, 0, 0))],
        out_specs=pl.BlockSpec((1, nblk, A_GROUP * hd, A_BLOCK), lambda b, h: (b, 0, h, 0)),
        out_shape=jax.ShapeDtypeStruct((B, nblk, A_WIDTH, A_BLOCK), BF16),
        scratch_shapes=[pltpu.VMEM((3, 3 * A_BLOCK, A_BLOCK), F32),
                        pltpu.VMEM((2, 3 * A_BLOCK + n_ctx_tok, cols), F32)],
        compiler_params=_params(("arbitrary", "arbitrary")),
        name="window_attn",
    )(aq, ak, avt, sink_rows)


def _head_norm(y):
    mu = jnp.mean(y, axis=-1, keepdims=True)
    yc = y - mu
    var = jnp.mean(yc * yc, axis=-1, keepdims=True)
    return yc * lax.rsqrt(var + NORM_EPS)


def _post_kernel(c_ref, x_ref, hb_ref, hm_ref, ya_ref, hr_ref, mod_ref, wg_ref, mnw_ref, rnw_ref,
                 wbm_ref, wba_ref, wbr_ref, wo_ref, o_ref, *, split):
    for i in range(x_ref.shape[0]):
        hb = hb_ref[i]
        off = 0
        hm = jnp.concatenate([_head_norm(hm_ref[i, h]) for h in range(M_HEADS)], axis=1) * mnw_ref[...]
        ym = (_sigmoid(_dot(hb, wg_ref[:, off:off + M_WIDTH])) * hm).astype(BF16)
        off += M_WIDTH
        hr = jnp.concatenate([_head_norm(hr_ref[i, h]) for h in range(R_HEADS)], axis=1) * rnw_ref[...]
        rg = _dot(hb, wg_ref[:, off:off + R_V_WIDTH])
        yr = (rg * _sigmoid(rg) * hr).astype(BF16)
        off += R_V_WIDTH
        z = _sigmoid(_dot(hb, wg_ref[:, off:off + D_MODEL])) * _dot(ym, wbm_ref[...])
        off += D_MODEL
        pa = jnp.concatenate([_dot_tn(ya_ref[i, c], wba_ref[...]) for c in range(TM // A_BLOCK)], axis=0)
        z = z + _sigmoid(_dot(hb, wg_ref[:, off:off + D_MODEL])) * pa
        off += D_MODEL
        z = z + _sigmoid(_dot(hb, wg_ref[:, off:off + D_MODEL])) * _dot(yr, wbr_ref[...])
        y = _dot(z.astype(BF16), wo_ref[...])
        one = pl.ds(i, 1)
        xres = _tile_rows(c_ref.at[one] if split else None, x_ref.at[one])
        o_ref[i] = xres + mod_ref[i, 0, 2:3, :] * y


def _post_call(ctx, xs, hb, hm, ya, hr, modsel, wg, mnw, rnw, wbm, wba, wbr, wo, t_off):
    B, T, D = hb.shape
    nt = T // TM - t_off
    split = ctx is not None
    assert not (split and t_off)

    def tile(t, b):
        return (b, t + t_off, 0)

    def head_tile(t, b):
        return (b, 0, t + t_off, 0)

    return pl.pallas_call(
        functools.partial(_post_kernel, split=split),
        grid=(nt, B // BB),
        in_specs=[pl.BlockSpec((BB, TM if split else HALO, D), lambda t, b: (b, 0, 0)),
                  pl.BlockSpec((BB, TM, D), (lambda t, b: (b, jnp.maximum(t - 1, 0), 0)) if split else tile),
                  pl.BlockSpec((BB, TM, D), tile),
                  pl.BlockSpec((BB, M_HEADS, TM, M_HEAD_DIM), head_tile),
                  pl.BlockSpec((BB, TM // A_BLOCK, A_WIDTH, A_BLOCK), lambda t, b: (b, t + t_off, 0, 0)),
                  pl.BlockSpec((BB, R_HEADS, TM, R_V_DIM), head_tile),
                  pl.BlockSpec((BB, 1, 6, D), lambda t, b: (b, jnp.minimum(t + t_off, 1), 0, 0)),
                  _const_spec(wg.shape), _const_spec(mnw.shape), _const_spec(rnw.shape),
                  _const_spec(wbm.shape), _const_spec(wba.shape), _const_spec(wbr.shape),
                  _const_spec(wo.shape)],
        out_specs=pl.BlockSpec((BB, TM, D), lambda t, b: (b, t, 0)),
        out_shape=jax.ShapeDtypeStruct((B, nt * TM, D), F32),
        compiler_params=_params(("parallel", "parallel")),
        name="post_merge",
    )(ctx if split else xs, xs, hb, hm, ya, hr, modsel, wg, mnw, rnw, wbm, wba, wbr, wo)


def _ffn_kernel(x_ref, xp_ref, xn_ref, mod_ref, nw_ref, wa_ref, wb_ref, cw_ref, cb_ref, wd_ref, fw_ref,
                o_ref, hext_ref, u_ref, *, has_ctx, final):
    t = pl.program_id(0)
    nt = pl.num_programs(0)
    first_lat = 1 if has_ctx else 0
    nw = nw_ref[...]
    ext = TM + 2 * HALO
    for i in range(x_ref.shape[0]):
        sh = mod_ref[i, 0, 3:4, :]
        sc = mod_ref[i, 0, 4:5, :]
        g2 = mod_ref[i, 0, 5:6, :]
        x = x_ref[i]
        hp = _norm_mod(xp_ref[i], nw, sc, sh)
        hn = _norm_mod(xn_ref[i], nw, sc, sh)
        hp = jnp.where(t <= first_lat, 0.0, hp)
        hn = jnp.where((t == first_lat - 1) | (t == nt - 1), 0.0, hn)
        hext_ref[i, 0:TM, :] = _norm_mod(x, nw, sc, sh).astype(BF16)
        hext_ref[i, TM:, :] = jnp.concatenate([hn, hp], axis=0).astype(BF16)
        for j in range(FFN_DIM // FFN_NC):
            cs = slice(j * FFN_NC, (j + 1) * FFN_NC)
            a = _dot(hext_ref[i], wa_ref[:, cs])
            b = _dot(hext_ref[i, 0:TM, :], wb_ref[:, cs])
            cw = cw_ref[:, cs]
            prev = pltpu.roll(a, 1, 0)[0:TM]
            nxt = pltpu.roll(a, ext - 1, 0)[0:TM]
            conv = prev * cw[0:1] + a[0:TM] * cw[1:2] + nxt * cw[2:3] + cb_ref[:, cs]
            u_ref[i, :, cs] = (conv * _sigmoid(conv) * b).astype(BF16)
        y = x + g2 * _dot(u_ref[i], wd_ref[...])
        if final:
            ms = jnp.mean(y * y, axis=-1, keepdims=True)
            y = y * lax.rsqrt(ms + NORM_EPS) * fw_ref[...]
        o_ref[i] = y


def _ffn_call(xs, modsel, nw, wa, wb, cw, cb, wd, fw, has_ctx, final):
    B, T, D = xs.shape
    nt = T // TM
    hb16 = TM // HALO
    first_lat = 1 if has_ctx else 0
    tile = pl.BlockSpec((BB, TM, D), lambda t, b: (b, t, 0))
    return pl.pallas_call(
        functools.partial(_ffn_kernel, has_ctx=has_ctx, final=final),
        grid=(nt, B // BB),
        in_specs=[tile,
                  pl.BlockSpec((BB, HALO, D), lambda t, b: (b, jnp.maximum(t * hb16 - 1, 0), 0)),
                  pl.BlockSpec((BB, HALO, D), lambda t, b: (b, jnp.minimum((t + 1) * hb16, T // HALO - 1), 0)),
                  pl.BlockSpec((BB, 1, 6, D), lambda t, b: (b, jnp.minimum(t + 1 - first_lat, 1), 0, 0)),
                  _const_spec(nw.shape), _const_spec(wa.shape), _const_spec(wb.shape),
                  _const_spec(cw.shape), _const_spec(cb.shape), _const_spec(wd.shape),
                  _const_spec(fw.shape)],
        out_specs=tile,
        out_shape=jax.ShapeDtypeStruct((B, T, D), F32),
        scratch_shapes=[pltpu.VMEM((BB, TM + 2 * HALO, D), BF16), pltpu.VMEM((BB, TM, FFN_DIM), BF16)],
        compiler_params=_params(("parallel", "parallel")),
        name="conv_ffn",
    )(xs, xs, xs, modsel, nw, wa, wb, cw, cb, wd, fw)


def _split_cols(w):
    out = []
    acc = 0
    for s in IN_SPLITS:
        out.append(w[:, acc:acc + s])
        acc += s
    return out


def _rope_tables(L, S):
    T = L + S
    f32 = np.float32
    j = np.arange(LANES)
    jj = j % 32
    inv_a = np.power(f32(ROPE_BASE), -(jj % 16).astype(f32) / f32(16.0)).astype(f32)
    t = np.arange(S)
    pos = np.where(((j % A_HEAD_DIM) // 32 == 0)[None, :], (t // GRID_W)[:, None], (t % GRID_W)[:, None])
    ang = pos.astype(f32) * inv_a[None, :]
    sign = np.where(jj < 16, f32(-1.0), f32(1.0))[None, :]
    ca = np.concatenate([np.ones((L, LANES), f32), np.cos(ang)], axis=0)
    sa = np.concatenate([np.zeros((L, LANES), f32), np.sin(ang) * sign], axis=0)
    half = R_QK_DIM // 2
    inv_r = np.power(f32(ROPE_BASE), -(j % half).astype(f32) / f32(half)).astype(f32)
    angr = np.arange(T).astype(f32)[:, None] * inv_r[None, :]
    signr = np.where(j < half, f32(-1.0), f32(1.0))[None, :]
    return tuple(jnp.asarray(a, F32) for a in (ca, sa, np.cos(angr), np.sin(angr) * signr))


def kernel(x, c, ctx, c_ctx, mod_w, mod_b, norm1_w, norm2_w, w_in, m_gate_bias, m_conv_w, m_norm_w,
           a_sink, ret_logit, ret_norm_w, w_br_m, w_br_a, w_br_r, w_out, ffn_up, ffn_conv_w, ffn_conv_b,
           ffn_down, final_norm_w):
    B, S, D = x.shape
    L = ctx.shape[1]
    depth = mod_w.shape[0]
    assert D == D_MODEL and L == TM and S % TM == 0 and S >= 3 * A_BLOCK and B % BB == 0
    T = L + S
    n_ctx = L // CHUNK

    rows = -(-(B + 1) // 8) * 8
    cpad = jnp.zeros((rows, D), F32).at[:B].set(c).at[B].set(c_ctx)
    mods = _modulation(cpad, mod_w, mod_b)
    tabs = _rope_tables(L, S)
    stream = (ctx, x) if depth > 1 else (None, jnp.concatenate([ctx, x], axis=1))

    gperm = jnp.arange(4 * M_HEADS).reshape(4, M_HEADS).T.reshape(-1)

    w_in, w_br_m, w_br_a, w_br_r, w_out, ffn_up, ffn_down = (
        w.astype(BF16) for w in (w_in, w_br_m, w_br_a, w_br_r, w_out, ffn_up, ffn_down))

    for l in range(depth):
        last = l == depth - 1
        lat = mods[l, :B].reshape(B, 6, D)
        cm = jnp.broadcast_to(mods[l, B].reshape(1, 6, D), (B, 6, D))
        modsel = jnp.stack([cm, lat], axis=1)

        (w_mq, w_mk, w_mv, w_mo, w_mg, w_aq, w_ak, w_av,
         w_rq, w_rk, w_rv, w_rg, w_gm, w_ga, w_gr) = _split_cols(w_in[l])
        wqk = jnp.concatenate([w_mq, w_mk], axis=1)
        wr = jnp.concatenate([w_mv, w_aq, w_ak, w_av, w_rq, w_rk, w_rv], axis=1)
        wgt = w_mg[:, gperm].T
        gbt = jnp.broadcast_to(m_gate_bias[l][gperm].reshape(-1, 1), (4 * M_HEADS, LANES))

        (hb, mq, mk, mv, gr, aq, ak, av, rq, rk, rv) = _pre_call(
            *stream, modsel, norm1_w[l].reshape(1, D), wqk, m_conv_w[l], wr, wgt, gbt, tabs)

        hm = _mlstm_call(mq, mk, mv, gr, n_ctx)
        sink_rows = jnp.repeat(a_sink[l].reshape(A_KV_HEADS, A_GROUP), A_BLOCK, axis=1)[:, None, :]
        ya = _attn_call(aq, ak, av, sink_rows.astype(F32) * LOG2E, L, not last)
        lgb = jnp.broadcast_to(ret_logit[l].T[:, :, None, None], (R_HEADS, 2, 8, R_V_DIM)).astype(F32)
        hr = _ret_call(rq, rk, rv, lgb, L // RCHUNK)

        t_off = 1 if last else 0
        wgates = jnp.concatenate([w_mo, w_rg, w_gm, w_ga, w_gr], axis=1)
        x1 = _post_call(*stream, hb, hm, ya, hr, modsel, wgates, m_norm_w[l].reshape(1, -1),
                        ret_norm_w[l].reshape(1, -1), w_br_m[l], w_br_a[l], w_br_r[l], w_out[l], t_off)
        xc = _ffn_call(x1, modsel, norm2_w[l].reshape(1, D), ffn_up[l][:, :FFN_DIM],
                       ffn_up[l][:, FFN_DIM:], ffn_conv_w[l], ffn_conv_b[l].reshape(1, -1),
                       ffn_down[l], final_norm_w.reshape(1, D), not last, last)
        stream = (None, xc)
    return xc
```

```python
import functools

import jax
import jax.numpy as jnp
import numpy as np
from jax import lax
from jax.experimental import pallas as pl
from jax.experimental.pallas import tpu as pltpu

F32 = jnp.float32
BF16 = jnp.bfloat16

D_MODEL = 1024
GRID_W = 64
NORM_EPS = 1e-6
ROPE_BASE = 10000.0
NEG_INF = -1e30
LOG2E = 1.4426950408889634

M_HEADS = 4
M_HEAD_DIM = 128
M_WIDTH = M_HEADS * M_HEAD_DIM
A_HEADS = 8
A_KV_HEADS = 2
A_GROUP = A_HEADS // A_KV_HEADS
A_HEAD_DIM = 64
A_WIDTH = A_HEADS * A_HEAD_DIM
A_KV_WIDTH = A_KV_HEADS * A_HEAD_DIM
A_WINDOW = 128
A_BLOCK = 128
R_HEADS = 4
R_QK_DIM = 128
R_V_DIM = 256
R_QK_WIDTH = R_HEADS * R_QK_DIM
R_V_WIDTH = R_HEADS * R_V_DIM
FFN_DIM = 2816

IN_SPLITS = (M_WIDTH, M_WIDTH, M_WIDTH, M_WIDTH, 4 * M_HEADS,
             A_WIDTH, A_KV_WIDTH, A_KV_WIDTH,
             R_QK_WIDTH, R_QK_WIDTH, R_V_WIDTH, R_V_WIDTH,
             D_MODEL, D_MODEL, D_MODEL)

TM = 256
BB = 2
HALO = 8
CHUNK = 128
RCHUNK = 256
FFN_NC = 256
LANES = 128
VMEM_LIMIT = 56 * 1024 * 1024


def _sigmoid(x):
    return 1.0 / (1.0 + jnp.exp(-x))


def _log_sigmoid(x):
    return jnp.minimum(x, 0.0) - jnp.log1p(jnp.exp(-jnp.abs(x)))


def _dot(a, b):
    return jnp.dot(a, b, preferred_element_type=F32)


def _dot_nt(a, b):
    return lax.dot_general(a, b, (((1,), (1,)), ((), ())), preferred_element_type=F32)


def _dot_tn(a, b):
    return lax.dot_general(a, b, (((0,), (0,)), ((), ())), preferred_element_type=F32)


def _const_spec(shape):
    nd = len(shape)
    return pl.BlockSpec(shape, lambda *_: (0,) * nd, pipeline_mode=pl.Buffered(1))


def _params(sem):
    return pltpu.CompilerParams(dimension_semantics=sem, vmem_limit_bytes=VMEM_LIMIT)


def _mod_kernel(c_ref, w_ref, b_ref, o_ref):
    c = c_ref[...]
    s = (c * _sigmoid(c)).astype(BF16)
    o_ref[0] = _dot(s, w_ref[0].astype(BF16)) + b_ref[0]


def _modulation(cpad, mod_w, mod_b):
    depth, d, n = mod_w.shape
    rows = cpad.shape[0]
    bn = 1024
    return pl.pallas_call(
        _mod_kernel,
        grid=(depth, n // bn),
        in_specs=[pl.BlockSpec((rows, d), lambda l, j: (0, 0)),
                  pl.BlockSpec((1, d, bn), lambda l, j: (l, 0, j)),
                  pl.BlockSpec((1, 1, bn), lambda l, j: (l, 0, j))],
        out_specs=pl.BlockSpec((1, rows, bn), lambda l, j: (l, 0, j)),
        out_shape=jax.ShapeDtypeStruct((depth, rows, n), F32),
        compiler_params=_params(("parallel", "parallel")),
        name="modulation",
    )(cpad, mod_w, mod_b.reshape(depth, 1, n))


def _norm_mod(xv, nw, sc, sh):
    ms = jnp.mean(xv * xv, axis=-1, keepdims=True)
    return (xv * lax.rsqrt(ms + NORM_EPS) * nw) * (1.0 + sc) + sh


def _seg_cumsum(x, axis, reverse):
    n = x.shape[axis]
    idx = lax.broadcasted_iota(jnp.int32, x.shape, axis)
    s = 1
    while s < n:
        if reverse:
            shifted = pltpu.roll(x, n - s, axis)
            x = x + jnp.where(idx < n - s, shifted, 0.0)
        else:
            shifted = pltpu.roll(x, s, axis)
            x = x + jnp.where(idx >= s, shifted, 0.0)
        s *= 2
    return x


def _gate_transform(raw, gate_axis):
    tok_axis = 1 - gate_axis
    k = lax.broadcasted_iota(jnp.int32, raw.shape, gate_axis) % 4
    lf = _log_sigmoid(raw)
    cum_f = _seg_cumsum(jnp.where(k == 1, lf, 0.0), tok_axis, False)
    cum_b = _seg_cumsum(jnp.where(k == 3, lf, 0.0), tok_axis, True)
    bsel = jnp.where(k == 1, cum_f, jnp.where(k == 3, cum_b, 0.0))
    n = raw.shape[gate_axis]
    bnext = pltpu.roll(bsel, n - 1, gate_axis)
    nt = raw.shape[tok_axis]
    tot_f = lax.slice_in_dim(cum_f, nt - 1, nt, axis=tok_axis)
    tot_b = lax.slice_in_dim(cum_b, 0, 1, axis=tok_axis)
    total = jnp.broadcast_to(tot_f + tot_b, raw.shape)
    return jnp.where(k % 2 == 1, bsel, raw - bnext), total


def _pre_kernel(c_ref, x_ref, xp_ref, xn_ref, mod_ref, nw_ref, wqk_ref, cw_ref, wr_ref, wgt_ref,
                gbt_ref, ca_ref, sa_ref, cr_ref, sr_ref, *out_and_scratch, split):
    for i in range(x_ref.shape[0]):
        one = pl.ds(i, 1)
        _pre_tile(c_ref.at[one] if split else None,
                  x_ref.at[one], xp_ref.at[one], xn_ref.at[one], mod_ref.at[one], nw_ref, wqk_ref, cw_ref,
                  wr_ref, wgt_ref, gbt_ref, ca_ref, sa_ref, cr_ref, sr_ref,
                  *[r.at[one] for r in out_and_scratch[:-1]], out_and_scratch[-1].at[i])


def _tile_rows(c_ref, x_ref):
    if c_ref is None:
        return x_ref[0]
    return jnp.where(pl.program_id(0) == 0, c_ref[0], x_ref[0])


def _pre_tile(c_ref, x_ref, xp_ref, xn_ref, mod_ref, nw_ref, wqk_ref, cw_ref, wr_ref, wgt_ref,
              gbt_ref, ca_ref, sa_ref, cr_ref, sr_ref,
              hb_ref, mq_ref, mk_ref, mv_ref, gr_ref, aq_ref, ak_ref, av_ref,
              rq_ref, rk_ref, rv_ref, hext_ref):
    t = pl.program_id(0)
    nt = pl.num_programs(0)
    sh = mod_ref[0, 0, 0:1, :]
    sc = mod_ref[0, 0, 1:2, :]
    nw = nw_ref[...]
    hb = _norm_mod(_tile_rows(c_ref, x_ref), nw, sc, sh).astype(BF16)
    hb_ref[0] = hb

    grow = _dot_nt(wgt_ref[...], hb) + gbt_ref[:, 0:1]
    for c in range(TM // CHUNK):
        gr, tot = _gate_transform(grow[:, c * CHUNK:(c + 1) * CHUNK], 0)
        aend = gr + pltpu.roll(tot, 4 * M_HEADS - 1, 0)
        amax = jnp.broadcast_to(jnp.max(aend, axis=1, keepdims=True), aend.shape)
        for h in range(M_HEADS):
            hp_, o0 = h // 2, 8 * (h % 2)
            gr_ref[0, hp_, c, o0:o0 + 4, :] = gr[4 * h:4 * h + 4]
            for d in range(2):
                gr_ref[0, hp_, c, o0 + 4 + d:o0 + 5 + d, :] = tot[4 * h + 2 * d + 1:4 * h + 2 * d + 2]
                gr_ref[0, hp_, c, o0 + 6 + d:o0 + 7 + d, :] = amax[4 * h + 2 * d:4 * h + 2 * d + 1]

    hp = _norm_mod(xp_ref[0], nw, sc, sh)
    hn = _norm_mod(xn_ref[0], nw, sc, sh)
    hp = jnp.where(t <= 1, 0.0, hp)
    hn = jnp.where((t == 0) | (t == nt - 1), 0.0, hn)
    hext_ref[0:TM, :] = hb
    hext_ref[TM:, :] = jnp.concatenate([hn, hp], axis=0).astype(BF16)
    hext = hext_ref[...]
    ext = TM + 2 * HALO

    for j, (dst, scale) in enumerate(((mq_ref, 1.0), (mk_ref, M_HEAD_DIM ** -0.5))):
        p = _dot(hext, wqk_ref[:, j * M_WIDTH:(j + 1) * M_WIDTH])
        cw = cw_ref[:, j * M_WIDTH:(j + 1) * M_WIDTH]
        prev = pltpu.roll(p, 1, 0)[0:TM]
        nxt = pltpu.roll(p, ext - 1, 0)[0:TM]
        conv = prev * cw[0:1] + p[0:TM] * cw[1:2] + nxt * cw[2:3]
        act = conv * _sigmoid(conv) * scale
        for h in range(M_HEADS):
            dst[0, h] = act[:, h * M_HEAD_DIM:(h + 1) * M_HEAD_DIM].astype(BF16)

    off = 0
    p = _dot(hb, wr_ref[:, off:off + M_WIDTH])
    off += M_WIDTH
    for h in range(M_HEADS):
        for c in range(TM // CHUNK):
            blk = p[c * CHUNK:(c + 1) * CHUNK, h * M_HEAD_DIM:(h + 1) * M_HEAD_DIM]
            mv_ref[0, h, c] = blk.T.astype(BF16)

    lane = lax.broadcasted_iota(jnp.int32, (TM, LANES), 1)
    first16 = (lane % 32) < 16
    ca = ca_ref[...]
    sa = sa_ref[...]

    def rope_a(xs):
        partner = jnp.where(first16, pltpu.roll(xs, LANES - 16, 1), pltpu.roll(xs, 16, 1))
        return xs * ca + partner * sa

    p = _dot(hb, wr_ref[:, off:off + A_WIDTH])
    off += A_WIDTH
    for s in range(A_WIDTH // LANES):
        r = (rope_a(p[:, s * LANES:(s + 1) * LANES]) * (A_HEAD_DIM ** -0.5 * LOG2E)).astype(BF16)
        aq_ref[0, 2 * s] = r[:, 0:A_HEAD_DIM]
        aq_ref[0, 2 * s + 1] = r[:, A_HEAD_DIM:]
    p = _dot(hb, wr_ref[:, off:off + 2 * A_KV_WIDTH])
    off += 2 * A_KV_WIDTH
    r = rope_a(p[:, 0:LANES]).astype(BF16)
    for h in range(A_KV_HEADS):
        ak_ref[0, h] = r[:, h * A_HEAD_DIM:(h + 1) * A_HEAD_DIM]
    pv = p[:, LANES:]
    for h in range(A_KV_HEADS):
        vh = pv if h == 0 else pltpu.roll(pv, A_HEAD_DIM, 1)
        ve = jnp.where(lane < A_HEAD_DIM, vh, 1.0)
        for c in range(TM // A_BLOCK):
            av_ref[0, h, c] = ve[c * A_BLOCK:(c + 1) * A_BLOCK].T.astype(BF16)

    cr = cr_ref[...]
    sr = sr_ref[...]
    for is_k in (False, True):
        p = _dot(hb, wr_ref[:, off:off + R_QK_WIDTH])
        off += R_QK_WIDTH
        for h in range(R_HEADS):
            xs = p[:, h * R_QK_DIM:(h + 1) * R_QK_DIM]
            rot = xs * cr + pltpu.roll(xs, R_QK_DIM // 2, 1) * sr
            if is_k:
                rot = rot * R_QK_DIM ** -0.5
                for c in range(TM // CHUNK):
                    rk_ref[0, h, c] = rot[c * CHUNK:(c + 1) * CHUNK].T.astype(BF16)
            else:
                rq_ref[0, h] = rot.astype(BF16)
    p = _dot(hb, wr_ref[:, off:off + R_V_WIDTH])
    for h in range(R_HEADS):
        rv_ref[0, h] = p[:, h * R_V_DIM:(h + 1) * R_V_DIM].astype(BF16)


def _stream_specs(ctx, xs):
    hb = TM // HALO
    D = xs.shape[-1]
    off = 0 if ctx is None else 1
    last = xs.shape[1] // HALO - 1
    return [
        pl.BlockSpec((BB, TM if off else HALO, D), lambda t, b: (b, 0, 0)),
        pl.BlockSpec((BB, TM, D), lambda t, b: (b, jnp.maximum(t - off, 0), 0)),
        pl.BlockSpec((BB, HALO, D), lambda t, b: (b, jnp.maximum((t - off) * hb - 1, 0), 0)),
        pl.BlockSpec((BB, HALO, D), lambda t, b: (b, jnp.clip((t - off + 1) * hb, 0, last), 0)),
    ]


def _pre_call(ctx, xs, modsel, nw, wqk, cw, wr, wgt, gbt, tabs):
    B, _, D = xs.shape
    T = xs.shape[1] + (0 if ctx is None else ctx.shape[1])
    nt = T // TM
    nch = T // CHUNK

    def tile(t, b):
        return (b, t, 0)

    def head_tile(t, b):
        return (b, 0, t, 0)

    in_specs = _stream_specs(ctx, xs) + [
        pl.BlockSpec((BB, 1, 6, D), lambda t, b: (b, jnp.minimum(t, 1), 0, 0)),
        _const_spec(nw.shape), _const_spec(wqk.shape), _const_spec(cw.shape), _const_spec(wr.shape),
        _const_spec(wgt.shape), _const_spec(gbt.shape),
    ] + [pl.BlockSpec((TM, LANES), lambda t, b: (t, 0)) for _ in range(4)]
    out_shape = [
        jax.ShapeDtypeStruct((B, T, D), BF16),
        jax.ShapeDtypeStruct((B, M_HEADS, T, M_HEAD_DIM), BF16),
        jax.ShapeDtypeStruct((B, M_HEADS, T, M_HEAD_DIM), BF16),
        jax.ShapeDtypeStruct((B, M_HEADS, nch, M_HEAD_DIM, CHUNK), BF16),
        jax.ShapeDtypeStruct((B, 2, nch, 16, CHUNK), F32),
        jax.ShapeDtypeStruct((B, A_HEADS, T, A_HEAD_DIM), BF16),
        jax.ShapeDtypeStruct((B, A_KV_HEADS, T, A_HEAD_DIM), BF16),
        jax.ShapeDtypeStruct((B, A_KV_HEADS, T // A_BLOCK, LANES, A_BLOCK), BF16),
        jax.ShapeDtypeStruct((B, R_HEADS, T, R_QK_DIM), BF16),
        jax.ShapeDtypeStruct((B, R_HEADS, nch, R_QK_DIM, CHUNK), BF16),
        jax.ShapeDtypeStruct((B, R_HEADS, T, R_V_DIM), BF16),
    ]
    out_specs = [
        pl.BlockSpec((BB, TM, D), tile),
        pl.BlockSpec((BB, M_HEADS, TM, M_HEAD_DIM), head_tile),
        pl.BlockSpec((BB, M_HEADS, TM, M_HEAD_DIM), head_tile),
        pl.BlockSpec((BB, M_HEADS, TM // CHUNK, M_HEAD_DIM, CHUNK), lambda t, b: (b, 0, t, 0, 0)),
        pl.BlockSpec((BB, 2, TM // CHUNK, 16, CHUNK), lambda t, b: (b, 0, t, 0, 0)),
        pl.BlockSpec((BB, A_HEADS, TM, A_HEAD_DIM), head_tile),
        pl.BlockSpec((BB, A_KV_HEADS, TM, A_HEAD_DIM), head_tile),
        pl.BlockSpec((BB, A_KV_HEADS, TM // A_BLOCK, LANES, A_BLOCK), lambda t, b: (b, 0, t, 0, 0)),
        pl.BlockSpec((BB, R_HEADS, TM, R_QK_DIM), head_tile),
        pl.BlockSpec((BB, R_HEADS, TM // CHUNK, R_QK_DIM, CHUNK), lambda t, b: (b, 0, t, 0, 0)),
        pl.BlockSpec((BB, R_HEADS, TM, R_V_DIM), head_tile),
    ]
    return pl.pallas_call(
        functools.partial(_pre_kernel, split=ctx is not None),
        grid=(nt, B // BB),
        in_specs=in_specs,
        out_specs=out_specs,
        out_shape=out_shape,
        scratch_shapes=[pltpu.VMEM((BB, TM + 2 * HALO, D), BF16)],
        compiler_params=_params(("parallel", "parallel")),
        name="pre_proj",
    )(xs if ctx is None else ctx, xs, xs, xs, modsel, nw, wqk, cw, wr, wgt, gbt, *tabs)


def _bwd_chunk(i, n_ctx, n_all):
    if isinstance(i, int):
        return n_ctx - 1 - i if i < n_ctx else n_all - 1 + n_ctx - i
    return n_all - 1 + n_ctx - i


def _chunk_start(c):
    return c * CHUNK if isinstance(c, int) else pl.multiple_of(c * CHUNK, CHUNK)


def _chunk_loop(body, n_ctx, n_all, unroll):
    for i in range(n_ctx):
        body(i, 0)
    n_lat = n_all - n_ctx
    while n_lat % unroll:
        unroll //= 2
    lax.fori_loop(n_ctx, n_all, body, 0, unroll=unroll)


def _mlstm_kernel(q_ref, k_ref, vt_ref, gr_ref, o_ref, c_ref, n_ref, m_ref,
                  cs_ref, ns_ref, ms_ref, qp_ref, nq_ref, *, n_ctx):
    n_all = gr_ref.shape[2]
    dh = q_ref.shape[3]
    c_ref[...] = jnp.zeros(c_ref.shape, F32)
    n_ref[...] = jnp.zeros(n_ref.shape, F32)
    m_ref[...] = jnp.zeros(m_ref.shape, F32)

    def scan_unit(c, hh, d):
        u = 2 * hh + d
        c0 = _chunk_start(c)
        k = k_ref[0, hh, pl.ds(c0, CHUNK), :]
        vt = vt_ref[0, hh, c]
        rows = gr_ref[0, 0, c, 8 * hh:8 * hh + 8, :]
        tot = rows[4 + d:5 + d]
        aend = rows[2 * d:2 * d + 1] + tot
        amax = rows[6 + d:7 + d]
        cst = c_ref[u]
        nv = n_ref[u]
        m = m_ref[u]
        cs_ref[u, c] = cst.astype(BF16)
        ns_ref[u, c] = jnp.broadcast_to(nv, (16, dh)).astype(BF16)
        ms_ref[u, c] = jnp.broadcast_to(m, (8, CHUNK))
        m_new = jnp.maximum(tot + m, amax)
        w = jnp.exp(aend - m_new)
        dec = jnp.exp(tot + m - m_new)
        vw = (vt.astype(F32) * w).astype(BF16)
        c_ref[u] = dec * cst + _dot(vw, k)
        wk = _dot(jnp.broadcast_to(w, (16, CHUNK)).astype(BF16), k)
        n_ref[u] = dec * nv + wk[0:1]
        m_ref[u] = m_new

    def scan_step(i, carry):
        cb = _bwd_chunk(i, n_ctx, n_all)
        for hh in range(2):
            scan_unit(i, hh, 0)
            scan_unit(cb, hh, 1)
        return carry

    _chunk_loop(scan_step, n_ctx, n_all, 4)

    row = lax.broadcasted_iota(jnp.int32, (CHUNK, CHUNK), 0)
    col = lax.broadcasted_iota(jnp.int32, (CHUNK, CHUNK), 1)
    tri = (row <= col, row >= col)

    def query_products(c, slot):
        c0 = _chunk_start(c)
        for hh in range(2):
            q = q_ref[0, hh, pl.ds(c0, CHUNK), :]
            k = k_ref[0, hh, pl.ds(c0, CHUNK), :]
            qp_ref[slot, hh, 0] = _dot_nt(k, q)
            for d in range(2):
                u = 2 * hh + d
                qp_ref[slot, hh, 1 + d] = _dot_nt(cs_ref[u, c], q)
                nq_ref[slot, u] = _dot_nt(ns_ref[u, c], q)[0:8]

    def outputs(c, slot):
        c0 = _chunk_start(c)
        gcv = jnp.concatenate([gr_ref[0, 0, c], jnp.zeros((CHUNK - 16, CHUNK), F32)], axis=0).T
        for hh in range(2):
            vt = vt_ref[0, hh, c]
            rows = gr_ref[0, 0, c, 8 * hh:8 * hh + 8, :]
            st = qp_ref[slot, hh, 0]
            sds, iws, invs = [], [], []
            for d in range(2):
                u = 2 * hh + d
                a0c = gcv[:, 8 * hh + 2 * d:8 * hh + 2 * d + 1]
                b = rows[2 * d + 1:2 * d + 2]
                m = ms_ref[u, c, 0:1, :]
                dlog = jnp.where(tri[d], a0c + b, NEG_INF)
                mj = jnp.maximum(b + m, jnp.max(dlog, axis=0, keepdims=True))
                sd = st * jnp.exp(dlog - mj)
                iw = jnp.exp(b + m - mj)
                den = iw * nq_ref[slot, u, 0:1, :] + jnp.sum(sd, axis=0, keepdims=True)
                sds.append(sd.astype(BF16))
                iws.append(iw)
                invs.append(1.0 / jnp.maximum(jnp.abs(den), jnp.exp(-mj)))
            num = _dot(vt, jnp.concatenate(sds, axis=1))
            acc = None
            for d in range(2):
                ht = (num[:, d * CHUNK:(d + 1) * CHUNK] + iws[d] * qp_ref[slot, hh, 1 + d]) * invs[d]
                acc = ht if acc is None else acc + ht
            o_ref[0, hh, pl.ds(c0, CHUNK), :] = acc.T

    query_products(0, 0)

    def out_pair(i, carry):
        c = 2 * i
        query_products(c + 1, 1)
        outputs(c, 0)
        query_products(jnp.minimum(c + 2, n_all - 1), 0)
        outputs(c + 1, 1)
        return carry

    n_pairs = n_all // 2
    unroll = 4
    first = n_pairs % unroll
    for i in range(first):
        out_pair(i, 0)
    lax.fori_loop(first, n_pairs, out_pair, 0, unroll=unroll)


def _mlstm_call(mq, mk, mvt, gr, n_ctx):
    B, H, T, dh = mq.shape
    nch = T // CHUNK
    qspec = pl.BlockSpec((1, 2, T, dh), lambda b, p: (b, p, 0, 0))
    return pl.pallas_call(
        functools.partial(_mlstm_kernel, n_ctx=n_ctx),
        grid=(B, H // 2),
        in_specs=[qspec, qspec,
                  pl.BlockSpec((1, 2, nch, dh, CHUNK), lambda b, p: (b, p, 0, 0, 0)),
                  pl.BlockSpec((1, 1, nch, 16, CHUNK), lambda b, p: (b, p, 0, 0, 0))],
        out_specs=pl.BlockSpec((1, 2, T, dh), lambda b, p: (b, p, 0, 0)),
        out_shape=jax.ShapeDtypeStruct((B, H, T, dh), F32),
        scratch_shapes=[pltpu.VMEM((4, dh, dh), F32), pltpu.VMEM((4, 1, dh), F32),
                        pltpu.VMEM((4, 1, CHUNK), F32),
                        pltpu.VMEM((4, nch, dh, dh), BF16), pltpu.VMEM((4, nch, 16, dh), BF16),
                        pltpu.VMEM((4, nch, 8, CHUNK), F32),
                        pltpu.VMEM((2, 2, 3, dh, CHUNK), F32), pltpu.VMEM((2, 4, 8, CHUNK), F32)],
        compiler_params=_params(("parallel", "parallel")),
        name="mlstm",
    )(mq, mk, mvt, gr)


def _ret_kernel(q_ref, kt_ref, v_ref, lg_ref, o_ref, st_ref, sts_ref, dec_ref, ib_ref, wb_ref, cd_ref,
                *, n_ctx):
    sub = RCHUNK // CHUNK
    n_all = kt_ref.shape[2] // sub
    dk = q_ref.shape[3]
    dv = v_ref.shape[3]
    st_ref[...] = jnp.zeros(st_ref.shape, F32)

    @pl.when(pl.program_id(1) == 0)
    def _():
        row = lax.broadcasted_iota(jnp.int32, (RCHUNK, RCHUNK), 0).astype(F32)
        col = lax.broadcasted_iota(jnp.int32, (RCHUNK, RCHUNK), 1).astype(F32)
        kcol = lax.broadcasted_iota(jnp.int32, (dk, RCHUNK), 1).astype(F32)
        dec = None
        for d in range(2):
            lg = _log_sigmoid(lg_ref[0, d])[0:1]
            diff = row - col if d == 0 else col - row
            dd = jnp.where(diff >= 0, jnp.exp(lg * jnp.maximum(diff, 0.0)), 0.0)
            dec = dd if dec is None else dec + dd
            pin = row if d == 0 else RCHUNK - 1.0 - row
            ib_ref[d] = jnp.exp(lg * (pin + 1.0))
            pkey = kcol if d == 0 else RCHUNK - 1.0 - kcol
            wb_ref[d] = jnp.exp(lg * (RCHUNK - 1.0 - pkey))
            cd_ref[d] = jnp.broadcast_to(jnp.exp(lg * RCHUNK), (8, dv))
        dec_ref[...] = dec

    cds = [cd_ref[d, 0:1, :] for d in range(2)]

    def rows(c):
        r0 = c * RCHUNK
        return pl.ds(r0 if isinstance(c, int) else pl.multiple_of(r0, RCHUNK), RCHUNK)

    def k_t(c):
        return jnp.concatenate([kt_ref[0, 0, sub * c + j] for j in range(sub)], axis=1)

    def scan_step(i, carry):
        cb = _bwd_chunk(i, n_ctx, n_all)
        for d, c in ((0, i), (1, cb)):
            st = st_ref[d]
            sts_ref[c, :, d * dv:(d + 1) * dv] = st.astype(BF16)
            kw = (k_t(c).astype(F32) * wb_ref[d]).astype(BF16)
            st_ref[d] = cds[d] * st + _dot(kw, v_ref[0, 0, rows(c), :])
        return carry

    _chunk_loop(scan_step, n_ctx, n_all, 8)

    def out_step(c, carry):
        q = q_ref[0, 0, rows(c), :]
        s = (_dot(q, k_t(c)) * dec_ref[...]).astype(BF16)
        qst = _dot(q, sts_ref[c])
        o_ref[0, 0, rows(c), :] = (_dot(s, v_ref[0, 0, rows(c), :])
                                   + ib_ref[0] * qst[:, 0:dv] + ib_ref[1] * qst[:, dv:])
        return carry

    _chunk_loop(out_step, n_ctx, n_all, 8)


def _ret_call(rq, rkt, rv, lgb, n_ctx):
    B, H, T, dk = rq.shape
    dv = rv.shape[-1]
    nch = T // CHUNK
    assert dv == RCHUNK and dk <= RCHUNK and T % RCHUNK == 0
    return pl.pallas_call(
        functools.partial(_ret_kernel, n_ctx=n_ctx),
        grid=(H, B),
        in_specs=[pl.BlockSpec((1, 1, T, dk), lambda h, b: (b, h, 0, 0)),
                  pl.BlockSpec((1, 1, nch, dk, CHUNK), lambda h, b: (b, h, 0, 0, 0)),
                  pl.BlockSpec((1, 1, T, dv), lambda h, b: (b, h, 0, 0)),
                  pl.BlockSpec((1, 2, 8, dv), lambda h, b: (h, 0, 0, 0))],
        out_specs=pl.BlockSpec((1, 1, T, dv), lambda h, b: (b, h, 0, 0)),
        out_shape=jax.ShapeDtypeStruct((B, H, T, dv), F32),
        scratch_shapes=[pltpu.VMEM((2, dk, dv), F32), pltpu.VMEM((T // RCHUNK, dk, 2 * dv), BF16),
                        pltpu.VMEM((RCHUNK, RCHUNK), F32), pltpu.VMEM((2, RCHUNK, dv), F32),
                        pltpu.VMEM((2, dk, RCHUNK), F32), pltpu.VMEM((2, 8, dv), F32)],
        compiler_params=_params(("parallel", "arbitrary")),
        name="retention",
    )(rq, rkt, rv, lgb)


def _attn_kernel(q_ref, k_ref, vt_ref, sink_ref, o_ref, bias_ref, s_ref, *, n_ctx_tok, need_ctx):
    T = q_ref.shape[2]
    L = n_ctx_tok
    S = T - L
    nb = S // A_BLOCK
    nspan = 3
    span = nspan * A_BLOCK
    ncb = L // A_BLOCK
    cols = A_GROUP * A_BLOCK
    kc = k_ref[0, 0, 0:L, :]
    vtc = jnp.concatenate([vt_ref[0, 0, j] for j in range(ncb)], axis=1)
    sink = sink_ref[0]

    @pl.when((pl.program_id(0) == 0) & (pl.program_id(1) == 0))
    def _():
        kpos = lax.broadcasted_iota(jnp.int32, (span, A_BLOCK), 0)
        qpos = lax.broadcasted_iota(jnp.int32, (span, A_BLOCK), 1)
        for delta in range(nspan):
            bias_ref[delta] = jnp.where(jnp.abs(qpos + delta * A_BLOCK - kpos) <= A_WINDOW, 0.0, NEG_INF)

    def finish(blk, acc, m):
        den = acc[A_HEAD_DIM:A_HEAD_DIM + 1] + jnp.exp2(sink - m)
        ot = (acc[0:A_HEAD_DIM] * (1.0 / den)).astype(BF16)
        for g in range(A_GROUP):
            o_ref[0, blk, g * A_HEAD_DIM:(g + 1) * A_HEAD_DIM, :] = ot[:, g * A_BLOCK:(g + 1) * A_BLOCK]

    for cbk in range(ncb):
        if need_ctx:
            qs = q_ref[0, :, cbk * A_BLOCK:(cbk + 1) * A_BLOCK, :].reshape(cols, A_HEAD_DIM)
            s = _dot_nt(kc, qs)
            m = jnp.maximum(jnp.max(s, axis=0, keepdims=True), sink)
            finish(cbk, _dot(vtc, jnp.exp2(s - m).astype(BF16)), m)
        else:
            o_ref[0, cbk] = jnp.zeros(o_ref.shape[2:], BF16)

    def first_key_block(n):
        return jnp.clip(n - 1, 0, nb - nspan)

    def scores(n, slot):
        q0 = pl.multiple_of(L + n * A_BLOCK, A_BLOCK)
        kb = first_key_block(n)
        k0 = pl.multiple_of(L + kb * A_BLOCK, A_BLOCK)
        qs = q_ref[0, :, pl.ds(q0, A_BLOCK), :].reshape(cols, A_HEAD_DIM)
        bias = bias_ref[n - kb]
        s_ref[slot, 0:span, :] = (_dot_nt(k_ref[0, 0, pl.ds(k0, span), :], qs)
                                  + jnp.concatenate([bias] * A_GROUP, axis=1))
        s_ref[slot, span:, :] = _dot_nt(kc, qs)

    def softmax_pv(n, slot):
        kb = first_key_block(n)
        s = s_ref[slot]
        m = jnp.maximum(jnp.max(s, axis=0, keepdims=True), sink)
        p = jnp.exp2(s - m).astype(BF16)
        vt = jnp.concatenate([vt_ref[0, 0, ncb + kb + j] for j in range(nspan)] + [vtc], axis=1)
        finish(ncb + n, _dot(vt, p), m)

    scores(0, 0)

    def block_pair(i, carry):
        n = 2 * i
        scores(n + 1, 1)
        softmax_pv(n, 0)
        scores(jnp.minimum(n + 2, nb - 1), 0)
        softmax_pv(n + 1, 1)
        return carry

    n_pairs = nb // 2
    unroll = 4
    first = n_pairs % unroll
    for i in range(first):
        block_pair(i, 0)
    lax.fori_loop(first, n_pairs, block_pair, 0, unroll=unroll)


def _attn_call(aq, ak, avt, sink_rows, n_ctx_tok, need_ctx):
    B, _, T, hd = aq.shape
    nblk = T // A_BLOCK
    cols = A_GROUP * A_BLOCK
    return pl.pallas_call(
        functools.partial(_attn_kernel, n_ctx_tok=n_ctx_tok, need_ctx=need_ctx),
        grid=(B, A_KV_HEADS),
        in_specs=[pl.BlockSpec((1, A_GROUP, T, hd), lambda b, h: (b, h, 0, 0)),
                  pl.BlockSpec((1, 1, T, hd), lambda b, h: (b, h, 0, 0)),
                  pl.BlockSpec((1, 1, nblk, LANES, A_BLOCK), lambda b, h: (b, h, 0, 0, 0)),
                  pl.BlockSpec((1, 1, cols), lambda b, h: (h, 0, 0))],
        out_specs=pl.BlockSpec((1, nblk, A_GROUP * hd, A_BLOCK), lambda b, h: (b, 0, h, 0)),
        out_shape=jax.ShapeDtypeStruct((B, nblk, A_WIDTH, A_BLOCK), BF16),
        scratch_shapes=[pltpu.VMEM((3, 3 * A_BLOCK, A_BLOCK), F32),
                        pltpu.VMEM((2, 3 * A_BLOCK + n_ctx_tok, cols), F32)],
        compiler_params=_params(("arbitrary", "arbitrary")),
        name="window_attn",
    )(aq, ak, avt, sink_rows)


def _head_norm(y):
    mu = jnp.mean(y, axis=-1, keepdims=True)
    yc = y - mu
    var = jnp.mean(yc * yc, axis=-1, keepdims=True)
    return yc * lax.rsqrt(var + NORM_EPS)


def _post_kernel(c_ref, x_ref, hb_ref, hm_ref, ya_ref, hr_ref, mod_ref, wg_ref, mnw_ref, rnw_ref,
                 wbm_ref, wba_ref, wbr_ref, wo_ref, o_ref, *, split):
    for i in range(x_ref.shape[0]):
        hb = hb_ref[i]
        off = 0
        hm = jnp.concatenate([_head_norm(hm_ref[i, h]) for h in range(M_HEADS)], axis=1) * mnw_ref[...]
        ym = (_sigmoid(_dot(hb, wg_ref[:, off:off + M_WIDTH])) * hm).astype(BF16)
        off += M_WIDTH
        hr = jnp.concatenate([_head_norm(hr_ref[i, h]) for h in range(R_HEADS)], axis=1) * rnw_ref[...]
        rg = _dot(hb, wg_ref[:, off:off + R_V_WIDTH])
        yr = (rg * _sigmoid(rg) * hr).astype(BF16)
        off += R_V_WIDTH
        z = _sigmoid(_dot(hb, wg_ref[:, off:off + D_MODEL])) * _dot(ym, wbm_ref[...])
        off += D_MODEL
        pa = jnp.concatenate([_dot_tn(ya_ref[i, c], wba_ref[...]) for c in range(TM // A_BLOCK)], axis=0)
        z = z + _sigmoid(_dot(hb, wg_ref[:, off:off + D_MODEL])) * pa
        off += D_MODEL
        z = z + _sigmoid(_dot(hb, wg_ref[:, off:off + D_MODEL])) * _dot(yr, wbr_ref[...])
        y = _dot(z.astype(BF16), wo_ref[...])
        one = pl.ds(i, 1)
        xres = _tile_rows(c_ref.at[one] if split else None, x_ref.at[one])
        o_ref[i] = xres + mod_ref[i, 0, 2:3, :] * y


def _post_call(ctx, xs, hb, hm, ya, hr, modsel, wg, mnw, rnw, wbm, wba, wbr, wo, t_off):
    B, T, D = hb.shape
    nt = T // TM - t_off
    split = ctx is not None
    assert not (split and t_off)

    def tile(t, b):
        return (b, t + t_off, 0)

    def head_tile(t, b):
        return (b, 0, t + t_off, 0)

    return pl.pallas_call(
        functools.partial(_post_kernel, split=split),
        grid=(nt, B // BB),
        in_specs=[pl.BlockSpec((BB, TM if split else HALO, D), lambda t, b: (b, 0, 0)),
                  pl.BlockSpec((BB, TM, D), (lambda t, b: (b, jnp.maximum(t - 1, 0), 0)) if split else tile),
                  pl.BlockSpec((BB, TM, D), tile),
                  pl.BlockSpec((BB, M_HEADS, TM, M_HEAD_DIM), head_tile),
                  pl.BlockSpec((BB, TM // A_BLOCK, A_WIDTH, A_BLOCK), lambda t, b: (b, t + t_off, 0, 0)),
                  pl.BlockSpec((BB, R_HEADS, TM, R_V_DIM), head_tile),
                  pl.BlockSpec((BB, 1, 6, D), lambda t, b: (b, jnp.minimum(t + t_off, 1), 0, 0)),
                  _const_spec(wg.shape), _const_spec(mnw.shape), _const_spec(rnw.shape),
                  _const_spec(wbm.shape), _const_spec(wba.shape), _const_spec(wbr.shape),
                  _const_spec(wo.shape)],
        out_specs=pl.BlockSpec((BB, TM, D), lambda t, b: (b, t, 0)),
        out_shape=jax.ShapeDtypeStruct((B, nt * TM, D), F32),
        compiler_params=_params(("parallel", "parallel")),
        name="post_merge",
    )(ctx if split else xs, xs, hb, hm, ya, hr, modsel, wg, mnw, rnw, wbm, wba, wbr, wo)


def _ffn_kernel(x_ref, xp_ref, xn_ref, mod_ref, nw_ref, wa_ref, wb_ref, cw_ref, cb_ref, wd_ref, fw_ref,
                o_ref, hext_ref, u_ref, *, has_ctx, final):
    t = pl.program_id(0)
    nt = pl.num_programs(0)
    first_lat = 1 if has_ctx else 0
    nw = nw_ref[...]
    ext = TM + 2 * HALO
    for i in range(x_ref.shape[0]):
        sh = mod_ref[i, 0, 3:4, :]
        sc = mod_ref[i, 0, 4:5, :]
        g2 = mod_ref[i, 0, 5:6, :]
        x = x_ref[i]
        hp = _norm_mod(xp_ref[i], nw, sc, sh)
        hn = _norm_mod(xn_ref[i], nw, sc, sh)
        hp = jnp.where(t <= first_lat, 0.0, hp)
        hn = jnp.where((t == first_lat - 1) | (t == nt - 1), 0.0, hn)
        hext_ref[i, 0:TM, :] = _norm_mod(x, nw, sc, sh).astype(BF16)
        hext_ref[i, TM:, :] = jnp.concatenate([hn, hp], axis=0).astype(BF16)
        for j in range(FFN_DIM // FFN_NC):
            cs = slice(j * FFN_NC, (j + 1) * FFN_NC)
            a = _dot(hext_ref[i], wa_ref[:, cs])
            b = _dot(hext_ref[i, 0:TM, :], wb_ref[:, cs])
            cw = cw_ref[:, cs]
            prev = pltpu.roll(a, 1, 0)[0:TM]
            nxt = pltpu.roll(a, ext - 1, 0)[0:TM]
            conv = prev * cw[0:1] + a[0:TM] * cw[1:2] + nxt * cw[2:3] + cb_ref[:, cs]
            u_ref[i, :, cs] = (conv * _sigmoid(conv) * b).astype(BF16)
        y = x + g2 * _dot(u_ref[i], wd_ref[...])
        if final:
            ms = jnp.mean(y * y, axis=-1, keepdims=True)
            y = y * lax.rsqrt(ms + NORM_EPS) * fw_ref[...]
        o_ref[i] = y


def _ffn_call(xs, modsel, nw, wa, wb, cw, cb, wd, fw, has_ctx, final):
    B, T, D = xs.shape
    nt = T // TM
    hb16 = TM // HALO
    first_lat = 1 if has_ctx else 0
    tile = pl.BlockSpec((BB, TM, D), lambda t, b: (b, t, 0))
    return pl.pallas_call(
        functools.partial(_ffn_kernel, has_ctx=has_ctx, final=final),
        grid=(nt, B // BB),
        in_specs=[tile,
                  pl.BlockSpec((BB, HALO, D), lambda t, b: (b, jnp.maximum(t * hb16 - 1, 0), 0)),
                  pl.BlockSpec((BB, HALO, D), lambda t, b: (b, jnp.minimum((t + 1) * hb16, T // HALO - 1), 0)),
                  pl.BlockSpec((BB, 1, 6, D), lambda t, b: (b, jnp.minimum(t + 1 - first_lat, 1), 0, 0)),
                  _const_spec(nw.shape), _const_spec(wa.shape), _const_spec(wb.shape),
                  _const_spec(cw.shape), _const_spec(cb.shape), _const_spec(wd.shape),
                  _const_spec(fw.shape)],
        out_specs=tile,
        out_shape=jax.ShapeDtypeStruct((B, T, D), F32),
        scratch_shapes=[pltpu.VMEM((BB, TM + 2 * HALO, D), BF16), pltpu.VMEM((BB, TM, FFN_DIM), BF16)],
        compiler_params=_params(("parallel", "parallel")),
        name="conv_ffn",
    )(xs, xs, xs, modsel, nw, wa, wb, cw, cb, wd, fw)


def _split_cols(w):
    out = []
    acc = 0
    for s in IN_SPLITS:
        out.append(w[:, acc:acc + s])
        acc += s
    return out


def _rope_tables(L, S):
    T = L + S
    f32 = np.float32
    j = np.arange(LANES)
    jj = j % 32
    inv_a = np.power(f32(ROPE_BASE), -(jj % 16).astype(f32) / f32(16.0)).astype(f32)
    t = np.arange(S)
    pos = np.where(((j % A_HEAD_DIM) // 32 == 0)[None, :], (t // GRID_W)[:, None], (t % GRID_W)[:, None])
    ang = pos.astype(f32) * inv_a[None, :]
    sign = np.where(jj < 16, f32(-1.0), f32(1.0))[None, :]
    ca = np.concatenate([np.ones((L, LANES), f32), np.cos(ang)], axis=0)
    sa = np.concatenate([np.zeros((L, LANES), f32), np.sin(ang) * sign], axis=0)
    half = R_QK_DIM // 2
    inv_r = np.power(f32(ROPE_BASE), -(j % half).astype(f32) / f32(half)).astype(f32)
    angr = np.arange(T).astype(f32)[:, None] * inv_r[None, :]
    signr = np.where(j < half, f32(-1.0), f32(1.0))[None, :]
    return tuple(jnp.asarray(a, F32) for a in (ca, sa, np.cos(angr), np.sin(angr) * signr))


def kernel(x, c, ctx, c_ctx, mod_w, mod_b, norm1_w, norm2_w, w_in, m_gate_bias, m_conv_w, m_norm_w,
           a_sink, ret_logit, ret_norm_w, w_br_m, w_br_a, w_br_r, w_out, ffn_up, ffn_conv_w, ffn_conv_b,
           ffn_down, final_norm_w):
    B, S, D = x.shape
    L = ctx.shape[1]
    depth = mod_w.shape[0]
    assert D == D_MODEL and L == TM and S % TM == 0 and S >= 3 * A_BLOCK and B % BB == 0
    T = L + S
    n_ctx = L // CHUNK

    rows = -(-(B + 1) // 8) * 8
    cpad = jnp.zeros((rows, D), F32).at[:B].set(c).at[B].set(c_ctx)
    mods = _modulation(cpad, mod_w, mod_b)
    tabs = _rope_tables(L, S)
    stream = (ctx, x) if depth > 1 else (None, jnp.concatenate([ctx, x], axis=1))

    gperm = jnp.arange(4 * M_HEADS).reshape(4, M_HEADS).T.reshape(-1)

    for l in range(depth):
        last = l == depth - 1
        lat = mods[l, :B].reshape(B, 6, D)
        cm = jnp.broadcast_to(mods[l, B].reshape(1, 6, D), (B, 6, D))
        modsel = jnp.stack([cm, lat], axis=1)

        (w_mq, w_mk, w_mv, w_mo, w_mg, w_aq, w_ak, w_av,
         w_rq, w_rk, w_rv, w_rg, w_gm, w_ga, w_gr) = _split_cols(w_in[l])
        wqk = jnp.concatenate([w_mq, w_mk], axis=1).astype(BF16)
        wr = jnp.concatenate([w_mv, w_aq, w_ak, w_av, w_rq, w_rk, w_rv], axis=1).astype(BF16)
        wgt = w_mg[:, gperm].T.astype(BF16)
        gbt = jnp.broadcast_to(m_gate_bias[l][gperm].reshape(-1, 1), (4 * M_HEADS, LANES))

        (hb, mq, mk, mv, gr, aq, ak, av, rq, rk, rv) = _pre_call(
            *stream, modsel, norm1_w[l].reshape(1, D), wqk, m_conv_w[l], wr, wgt, gbt, tabs)

        hm = _mlstm_call(mq, mk, mv, gr, n_ctx)
        sink_rows = jnp.repeat(a_sink[l].reshape(A_KV_HEADS, A_GROUP), A_BLOCK, axis=1)[:, None, :]
        ya = _attn_call(aq, ak, av, sink_rows.astype(F32) * LOG2E, L, not last)
        lgb = jnp.broadcast_to(ret_logit[l].T[:, :, None, None], (R_HEADS, 2, 8, R_V_DIM)).astype(F32)
        hr = _ret_call(rq, rk, rv, lgb, L // RCHUNK)

        t_off = 1 if last else 0
        wgates = jnp.concatenate([w_mo, w_rg, w_gm, w_ga, w_gr], axis=1).astype(BF16)
        x1 = _post_call(*stream, hb, hm, ya, hr, modsel, wgates, m_norm_w[l].reshape(1, -1),
                        ret_norm_w[l].reshape(1, -1), w_br_m[l].astype(BF16), w_br_a[l].astype(BF16),
                        w_br_r[l].astype(BF16), w_out[l].astype(BF16), t_off)
        xc = _ffn_call(x1, modsel, norm2_w[l].reshape(1, D), ffn_up[l][:, :FFN_DIM].astype(BF16),
                       ffn_up[l][:, FFN_DIM:].astype(BF16), ffn_conv_w[l], ffn_conv_b[l].reshape(1, -1),
                       ffn_down[l].astype(BF16), final_norm_w.reshape(1, D), not last, last)
        stream = (None, xc)
    return xc
```

```python
import functools

import jax
import jax.numpy as jnp
import numpy as np
from jax import lax
from jax.experimental import pallas as pl
from jax.experimental.pallas import tpu as pltpu

F32 = jnp.float32
BF16 = jnp.bfloat16

D_MODEL = 1024
GRID_W = 64
NORM_EPS = 1e-6
ROPE_BASE = 10000.0
NEG_INF = -1e30
LOG2E = 1.4426950408889634

M_HEADS = 4
M_HEAD_DIM = 128
M_WIDTH = M_HEADS * M_HEAD_DIM
A_HEADS = 8
A_KV_HEADS = 2
A_GROUP = A_HEADS // A_KV_HEADS
A_HEAD_DIM = 64
A_WIDTH = A_HEADS * A_HEAD_DIM
A_KV_WIDTH = A_KV_HEADS * A_HEAD_DIM
A_WINDOW = 128
A_BLOCK = 128
R_HEADS = 4
R_QK_DIM = 128
R_V_DIM = 256
R_QK_WIDTH = R_HEADS * R_QK_DIM
R_V_WIDTH = R_HEADS * R_V_DIM
FFN_DIM = 2816

IN_SPLITS = (M_WIDTH, M_WIDTH, M_WIDTH, M_WIDTH, 4 * M_HEADS,
             A_WIDTH, A_KV_WIDTH, A_KV_WIDTH,
             R_QK_WIDTH, R_QK_WIDTH, R_V_WIDTH, R_V_WIDTH,
             D_MODEL, D_MODEL, D_MODEL)

TM = 256
BB = 2
HALO = 8
CHUNK = 128
RCHUNK = 256
FFN_NC = 256
LANES = 128
VMEM_LIMIT = 56 * 1024 * 1024


def _sigmoid(x):
    return 1.0 / (1.0 + jnp.exp(-x))


def _log_sigmoid(x):
    return jnp.minimum(x, 0.0) - jnp.log1p(jnp.exp(-jnp.abs(x)))


def _dot(a, b):
    return jnp.dot(a, b, preferred_element_type=F32)


def _dot_nt(a, b):
    return lax.dot_general(a, b, (((1,), (1,)), ((), ())), preferred_element_type=F32)


def _dot_tn(a, b):
    return lax.dot_general(a, b, (((0,), (0,)), ((), ())), preferred_element_type=F32)


def _const_spec(shape):
    nd = len(shape)
    return pl.BlockSpec(shape, lambda *_: (0,) * nd, pipeline_mode=pl.Buffered(1))


def _params(sem):
    return pltpu.CompilerParams(dimension_semantics=sem, vmem_limit_bytes=VMEM_LIMIT)


def _mod_kernel(c_ref, w_ref, b_ref, o_ref):
    c = c_ref[...]
    s = (c * _sigmoid(c)).astype(BF16)
    o_ref[0] = _dot(s, w_ref[0].astype(BF16)) + b_ref[0]


def _modulation(cpad, mod_w, mod_b):
    depth, d, n = mod_w.shape
    rows = cpad.shape[0]
    bn = 1024
    return pl.pallas_call(
        _mod_kernel,
        grid=(depth, n // bn),
        in_specs=[pl.BlockSpec((rows, d), lambda l, j: (0, 0)),
                  pl.BlockSpec((1, d, bn), lambda l, j: (l, 0, j)),
                  pl.BlockSpec((1, 1, bn), lambda l, j: (l, 0, j))],
        out_specs=pl.BlockSpec((1, rows, bn), lambda l, j: (l, 0, j)),
        out_shape=jax.ShapeDtypeStruct((depth, rows, n), F32),
        compiler_params=_params(("parallel", "parallel")),
        name="modulation",
    )(cpad, mod_w, mod_b.reshape(depth, 1, n))


def _norm_mod(xv, nw, sc, sh):
    ms = jnp.mean(xv * xv, axis=-1, keepdims=True)
    return (xv * lax.rsqrt(ms + NORM_EPS) * nw) * (1.0 + sc) + sh


def _seg_cumsum(x, axis, reverse):
    n = x.shape[axis]
    idx = lax.broadcasted_iota(jnp.int32, x.shape, axis)
    s = 1
    while s < n:
        if reverse:
            shifted = pltpu.roll(x, n - s, axis)
            x = x + jnp.where(idx < n - s, shifted, 0.0)
        else:
            shifted = pltpu.roll(x, s, axis)
            x = x + jnp.where(idx >= s, shifted, 0.0)
        s *= 2
    return x


def _gate_transform(raw, gate_axis):
    tok_axis = 1 - gate_axis
    k = lax.broadcasted_iota(jnp.int32, raw.shape, gate_axis) % 4
    lf = _log_sigmoid(raw)
    cum_f = _seg_cumsum(jnp.where(k == 1, lf, 0.0), tok_axis, False)
    cum_b = _seg_cumsum(jnp.where(k == 3, lf, 0.0), tok_axis, True)
    bsel = jnp.where(k == 1, cum_f, jnp.where(k == 3, cum_b, 0.0))
    n = raw.shape[gate_axis]
    bnext = pltpu.roll(bsel, n - 1, gate_axis)
    nt = raw.shape[tok_axis]
    tot_f = lax.slice_in_dim(cum_f, nt - 1, nt, axis=tok_axis)
    tot_b = lax.slice_in_dim(cum_b, 0, 1, axis=tok_axis)
    total = jnp.broadcast_to(tot_f + tot_b, raw.shape)
    return jnp.where(k % 2 == 1, bsel, raw - bnext), total


def _pre_kernel(c_ref, x_ref, xp_ref, xn_ref, mod_ref, nw_ref, wqk_ref, cw_ref, wr_ref, wgt_ref,
                gbt_ref, ca_ref, sa_ref, cr_ref, sr_ref, *out_and_scratch, split):
    for i in range(x_ref.shape[0]):
        one = pl.ds(i, 1)
        _pre_tile(c_ref.at[one] if split else None,
                  x_ref.at[one], xp_ref.at[one], xn_ref.at[one], mod_ref.at[one], nw_ref, wqk_ref, cw_ref,
                  wr_ref, wgt_ref, gbt_ref, ca_ref, sa_ref, cr_ref, sr_ref,
                  *[r.at[one] for r in out_and_scratch[:-1]], out_and_scratch[-1].at[i])


def _tile_rows(c_ref, x_ref):
    if c_ref is None:
        return x_ref[0]
    return jnp.where(pl.program_id(0) == 0, c_ref[0], x_ref[0])


def _pre_tile(c_ref, x_ref, xp_ref, xn_ref, mod_ref, nw_ref, wqk_ref, cw_ref, wr_ref, wgt_ref,
              gbt_ref, ca_ref, sa_ref, cr_ref, sr_ref,
              hb_ref, mq_ref, mk_ref, mv_ref, gr_ref, aq_ref, ak_ref, av_ref,
              rq_ref, rk_ref, rv_ref, hext_ref):
    t = pl.program_id(0)
    nt = pl.num_programs(0)
    sh = mod_ref[0, 0, 0:1, :]
    sc = mod_ref[0, 0, 1:2, :]
    nw = nw_ref[...]
    hb = _norm_mod(_tile_rows(c_ref, x_ref), nw, sc, sh).astype(BF16)
    hb_ref[0] = hb

    grow = _dot_nt(wgt_ref[...], hb) + gbt_ref[:, 0:1]
    for c in range(TM // CHUNK):
        gr, tot = _gate_transform(grow[:, c * CHUNK:(c + 1) * CHUNK], 0)
        aend = gr + pltpu.roll(tot, 4 * M_HEADS - 1, 0)
        amax = jnp.broadcast_to(jnp.max(aend, axis=1, keepdims=True), aend.shape)
        for h in range(M_HEADS):
            hp_, o0 = h // 2, 8 * (h % 2)
            gr_ref[0, hp_, c, o0:o0 + 4, :] = gr[4 * h:4 * h + 4]
            for d in range(2):
                gr_ref[0, hp_, c, o0 + 4 + d:o0 + 5 + d, :] = tot[4 * h + 2 * d + 1:4 * h + 2 * d + 2]
                gr_ref[0, hp_, c, o0 + 6 + d:o0 + 7 + d, :] = amax[4 * h + 2 * d:4 * h + 2 * d + 1]

    hp = _norm_mod(xp_ref[0], nw, sc, sh)
    hn = _norm_mod(xn_ref[0], nw, sc, sh)
    hp = jnp.where(t <= 1, 0.0, hp)
    hn = jnp.where((t == 0) | (t == nt - 1), 0.0, hn)
    hext_ref[0:TM, :] = hb
    hext_ref[TM:, :] = jnp.concatenate([hn, hp], axis=0).astype(BF16)
    hext = hext_ref[...]
    ext = TM + 2 * HALO

    for j, (dst, scale) in enumerate(((mq_ref, 1.0), (mk_ref, M_HEAD_DIM ** -0.5))):
        p = _dot(hext, wqk_ref[:, j * M_WIDTH:(j + 1) * M_WIDTH])
        cw = cw_ref[:, j * M_WIDTH:(j + 1) * M_WIDTH]
        prev = pltpu.roll(p, 1, 0)[0:TM]
        nxt = pltpu.roll(p, ext - 1, 0)[0:TM]
        conv = prev * cw[0:1] + p[0:TM] * cw[1:2] + nxt * cw[2:3]
        act = conv * _sigmoid(conv) * scale
        for h in range(M_HEADS):
            dst[0, h] = act[:, h * M_HEAD_DIM:(h + 1) * M_HEAD_DIM].astype(BF16)

    off = 0
    p = _dot(hb, wr_ref[:, off:off + M_WIDTH])
    off += M_WIDTH
    for h in range(M_HEADS):
        for c in range(TM // CHUNK):
            blk = p[c * CHUNK:(c + 1) * CHUNK, h * M_HEAD_DIM:(h + 1) * M_HEAD_DIM]
            mv_ref[0, h, c] = blk.T.astype(BF16)

    lane = lax.broadcasted_iota(jnp.int32, (TM, LANES), 1)
    first16 = (lane % 32) < 16
    ca = ca_ref[...]
    sa = sa_ref[...]

    def rope_a(xs):
        partner = jnp.where(first16, pltpu.roll(xs, LANES - 16, 1), pltpu.roll(xs, 16, 1))
        return xs * ca + partner * sa

    p = _dot(hb, wr_ref[:, off:off + A_WIDTH])
    off += A_WIDTH
    for s in range(A_WIDTH // LANES):
        r = (rope_a(p[:, s * LANES:(s + 1) * LANES]) * (A_HEAD_DIM ** -0.5 * LOG2E)).astype(BF16)
        aq_ref[0, 2 * s] = r[:, 0:A_HEAD_DIM]
        aq_ref[0, 2 * s + 1] = r[:, A_HEAD_DIM:]
    p = _dot(hb, wr_ref[:, off:off + 2 * A_KV_WIDTH])
    off += 2 * A_KV_WIDTH
    r = rope_a(p[:, 0:LANES]).astype(BF16)
    for h in range(A_KV_HEADS):
        ak_ref[0, h] = r[:, h * A_HEAD_DIM:(h + 1) * A_HEAD_DIM]
    pv = p[:, LANES:]
    for h in range(A_KV_HEADS):
        vh = pv if h == 0 else pltpu.roll(pv, A_HEAD_DIM, 1)
        ve = jnp.where(lane < A_HEAD_DIM, vh, 1.0)
        for c in range(TM // A_BLOCK):
            av_ref[0, h, c] = ve[c * A_BLOCK:(c + 1) * A_BLOCK].T.astype(BF16)

    cr = cr_ref[...]
    sr = sr_ref[...]
    for is_k in (False, True):
        p = _dot(hb, wr_ref[:, off:off + R_QK_WIDTH])
        off += R_QK_WIDTH
        for h in range(R_HEADS):
            xs = p[:, h * R_QK_DIM:(h + 1) * R_QK_DIM]
            rot = xs * cr + pltpu.roll(xs, R_QK_DIM // 2, 1) * sr
            if is_k:
                rot = rot * R_QK_DIM ** -0.5
                for c in range(TM // CHUNK):
                    rk_ref[0, h, c] = rot[c * CHUNK:(c + 1) * CHUNK].T.astype(BF16)
            else:
                rq_ref[0, h] = rot.astype(BF16)
    p = _dot(hb, wr_ref[:, off:off + R_V_WIDTH])
    for h in range(R_HEADS):
        rv_ref[0, h] = p[:, h * R_V_DIM:(h + 1) * R_V_DIM].astype(BF16)


def _stream_specs(ctx, xs):
    hb = TM // HALO
    D = xs.shape[-1]
    off = 0 if ctx is None else 1
    last = xs.shape[1] // HALO - 1
    return [
        pl.BlockSpec((BB, TM if off else HALO, D), lambda t, b: (b, 0, 0)),
        pl.BlockSpec((BB, TM, D), lambda t, b: (b, jnp.maximum(t - off, 0), 0)),
        pl.BlockSpec((BB, HALO, D), lambda t, b: (b, jnp.maximum((t - off) * hb - 1, 0), 0)),
        pl.BlockSpec((BB, HALO, D), lambda t, b: (b, jnp.clip((t - off + 1) * hb, 0, last), 0)),
    ]


def _pre_call(ctx, xs, modsel, nw, wqk, cw, wr, wgt, gbt, tabs):
    B, _, D = xs.shape
    T = xs.shape[1] + (0 if ctx is None else ctx.shape[1])
    nt = T // TM
    nch = T // CHUNK

    def tile(t, b):
        return (b, t, 0)

    def head_tile(t, b):
        return (b, 0, t, 0)

    in_specs = _stream_specs(ctx, xs) + [
        pl.BlockSpec((BB, 1, 6, D), lambda t, b: (b, jnp.minimum(t, 1), 0, 0)),
        _const_spec(nw.shape), _const_spec(wqk.shape), _const_spec(cw.shape), _const_spec(wr.shape),
        _const_spec(wgt.shape), _const_spec(gbt.shape),
    ] + [pl.BlockSpec((TM, LANES), lambda t, b: (t, 0)) for _ in range(4)]
    out_shape = [
        jax.ShapeDtypeStruct((B, T, D), BF16),
        jax.ShapeDtypeStruct((B, M_HEADS, T, M_HEAD_DIM), BF16),
        jax.ShapeDtypeStruct((B, M_HEADS, T, M_HEAD_DIM), BF16),
        jax.ShapeDtypeStruct((B, M_HEADS, nch, M_HEAD_DIM, CHUNK), BF16),
        jax.ShapeDtypeStruct((B, 2, nch, 16, CHUNK), F32),
        jax.ShapeDtypeStruct((B, A_HEADS, T, A_HEAD_DIM), BF16),
        jax.ShapeDtypeStruct((B, A_KV_HEADS, T, A_HEAD_DIM), BF16),
        jax.ShapeDtypeStruct((B, A_KV_HEADS, T // A_BLOCK, LANES, A_BLOCK), BF16),
        jax.ShapeDtypeStruct((B, R_HEADS, T, R_QK_DIM), BF16),
        jax.ShapeDtypeStruct((B, R_HEADS, nch, R_QK_DIM, CHUNK), BF16),
        jax.ShapeDtypeStruct((B, R_HEADS, T, R_V_DIM), BF16),
    ]
    out_specs = [
        pl.BlockSpec((BB, TM, D), tile),
        pl.BlockSpec((BB, M_HEADS, TM, M_HEAD_DIM), head_tile),
        pl.BlockSpec((BB, M_HEADS, TM, M_HEAD_DIM), head_tile),
        pl.BlockSpec((BB, M_HEADS, TM // CHUNK, M_HEAD_DIM, CHUNK), lambda t, b: (b, 0, t, 0, 0)),
        pl.BlockSpec((BB, 2, TM // CHUNK, 16, CHUNK), lambda t, b: (b, 0, t, 0, 0)),
        pl.BlockSpec((BB, A_HEADS, TM, A_HEAD_DIM), head_tile),
        pl.BlockSpec((BB, A_KV_HEADS, TM, A_HEAD_DIM), head_tile),
        pl.BlockSpec((BB, A_KV_HEADS, TM // A_BLOCK, LANES, A_BLOCK), lambda t, b: (b, 0, t, 0, 0)),
        pl.BlockSpec((BB, R_HEADS, TM, R_QK_DIM), head_tile),
        pl.BlockSpec((BB, R_HEADS, TM // CHUNK, R_QK_DIM, CHUNK), lambda t, b: (b, 0, t, 0, 0)),
        pl.BlockSpec((BB, R_HEADS, TM, R_V_DIM), head_tile),
    ]
    return pl.pallas_call(
        functools.partial(_pre_kernel, split=ctx is not None),
        grid=(nt, B // BB),
        in_specs=in_specs,
        out_specs=out_specs,
        out_shape=out_shape,
        scratch_shapes=[pltpu.VMEM((BB, TM + 2 * HALO, D), BF16)],
        compiler_params=_params(("parallel", "parallel")),
        name="pre_proj",
    )(xs if ctx is None else ctx, xs, xs, xs, modsel, nw, wqk, cw, wr, wgt, gbt, *tabs)


def _bwd_chunk(i, n_ctx, n_all):
    if isinstance(i, int):
        return n_ctx - 1 - i if i < n_ctx else n_all - 1 + n_ctx - i
    return n_all - 1 + n_ctx - i


def _chunk_start(c):
    return c * CHUNK if isinstance(c, int) else pl.multiple_of(c * CHUNK, CHUNK)


def _chunk_loop(body, n_ctx, n_all, unroll):
    for i in range(n_ctx):
        body(i, 0)
    n_lat = n_all - n_ctx
    while n_lat % unroll:
        unroll //= 2
    lax.fori_loop(n_ctx, n_all, body, 0, unroll=unroll)


def _mlstm_kernel(q_ref, k_ref, vt_ref, gr_ref, o_ref, c_ref, n_ref, m_ref,
                  cs_ref, ns_ref, ms_ref, qp_ref, nq_ref, *, n_ctx):
    n_all = gr_ref.shape[2]
    dh = q_ref.shape[3]
    c_ref[...] = jnp.zeros(c_ref.shape, F32)
    n_ref[...] = jnp.zeros(n_ref.shape, F32)
    m_ref[...] = jnp.zeros(m_ref.shape, F32)

    def scan_unit(c, hh, d):
        u = 2 * hh + d
        c0 = _chunk_start(c)
        k = k_ref[0, hh, pl.ds(c0, CHUNK), :]
        vt = vt_ref[0, hh, c]
        rows = gr_ref[0, 0, c, 8 * hh:8 * hh + 8, :]
        tot = rows[4 + d:5 + d]
        aend = rows[2 * d:2 * d + 1] + tot
        amax = rows[6 + d:7 + d]
        cst = c_ref[u]
        nv = n_ref[u]
        m = m_ref[u]
        cs_ref[u, c] = cst.astype(BF16)
        ns_ref[u, c] = jnp.broadcast_to(nv, (16, dh)).astype(BF16)
        ms_ref[u, c] = jnp.broadcast_to(m, (8, CHUNK))
        m_new = jnp.maximum(tot + m, amax)
        w = jnp.exp(aend - m_new)
        dec = jnp.exp(tot + m - m_new)
        vw = (vt.astype(F32) * w).astype(BF16)
        c_ref[u] = dec * cst + _dot(vw, k)
        wk = _dot(jnp.broadcast_to(w, (16, CHUNK)).astype(BF16), k)
        n_ref[u] = dec * nv + wk[0:1]
        m_ref[u] = m_new

    def scan_step(i, carry):
        cb = _bwd_chunk(i, n_ctx, n_all)
        for hh in range(2):
            scan_unit(i, hh, 0)
            scan_unit(cb, hh, 1)
        return carry

    _chunk_loop(scan_step, n_ctx, n_all, 8)

    row = lax.broadcasted_iota(jnp.int32, (CHUNK, CHUNK), 0)
    col = lax.broadcasted_iota(jnp.int32, (CHUNK, CHUNK), 1)
    tri = (row <= col, row >= col)

    def query_products(c, slot):
        c0 = _chunk_start(c)
        for hh in range(2):
            q = q_ref[0, hh, pl.ds(c0, CHUNK), :]
            k = k_ref[0, hh, pl.ds(c0, CHUNK), :]
            qp_ref[slot, hh, 0] = _dot_nt(k, q)
            for d in range(2):
                u = 2 * hh + d
                qp_ref[slot, hh, 1 + d] = _dot_nt(cs_ref[u, c], q)
                nq_ref[slot, u] = _dot_nt(ns_ref[u, c], q)[0:8]

    def outputs(c, slot):
        c0 = _chunk_start(c)
        gcv = jnp.concatenate([gr_ref[0, 0, c], jnp.zeros((CHUNK - 16, CHUNK), F32)], axis=0).T
        for hh in range(2):
            vt = vt_ref[0, hh, c]
            rows = gr_ref[0, 0, c, 8 * hh:8 * hh + 8, :]
            st = qp_ref[slot, hh, 0]
            sds, iws, invs = [], [], []
            for d in range(2):
                u = 2 * hh + d
                a0c = gcv[:, 8 * hh + 2 * d:8 * hh + 2 * d + 1]
                b = rows[2 * d + 1:2 * d + 2]
                m = ms_ref[u, c, 0:1, :]
                dlog = jnp.where(tri[d], a0c + b, NEG_INF)
                mj = jnp.maximum(b + m, jnp.max(dlog, axis=0, keepdims=True))
                sd = st * jnp.exp(dlog - mj)
                iw = jnp.exp(b + m - mj)
                den = iw * nq_ref[slot, u, 0:1, :] + jnp.sum(sd, axis=0, keepdims=True)
                sds.append(sd.astype(BF16))
                iws.append(iw)
                invs.append(1.0 / jnp.maximum(jnp.abs(den), jnp.exp(-mj)))
            num = _dot(vt, jnp.concatenate(sds, axis=1))
            acc = None
            for d in range(2):
                ht = (num[:, d * CHUNK:(d + 1) * CHUNK] + iws[d] * qp_ref[slot, hh, 1 + d]) * invs[d]
                acc = ht if acc is None else acc + ht
            o_ref[0, hh, pl.ds(c0, CHUNK), :] = acc.T

    query_products(0, 0)

    def out_pair(i, carry):
        c = 2 * i
        query_products(c + 1, 1)
        outputs(c, 0)
        query_products(jnp.minimum(c + 2, n_all - 1), 0)
        outputs(c + 1, 1)
        return carry

    n_pairs = n_all // 2
    unroll = 4
    first = n_pairs % unroll
    for i in range(first):
        out_pair(i, 0)
    lax.fori_loop(first, n_pairs, out_pair, 0, unroll=unroll)


def _mlstm_call(mq, mk, mvt, gr, n_ctx):
    B, H, T, dh = mq.shape
    nch = T // CHUNK
    qspec = pl.BlockSpec((1, 2, T, dh), lambda b, p: (b, p, 0, 0))
    return pl.pallas_call(
        functools.partial(_mlstm_kernel, n_ctx=n_ctx),
        grid=(B, H // 2),
        in_specs=[qspec, qspec,
                  pl.BlockSpec((1, 2, nch, dh, CHUNK), lambda b, p: (b, p, 0, 0, 0)),
                  pl.BlockSpec((1, 1, nch, 16, CHUNK), lambda b, p: (b, p, 0, 0, 0))],
        out_specs=pl.BlockSpec((1, 2, T, dh), lambda b, p: (b, p, 0, 0)),
        out_shape=jax.ShapeDtypeStruct((B, H, T, dh), F32),
        scratch_shapes=[pltpu.VMEM((4, dh, dh), F32), pltpu.VMEM((4, 1, dh), F32),
                        pltpu.VMEM((4, 1, CHUNK), F32),
                        pltpu.VMEM((4, nch, dh, dh), BF16), pltpu.VMEM((4, nch, 16, dh), BF16),
                        pltpu.VMEM((4, nch, 8, CHUNK), F32),
                        pltpu.VMEM((2, 2, 3, dh, CHUNK), F32), pltpu.VMEM((2, 4, 8, CHUNK), F32)],
        compiler_params=_params(("parallel", "parallel")),
        name="mlstm",
    )(mq, mk, mvt, gr)


def _ret_kernel(q_ref, kt_ref, v_ref, lg_ref, o_ref, st_ref, sts_ref, dec_ref, ib_ref, wb_ref, cd_ref,
                *, n_ctx):
    sub = RCHUNK // CHUNK
    n_all = kt_ref.shape[2] // sub
    dk = q_ref.shape[3]
    dv = v_ref.shape[3]
    st_ref[...] = jnp.zeros(st_ref.shape, F32)

    @pl.when(pl.program_id(1) == 0)
    def _():
        row = lax.broadcasted_iota(jnp.int32, (RCHUNK, RCHUNK), 0).astype(F32)
        col = lax.broadcasted_iota(jnp.int32, (RCHUNK, RCHUNK), 1).astype(F32)
        kcol = lax.broadcasted_iota(jnp.int32, (dk, RCHUNK), 1).astype(F32)
        dec = None
        for d in range(2):
            lg = _log_sigmoid(lg_ref[0, d])[0:1]
            diff = row - col if d == 0 else col - row
            dd = jnp.where(diff >= 0, jnp.exp(lg * jnp.maximum(diff, 0.0)), 0.0)
            dec = dd if dec is None else dec + dd
            pin = row if d == 0 else RCHUNK - 1.0 - row
            ib_ref[d] = jnp.exp(lg * (pin + 1.0))
            pkey = kcol if d == 0 else RCHUNK - 1.0 - kcol
            wb_ref[d] = jnp.exp(lg * (RCHUNK - 1.0 - pkey))
            cd_ref[d] = jnp.broadcast_to(jnp.exp(lg * RCHUNK), (8, dv))
        dec_ref[...] = dec

    cds = [cd_ref[d, 0:1, :] for d in range(2)]

    def rows(c):
        r0 = c * RCHUNK
        return pl.ds(r0 if isinstance(c, int) else pl.multiple_of(r0, RCHUNK), RCHUNK)

    def k_t(c):
        return jnp.concatenate([kt_ref[0, 0, sub * c + j] for j in range(sub)], axis=1)

    def scan_step(i, carry):
        cb = _bwd_chunk(i, n_ctx, n_all)
        for d, c in ((0, i), (1, cb)):
            st = st_ref[d]
            sts_ref[c, :, d * dv:(d + 1) * dv] = st.astype(BF16)
            kw = (k_t(c).astype(F32) * wb_ref[d]).astype(BF16)
            st_ref[d] = cds[d] * st + _dot(kw, v_ref[0, 0, rows(c), :])
        return carry

    _chunk_loop(scan_step, n_ctx, n_all, 16)

    def out_step(c, carry):
        q = q_ref[0, 0, rows(c), :]
        s = (_dot(q, k_t(c)) * dec_ref[...]).astype(BF16)
        qst = _dot(q, sts_ref[c])
        o_ref[0, 0, rows(c), :] = (_dot(s, v_ref[0, 0, rows(c), :])
                                   + ib_ref[0] * qst[:, 0:dv] + ib_ref[1] * qst[:, dv:])
        return carry

    _chunk_loop(out_step, n_ctx, n_all, 8)


def _ret_call(rq, rkt, rv, lgb, n_ctx):
    B, H, T, dk = rq.shape
    dv = rv.shape[-1]
    nch = T // CHUNK
    assert dv == RCHUNK and dk <= RCHUNK and T % RCHUNK == 0
    return pl.pallas_call(
        functools.partial(_ret_kernel, n_ctx=n_ctx),
        grid=(H, B),
        in_specs=[pl.BlockSpec((1, 1, T, dk), lambda h, b: (b, h, 0, 0)),
                  pl.BlockSpec((1, 1, nch, dk, CHUNK), lambda h, b: (b, h, 0, 0, 0)),
                  pl.BlockSpec((1, 1, T, dv), lambda h, b: (b, h, 0, 0)),
                  pl.BlockSpec((1, 2, 8, dv), lambda h, b: (h, 0, 0, 0))],
        out_specs=pl.BlockSpec((1, 1, T, dv), lambda h, b: (b, h, 0, 0)),
        out_shape=jax.ShapeDtypeStruct((B, H, T, dv), F32),
        scratch_shapes=[pltpu.VMEM((2, dk, dv), F32), pltpu.VMEM((T // RCHUNK, dk, 2 * dv), BF16),
                        pltpu.VMEM((RCHUNK, RCHUNK), F32), pltpu.VMEM((2, RCHUNK, dv), F32),
                        pltpu.VMEM((2, dk, RCHUNK), F32), pltpu.VMEM((2, 8, dv), F32)],
        compiler_params=_params(("parallel", "arbitrary")),
        name="retention",
    )(rq, rkt, rv, lgb)


def _attn_kernel(q_ref, k_ref, vt_ref, sink_ref, o_ref, bias_ref, s_ref, *, n_ctx_tok, need_ctx):
    T = q_ref.shape[2]
    L = n_ctx_tok
    S = T - L
    nb = S // A_BLOCK
    nspan = 3
    span = nspan * A_BLOCK
    ncb = L // A_BLOCK
    cols = A_GROUP * A_BLOCK
    kc = k_ref[0, 0, 0:L, :]
    vtc = jnp.concatenate([vt_ref[0, 0, j] for j in range(ncb)], axis=1)
    sink = sink_ref[0]

    @pl.when((pl.program_id(0) == 0) & (pl.program_id(1) == 0))
    def _():
        kpos = lax.broadcasted_iota(jnp.int32, (span, A_BLOCK), 0)
        qpos = lax.broadcasted_iota(jnp.int32, (span, A_BLOCK), 1)
        for delta in range(nspan):
            bias_ref[delta] = jnp.where(jnp.abs(qpos + delta * A_BLOCK - kpos) <= A_WINDOW, 0.0, NEG_INF)

    def finish(blk, acc, m):
        den = acc[A_HEAD_DIM:A_HEAD_DIM + 1] + jnp.exp2(sink - m)
        ot = (acc[0:A_HEAD_DIM] * (1.0 / den)).astype(BF16)
        for g in range(A_GROUP):
            o_ref[0, blk, g * A_HEAD_DIM:(g + 1) * A_HEAD_DIM, :] = ot[:, g * A_BLOCK:(g + 1) * A_BLOCK]

    for cbk in range(ncb):
        if need_ctx:
            qs = q_ref[0, :, cbk * A_BLOCK:(cbk + 1) * A_BLOCK, :].reshape(cols, A_HEAD_DIM)
            s = _dot_nt(kc, qs)
            m = jnp.maximum(jnp.max(s, axis=0, keepdims=True), sink)
            finish(cbk, _dot(vtc, jnp.exp2(s - m).astype(BF16)), m)
        else:
            o_ref[0, cbk] = jnp.zeros(o_ref.shape[2:], BF16)

    def first_key_block(n):
        return jnp.clip(n - 1, 0, nb - nspan)

    def scores(n, slot):
        q0 = pl.multiple_of(L + n * A_BLOCK, A_BLOCK)
        kb = first_key_block(n)
        k0 = pl.multiple_of(L + kb * A_BLOCK, A_BLOCK)
        qs = q_ref[0, :, pl.ds(q0, A_BLOCK), :].reshape(cols, A_HEAD_DIM)
        bias = bias_ref[n - kb]
        s_ref[slot, 0:span, :] = (_dot_nt(k_ref[0, 0, pl.ds(k0, span), :], qs)
                                  + jnp.concatenate([bias] * A_GROUP, axis=1))
        s_ref[slot, span:, :] = _dot_nt(kc, qs)

    def softmax_pv(n, slot):
        kb = first_key_block(n)
        s = s_ref[slot]
        m = jnp.maximum(jnp.max(s, axis=0, keepdims=True), sink)
        p = jnp.exp2(s - m).astype(BF16)
        vt = jnp.concatenate([vt_ref[0, 0, ncb + kb + j] for j in range(nspan)] + [vtc], axis=1)
        finish(ncb + n, _dot(vt, p), m)

    scores(0, 0)

    def block_pair(i, carry):
        n = 2 * i
        scores(n + 1, 1)
        softmax_pv(n, 0)
        scores(jnp.minimum(n + 2, nb - 1), 0)
        softmax_pv(n + 1, 1)
        return carry

    n_pairs = nb // 2
    unroll = 4
    first = n_pairs % unroll
    for i in range(first):
        block_pair(i, 0)
    lax.fori_loop(first, n_pairs, block_pair, 0, unroll=unroll)


def _attn_call(aq, ak, avt, sink_rows, n_ctx_tok, need_ctx):
    B, _, T, hd = aq.shape
    nblk = T // A_BLOCK
    cols = A_GROUP * A_BLOCK
    return pl.pallas_call(
        functools.partial(_attn_kernel, n_ctx_tok=n_ctx_tok, need_ctx=need_ctx),
        grid=(B, A_KV_HEADS),
        in_specs=[pl.BlockSpec((1, A_GROUP, T, hd), lambda b, h: (b, h, 0, 0)),
                  pl.BlockSpec((1, 1, T, hd), lambda b, h: (b, h, 0, 0)),
                  pl.BlockSpec((1, 1, nblk, LANES, A_BLOCK), lambda b, h: (b, h, 0, 0, 0)),
                  pl.BlockSpec((1, 1, cols), lambda b, h: (h, 0, 0))],
        out_specs=pl.BlockSpec((1, nblk, A_GROUP * hd, A_BLOCK), lambda b, h: (b, 0, h, 0)),
        out_shape=jax.ShapeDtypeStruct((B, nblk, A_WIDTH, A_BLOCK), BF16),
        scratch_shapes=[pltpu.VMEM((3, 3 * A_BLOCK, A_BLOCK), F32),
                        pltpu.VMEM((2, 3 * A_BLOCK + n_ctx_tok, cols), F32)],
        compiler_params=_params(("arbitrary", "arbitrary")),
        name="window_attn",
    )(aq, ak, avt, sink_rows)


def _head_norm(y):
    mu = jnp.mean(y, axis=-1, keepdims=True)
    yc = y - mu
    var = jnp.mean(yc * yc, axis=-1, keepdims=True)
    return yc * lax.rsqrt(var + NORM_EPS)


def _post_kernel(c_ref, x_ref, hb_ref, hm_ref, ya_ref, hr_ref, mod_ref, wg_ref, mnw_ref, rnw_ref,
                 wbm_ref, wba_ref, wbr_ref, wo_ref, o_ref, *, split):
    for i in range(x_ref.shape[0]):
        hb = hb_ref[i]
        off = 0
        hm = jnp.concatenate([_head_norm(hm_ref[i, h]) for h in range(M_HEADS)], axis=1) * mnw_ref[...]
        ym = (_sigmoid(_dot(hb, wg_ref[:, off:off + M_WIDTH])) * hm).astype(BF16)
        off += M_WIDTH
        hr = jnp.concatenate([_head_norm(hr_ref[i, h]) for h in range(R_HEADS)], axis=1) * rnw_ref[...]
        rg = _dot(hb, wg_ref[:, off:off + R_V_WIDTH])
        yr = (rg * _sigmoid(rg) * hr).astype(BF16)
        off += R_V_WIDTH
        z = _sigmoid(_dot(hb, wg_ref[:, off:off + D_MODEL])) * _dot(ym, wbm_ref[...])
        off += D_MODEL
        pa = jnp.concatenate([_dot_tn(ya_ref[i, c], wba_ref[...]) for c in range(TM // A_BLOCK)], axis=0)
        z = z + _sigmoid(_dot(hb, wg_ref[:, off:off + D_MODEL])) * pa
        off += D_MODEL
        z = z + _sigmoid(_dot(hb, wg_ref[:, off:off + D_MODEL])) * _dot(yr, wbr_ref[...])
        y = _dot(z.astype(BF16), wo_ref[...])
        one = pl.ds(i, 1)
        xres = _tile_rows(c_ref.at[one] if split else None, x_ref.at[one])
        o_ref[i] = xres + mod_ref[i, 0, 2:3, :] * y


def _post_call(ctx, xs, hb, hm, ya, hr, modsel, wg, mnw, rnw, wbm, wba, wbr, wo, t_off):
    B, T, D = hb.shape
    nt = T // TM - t_off
    split = ctx is not None
    assert not (split and t_off)

    def tile(t, b):
        return (b, t + t_off, 0)

    def head_tile(t, b):
        return (b, 0, t + t_off, 0)

    return pl.pallas_call(
        functools.partial(_post_kernel, split=split),
        grid=(nt, B // BB),
        in_specs=[pl.BlockSpec((BB, TM if split else HALO, D), lambda t, b: (b, 0, 0)),
                  pl.BlockSpec((BB, TM, D), (lambda t, b: (b, jnp.maximum(t - 1, 0), 0)) if split else tile),
                  pl.BlockSpec((BB, TM, D), tile),
                  pl.BlockSpec((BB, M_HEADS, TM, M_HEAD_DIM), head_tile),
                  pl.BlockSpec((BB, TM // A_BLOCK, A_WIDTH, A_BLOCK), lambda t, b: (b, t + t_off, 0, 0)),
                  pl.BlockSpec((BB, R_HEADS, TM, R_V_DIM), head_tile),
                  pl.BlockSpec((BB, 1, 6, D), lambda t, b: (b, jnp.minimum(t + t_off, 1), 0, 0)),
                  _const_spec(wg.shape), _const_spec(mnw.shape), _const_spec(rnw.shape),
                  _const_spec(wbm.shape), _const_spec(wba.shape), _const_spec(wbr.shape),
                  _const_spec(wo.shape)],
        out_specs=pl.BlockSpec((BB, TM, D), lambda t, b: (b, t, 0)),
        out_shape=jax.ShapeDtypeStruct((B, nt * TM, D), F32),
        compiler_params=_params(("parallel", "parallel")),
        name="post_merge",
    )(ctx if split else xs, xs, hb, hm, ya, hr, modsel, wg, mnw, rnw, wbm, wba, wbr, wo)


def _ffn_kernel(x_ref, xp_ref, xn_ref, mod_ref, nw_ref, wa_ref, wb_ref, cw_ref, cb_ref, wd_ref, fw_ref,
                o_ref, hext_ref, u_ref, *, has_ctx, final):
    t = pl.program_id(0)
    nt = pl.num_programs(0)
    first_lat = 1 if has_ctx else 0
    nw = nw_ref[...]
    ext = TM + 2 * HALO
    for i in range(x_ref.shape[0]):
        sh = mod_ref[i, 0, 3:4, :]
        sc = mod_ref[i, 0, 4:5, :]
        g2 = mod_ref[i, 0, 5:6, :]
        x = x_ref[i]
        hp = _norm_mod(xp_ref[i], nw, sc, sh)
        hn = _norm_mod(xn_ref[i], nw, sc, sh)
        hp = jnp.where(t <= first_lat, 0.0, hp)
        hn = jnp.where((t == first_lat - 1) | (t == nt - 1), 0.0, hn)
        hext_ref[i, 0:TM, :] = _norm_mod(x, nw, sc, sh).astype(BF16)
        hext_ref[i, TM:, :] = jnp.concatenate([hn, hp], axis=0).astype(BF16)
        for j in range(FFN_DIM // FFN_NC):
            cs = slice(j * FFN_NC, (j + 1) * FFN_NC)
            a = _dot(hext_ref[i], wa_ref[:, cs])
            b = _dot(hext_ref[i, 0:TM, :], wb_ref[:, cs])
            cw = cw_ref[:, cs]
            prev = pltpu.roll(a, 1, 0)[0:TM]
            nxt = pltpu.roll(a, ext - 1, 0)[0:TM]
            conv = prev * cw[0:1] + a[0:TM] * cw[1:2] + nxt * cw[2:3] + cb_ref[:, cs]
            u_ref[i, :, cs] = (conv * _sigmoid(conv) * b).astype(BF16)
        y = x + g2 * _dot(u_ref[i], wd_ref[...])
        if final:
            ms = jnp.mean(y * y, axis=-1, keepdims=True)
            y = y * lax.rsqrt(ms + NORM_EPS) * fw_ref[...]
        o_ref[i] = y


def _ffn_call(xs, modsel, nw, wa, wb, cw, cb, wd, fw, has_ctx, final):
    B, T, D = xs.shape
    nt = T // TM
    hb16 = TM // HALO
    first_lat = 1 if has_ctx else 0
    tile = pl.BlockSpec((BB, TM, D), lambda t, b: (b, t, 0))
    return pl.pallas_call(
        functools.partial(_ffn_kernel, has_ctx=has_ctx, final=final),
        grid=(nt, B // BB),
        in_specs=[tile,
                  pl.BlockSpec((BB, HALO, D), lambda t, b: (b, jnp.maximum(t * hb16 - 1, 0), 0)),
                  pl.BlockSpec((BB, HALO, D), lambda t, b: (b, jnp.minimum((t + 1) * hb16, T // HALO - 1), 0)),
                  pl.BlockSpec((BB, 1, 6, D), lambda t, b: (b, jnp.minimum(t + 1 - first_lat, 1), 0, 0)),
                  _const_spec(nw.shape), _const_spec(wa.shape), _const_spec(wb.shape),
                  _const_spec(cw.shape), _const_spec(cb.shape), _const_spec(wd.shape),
                  _const_spec(fw.shape)],
        out_specs=tile,
        out_shape=jax.ShapeDtypeStruct((B, T, D), F32),
        scratch_shapes=[pltpu.VMEM((BB, TM + 2 * HALO, D), BF16), pltpu.VMEM((BB, TM, FFN_DIM), BF16)],
        compiler_params=_params(("parallel", "parallel")),
        name="conv_ffn",
    )(xs, xs, xs, modsel, nw, wa, wb, cw, cb, wd, fw)


def _split_cols(w):
    out = []
    acc = 0
    for s in IN_SPLITS:
        out.append(w[:, acc:acc + s])
        acc += s
    return out


def _rope_tables(L, S):
    T = L + S
    f32 = np.float32
    j = np.arange(LANES)
    jj = j % 32
    inv_a = np.power(f32(ROPE_BASE), -(jj % 16).astype(f32) / f32(16.0)).astype(f32)
    t = np.arange(S)
    pos = np.where(((j % A_HEAD_DIM) // 32 == 0)[None, :], (t // GRID_W)[:, None], (t % GRID_W)[:, None])
    ang = pos.astype(f32) * inv_a[None, :]
    sign = np.where(jj < 16, f32(-1.0), f32(1.0))[None, :]
    ca = np.concatenate([np.ones((L, LANES), f32), np.cos(ang)], axis=0)
    sa = np.concatenate([np.zeros((L, LANES), f32), np.sin(ang) * sign], axis=0)
    half = R_QK_DIM // 2
    inv_r = np.power(f32(ROPE_BASE), -(j % half).astype(f32) / f32(half)).astype(f32)
    angr = np.arange(T).astype(f32)[:, None] * inv_r[None, :]
    signr = np.where(j < half, f32(-1.0), f32(1.0))[None, :]
    return tuple(jnp.asarray(a, F32) for a in (ca, sa, np.cos(angr), np.sin(angr) * signr))


def kernel(x, c, ctx, c_ctx, mod_w, mod_b, norm1_w, norm2_w, w_in, m_gate_bias, m_conv_w, m_norm_w,
           a_sink, ret_logit, ret_norm_w, w_br_m, w_br_a, w_br_r, w_out, ffn_up, ffn_conv_w, ffn_conv_b,
           ffn_down, final_norm_w):
    B, S, D = x.shape
    L = ctx.shape[1]
    depth = mod_w.shape[0]
    assert D == D_MODEL and L == TM and S % TM == 0 and S >= 3 * A_BLOCK and B % BB == 0
    T = L + S
    n_ctx = L // CHUNK

    rows = -(-(B + 1) // 8) * 8
    cpad = jnp.zeros((rows, D), F32).at[:B].set(c).at[B].set(c_ctx)
    mods = _modulation(cpad, mod_w, mod_b)
    tabs = _rope_tables(L, S)
    stream = (ctx, x) if depth > 1 else (None, jnp.concatenate([ctx, x], axis=1))

    gperm = jnp.arange(4 * M_HEADS).reshape(4, M_HEADS).T.reshape(-1)

    for l in range(depth):
        last = l == depth - 1
        lat = mods[l, :B].reshape(B, 6, D)
        cm = jnp.broadcast_to(mods[l, B].reshape(1, 6, D), (B, 6, D))
        modsel = jnp.stack([cm, lat], axis=1)

        (w_mq, w_mk, w_mv, w_mo, w_mg, w_aq, w_ak, w_av,
         w_rq, w_rk, w_rv, w_rg, w_gm, w_ga, w_gr) = _split_cols(w_in[l])
        wqk = jnp.concatenate([w_mq, w_mk], axis=1).astype(BF16)
        wr = jnp.concatenate([w_mv, w_aq, w_ak, w_av, w_rq, w_rk, w_rv], axis=1).astype(BF16)
        wgt = w_mg[:, gperm].T.astype(BF16)
        gbt = jnp.broadcast_to(m_gate_bias[l][gperm].reshape(-1, 1), (4 * M_HEADS, LANES))

        (hb, mq, mk, mv, gr, aq, ak, av, rq, rk, rv) = _pre_call(
            *stream, modsel, norm1_w[l].reshape(1, D), wqk, m_conv_w[l], wr, wgt, gbt, tabs)

        hm = _mlstm_call(mq, mk, mv, gr, n_ctx)
        sink_rows = jnp.repeat(a_sink[l].reshape(A_KV_HEADS, A_GROUP), A_BLOCK, axis=1)[:, None, :]
        ya = _attn_call(aq, ak, av, sink_rows.astype(F32) * LOG2E, L, not last)
        lgb = jnp.broadcast_to(ret_logit[l].T[:, :, None, None], (R_HEADS, 2, 8, R_V_DIM)).astype(F32)
        hr = _ret_call(rq, rk, rv, lgb, L // RCHUNK)

        t_off = 1 if last else 0
        wgates = jnp.concatenate([w_mo, w_rg, w_gm, w_ga, w_gr], axis=1).astype(BF16)
        x1 = _post_call(*stream, hb, hm, ya, hr, modsel, wgates, m_norm_w[l].reshape(1, -1),
                        ret_norm_w[l].reshape(1, -1), w_br_m[l].astype(BF16), w_br_a[l].astype(BF16),
                        w_br_r[l].astype(BF16), w_out[l].astype(BF16), t_off)
        xc = _ffn_call(x1, modsel, norm2_w[l].reshape(1, D), ffn_up[l][:, :FFN_DIM].astype(BF16),
                       ffn_up[l][:, FFN_DIM:].astype(BF16), ffn_conv_w[l], ffn_conv_b[l].reshape(1, -1),
                       ffn_down[l].astype(BF16), final_norm_w.reshape(1, D), not last, last)
        stream = (None, xc)
    return xc
```

```python
import functools

import jax
import jax.numpy as jnp
import numpy as np
from jax import lax
from jax.experimental import pallas as pl
from jax.experimental.pallas import tpu as pltpu

F32 = jnp.float32
BF16 = jnp.bfloat16

D_MODEL = 1024
GRID_W = 64
NORM_EPS = 1e-6
ROPE_BASE = 10000.0
NEG_INF = -1e30
LOG2E = 1.4426950408889634

M_HEADS = 4
M_HEAD_DIM = 128
M_WIDTH = M_HEADS * M_HEAD_DIM
A_HEADS = 8
A_KV_HEADS = 2
A_GROUP = A_HEADS // A_KV_HEADS
A_HEAD_DIM = 64
A_WIDTH = A_HEADS * A_HEAD_DIM
A_KV_WIDTH = A_KV_HEADS * A_HEAD_DIM
A_WINDOW = 128
A_BLOCK = 128
R_HEADS = 4
R_QK_DIM = 128
R_V_DIM = 256
R_QK_WIDTH = R_HEADS * R_QK_DIM
R_V_WIDTH = R_HEADS * R_V_DIM
FFN_DIM = 2816

IN_SPLITS = (M_WIDTH, M_WIDTH, M_WIDTH, M_WIDTH, 4 * M_HEADS,
             A_WIDTH, A_KV_WIDTH, A_KV_WIDTH,
             R_QK_WIDTH, R_QK_WIDTH, R_V_WIDTH, R_V_WIDTH,
             D_MODEL, D_MODEL, D_MODEL)

TM = 256
BB = 2
HALO = 8
CHUNK = 128
RCHUNK = 256
FFN_NC = 256
LANES = 128
VMEM_LIMIT = 56 * 1024 * 1024


def _sigmoid(x):
    return 1.0 / (1.0 + jnp.exp(-x))


def _log_sigmoid(x):
    return jnp.minimum(x, 0.0) - jnp.log1p(jnp.exp(-jnp.abs(x)))


def _dot(a, b):
    return jnp.dot(a, b, preferred_element_type=F32)


def _dot_nt(a, b):
    return lax.dot_general(a, b, (((1,), (1,)), ((), ())), preferred_element_type=F32)


def _dot_tn(a, b):
    return lax.dot_general(a, b, (((0,), (0,)), ((), ())), preferred_element_type=F32)


def _const_spec(shape):
    nd = len(shape)
    return pl.BlockSpec(shape, lambda *_: (0,) * nd, pipeline_mode=pl.Buffered(1))


def _params(sem):
    return pltpu.CompilerParams(dimension_semantics=sem, vmem_limit_bytes=VMEM_LIMIT)


def _mod_kernel(c_ref, w_ref, b_ref, o_ref):
    c = c_ref[...]
    s = (c * _sigmoid(c)).astype(BF16)
    o_ref[0] = _dot(s, w_ref[0].astype(BF16)) + b_ref[0]


def _modulation(cpad, mod_w, mod_b):
    depth, d, n = mod_w.shape
    rows = cpad.shape[0]
    bn = 1024
    return pl.pallas_call(
        _mod_kernel,
        grid=(depth, n // bn),
        in_specs=[pl.BlockSpec((rows, d), lambda l, j: (0, 0)),
                  pl.BlockSpec((1, d, bn), lambda l, j: (l, 0, j)),
                  pl.BlockSpec((1, 1, bn), lambda l, j: (l, 0, j))],
        out_specs=pl.BlockSpec((1, rows, bn), lambda l, j: (l, 0, j)),
        out_shape=jax.ShapeDtypeStruct((depth, rows, n), F32),
        compiler_params=_params(("parallel", "parallel")),
        name="modulation",
    )(cpad, mod_w, mod_b.reshape(depth, 1, n))


def _norm_mod(xv, nw, sc, sh):
    ms = jnp.mean(xv * xv, axis=-1, keepdims=True)
    return (xv * lax.rsqrt(ms + NORM_EPS) * nw) * (1.0 + sc) + sh


def _seg_cumsum(x, axis, reverse):
    n = x.shape[axis]
    idx = lax.broadcasted_iota(jnp.int32, x.shape, axis)
    s = 1
    while s < n:
        if reverse:
            shifted = pltpu.roll(x, n - s, axis)
            x = x + jnp.where(idx < n - s, shifted, 0.0)
        else:
            shifted = pltpu.roll(x, s, axis)
            x = x + jnp.where(idx >= s, shifted, 0.0)
        s *= 2
    return x


def _gate_transform(raw, gate_axis):
    tok_axis = 1 - gate_axis
    k = lax.broadcasted_iota(jnp.int32, raw.shape, gate_axis) % 4
    lf = _log_sigmoid(raw)
    cum_f = _seg_cumsum(jnp.where(k == 1, lf, 0.0), tok_axis, False)
    cum_b = _seg_cumsum(jnp.where(k == 3, lf, 0.0), tok_axis, True)
    bsel = jnp.where(k == 1, cum_f, jnp.where(k == 3, cum_b, 0.0))
    n = raw.shape[gate_axis]
    bnext = pltpu.roll(bsel, n - 1, gate_axis)
    nt = raw.shape[tok_axis]
    tot_f = lax.slice_in_dim(cum_f, nt - 1, nt, axis=tok_axis)
    tot_b = lax.slice_in_dim(cum_b, 0, 1, axis=tok_axis)
    total = jnp.broadcast_to(tot_f + tot_b, raw.shape)
    return jnp.where(k % 2 == 1, bsel, raw - bnext), total


def _pre_kernel(c_ref, x_ref, xp_ref, xn_ref, mod_ref, nw_ref, wqk_ref, cw_ref, wr_ref, wgt_ref,
                gbt_ref, ca_ref, sa_ref, cr_ref, sr_ref, *out_and_scratch, split):
    for i in range(x_ref.shape[0]):
        one = pl.ds(i, 1)
        _pre_tile(c_ref.at[one] if split else None,
                  x_ref.at[one], xp_ref.at[one], xn_ref.at[one], mod_ref.at[one], nw_ref, wqk_ref, cw_ref,
                  wr_ref, wgt_ref, gbt_ref, ca_ref, sa_ref, cr_ref, sr_ref,
                  *[r.at[one] for r in out_and_scratch[:-1]], out_and_scratch[-1].at[i])


def _tile_rows(c_ref, x_ref):
    if c_ref is None:
        return x_ref[0]
    return jnp.where(pl.program_id(0) == 0, c_ref[0], x_ref[0])


def _pre_tile(c_ref, x_ref, xp_ref, xn_ref, mod_ref, nw_ref, wqk_ref, cw_ref, wr_ref, wgt_ref,
              gbt_ref, ca_ref, sa_ref, cr_ref, sr_ref,
              hb_ref, mq_ref, mk_ref, mv_ref, gr_ref, aq_ref, ak_ref, av_ref,
              rq_ref, rk_ref, rv_ref, hext_ref):
    t = pl.program_id(0)
    nt = pl.num_programs(0)
    sh = mod_ref[0, 0, 0:1, :]
    sc = mod_ref[0, 0, 1:2, :]
    nw = nw_ref[...]
    hb = _norm_mod(_tile_rows(c_ref, x_ref), nw, sc, sh).astype(BF16)
    hb_ref[0] = hb

    grow = _dot_nt(wgt_ref[...], hb) + gbt_ref[:, 0:1]
    for c in range(TM // CHUNK):
        gr, tot = _gate_transform(grow[:, c * CHUNK:(c + 1) * CHUNK], 0)
        aend = gr + pltpu.roll(tot, 4 * M_HEADS - 1, 0)
        amax = jnp.broadcast_to(jnp.max(aend, axis=1, keepdims=True), aend.shape)
        for h in range(M_HEADS):
            hp_, o0 = h // 2, 8 * (h % 2)
            gr_ref[0, hp_, c, o0:o0 + 4, :] = gr[4 * h:4 * h + 4]
            for d in range(2):
                gr_ref[0, hp_, c, o0 + 4 + d:o0 + 5 + d, :] = tot[4 * h + 2 * d + 1:4 * h + 2 * d + 2]
                gr_ref[0, hp_, c, o0 + 6 + d:o0 + 7 + d, :] = amax[4 * h + 2 * d:4 * h + 2 * d + 1]

    hp = _norm_mod(xp_ref[0], nw, sc, sh)
    hn = _norm_mod(xn_ref[0], nw, sc, sh)
    hp = jnp.where(t <= 1, 0.0, hp)
    hn = jnp.where((t == 0) | (t == nt - 1), 0.0, hn)
    hext_ref[0:TM, :] = hb
    hext_ref[TM:, :] = jnp.concatenate([hn, hp], axis=0).astype(BF16)
    hext = hext_ref[...]
    ext = TM + 2 * HALO

    for j, (dst, scale) in enumerate(((mq_ref, 1.0), (mk_ref, M_HEAD_DIM ** -0.5))):
        p = _dot(hext, wqk_ref[:, j * M_WIDTH:(j + 1) * M_WIDTH])
        cw = cw_ref[:, j * M_WIDTH:(j + 1) * M_WIDTH]
        prev = pltpu.roll(p, 1, 0)[0:TM]
        nxt = pltpu.roll(p, ext - 1, 0)[0:TM]
        conv = prev * cw[0:1] + p[0:TM] * cw[1:2] + nxt * cw[2:3]
        act = conv * _sigmoid(conv) * scale
        for h in range(M_HEADS):
            dst[0, h] = act[:, h * M_HEAD_DIM:(h + 1) * M_HEAD_DIM].astype(BF16)

    off = 0
    p = _dot(hb, wr_ref[:, off:off + M_WIDTH])
    off += M_WIDTH
    for h in range(M_HEADS):
        for c in range(TM // CHUNK):
            blk = p[c * CHUNK:(c + 1) * CHUNK, h * M_HEAD_DIM:(h + 1) * M_HEAD_DIM]
            mv_ref[0, h, c] = blk.T.astype(BF16)

    lane = lax.broadcasted_iota(jnp.int32, (TM, LANES), 1)
    first16 = (lane % 32) < 16
    ca = ca_ref[...]
    sa = sa_ref[...]

    def rope_a(xs):
        partner = jnp.where(first16, pltpu.roll(xs, LANES - 16, 1), pltpu.roll(xs, 16, 1))
        return xs * ca + partner * sa

    p = _dot(hb, wr_ref[:, off:off + A_WIDTH])
    off += A_WIDTH
    for s in range(A_WIDTH // LANES):
        r = (rope_a(p[:, s * LANES:(s + 1) * LANES]) * (A_HEAD_DIM ** -0.5 * LOG2E)).astype(BF16)
        aq_ref[0, 2 * s] = r[:, 0:A_HEAD_DIM]
        aq_ref[0, 2 * s + 1] = r[:, A_HEAD_DIM:]
    p = _dot(hb, wr_ref[:, off:off + 2 * A_KV_WIDTH])
    off += 2 * A_KV_WIDTH
    r = rope_a(p[:, 0:LANES]).astype(BF16)
    for h in range(A_KV_HEADS):
        ak_ref[0, h] = r[:, h * A_HEAD_DIM:(h + 1) * A_HEAD_DIM]
    pv = p[:, LANES:]
    for h in range(A_KV_HEADS):
        vh = pv if h == 0 else pltpu.roll(pv, A_HEAD_DIM, 1)
        ve = jnp.where(lane < A_HEAD_DIM, vh, 1.0)
        for c in range(TM // A_BLOCK):
            av_ref[0, h, c] = ve[c * A_BLOCK:(c + 1) * A_BLOCK].T.astype(BF16)

    cr = cr_ref[...]
    sr = sr_ref[...]
    for is_k in (False, True):
        p = _dot(hb, wr_ref[:, off:off + R_QK_WIDTH])
        off += R_QK_WIDTH
        for h in range(R_HEADS):
            xs = p[:, h * R_QK_DIM:(h + 1) * R_QK_DIM]
            rot = xs * cr + pltpu.roll(xs, R_QK_DIM // 2, 1) * sr
            if is_k:
                rot = rot * R_QK_DIM ** -0.5
                for c in range(TM // CHUNK):
                    rk_ref[0, h, c] = rot[c * CHUNK:(c + 1) * CHUNK].T.astype(BF16)
            else:
                rq_ref[0, h] = rot.astype(BF16)
    p = _dot(hb, wr_ref[:, off:off + R_V_WIDTH])
    for h in range(R_HEADS):
        rv_ref[0, h] = p[:, h * R_V_DIM:(h + 1) * R_V_DIM].astype(BF16)


def _stream_specs(ctx, xs):
    hb = TM // HALO
    D = xs.shape[-1]
    off = 0 if ctx is None else 1
    last = xs.shape[1] // HALO - 1
    return [
        pl.BlockSpec((BB, TM if off else HALO, D), lambda t, b: (b, 0, 0)),
        pl.BlockSpec((BB, TM, D), lambda t, b: (b, jnp.maximum(t - off, 0), 0)),
        pl.BlockSpec((BB, HALO, D), lambda t, b: (b, jnp.maximum((t - off) * hb - 1, 0), 0)),
        pl.BlockSpec((BB, HALO, D), lambda t, b: (b, jnp.clip((t - off + 1) * hb, 0, last), 0)),
    ]


def _pre_call(ctx, xs, modsel, nw, wqk, cw, wr, wgt, gbt, tabs):
    B, _, D = xs.shape
    T = xs.shape[1] + (0 if ctx is None else ctx.shape[1])
    nt = T // TM
    nch = T // CHUNK

    def tile(t, b):
        return (b, t, 0)

    def head_tile(t, b):
        return (b, 0, t, 0)

    in_specs = _stream_specs(ctx, xs) + [
        pl.BlockSpec((BB, 1, 6, D), lambda t, b: (b, jnp.minimum(t, 1), 0, 0)),
        _const_spec(nw.shape), _const_spec(wqk.shape), _const_spec(cw.shape), _const_spec(wr.shape),
        _const_spec(wgt.shape), _const_spec(gbt.shape),
    ] + [pl.BlockSpec((TM, LANES), lambda t, b: (t, 0)) for _ in range(4)]
    out_shape = [
        jax.ShapeDtypeStruct((B, T, D), BF16),
        jax.ShapeDtypeStruct((B, M_HEADS, T, M_HEAD_DIM), BF16),
        jax.ShapeDtypeStruct((B, M_HEADS, T, M_HEAD_DIM), BF16),
        jax.ShapeDtypeStruct((B, M_HEADS, nch, M_HEAD_DIM, CHUNK), BF16),
        jax.ShapeDtypeStruct((B, 2, nch, 16, CHUNK), F32),
        jax.ShapeDtypeStruct((B, A_HEADS, T, A_HEAD_DIM), BF16),
        jax.ShapeDtypeStruct((B, A_KV_HEADS, T, A_HEAD_DIM), BF16),
        jax.ShapeDtypeStruct((B, A_KV_HEADS, T // A_BLOCK, LANES, A_BLOCK), BF16),
        jax.ShapeDtypeStruct((B, R_HEADS, T, R_QK_DIM), BF16),
        jax.ShapeDtypeStruct((B, R_HEADS, nch, R_QK_DIM, CHUNK), BF16),
        jax.ShapeDtypeStruct((B, R_HEADS, T, R_V_DIM), BF16),
    ]
    out_specs = [
        pl.BlockSpec((BB, TM, D), tile),
        pl.BlockSpec((BB, M_HEADS, TM, M_HEAD_DIM), head_tile),
        pl.BlockSpec((BB, M_HEADS, TM, M_HEAD_DIM), head_tile),
        pl.BlockSpec((BB, M_HEADS, TM // CHUNK, M_HEAD_DIM, CHUNK), lambda t, b: (b, 0, t, 0, 0)),
        pl.BlockSpec((BB, 2, TM // CHUNK, 16, CHUNK), lambda t, b: (b, 0, t, 0, 0)),
        pl.BlockSpec((BB, A_HEADS, TM, A_HEAD_DIM), head_tile),
        pl.BlockSpec((BB, A_KV_HEADS, TM, A_HEAD_DIM), head_tile),
        pl.BlockSpec((BB, A_KV_HEADS, TM // A_BLOCK, LANES, A_BLOCK), lambda t, b: (b, 0, t, 0, 0)),
        pl.BlockSpec((BB, R_HEADS, TM, R_QK_DIM), head_tile),
        pl.BlockSpec((BB, R_HEADS, TM // CHUNK, R_QK_DIM, CHUNK), lambda t, b: (b, 0, t, 0, 0)),
        pl.BlockSpec((BB, R_HEADS, TM, R_V_DIM), head_tile),
    ]
    return pl.pallas_call(
        functools.partial(_pre_kernel, split=ctx is not None),
        grid=(nt, B // BB),
        in_specs=in_specs,
        out_specs=out_specs,
        out_shape=out_shape,
        scratch_shapes=[pltpu.VMEM((BB, TM + 2 * HALO, D), BF16)],
        compiler_params=_params(("parallel", "parallel")),
        name="pre_proj",
    )(xs if ctx is None else ctx, xs, xs, xs, modsel, nw, wqk, cw, wr, wgt, gbt, *tabs)


def _bwd_chunk(i, n_ctx, n_all):
    if isinstance(i, int):
        return n_ctx - 1 - i if i < n_ctx else n_all - 1 + n_ctx - i
    return n_all - 1 + n_ctx - i


def _chunk_start(c):
    return c * CHUNK if isinstance(c, int) else pl.multiple_of(c * CHUNK, CHUNK)


def _chunk_loop(body, n_ctx, n_all, unroll):
    for i in range(n_ctx):
        body(i, 0)
    n_lat = n_all - n_ctx
    while n_lat % unroll:
        unroll //= 2
    lax.fori_loop(n_ctx, n_all, body, 0, unroll=unroll)


def _mlstm_kernel(q_ref, k_ref, vt_ref, gr_ref, o_ref, c_ref, n_ref, m_ref,
                  cs_ref, ns_ref, ms_ref, qp_ref, nq_ref, *, n_ctx):
    n_all = gr_ref.shape[2]
    dh = q_ref.shape[3]
    c_ref[...] = jnp.zeros(c_ref.shape, F32)
    n_ref[...] = jnp.zeros(n_ref.shape, F32)
    m_ref[...] = jnp.zeros(m_ref.shape, F32)

    def scan_unit(c, hh, d):
        u = 2 * hh + d
        c0 = _chunk_start(c)
        k = k_ref[0, hh, pl.ds(c0, CHUNK), :]
        vt = vt_ref[0, hh, c]
        rows = gr_ref[0, 0, c, 8 * hh:8 * hh + 8, :]
        tot = rows[4 + d:5 + d]
        aend = rows[2 * d:2 * d + 1] + tot
        amax = rows[6 + d:7 + d]
        cst = c_ref[u]
        nv = n_ref[u]
        m = m_ref[u]
        cs_ref[u, c] = cst.astype(BF16)
        ns_ref[u, c] = jnp.broadcast_to(nv, (16, dh)).astype(BF16)
        ms_ref[u, c] = jnp.broadcast_to(m, (8, CHUNK))
        m_new = jnp.maximum(tot + m, amax)
        w = jnp.exp(aend - m_new)
        dec = jnp.exp(tot + m - m_new)
        vw = (vt.astype(F32) * w).astype(BF16)
        c_ref[u] = dec * cst + _dot(vw, k)
        wk = _dot(jnp.broadcast_to(w, (16, CHUNK)).astype(BF16), k)
        n_ref[u] = dec * nv + wk[0:1]
        m_ref[u] = m_new

    def scan_step(i, carry):
        cb = _bwd_chunk(i, n_ctx, n_all)
        for hh in range(2):
            scan_unit(i, hh, 0)
            scan_unit(cb, hh, 1)
        return carry

    _chunk_loop(scan_step, n_ctx, n_all, 16)

    row = lax.broadcasted_iota(jnp.int32, (CHUNK, CHUNK), 0)
    col = lax.broadcasted_iota(jnp.int32, (CHUNK, CHUNK), 1)
    tri = (row <= col, row >= col)

    def query_products(c, slot):
        c0 = _chunk_start(c)
        for hh in range(2):
            q = q_ref[0, hh, pl.ds(c0, CHUNK), :]
            k = k_ref[0, hh, pl.ds(c0, CHUNK), :]
            qp_ref[slot, hh, 0] = _dot_nt(k, q)
            for d in range(2):
                u = 2 * hh + d
                qp_ref[slot, hh, 1 + d] = _dot_nt(cs_ref[u, c], q)
                nq_ref[slot, u] = _dot_nt(ns_ref[u, c], q)[0:8]

    def outputs(c, slot):
        c0 = _chunk_start(c)
        gcv = jnp.concatenate([gr_ref[0, 0, c], jnp.zeros((CHUNK - 16, CHUNK), F32)], axis=0).T
        for hh in range(2):
            vt = vt_ref[0, hh, c]
            rows = gr_ref[0, 0, c, 8 * hh:8 * hh + 8, :]
            st = qp_ref[slot, hh, 0]
            sds, iws, invs = [], [], []
            for d in range(2):
                u = 2 * hh + d
                a0c = gcv[:, 8 * hh + 2 * d:8 * hh + 2 * d + 1]
                b = rows[2 * d + 1:2 * d + 2]
                m = ms_ref[u, c, 0:1, :]
                dlog = jnp.where(tri[d], a0c + b, NEG_INF)
                mj = jnp.maximum(b + m, jnp.max(dlog, axis=0, keepdims=True))
                sd = st * jnp.exp(dlog - mj)
                iw = jnp.exp(b + m - mj)
                den = iw * nq_ref[slot, u, 0:1, :] + jnp.sum(sd, axis=0, keepdims=True)
                sds.append(sd.astype(BF16))
                iws.append(iw)
                invs.append(1.0 / jnp.maximum(jnp.abs(den), jnp.exp(-mj)))
            num = _dot(vt, jnp.concatenate(sds, axis=1))
            acc = None
            for d in range(2):
                ht = (num[:, d * CHUNK:(d + 1) * CHUNK] + iws[d] * qp_ref[slot, hh, 1 + d]) * invs[d]
                acc = ht if acc is None else acc + ht
            o_ref[0, hh, pl.ds(c0, CHUNK), :] = acc.T

    query_products(0, 0)

    def out_pair(i, carry):
        c = 2 * i
        query_products(c + 1, 1)
        outputs(c, 0)
        query_products(jnp.minimum(c + 2, n_all - 1), 0)
        outputs(c + 1, 1)
        return carry

    n_pairs = n_all // 2
    unroll = 4
    first = n_pairs % unroll
    for i in range(first):
        out_pair(i, 0)
    lax.fori_loop(first, n_pairs, out_pair, 0, unroll=unroll)


def _mlstm_call(mq, mk, mvt, gr, n_ctx):
    B, H, T, dh = mq.shape
    nch = T // CHUNK
    qspec = pl.BlockSpec((1, 2, T, dh), lambda b, p: (b, p, 0, 0))
    return pl.pallas_call(
        functools.partial(_mlstm_kernel, n_ctx=n_ctx),
        grid=(B, H // 2),
        in_specs=[qspec, qspec,
                  pl.BlockSpec((1, 2, nch, dh, CHUNK), lambda b, p: (b, p, 0, 0, 0)),
                  pl.BlockSpec((1, 1, nch, 16, CHUNK), lambda b, p: (b, p, 0, 0, 0))],
        out_specs=pl.BlockSpec((1, 2, T, dh), lambda b, p: (b, p, 0, 0)),
        out_shape=jax.ShapeDtypeStruct((B, H, T, dh), F32),
        scratch_shapes=[pltpu.VMEM((4, dh, dh), F32), pltpu.VMEM((4, 1, dh), F32),
                        pltpu.VMEM((4, 1, CHUNK), F32),
                        pltpu.VMEM((4, nch, dh, dh), BF16), pltpu.VMEM((4, nch, 16, dh), BF16),
                        pltpu.VMEM((4, nch, 8, CHUNK), F32),
                        pltpu.VMEM((2, 2, 3, dh, CHUNK), F32), pltpu.VMEM((2, 4, 8, CHUNK), F32)],
        compiler_params=_params(("parallel", "parallel")),
        name="mlstm",
    )(mq, mk, mvt, gr)


def _ret_kernel(q_ref, kt_ref, v_ref, lg_ref, o_ref, st_ref, sts_ref, dec_ref, ib_ref, wb_ref, cd_ref,
                *, n_ctx):
    sub = RCHUNK // CHUNK
    n_all = kt_ref.shape[2] // sub
    dk = q_ref.shape[3]
    dv = v_ref.shape[3]
    st_ref[...] = jnp.zeros(st_ref.shape, F32)

    @pl.when(pl.program_id(1) == 0)
    def _():
        row = lax.broadcasted_iota(jnp.int32, (RCHUNK, RCHUNK), 0).astype(F32)
        col = lax.broadcasted_iota(jnp.int32, (RCHUNK, RCHUNK), 1).astype(F32)
        kcol = lax.broadcasted_iota(jnp.int32, (dk, RCHUNK), 1).astype(F32)
        dec = None
        for d in range(2):
            lg = _log_sigmoid(lg_ref[0, d])[0:1]
            diff = row - col if d == 0 else col - row
            dd = jnp.where(diff >= 0, jnp.exp(lg * jnp.maximum(diff, 0.0)), 0.0)
            dec = dd if dec is None else dec + dd
            pin = row if d == 0 else RCHUNK - 1.0 - row
            ib_ref[d] = jnp.exp(lg * (pin + 1.0))
            pkey = kcol if d == 0 else RCHUNK - 1.0 - kcol
            wb_ref[d] = jnp.exp(lg * (RCHUNK - 1.0 - pkey))
            cd_ref[d] = jnp.broadcast_to(jnp.exp(lg * RCHUNK), (8, dv))
        dec_ref[...] = dec

    cds = [cd_ref[d, 0:1, :] for d in range(2)]

    def rows(c):
        r0 = c * RCHUNK
        return pl.ds(r0 if isinstance(c, int) else pl.multiple_of(r0, RCHUNK), RCHUNK)

    def k_t(c):
        return jnp.concatenate([kt_ref[0, 0, sub * c + j] for j in range(sub)], axis=1)

    def scan_step(i, carry):
        cb = _bwd_chunk(i, n_ctx, n_all)
        for d, c in ((0, i), (1, cb)):
            st = st_ref[d]
            sts_ref[c, :, d * dv:(d + 1) * dv] = st.astype(BF16)
            kw = (k_t(c).astype(F32) * wb_ref[d]).astype(BF16)
            st_ref[d] = cds[d] * st + _dot(kw, v_ref[0, 0, rows(c), :])
        return carry

    _chunk_loop(scan_step, n_ctx, n_all, 16)

    def out_step(c, carry):
        q = q_ref[0, 0, rows(c), :]
        s = (_dot(q, k_t(c)) * dec_ref[...]).astype(BF16)
        qst = _dot(q, sts_ref[c])
        o_ref[0, 0, rows(c), :] = (_dot(s, v_ref[0, 0, rows(c), :])
                                   + ib_ref[0] * qst[:, 0:dv] + ib_ref[1] * qst[:, dv:])
        return carry

    _chunk_loop(out_step, n_ctx, n_all, 16)


def _ret_call(rq, rkt, rv, lgb, n_ctx):
    B, H, T, dk = rq.shape
    dv = rv.shape[-1]
    nch = T // CHUNK
    assert dv == RCHUNK and dk <= RCHUNK and T % RCHUNK == 0
    return pl.pallas_call(
        functools.partial(_ret_kernel, n_ctx=n_ctx),
        grid=(H, B),
        in_specs=[pl.BlockSpec((1, 1, T, dk), lambda h, b: (b, h, 0, 0)),
                  pl.BlockSpec((1, 1, nch, dk, CHUNK), lambda h, b: (b, h, 0, 0, 0)),
                  pl.BlockSpec((1, 1, T, dv), lambda h, b: (b, h, 0, 0)),
                  pl.BlockSpec((1, 2, 8, dv), lambda h, b: (h, 0, 0, 0))],
        out_specs=pl.BlockSpec((1, 1, T, dv), lambda h, b: (b, h, 0, 0)),
        out_shape=jax.ShapeDtypeStruct((B, H, T, dv), F32),
        scratch_shapes=[pltpu.VMEM((2, dk, dv), F32), pltpu.VMEM((T // RCHUNK, dk, 2 * dv), BF16),
                        pltpu.VMEM((RCHUNK, RCHUNK), F32), pltpu.VMEM((2, RCHUNK, dv), F32),
                        pltpu.VMEM((2, dk, RCHUNK), F32), pltpu.VMEM((2, 8, dv), F32)],
        compiler_params=_params(("parallel", "arbitrary")),
        name="retention",
    )(rq, rkt, rv, lgb)


def _attn_kernel(q_ref, k_ref, vt_ref, sink_ref, o_ref, bias_ref, s_ref, *, n_ctx_tok, need_ctx):
    T = q_ref.shape[2]
    L = n_ctx_tok
    S = T - L
    nb = S // A_BLOCK
    nspan = 3
    span = nspan * A_BLOCK
    ncb = L // A_BLOCK
    cols = A_GROUP * A_BLOCK
    kc = k_ref[0, 0, 0:L, :]
    vtc = jnp.concatenate([vt_ref[0, 0, j] for j in range(ncb)], axis=1)
    sink = sink_ref[0]

    @pl.when((pl.program_id(0) == 0) & (pl.program_id(1) == 0))
    def _():
        kpos = lax.broadcasted_iota(jnp.int32, (span, A_BLOCK), 0)
        qpos = lax.broadcasted_iota(jnp.int32, (span, A_BLOCK), 1)
        for delta in range(nspan):
            bias_ref[delta] = jnp.where(jnp.abs(qpos + delta * A_BLOCK - kpos) <= A_WINDOW, 0.0, NEG_INF)

    def finish(blk, acc, m):
        den = acc[A_HEAD_DIM:A_HEAD_DIM + 1] + jnp.exp2(sink - m)
        ot = (acc[0:A_HEAD_DIM] * (1.0 / den)).astype(BF16)
        for g in range(A_GROUP):
            o_ref[0, blk, g * A_HEAD_DIM:(g + 1) * A_HEAD_DIM, :] = ot[:, g * A_BLOCK:(g + 1) * A_BLOCK]

    for cbk in range(ncb):
        if need_ctx:
            qs = q_ref[0, :, cbk * A_BLOCK:(cbk + 1) * A_BLOCK, :].reshape(cols, A_HEAD_DIM)
            s = _dot_nt(kc, qs)
            m = jnp.maximum(jnp.max(s, axis=0, keepdims=True), sink)
            finish(cbk, _dot(vtc, jnp.exp2(s - m).astype(BF16)), m)
        else:
            o_ref[0, cbk] = jnp.zeros(o_ref.shape[2:], BF16)

    def first_key_block(n):
        return jnp.clip(n - 1, 0, nb - nspan)

    def scores(n, slot):
        q0 = pl.multiple_of(L + n * A_BLOCK, A_BLOCK)
        kb = first_key_block(n)
        k0 = pl.multiple_of(L + kb * A_BLOCK, A_BLOCK)
        qs = q_ref[0, :, pl.ds(q0, A_BLOCK), :].reshape(cols, A_HEAD_DIM)
        bias = bias_ref[n - kb]
        s_ref[slot, 0:span, :] = (_dot_nt(k_ref[0, 0, pl.ds(k0, span), :], qs)
                                  + jnp.concatenate([bias] * A_GROUP, axis=1))
        s_ref[slot, span:, :] = _dot_nt(kc, qs)

    def softmax_pv(n, slot):
        kb = first_key_block(n)
        s = s_ref[slot]
        m = jnp.maximum(jnp.max(s, axis=0, keepdims=True), sink)
        p = jnp.exp2(s - m).astype(BF16)
        vt = jnp.concatenate([vt_ref[0, 0, ncb + kb + j] for j in range(nspan)] + [vtc], axis=1)
        finish(ncb + n, _dot(vt, p), m)

    scores(0, 0)

    def block_pair(i, carry):
        n = 2 * i
        scores(n + 1, 1)
        softmax_pv(n, 0)
        scores(jnp.minimum(n + 2, nb - 1), 0)
        softmax_pv(n + 1, 1)
        return carry

    n_pairs = nb // 2
    unroll = 8
    first = n_pairs % unroll
    for i in range(first):
        block_pair(i, 0)
    lax.fori_loop(first, n_pairs, block_pair, 0, unroll=unroll)


def _attn_call(aq, ak, avt, sink_rows, n_ctx_tok, need_ctx):
    B, _, T, hd = aq.shape
    nblk = T // A_BLOCK
    cols = A_GROUP * A_BLOCK
    return pl.pallas_call(
        functools.partial(_attn_kernel, n_ctx_tok=n_ctx_tok, need_ctx=need_ctx),
        grid=(B, A_KV_HEADS),
        in_specs=[pl.BlockSpec((1, A_GROUP, T, hd), lambda b, h: (b, h, 0, 0)),
                  pl.BlockSpec((1, 1, T, hd), lambda b, h: (b, h, 0, 0)),
                  pl.BlockSpec((1, 1, nblk, LANES, A_BLOCK), lambda b, h: (b, h, 0, 0, 0)),
                  pl.BlockSpec((1, 1, cols), lambda b, h: (h, 0, 0))],
        out_specs=pl.BlockSpec((1, nblk, A_GROUP * hd, A_BLOCK), lambda b, h: (b, 0, h, 0)),
        out_shape=jax.ShapeDtypeStruct((B, nblk, A_WIDTH, A_BLOCK), BF16),
        scratch_shapes=[pltpu.VMEM((3, 3 * A_BLOCK, A_BLOCK), F32),
                        pltpu.VMEM((2, 3 * A_BLOCK + n_ctx_tok, cols), F32)],
        compiler_params=_params(("arbitrary", "arbitrary")),
        name="window_attn",
    )(aq, ak, avt, sink_rows)


def _head_norm(y):
    mu = jnp.mean(y, axis=-1, keepdims=True)
    yc = y - mu
    var = jnp.mean(yc * yc, axis=-1, keepdims=True)
    return yc * lax.rsqrt(var + NORM_EPS)


def _post_kernel(c_ref, x_ref, hb_ref, hm_ref, ya_ref, hr_ref, mod_ref, wg_ref, mnw_ref, rnw_ref,
                 wbm_ref, wba_ref, wbr_ref, wo_ref, o_ref, *, split):
    for i in range(x_ref.shape[0]):
        hb = hb_ref[i]
        off = 0
        hm = jnp.concatenate([_head_norm(hm_ref[i, h]) for h in range(M_HEADS)], axis=1) * mnw_ref[...]
        ym = (_sigmoid(_dot(hb, wg_ref[:, off:off + M_WIDTH])) * hm).astype(BF16)
        off += M_WIDTH
        hr = jnp.concatenate([_head_norm(hr_ref[i, h]) for h in range(R_HEADS)], axis=1) * rnw_ref[...]
        rg = _dot(hb, wg_ref[:, off:off + R_V_WIDTH])
        yr = (rg * _sigmoid(rg) * hr).astype(BF16)
        off += R_V_WIDTH
        z = _sigmoid(_dot(hb, wg_ref[:, off:off + D_MODEL])) * _dot(ym, wbm_ref[...])
        off += D_MODEL
        pa = jnp.concatenate([_dot_tn(ya_ref[i, c], wba_ref[...]) for c in range(TM // A_BLOCK)], axis=0)
        z = z + _sigmoid(_dot(hb, wg_ref[:, off:off + D_MODEL])) * pa
        off += D_MODEL
        z = z + _sigmoid(_dot(hb, wg_ref[:, off:off + D_MODEL])) * _dot(yr, wbr_ref[...])
        y = _dot(z.astype(BF16), wo_ref[...])
        one = pl.ds(i, 1)
        xres = _tile_rows(c_ref.at[one] if split else None, x_ref.at[one])
        o_ref[i] = xres + mod_ref[i, 0, 2:3, :] * y


def _post_call(ctx, xs, hb, hm, ya, hr, modsel, wg, mnw, rnw, wbm, wba, wbr, wo, t_off):
    B, T, D = hb.shape
    nt = T // TM - t_off
    split = ctx is not None
    assert not (split and t_off)

    def tile(t, b):
        return (b, t + t_off, 0)

    def head_tile(t, b):
        return (b, 0, t + t_off, 0)

    return pl.pallas_call(
        functools.partial(_post_kernel, split=split),
        grid=(nt, B // BB),
        in_specs=[pl.BlockSpec((BB, TM if split else HALO, D), lambda t, b: (b, 0, 0)),
                  pl.BlockSpec((BB, TM, D), (lambda t, b: (b, jnp.maximum(t - 1, 0), 0)) if split else tile),
                  pl.BlockSpec((BB, TM, D), tile),
                  pl.BlockSpec((BB, M_HEADS, TM, M_HEAD_DIM), head_tile),
                  pl.BlockSpec((BB, TM // A_BLOCK, A_WIDTH, A_BLOCK), lambda t, b: (b, t + t_off, 0, 0)),
                  pl.BlockSpec((BB, R_HEADS, TM, R_V_DIM), head_tile),
                  pl.BlockSpec((BB, 1, 6, D), lambda t, b: (b, jnp.minimum(t + t_off, 1), 0, 0)),
                  _const_spec(wg.shape), _const_spec(mnw.shape), _const_spec(rnw.shape),
                  _const_spec(wbm.shape), _const_spec(wba.shape), _const_spec(wbr.shape),
                  _const_spec(wo.shape)],
        out_specs=pl.BlockSpec((BB, TM, D), lambda t, b: (b, t, 0)),
        out_shape=jax.ShapeDtypeStruct((B, nt * TM, D), F32),
        compiler_params=_params(("parallel", "parallel")),
        name="post_merge",
    )(ctx if split else xs, xs, hb, hm, ya, hr, modsel, wg, mnw, rnw, wbm, wba, wbr, wo)


def _ffn_kernel(x_ref, xp_ref, xn_ref, mod_ref, nw_ref, wa_ref, wb_ref, cw_ref, cb_ref, wd_ref, fw_ref,
                o_ref, hext_ref, u_ref, *, has_ctx, final):
    t = pl.program_id(0)
    nt = pl.num_programs(0)
    first_lat = 1 if has_ctx else 0
    nw = nw_ref[...]
    ext = TM + 2 * HALO
    for i in range(x_ref.shape[0]):
        sh = mod_ref[i, 0, 3:4, :]
        sc = mod_ref[i, 0, 4:5, :]
        g2 = mod_ref[i, 0, 5:6, :]
        x = x_ref[i]
        hp = _norm_mod(xp_ref[i], nw, sc, sh)
        hn = _norm_mod(xn_ref[i], nw, sc, sh)
        hp = jnp.where(t <= first_lat, 0.0, hp)
        hn = jnp.where((t == first_lat - 1) | (t == nt - 1), 0.0, hn)
        hext_ref[i, 0:TM, :] = _norm_mod(x, nw, sc, sh).astype(BF16)
        hext_ref[i, TM:, :] = jnp.concatenate([hn, hp], axis=0).astype(BF16)
        for j in range(FFN_DIM // FFN_NC):
            cs = slice(j * FFN_NC, (j + 1) * FFN_NC)
            a = _dot(hext_ref[i], wa_ref[:, cs])
            b = _dot(hext_ref[i, 0:TM, :], wb_ref[:, cs])
            cw = cw_ref[:, cs]
            prev = pltpu.roll(a, 1, 0)[0:TM]
            nxt = pltpu.roll(a, ext - 1, 0)[0:TM]
            conv = prev * cw[0:1] + a[0:TM] * cw[1:2] + nxt * cw[2:3] + cb_ref[:, cs]
            u_ref[i, :, cs] = (conv * _sigmoid(conv) * b).astype(BF16)
        y = x + g2 * _dot(u_ref[i], wd_ref[...])
        if final:
            ms = jnp.mean(y * y, axis=-1, keepdims=True)
            y = y * lax.rsqrt(ms + NORM_EPS) * fw_ref[...]
        o_ref[i] = y


def _ffn_call(xs, modsel, nw, wa, wb, cw, cb, wd, fw, has_ctx, final):
    B, T, D = xs.shape
    nt = T // TM
    hb16 = TM // HALO
    first_lat = 1 if has_ctx else 0
    tile = pl.BlockSpec((BB, TM, D), lambda t, b: (b, t, 0))
    return pl.pallas_call(
        functools.partial(_ffn_kernel, has_ctx=has_ctx, final=final),
        grid=(nt, B // BB),
        in_specs=[tile,
                  pl.BlockSpec((BB, HALO, D), lambda t, b: (b, jnp.maximum(t * hb16 - 1, 0), 0)),
                  pl.BlockSpec((BB, HALO, D), lambda t, b: (b, jnp.minimum((t + 1) * hb16, T // HALO - 1), 0)),
                  pl.BlockSpec((BB, 1, 6, D), lambda t, b: (b, jnp.minimum(t + 1 - first_lat, 1), 0, 0)),
                  _const_spec(nw.shape), _const_spec(wa.shape), _const_spec(wb.shape),
                  _const_spec(cw.shape), _const_spec(cb.shape), _const_spec(wd.shape),
                  _const_spec(fw.shape)],
        out_specs=tile,
        out_shape=jax.ShapeDtypeStruct((B, T, D), F32),
        scratch_shapes=[pltpu.VMEM((BB, TM + 2 * HALO, D), BF16), pltpu.VMEM((BB, TM, FFN_DIM), BF16)],
        compiler_params=_params(("parallel", "parallel")),
        name="conv_ffn",
    )(xs, xs, xs, modsel, nw, wa, wb, cw, cb, wd, fw)


def _split_cols(w):
    out = []
    acc = 0
    for s in IN_SPLITS:
        out.append(w[:, acc:acc + s])
        acc += s
    return out


def _rope_tables(L, S):
    T = L + S
    f32 = np.float32
    j = np.arange(LANES)
    jj = j % 32
    inv_a = np.power(f32(ROPE_BASE), -(jj % 16).astype(f32) / f32(16.0)).astype(f32)
    t = np.arange(S)
    pos = np.where(((j % A_HEAD_DIM) // 32 == 0)[None, :], (t // GRID_W)[:, None], (t % GRID_W)[:, None])
    ang = pos.astype(f32) * inv_a[None, :]
    sign = np.where(jj < 16, f32(-1.0), f32(1.0))[None, :]
    ca = np.concatenate([np.ones((L, LANES), f32), np.cos(ang)], axis=0)
    sa = np.concatenate([np.zeros((L, LANES), f32), np.sin(ang) * sign], axis=0)
    half = R_QK_DIM // 2
    inv_r = np.power(f32(ROPE_BASE), -(j % half).astype(f32) / f32(half)).astype(f32)
    angr = np.arange(T).astype(f32)[:, None] * inv_r[None, :]
    signr = np.where(j < half, f32(-1.0), f32(1.0))[None, :]
    return tuple(jnp.asarray(a, F32) for a in (ca, sa, np.cos(angr), np.sin(angr) * signr))


def kernel(x, c, ctx, c_ctx, mod_w, mod_b, norm1_w, norm2_w, w_in, m_gate_bias, m_conv_w, m_norm_w,
           a_sink, ret_logit, ret_norm_w, w_br_m, w_br_a, w_br_r, w_out, ffn_up, ffn_conv_w, ffn_conv_b,
           ffn_down, final_norm_w):
    B, S, D = x.shape
    L = ctx.shape[1]
    depth = mod_w.shape[0]
    assert D == D_MODEL and L == TM and S % TM == 0 and S >= 3 * A_BLOCK and B % BB == 0
    T = L + S
    n_ctx = L // CHUNK

    rows = -(-(B + 1) // 8) * 8
    cpad = jnp.zeros((rows, D), F32).at[:B].set(c).at[B].set(c_ctx)
    mods = _modulation(cpad, mod_w, mod_b)
    tabs = _rope_tables(L, S)
    stream = (ctx, x) if depth > 1 else (None, jnp.concatenate([ctx, x], axis=1))

    gperm = jnp.arange(4 * M_HEADS).reshape(4, M_HEADS).T.reshape(-1)

    for l in range(depth):
        last = l == depth - 1
        lat = mods[l, :B].reshape(B, 6, D)
        cm = jnp.broadcast_to(mods[l, B].reshape(1, 6, D), (B, 6, D))
        modsel = jnp.stack([cm, lat], axis=1)

        (w_mq, w_mk, w_mv, w_mo, w_mg, w_aq, w_ak, w_av,
         w_rq, w_rk, w_rv, w_rg, w_gm, w_ga, w_gr) = _split_cols(w_in[l])
        wqk = jnp.concatenate([w_mq, w_mk], axis=1).astype(BF16)
        wr = jnp.concatenate([w_mv, w_aq, w_ak, w_av, w_rq, w_rk, w_rv], axis=1).astype(BF16)
        wgt = w_mg[:, gperm].T.astype(BF16)
        gbt = jnp.broadcast_to(m_gate_bias[l][gperm].reshape(-1, 1), (4 * M_HEADS, LANES))

        (hb, mq, mk, mv, gr, aq, ak, av, rq, rk, rv) = _pre_call(
            *stream, modsel, norm1_w[l].reshape(1, D), wqk, m_conv_w[l], wr, wgt, gbt, tabs)

        hm = _mlstm_call(mq, mk, mv, gr, n_ctx)
        sink_rows = jnp.repeat(a_sink[l].reshape(A_KV_HEADS, A_GROUP), A_BLOCK, axis=1)[:, None, :]
        ya = _attn_call(aq, ak, av, sink_rows.astype(F32) * LOG2E, L, not last)
        lgb = jnp.broadcast_to(ret_logit[l].T[:, :, None, None], (R_HEADS, 2, 8, R_V_DIM)).astype(F32)
        hr = _ret_call(rq, rk, rv, lgb, L // RCHUNK)

        t_off = 1 if last else 0
        wgates = jnp.concatenate([w_mo, w_rg, w_gm, w_ga, w_gr], axis=1).astype(BF16)
        x1 = _post_call(*stream, hb, hm, ya, hr, modsel, wgates, m_norm_w[l].reshape(1, -1),
                        ret_norm_w[l].reshape(1, -1), w_br_m[l].astype(BF16), w_br_a[l].astype(BF16),
                        w_br_r[l].astype(BF16), w_out[l].astype(BF16), t_off)
        xc = _ffn_call(x1, modsel, norm2_w[l].reshape(1, D), ffn_up[l][:, :FFN_DIM].astype(BF16),
                       ffn_up[l][:, FFN_DIM:].astype(BF16), ffn_conv_w[l], ffn_conv_b[l].reshape(1, -1),
                       ffn_down[l].astype(BF16), final_norm_w.reshape(1, D), not last, last)
        stream = (None, xc)
    return xc
```

```python
import functools

import jax
import jax.numpy as jnp
import numpy as np
from jax import lax
from jax.experimental import pallas as pl
from jax.experimental.pallas import tpu as pltpu

F32 = jnp.float32
BF16 = jnp.bfloat16

D_MODEL = 1024
GRID_W = 64
NORM_EPS = 1e-6
ROPE_BASE = 10000.0
NEG_INF = -1e30
LOG2E = 1.4426950408889634

M_HEADS = 4
M_HEAD_DIM = 128
M_WIDTH = M_HEADS * M_HEAD_DIM
A_HEADS = 8
A_KV_HEADS = 2
A_GROUP = A_HEADS // A_KV_HEADS
A_HEAD_DIM = 64
A_WIDTH = A_HEADS * A_HEAD_DIM
A_KV_WIDTH = A_KV_HEADS * A_HEAD_DIM
A_WINDOW = 128
A_BLOCK = 128
R_HEADS = 4
R_QK_DIM = 128
R_V_DIM = 256
R_QK_WIDTH = R_HEADS * R_QK_DIM
R_V_WIDTH = R_HEADS * R_V_DIM
FFN_DIM = 2816

IN_SPLITS = (M_WIDTH, M_WIDTH, M_WIDTH, M_WIDTH, 4 * M_HEADS,
             A_WIDTH, A_KV_WIDTH, A_KV_WIDTH,
             R_QK_WIDTH, R_QK_WIDTH, R_V_WIDTH, R_V_WIDTH,
             D_MODEL, D_MODEL, D_MODEL)

TM = 256
BB = 2
HALO = 8
CHUNK = 128
RCHUNK = 256
FFN_NC = 256
LANES = 128
VMEM_LIMIT = 56 * 1024 * 1024


def _sigmoid(x):
    return 1.0 / (1.0 + jnp.exp(-x))


def _log_sigmoid(x):
    return jnp.minimum(x, 0.0) - jnp.log1p(jnp.exp(-jnp.abs(x)))


def _dot(a, b):
    return jnp.dot(a, b, preferred_element_type=F32)


def _dot_nt(a, b):
    return lax.dot_general(a, b, (((1,), (1,)), ((), ())), preferred_element_type=F32)


def _dot_tn(a, b):
    return lax.dot_general(a, b, (((0,), (0,)), ((), ())), preferred_element_type=F32)


def _const_spec(shape):
    nd = len(shape)
    return pl.BlockSpec(shape, lambda *_: (0,) * nd, pipeline_mode=pl.Buffered(1))


def _params(sem):
    return pltpu.CompilerParams(dimension_semantics=sem, vmem_limit_bytes=VMEM_LIMIT)


def _mod_kernel(c_ref, w_ref, b_ref, o_ref):
    c = c_ref[...]
    s = (c * _sigmoid(c)).astype(BF16)
    o_ref[0] = _dot(s, w_ref[0].astype(BF16)) + b_ref[0]


def _modulation(cpad, mod_w, mod_b):
    depth, d, n = mod_w.shape
    rows = cpad.shape[0]
    bn = 1024
    return pl.pallas_call(
        _mod_kernel,
        grid=(depth, n // bn),
        in_specs=[pl.BlockSpec((rows, d), lambda l, j: (0, 0)),
                  pl.BlockSpec((1, d, bn), lambda l, j: (l, 0, j)),
                  pl.BlockSpec((1, 1, bn), lambda l, j: (l, 0, j))],
        out_specs=pl.BlockSpec((1, rows, bn), lambda l, j: (l, 0, j)),
        out_shape=jax.ShapeDtypeStruct((depth, rows, n), F32),
        compiler_params=_params(("parallel", "parallel")),
        name="modulation",
    )(cpad, mod_w, mod_b.reshape(depth, 1, n))


def _norm_mod(xv, nw, sc, sh):
    ms = jnp.mean(xv * xv, axis=-1, keepdims=True)
    return (xv * lax.rsqrt(ms + NORM_EPS) * nw) * (1.0 + sc) + sh


def _seg_cumsum(x, axis, reverse):
    n = x.shape[axis]
    idx = lax.broadcasted_iota(jnp.int32, x.shape, axis)
    s = 1
    while s < n:
        if reverse:
            shifted = pltpu.roll(x, n - s, axis)
            x = x + jnp.where(idx < n - s, shifted, 0.0)
        else:
            shifted = pltpu.roll(x, s, axis)
            x = x + jnp.where(idx >= s, shifted, 0.0)
        s *= 2
    return x


def _gate_transform(raw, gate_axis):
    tok_axis = 1 - gate_axis
    k = lax.broadcasted_iota(jnp.int32, raw.shape, gate_axis) % 4
    lf = _log_sigmoid(raw)
    cum_f = _seg_cumsum(jnp.where(k == 1, lf, 0.0), tok_axis, False)
    cum_b = _seg_cumsum(jnp.where(k == 3, lf, 0.0), tok_axis, True)
    bsel = jnp.where(k == 1, cum_f, jnp.where(k == 3, cum_b, 0.0))
    n = raw.shape[gate_axis]
    bnext = pltpu.roll(bsel, n - 1, gate_axis)
    nt = raw.shape[tok_axis]
    tot_f = lax.slice_in_dim(cum_f, nt - 1, nt, axis=tok_axis)
    tot_b = lax.slice_in_dim(cum_b, 0, 1, axis=tok_axis)
    total = jnp.broadcast_to(tot_f + tot_b, raw.shape)
    return jnp.where(k % 2 == 1, bsel, raw - bnext), total


def _pre_kernel(c_ref, x_ref, xp_ref, xn_ref, mod_ref, nw_ref, wqk_ref, cw_ref, wr_ref, wgt_ref,
                gbt_ref, ca_ref, sa_ref, cr_ref, sr_ref, *out_and_scratch, split):
    for i in range(x_ref.shape[0]):
        one = pl.ds(i, 1)
        _pre_tile(c_ref.at[one] if split else None,
                  x_ref.at[one], xp_ref.at[one], xn_ref.at[one], mod_ref.at[one], nw_ref, wqk_ref, cw_ref,
                  wr_ref, wgt_ref, gbt_ref, ca_ref, sa_ref, cr_ref, sr_ref,
                  *[r.at[one] for r in out_and_scratch[:-1]], out_and_scratch[-1].at[i])


def _tile_rows(c_ref, x_ref):
    if c_ref is None:
        return x_ref[0]
    return jnp.where(pl.program_id(0) == 0, c_ref[0], x_ref[0])


def _pre_tile(c_ref, x_ref, xp_ref, xn_ref, mod_ref, nw_ref, wqk_ref, cw_ref, wr_ref, wgt_ref,
              gbt_ref, ca_ref, sa_ref, cr_ref, sr_ref,
              hb_ref, mq_ref, mk_ref, mv_ref, gr_ref, aq_ref, ak_ref, av_ref,
              rq_ref, rk_ref, rv_ref, hext_ref):
    t = pl.program_id(0)
    nt = pl.num_programs(0)
    sh = mod_ref[0, 0, 0:1, :]
    sc = mod_ref[0, 0, 1:2, :]
    nw = nw_ref[...]
    hb = _norm_mod(_tile_rows(c_ref, x_ref), nw, sc, sh).astype(BF16)
    hb_ref[0] = hb

    grow = _dot_nt(wgt_ref[...], hb) + gbt_ref[:, 0:1]
    for c in range(TM // CHUNK):
        gr, tot = _gate_transform(grow[:, c * CHUNK:(c + 1) * CHUNK], 0)
        aend = gr + pltpu.roll(tot, 4 * M_HEADS - 1, 0)
        amax = jnp.broadcast_to(jnp.max(aend, axis=1, keepdims=True), aend.shape)
        for h in range(M_HEADS):
            hp_, o0 = h // 2, 8 * (h % 2)
            gr_ref[0, hp_, c, o0:o0 + 4, :] = gr[4 * h:4 * h + 4]
            for d in range(2):
                gr_ref[0, hp_, c, o0 + 4 + d:o0 + 5 + d, :] = tot[4 * h + 2 * d + 1:4 * h + 2 * d + 2]
                gr_ref[0, hp_, c, o0 + 6 + d:o0 + 7 + d, :] = amax[4 * h + 2 * d:4 * h + 2 * d + 1]

    hp = _norm_mod(xp_ref[0], nw, sc, sh)
    hn = _norm_mod(xn_ref[0], nw, sc, sh)
    hp = jnp.where(t <= 1, 0.0, hp)
    hn = jnp.where((t == 0) | (t == nt - 1), 0.0, hn)
    hext_ref[0:TM, :] = hb
    hext_ref[TM:, :] = jnp.concatenate([hn, hp], axis=0).astype(BF16)
    hext = hext_ref[...]
    ext = TM + 2 * HALO

    for j, (dst, scale) in enumerate(((mq_ref, 1.0), (mk_ref, M_HEAD_DIM ** -0.5))):
        p = _dot(hext, wqk_ref[:, j * M_WIDTH:(j + 1) * M_WIDTH])
        cw = cw_ref[:, j * M_WIDTH:(j + 1) * M_WIDTH]
        prev = pltpu.roll(p, 1, 0)[0:TM]
        nxt = pltpu.roll(p, ext - 1, 0)[0:TM]
        conv = prev * cw[0:1] + p[0:TM] * cw[1:2] + nxt * cw[2:3]
        act = conv * _sigmoid(conv) * scale
        for h in range(M_HEADS):
            dst[0, h] = act[:, h * M_HEAD_DIM:(h + 1) * M_HEAD_DIM].astype(BF16)

    off = 0
    p = _dot(hb, wr_ref[:, off:off + M_WIDTH])
    off += M_WIDTH
    for h in range(M_HEADS):
        for c in range(TM // CHUNK):
            blk = p[c * CHUNK:(c + 1) * CHUNK, h * M_HEAD_DIM:(h + 1) * M_HEAD_DIM]
            mv_ref[0, h, c] = blk.T.astype(BF16)

    lane = lax.broadcasted_iota(jnp.int32, (TM, LANES), 1)
    first16 = (lane % 32) < 16
    ca = ca_ref[...]
    sa = sa_ref[...]

    def rope_a(xs):
        partner = jnp.where(first16, pltpu.roll(xs, LANES - 16, 1), pltpu.roll(xs, 16, 1))
        return xs * ca + partner * sa

    p = _dot(hb, wr_ref[:, off:off + A_WIDTH])
    off += A_WIDTH
    for s in range(A_WIDTH // LANES):
        r = (rope_a(p[:, s * LANES:(s + 1) * LANES]) * (A_HEAD_DIM ** -0.5 * LOG2E)).astype(BF16)
        aq_ref[0, 2 * s] = r[:, 0:A_HEAD_DIM]
        aq_ref[0, 2 * s + 1] = r[:, A_HEAD_DIM:]
    p = _dot(hb, wr_ref[:, off:off + 2 * A_KV_WIDTH])
    off += 2 * A_KV_WIDTH
    r = rope_a(p[:, 0:LANES]).astype(BF16)
    for h in range(A_KV_HEADS):
        ak_ref[0, h] = r[:, h * A_HEAD_DIM:(h + 1) * A_HEAD_DIM]
    pv = p[:, LANES:]
    for h in range(A_KV_HEADS):
        vh = pv if h == 0 else pltpu.roll(pv, A_HEAD_DIM, 1)
        ve = jnp.where(lane < A_HEAD_DIM, vh, 1.0)
        for c in range(TM // A_BLOCK):
            av_ref[0, h, c] = ve[c * A_BLOCK:(c + 1) * A_BLOCK].T.astype(BF16)

    cr = cr_ref[...]
    sr = sr_ref[...]
    for is_k in (False, True):
        p = _dot(hb, wr_ref[:, off:off + R_QK_WIDTH])
        off += R_QK_WIDTH
        for h in range(R_HEADS):
            xs = p[:, h * R_QK_DIM:(h + 1) * R_QK_DIM]
            rot = xs * cr + pltpu.roll(xs, R_QK_DIM // 2, 1) * sr
            if is_k:
                rot = rot * R_QK_DIM ** -0.5
                for c in range(TM // CHUNK):
                    rk_ref[0, h, c] = rot[c * CHUNK:(c + 1) * CHUNK].T.astype(BF16)
            else:
                rq_ref[0, h] = rot.astype(BF16)
    p = _dot(hb, wr_ref[:, off:off + R_V_WIDTH])
    for h in range(R_HEADS):
        rv_ref[0, h] = p[:, h * R_V_DIM:(h + 1) * R_V_DIM].astype(BF16)


def _stream_specs(ctx, xs):
    hb = TM // HALO
    D = xs.shape[-1]
    off = 0 if ctx is None else 1
    last = xs.shape[1] // HALO - 1
    return [
        pl.BlockSpec((BB, TM if off else HALO, D), lambda t, b: (b, 0, 0)),
        pl.BlockSpec((BB, TM, D), lambda t, b: (b, jnp.maximum(t - off, 0), 0)),
        pl.BlockSpec((BB, HALO, D), lambda t, b: (b, jnp.maximum((t - off) * hb - 1, 0), 0)),
        pl.BlockSpec((BB, HALO, D), lambda t, b: (b, jnp.clip((t - off + 1) * hb, 0, last), 0)),
    ]


def _pre_call(ctx, xs, modsel, nw, wqk, cw, wr, wgt, gbt, tabs):
    B, _, D = xs.shape
    T = xs.shape[1] + (0 if ctx is None else ctx.shape[1])
    nt = T // TM
    nch = T // CHUNK

    def tile(t, b):
        return (b, t, 0)

    def head_tile(t, b):
        return (b, 0, t, 0)

    in_specs = _stream_specs(ctx, xs) + [
        pl.BlockSpec((BB, 1, 6, D), lambda t, b: (b, jnp.minimum(t, 1), 0, 0)),
        _const_spec(nw.shape), _const_spec(wqk.shape), _const_spec(cw.shape), _const_spec(wr.shape),
        _const_spec(wgt.shape), _const_spec(gbt.shape),
    ] + [pl.BlockSpec((TM, LANES), lambda t, b: (t, 0)) for _ in range(4)]
    out_shape = [
        jax.ShapeDtypeStruct((B, T, D), BF16),
        jax.ShapeDtypeStruct((B, M_HEADS, T, M_HEAD_DIM), BF16),
        jax.ShapeDtypeStruct((B, M_HEADS, T, M_HEAD_DIM), BF16),
        jax.ShapeDtypeStruct((B, M_HEADS, nch, M_HEAD_DIM, CHUNK), BF16),
        jax.ShapeDtypeStruct((B, 2, nch, 16, CHUNK), F32),
        jax.ShapeDtypeStruct((B, A_HEADS, T, A_HEAD_DIM), BF16),
        jax.ShapeDtypeStruct((B, A_KV_HEADS, T, A_HEAD_DIM), BF16),
        jax.ShapeDtypeStruct((B, A_KV_HEADS, T // A_BLOCK, LANES, A_BLOCK), BF16),
        jax.ShapeDtypeStruct((B, R_HEADS, T, R_QK_DIM), BF16),
        jax.ShapeDtypeStruct((B, R_HEADS, nch, R_QK_DIM, CHUNK), BF16),
        jax.ShapeDtypeStruct((B, R_HEADS, T, R_V_DIM), BF16),
    ]
    out_specs = [
        pl.BlockSpec((BB, TM, D), tile),
        pl.BlockSpec((BB, M_HEADS, TM, M_HEAD_DIM), head_tile),
        pl.BlockSpec((BB, M_HEADS, TM, M_HEAD_DIM), head_tile),
        pl.BlockSpec((BB, M_HEADS, TM // CHUNK, M_HEAD_DIM, CHUNK), lambda t, b: (b, 0, t, 0, 0)),
        pl.BlockSpec((BB, 2, TM // CHUNK, 16, CHUNK), lambda t, b: (b, 0, t, 0, 0)),
        pl.BlockSpec((BB, A_HEADS, TM, A_HEAD_DIM), head_tile),
        pl.BlockSpec((BB, A_KV_HEADS, TM, A_HEAD_DIM), head_tile),
        pl.BlockSpec((BB, A_KV_HEADS, TM // A_BLOCK, LANES, A_BLOCK), lambda t, b: (b, 0, t, 0, 0)),
        pl.BlockSpec((BB, R_HEADS, TM, R_QK_DIM), head_tile),
        pl.BlockSpec((BB, R_HEADS, TM // CHUNK, R_QK_DIM, CHUNK), lambda t, b: (b, 0, t, 0, 0)),
        pl.BlockSpec((BB, R_HEADS, TM, R_V_DIM), head_tile),
    ]
    return pl.pallas_call(
        functools.partial(_pre_kernel, split=ctx is not None),
        grid=(nt, B // BB),
        in_specs=in_specs,
        out_specs=out_specs,
        out_shape=out_shape,
        scratch_shapes=[pltpu.VMEM((BB, TM + 2 * HALO, D), BF16)],
        compiler_params=_params(("parallel", "parallel")),
        name="pre_proj",
    )(xs if ctx is None else ctx, xs, xs, xs, modsel, nw, wqk, cw, wr, wgt, gbt, *tabs)


def _bwd_chunk(i, n_ctx, n_all):
    if isinstance(i, int):
        return n_ctx - 1 - i if i < n_ctx else n_all - 1 + n_ctx - i
    return n_all - 1 + n_ctx - i


def _chunk_start(c):
    return c * CHUNK if isinstance(c, int) else pl.multiple_of(c * CHUNK, CHUNK)


def _chunk_loop(body, n_ctx, n_all, unroll):
    for i in range(n_ctx):
        body(i, 0)
    n_lat = n_all - n_ctx
    while n_lat % unroll:
        unroll //= 2
    lax.fori_loop(n_ctx, n_all, body, 0, unroll=unroll)


def _mlstm_kernel(q_ref, k_ref, vt_ref, gr_ref, o_ref, c_ref, n_ref, m_ref,
                  cs_ref, ns_ref, ms_ref, qp_ref, nq_ref, *, n_ctx):
    n_all = gr_ref.shape[2]
    dh = q_ref.shape[3]
    c_ref[...] = jnp.zeros(c_ref.shape, F32)
    n_ref[...] = jnp.zeros(n_ref.shape, F32)
    m_ref[...] = jnp.zeros(m_ref.shape, F32)

    def scan_unit(c, hh, d):
        u = 2 * hh + d
        c0 = _chunk_start(c)
        k = k_ref[0, hh, pl.ds(c0, CHUNK), :]
        vt = vt_ref[0, hh, c]
        rows = gr_ref[0, 0, c, 8 * hh:8 * hh + 8, :]
        tot = rows[4 + d:5 + d]
        aend = rows[2 * d:2 * d + 1] + tot
        amax = rows[6 + d:7 + d]
        cst = c_ref[u]
        nv = n_ref[u]
        m = m_ref[u]
        cs_ref[u, c] = cst.astype(BF16)
        ns_ref[u, c] = jnp.broadcast_to(nv, (16, dh)).astype(BF16)
        ms_ref[u, c] = jnp.broadcast_to(m, (8, CHUNK))
        m_new = jnp.maximum(tot + m, amax)
        w = jnp.exp(aend - m_new)
        dec = jnp.exp(tot + m - m_new)
        vw = (vt.astype(F32) * w).astype(BF16)
        c_ref[u] = dec * cst + _dot(vw, k)
        wk = _dot(jnp.broadcast_to(w, (16, CHUNK)).astype(BF16), k)
        n_ref[u] = dec * nv + wk[0:1]
        m_ref[u] = m_new

    def scan_step(i, carry):
        cb = _bwd_chunk(i, n_ctx, n_all)
        for hh in range(2):
            scan_unit(i, hh, 0)
            scan_unit(cb, hh, 1)
        return carry

    _chunk_loop(scan_step, n_ctx, n_all, 32)

    row = lax.broadcasted_iota(jnp.int32, (CHUNK, CHUNK), 0)
    col = lax.broadcasted_iota(jnp.int32, (CHUNK, CHUNK), 1)
    tri = (row <= col, row >= col)

    def query_products(c, slot):
        c0 = _chunk_start(c)
        for hh in range(2):
            q = q_ref[0, hh, pl.ds(c0, CHUNK), :]
            k = k_ref[0, hh, pl.ds(c0, CHUNK), :]
            qp_ref[slot, hh, 0] = _dot_nt(k, q)
            for d in range(2):
                u = 2 * hh + d
                qp_ref[slot, hh, 1 + d] = _dot_nt(cs_ref[u, c], q)
                nq_ref[slot, u] = _dot_nt(ns_ref[u, c], q)[0:8]

    def outputs(c, slot):
        c0 = _chunk_start(c)
        gcv = jnp.concatenate([gr_ref[0, 0, c], jnp.zeros((CHUNK - 16, CHUNK), F32)], axis=0).T
        for hh in range(2):
            vt = vt_ref[0, hh, c]
            rows = gr_ref[0, 0, c, 8 * hh:8 * hh + 8, :]
            st = qp_ref[slot, hh, 0]
            sds, iws, invs = [], [], []
            for d in range(2):
                u = 2 * hh + d
                a0c = gcv[:, 8 * hh + 2 * d:8 * hh + 2 * d + 1]
                b = rows[2 * d + 1:2 * d + 2]
                m = ms_ref[u, c, 0:1, :]
                dlog = jnp.where(tri[d], a0c + b, NEG_INF)
                mj = jnp.maximum(b + m, jnp.max(dlog, axis=0, keepdims=True))
                sd = st * jnp.exp(dlog - mj)
                iw = jnp.exp(b + m - mj)
                den = iw * nq_ref[slot, u, 0:1, :] + jnp.sum(sd, axis=0, keepdims=True)
                sds.append(sd.astype(BF16))
                iws.append(iw)
                invs.append(1.0 / jnp.maximum(jnp.abs(den), jnp.exp(-mj)))
            num = _dot(vt, jnp.concatenate(sds, axis=1))
            acc = None
            for d in range(2):
                ht = (num[:, d * CHUNK:(d + 1) * CHUNK] + iws[d] * qp_ref[slot, hh, 1 + d]) * invs[d]
                acc = ht if acc is None else acc + ht
            o_ref[0, hh, pl.ds(c0, CHUNK), :] = acc.T

    query_products(0, 0)

    def out_pair(i, carry):
        c = 2 * i
        query_products(c + 1, 1)
        outputs(c, 0)
        query_products(jnp.minimum(c + 2, n_all - 1), 0)
        outputs(c + 1, 1)
        return carry

    n_pairs = n_all // 2
    unroll = 8
    first = n_pairs % unroll
    for i in range(first):
        out_pair(i, 0)
    lax.fori_loop(first, n_pairs, out_pair, 0, unroll=unroll)


def _mlstm_call(mq, mk, mvt, gr, n_ctx):
    B, H, T, dh = mq.shape
    nch = T // CHUNK
    qspec = pl.BlockSpec((1, 2, T, dh), lambda b, p: (b, p, 0, 0))
    return pl.pallas_call(
        functools.partial(_mlstm_kernel, n_ctx=n_ctx),
        grid=(B, H // 2),
        in_specs=[qspec, qspec,
                  pl.BlockSpec((1, 2, nch, dh, CHUNK), lambda b, p: (b, p, 0, 0, 0)),
                  pl.BlockSpec((1, 1, nch, 16, CHUNK), lambda b, p: (b, p, 0, 0, 0))],
        out_specs=pl.BlockSpec((1, 2, T, dh), lambda b, p: (b, p, 0, 0)),
        out_shape=jax.ShapeDtypeStruct((B, H, T, dh), F32),
        scratch_shapes=[pltpu.VMEM((4, dh, dh), F32), pltpu.VMEM((4, 1, dh), F32),
                        pltpu.VMEM((4, 1, CHUNK), F32),
                        pltpu.VMEM((4, nch, dh, dh), BF16), pltpu.VMEM((4, nch, 16, dh), BF16),
                        pltpu.VMEM((4, nch, 8, CHUNK), F32),
                        pltpu.VMEM((2, 2, 3, dh, CHUNK), F32), pltpu.VMEM((2, 4, 8, CHUNK), F32)],
        compiler_params=_params(("parallel", "parallel")),
        name="mlstm",
    )(mq, mk, mvt, gr)


def _ret_kernel(q_ref, kt_ref, v_ref, lg_ref, o_ref, st_ref, sts_ref, dec_ref, ib_ref, wb_ref, cd_ref,
                *, n_ctx):
    sub = RCHUNK // CHUNK
    n_all = kt_ref.shape[2] // sub
    dk = q_ref.shape[3]
    dv = v_ref.shape[3]
    st_ref[...] = jnp.zeros(st_ref.shape, F32)

    @pl.when(pl.program_id(1) == 0)
    def _():
        row = lax.broadcasted_iota(jnp.int32, (RCHUNK, RCHUNK), 0).astype(F32)
        col = lax.broadcasted_iota(jnp.int32, (RCHUNK, RCHUNK), 1).astype(F32)
        kcol = lax.broadcasted_iota(jnp.int32, (dk, RCHUNK), 1).astype(F32)
        dec = None
        for d in range(2):
            lg = _log_sigmoid(lg_ref[0, d])[0:1]
            diff = row - col if d == 0 else col - row
            dd = jnp.where(diff >= 0, jnp.exp(lg * jnp.maximum(diff, 0.0)), 0.0)
            dec = dd if dec is None else dec + dd
            pin = row if d == 0 else RCHUNK - 1.0 - row
            ib_ref[d] = jnp.exp(lg * (pin + 1.0))
            pkey = kcol if d == 0 else RCHUNK - 1.0 - kcol
            wb_ref[d] = jnp.exp(lg * (RCHUNK - 1.0 - pkey))
            cd_ref[d] = jnp.broadcast_to(jnp.exp(lg * RCHUNK), (8, dv))
        dec_ref[...] = dec

    cds = [cd_ref[d, 0:1, :] for d in range(2)]

    def rows(c):
        r0 = c * RCHUNK
        return pl.ds(r0 if isinstance(c, int) else pl.multiple_of(r0, RCHUNK), RCHUNK)

    def k_t(c):
        return jnp.concatenate([kt_ref[0, 0, sub * c + j] for j in range(sub)], axis=1)

    def scan_step(i, carry):
        cb = _bwd_chunk(i, n_ctx, n_all)
        for d, c in ((0, i), (1, cb)):
            st = st_ref[d]
            sts_ref[c, :, d * dv:(d + 1) * dv] = st.astype(BF16)
            kw = (k_t(c).astype(F32) * wb_ref[d]).astype(BF16)
            st_ref[d] = cds[d] * st + _dot(kw, v_ref[0, 0, rows(c), :])
        return carry

    _chunk_loop(scan_step, n_ctx, n_all, 16)

    def out_step(c, carry):
        q = q_ref[0, 0, rows(c), :]
        s = (_dot(q, k_t(c)) * dec_ref[...]).astype(BF16)
        qst = _dot(q, sts_ref[c])
        o_ref[0, 0, rows(c), :] = (_dot(s, v_ref[0, 0, rows(c), :])
                                   + ib_ref[0] * qst[:, 0:dv] + ib_ref[1] * qst[:, dv:])
        return carry

    _chunk_loop(out_step, n_ctx, n_all, 16)


def _ret_call(rq, rkt, rv, lgb, n_ctx):
    B, H, T, dk = rq.shape
    dv = rv.shape[-1]
    nch = T // CHUNK
    assert dv == RCHUNK and dk <= RCHUNK and T % RCHUNK == 0
    return pl.pallas_call(
        functools.partial(_ret_kernel, n_ctx=n_ctx),
        grid=(H, B),
        in_specs=[pl.BlockSpec((1, 1, T, dk), lambda h, b: (b, h, 0, 0)),
                  pl.BlockSpec((1, 1, nch, dk, CHUNK), lambda h, b: (b, h, 0, 0, 0)),
                  pl.BlockSpec((1, 1, T, dv), lambda h, b: (b, h, 0, 0)),
                  pl.BlockSpec((1, 2, 8, dv), lambda h, b: (h, 0, 0, 0))],
        out_specs=pl.BlockSpec((1, 1, T, dv), lambda h, b: (b, h, 0, 0)),
        out_shape=jax.ShapeDtypeStruct((B, H, T, dv), F32),
        scratch_shapes=[pltpu.VMEM((2, dk, dv), F32), pltpu.VMEM((T // RCHUNK, dk, 2 * dv), BF16),
                        pltpu.VMEM((RCHUNK, RCHUNK), F32), pltpu.VMEM((2, RCHUNK, dv), F32),
                        pltpu.VMEM((2, dk, RCHUNK), F32), pltpu.VMEM((2, 8, dv), F32)],
        compiler_params=_params(("parallel", "arbitrary")),
        name="retention",
    )(rq, rkt, rv, lgb)


def _attn_kernel(q_ref, k_ref, vt_ref, sink_ref, o_ref, bias_ref, s_ref, *, n_ctx_tok, need_ctx):
    T = q_ref.shape[2]
    L = n_ctx_tok
    S = T - L
    nb = S // A_BLOCK
    nspan = 3
    span = nspan * A_BLOCK
    ncb = L // A_BLOCK
    cols = A_GROUP * A_BLOCK
    kc = k_ref[0, 0, 0:L, :]
    vtc = jnp.concatenate([vt_ref[0, 0, j] for j in range(ncb)], axis=1)
    sink = sink_ref[0]

    @pl.when((pl.program_id(0) == 0) & (pl.program_id(1) == 0))
    def _():
        kpos = lax.broadcasted_iota(jnp.int32, (span, A_BLOCK), 0)
        qpos = lax.broadcasted_iota(jnp.int32, (span, A_BLOCK), 1)
        for delta in range(nspan):
            bias_ref[delta] = jnp.where(jnp.abs(qpos + delta * A_BLOCK - kpos) <= A_WINDOW, 0.0, NEG_INF)

    def finish(blk, acc, m):
        den = acc[A_HEAD_DIM:A_HEAD_DIM + 1] + jnp.exp2(sink - m)
        ot = (acc[0:A_HEAD_DIM] * (1.0 / den)).astype(BF16)
        for g in range(A_GROUP):
            o_ref[0, blk, g * A_HEAD_DIM:(g + 1) * A_HEAD_DIM, :] = ot[:, g * A_BLOCK:(g + 1) * A_BLOCK]

    for cbk in range(ncb):
        if need_ctx:
            qs = q_ref[0, :, cbk * A_BLOCK:(cbk + 1) * A_BLOCK, :].reshape(cols, A_HEAD_DIM)
            s = _dot_nt(kc, qs)
            m = jnp.maximum(jnp.max(s, axis=0, keepdims=True), sink)
            finish(cbk, _dot(vtc, jnp.exp2(s - m).astype(BF16)), m)
        else:
            o_ref[0, cbk] = jnp.zeros(o_ref.shape[2:], BF16)

    def first_key_block(n):
        return jnp.clip(n - 1, 0, nb - nspan)

    def scores(n, slot):
        q0 = pl.multiple_of(L + n * A_BLOCK, A_BLOCK)
        kb = first_key_block(n)
        k0 = pl.multiple_of(L + kb * A_BLOCK, A_BLOCK)
        qs = q_ref[0, :, pl.ds(q0, A_BLOCK), :].reshape(cols, A_HEAD_DIM)
        bias = bias_ref[n - kb]
        s_ref[slot, 0:span, :] = (_dot_nt(k_ref[0, 0, pl.ds(k0, span), :], qs)
                                  + jnp.concatenate([bias] * A_GROUP, axis=1))
        s_ref[slot, span:, :] = _dot_nt(kc, qs)

    def softmax_pv(n, slot):
        kb = first_key_block(n)
        s = s_ref[slot]
        m = jnp.maximum(jnp.max(s, axis=0, keepdims=True), sink)
        p = jnp.exp2(s - m).astype(BF16)
        vt = jnp.concatenate([vt_ref[0, 0, ncb + kb + j] for j in range(nspan)] + [vtc], axis=1)
        finish(ncb + n, _dot(vt, p), m)

    scores(0, 0)

    def block_pair(i, carry):
        n = 2 * i
        scores(n + 1, 1)
        softmax_pv(n, 0)
        scores(jnp.minimum(n + 2, nb - 1), 0)
        softmax_pv(n + 1, 1)
        return carry

    n_pairs = nb // 2
    unroll = 8
    first = n_pairs % unroll
    for i in range(first):
        block_pair(i, 0)
    lax.fori_loop(first, n_pairs, block_pair, 0, unroll=unroll)


def _attn_call(aq, ak, avt, sink_rows, n_ctx_tok, need_ctx):
    B, _, T, hd = aq.shape
    nblk = T // A_BLOCK
    cols = A_GROUP * A_BLOCK
    return pl.pallas_call(
        functools.partial(_attn_kernel, n_ctx_tok=n_ctx_tok, need_ctx=need_ctx),
        grid=(B, A_KV_HEADS),
        in_specs=[pl.BlockSpec((1, A_GROUP, T, hd), lambda b, h: (b, h, 0, 0)),
                  pl.BlockSpec((1, 1, T, hd), lambda b, h: (b, h, 0, 0)),
                  pl.BlockSpec((1, 1, nblk, LANES, A_BLOCK), lambda b, h: (b, h, 0, 0, 0)),
                  pl.BlockSpec((1, 1, cols), lambda b, h: (h, 0, 0))],
        out_specs=pl.BlockSpec((1, nblk, A_GROUP * hd, A_BLOCK), lambda b, h: (b, 0, h, 0)),
        out_shape=jax.ShapeDtypeStruct((B, nblk, A_WIDTH, A_BLOCK), BF16),
        scratch_shapes=[pltpu.VMEM((3, 3 * A_BLOCK, A_BLOCK), F32),
                        pltpu.VMEM((2, 3 * A_BLOCK + n_ctx_tok, cols), F32)],
        compiler_params=_params(("arbitrary", "arbitrary")),
        name="window_attn",
    )(aq, ak, avt, sink_rows)


def _head_norm(y):
    mu = jnp.mean(y, axis=-1, keepdims=True)
    yc = y - mu
    var = jnp.mean(yc * yc, axis=-1, keepdims=True)
    return yc * lax.rsqrt(var + NORM_EPS)


def _post_kernel(c_ref, x_ref, hb_ref, hm_ref, ya_ref, hr_ref, mod_ref, wg_ref, mnw_ref, rnw_ref,
                 wbm_ref, wba_ref, wbr_ref, wo_ref, o_ref, *, split):
    for i in range(x_ref.shape[0]):
        hb = hb_ref[i]
        off = 0
        hm = jnp.concatenate([_head_norm(hm_ref[i, h]) for h in range(M_HEADS)], axis=1) * mnw_ref[...]
        ym = (_sigmoid(_dot(hb, wg_ref[:, off:off + M_WIDTH])) * hm).astype(BF16)
        off += M_WIDTH
        hr = jnp.concatenate([_head_norm(hr_ref[i, h]) for h in range(R_HEADS)], axis=1) * rnw_ref[...]
        rg = _dot(hb, wg_ref[:, off:off + R_V_WIDTH])
        yr = (rg * _sigmoid(rg) * hr).astype(BF16)
        off += R_V_WIDTH
        z = _sigmoid(_dot(hb, wg_ref[:, off:off + D_MODEL])) * _dot(ym, wbm_ref[...])
        off += D_MODEL
        pa = jnp.concatenate([_dot_tn(ya_ref[i, c], wba_ref[...]) for c in range(TM // A_BLOCK)], axis=0)
        z = z + _sigmoid(_dot(hb, wg_ref[:, off:off + D_MODEL])) * pa
        off += D_MODEL
        z = z + _sigmoid(_dot(hb, wg_ref[:, off:off + D_MODEL])) * _dot(yr, wbr_ref[...])
        y = _dot(z.astype(BF16), wo_ref[...])
        one = pl.ds(i, 1)
        xres = _tile_rows(c_ref.at[one] if split else None, x_ref.at[one])
        o_ref[i] = xres + mod_ref[i, 0, 2:3, :] * y


def _post_call(ctx, xs, hb, hm, ya, hr, modsel, wg, mnw, rnw, wbm, wba, wbr, wo, t_off):
    B, T, D = hb.shape
    nt = T // TM - t_off
    split = ctx is not None
    assert not (split and t_off)

    def tile(t, b):
        return (b, t + t_off, 0)

    def head_tile(t, b):
        return (b, 0, t + t_off, 0)

    return pl.pallas_call(
        functools.partial(_post_kernel, split=split),
        grid=(nt, B // BB),
        in_specs=[pl.BlockSpec((BB, TM if split else HALO, D), lambda t, b: (b, 0, 0)),
                  pl.BlockSpec((BB, TM, D), (lambda t, b: (b, jnp.maximum(t - 1, 0), 0)) if split else tile),
                  pl.BlockSpec((BB, TM, D), tile),
                  pl.BlockSpec((BB, M_HEADS, TM, M_HEAD_DIM), head_tile),
                  pl.BlockSpec((BB, TM // A_BLOCK, A_WIDTH, A_BLOCK), lambda t, b: (b, t + t_off, 0, 0)),
                  pl.BlockSpec((BB, R_HEADS, TM, R_V_DIM), head_tile),
                  pl.BlockSpec((BB, 1, 6, D), lambda t, b: (b, jnp.minimum(t + t_off, 1), 0, 0)),
                  _const_spec(wg.shape), _const_spec(mnw.shape), _const_spec(rnw.shape),
                  _const_spec(wbm.shape), _const_spec(wba.shape), _const_spec(wbr.shape),
                  _const_spec(wo.shape)],
        out_specs=pl.BlockSpec((BB, TM, D), lambda t, b: (b, t, 0)),
        out_shape=jax.ShapeDtypeStruct((B, nt * TM, D), F32),
        compiler_params=_params(("parallel", "parallel")),
        name="post_merge",
    )(ctx if split else xs, xs, hb, hm, ya, hr, modsel, wg, mnw, rnw, wbm, wba, wbr, wo)


def _ffn_kernel(x_ref, xp_ref, xn_ref, mod_ref, nw_ref, wa_ref, wb_ref, cw_ref, cb_ref, wd_ref, fw_ref,
                o_ref, hext_ref, u_ref, *, has_ctx, final):
    t = pl.program_id(0)
    nt = pl.num_programs(0)
    first_lat = 1 if has_ctx else 0
    nw = nw_ref[...]
    ext = TM + 2 * HALO
    for i in range(x_ref.shape[0]):
        sh = mod_ref[i, 0, 3:4, :]
        sc = mod_ref[i, 0, 4:5, :]
        g2 = mod_ref[i, 0, 5:6, :]
        x = x_ref[i]
        hp = _norm_mod(xp_ref[i], nw, sc, sh)
        hn = _norm_mod(xn_ref[i], nw, sc, sh)
        hp = jnp.where(t <= first_lat, 0.0, hp)
        hn = jnp.where((t == first_lat - 1) | (t == nt - 1), 0.0, hn)
        hext_ref[i, 0:TM, :] = _norm_mod(x, nw, sc, sh).astype(BF16)
        hext_ref[i, TM:, :] = jnp.concatenate([hn, hp], axis=0).astype(BF16)
        for j in range(FFN_DIM // FFN_NC):
            cs = slice(j * FFN_NC, (j + 1) * FFN_NC)
            a = _dot(hext_ref[i], wa_ref[:, cs])
            b = _dot(hext_ref[i, 0:TM, :], wb_ref[:, cs])
            cw = cw_ref[:, cs]
            prev = pltpu.roll(a, 1, 0)[0:TM]
            nxt = pltpu.roll(a, ext - 1, 0)[0:TM]
            conv = prev * cw[0:1] + a[0:TM] * cw[1:2] + nxt * cw[2:3] + cb_ref[:, cs]
            u_ref[i, :, cs] = (conv * _sigmoid(conv) * b).astype(BF16)
        y = x + g2 * _dot(u_ref[i], wd_ref[...])
        if final:
            ms = jnp.mean(y * y, axis=-1, keepdims=True)
            y = y * lax.rsqrt(ms + NORM_EPS) * fw_ref[...]
        o_ref[i] = y


def _ffn_call(xs, modsel, nw, wa, wb, cw, cb, wd, fw, has_ctx, final):
    B, T, D = xs.shape
    nt = T // TM
    hb16 = TM // HALO
    first_lat = 1 if has_ctx else 0
    tile = pl.BlockSpec((BB, TM, D), lambda t, b: (b, t, 0))
    return pl.pallas_call(
        functools.partial(_ffn_kernel, has_ctx=has_ctx, final=final),
        grid=(nt, B // BB),
        in_specs=[tile,
                  pl.BlockSpec((BB, HALO, D), lambda t, b: (b, jnp.maximum(t * hb16 - 1, 0), 0)),
                  pl.BlockSpec((BB, HALO, D), lambda t, b: (b, jnp.minimum((t + 1) * hb16, T // HALO - 1), 0)),
                  pl.BlockSpec((BB, 1, 6, D), lambda t, b: (b, jnp.minimum(t + 1 - first_lat, 1), 0, 0)),
                  _const_spec(nw.shape), _const_spec(wa.shape), _const_spec(wb.shape),
                  _const_spec(cw.shape), _const_spec(cb.shape), _const_spec(wd.shape),
                  _const_spec(fw.shape)],
        out_specs=tile,
        out_shape=jax.ShapeDtypeStruct((B, T, D), F32),
        scratch_shapes=[pltpu.VMEM((BB, TM + 2 * HALO, D), BF16), pltpu.VMEM((BB, TM, FFN_DIM), BF16)],
        compiler_params=_params(("parallel", "parallel")),
        name="conv_ffn",
    )(xs, xs, xs, modsel, nw, wa, wb, cw, cb, wd, fw)


def _split_cols(w):
    out = []
    acc = 0
    for s in IN_SPLITS:
        out.append(w[:, acc:acc + s])
        acc += s
    return out


def _rope_tables(L, S):
    T = L + S
    f32 = np.float32
    j = np.arange(LANES)
    jj = j % 32
    inv_a = np.power(f32(ROPE_BASE), -(jj % 16).astype(f32) / f32(16.0)).astype(f32)
    t = np.arange(S)
    pos = np.where(((j % A_HEAD_DIM) // 32 == 0)[None, :], (t // GRID_W)[:, None], (t % GRID_W)[:, None])
    ang = pos.astype(f32) * inv_a[None, :]
    sign = np.where(jj < 16, f32(-1.0), f32(1.0))[None, :]
    ca = np.concatenate([np.ones((L, LANES), f32), np.cos(ang)], axis=0)
    sa = np.concatenate([np.zeros((L, LANES), f32), np.sin(ang) * sign], axis=0)
    half = R_QK_DIM // 2
    inv_r = np.power(f32(ROPE_BASE), -(j % half).astype(f32) / f32(half)).astype(f32)
    angr = np.arange(T).astype(f32)[:, None] * inv_r[None, :]
    signr = np.where(j < half, f32(-1.0), f32(1.0))[None, :]
    return tuple(jnp.asarray(a, F32) for a in (ca, sa, np.cos(angr), np.sin(angr) * signr))


def kernel(x, c, ctx, c_ctx, mod_w, mod_b, norm1_w, norm2_w, w_in, m_gate_bias, m_conv_w, m_norm_w,
           a_sink, ret_logit, ret_norm_w, w_br_m, w_br_a, w_br_r, w_out, ffn_up, ffn_conv_w, ffn_conv_b,
           ffn_down, final_norm_w):
    B, S, D = x.shape
    L = ctx.shape[1]
    depth = mod_w.shape[0]
    assert D == D_MODEL and L == TM and S % TM == 0 and S >= 3 * A_BLOCK and B % BB == 0
    T = L + S
    n_ctx = L // CHUNK

    rows = -(-(B + 1) // 8) * 8
    cpad = jnp.zeros((rows, D), F32).at[:B].set(c).at[B].set(c_ctx)
    mods = _modulation(cpad, mod_w, mod_b)
    tabs = _rope_tables(L, S)
    stream = (ctx, x) if depth > 1 else (None, jnp.concatenate([ctx, x], axis=1))

    gperm = jnp.arange(4 * M_HEADS).reshape(4, M_HEADS).T.reshape(-1)

    for l in range(depth):
        last = l == depth - 1
        lat = mods[l, :B].reshape(B, 6, D)
        cm = jnp.broadcast_to(mods[l, B].reshape(1, 6, D), (B, 6, D))
        modsel = jnp.stack([cm, lat], axis=1)

        (w_mq, w_mk, w_mv, w_mo, w_mg, w_aq, w_ak, w_av,
         w_rq, w_rk, w_rv, w_rg, w_gm, w_ga, w_gr) = _split_cols(w_in[l])
        wqk = jnp.concatenate([w_mq, w_mk], axis=1).astype(BF16)
        wr = jnp.concatenate([w_mv, w_aq, w_ak, w_av, w_rq, w_rk, w_rv], axis=1).astype(BF16)
        wgt = w_mg[:, gperm].T.astype(BF16)
        gbt = jnp.broadcast_to(m_gate_bias[l][gperm].reshape(-1, 1), (4 * M_HEADS, LANES))

        (hb, mq, mk, mv, gr, aq, ak, av, rq, rk, rv) = _pre_call(
            *stream, modsel, norm1_w[l].reshape(1, D), wqk, m_conv_w[l], wr, wgt, gbt, tabs)

        hm = _mlstm_call(mq, mk, mv, gr, n_ctx)
        sink_rows = jnp.repeat(a_sink[l].reshape(A_KV_HEADS, A_GROUP), A_BLOCK, axis=1)[:, None, :]
        ya = _attn_call(aq, ak, av, sink_rows.astype(F32) * LOG2E, L, not last)
        lgb = jnp.broadcast_to(ret_logit[l].T[:, :, None, None], (R_HEADS, 2, 8, R_V_DIM)).astype(F32)
        hr = _ret_call(rq, rk, rv, lgb, L // RCHUNK)

        t_off = 1 if last else 0
        wgates = jnp.concatenate([w_mo, w_rg, w_gm, w_ga, w_gr], axis=1).astype(BF16)
        x1 = _post_call(*stream, hb, hm, ya, hr, modsel, wgates, m_norm_w[l].reshape(1, -1),
                        ret_norm_w[l].reshape(1, -1), w_br_m[l].astype(BF16), w_br_a[l].astype(BF16),
                        w_br_r[l].astype(BF16), w_out[l].astype(BF16), t_off)
        xc = _ffn_call(x1, modsel, norm2_w[l].reshape(1, D), ffn_up[l][:, :FFN_DIM].astype(BF16),
                       ffn_up[l][:, FFN_DIM:].astype(BF16), ffn_conv_w[l], ffn_conv_b[l].reshape(1, -1),
                       ffn_down[l].astype(BF16), final_norm_w.reshape(1, D), not last, last)
        stream = (None, xc)
    return xc
```

```python
import functools

import jax
import jax.numpy as jnp
import numpy as np
from jax import lax
from jax.experimental import pallas as pl
from jax.experimental.pallas import tpu as pltpu

F32 = jnp.float32
BF16 = jnp.bfloat16

D_MODEL = 1024
GRID_W = 64
NORM_EPS = 1e-6
ROPE_BASE = 10000.0
NEG_INF = -1e30
LOG2E = 1.4426950408889634

M_HEADS = 4
M_HEAD_DIM = 128
M_WIDTH = M_HEADS * M_HEAD_DIM
A_HEADS = 8
A_KV_HEADS = 2
A_GROUP = A_HEADS // A_KV_HEADS
A_HEAD_DIM = 64
A_WIDTH = A_HEADS * A_HEAD_DIM
A_KV_WIDTH = A_KV_HEADS * A_HEAD_DIM
A_WINDOW = 128
A_BLOCK = 128
R_HEADS = 4
R_QK_DIM = 128
R_V_DIM = 256
R_QK_WIDTH = R_HEADS * R_QK_DIM
R_V_WIDTH = R_HEADS * R_V_DIM
FFN_DIM = 2816

IN_SPLITS = (M_WIDTH, M_WIDTH, M_WIDTH, M_WIDTH, 4 * M_HEADS,
             A_WIDTH, A_KV_WIDTH, A_KV_WIDTH,
             R_QK_WIDTH, R_QK_WIDTH, R_V_WIDTH, R_V_WIDTH,
             D_MODEL, D_MODEL, D_MODEL)

TM = 256
BB = 2
HALO = 8
CHUNK = 128
RCHUNK = 256
FFN_NC = 256
LANES = 128
VMEM_LIMIT = 56 * 1024 * 1024


def _sigmoid(x):
    return 1.0 / (1.0 + jnp.exp(-x))


def _log_sigmoid(x):
    return jnp.minimum(x, 0.0) - jnp.log1p(jnp.exp(-jnp.abs(x)))


def _dot(a, b):
    return jnp.dot(a, b, preferred_element_type=F32)


def _dot_nt(a, b):
    return lax.dot_general(a, b, (((1,), (1,)), ((), ())), preferred_element_type=F32)


def _dot_tn(a, b):
    return lax.dot_general(a, b, (((0,), (0,)), ((), ())), preferred_element_type=F32)


def _const_spec(shape):
    nd = len(shape)
    return pl.BlockSpec(shape, lambda *_: (0,) * nd, pipeline_mode=pl.Buffered(1))


def _params(sem):
    return pltpu.CompilerParams(dimension_semantics=sem, vmem_limit_bytes=VMEM_LIMIT)


def _mod_kernel(c_ref, w_ref, b_ref, o_ref):
    c = c_ref[...]
    s = (c * _sigmoid(c)).astype(BF16)
    o_ref[0] = _dot(s, w_ref[0].astype(BF16)) + b_ref[0]


def _modulation(cpad, mod_w, mod_b):
    depth, d, n = mod_w.shape
    rows = cpad.shape[0]
    bn = 1024
    return pl.pallas_call(
        _mod_kernel,
        grid=(depth, n // bn),
        in_specs=[pl.BlockSpec((rows, d), lambda l, j: (0, 0)),
                  pl.BlockSpec((1, d, bn), lambda l, j: (l, 0, j)),
                  pl.BlockSpec((1, 1, bn), lambda l, j: (l, 0, j))],
        out_specs=pl.BlockSpec((1, rows, bn), lambda l, j: (l, 0, j)),
        out_shape=jax.ShapeDtypeStruct((depth, rows, n), F32),
        compiler_params=_params(("parallel", "parallel")),
        name="modulation",
    )(cpad, mod_w, mod_b.reshape(depth, 1, n))


def _norm_mod(xv, nw, sc, sh):
    ms = jnp.mean(xv * xv, axis=-1, keepdims=True)
    return (xv * lax.rsqrt(ms + NORM_EPS) * nw) * (1.0 + sc) + sh


def _seg_cumsum(x, axis, reverse):
    n = x.shape[axis]
    idx = lax.broadcasted_iota(jnp.int32, x.shape, axis)
    s = 1
    while s < n:
        if reverse:
            shifted = pltpu.roll(x, n - s, axis)
            x = x + jnp.where(idx < n - s, shifted, 0.0)
        else:
            shifted = pltpu.roll(x, s, axis)
            x = x + jnp.where(idx >= s, shifted, 0.0)
        s *= 2
    return x


def _gate_transform(raw, gate_axis):
    tok_axis = 1 - gate_axis
    k = lax.broadcasted_iota(jnp.int32, raw.shape, gate_axis) % 4
    lf = _log_sigmoid(raw)
    cum_f = _seg_cumsum(jnp.where(k == 1, lf, 0.0), tok_axis, False)
    cum_b = _seg_cumsum(jnp.where(k == 3, lf, 0.0), tok_axis, True)
    bsel = jnp.where(k == 1, cum_f, jnp.where(k == 3, cum_b, 0.0))
    n = raw.shape[gate_axis]
    bnext = pltpu.roll(bsel, n - 1, gate_axis)
    nt = raw.shape[tok_axis]
    tot_f = lax.slice_in_dim(cum_f, nt - 1, nt, axis=tok_axis)
    tot_b = lax.slice_in_dim(cum_b, 0, 1, axis=tok_axis)
    total = jnp.broadcast_to(tot_f + tot_b, raw.shape)
    return jnp.where(k % 2 == 1, bsel, raw - bnext), total


def _pre_kernel(c_ref, x_ref, xp_ref, xn_ref, mod_ref, nw_ref, wqk_ref, cw_ref, wr_ref, wgt_ref,
                gbt_ref, ca_ref, sa_ref, cr_ref, sr_ref, *out_and_scratch, split):
    for i in range(x_ref.shape[0]):
        one = pl.ds(i, 1)
        _pre_tile(c_ref.at[one] if split else None,
                  x_ref.at[one], xp_ref.at[one], xn_ref.at[one], mod_ref.at[one], nw_ref, wqk_ref, cw_ref,
                  wr_ref, wgt_ref, gbt_ref, ca_ref, sa_ref, cr_ref, sr_ref,
                  *[r.at[one] for r in out_and_scratch[:-1]], out_and_scratch[-1].at[i])


def _tile_rows(c_ref, x_ref):
    if c_ref is None:
        return x_ref[0]
    return jnp.where(pl.program_id(0) == 0, c_ref[0], x_ref[0])


def _pre_tile(c_ref, x_ref, xp_ref, xn_ref, mod_ref, nw_ref, wqk_ref, cw_ref, wr_ref, wgt_ref,
              gbt_ref, ca_ref, sa_ref, cr_ref, sr_ref,
              hb_ref, mq_ref, mk_ref, mv_ref, gr_ref, aq_ref, ak_ref, av_ref,
              rq_ref, rk_ref, rv_ref, hext_ref):
    t = pl.program_id(0)
    nt = pl.num_programs(0)
    sh = mod_ref[0, 0, 0:1, :]
    sc = mod_ref[0, 0, 1:2, :]
    nw = nw_ref[...]
    hb = _norm_mod(_tile_rows(c_ref, x_ref), nw, sc, sh).astype(BF16)
    hb_ref[0] = hb

    grow = _dot_nt(wgt_ref[...], hb) + gbt_ref[:, 0:1]
    for c in range(TM // CHUNK):
        gr, tot = _gate_transform(grow[:, c * CHUNK:(c + 1) * CHUNK], 0)
        aend = gr + pltpu.roll(tot, 4 * M_HEADS - 1, 0)
        amax = jnp.broadcast_to(jnp.max(aend, axis=1, keepdims=True), aend.shape)
        for h in range(M_HEADS):
            hp_, o0 = h // 2, 8 * (h % 2)
            gr_ref[0, hp_, c, o0:o0 + 4, :] = gr[4 * h:4 * h + 4]
            for d in range(2):
                gr_ref[0, hp_, c, o0 + 4 + d:o0 + 5 + d, :] = tot[4 * h + 2 * d + 1:4 * h + 2 * d + 2]
                gr_ref[0, hp_, c, o0 + 6 + d:o0 + 7 + d, :] = amax[4 * h + 2 * d:4 * h + 2 * d + 1]

    hp = _norm_mod(xp_ref[0], nw, sc, sh)
    hn = _norm_mod(xn_ref[0], nw, sc, sh)
    hp = jnp.where(t <= 1, 0.0, hp)
    hn = jnp.where((t == 0) | (t == nt - 1), 0.0, hn)
    hext_ref[0:TM, :] = hb
    hext_ref[TM:, :] = jnp.concatenate([hn, hp], axis=0).astype(BF16)
    hext = hext_ref[...]
    ext = TM + 2 * HALO

    for j, (dst, scale) in enumerate(((mq_ref, 1.0), (mk_ref, M_HEAD_DIM ** -0.5))):
        p = _dot(hext, wqk_ref[:, j * M_WIDTH:(j + 1) * M_WIDTH])
        cw = cw_ref[:, j * M_WIDTH:(j + 1) * M_WIDTH]
        prev = pltpu.roll(p, 1, 0)[0:TM]
        nxt = pltpu.roll(p, ext - 1, 0)[0:TM]
        conv = prev * cw[0:1] + p[0:TM] * cw[1:2] + nxt * cw[2:3]
        act = conv * _sigmoid(conv) * scale
        for h in range(M_HEADS):
            dst[0, h] = act[:, h * M_HEAD_DIM:(h + 1) * M_HEAD_DIM].astype(BF16)

    off = 0
    p = _dot(hb, wr_ref[:, off:off + M_WIDTH])
    off += M_WIDTH
    for h in range(M_HEADS):
        for c in range(TM // CHUNK):
            blk = p[c * CHUNK:(c + 1) * CHUNK, h * M_HEAD_DIM:(h + 1) * M_HEAD_DIM]
            mv_ref[0, h, c] = blk.T.astype(BF16)

    lane = lax.broadcasted_iota(jnp.int32, (TM, LANES), 1)
    first16 = (lane % 32) < 16
    ca = ca_ref[...]
    sa = sa_ref[...]

    def rope_a(xs):
        partner = jnp.where(first16, pltpu.roll(xs, LANES - 16, 1), pltpu.roll(xs, 16, 1))
        return xs * ca + partner * sa

    p = _dot(hb, wr_ref[:, off:off + A_WIDTH])
    off += A_WIDTH
    for s in range(A_WIDTH // LANES):
        r = (rope_a(p[:, s * LANES:(s + 1) * LANES]) * (A_HEAD_DIM ** -0.5 * LOG2E)).astype(BF16)
        aq_ref[0, 2 * s] = r[:, 0:A_HEAD_DIM]
        aq_ref[0, 2 * s + 1] = r[:, A_HEAD_DIM:]
    p = _dot(hb, wr_ref[:, off:off + 2 * A_KV_WIDTH])
    off += 2 * A_KV_WIDTH
    r = rope_a(p[:, 0:LANES]).astype(BF16)
    for h in range(A_KV_HEADS):
        ak_ref[0, h] = r[:, h * A_HEAD_DIM:(h + 1) * A_HEAD_DIM]
    pv = p[:, LANES:]
    for h in range(A_KV_HEADS):
        vh = pv if h == 0 else pltpu.roll(pv, A_HEAD_DIM, 1)
        ve = jnp.where(lane < A_HEAD_DIM, vh, 1.0)
        for c in range(TM // A_BLOCK):
            av_ref[0, h, c] = ve[c * A_BLOCK:(c + 1) * A_BLOCK].T.astype(BF16)

    cr = cr_ref[...]
    sr = sr_ref[...]
    for is_k in (False, True):
        p = _dot(hb, wr_ref[:, off:off + R_QK_WIDTH])
        off += R_QK_WIDTH
        for h in range(R_HEADS):
            xs = p[:, h * R_QK_DIM:(h + 1) * R_QK_DIM]
            rot = xs * cr + pltpu.roll(xs, R_QK_DIM // 2, 1) * sr
            if is_k:
                rot = rot * R_QK_DIM ** -0.5
                for c in range(TM // CHUNK):
                    rk_ref[0, h, c] = rot[c * CHUNK:(c + 1) * CHUNK].T.astype(BF16)
            else:
                rq_ref[0, h] = rot.astype(BF16)
    p = _dot(hb, wr_ref[:, off:off + R_V_WIDTH])
    for h in range(R_HEADS):
        rv_ref[0, h] = p[:, h * R_V_DIM:(h + 1) * R_V_DIM].astype(BF16)


def _stream_specs(ctx, xs):
    hb = TM // HALO
    D = xs.shape[-1]
    off = 0 if ctx is None else 1
    last = xs.shape[1] // HALO - 1
    return [
        pl.BlockSpec((BB, TM if off else HALO, D), lambda t, b: (b, 0, 0)),
        pl.BlockSpec((BB, TM, D), lambda t, b: (b, jnp.maximum(t - off, 0), 0)),
        pl.BlockSpec((BB, HALO, D), lambda t, b: (b, jnp.maximum((t - off) * hb - 1, 0), 0)),
        pl.BlockSpec((BB, HALO, D), lambda t, b: (b, jnp.clip((t - off + 1) * hb, 0, last), 0)),
    ]


def _pre_call(ctx, xs, modsel, nw, wqk, cw, wr, wgt, gbt, tabs):
    B, _, D = xs.shape
    T = xs.shape[1] + (0 if ctx is None else ctx.shape[1])
    nt = T // TM
    nch = T // CHUNK

    def tile(t, b):
        return (b, t, 0)

    def head_tile(t, b):
        return (b, 0, t, 0)

    in_specs = _stream_specs(ctx, xs) + [
        pl.BlockSpec((BB, 1, 6, D), lambda t, b: (b, jnp.minimum(t, 1), 0, 0)),
        _const_spec(nw.shape), _const_spec(wqk.shape), _const_spec(cw.shape), _const_spec(wr.shape),
        _const_spec(wgt.shape), _const_spec(gbt.shape),
    ] + [pl.BlockSpec((TM, LANES), lambda t, b: (t, 0)) for _ in range(4)]
    out_shape = [
        jax.ShapeDtypeStruct((B, T, D), BF16),
        jax.ShapeDtypeStruct((B, M_HEADS, T, M_HEAD_DIM), BF16),
        jax.ShapeDtypeStruct((B, M_HEADS, T, M_HEAD_DIM), BF16),
        jax.ShapeDtypeStruct((B, M_HEADS, nch, M_HEAD_DIM, CHUNK), BF16),
        jax.ShapeDtypeStruct((B, 2, nch, 16, CHUNK), F32),
        jax.ShapeDtypeStruct((B, A_HEADS, T, A_HEAD_DIM), BF16),
        jax.ShapeDtypeStruct((B, A_KV_HEADS, T, A_HEAD_DIM), BF16),
        jax.ShapeDtypeStruct((B, A_KV_HEADS, T // A_BLOCK, LANES, A_BLOCK), BF16),
        jax.ShapeDtypeStruct((B, R_HEADS, T, R_QK_DIM), BF16),
        jax.ShapeDtypeStruct((B, R_HEADS, nch, R_QK_DIM, CHUNK), BF16),
        jax.ShapeDtypeStruct((B, R_HEADS, T, R_V_DIM), BF16),
    ]
    out_specs = [
        pl.BlockSpec((BB, TM, D), tile),
        pl.BlockSpec((BB, M_HEADS, TM, M_HEAD_DIM), head_tile),
        pl.BlockSpec((BB, M_HEADS, TM, M_HEAD_DIM), head_tile),
        pl.BlockSpec((BB, M_HEADS, TM // CHUNK, M_HEAD_DIM, CHUNK), lambda t, b: (b, 0, t, 0, 0)),
        pl.BlockSpec((BB, 2, TM // CHUNK, 16, CHUNK), lambda t, b: (b, 0, t, 0, 0)),
        pl.BlockSpec((BB, A_HEADS, TM, A_HEAD_DIM), head_tile),
        pl.BlockSpec((BB, A_KV_HEADS, TM, A_HEAD_DIM), head_tile),
        pl.BlockSpec((BB, A_KV_HEADS, TM // A_BLOCK, LANES, A_BLOCK), lambda t, b: (b, 0, t, 0, 0)),
        pl.BlockSpec((BB, R_HEADS, TM, R_QK_DIM), head_tile),
        pl.BlockSpec((BB, R_HEADS, TM // CHUNK, R_QK_DIM, CHUNK), lambda t, b: (b, 0, t, 0, 0)),
        pl.BlockSpec((BB, R_HEADS, TM, R_V_DIM), head_tile),
    ]
    return pl.pallas_call(
        functools.partial(_pre_kernel, split=ctx is not None),
        grid=(nt, B // BB),
        in_specs=in_specs,
        out_specs=out_specs,
        out_shape=out_shape,
        scratch_shapes=[pltpu.VMEM((BB, TM + 2 * HALO, D), BF16)],
        compiler_params=_params(("parallel", "parallel")),
        name="pre_proj",
    )(xs if ctx is None else ctx, xs, xs, xs, modsel, nw, wqk, cw, wr, wgt, gbt, *tabs)


def _bwd_chunk(i, n_ctx, n_all):
    if isinstance(i, int):
        return n_ctx - 1 - i if i < n_ctx else n_all - 1 + n_ctx - i
    return n_all - 1 + n_ctx - i


def _chunk_start(c):
    return c * CHUNK if isinstance(c, int) else pl.multiple_of(c * CHUNK, CHUNK)


def _chunk_loop(body, n_ctx, n_all, unroll):
    for i in range(n_ctx):
        body(i, 0)
    n_lat = n_all - n_ctx
    while n_lat % unroll:
        unroll //= 2
    lax.fori_loop(n_ctx, n_all, body, 0, unroll=unroll)


def _mlstm_kernel(q_ref, k_ref, vt_ref, gr_ref, o_ref, c_ref, n_ref, m_ref,
                  cs_ref, ns_ref, ms_ref, qp_ref, nq_ref, *, n_ctx):
    n_all = gr_ref.shape[2]
    dh = q_ref.shape[3]
    c_ref[...] = jnp.zeros(c_ref.shape, F32)
    n_ref[...] = jnp.zeros(n_ref.shape, F32)
    m_ref[...] = jnp.zeros(m_ref.shape, F32)

    def scan_unit(c, hh, d):
        u = 2 * hh + d
        c0 = _chunk_start(c)
        k = k_ref[0, hh, pl.ds(c0, CHUNK), :]
        vt = vt_ref[0, hh, c]
        rows = gr_ref[0, 0, c, 8 * hh:8 * hh + 8, :]
        tot = rows[4 + d:5 + d]
        aend = rows[2 * d:2 * d + 1] + tot
        amax = rows[6 + d:7 + d]
        cst = c_ref[u]
        nv = n_ref[u]
        m = m_ref[u]
        cs_ref[u, c] = cst.astype(BF16)
        ns_ref[u, c] = jnp.broadcast_to(nv, (16, dh)).astype(BF16)
        ms_ref[u, c] = jnp.broadcast_to(m, (8, CHUNK))
        m_new = jnp.maximum(tot + m, amax)
        w = jnp.exp(aend - m_new)
        dec = jnp.exp(tot + m - m_new)
        vw = (vt.astype(F32) * w).astype(BF16)
        c_ref[u] = dec * cst + _dot(vw, k)
        wk = _dot(jnp.broadcast_to(w, (16, CHUNK)).astype(BF16), k)
        n_ref[u] = dec * nv + wk[0:1]
        m_ref[u] = m_new

    def scan_step(i, carry):
        cb = _bwd_chunk(i, n_ctx, n_all)
        for hh in range(2):
            scan_unit(i, hh, 0)
            scan_unit(cb, hh, 1)
        return carry

    _chunk_loop(scan_step, n_ctx, n_all, 32)

    row = lax.broadcasted_iota(jnp.int32, (CHUNK, CHUNK), 0)
    col = lax.broadcasted_iota(jnp.int32, (CHUNK, CHUNK), 1)
    tri = (row <= col, row >= col)

    def query_products(c, slot):
        c0 = _chunk_start(c)
        for hh in range(2):
            q = q_ref[0, hh, pl.ds(c0, CHUNK), :]
            k = k_ref[0, hh, pl.ds(c0, CHUNK), :]
            qp_ref[slot, hh, 0] = _dot_nt(k, q)
            for d in range(2):
                u = 2 * hh + d
                qp_ref[slot, hh, 1 + d] = _dot_nt(cs_ref[u, c], q)
                nq_ref[slot, u] = _dot_nt(ns_ref[u, c], q)[0:8]

    def outputs(c, slot):
        c0 = _chunk_start(c)
        gcv = jnp.concatenate([gr_ref[0, 0, c], jnp.zeros((CHUNK - 16, CHUNK), F32)], axis=0).T
        for hh in range(2):
            vt = vt_ref[0, hh, c]
            rows = gr_ref[0, 0, c, 8 * hh:8 * hh + 8, :]
            st = qp_ref[slot, hh, 0]
            sds, iws, invs = [], [], []
            for d in range(2):
                u = 2 * hh + d
                a0c = gcv[:, 8 * hh + 2 * d:8 * hh + 2 * d + 1]
                b = rows[2 * d + 1:2 * d + 2]
                m = ms_ref[u, c, 0:1, :]
                dlog = jnp.where(tri[d], a0c + b, NEG_INF)
                mj = jnp.maximum(b + m, jnp.max(dlog, axis=0, keepdims=True))
                sd = st * jnp.exp(dlog - mj)
                iw = jnp.exp(b + m - mj)
                den = iw * nq_ref[slot, u, 0:1, :] + jnp.sum(sd, axis=0, keepdims=True)
                sds.append(sd.astype(BF16))
                iws.append(iw)
                invs.append(1.0 / jnp.maximum(jnp.abs(den), jnp.exp(-mj)))
            num = _dot(vt, jnp.concatenate(sds, axis=1))
            acc = None
            for d in range(2):
                ht = (num[:, d * CHUNK:(d + 1) * CHUNK] + iws[d] * qp_ref[slot, hh, 1 + d]) * invs[d]
                acc = ht if acc is None else acc + ht
            o_ref[0, hh, pl.ds(c0, CHUNK), :] = acc.T

    query_products(0, 0)

    def out_pair(i, carry):
        c = 2 * i
        query_products(c + 1, 1)
        outputs(c, 0)
        query_products(jnp.minimum(c + 2, n_all - 1), 0)
        outputs(c + 1, 1)
        return carry

    n_pairs = n_all // 2
    unroll = 16
    first = n_pairs % unroll
    for i in range(first):
        out_pair(i, 0)
    lax.fori_loop(first, n_pairs, out_pair, 0, unroll=unroll)


def _mlstm_call(mq, mk, mvt, gr, n_ctx):
    B, H, T, dh = mq.shape
    nch = T // CHUNK
    qspec = pl.BlockSpec((1, 2, T, dh), lambda b, p: (b, p, 0, 0))
    return pl.pallas_call(
        functools.partial(_mlstm_kernel, n_ctx=n_ctx),
        grid=(B, H // 2),
        in_specs=[qspec, qspec,
                  pl.BlockSpec((1, 2, nch, dh, CHUNK), lambda b, p: (b, p, 0, 0, 0)),
                  pl.BlockSpec((1, 1, nch, 16, CHUNK), lambda b, p: (b, p, 0, 0, 0))],
        out_specs=pl.BlockSpec((1, 2, T, dh), lambda b, p: (b, p, 0, 0)),
        out_shape=jax.ShapeDtypeStruct((B, H, T, dh), F32),
        scratch_shapes=[pltpu.VMEM((4, dh, dh), F32), pltpu.VMEM((4, 1, dh), F32),
                        pltpu.VMEM((4, 1, CHUNK), F32),
                        pltpu.VMEM((4, nch, dh, dh), BF16), pltpu.VMEM((4, nch, 16, dh), BF16),
                        pltpu.VMEM((4, nch, 8, CHUNK), F32),
                        pltpu.VMEM((2, 2, 3, dh, CHUNK), F32), pltpu.VMEM((2, 4, 8, CHUNK), F32)],
        compiler_params=_params(("parallel", "parallel")),
        name="mlstm",
    )(mq, mk, mvt, gr)


def _ret_kernel(q_ref, kt_ref, v_ref, lg_ref, o_ref, st_ref, sts_ref, dec_ref, ib_ref, wb_ref, cd_ref,
                *, n_ctx):
    sub = RCHUNK // CHUNK
    n_all = kt_ref.shape[2] // sub
    dk = q_ref.shape[3]
    dv = v_ref.shape[3]
    st_ref[...] = jnp.zeros(st_ref.shape, F32)

    @pl.when(pl.program_id(1) == 0)
    def _():
        row = lax.broadcasted_iota(jnp.int32, (RCHUNK, RCHUNK), 0).astype(F32)
        col = lax.broadcasted_iota(jnp.int32, (RCHUNK, RCHUNK), 1).astype(F32)
        kcol = lax.broadcasted_iota(jnp.int32, (dk, RCHUNK), 1).astype(F32)
        dec = None
        for d in range(2):
            lg = _log_sigmoid(lg_ref[0, d])[0:1]
            diff = row - col if d == 0 else col - row
            dd = jnp.where(diff >= 0, jnp.exp(lg * jnp.maximum(diff, 0.0)), 0.0)
            dec = dd if dec is None else dec + dd
            pin = row if d == 0 else RCHUNK - 1.0 - row
            ib_ref[d] = jnp.exp(lg * (pin + 1.0))
            pkey = kcol if d == 0 else RCHUNK - 1.0 - kcol
            wb_ref[d] = jnp.exp(lg * (RCHUNK - 1.0 - pkey))
            cd_ref[d] = jnp.broadcast_to(jnp.exp(lg * RCHUNK), (8, dv))
        dec_ref[...] = dec

    cds = [cd_ref[d, 0:1, :] for d in range(2)]

    def rows(c):
        r0 = c * RCHUNK
        return pl.ds(r0 if isinstance(c, int) else pl.multiple_of(r0, RCHUNK), RCHUNK)

    def k_t(c):
        return jnp.concatenate([kt_ref[0, 0, sub * c + j] for j in range(sub)], axis=1)

    def scan_step(i, carry):
        cb = _bwd_chunk(i, n_ctx, n_all)
        for d, c in ((0, i), (1, cb)):
            st = st_ref[d]
            sts_ref[c, :, d * dv:(d + 1) * dv] = st.astype(BF16)
            kw = (k_t(c).astype(F32) * wb_ref[d]).astype(BF16)
            st_ref[d] = cds[d] * st + _dot(kw, v_ref[0, 0, rows(c), :])
        return carry

    _chunk_loop(scan_step, n_ctx, n_all, 16)

    def out_step(c, carry):
        q = q_ref[0, 0, rows(c), :]
        s = (_dot(q, k_t(c)) * dec_ref[...]).astype(BF16)
        qst = _dot(q, sts_ref[c])
        o_ref[0, 0, rows(c), :] = (_dot(s, v_ref[0, 0, rows(c), :])
                                   + ib_ref[0] * qst[:, 0:dv] + ib_ref[1] * qst[:, dv:])
        return carry

    _chunk_loop(out_step, n_ctx, n_all, 16)


def _ret_call(rq, rkt, rv, lgb, n_ctx):
    B, H, T, dk = rq.shape
    dv = rv.shape[-1]
    nch = T // CHUNK
    assert dv == RCHUNK and dk <= RCHUNK and T % RCHUNK == 0
    return pl.pallas_call(
        functools.partial(_ret_kernel, n_ctx=n_ctx),
        grid=(H, B),
        in_specs=[pl.BlockSpec((1, 1, T, dk), lambda h, b: (b, h, 0, 0)),
                  pl.BlockSpec((1, 1, nch, dk, CHUNK), lambda h, b: (b, h, 0, 0, 0)),
                  pl.BlockSpec((1, 1, T, dv), lambda h, b: (b, h, 0, 0)),
                  pl.BlockSpec((1, 2, 8, dv), lambda h, b: (h, 0, 0, 0))],
        out_specs=pl.BlockSpec((1, 1, T, dv), lambda h, b: (b, h, 0, 0)),
        out_shape=jax.ShapeDtypeStruct((B, H, T, dv), F32),
        scratch_shapes=[pltpu.VMEM((2, dk, dv), F32), pltpu.VMEM((T // RCHUNK, dk, 2 * dv), BF16),
                        pltpu.VMEM((RCHUNK, RCHUNK), F32), pltpu.VMEM((2, RCHUNK, dv), F32),
                        pltpu.VMEM((2, dk, RCHUNK), F32), pltpu.VMEM((2, 8, dv), F32)],
        compiler_params=_params(("parallel", "arbitrary")),
        name="retention",
    )(rq, rkt, rv, lgb)


def _attn_kernel(q_ref, k_ref, vt_ref, sink_ref, o_ref, bias_ref, s_ref, *, n_ctx_tok, need_ctx):
    T = q_ref.shape[2]
    L = n_ctx_tok
    S = T - L
    nb = S // A_BLOCK
    nspan = 3
    span = nspan * A_BLOCK
    ncb = L // A_BLOCK
    cols = A_GROUP * A_BLOCK
    kc = k_ref[0, 0, 0:L, :]
    vtc = jnp.concatenate([vt_ref[0, 0, j] for j in range(ncb)], axis=1)
    sink = sink_ref[0]

    @pl.when((pl.program_id(0) == 0) & (pl.program_id(1) == 0))
    def _():
        kpos = lax.broadcasted_iota(jnp.int32, (span, A_BLOCK), 0)
        qpos = lax.broadcasted_iota(jnp.int32, (span, A_BLOCK), 1)
        for delta in range(nspan):
            bias_ref[delta] = jnp.where(jnp.abs(qpos + delta * A_BLOCK - kpos) <= A_WINDOW, 0.0, NEG_INF)

    def finish(blk, acc, m):
        den = acc[A_HEAD_DIM:A_HEAD_DIM + 1] + jnp.exp2(sink - m)
        ot = (acc[0:A_HEAD_DIM] * (1.0 / den)).astype(BF16)
        for g in range(A_GROUP):
            o_ref[0, blk, g * A_HEAD_DIM:(g + 1) * A_HEAD_DIM, :] = ot[:, g * A_BLOCK:(g + 1) * A_BLOCK]

    for cbk in range(ncb):
        if need_ctx:
            qs = q_ref[0, :, cbk * A_BLOCK:(cbk + 1) * A_BLOCK, :].reshape(cols, A_HEAD_DIM)
            s = _dot_nt(kc, qs)
            m = jnp.maximum(jnp.max(s, axis=0, keepdims=True), sink)
            finish(cbk, _dot(vtc, jnp.exp2(s - m).astype(BF16)), m)
        else:
            o_ref[0, cbk] = jnp.zeros(o_ref.shape[2:], BF16)

    def first_key_block(n):
        return jnp.clip(n - 1, 0, nb - nspan)

    def scores(n, slot):
        q0 = pl.multiple_of(L + n * A_BLOCK, A_BLOCK)
        kb = first_key_block(n)
        k0 = pl.multiple_of(L + kb * A_BLOCK, A_BLOCK)
        qs = q_ref[0, :, pl.ds(q0, A_BLOCK), :].reshape(cols, A_HEAD_DIM)
        bias = bias_ref[n - kb]
        s_ref[slot, 0:span, :] = (_dot_nt(k_ref[0, 0, pl.ds(k0, span), :], qs)
                                  + jnp.concatenate([bias] * A_GROUP, axis=1))
        s_ref[slot, span:, :] = _dot_nt(kc, qs)

    def softmax_pv(n, slot):
        kb = first_key_block(n)
        s = s_ref[slot]
        m = jnp.maximum(jnp.max(s, axis=0, keepdims=True), sink)
        p = jnp.exp2(s - m).astype(BF16)
        vt = jnp.concatenate([vt_ref[0, 0, ncb + kb + j] for j in range(nspan)] + [vtc], axis=1)
        finish(ncb + n, _dot(vt, p), m)

    scores(0, 0)

    def block_pair(i, carry):
        n = 2 * i
        scores(n + 1, 1)
        softmax_pv(n, 0)
        scores(jnp.minimum(n + 2, nb - 1), 0)
        softmax_pv(n + 1, 1)
        return carry

    n_pairs = nb // 2
    unroll = 8
    first = n_pairs % unroll
    for i in range(first):
        block_pair(i, 0)
    lax.fori_loop(first, n_pairs, block_pair, 0, unroll=unroll)


def _attn_call(aq, ak, avt, sink_rows, n_ctx_tok, need_ctx):
    B, _, T, hd = aq.shape
    nblk = T // A_BLOCK
    cols = A_GROUP * A_BLOCK
    return pl.pallas_call(
        functools.partial(_attn_kernel, n_ctx_tok=n_ctx_tok, need_ctx=need_ctx),
        grid=(B, A_KV_HEADS),
        in_specs=[pl.BlockSpec((1, A_GROUP, T, hd), lambda b, h: (b, h, 0, 0)),
                  pl.BlockSpec((1, 1, T, hd), lambda b, h: (b, h, 0, 0)),
                  pl.BlockSpec((1, 1, nblk, LANES, A_BLOCK), lambda b, h: (b, h, 0, 0, 0)),
                  pl.BlockSpec((1, 1, cols), lambda b, h: (h, 0, 0))],
        out_specs=pl.BlockSpec((1, nblk, A_GROUP * hd, A_BLOCK), lambda b, h: (b, 0, h, 0)),
        out_shape=jax.ShapeDtypeStruct((B, nblk, A_WIDTH, A_BLOCK), BF16),
        scratch_shapes=[pltpu.VMEM((3, 3 * A_BLOCK, A_BLOCK), F32),
                        pltpu.VMEM((2, 3 * A_BLOCK + n_ctx_tok, cols), F32)],
        compiler_params=_params(("arbitrary", "arbitrary")),
        name="window_attn",
    )(aq, ak, avt, sink_rows)


def _head_norm(y):
    mu = jnp.mean(y, axis=-1, keepdims=True)
    yc = y - mu
    var = jnp.mean(yc * yc, axis=-1, keepdims=True)
    return yc * lax.rsqrt(var + NORM_EPS)


def _post_kernel(c_ref, x_ref, hb_ref, hm_ref, ya_ref, hr_ref, mod_ref, wg_ref, mnw_ref, rnw_ref,
                 wbm_ref, wba_ref, wbr_ref, wo_ref, o_ref, *, split):
    for i in range(x_ref.shape[0]):
        hb = hb_ref[i]
        off = 0
        hm = jnp.concatenate([_head_norm(hm_ref[i, h]) for h in range(M_HEADS)], axis=1) * mnw_ref[...]
        ym = (_sigmoid(_dot(hb, wg_ref[:, off:off + M_WIDTH])) * hm).astype(BF16)
        off += M_WIDTH
        hr = jnp.concatenate([_head_norm(hr_ref[i, h]) for h in range(R_HEADS)], axis=1) * rnw_ref[...]
        rg = _dot(hb, wg_ref[:, off:off + R_V_WIDTH])
        yr = (rg * _sigmoid(rg) * hr).astype(BF16)
        off += R_V_WIDTH
        z = _sigmoid(_dot(hb, wg_ref[:, off:off + D_MODEL])) * _dot(ym, wbm_ref[...])
        off += D_MODEL
        pa = jnp.concatenate([_dot_tn(ya_ref[i, c], wba_ref[...]) for c in range(TM // A_BLOCK)], axis=0)
        z = z + _sigmoid(_dot(hb, wg_ref[:, off:off + D_MODEL])) * pa
        off += D_MODEL
        z = z + _sigmoid(_dot(hb, wg_ref[:, off:off + D_MODEL])) * _dot(yr, wbr_ref[...])
        y = _dot(z.astype(BF16), wo_ref[...])
        one = pl.ds(i, 1)
        xres = _tile_rows(c_ref.at[one] if split else None, x_ref.at[one])
        o_ref[i] = xres + mod_ref[i, 0, 2:3, :] * y


def _post_call(ctx, xs, hb, hm, ya, hr, modsel, wg, mnw, rnw, wbm, wba, wbr, wo, t_off):
    B, T, D = hb.shape
    nt = T // TM - t_off
    split = ctx is not None
    assert not (split and t_off)

    def tile(t, b):
        return (b, t + t_off, 0)

    def head_tile(t, b):
        return (b, 0, t + t_off, 0)

    return pl.pallas_call(
        functools.partial(_post_kernel, split=split),
        grid=(nt, B // BB),
        in_specs=[pl.BlockSpec((BB, TM if split else HALO, D), lambda t, b: (b, 0, 0)),
                  pl.BlockSpec((BB, TM, D), (lambda t, b: (b, jnp.maximum(t - 1, 0), 0)) if split else tile),
                  pl.BlockSpec((BB, TM, D), tile),
                  pl.BlockSpec((BB, M_HEADS, TM, M_HEAD_DIM), head_tile),
                  pl.BlockSpec((BB, TM // A_BLOCK, A_WIDTH, A_BLOCK), lambda t, b: (b, t + t_off, 0, 0)),
                  pl.BlockSpec((BB, R_HEADS, TM, R_V_DIM), head_tile),
                  pl.BlockSpec((BB, 1, 6, D), lambda t, b: (b, jnp.minimum(t + t_off, 1), 0, 0)),
                  _const_spec(wg.shape), _const_spec(mnw.shape), _const_spec(rnw.shape),
                  _const_spec(wbm.shape), _const_spec(wba.shape), _const_spec(wbr.shape),
                  _const_spec(wo.shape)],
        out_specs=pl.BlockSpec((BB, TM, D), lambda t, b: (b, t, 0)),
        out_shape=jax.ShapeDtypeStruct((B, nt * TM, D), F32),
        compiler_params=_params(("parallel", "parallel")),
        name="post_merge",
    )(ctx if split else xs, xs, hb, hm, ya, hr, modsel, wg, mnw, rnw, wbm, wba, wbr, wo)


def _ffn_kernel(x_ref, xp_ref, xn_ref, mod_ref, nw_ref, wa_ref, wb_ref, cw_ref, cb_ref, wd_ref, fw_ref,
                o_ref, hext_ref, u_ref, *, has_ctx, final):
    t = pl.program_id(0)
    nt = pl.num_programs(0)
    first_lat = 1 if has_ctx else 0
    nw = nw_ref[...]
    ext = TM + 2 * HALO
    for i in range(x_ref.shape[0]):
        sh = mod_ref[i, 0, 3:4, :]
        sc = mod_ref[i, 0, 4:5, :]
        g2 = mod_ref[i, 0, 5:6, :]
        x = x_ref[i]
        hp = _norm_mod(xp_ref[i], nw, sc, sh)
        hn = _norm_mod(xn_ref[i], nw, sc, sh)
        hp = jnp.where(t <= first_lat, 0.0, hp)
        hn = jnp.where((t == first_lat - 1) | (t == nt - 1), 0.0, hn)
        hext_ref[i, 0:TM, :] = _norm_mod(x, nw, sc, sh).astype(BF16)
        hext_ref[i, TM:, :] = jnp.concatenate([hn, hp], axis=0).astype(BF16)
        for j in range(FFN_DIM // FFN_NC):
            cs = slice(j * FFN_NC, (j + 1) * FFN_NC)
            a = _dot(hext_ref[i], wa_ref[:, cs])
            b = _dot(hext_ref[i, 0:TM, :], wb_ref[:, cs])
            cw = cw_ref[:, cs]
            prev = pltpu.roll(a, 1, 0)[0:TM]
            nxt = pltpu.roll(a, ext - 1, 0)[0:TM]
            conv = prev * cw[0:1] + a[0:TM] * cw[1:2] + nxt * cw[2:3] + cb_ref[:, cs]
            u_ref[i, :, cs] = (conv * _sigmoid(conv) * b).astype(BF16)
        y = x + g2 * _dot(u_ref[i], wd_ref[...])
        if final:
            ms = jnp.mean(y * y, axis=-1, keepdims=True)
            y = y * lax.rsqrt(ms + NORM_EPS) * fw_ref[...]
        o_ref[i] = y


def _ffn_call(xs, modsel, nw, wa, wb, cw, cb, wd, fw, has_ctx, final):
    B, T, D = xs.shape
    nt = T // TM
    hb16 = TM // HALO
    first_lat = 1 if has_ctx else 0
    tile = pl.BlockSpec((BB, TM, D), lambda t, b: (b, t, 0))
    return pl.pallas_call(
        functools.partial(_ffn_kernel, has_ctx=has_ctx, final=final),
        grid=(nt, B // BB),
        in_specs=[tile,
                  pl.BlockSpec((BB, HALO, D), lambda t, b: (b, jnp.maximum(t * hb16 - 1, 0), 0)),
                  pl.BlockSpec((BB, HALO, D), lambda t, b: (b, jnp.minimum((t + 1) * hb16, T // HALO - 1), 0)),
                  pl.BlockSpec((BB, 1, 6, D), lambda t, b: (b, jnp.minimum(t + 1 - first_lat, 1), 0, 0)),
                  _const_spec(nw.shape), _const_spec(wa.shape), _const_spec(wb.shape),
                  _const_spec(cw.shape), _const_spec(cb.shape), _const_spec(wd.shape),
                  _const_spec(fw.shape)],
        out_specs=tile,
        out_shape=jax.ShapeDtypeStruct((B, T, D), F32),
        scratch_shapes=[pltpu.VMEM((BB, TM + 2 * HALO, D), BF16), pltpu.VMEM((BB, TM, FFN_DIM), BF16)],
        compiler_params=_params(("parallel", "parallel")),
        name="conv_ffn",
    )(xs, xs, xs, modsel, nw, wa, wb, cw, cb, wd, fw)


def _split_cols(w):
    out = []
    acc = 0
    for s in IN_SPLITS:
        out.append(w[:, acc:acc + s])
        acc += s
    return out


def _rope_tables(L, S):
    T = L + S
    f32 = np.float32
    j = np.arange(LANES)
    jj = j % 32
    inv_a = np.power(f32(ROPE_BASE), -(jj % 16).astype(f32) / f32(16.0)).astype(f32)
    t = np.arange(S)
    pos = np.where(((j % A_HEAD_DIM) // 32 == 0)[None, :], (t // GRID_W)[:, None], (t % GRID_W)[:, None])
    ang = pos.astype(f32) * inv_a[None, :]
    sign = np.where(jj < 16, f32(-1.0), f32(1.0))[None, :]
    ca = np.concatenate([np.ones((L, LANES), f32), np.cos(ang)], axis=0)
    sa = np.concatenate([np.zeros((L, LANES), f32), np.sin(ang) * sign], axis=0)
    half = R_QK_DIM // 2
    inv_r = np.power(f32(ROPE_BASE), -(j % half).astype(f32) / f32(half)).astype(f32)
    angr = np.arange(T).astype(f32)[:, None] * inv_r[None, :]
    signr = np.where(j < half, f32(-1.0), f32(1.0))[None, :]
    return tuple(jnp.asarray(a, F32) for a in (ca, sa, np.cos(angr), np.sin(angr) * signr))


def kernel(x, c, ctx, c_ctx, mod_w, mod_b, norm1_w, norm2_w, w_in, m_gate_bias, m_conv_w, m_norm_w,
           a_sink, ret_logit, ret_norm_w, w_br_m, w_br_a, w_br_r, w_out, ffn_up, ffn_conv_w, ffn_conv_b,
           ffn_down, final_norm_w):
    B, S, D = x.shape
    L = ctx.shape[1]
    depth = mod_w.shape[0]
    assert D == D_MODEL and L == TM and S % TM == 0 and S >= 3 * A_BLOCK and B % BB == 0
    T = L + S
    n_ctx = L // CHUNK

    rows = -(-(B + 1) // 8) * 8
    cpad = jnp.zeros((rows, D), F32).at[:B].set(c).at[B].set(c_ctx)
    mods = _modulation(cpad, mod_w, mod_b)
    tabs = _rope_tables(L, S)
    stream = (ctx, x) if depth > 1 else (None, jnp.concatenate([ctx, x], axis=1))

    gperm = jnp.arange(4 * M_HEADS).reshape(4, M_HEADS).T.reshape(-1)

    for l in range(depth):
        last = l == depth - 1
        lat = mods[l, :B].reshape(B, 6, D)
        cm = jnp.broadcast_to(mods[l, B].reshape(1, 6, D), (B, 6, D))
        modsel = jnp.stack([cm, lat], axis=1)

        (w_mq, w_mk, w_mv, w_mo, w_mg, w_aq, w_ak, w_av,
         w_rq, w_rk, w_rv, w_rg, w_gm, w_ga, w_gr) = _split_cols(w_in[l])
        wqk = jnp.concatenate([w_mq, w_mk], axis=1).astype(BF16)
        wr = jnp.concatenate([w_mv, w_aq, w_ak, w_av, w_rq, w_rk, w_rv], axis=1).astype(BF16)
        wgt = w_mg[:, gperm].T.astype(BF16)
        gbt = jnp.broadcast_to(m_gate_bias[l][gperm].reshape(-1, 1), (4 * M_HEADS, LANES))

        (hb, mq, mk, mv, gr, aq, ak, av, rq, rk, rv) = _pre_call(
            *stream, modsel, norm1_w[l].reshape(1, D), wqk, m_conv_w[l], wr, wgt, gbt, tabs)

        hm = _mlstm_call(mq, mk, mv, gr, n_ctx)
        sink_rows = jnp.repeat(a_sink[l].reshape(A_KV_HEADS, A_GROUP), A_BLOCK, axis=1)[:, None, :]
        ya = _attn_call(aq, ak, av, sink_rows.astype(F32) * LOG2E, L, not last)
        lgb = jnp.broadcast_to(ret_logit[l].T[:, :, None, None], (R_HEADS, 2, 8, R_V_DIM)).astype(F32)
        hr = _ret_call(rq, rk, rv, lgb, L // RCHUNK)

        t_off = 1 if last else 0
        wgates = jnp.concatenate([w_mo, w_rg, w_gm, w_ga, w_gr], axis=1).astype(BF16)
        x1 = _post_call(*stream, hb, hm, ya, hr, modsel, wgates, m_norm_w[l].reshape(1, -1),
                        ret_norm_w[l].reshape(1, -1), w_br_m[l].astype(BF16), w_br_a[l].astype(BF16),
                        w_br_r[l].astype(BF16), w_out[l].astype(BF16), t_off)
        xc = _ffn_call(x1, modsel, norm2_w[l].reshape(1, D), ffn_up[l][:, :FFN_DIM].astype(BF16),
                       ffn_up[l][:, FFN_DIM:].astype(BF16), ffn_conv_w[l], ffn_conv_b[l].reshape(1, -1),
                       ffn_down[l].astype(BF16), final_norm_w.reshape(1, D), not last, last)
        stream = (None, xc)
    return xc
```

```python
import functools

import jax
import jax.numpy as jnp
import numpy as np
from jax import lax
from jax.experimental import pallas as pl
from jax.experimental.pallas import tpu as pltpu

F32 = jnp.float32
BF16 = jnp.bfloat16

D_MODEL = 1024
GRID_W = 64
NORM_EPS = 1e-6
ROPE_BASE = 10000.0
NEG_INF = -1e30
LOG2E = 1.4426950408889634

M_HEADS = 4
M_HEAD_DIM = 128
M_WIDTH = M_HEADS * M_HEAD_DIM
A_HEADS = 8
A_KV_HEADS = 2
A_GROUP = A_HEADS // A_KV_HEADS
A_HEAD_DIM = 64
A_WIDTH = A_HEADS * A_HEAD_DIM
A_KV_WIDTH = A_KV_HEADS * A_HEAD_DIM
A_WINDOW = 128
A_BLOCK = 128
R_HEADS = 4
R_QK_DIM = 128
R_V_DIM = 256
R_QK_WIDTH = R_HEADS * R_QK_DIM
R_V_WIDTH = R_HEADS * R_V_DIM
FFN_DIM = 2816

IN_SPLITS = (M_WIDTH, M_WIDTH, M_WIDTH, M_WIDTH, 4 * M_HEADS,
             A_WIDTH, A_KV_WIDTH, A_KV_WIDTH,
             R_QK_WIDTH, R_QK_WIDTH, R_V_WIDTH, R_V_WIDTH,
             D_MODEL, D_MODEL, D_MODEL)

TM = 256
BB = 2
HALO = 8
CHUNK = 128
RCHUNK = 256
FFN_NC = 256
LANES = 128
VMEM_LIMIT = 56 * 1024 * 1024


def _sigmoid(x):
    return 1.0 / (1.0 + jnp.exp(-x))


def _log_sigmoid(x):
    return jnp.minimum(x, 0.0) - jnp.log1p(jnp.exp(-jnp.abs(x)))


def _dot(a, b):
    return jnp.dot(a, b, preferred_element_type=F32)


def _dot_nt(a, b):
    return lax.dot_general(a, b, (((1,), (1,)), ((), ())), preferred_element_type=F32)


def _dot_tn(a, b):
    return lax.dot_general(a, b, (((0,), (0,)), ((), ())), preferred_element_type=F32)


def _const_spec(shape):
    nd = len(shape)
    return pl.BlockSpec(shape, lambda *_: (0,) * nd, pipeline_mode=pl.Buffered(1))


def _params(sem):
    return pltpu.CompilerParams(dimension_semantics=sem, vmem_limit_bytes=VMEM_LIMIT)


def _mod_kernel(c_ref, w_ref, b_ref, o_ref):
    c = c_ref[...]
    s = (c * _sigmoid(c)).astype(BF16)
    o_ref[0] = _dot(s, w_ref[0].astype(BF16)) + b_ref[0]


def _modulation(cpad, mod_w, mod_b):
    depth, d, n = mod_w.shape
    rows = cpad.shape[0]
    bn = 1024
    return pl.pallas_call(
        _mod_kernel,
        grid=(depth, n // bn),
        in_specs=[pl.BlockSpec((rows, d), lambda l, j: (0, 0)),
                  pl.BlockSpec((1, d, bn), lambda l, j: (l, 0, j)),
                  pl.BlockSpec((1, 1, bn), lambda l, j: (l, 0, j))],
        out_specs=pl.BlockSpec((1, rows, bn), lambda l, j: (l, 0, j)),
        out_shape=jax.ShapeDtypeStruct((depth, rows, n), F32),
        compiler_params=_params(("parallel", "parallel")),
        name="modulation",
    )(cpad, mod_w, mod_b.reshape(depth, 1, n))


def _norm_mod(xv, nw, sc, sh):
    ms = jnp.mean(xv * xv, axis=-1, keepdims=True)
    return (xv * lax.rsqrt(ms + NORM_EPS) * nw) * (1.0 + sc) + sh


def _seg_cumsum(x, axis, reverse):
    n = x.shape[axis]
    idx = lax.broadcasted_iota(jnp.int32, x.shape, axis)
    s = 1
    while s < n:
        if reverse:
            shifted = pltpu.roll(x, n - s, axis)
            x = x + jnp.where(idx < n - s, shifted, 0.0)
        else:
            shifted = pltpu.roll(x, s, axis)
            x = x + jnp.where(idx >= s, shifted, 0.0)
        s *= 2
    return x


def _gate_transform(raw, gate_axis):
    tok_axis = 1 - gate_axis
    k = lax.broadcasted_iota(jnp.int32, raw.shape, gate_axis) % 4
    lf = _log_sigmoid(raw)
    cum_f = _seg_cumsum(jnp.where(k == 1, lf, 0.0), tok_axis, False)
    cum_b = _seg_cumsum(jnp.where(k == 3, lf, 0.0), tok_axis, True)
    bsel = jnp.where(k == 1, cum_f, jnp.where(k == 3, cum_b, 0.0))
    n = raw.shape[gate_axis]
    bnext = pltpu.roll(bsel, n - 1, gate_axis)
    nt = raw.shape[tok_axis]
    tot_f = lax.slice_in_dim(cum_f, nt - 1, nt, axis=tok_axis)
    tot_b = lax.slice_in_dim(cum_b, 0, 1, axis=tok_axis)
    total = jnp.broadcast_to(tot_f + tot_b, raw.shape)
    return jnp.where(k % 2 == 1, bsel, raw - bnext), total


def _pre_kernel(c_ref, x_ref, xp_ref, xn_ref, mod_ref, nw_ref, wqk_ref, cw_ref, wr_ref, wgt_ref,
                gbt_ref, ca_ref, sa_ref, cr_ref, sr_ref, *out_and_scratch, split):
    for i in range(x_ref.shape[0]):
        one = pl.ds(i, 1)
        _pre_tile(c_ref.at[one] if split else None,
                  x_ref.at[one], xp_ref.at[one], xn_ref.at[one], mod_ref.at[one], nw_ref, wqk_ref, cw_ref,
                  wr_ref, wgt_ref, gbt_ref, ca_ref, sa_ref, cr_ref, sr_ref,
                  *[r.at[one] for r in out_and_scratch[:-1]], out_and_scratch[-1].at[i])


def _tile_rows(c_ref, x_ref):
    if c_ref is None:
        return x_ref[0]
    return jnp.where(pl.program_id(0) == 0, c_ref[0], x_ref[0])


def _pre_tile(c_ref, x_ref, xp_ref, xn_ref, mod_ref, nw_ref, wqk_ref, cw_ref, wr_ref, wgt_ref,
              gbt_ref, ca_ref, sa_ref, cr_ref, sr_ref,
              hb_ref, mq_ref, mk_ref, mv_ref, gr_ref, aq_ref, ak_ref, av_ref,
              rq_ref, rk_ref, rv_ref, hext_ref):
    t = pl.program_id(0)
    nt = pl.num_programs(0)
    sh = mod_ref[0, 0, 0:1, :]
    sc = mod_ref[0, 0, 1:2, :]
    nw = nw_ref[...]
    hb = _norm_mod(_tile_rows(c_ref, x_ref), nw, sc, sh).astype(BF16)
    hb_ref[0] = hb

    grow = _dot_nt(wgt_ref[...], hb) + gbt_ref[:, 0:1]
    for c in range(TM // CHUNK):
        gr, tot = _gate_transform(grow[:, c * CHUNK:(c + 1) * CHUNK], 0)
        aend = gr + pltpu.roll(tot, 4 * M_HEADS - 1, 0)
        amax = jnp.broadcast_to(jnp.max(aend, axis=1, keepdims=True), aend.shape)
        for h in range(M_HEADS):
            hp_, o0 = h // 2, 8 * (h % 2)
            gr_ref[0, hp_, c, o0:o0 + 4, :] = gr[4 * h:4 * h + 4]
            for d in range(2):
                gr_ref[0, hp_, c, o0 + 4 + d:o0 + 5 + d, :] = tot[4 * h + 2 * d + 1:4 * h + 2 * d + 2]
                gr_ref[0, hp_, c, o0 + 6 + d:o0 + 7 + d, :] = amax[4 * h + 2 * d:4 * h + 2 * d + 1]

    hp = _norm_mod(xp_ref[0], nw, sc, sh)
    hn = _norm_mod(xn_ref[0], nw, sc, sh)
    hp = jnp.where(t <= 1, 0.0, hp)
    hn = jnp.where((t == 0) | (t == nt - 1), 0.0, hn)
    hext_ref[0:TM, :] = hb
    hext_ref[TM:, :] = jnp.concatenate([hn, hp], axis=0).astype(BF16)
    hext = hext_ref[...]
    ext = TM + 2 * HALO

    for j, (dst, scale) in enumerate(((mq_ref, 1.0), (mk_ref, M_HEAD_DIM ** -0.5))):
        p = _dot(hext, wqk_ref[:, j * M_WIDTH:(j + 1) * M_WIDTH])
        cw = cw_ref[:, j * M_WIDTH:(j + 1) * M_WIDTH]
        prev = pltpu.roll(p, 1, 0)[0:TM]
        nxt = pltpu.roll(p, ext - 1, 0)[0:TM]
        conv = prev * cw[0:1] + p[0:TM] * cw[1:2] + nxt * cw[2:3]
        act = conv * _sigmoid(conv) * scale
        for h in range(M_HEADS):
            dst[0, h] = act[:, h * M_HEAD_DIM:(h + 1) * M_HEAD_DIM].astype(BF16)

    off = 0
    p = _dot(hb, wr_ref[:, off:off + M_WIDTH])
    off += M_WIDTH
    for h in range(M_HEADS):
        for c in range(TM // CHUNK):
            blk = p[c * CHUNK:(c + 1) * CHUNK, h * M_HEAD_DIM:(h + 1) * M_HEAD_DIM]
            mv_ref[0, h, c] = blk.T.astype(BF16)

    lane = lax.broadcasted_iota(jnp.int32, (TM, LANES), 1)
    first16 = (lane % 32) < 16
    ca = ca_ref[...]
    sa = sa_ref[...]

    def rope_a(xs):
        partner = jnp.where(first16, pltpu.roll(xs, LANES - 16, 1), pltpu.roll(xs, 16, 1))
        return xs * ca + partner * sa

    p = _dot(hb, wr_ref[:, off:off + A_WIDTH])
    off += A_WIDTH
    for s in range(A_WIDTH // LANES):
        r = (rope_a(p[:, s * LANES:(s + 1) * LANES]) * (A_HEAD_DIM ** -0.5 * LOG2E)).astype(BF16)
        aq_ref[0, 2 * s] = r[:, 0:A_HEAD_DIM]
        aq_ref[0, 2 * s + 1] = r[:, A_HEAD_DIM:]
    p = _dot(hb, wr_ref[:, off:off + 2 * A_KV_WIDTH])
    off += 2 * A_KV_WIDTH
    r = rope_a(p[:, 0:LANES]).astype(BF16)
    for h in range(A_KV_HEADS):
        ak_ref[0, h] = r[:, h * A_HEAD_DIM:(h + 1) * A_HEAD_DIM]
    pv = p[:, LANES:]
    for h in range(A_KV_HEADS):
        vh = pv if h == 0 else pltpu.roll(pv, A_HEAD_DIM, 1)
        ve = jnp.where(lane < A_HEAD_DIM, vh, 1.0)
        for c in range(TM // A_BLOCK):
            av_ref[0, h, c] = ve[c * A_BLOCK:(c + 1) * A_BLOCK].T.astype(BF16)

    cr = cr_ref[...]
    sr = sr_ref[...]
    for is_k in (False, True):
        p = _dot(hb, wr_ref[:, off:off + R_QK_WIDTH])
        off += R_QK_WIDTH
        for h in range(R_HEADS):
            xs = p[:, h * R_QK_DIM:(h + 1) * R_QK_DIM]
            rot = xs * cr + pltpu.roll(xs, R_QK_DIM // 2, 1) * sr
            if is_k:
                rot = rot * R_QK_DIM ** -0.5
                for c in range(TM // CHUNK):
                    rk_ref[0, h, c] = rot[c * CHUNK:(c + 1) * CHUNK].T.astype(BF16)
            else:
                rq_ref[0, h] = rot.astype(BF16)
    p = _dot(hb, wr_ref[:, off:off + R_V_WIDTH])
    for h in range(R_HEADS):
        rv_ref[0, h] = p[:, h * R_V_DIM:(h + 1) * R_V_DIM].astype(BF16)


def _stream_specs(ctx, xs):
    hb = TM // HALO
    D = xs.shape[-1]
    off = 0 if ctx is None else 1
    last = xs.shape[1] // HALO - 1
    return [
        pl.BlockSpec((BB, TM if off else HALO, D), lambda t, b: (b, 0, 0)),
        pl.BlockSpec((BB, TM, D), lambda t, b: (b, jnp.maximum(t - off, 0), 0)),
        pl.BlockSpec((BB, HALO, D), lambda t, b: (b, jnp.maximum((t - off) * hb - 1, 0), 0)),
        pl.BlockSpec((BB, HALO, D), lambda t, b: (b, jnp.clip((t - off + 1) * hb, 0, last), 0)),
    ]


def _pre_call(ctx, xs, modsel, nw, wqk, cw, wr, wgt, gbt, tabs):
    B, _, D = xs.shape
    T = xs.shape[1] + (0 if ctx is None else ctx.shape[1])
    nt = T // TM
    nch = T // CHUNK

    def tile(t, b):
        return (b, t, 0)

    def head_tile(t, b):
        return (b, 0, t, 0)

    in_specs = _stream_specs(ctx, xs) + [
        pl.BlockSpec((BB, 1, 6, D), lambda t, b: (b, jnp.minimum(t, 1), 0, 0)),
        _const_spec(nw.shape), _const_spec(wqk.shape), _const_spec(cw.shape), _const_spec(wr.shape),
        _const_spec(wgt.shape), _const_spec(gbt.shape),
    ] + [pl.BlockSpec((TM, LANES), lambda t, b: (t, 0)) for _ in range(4)]
    out_shape = [
        jax.ShapeDtypeStruct((B, T, D), BF16),
        jax.ShapeDtypeStruct((B, M_HEADS, T, M_HEAD_DIM), BF16),
        jax.ShapeDtypeStruct((B, M_HEADS, T, M_HEAD_DIM), BF16),
        jax.ShapeDtypeStruct((B, M_HEADS, nch, M_HEAD_DIM, CHUNK), BF16),
        jax.ShapeDtypeStruct((B, 2, nch, 16, CHUNK), F32),
        jax.ShapeDtypeStruct((B, A_HEADS, T, A_HEAD_DIM), BF16),
        jax.ShapeDtypeStruct((B, A_KV_HEADS, T, A_HEAD_DIM), BF16),
        jax.ShapeDtypeStruct((B, A_KV_HEADS, T // A_BLOCK, LANES, A_BLOCK), BF16),
        jax.ShapeDtypeStruct((B, R_HEADS, T, R_QK_DIM), BF16),
        jax.ShapeDtypeStruct((B, R_HEADS, nch, R_QK_DIM, CHUNK), BF16),
        jax.ShapeDtypeStruct((B, R_HEADS, T, R_V_DIM), BF16),
    ]
    out_specs = [
        pl.BlockSpec((BB, TM, D), tile),
        pl.BlockSpec((BB, M_HEADS, TM, M_HEAD_DIM), head_tile),
        pl.BlockSpec((BB, M_HEADS, TM, M_HEAD_DIM), head_tile),
        pl.BlockSpec((BB, M_HEADS, TM // CHUNK, M_HEAD_DIM, CHUNK), lambda t, b: (b, 0, t, 0, 0)),
        pl.BlockSpec((BB, 2, TM // CHUNK, 16, CHUNK), lambda t, b: (b, 0, t, 0, 0)),
        pl.BlockSpec((BB, A_HEADS, TM, A_HEAD_DIM), head_tile),
        pl.BlockSpec((BB, A_KV_HEADS, TM, A_HEAD_DIM), head_tile),
        pl.BlockSpec((BB, A_KV_HEADS, TM // A_BLOCK, LANES, A_BLOCK), lambda t, b: (b, 0, t, 0, 0)),
        pl.BlockSpec((BB, R_HEADS, TM, R_QK_DIM), head_tile),
        pl.BlockSpec((BB, R_HEADS, TM // CHUNK, R_QK_DIM, CHUNK), lambda t, b: (b, 0, t, 0, 0)),
        pl.BlockSpec((BB, R_HEADS, TM, R_V_DIM), head_tile),
    ]
    return pl.pallas_call(
        functools.partial(_pre_kernel, split=ctx is not None),
        grid=(nt, B // BB),
        in_specs=in_specs,
        out_specs=out_specs,
        out_shape=out_shape,
        scratch_shapes=[pltpu.VMEM((BB, TM + 2 * HALO, D), BF16)],
        compiler_params=_params(("parallel", "parallel")),
        name="pre_proj",
    )(xs if ctx is None else ctx, xs, xs, xs, modsel, nw, wqk, cw, wr, wgt, gbt, *tabs)


def _bwd_chunk(i, n_ctx, n_all):
    if isinstance(i, int):
        return n_ctx - 1 - i if i < n_ctx else n_all - 1 + n_ctx - i
    return n_all - 1 + n_ctx - i


def _chunk_start(c):
    return c * CHUNK if isinstance(c, int) else pl.multiple_of(c * CHUNK, CHUNK)


def _chunk_loop(body, n_ctx, n_all, unroll):
    for i in range(n_ctx):
        body(i, 0)
    n_lat = n_all - n_ctx
    while n_lat % unroll:
        unroll //= 2
    lax.fori_loop(n_ctx, n_all, body, 0, unroll=unroll)


def _mlstm_kernel(q_ref, k_ref, vt_ref, gr_ref, o_ref, c_ref, n_ref, m_ref,
                  cs_ref, ns_ref, ms_ref, qp_ref, nq_ref, *, n_ctx):
    n_all = gr_ref.shape[2]
    dh = q_ref.shape[3]
    c_ref[...] = jnp.zeros(c_ref.shape, F32)
    n_ref[...] = jnp.zeros(n_ref.shape, F32)
    m_ref[...] = jnp.zeros(m_ref.shape, F32)

    def scan_unit(c, hh, d):
        u = 2 * hh + d
        c0 = _chunk_start(c)
        k = k_ref[0, hh, pl.ds(c0, CHUNK), :]
        vt = vt_ref[0, hh, c]
        rows = gr_ref[0, 0, c, 8 * hh:8 * hh + 8, :]
        tot = rows[4 + d:5 + d]
        aend = rows[2 * d:2 * d + 1] + tot
        amax = rows[6 + d:7 + d]
        cst = c_ref[u]
        nv = n_ref[u]
        m = m_ref[u]
        cs_ref[u, c] = cst.astype(BF16)
        ns_ref[u, c] = jnp.broadcast_to(nv, (16, dh)).astype(BF16)
        ms_ref[u, c] = jnp.broadcast_to(m, (8, CHUNK))
        m_new = jnp.maximum(tot + m, amax)
        w = jnp.exp(aend - m_new)
        dec = jnp.exp(tot + m - m_new)
        vw = (vt.astype(F32) * w).astype(BF16)
        c_ref[u] = dec * cst + _dot(vw, k)
        wk = _dot(jnp.broadcast_to(w, (16, CHUNK)).astype(BF16), k)
        n_ref[u] = dec * nv + wk[0:1]
        m_ref[u] = m_new

    def scan_step(i, carry):
        cb = _bwd_chunk(i, n_ctx, n_all)
        for hh in range(2):
            scan_unit(i, hh, 0)
            scan_unit(cb, hh, 1)
        return carry

    _chunk_loop(scan_step, n_ctx, n_all, 32)

    row = lax.broadcasted_iota(jnp.int32, (CHUNK, CHUNK), 0)
    col = lax.broadcasted_iota(jnp.int32, (CHUNK, CHUNK), 1)
    tri = (row <= col, row >= col)

    def query_products(c, slot):
        c0 = _chunk_start(c)
        for hh in range(2):
            q = q_ref[0, hh, pl.ds(c0, CHUNK), :]
            k = k_ref[0, hh, pl.ds(c0, CHUNK), :]
            qp_ref[slot, hh, 0] = _dot_nt(k, q)
            for d in range(2):
                u = 2 * hh + d
                qp_ref[slot, hh, 1 + d] = _dot_nt(cs_ref[u, c], q)
                nq_ref[slot, u] = _dot_nt(ns_ref[u, c], q)[0:8]

    def outputs(c, slot):
        c0 = _chunk_start(c)
        gcv = jnp.concatenate([gr_ref[0, 0, c], jnp.zeros((CHUNK - 16, CHUNK), F32)], axis=0).T
        for hh in range(2):
            vt = vt_ref[0, hh, c]
            rows = gr_ref[0, 0, c, 8 * hh:8 * hh + 8, :]
            st = qp_ref[slot, hh, 0]
            sds, iws, invs = [], [], []
            for d in range(2):
                u = 2 * hh + d
                a0c = gcv[:, 8 * hh + 2 * d:8 * hh + 2 * d + 1]
                b = rows[2 * d + 1:2 * d + 2]
                m = ms_ref[u, c, 0:1, :]
                dlog = jnp.where(tri[d], a0c + b, NEG_INF)
                mj = jnp.maximum(b + m, jnp.max(dlog, axis=0, keepdims=True))
                sd = st * jnp.exp(dlog - mj)
                iw = jnp.exp(b + m - mj)
                den = iw * nq_ref[slot, u, 0:1, :] + jnp.sum(sd, axis=0, keepdims=True)
                sds.append(sd.astype(BF16))
                iws.append(iw)
                invs.append(1.0 / jnp.maximum(jnp.abs(den), jnp.exp(-mj)))
            num = _dot(vt, jnp.concatenate(sds, axis=1))
            acc = None
            for d in range(2):
                ht = (num[:, d * CHUNK:(d + 1) * CHUNK] + iws[d] * qp_ref[slot, hh, 1 + d]) * invs[d]
                acc = ht if acc is None else acc + ht
            o_ref[0, hh, pl.ds(c0, CHUNK), :] = acc.T

    query_products(0, 0)

    def out_pair(i, carry):
        c = 2 * i
        query_products(c + 1, 1)
        outputs(c, 0)
        query_products(jnp.minimum(c + 2, n_all - 1), 0)
        outputs(c + 1, 1)
        return carry

    n_pairs = n_all // 2
    unroll = 16
    first = n_pairs % unroll
    for i in range(first):
        out_pair(i, 0)
    lax.fori_loop(first, n_pairs, out_pair, 0, unroll=unroll)


def _mlstm_call(mq, mk, mvt, gr, n_ctx):
    B, H, T, dh = mq.shape
    nch = T // CHUNK
    qspec = pl.BlockSpec((1, 2, T, dh), lambda b, p: (b, p, 0, 0))
    return pl.pallas_call(
        functools.partial(_mlstm_kernel, n_ctx=n_ctx),
        grid=(B, H // 2),
        in_specs=[qspec, qspec,
                  pl.BlockSpec((1, 2, nch, dh, CHUNK), lambda b, p: (b, p, 0, 0, 0)),
                  pl.BlockSpec((1, 1, nch, 16, CHUNK), lambda b, p: (b, p, 0, 0, 0))],
        out_specs=pl.BlockSpec((1, 2, T, dh), lambda b, p: (b, p, 0, 0)),
        out_shape=jax.ShapeDtypeStruct((B, H, T, dh), F32),
        scratch_shapes=[pltpu.VMEM((4, dh, dh), F32), pltpu.VMEM((4, 1, dh), F32),
                        pltpu.VMEM((4, 1, CHUNK), F32),
                        pltpu.VMEM((4, nch, dh, dh), BF16), pltpu.VMEM((4, nch, 16, dh), BF16),
                        pltpu.VMEM((4, nch, 8, CHUNK), F32),
                        pltpu.VMEM((2, 2, 3, dh, CHUNK), F32), pltpu.VMEM((2, 4, 8, CHUNK), F32)],
        compiler_params=_params(("parallel", "parallel")),
        name="mlstm",
    )(mq, mk, mvt, gr)


def _ret_kernel(q_ref, kt_ref, v_ref, lg_ref, o_ref, st_ref, sts_ref, dec_ref, ib_ref, wb_ref, cd_ref,
                *, n_ctx):
    sub = RCHUNK // CHUNK
    n_all = kt_ref.shape[2] // sub
    dk = q_ref.shape[3]
    dv = v_ref.shape[3]
    st_ref[...] = jnp.zeros(st_ref.shape, F32)

    @pl.when(pl.program_id(1) == 0)
    def _():
        row = lax.broadcasted_iota(jnp.int32, (RCHUNK, RCHUNK), 0).astype(F32)
        col = lax.broadcasted_iota(jnp.int32, (RCHUNK, RCHUNK), 1).astype(F32)
        kcol = lax.broadcasted_iota(jnp.int32, (dk, RCHUNK), 1).astype(F32)
        dec = None
        for d in range(2):
            lg = _log_sigmoid(lg_ref[0, d])[0:1]
            diff = row - col if d == 0 else col - row
            dd = jnp.where(diff >= 0, jnp.exp(lg * jnp.maximum(diff, 0.0)), 0.0)
            dec = dd if dec is None else dec + dd
            pin = row if d == 0 else RCHUNK - 1.0 - row
            ib_ref[d] = jnp.exp(lg * (pin + 1.0))
            pkey = kcol if d == 0 else RCHUNK - 1.0 - kcol
            wb_ref[d] = jnp.exp(lg * (RCHUNK - 1.0 - pkey))
            cd_ref[d] = jnp.broadcast_to(jnp.exp(lg * RCHUNK), (8, dv))
        dec_ref[...] = dec

    cds = [cd_ref[d, 0:1, :] for d in range(2)]

    def rows(c):
        r0 = c * RCHUNK
        return pl.ds(r0 if isinstance(c, int) else pl.multiple_of(r0, RCHUNK), RCHUNK)

    def k_t(c):
        return jnp.concatenate([kt_ref[0, 0, sub * c + j] for j in range(sub)], axis=1)

    def scan_step(i, carry):
        cb = _bwd_chunk(i, n_ctx, n_all)
        for d, c in ((0, i), (1, cb)):
            st = st_ref[d]
            sts_ref[c, :, d * dv:(d + 1) * dv] = st.astype(BF16)
            kw = (k_t(c).astype(F32) * wb_ref[d]).astype(BF16)
            st_ref[d] = cds[d] * st + _dot(kw, v_ref[0, 0, rows(c), :])
        return carry

    _chunk_loop(scan_step, n_ctx, n_all, 16)

    def out_step(c, carry):
        q = q_ref[0, 0, rows(c), :]
        s = (_dot(q, k_t(c)) * dec_ref[...]).astype(BF16)
        qst = _dot(q, sts_ref[c])
        o_ref[0, 0, rows(c), :] = (_dot(s, v_ref[0, 0, rows(c), :])
                                   + ib_ref[0] * qst[:, 0:dv] + ib_ref[1] * qst[:, dv:])
        return carry

    _chunk_loop(out_step, n_ctx, n_all, 16)


def _ret_call(rq, rkt, rv, lgb, n_ctx):
    B, H, T, dk = rq.shape
    dv = rv.shape[-1]
    nch = T // CHUNK
    assert dv == RCHUNK and dk <= RCHUNK and T % RCHUNK == 0
    return pl.pallas_call(
        functools.partial(_ret_kernel, n_ctx=n_ctx),
        grid=(H, B),
        in_specs=[pl.BlockSpec((1, 1, T, dk), lambda h, b: (b, h, 0, 0)),
                  pl.BlockSpec((1, 1, nch, dk, CHUNK), lambda h, b: (b, h, 0, 0, 0)),
                  pl.BlockSpec((1, 1, T, dv), lambda h, b: (b, h, 0, 0)),
                  pl.BlockSpec((1, 2, 8, dv), lambda h, b: (h, 0, 0, 0))],
        out_specs=pl.BlockSpec((1, 1, T, dv), lambda h, b: (b, h, 0, 0)),
        out_shape=jax.ShapeDtypeStruct((B, H, T, dv), F32),
        scratch_shapes=[pltpu.VMEM((2, dk, dv), F32), pltpu.VMEM((T // RCHUNK, dk, 2 * dv), BF16),
                        pltpu.VMEM((RCHUNK, RCHUNK), F32), pltpu.VMEM((2, RCHUNK, dv), F32),
                        pltpu.VMEM((2, dk, RCHUNK), F32), pltpu.VMEM((2, 8, dv), F32)],
        compiler_params=_params(("parallel", "arbitrary")),
        name="retention",
    )(rq, rkt, rv, lgb)


def _attn_kernel(q_ref, k_ref, vt_ref, sink_ref, o_ref, bias_ref, s_ref, *, n_ctx_tok, need_ctx):
    T = q_ref.shape[2]
    L = n_ctx_tok
    S = T - L
    nb = S // A_BLOCK
    nspan = 3
    span = nspan * A_BLOCK
    ncb = L // A_BLOCK
    cols = A_GROUP * A_BLOCK
    kc = k_ref[0, 0, 0:L, :]
    vtc = jnp.concatenate([vt_ref[0, 0, j] for j in range(ncb)], axis=1)
    sink = sink_ref[0]

    @pl.when((pl.program_id(0) == 0) & (pl.program_id(1) == 0))
    def _():
        kpos = lax.broadcasted_iota(jnp.int32, (span, A_BLOCK), 0)
        qpos = lax.broadcasted_iota(jnp.int32, (span, A_BLOCK), 1)
        for delta in range(nspan):
            bias_ref[delta] = jnp.where(jnp.abs(qpos + delta * A_BLOCK - kpos) <= A_WINDOW, 0.0, NEG_INF)

    def finish(blk, acc, m):
        den = acc[A_HEAD_DIM:A_HEAD_DIM + 1] + jnp.exp2(sink - m)
        ot = (acc[0:A_HEAD_DIM] * (1.0 / den)).astype(BF16)
        for g in range(A_GROUP):
            o_ref[0, blk, g * A_HEAD_DIM:(g + 1) * A_HEAD_DIM, :] = ot[:, g * A_BLOCK:(g + 1) * A_BLOCK]

    for cbk in range(ncb):
        if need_ctx:
            qs = q_ref[0, :, cbk * A_BLOCK:(cbk + 1) * A_BLOCK, :].reshape(cols, A_HEAD_DIM)
            s = _dot_nt(kc, qs)
            m = jnp.maximum(jnp.max(s, axis=0, keepdims=True), sink)
            finish(cbk, _dot(vtc, jnp.exp2(s - m).astype(BF16)), m)
        else:
            o_ref[0, cbk] = jnp.zeros(o_ref.shape[2:], BF16)

    def first_key_block(n):
        return jnp.clip(n - 1, 0, nb - nspan)

    def scores(n, slot):
        q0 = pl.multiple_of(L + n * A_BLOCK, A_BLOCK)
        kb = first_key_block(n)
        k0 = pl.multiple_of(L + kb * A_BLOCK, A_BLOCK)
        qs = q_ref[0, :, pl.ds(q0, A_BLOCK), :].reshape(cols, A_HEAD_DIM)
        bias = bias_ref[n - kb]
        s_ref[slot, 0:span, :] = (_dot_nt(k_ref[0, 0, pl.ds(k0, span), :], qs)
                                  + jnp.concatenate([bias] * A_GROUP, axis=1))
        s_ref[slot, span:, :] = _dot_nt(kc, qs)

    def softmax_pv(n, slot):
        kb = first_key_block(n)
        s = s_ref[slot]
        m = jnp.maximum(jnp.max(s, axis=0, keepdims=True), sink)
        p = jnp.exp2(s - m).astype(BF16)
        vt = jnp.concatenate([vt_ref[0, 0, ncb + kb + j] for j in range(nspan)] + [vtc], axis=1)
        finish(ncb + n, _dot(vt, p), m)

    scores(0, 0)

    def block_pair(i, carry):
        n = 2 * i
        scores(n + 1, 1)
        softmax_pv(n, 0)
        scores(jnp.minimum(n + 2, nb - 1), 0)
        softmax_pv(n + 1, 1)
        return carry

    n_pairs = nb // 2
    unroll = 16
    first = n_pairs % unroll
    for i in range(first):
        block_pair(i, 0)
    lax.fori_loop(first, n_pairs, block_pair, 0, unroll=unroll)


def _attn_call(aq, ak, avt, sink_rows, n_ctx_tok, need_ctx):
    B, _, T, hd = aq.shape
    nblk = T // A_BLOCK
    cols = A_GROUP * A_BLOCK
    return pl.pallas_call(
        functools.partial(_attn_kernel, n_ctx_tok=n_ctx_tok, need_ctx=need_ctx),
        grid=(B, A_KV_HEADS),
        in_specs=[pl.BlockSpec((1, A_GROUP, T, hd), lambda b, h: (b, h, 0, 0)),
                  pl.BlockSpec((1, 1, T, hd), lambda b, h: (b, h, 0, 0)),
                  pl.BlockSpec((1, 1, nblk, LANES, A_BLOCK), lambda b, h: (b, h, 0, 0, 0)),
                  pl.BlockSpec((1, 1, cols), lambda b, h: (h, 0, 0))],
        out_specs=pl.BlockSpec((1, nblk, A_GROUP * hd, A_BLOCK), lambda b, h: (b, 0, h, 0)),
        out_shape=jax.ShapeDtypeStruct((B, nblk, A_WIDTH, A_BLOCK), BF16),
        scratch_shapes=[pltpu.VMEM((3, 3 * A_BLOCK, A_BLOCK), F32),
                        pltpu.VMEM((2, 3 * A_BLOCK + n_ctx_tok, cols), F32)],
        compiler_params=_params(("arbitrary", "arbitrary")),
        name="window_attn",
    )(aq, ak, avt, sink_rows)


def _head_norm(y):
    mu = jnp.mean(y, axis=-1, keepdims=True)
    yc = y - mu
    var = jnp.mean(yc * yc, axis=-1, keepdims=True)
    return yc * lax.rsqrt(var + NORM_EPS)


def _post_kernel(c_ref, x_ref, hb_ref, hm_ref, ya_ref, hr_ref, mod_ref, wg_ref, mnw_ref, rnw_ref,
                 wbm_ref, wba_ref, wbr_ref, wo_ref, o_ref, *, split):
    for i in range(x_ref.shape[0]):
        hb = hb_ref[i]
        off = 0
        hm = jnp.concatenate([_head_norm(hm_ref[i, h]) for h in range(M_HEADS)], axis=1) * mnw_ref[...]
        ym = (_sigmoid(_dot(hb, wg_ref[:, off:off + M_WIDTH])) * hm).astype(BF16)
        off += M_WIDTH
        hr = jnp.concatenate([_head_norm(hr_ref[i, h]) for h in range(R_HEADS)], axis=1) * rnw_ref[...]
        rg = _dot(hb, wg_ref[:, off:off + R_V_WIDTH])
        yr = (rg * _sigmoid(rg) * hr).astype(BF16)
        off += R_V_WIDTH
        z = _sigmoid(_dot(hb, wg_ref[:, off:off + D_MODEL])) * _dot(ym, wbm_ref[...])
        off += D_MODEL
        pa = jnp.concatenate([_dot_tn(ya_ref[i, c], wba_ref[...]) for c in range(TM // A_BLOCK)], axis=0)
        z = z + _sigmoid(_dot(hb, wg_ref[:, off:off + D_MODEL])) * pa
        off += D_MODEL
        z = z + _sigmoid(_dot(hb, wg_ref[:, off:off + D_MODEL])) * _dot(yr, wbr_ref[...])
        y = _dot(z.astype(BF16), wo_ref[...])
        one = pl.ds(i, 1)
        xres = _tile_rows(c_ref.at[one] if split else None, x_ref.at[one])
        o_ref[i] = xres + mod_ref[i, 0, 2:3, :] * y


def _post_call(ctx, xs, hb, hm, ya, hr, modsel, wg, mnw, rnw, wbm, wba, wbr, wo, t_off):
    B, T, D = hb.shape
    nt = T // TM - t_off
    split = ctx is not None
    assert not (split and t_off)

    def tile(t, b):
        return (b, t + t_off, 0)

    def head_tile(t, b):
        return (b, 0, t + t_off, 0)

    return pl.pallas_call(
        functools.partial(_post_kernel, split=split),
        grid=(nt, B // BB),
        in_specs=[pl.BlockSpec((BB, TM if split else HALO, D), lambda t, b: (b, 0, 0)),
                  pl.BlockSpec((BB, TM, D), (lambda t, b: (b, jnp.maximum(t - 1, 0), 0)) if split else tile),
                  pl.BlockSpec((BB, TM, D), tile),
                  pl.BlockSpec((BB, M_HEADS, TM, M_HEAD_DIM), head_tile),
                  pl.BlockSpec((BB, TM // A_BLOCK, A_WIDTH, A_BLOCK), lambda t, b: (b, t + t_off, 0, 0)),
                  pl.BlockSpec((BB, R_HEADS, TM, R_V_DIM), head_tile),
                  pl.BlockSpec((BB, 1, 6, D), lambda t, b: (b, jnp.minimum(t + t_off, 1), 0, 0)),
                  _const_spec(wg.shape), _const_spec(mnw.shape), _const_spec(rnw.shape),
                  _const_spec(wbm.shape), _const_spec(wba.shape), _const_spec(wbr.shape),
                  _const_spec(wo.shape)],
        out_specs=pl.BlockSpec((BB, TM, D), lambda t, b: (b, t, 0)),
        out_shape=jax.ShapeDtypeStruct((B, nt * TM, D), F32),
        compiler_params=_params(("parallel", "parallel")),
        name="post_merge",
    )(ctx if split else xs, xs, hb, hm, ya, hr, modsel, wg, mnw, rnw, wbm, wba, wbr, wo)


def _ffn_kernel(x_ref, xp_ref, xn_ref, mod_ref, nw_ref, wa_ref, wb_ref, cw_ref, cb_ref, wd_ref, fw_ref,
                o_ref, hext_ref, u_ref, *, has_ctx, final):
    t = pl.program_id(0)
    nt = pl.num_programs(0)
    first_lat = 1 if has_ctx else 0
    nw = nw_ref[...]
    ext = TM + 2 * HALO
    for i in range(x_ref.shape[0]):
        sh = mod_ref[i, 0, 3:4, :]
        sc = mod_ref[i, 0, 4:5, :]
        g2 = mod_ref[i, 0, 5:6, :]
        x = x_ref[i]
        hp = _norm_mod(xp_ref[i], nw, sc, sh)
        hn = _norm_mod(xn_ref[i], nw, sc, sh)
        hp = jnp.where(t <= first_lat, 0.0, hp)
        hn = jnp.where((t == first_lat - 1) | (t == nt - 1), 0.0, hn)
        hext_ref[i, 0:TM, :] = _norm_mod(x, nw, sc, sh).astype(BF16)
        hext_ref[i, TM:, :] = jnp.concatenate([hn, hp], axis=0).astype(BF16)
        for j in range(FFN_DIM // FFN_NC):
            cs = slice(j * FFN_NC, (j + 1) * FFN_NC)
            a = _dot(hext_ref[i], wa_ref[:, cs])
            b = _dot(hext_ref[i, 0:TM, :], wb_ref[:, cs])
            cw = cw_ref[:, cs]
            prev = pltpu.roll(a, 1, 0)[0:TM]
            nxt = pltpu.roll(a, ext - 1, 0)[0:TM]
            conv = prev * cw[0:1] + a[0:TM] * cw[1:2] + nxt * cw[2:3] + cb_ref[:, cs]
            u_ref[i, :, cs] = (conv * _sigmoid(conv) * b).astype(BF16)
        y = x + g2 * _dot(u_ref[i], wd_ref[...])
        if final:
            ms = jnp.mean(y * y, axis=-1, keepdims=True)
            y = y * lax.rsqrt(ms + NORM_EPS) * fw_ref[...]
        o_ref[i] = y


def _ffn_call(xs, modsel, nw, wa, wb, cw, cb, wd, fw, has_ctx, final):
    B, T, D = xs.shape
    nt = T // TM
    hb16 = TM // HALO
    first_lat = 1 if has_ctx else 0
    tile = pl.BlockSpec((BB, TM, D), lambda t, b: (b, t, 0))
    return pl.pallas_call(
        functools.partial(_ffn_kernel, has_ctx=has_ctx, final=final),
        grid=(nt, B // BB),
        in_specs=[tile,
                  pl.BlockSpec((BB, HALO, D), lambda t, b: (b, jnp.maximum(t * hb16 - 1, 0), 0)),
                  pl.BlockSpec((BB, HALO, D), lambda t, b: (b, jnp.minimum((t + 1) * hb16, T // HALO - 1), 0)),
                  pl.BlockSpec((BB, 1, 6, D), lambda t, b: (b, jnp.minimum(t + 1 - first_lat, 1), 0, 0)),
                  _const_spec(nw.shape), _const_spec(wa.shape), _const_spec(wb.shape),
                  _const_spec(cw.shape), _const_spec(cb.shape), _const_spec(wd.shape),
                  _const_spec(fw.shape)],
        out_specs=tile,
        out_shape=jax.ShapeDtypeStruct((B, T, D), F32),
        scratch_shapes=[pltpu.VMEM((BB, TM + 2 * HALO, D), BF16), pltpu.VMEM((BB, TM, FFN_DIM), BF16)],
        compiler_params=_params(("parallel", "parallel")),
        name="conv_ffn",
    )(xs, xs, xs, modsel, nw, wa, wb, cw, cb, wd, fw)


def _split_cols(w):
    out = []
    acc = 0
    for s in IN_SPLITS:
        out.append(w[:, acc:acc + s])
        acc += s
    return out


def _rope_tables(L, S):
    T = L + S
    f32 = np.float32
    j = np.arange(LANES)
    jj = j % 32
    inv_a = np.power(f32(ROPE_BASE), -(jj % 16).astype(f32) / f32(16.0)).astype(f32)
    t = np.arange(S)
    pos = np.where(((j % A_HEAD_DIM) // 32 == 0)[None, :], (t // GRID_W)[:, None], (t % GRID_W)[:, None])
    ang = pos.astype(f32) * inv_a[None, :]
    sign = np.where(jj < 16, f32(-1.0), f32(1.0))[None, :]
    ca = np.concatenate([np.ones((L, LANES), f32), np.cos(ang)], axis=0)
    sa = np.concatenate([np.zeros((L, LANES), f32), np.sin(ang) * sign], axis=0)
    half = R_QK_DIM // 2
    inv_r = np.power(f32(ROPE_BASE), -(j % half).astype(f32) / f32(half)).astype(f32)
    angr = np.arange(T).astype(f32)[:, None] * inv_r[None, :]
    signr = np.where(j < half, f32(-1.0), f32(1.0))[None, :]
    return tuple(jnp.asarray(a, F32) for a in (ca, sa, np.cos(angr), np.sin(angr) * signr))


def kernel(x, c, ctx, c_ctx, mod_w, mod_b, norm1_w, norm2_w, w_in, m_gate_bias, m_conv_w, m_norm_w,
           a_sink, ret_logit, ret_norm_w, w_br_m, w_br_a, w_br_r, w_out, ffn_up, ffn_conv_w, ffn_conv_b,
           ffn_down, final_norm_w):
    B, S, D = x.shape
    L = ctx.shape[1]
    depth = mod_w.shape[0]
    assert D == D_MODEL and L == TM and S % TM == 0 and S >= 3 * A_BLOCK and B % BB == 0
    T = L + S
    n_ctx = L // CHUNK

    rows = -(-(B + 1) // 8) * 8
    cpad = jnp.zeros((rows, D), F32).at[:B].set(c).at[B].set(c_ctx)
    mods = _modulation(cpad, mod_w, mod_b)
    tabs = _rope_tables(L, S)
    stream = (ctx, x) if depth > 1 else (None, jnp.concatenate([ctx, x], axis=1))

    gperm = jnp.arange(4 * M_HEADS).reshape(4, M_HEADS).T.reshape(-1)

    for l in range(depth):
        last = l == depth - 1
        lat = mods[l, :B].reshape(B, 6, D)
        cm = jnp.broadcast_to(mods[l, B].reshape(1, 6, D), (B, 6, D))
        modsel = jnp.stack([cm, lat], axis=1)

        (w_mq, w_mk, w_mv, w_mo, w_mg, w_aq, w_ak, w_av,
         w_rq, w_rk, w_rv, w_rg, w_gm, w_ga, w_gr) = _split_cols(w_in[l])
        wqk = jnp.concatenate([w_mq, w_mk], axis=1).astype(BF16)
        wr = jnp.concatenate([w_mv, w_aq, w_ak, w_av, w_rq, w_rk, w_rv], axis=1).astype(BF16)
        wgt = w_mg[:, gperm].T.astype(BF16)
        gbt = jnp.broadcast_to(m_gate_bias[l][gperm].reshape(-1, 1), (4 * M_HEADS, LANES))

        (hb, mq, mk, mv, gr, aq, ak, av, rq, rk, rv) = _pre_call(
            *stream, modsel, norm1_w[l].reshape(1, D), wqk, m_conv_w[l], wr, wgt, gbt, tabs)

        hm = _mlstm_call(mq, mk, mv, gr, n_ctx)
        sink_rows = jnp.repeat(a_sink[l].reshape(A_KV_HEADS, A_GROUP), A_BLOCK, axis=1)[:, None, :]
        ya = _attn_call(aq, ak, av, sink_rows.astype(F32) * LOG2E, L, not last)
        lgb = jnp.broadcast_to(ret_logit[l].T[:, :, None, None], (R_HEADS, 2, 8, R_V_DIM)).astype(F32)
        hr = _ret_call(rq, rk, rv, lgb, L // RCHUNK)

        t_off = 1 if last else 0
        wgates = jnp.concatenate([w_mo, w_rg, w_gm, w_ga, w_gr], axis=1).astype(BF16)
        x1 = _post_call(*stream, hb, hm, ya, hr, modsel, wgates, m_norm_w[l].reshape(1, -1),
                        ret_norm_w[l].reshape(1, -1), w_br_m[l].astype(BF16), w_br_a[l].astype(BF16),
                        w_br_r[l].astype(BF16), w_out[l].astype(BF16), t_off)
        xc = _ffn_call(x1, modsel, norm2_w[l].reshape(1, D), ffn_up[l][:, :FFN_DIM].astype(BF16),
                       ffn_up[l][:, FFN_DIM:].astype(BF16), ffn_conv_w[l], ffn_conv_b[l].reshape(1, -1),
                       ffn_down[l].astype(BF16), final_norm_w.reshape(1, D), not last, last)
        stream = (None, xc)
    return xc
```

```python
import functools

import jax
import jax.numpy as jnp
import numpy as np
from jax import lax
from jax.experimental import pallas as pl
from jax.experimental.pallas import tpu as pltpu

F32 = jnp.float32
BF16 = jnp.bfloat16

D_MODEL = 1024
GRID_W = 64
NORM_EPS = 1e-6
ROPE_BASE = 10000.0
NEG_INF = -1e30
LOG2E = 1.4426950408889634

M_HEADS = 4
M_HEAD_DIM = 128
M_WIDTH = M_HEADS * M_HEAD_DIM
A_HEADS = 8
A_KV_HEADS = 2
A_GROUP = A_HEADS // A_KV_HEADS
A_HEAD_DIM = 64
A_WIDTH = A_HEADS * A_HEAD_DIM
A_KV_WIDTH = A_KV_HEADS * A_HEAD_DIM
A_WINDOW = 128
A_BLOCK = 128
R_HEADS = 4
R_QK_DIM = 128
R_V_DIM = 256
R_QK_WIDTH = R_HEADS * R_QK_DIM
R_V_WIDTH = R_HEADS * R_V_DIM
FFN_DIM = 2816

IN_SPLITS = (M_WIDTH, M_WIDTH, M_WIDTH, M_WIDTH, 4 * M_HEADS,
             A_WIDTH, A_KV_WIDTH, A_KV_WIDTH,
             R_QK_WIDTH, R_QK_WIDTH, R_V_WIDTH, R_V_WIDTH,
             D_MODEL, D_MODEL, D_MODEL)

TM = 256
BB = 2
FFN_BB = 4
HALO = 8
CHUNK = 128
RCHUNK = 256
FFN_NC = 256
LANES = 128
VMEM_LIMIT = 56 * 1024 * 1024


def _sigmoid(x):
    return 1.0 / (1.0 + jnp.exp(-x))


def _log_sigmoid(x):
    return jnp.minimum(x, 0.0) - jnp.log1p(jnp.exp(-jnp.abs(x)))


def _dot(a, b):
    return jnp.dot(a, b, preferred_element_type=F32)


def _dot_nt(a, b):
    return lax.dot_general(a, b, (((1,), (1,)), ((), ())), preferred_element_type=F32)


def _dot_tn(a, b):
    return lax.dot_general(a, b, (((0,), (0,)), ((), ())), preferred_element_type=F32)


def _const_spec(shape):
    nd = len(shape)
    return pl.BlockSpec(shape, lambda *_: (0,) * nd, pipeline_mode=pl.Buffered(1))


def _params(sem):
    return pltpu.CompilerParams(dimension_semantics=sem, vmem_limit_bytes=VMEM_LIMIT)


def _mod_kernel(c_ref, w_ref, b_ref, o_ref):
    c = c_ref[...]
    s = (c * _sigmoid(c)).astype(BF16)
    o_ref[0] = _dot(s, w_ref[0].astype(BF16)) + b_ref[0]


def _modulation(cpad, mod_w, mod_b):
    depth, d, n = mod_w.shape
    rows = cpad.shape[0]
    bn = 1024
    return pl.pallas_call(
        _mod_kernel,
        grid=(depth, n // bn),
        in_specs=[pl.BlockSpec((rows, d), lambda l, j: (0, 0)),
                  pl.BlockSpec((1, d, bn), lambda l, j: (l, 0, j)),
                  pl.BlockSpec((1, 1, bn), lambda l, j: (l, 0, j))],
        out_specs=pl.BlockSpec((1, rows, bn), lambda l, j: (l, 0, j)),
        out_shape=jax.ShapeDtypeStruct((depth, rows, n), F32),
        compiler_params=_params(("parallel", "parallel")),
        name="modulation",
    )(cpad, mod_w, mod_b.reshape(depth, 1, n))


def _norm_mod(xv, nw, sc, sh):
    ms = jnp.mean(xv * xv, axis=-1, keepdims=True)
    return (xv * lax.rsqrt(ms + NORM_EPS) * nw) * (1.0 + sc) + sh


def _seg_cumsum(x, axis, reverse):
    n = x.shape[axis]
    idx = lax.broadcasted_iota(jnp.int32, x.shape, axis)
    s = 1
    while s < n:
        if reverse:
            shifted = pltpu.roll(x, n - s, axis)
            x = x + jnp.where(idx < n - s, shifted, 0.0)
        else:
            shifted = pltpu.roll(x, s, axis)
            x = x + jnp.where(idx >= s, shifted, 0.0)
        s *= 2
    return x


def _gate_transform(raw, gate_axis):
    tok_axis = 1 - gate_axis
    k = lax.broadcasted_iota(jnp.int32, raw.shape, gate_axis) % 4
    lf = _log_sigmoid(raw)
    cum_f = _seg_cumsum(jnp.where(k == 1, lf, 0.0), tok_axis, False)
    cum_b = _seg_cumsum(jnp.where(k == 3, lf, 0.0), tok_axis, True)
    bsel = jnp.where(k == 1, cum_f, jnp.where(k == 3, cum_b, 0.0))
    n = raw.shape[gate_axis]
    bnext = pltpu.roll(bsel, n - 1, gate_axis)
    nt = raw.shape[tok_axis]
    tot_f = lax.slice_in_dim(cum_f, nt - 1, nt, axis=tok_axis)
    tot_b = lax.slice_in_dim(cum_b, 0, 1, axis=tok_axis)
    total = jnp.broadcast_to(tot_f + tot_b, raw.shape)
    return jnp.where(k % 2 == 1, bsel, raw - bnext), total


def _pre_kernel(c_ref, x_ref, xp_ref, xn_ref, mod_ref, nw_ref, wqk_ref, cw_ref, wr_ref, wgt_ref,
                gbt_ref, ca_ref, sa_ref, cr_ref, sr_ref, *out_and_scratch, split):
    for i in range(x_ref.shape[0]):
        one = pl.ds(i, 1)
        _pre_tile(c_ref.at[one] if split else None,
                  x_ref.at[one], xp_ref.at[one], xn_ref.at[one], mod_ref.at[one], nw_ref, wqk_ref, cw_ref,
                  wr_ref, wgt_ref, gbt_ref, ca_ref, sa_ref, cr_ref, sr_ref,
                  *[r.at[one] for r in out_and_scratch[:-1]], out_and_scratch[-1].at[i])


def _tile_rows(c_ref, x_ref):
    if c_ref is None:
        return x_ref[0]
    return jnp.where(pl.program_id(0) == 0, c_ref[0], x_ref[0])


def _pre_tile(c_ref, x_ref, xp_ref, xn_ref, mod_ref, nw_ref, wqk_ref, cw_ref, wr_ref, wgt_ref,
              gbt_ref, ca_ref, sa_ref, cr_ref, sr_ref,
              hb_ref, mq_ref, mk_ref, mv_ref, gr_ref, aq_ref, ak_ref, av_ref,
              rq_ref, rk_ref, rv_ref, hext_ref):
    t = pl.program_id(0)
    nt = pl.num_programs(0)
    sh = mod_ref[0, 0, 0:1, :]
    sc = mod_ref[0, 0, 1:2, :]
    nw = nw_ref[...]
    hb = _norm_mod(_tile_rows(c_ref, x_ref), nw, sc, sh).astype(BF16)
    hb_ref[0] = hb

    grow = _dot_nt(wgt_ref[...], hb) + gbt_ref[:, 0:1]
    for c in range(TM // CHUNK):
        gr, tot = _gate_transform(grow[:, c * CHUNK:(c + 1) * CHUNK], 0)
        aend = gr + pltpu.roll(tot, 4 * M_HEADS - 1, 0)
        amax = jnp.broadcast_to(jnp.max(aend, axis=1, keepdims=True), aend.shape)
        for h in range(M_HEADS):
            hp_, o0 = h // 2, 8 * (h % 2)
            gr_ref[0, hp_, c, o0:o0 + 4, :] = gr[4 * h:4 * h + 4]
            for d in range(2):
                gr_ref[0, hp_, c, o0 + 4 + d:o0 + 5 + d, :] = tot[4 * h + 2 * d + 1:4 * h + 2 * d + 2]
                gr_ref[0, hp_, c, o0 + 6 + d:o0 + 7 + d, :] = amax[4 * h + 2 * d:4 * h + 2 * d + 1]

    hp = _norm_mod(xp_ref[0], nw, sc, sh)
    hn = _norm_mod(xn_ref[0], nw, sc, sh)
    hp = jnp.where(t <= 1, 0.0, hp)
    hn = jnp.where((t == 0) | (t == nt - 1), 0.0, hn)
    hext_ref[0:TM, :] = hb
    hext_ref[TM:, :] = jnp.concatenate([hn, hp], axis=0).astype(BF16)
    hext = hext_ref[...]
    ext = TM + 2 * HALO

    for j, (dst, scale) in enumerate(((mq_ref, 1.0), (mk_ref, M_HEAD_DIM ** -0.5))):
        p = _dot(hext, wqk_ref[:, j * M_WIDTH:(j + 1) * M_WIDTH])
        cw = cw_ref[:, j * M_WIDTH:(j + 1) * M_WIDTH]
        prev = pltpu.roll(p, 1, 0)[0:TM]
        nxt = pltpu.roll(p, ext - 1, 0)[0:TM]
        conv = prev * cw[0:1] + p[0:TM] * cw[1:2] + nxt * cw[2:3]
        act = conv * _sigmoid(conv) * scale
        for h in range(M_HEADS):
            dst[0, h] = act[:, h * M_HEAD_DIM:(h + 1) * M_HEAD_DIM].astype(BF16)

    off = 0
    p = _dot(hb, wr_ref[:, off:off + M_WIDTH])
    off += M_WIDTH
    for h in range(M_HEADS):
        for c in range(TM // CHUNK):
            blk = p[c * CHUNK:(c + 1) * CHUNK, h * M_HEAD_DIM:(h + 1) * M_HEAD_DIM]
            mv_ref[0, h, c] = blk.T.astype(BF16)

    lane = lax.broadcasted_iota(jnp.int32, (TM, LANES), 1)
    first16 = (lane % 32) < 16
    ca = ca_ref[...]
    sa = sa_ref[...]

    def rope_a(xs):
        partner = jnp.where(first16, pltpu.roll(xs, LANES - 16, 1), pltpu.roll(xs, 16, 1))
        return xs * ca + partner * sa

    p = _dot(hb, wr_ref[:, off:off + A_WIDTH])
    off += A_WIDTH
    for s in range(A_WIDTH // LANES):
        r = (rope_a(p[:, s * LANES:(s + 1) * LANES]) * (A_HEAD_DIM ** -0.5 * LOG2E)).astype(BF16)
        aq_ref[0, 2 * s] = r[:, 0:A_HEAD_DIM]
        aq_ref[0, 2 * s + 1] = r[:, A_HEAD_DIM:]
    p = _dot(hb, wr_ref[:, off:off + 2 * A_KV_WIDTH])
    off += 2 * A_KV_WIDTH
    r = rope_a(p[:, 0:LANES]).astype(BF16)
    for h in range(A_KV_HEADS):
        ak_ref[0, h] = r[:, h * A_HEAD_DIM:(h + 1) * A_HEAD_DIM]
    pv = p[:, LANES:]
    for h in range(A_KV_HEADS):
        vh = pv if h == 0 else pltpu.roll(pv, A_HEAD_DIM, 1)
        ve = jnp.where(lane < A_HEAD_DIM, vh, 1.0)
        for c in range(TM // A_BLOCK):
            av_ref[0, h, c] = ve[c * A_BLOCK:(c + 1) * A_BLOCK].T.astype(BF16)

    cr = cr_ref[...]
    sr = sr_ref[...]
    for is_k in (False, True):
        p = _dot(hb, wr_ref[:, off:off + R_QK_WIDTH])
        off += R_QK_WIDTH
        for h in range(R_HEADS):
            xs = p[:, h * R_QK_DIM:(h + 1) * R_QK_DIM]
            rot = xs * cr + pltpu.roll(xs, R_QK_DIM // 2, 1) * sr
            if is_k:
                rot = rot * R_QK_DIM ** -0.5
                for c in range(TM // CHUNK):
                    rk_ref[0, h, c] = rot[c * CHUNK:(c + 1) * CHUNK].T.astype(BF16)
            else:
                rq_ref[0, h] = rot.astype(BF16)
    p = _dot(hb, wr_ref[:, off:off + R_V_WIDTH])
    for h in range(R_HEADS):
        rv_ref[0, h] = p[:, h * R_V_DIM:(h + 1) * R_V_DIM].astype(BF16)


def _stream_specs(ctx, xs):
    hb = TM // HALO
    D = xs.shape[-1]
    off = 0 if ctx is None else 1
    last = xs.shape[1] // HALO - 1
    return [
        pl.BlockSpec((BB, TM if off else HALO, D), lambda t, b: (b, 0, 0)),
        pl.BlockSpec((BB, TM, D), lambda t, b: (b, jnp.maximum(t - off, 0), 0)),
        pl.BlockSpec((BB, HALO, D), lambda t, b: (b, jnp.maximum((t - off) * hb - 1, 0), 0)),
        pl.BlockSpec((BB, HALO, D), lambda t, b: (b, jnp.clip((t - off + 1) * hb, 0, last), 0)),
    ]


def _pre_call(ctx, xs, modsel, nw, wqk, cw, wr, wgt, gbt, tabs):
    B, _, D = xs.shape
    T = xs.shape[1] + (0 if ctx is None else ctx.shape[1])
    nt = T // TM
    nch = T // CHUNK

    def tile(t, b):
        return (b, t, 0)

    def head_tile(t, b):
        return (b, 0, t, 0)

    in_specs = _stream_specs(ctx, xs) + [
        pl.BlockSpec((BB, 1, 6, D), lambda t, b: (b, jnp.minimum(t, 1), 0, 0)),
        _const_spec(nw.shape), _const_spec(wqk.shape), _const_spec(cw.shape), _const_spec(wr.shape),
        _const_spec(wgt.shape), _const_spec(gbt.shape),
    ] + [pl.BlockSpec((TM, LANES), lambda t, b: (t, 0)) for _ in range(4)]
    out_shape = [
        jax.ShapeDtypeStruct((B, T, D), BF16),
        jax.ShapeDtypeStruct((B, M_HEADS, T, M_HEAD_DIM), BF16),
        jax.ShapeDtypeStruct((B, M_HEADS, T, M_HEAD_DIM), BF16),
        jax.ShapeDtypeStruct((B, M_HEADS, nch, M_HEAD_DIM, CHUNK), BF16),
        jax.ShapeDtypeStruct((B, 2, nch, 16, CHUNK), F32),
        jax.ShapeDtypeStruct((B, A_HEADS, T, A_HEAD_DIM), BF16),
        jax.ShapeDtypeStruct((B, A_KV_HEADS, T, A_HEAD_DIM), BF16),
        jax.ShapeDtypeStruct((B, A_KV_HEADS, T // A_BLOCK, LANES, A_BLOCK), BF16),
        jax.ShapeDtypeStruct((B, R_HEADS, T, R_QK_DIM), BF16),
        jax.ShapeDtypeStruct((B, R_HEADS, nch, R_QK_DIM, CHUNK), BF16),
        jax.ShapeDtypeStruct((B, R_HEADS, T, R_V_DIM), BF16),
    ]
    out_specs = [
        pl.BlockSpec((BB, TM, D), tile),
        pl.BlockSpec((BB, M_HEADS, TM, M_HEAD_DIM), head_tile),
        pl.BlockSpec((BB, M_HEADS, TM, M_HEAD_DIM), head_tile),
        pl.BlockSpec((BB, M_HEADS, TM // CHUNK, M_HEAD_DIM, CHUNK), lambda t, b: (b, 0, t, 0, 0)),
        pl.BlockSpec((BB, 2, TM // CHUNK, 16, CHUNK), lambda t, b: (b, 0, t, 0, 0)),
        pl.BlockSpec((BB, A_HEADS, TM, A_HEAD_DIM), head_tile),
        pl.BlockSpec((BB, A_KV_HEADS, TM, A_HEAD_DIM), head_tile),
        pl.BlockSpec((BB, A_KV_HEADS, TM // A_BLOCK, LANES, A_BLOCK), lambda t, b: (b, 0, t, 0, 0)),
        pl.BlockSpec((BB, R_HEADS, TM, R_QK_DIM), head_tile),
        pl.BlockSpec((BB, R_HEADS, TM // CHUNK, R_QK_DIM, CHUNK), lambda t, b: (b, 0, t, 0, 0)),
        pl.BlockSpec((BB, R_HEADS, TM, R_V_DIM), head_tile),
    ]
    return pl.pallas_call(
        functools.partial(_pre_kernel, split=ctx is not None),
        grid=(nt, B // BB),
        in_specs=in_specs,
        out_specs=out_specs,
        out_shape=out_shape,
        scratch_shapes=[pltpu.VMEM((BB, TM + 2 * HALO, D), BF16)],
        compiler_params=_params(("parallel", "parallel")),
        name="pre_proj",
    )(xs if ctx is None else ctx, xs, xs, xs, modsel, nw, wqk, cw, wr, wgt, gbt, *tabs)


def _bwd_chunk(i, n_ctx, n_all):
    if isinstance(i, int):
        return n_ctx - 1 - i if i < n_ctx else n_all - 1 + n_ctx - i
    return n_all - 1 + n_ctx - i


def _chunk_start(c):
    return c * CHUNK if isinstance(c, int) else pl.multiple_of(c * CHUNK, CHUNK)


def _chunk_loop(body, n_ctx, n_all, unroll):
    for i in range(n_ctx):
        body(i, 0)
    n_lat = n_all - n_ctx
    while n_lat % unroll:
        unroll //= 2
    lax.fori_loop(n_ctx, n_all, body, 0, unroll=unroll)


def _mlstm_kernel(q_ref, k_ref, vt_ref, gr_ref, o_ref, c_ref, n_ref, m_ref,
                  cs_ref, ns_ref, ms_ref, qp_ref, nq_ref, *, n_ctx):
    n_all = gr_ref.shape[2]
    dh = q_ref.shape[3]
    c_ref[...] = jnp.zeros(c_ref.shape, F32)
    n_ref[...] = jnp.zeros(n_ref.shape, F32)
    m_ref[...] = jnp.zeros(m_ref.shape, F32)

    def scan_unit(c, hh, d):
        u = 2 * hh + d
        c0 = _chunk_start(c)
        k = k_ref[0, hh, pl.ds(c0, CHUNK), :]
        vt = vt_ref[0, hh, c]
        rows = gr_ref[0, 0, c, 8 * hh:8 * hh + 8, :]
        tot = rows[4 + d:5 + d]
        aend = rows[2 * d:2 * d + 1] + tot
        amax = rows[6 + d:7 + d]
        cst = c_ref[u]
        nv = n_ref[u]
        m = m_ref[u]
        cs_ref[u, c] = cst.astype(BF16)
        ns_ref[u, c] = jnp.broadcast_to(nv, (16, dh)).astype(BF16)
        ms_ref[u, c] = jnp.broadcast_to(m, (8, CHUNK))
        m_new = jnp.maximum(tot + m, amax)
        w = jnp.exp(aend - m_new)
        dec = jnp.exp(tot + m - m_new)
        vw = (vt.astype(F32) * w).astype(BF16)
        c_ref[u] = dec * cst + _dot(vw, k)
        wk = _dot(jnp.broadcast_to(w, (16, CHUNK)).astype(BF16), k)
        n_ref[u] = dec * nv + wk[0:1]
        m_ref[u] = m_new

    def scan_step(i, carry):
        cb = _bwd_chunk(i, n_ctx, n_all)
        for hh in range(2):
            scan_unit(i, hh, 0)
            scan_unit(cb, hh, 1)
        return carry

    _chunk_loop(scan_step, n_ctx, n_all, 32)

    row = lax.broadcasted_iota(jnp.int32, (CHUNK, CHUNK), 0)
    col = lax.broadcasted_iota(jnp.int32, (CHUNK, CHUNK), 1)
    tri = (row <= col, row >= col)

    def query_products(c, slot):
        c0 = _chunk_start(c)
        for hh in range(2):
            q = q_ref[0, hh, pl.ds(c0, CHUNK), :]
            k = k_ref[0, hh, pl.ds(c0, CHUNK), :]
            qp_ref[slot, hh, 0] = _dot_nt(k, q)
            for d in range(2):
                u = 2 * hh + d
                qp_ref[slot, hh, 1 + d] = _dot_nt(cs_ref[u, c], q)
                nq_ref[slot, u] = _dot_nt(ns_ref[u, c], q)[0:8]

    def outputs(c, slot):
        c0 = _chunk_start(c)
        gcv = jnp.concatenate([gr_ref[0, 0, c], jnp.zeros((CHUNK - 16, CHUNK), F32)], axis=0).T
        for hh in range(2):
            vt = vt_ref[0, hh, c]
            rows = gr_ref[0, 0, c, 8 * hh:8 * hh + 8, :]
            st = qp_ref[slot, hh, 0]
            sds, iws, invs = [], [], []
            for d in range(2):
                u = 2 * hh + d
                a0c = gcv[:, 8 * hh + 2 * d:8 * hh + 2 * d + 1]
                b = rows[2 * d + 1:2 * d + 2]
                m = ms_ref[u, c, 0:1, :]
                dlog = jnp.where(tri[d], a0c + b, NEG_INF)
                mj = jnp.maximum(b + m, jnp.max(dlog, axis=0, keepdims=True))
                sd = st * jnp.exp(dlog - mj)
                iw = jnp.exp(b + m - mj)
                den = iw * nq_ref[slot, u, 0:1, :] + jnp.sum(sd, axis=0, keepdims=True)
                sds.append(sd.astype(BF16))
                iws.append(iw)
                invs.append(1.0 / jnp.maximum(jnp.abs(den), jnp.exp(-mj)))
            num = _dot(vt, jnp.concatenate(sds, axis=1))
            acc = None
            for d in range(2):
                ht = (num[:, d * CHUNK:(d + 1) * CHUNK] + iws[d] * qp_ref[slot, hh, 1 + d]) * invs[d]
                acc = ht if acc is None else acc + ht
            o_ref[0, hh, pl.ds(c0, CHUNK), :] = acc.T

    query_products(0, 0)

    def out_pair(i, carry):
        c = 2 * i
        query_products(c + 1, 1)
        outputs(c, 0)
        query_products(jnp.minimum(c + 2, n_all - 1), 0)
        outputs(c + 1, 1)
        return carry

    n_pairs = n_all // 2
    unroll = 16
    first = n_pairs % unroll
    for i in range(first):
        out_pair(i, 0)
    lax.fori_loop(first, n_pairs, out_pair, 0, unroll=unroll)


def _mlstm_call(mq, mk, mvt, gr, n_ctx):
    B, H, T, dh = mq.shape
    nch = T // CHUNK
    qspec = pl.BlockSpec((1, 2, T, dh), lambda b, p: (b, p, 0, 0))
    return pl.pallas_call(
        functools.partial(_mlstm_kernel, n_ctx=n_ctx),
        grid=(B, H // 2),
        in_specs=[qspec, qspec,
                  pl.BlockSpec((1, 2, nch, dh, CHUNK), lambda b, p: (b, p, 0, 0, 0)),
                  pl.BlockSpec((1, 1, nch, 16, CHUNK), lambda b, p: (b, p, 0, 0, 0))],
        out_specs=pl.BlockSpec((1, 2, T, dh), lambda b, p: (b, p, 0, 0)),
        out_shape=jax.ShapeDtypeStruct((B, H, T, dh), F32),
        scratch_shapes=[pltpu.VMEM((4, dh, dh), F32), pltpu.VMEM((4, 1, dh), F32),
                        pltpu.VMEM((4, 1, CHUNK), F32),
                        pltpu.VMEM((4, nch, dh, dh), BF16), pltpu.VMEM((4, nch, 16, dh), BF16),
                        pltpu.VMEM((4, nch, 8, CHUNK), F32),
                        pltpu.VMEM((2, 2, 3, dh, CHUNK), F32), pltpu.VMEM((2, 4, 8, CHUNK), F32)],
        compiler_params=_params(("parallel", "parallel")),
        name="mlstm",
    )(mq, mk, mvt, gr)


def _ret_kernel(q_ref, kt_ref, v_ref, lg_ref, o_ref, st_ref, sts_ref, dec_ref, ib_ref, wb_ref, cd_ref,
                *, n_ctx):
    sub = RCHUNK // CHUNK
    n_all = kt_ref.shape[2] // sub
    dk = q_ref.shape[3]
    dv = v_ref.shape[3]
    st_ref[...] = jnp.zeros(st_ref.shape, F32)

    @pl.when(pl.program_id(1) == 0)
    def _():
        row = lax.broadcasted_iota(jnp.int32, (RCHUNK, RCHUNK), 0).astype(F32)
        col = lax.broadcasted_iota(jnp.int32, (RCHUNK, RCHUNK), 1).astype(F32)
        kcol = lax.broadcasted_iota(jnp.int32, (dk, RCHUNK), 1).astype(F32)
        dec = None
        for d in range(2):
            lg = _log_sigmoid(lg_ref[0, d])[0:1]
            diff = row - col if d == 0 else col - row
            dd = jnp.where(diff >= 0, jnp.exp(lg * jnp.maximum(diff, 0.0)), 0.0)
            dec = dd if dec is None else dec + dd
            pin = row if d == 0 else RCHUNK - 1.0 - row
            ib_ref[d] = jnp.exp(lg * (pin + 1.0))
            pkey = kcol if d == 0 else RCHUNK - 1.0 - kcol
            wb_ref[d] = jnp.exp(lg * (RCHUNK - 1.0 - pkey))
            cd_ref[d] = jnp.broadcast_to(jnp.exp(lg * RCHUNK), (8, dv))
        dec_ref[...] = dec

    cds = [cd_ref[d, 0:1, :] for d in range(2)]

    def rows(c):
        r0 = c * RCHUNK
        return pl.ds(r0 if isinstance(c, int) else pl.multiple_of(r0, RCHUNK), RCHUNK)

    def k_t(c):
        return jnp.concatenate([kt_ref[0, 0, sub * c + j] for j in range(sub)], axis=1)

    def scan_step(i, carry):
        cb = _bwd_chunk(i, n_ctx, n_all)
        for d, c in ((0, i), (1, cb)):
            st = st_ref[d]
            sts_ref[c, :, d * dv:(d + 1) * dv] = st.astype(BF16)
            kw = (k_t(c).astype(F32) * wb_ref[d]).astype(BF16)
            st_ref[d] = cds[d] * st + _dot(kw, v_ref[0, 0, rows(c), :])
        return carry

    _chunk_loop(scan_step, n_ctx, n_all, 16)

    def out_step(c, carry):
        q = q_ref[0, 0, rows(c), :]
        s = (_dot(q, k_t(c)) * dec_ref[...]).astype(BF16)
        qst = _dot(q, sts_ref[c])
        o_ref[0, 0, rows(c), :] = (_dot(s, v_ref[0, 0, rows(c), :])
                                   + ib_ref[0] * qst[:, 0:dv] + ib_ref[1] * qst[:, dv:])
        return carry

    _chunk_loop(out_step, n_ctx, n_all, 16)


def _ret_call(rq, rkt, rv, lgb, n_ctx):
    B, H, T, dk = rq.shape
    dv = rv.shape[-1]
    nch = T // CHUNK
    assert dv == RCHUNK and dk <= RCHUNK and T % RCHUNK == 0
    return pl.pallas_call(
        functools.partial(_ret_kernel, n_ctx=n_ctx),
        grid=(H, B),
        in_specs=[pl.BlockSpec((1, 1, T, dk), lambda h, b: (b, h, 0, 0)),
                  pl.BlockSpec((1, 1, nch, dk, CHUNK), lambda h, b: (b, h, 0, 0, 0)),
                  pl.BlockSpec((1, 1, T, dv), lambda h, b: (b, h, 0, 0)),
                  pl.BlockSpec((1, 2, 8, dv), lambda h, b: (h, 0, 0, 0))],
        out_specs=pl.BlockSpec((1, 1, T, dv), lambda h, b: (b, h, 0, 0)),
        out_shape=jax.ShapeDtypeStruct((B, H, T, dv), F32),
        scratch_shapes=[pltpu.VMEM((2, dk, dv), F32), pltpu.VMEM((T // RCHUNK, dk, 2 * dv), BF16),
                        pltpu.VMEM((RCHUNK, RCHUNK), F32), pltpu.VMEM((2, RCHUNK, dv), F32),
                        pltpu.VMEM((2, dk, RCHUNK), F32), pltpu.VMEM((2, 8, dv), F32)],
        compiler_params=_params(("parallel", "arbitrary")),
        name="retention",
    )(rq, rkt, rv, lgb)


def _attn_kernel(q_ref, k_ref, vt_ref, sink_ref, o_ref, bias_ref, s_ref, *, n_ctx_tok, need_ctx):
    T = q_ref.shape[2]
    L = n_ctx_tok
    S = T - L
    nb = S // A_BLOCK
    nspan = 3
    span = nspan * A_BLOCK
    ncb = L // A_BLOCK
    cols = A_GROUP * A_BLOCK
    kc = k_ref[0, 0, 0:L, :]
    vtc = jnp.concatenate([vt_ref[0, 0, j] for j in range(ncb)], axis=1)
    sink = sink_ref[0]

    @pl.when((pl.program_id(0) == 0) & (pl.program_id(1) == 0))
    def _():
        kpos = lax.broadcasted_iota(jnp.int32, (span, A_BLOCK), 0)
        qpos = lax.broadcasted_iota(jnp.int32, (span, A_BLOCK), 1)
        for delta in range(nspan):
            bias_ref[delta] = jnp.where(jnp.abs(qpos + delta * A_BLOCK - kpos) <= A_WINDOW, 0.0, NEG_INF)

    def finish(blk, acc, m):
        den = acc[A_HEAD_DIM:A_HEAD_DIM + 1] + jnp.exp2(sink - m)
        ot = (acc[0:A_HEAD_DIM] * (1.0 / den)).astype(BF16)
        for g in range(A_GROUP):
            o_ref[0, blk, g * A_HEAD_DIM:(g + 1) * A_HEAD_DIM, :] = ot[:, g * A_BLOCK:(g + 1) * A_BLOCK]

    for cbk in range(ncb):
        if need_ctx:
            qs = q_ref[0, :, cbk * A_BLOCK:(cbk + 1) * A_BLOCK, :].reshape(cols, A_HEAD_DIM)
            s = _dot_nt(kc, qs)
            m = jnp.maximum(jnp.max(s, axis=0, keepdims=True), sink)
            finish(cbk, _dot(vtc, jnp.exp2(s - m).astype(BF16)), m)
        else:
            o_ref[0, cbk] = jnp.zeros(o_ref.shape[2:], BF16)

    def first_key_block(n):
        return jnp.clip(n - 1, 0, nb - nspan)

    def scores(n, slot):
        q0 = pl.multiple_of(L + n * A_BLOCK, A_BLOCK)
        kb = first_key_block(n)
        k0 = pl.multiple_of(L + kb * A_BLOCK, A_BLOCK)
        qs = q_ref[0, :, pl.ds(q0, A_BLOCK), :].reshape(cols, A_HEAD_DIM)
        bias = bias_ref[n - kb]
        s_ref[slot, 0:span, :] = (_dot_nt(k_ref[0, 0, pl.ds(k0, span), :], qs)
                                  + jnp.concatenate([bias] * A_GROUP, axis=1))
        s_ref[slot, span:, :] = _dot_nt(kc, qs)

    def softmax_pv(n, slot):
        kb = first_key_block(n)
        s = s_ref[slot]
        m = jnp.maximum(jnp.max(s, axis=0, keepdims=True), sink)
        p = jnp.exp2(s - m).astype(BF16)
        vt = jnp.concatenate([vt_ref[0, 0, ncb + kb + j] for j in range(nspan)] + [vtc], axis=1)
        finish(ncb + n, _dot(vt, p), m)

    scores(0, 0)

    def block_pair(i, carry):
        n = 2 * i
        scores(n + 1, 1)
        softmax_pv(n, 0)
        scores(jnp.minimum(n + 2, nb - 1), 0)
        softmax_pv(n + 1, 1)
        return carry

    n_pairs = nb // 2
    unroll = 8
    first = n_pairs % unroll
    for i in range(first):
        block_pair(i, 0)
    lax.fori_loop(first, n_pairs, block_pair, 0, unroll=unroll)


def _attn_call(aq, ak, avt, sink_rows, n_ctx_tok, need_ctx):
    B, _, T, hd = aq.shape
    nblk = T // A_BLOCK
    cols = A_GROUP * A_BLOCK
    return pl.pallas_call(
        functools.partial(_attn_kernel, n_ctx_tok=n_ctx_tok, need_ctx=need_ctx),
        grid=(B, A_KV_HEADS),
        in_specs=[pl.BlockSpec((1, A_GROUP, T, hd), lambda b, h: (b, h, 0, 0)),
                  pl.BlockSpec((1, 1, T, hd), lambda b, h: (b, h, 0, 0)),
                  pl.BlockSpec((1, 1, nblk, LANES, A_BLOCK), lambda b, h: (b, h, 0, 0, 0)),
                  pl.BlockSpec((1, 1, cols), lambda b, h: (h, 0, 0))],
        out_specs=pl.BlockSpec((1, nblk, A_GROUP * hd, A_BLOCK), lambda b, h: (b, 0, h, 0)),
        out_shape=jax.ShapeDtypeStruct((B, nblk, A_WIDTH, A_BLOCK), BF16),
        scratch_shapes=[pltpu.VMEM((3, 3 * A_BLOCK, A_BLOCK), F32),
                        pltpu.VMEM((2, 3 * A_BLOCK + n_ctx_tok, cols), F32)],
        compiler_params=_params(("arbitrary", "arbitrary")),
        name="window_attn",
    )(aq, ak, avt, sink_rows)


def _head_norm(y):
    mu = jnp.mean(y, axis=-1, keepdims=True)
    yc = y - mu
    var = jnp.mean(yc * yc, axis=-1, keepdims=True)
    return yc * lax.rsqrt(var + NORM_EPS)


def _post_kernel(c_ref, x_ref, hb_ref, hm_ref, ya_ref, hr_ref, mod_ref, wg_ref, mnw_ref, rnw_ref,
                 wbm_ref, wba_ref, wbr_ref, wo_ref, o_ref, *, split):
    for i in range(x_ref.shape[0]):
        hb = hb_ref[i]
        off = 0
        hm = jnp.concatenate([_head_norm(hm_ref[i, h]) for h in range(M_HEADS)], axis=1) * mnw_ref[...]
        ym = (_sigmoid(_dot(hb, wg_ref[:, off:off + M_WIDTH])) * hm).astype(BF16)
        off += M_WIDTH
        hr = jnp.concatenate([_head_norm(hr_ref[i, h]) for h in range(R_HEADS)], axis=1) * rnw_ref[...]
        rg = _dot(hb, wg_ref[:, off:off + R_V_WIDTH])
        yr = (rg * _sigmoid(rg) * hr).astype(BF16)
        off += R_V_WIDTH
        z = _sigmoid(_dot(hb, wg_ref[:, off:off + D_MODEL])) * _dot(ym, wbm_ref[...])
        off += D_MODEL
        pa = jnp.concatenate([_dot_tn(ya_ref[i, c], wba_ref[...]) for c in range(TM // A_BLOCK)], axis=0)
        z = z + _sigmoid(_dot(hb, wg_ref[:, off:off + D_MODEL])) * pa
        off += D_MODEL
        z = z + _sigmoid(_dot(hb, wg_ref[:, off:off + D_MODEL])) * _dot(yr, wbr_ref[...])
        y = _dot(z.astype(BF16), wo_ref[...])
        one = pl.ds(i, 1)
        xres = _tile_rows(c_ref.at[one] if split else None, x_ref.at[one])
        o_ref[i] = xres + mod_ref[i, 0, 2:3, :] * y


def _post_call(ctx, xs, hb, hm, ya, hr, modsel, wg, mnw, rnw, wbm, wba, wbr, wo, t_off):
    B, T, D = hb.shape
    nt = T // TM - t_off
    split = ctx is not None
    assert not (split and t_off)

    def tile(t, b):
        return (b, t + t_off, 0)

    def head_tile(t, b):
        return (b, 0, t + t_off, 0)

    return pl.pallas_call(
        functools.partial(_post_kernel, split=split),
        grid=(nt, B // BB),
        in_specs=[pl.BlockSpec((BB, TM if split else HALO, D), lambda t, b: (b, 0, 0)),
                  pl.BlockSpec((BB, TM, D), (lambda t, b: (b, jnp.maximum(t - 1, 0), 0)) if split else tile),
                  pl.BlockSpec((BB, TM, D), tile),
                  pl.BlockSpec((BB, M_HEADS, TM, M_HEAD_DIM), head_tile),
                  pl.BlockSpec((BB, TM // A_BLOCK, A_WIDTH, A_BLOCK), lambda t, b: (b, t + t_off, 0, 0)),
                  pl.BlockSpec((BB, R_HEADS, TM, R_V_DIM), head_tile),
                  pl.BlockSpec((BB, 1, 6, D), lambda t, b: (b, jnp.minimum(t + t_off, 1), 0, 0)),
                  _const_spec(wg.shape), _const_spec(mnw.shape), _const_spec(rnw.shape),
                  _const_spec(wbm.shape), _const_spec(wba.shape), _const_spec(wbr.shape),
                  _const_spec(wo.shape)],
        out_specs=pl.BlockSpec((BB, TM, D), lambda t, b: (b, t, 0)),
        out_shape=jax.ShapeDtypeStruct((B, nt * TM, D), F32),
        compiler_params=_params(("parallel", "parallel")),
        name="post_merge",
    )(ctx if split else xs, xs, hb, hm, ya, hr, modsel, wg, mnw, rnw, wbm, wba, wbr, wo)


def _ffn_kernel(x_ref, xp_ref, xn_ref, mod_ref, nw_ref, wa_ref, wb_ref, cw_ref, cb_ref, wd_ref, fw_ref,
                o_ref, hext_ref, u_ref, *, has_ctx, final):
    t = pl.program_id(0)
    nt = pl.num_programs(0)
    first_lat = 1 if has_ctx else 0
    nw = nw_ref[...]
    ext = TM + 2 * HALO
    for i in range(x_ref.shape[0]):
        sh = mod_ref[i, 0, 3:4, :]
        sc = mod_ref[i, 0, 4:5, :]
        g2 = mod_ref[i, 0, 5:6, :]
        x = x_ref[i]
        hp = _norm_mod(xp_ref[i], nw, sc, sh)
        hn = _norm_mod(xn_ref[i], nw, sc, sh)
        hp = jnp.where(t <= first_lat, 0.0, hp)
        hn = jnp.where((t == first_lat - 1) | (t == nt - 1), 0.0, hn)
        hext_ref[i, 0:TM, :] = _norm_mod(x, nw, sc, sh).astype(BF16)
        hext_ref[i, TM:, :] = jnp.concatenate([hn, hp], axis=0).astype(BF16)
        for j in range(FFN_DIM // FFN_NC):
            cs = slice(j * FFN_NC, (j + 1) * FFN_NC)
            a = _dot(hext_ref[i], wa_ref[:, cs])
            b = _dot(hext_ref[i, 0:TM, :], wb_ref[:, cs])
            cw = cw_ref[:, cs]
            prev = pltpu.roll(a, 1, 0)[0:TM]
            nxt = pltpu.roll(a, ext - 1, 0)[0:TM]
            conv = prev * cw[0:1] + a[0:TM] * cw[1:2] + nxt * cw[2:3] + cb_ref[:, cs]
            u_ref[i, :, cs] = (conv * _sigmoid(conv) * b).astype(BF16)
        y = x + g2 * _dot(u_ref[i], wd_ref[...])
        if final:
            ms = jnp.mean(y * y, axis=-1, keepdims=True)
            y = y * lax.rsqrt(ms + NORM_EPS) * fw_ref[...]
        o_ref[i] = y


def _ffn_call(xs, modsel, nw, wa, wb, cw, cb, wd, fw, has_ctx, final):
    B, T, D = xs.shape
    nt = T // TM
    hb16 = TM // HALO
    first_lat = 1 if has_ctx else 0
    bb = FFN_BB if B % FFN_BB == 0 else BB
    tile = pl.BlockSpec((bb, TM, D), lambda t, b: (b, t, 0))
    return pl.pallas_call(
        functools.partial(_ffn_kernel, has_ctx=has_ctx, final=final),
        grid=(nt, B // bb),
        in_specs=[tile,
                  pl.BlockSpec((bb, HALO, D), lambda t, b: (b, jnp.maximum(t * hb16 - 1, 0), 0)),
                  pl.BlockSpec((bb, HALO, D), lambda t, b: (b, jnp.minimum((t + 1) * hb16, T // HALO - 1), 0)),
                  pl.BlockSpec((bb, 1, 6, D), lambda t, b: (b, jnp.minimum(t + 1 - first_lat, 1), 0, 0)),
                  _const_spec(nw.shape), _const_spec(wa.shape), _const_spec(wb.shape),
                  _const_spec(cw.shape), _const_spec(cb.shape), _const_spec(wd.shape),
                  _const_spec(fw.shape)],
        out_specs=tile,
        out_shape=jax.ShapeDtypeStruct((B, T, D), F32),
        scratch_shapes=[pltpu.VMEM((bb, TM + 2 * HALO, D), BF16), pltpu.VMEM((bb, TM, FFN_DIM), BF16)],
        compiler_params=_params(("parallel", "parallel")),
        name="conv_ffn",
    )(xs, xs, xs, modsel, nw, wa, wb, cw, cb, wd, fw)


def _split_cols(w):
    out = []
    acc = 0
    for s in IN_SPLITS:
        out.append(w[:, acc:acc + s])
        acc += s
    return out


def _rope_tables(L, S):
    T = L + S
    f32 = np.float32
    j = np.arange(LANES)
    jj = j % 32
    inv_a = np.power(f32(ROPE_BASE), -(jj % 16).astype(f32) / f32(16.0)).astype(f32)
    t = np.arange(S)
    pos = np.where(((j % A_HEAD_DIM) // 32 == 0)[None, :], (t // GRID_W)[:, None], (t % GRID_W)[:, None])
    ang = pos.astype(f32) * inv_a[None, :]
    sign = np.where(jj < 16, f32(-1.0), f32(1.0))[None, :]
    ca = np.concatenate([np.ones((L, LANES), f32), np.cos(ang)], axis=0)
    sa = np.concatenate([np.zeros((L, LANES), f32), np.sin(ang) * sign], axis=0)
    half = R_QK_DIM // 2
    inv_r = np.power(f32(ROPE_BASE), -(j % half).astype(f32) / f32(half)).astype(f32)
    angr = np.arange(T).astype(f32)[:, None] * inv_r[None, :]
    signr = np.where(j < half, f32(-1.0), f32(1.0))[None, :]
    return tuple(jnp.asarray(a, F32) for a in (ca, sa, np.cos(angr), np.sin(angr) * signr))


def kernel(x, c, ctx, c_ctx, mod_w, mod_b, norm1_w, norm2_w, w_in, m_gate_bias, m_conv_w, m_norm_w,
           a_sink, ret_logit, ret_norm_w, w_br_m, w_br_a, w_br_r, w_out, ffn_up, ffn_conv_w, ffn_conv_b,
           ffn_down, final_norm_w):
    B, S, D = x.shape
    L = ctx.shape[1]
    depth = mod_w.shape[0]
    assert D == D_MODEL and L == TM and S % TM == 0 and S >= 3 * A_BLOCK and B % BB == 0
    T = L + S
    n_ctx = L // CHUNK

    rows = -(-(B + 1) // 8) * 8
    cpad = jnp.zeros((rows, D), F32).at[:B].set(c).at[B].set(c_ctx)
    mods = _modulation(cpad, mod_w, mod_b)
    tabs = _rope_tables(L, S)
    stream = (ctx, x) if depth > 1 else (None, jnp.concatenate([ctx, x], axis=1))

    gperm = jnp.arange(4 * M_HEADS).reshape(4, M_HEADS).T.reshape(-1)

    for l in range(depth):
        last = l == depth - 1
        lat = mods[l, :B].reshape(B, 6, D)
        cm = jnp.broadcast_to(mods[l, B].reshape(1, 6, D), (B, 6, D))
        modsel = jnp.stack([cm, lat], axis=1)

        (w_mq, w_mk, w_mv, w_mo, w_mg, w_aq, w_ak, w_av,
         w_rq, w_rk, w_rv, w_rg, w_gm, w_ga, w_gr) = _split_cols(w_in[l])
        wqk = jnp.concatenate([w_mq, w_mk], axis=1).astype(BF16)
        wr = jnp.concatenate([w_mv, w_aq, w_ak, w_av, w_rq, w_rk, w_rv], axis=1).astype(BF16)
        wgt = w_mg[:, gperm].T.astype(BF16)
        gbt = jnp.broadcast_to(m_gate_bias[l][gperm].reshape(-1, 1), (4 * M_HEADS, LANES))

        (hb, mq, mk, mv, gr, aq, ak, av, rq, rk, rv) = _pre_call(
            *stream, modsel, norm1_w[l].reshape(1, D), wqk, m_conv_w[l], wr, wgt, gbt, tabs)

        hm = _mlstm_call(mq, mk, mv, gr, n_ctx)
        sink_rows = jnp.repeat(a_sink[l].reshape(A_KV_HEADS, A_GROUP), A_BLOCK, axis=1)[:, None, :]
        ya = _attn_call(aq, ak, av, sink_rows.astype(F32) * LOG2E, L, not last)
        lgb = jnp.broadcast_to(ret_logit[l].T[:, :, None, None], (R_HEADS, 2, 8, R_V_DIM)).astype(F32)
        hr = _ret_call(rq, rk, rv, lgb, L // RCHUNK)

        t_off = 1 if last else 0
        wgates = jnp.concatenate([w_mo, w_rg, w_gm, w_ga, w_gr], axis=1).astype(BF16)
        x1 = _post_call(*stream, hb, hm, ya, hr, modsel, wgates, m_norm_w[l].reshape(1, -1),
                        ret_norm_w[l].reshape(1, -1), w_br_m[l].astype(BF16), w_br_a[l].astype(BF16),
                        w_br_r[l].astype(BF16), w_out[l].astype(BF16), t_off)
        xc = _ffn_call(x1, modsel, norm2_w[l].reshape(1, D), ffn_up[l][:, :FFN_DIM].astype(BF16),
                       ffn_up[l][:, FFN_DIM:].astype(BF16), ffn_conv_w[l], ffn_conv_b[l].reshape(1, -1),
                       ffn_down[l].astype(BF16), final_norm_w.reshape(1, D), not last, last)
        stream = (None, xc)
    return xc
```
